```python
import math
import jax, jax.numpy as jnp
from jax import lax
import numpy as np

D_MODEL = 1024
BATCH = 8
SEQ = 8192
DEPTH = 4

N_MIXERS = 2
N_S5_LAYERS = (DEPTH + 1) // 2
N_ATTN_LAYERS = DEPTH // 2
N_META = 16
GRID_W = 64
HEAD_DIM = 64
N_Q_HEADS = D_MODEL // HEAD_DIM
N_KV_HEADS = N_Q_HEADS // 4
Q_PER_KV = N_Q_HEADS // N_KV_HEADS
QKV_WIDTH = (N_Q_HEADS + 2 * N_KV_HEADS) * HEAD_DIM
QUERY_BLOCK = 128
ROPE_THETA = 10000.0
ROPE_AXIS_DIM = HEAD_DIM // 2
QK_EPS = 1e-6
S5_GROUP_CH = 16
S5_GROUPS = D_MODEL // S5_GROUP_CH
S5_STATE = 64
S5_DT_MIN = 1e-3
S5_DT_MAX = 1e-1
D_FF = -(-8 * D_MODEL // (3 * 256)) * 256
LN_EPS = 1e-5
DEEPNORM_ALPHA = (2.0 * DEPTH) ** 0.25
DEEPNORM_BETA = (8.0 * DEPTH) ** -0.25

kernel_name = "hybrid_s5_gqa_deepnorm_encoder"


def layer_norm(x, gain, bias):
    xf = x.astype(jnp.float32)
    mean = jnp.mean(xf, axis=-1, keepdims=True)
    var = jnp.mean(jnp.square(xf - mean), axis=-1, keepdims=True)
    y = (xf - mean) * lax.rsqrt(var + LN_EPS) * gain.astype(jnp.float32) + bias.astype(jnp.float32)
    return y.astype(x.dtype)


def swiglu_ffn(h, w_gate, w_up, w_down):
    return (jax.nn.silu(h @ w_gate) * (h @ w_up)) @ w_down


def _s5_combine(left, right):
    ar1, ai1, br1, bi1 = left
    ar2, ai2, br2, bi2 = right
    ar = ar1 * ar2 - ai1 * ai2
    ai = ar1 * ai2 + ai1 * ar2
    br = ar2 * br1 - ai2 * bi1 + br2
    bi = ar2 * bi1 + ai2 * br1 + bi2
    return (ar, ai, br, bi)


def s5_direction(u, lam_re, lam_im, log_dt, b_re, b_im, c_re, c_im, reverse):
    L = u.shape[1]
    lr = lam_re.astype(jnp.float32)
    li = lam_im.astype(jnp.float32)
    dt = jnp.exp(log_dt.astype(jnp.float32))[:, None]
    mag = jnp.exp(lr * dt)
    abr = mag * jnp.cos(li * dt)
    abi = mag * jnp.sin(li * dt)
    nr, ni = abr - 1.0, abi
    den = lr * lr + li * li
    cr = (nr * lr + ni * li) / den
    ci = (ni * lr - nr * li) / den
    br = b_re.astype(jnp.float32)
    bi = b_im.astype(jnp.float32)
    bbr = cr[..., None] * br - ci[..., None] * bi
    bbi = cr[..., None] * bi + ci[..., None] * br
    bu_r = jnp.einsum('blgc,gpc->blgp', u, bbr)
    bu_i = jnp.einsum('blgc,gpc->blgp', u, bbi)
    a_r = jnp.broadcast_to(abr[None, None], (1, L) + abr.shape)
    a_i = jnp.broadcast_to(abi[None, None], (1, L) + abi.shape)
    _, _, sr, si = lax.associative_scan(_s5_combine, (a_r, a_i, bu_r, bu_i),
                                        reverse=reverse, axis=1)
    return (jnp.einsum('blgp,gcp->blgc', sr, c_re.astype(jnp.float32))
            - jnp.einsum('blgp,gcp->blgc', si, c_im.astype(jnp.float32)))


def s5_mixer(h, lam_re, lam_im, log_dt, b_re, b_im, c_re, c_im, d_skip, w_glu, w_out):
    bsz, L, _ = h.shape
    hf = h.astype(jnp.float32)
    u = hf.reshape(bsz, L, S5_GROUPS, S5_GROUP_CH)
    y = (s5_direction(u, lam_re[0], lam_im[0], log_dt[0], b_re[0], b_im[0], c_re[0], c_im[0], False)
         + s5_direction(u, lam_re[1], lam_im[1], log_dt[1], b_re[1], b_im[1], c_re[1], c_im[1], True))
    y = y.reshape(bsz, L, D_MODEL) + d_skip.astype(jnp.float32) * hf
    g = jax.nn.gelu(y, approximate=False).astype(h.dtype)
    z = g * jax.nn.sigmoid(g @ w_glu)
    return z @ w_out


def axial_rope_tables(n_real):
    rows = n_real // GRID_W
    row_ids = jnp.repeat(jnp.arange(rows, dtype=jnp.int32), GRID_W)
    col_ids = jnp.tile(jnp.arange(GRID_W, dtype=jnp.int32), rows)
    pad = jnp.zeros((N_META,), jnp.int32)
    row_ids = jnp.concatenate([pad, row_ids]).astype(jnp.float32)
    col_ids = jnp.concatenate([pad, col_ids]).astype(jnp.float32)
    inv_freq = ROPE_THETA ** (-jnp.arange(0, ROPE_AXIS_DIM, 2, dtype=jnp.float32) / ROPE_AXIS_DIM)
    ang_r = row_ids[:, None] * inv_freq[None, :]
    ang_c = col_ids[:, None] * inv_freq[None, :]
    return jnp.cos(ang_r), jnp.sin(ang_r), jnp.cos(ang_c), jnp.sin(ang_c)


def _rotate_half_block(xh, cos, sin):
    half = ROPE_AXIS_DIM // 2
    x1, x2 = xh[..., :half], xh[..., half:]
    c = cos[None, :, None, :]
    s = sin[None, :, None, :]
    return jnp.concatenate([x1 * c - x2 * s, x2 * c + x1 * s], axis=-1)


def rms_axial_rope(t, gain, cos_r, sin_r, cos_c, sin_c):
    tf = t.astype(jnp.float32)
    tf = tf * lax.rsqrt(jnp.mean(jnp.square(tf), axis=-1, keepdims=True) + QK_EPS) * gain.astype(jnp.float32)
    return jnp.concatenate([_rotate_half_block(tf[..., :ROPE_AXIS_DIM], cos_r, sin_r),
                            _rotate_half_block(tf[..., ROPE_AXIS_DIM:], cos_c, sin_c)], axis=-1)


def _attend(q, k, v):
    s = jnp.einsum('bqkgd,bskd->bkgqs', q, k).astype(jnp.float32) * (HEAD_DIM ** -0.5)
    p = jax.nn.softmax(s, axis=-1).astype(v.dtype)
    return jnp.einsum('bkgqs,bskd->bqkgd', p, v)


def gqa_mixer(h, w_qkv, q_gain, k_gain, w_out, cos_r, sin_r, cos_c, sin_c):
    bsz, L, _ = h.shape
    qkv = h @ w_qkv
    nq, nk = N_Q_HEADS * HEAD_DIM, N_KV_HEADS * HEAD_DIM
    q = qkv[..., :nq].reshape(bsz, L, N_Q_HEADS, HEAD_DIM)
    k = qkv[..., nq:nq + nk].reshape(bsz, L, N_KV_HEADS, HEAD_DIM)
    v = qkv[..., nq + nk:].reshape(bsz, L, N_KV_HEADS, HEAD_DIM)
    q = rms_axial_rope(q, q_gain, cos_r, sin_r, cos_c, sin_c).astype(v.dtype)
    k = rms_axial_rope(k, k_gain, cos_r, sin_r, cos_c, sin_c).astype(v.dtype)
    q = q.reshape(bsz, L, N_KV_HEADS, Q_PER_KV, HEAD_DIM)
    out_meta = _attend(q[:, :N_META], k, v)
    n_real = L - N_META
    n_blk = n_real // QUERY_BLOCK
    qb = q[:, N_META:].reshape(bsz, n_blk, QUERY_BLOCK, N_KV_HEADS, Q_PER_KV, HEAD_DIM).swapaxes(0, 1)
    out_real = lax.map(lambda blk: _attend(blk, k, v), qb)
    out_real = out_real.swapaxes(0, 1).reshape(bsz, n_real, N_KV_HEADS, Q_PER_KV, HEAD_DIM)
    out = jnp.concatenate([out_meta, out_real], axis=1).reshape(bsz, L, D_MODEL)
    return out @ w_out


def _fwd_setup_inputs(seed: int = 0) -> dict:
    key = jax.random.key(seed)
    ks = jax.random.split(key, 24)
    f32 = jnp.float32
    D, G, P, C = D_MODEL, S5_GROUPS, S5_STATE, S5_GROUP_CH
    nS, nA = N_S5_LAYERS, N_ATTN_LAYERS
    x = jax.random.normal(ks[0], (BATCH, SEQ, D), f32)
    meta_tokens = jax.random.normal(ks[1], (N_META, D), f32)
    s5_lambda_re = -0.5 * jnp.exp(0.02 * jax.random.normal(ks[2], (nS, 2, G, P), f32))
    s5_lambda_im = (math.pi * jnp.arange(P, dtype=f32))[None, None, None, :] \
        + 0.01 * jax.random.normal(ks[3], (nS, 2, G, P), f32)
    s5_log_dt = jax.random.uniform(ks[4], (nS, 2, G), f32,
                                   minval=math.log(S5_DT_MIN), maxval=math.log(S5_DT_MAX))
    b_scale = (2.0 * C) ** -0.5
    s5_b_re = jax.random.normal(ks[5], (nS, 2, G, P, C), f32) * b_scale
    s5_b_im = jax.random.normal(ks[6], (nS, 2, G, P, C), f32) * b_scale
    c_scale = (2.0 * P) ** -0.5
    s5_c_re = jax.random.normal(ks[7], (nS, 2, G, C, P), f32) * c_scale
    s5_c_im = jax.random.normal(ks[8], (nS, 2, G, C, P), f32) * c_scale
    s5_d = jax.random.normal(ks[9], (nS, D), f32)
    s5_w_glu = jax.random.normal(ks[10], (nS, D, D), f32) * D ** -0.5
    s5_w_out = jax.random.normal(ks[11], (nS, D, D), f32) * (D ** -0.5 * DEEPNORM_BETA)
    w_qk = jax.random.normal(ks[12], (nA, D, (N_Q_HEADS + N_KV_HEADS) * HEAD_DIM), f32) * D ** -0.5
    w_v = jax.random.normal(ks[13], (nA, D, N_KV_HEADS * HEAD_DIM), f32) * (D ** -0.5 * DEEPNORM_BETA)
    attn_w_qkv = jnp.concatenate([w_qk, w_v], axis=-1)
    attn_q_gain = 1.0 + 0.02 * jax.random.normal(ks[14], (nA, HEAD_DIM), f32)
    attn_k_gain = 1.0 + 0.02 * jax.random.normal(ks[15], (nA, HEAD_DIM), f32)
    attn_w_out = jax.random.normal(ks[16], (nA, D, D), f32) * (D ** -0.5 * DEEPNORM_BETA)
    ffn_w_gate = jax.random.normal(ks[17], (DEPTH, D, D_FF), f32) * D ** -0.5
    ffn_w_up = jax.random.normal(ks[18], (DEPTH, D, D_FF), f32) * D ** -0.5
    ffn_w_down = jax.random.normal(ks[19], (DEPTH, D_FF, D), f32) * (D_FF ** -0.5 * DEEPNORM_BETA)
    ln_gain = 1.0 + 0.02 * jax.random.normal(ks[20], (DEPTH, 2, D), f32)
    ln_bias = 0.02 * jax.random.normal(ks[21], (DEPTH, 2, D), f32)
    return {"x": x, "meta_tokens": meta_tokens,
            "s5_lambda_re": s5_lambda_re, "s5_lambda_im": s5_lambda_im, "s5_log_dt": s5_log_dt,
            "s5_b_re": s5_b_re, "s5_b_im": s5_b_im, "s5_c_re": s5_c_re, "s5_c_im": s5_c_im,
            "s5_d": s5_d, "s5_w_glu": s5_w_glu, "s5_w_out": s5_w_out,
            "attn_w_qkv": attn_w_qkv, "attn_q_gain": attn_q_gain, "attn_k_gain": attn_k_gain,
            "attn_w_out": attn_w_out,
            "ffn_w_gate": ffn_w_gate, "ffn_w_up": ffn_w_up, "ffn_w_down": ffn_w_down,
            "ln_gain": ln_gain, "ln_bias": ln_bias}


def _fwd_reference(x, meta_tokens, s5_lambda_re, s5_lambda_im, s5_log_dt, s5_b_re, s5_b_im,
              s5_c_re, s5_c_im, s5_d, s5_w_glu, s5_w_out, attn_w_qkv, attn_q_gain,
              attn_k_gain, attn_w_out, ffn_w_gate, ffn_w_up, ffn_w_down, ln_gain, ln_bias):
    bsz, n_real, d = x.shape
    meta = jnp.broadcast_to(meta_tokens.astype(x.dtype)[None], (bsz, N_META, d))
    h = jnp.concatenate([meta, x], axis=1)
    cos_r, sin_r, cos_c, sin_c = axial_rope_tables(n_real)
    for i in range(DEPTH):
        j = i // N_MIXERS
        if i % N_MIXERS == 0:
            mix = s5_mixer(h, s5_lambda_re[j], s5_lambda_im[j], s5_log_dt[j], s5_b_re[j], s5_b_im[j],
                           s5_c_re[j], s5_c_im[j], s5_d[j], s5_w_glu[j], s5_w_out[j])
        else:
            mix = gqa_mixer(h, attn_w_qkv[j], attn_q_gain[j], attn_k_gain[j], attn_w_out[j],
                            cos_r, sin_r, cos_c, sin_c)
        h = layer_norm(DEEPNORM_ALPHA * h + mix, ln_gain[i, 0], ln_bias[i, 0])
        h = layer_norm(DEEPNORM_ALPHA * h + swiglu_ffn(h, ffn_w_gate[i], ffn_w_up[i], ffn_w_down[i]),
                       ln_gain[i, 1], ln_bias[i, 1])
    return h[:, N_META:]


import jax as _jax
import jax.numpy as _jnp

TWIN_FORMAT = 'train_step'
FWD_PARAMS = ['x', 'meta_tokens', 's5_lambda_re', 's5_lambda_im', 's5_log_dt', 's5_b_re', 's5_b_im', 's5_c_re', 's5_c_im', 's5_d', 's5_w_glu', 's5_w_out', 'attn_w_qkv', 'attn_q_gain', 'attn_k_gain', 'attn_w_out', 'ffn_w_gate', 'ffn_w_up', 'ffn_w_down', 'ln_gain', 'ln_bias']
TWIN_WEIGHTS = ['meta_tokens', 's5_lambda_re', 's5_lambda_im', 's5_log_dt', 's5_b_re', 's5_b_im', 's5_c_re', 's5_c_im', 's5_d', 's5_w_glu', 's5_w_out', 'attn_w_qkv', 'attn_q_gain', 'attn_k_gain', 'attn_w_out', 'ffn_w_gate', 'ffn_w_up', 'ffn_w_down', 'ln_gain', 'ln_bias']
TWIN_DIFF_INPUT = 'x'
TWIN_INPUTS = ['x', 'meta_tokens', 's5_lambda_re', 's5_lambda_im', 's5_log_dt', 's5_b_re', 's5_b_im', 's5_c_re', 's5_c_im', 's5_d', 's5_w_glu', 's5_w_out', 'attn_w_qkv', 'attn_q_gain', 'attn_k_gain', 'attn_w_out', 'ffn_w_gate', 'ffn_w_up', 'ffn_w_down', 'ln_gain', 'ln_bias', 'loss_target', 'm_meta_tokens', 'm_s5_lambda_re', 'm_s5_lambda_im', 'm_s5_log_dt', 'm_s5_b_re', 'm_s5_b_im', 'm_s5_c_re', 'm_s5_c_im', 'm_s5_d', 'm_s5_w_glu', 'm_s5_w_out', 'm_attn_w_qkv', 'm_attn_q_gain', 'm_attn_k_gain', 'm_attn_w_out', 'm_ffn_w_gate', 'm_ffn_w_up', 'm_ffn_w_down', 'm_ln_gain', 'm_ln_bias', 'v_meta_tokens', 'v_s5_lambda_re', 'v_s5_lambda_im', 'v_s5_log_dt', 'v_s5_b_re', 'v_s5_b_im', 'v_s5_c_re', 'v_s5_c_im', 'v_s5_d', 'v_s5_w_glu', 'v_s5_w_out', 'v_attn_w_qkv', 'v_attn_q_gain', 'v_attn_k_gain', 'v_attn_w_out', 'v_ffn_w_gate', 'v_ffn_w_up', 'v_ffn_w_down', 'v_ln_gain', 'v_ln_bias']
TWIN_OUTPUTS = ['loss', 'grad_x', 'grad_meta_tokens', 'grad_s5_lambda_re', 'grad_s5_lambda_im', 'grad_s5_log_dt', 'grad_s5_b_re', 'grad_s5_b_im', 'grad_s5_c_re', 'grad_s5_c_im', 'grad_s5_d', 'grad_s5_w_glu', 'grad_s5_w_out', 'grad_attn_w_qkv', 'grad_attn_q_gain', 'grad_attn_k_gain', 'grad_attn_w_out', 'grad_ffn_w_gate', 'grad_ffn_w_up', 'grad_ffn_w_down', 'grad_ln_gain', 'grad_ln_bias', 'delta_meta_tokens', 'delta_s5_lambda_re', 'delta_s5_lambda_im', 'delta_s5_log_dt', 'delta_s5_b_re', 'delta_s5_b_im', 'delta_s5_c_re', 'delta_s5_c_im', 'delta_s5_d', 'delta_s5_w_glu', 'delta_s5_w_out', 'delta_attn_w_qkv', 'delta_attn_q_gain', 'delta_attn_k_gain', 'delta_attn_w_out', 'delta_ffn_w_gate', 'delta_ffn_w_up', 'delta_ffn_w_down', 'delta_ln_gain', 'delta_ln_bias', 'new_m_meta_tokens', 'new_m_s5_lambda_re', 'new_m_s5_lambda_im', 'new_m_s5_log_dt', 'new_m_s5_b_re', 'new_m_s5_b_im', 'new_m_s5_c_re', 'new_m_s5_c_im', 'new_m_s5_d', 'new_m_s5_w_glu', 'new_m_s5_w_out', 'new_m_attn_w_qkv', 'new_m_attn_q_gain', 'new_m_attn_k_gain', 'new_m_attn_w_out', 'new_m_ffn_w_gate', 'new_m_ffn_w_up', 'new_m_ffn_w_down', 'new_m_ln_gain', 'new_m_ln_bias', 'new_v_meta_tokens', 'new_v_s5_lambda_re', 'new_v_s5_lambda_im', 'new_v_s5_log_dt', 'new_v_s5_b_re', 'new_v_s5_b_im', 'new_v_s5_c_re', 'new_v_s5_c_im', 'new_v_s5_d', 'new_v_s5_w_glu', 'new_v_s5_w_out', 'new_v_attn_w_qkv', 'new_v_attn_q_gain', 'new_v_attn_k_gain', 'new_v_attn_w_out', 'new_v_ffn_w_gate', 'new_v_ffn_w_up', 'new_v_ffn_w_down', 'new_v_ln_gain', 'new_v_ln_bias']
TWIN_LEAF_KINDS = {'loss': 'loss', 'grad_x': 'grad_x', 'grad_meta_tokens': 'grad_w', 'grad_s5_lambda_re': 'grad_w', 'grad_s5_lambda_im': 'grad_w', 'grad_s5_log_dt': 'grad_w', 'grad_s5_b_re': 'grad_w', 'grad_s5_b_im': 'grad_w', 'grad_s5_c_re': 'grad_w', 'grad_s5_c_im': 'grad_w', 'grad_s5_d': 'grad_w', 'grad_s5_w_glu': 'grad_w', 'grad_s5_w_out': 'grad_w', 'grad_attn_w_qkv': 'grad_w', 'grad_attn_q_gain': 'grad_w', 'grad_attn_k_gain': 'grad_w', 'grad_attn_w_out': 'grad_w', 'grad_ffn_w_gate': 'grad_w', 'grad_ffn_w_up': 'grad_w', 'grad_ffn_w_down': 'grad_w', 'grad_ln_gain': 'grad_w', 'grad_ln_bias': 'grad_w', 'delta_meta_tokens': 'delta_w', 'delta_s5_lambda_re': 'delta_w', 'delta_s5_lambda_im': 'delta_w', 'delta_s5_log_dt': 'delta_w', 'delta_s5_b_re': 'delta_w', 'delta_s5_b_im': 'delta_w', 'delta_s5_c_re': 'delta_w', 'delta_s5_c_im': 'delta_w', 'delta_s5_d': 'delta_w', 'delta_s5_w_glu': 'delta_w', 'delta_s5_w_out': 'delta_w', 'delta_attn_w_qkv': 'delta_w', 'delta_attn_q_gain': 'delta_w', 'delta_attn_k_gain': 'delta_w', 'delta_attn_w_out': 'delta_w', 'delta_ffn_w_gate': 'delta_w', 'delta_ffn_w_up': 'delta_w', 'delta_ffn_w_down': 'delta_w', 'delta_ln_gain': 'delta_w', 'delta_ln_bias': 'delta_w', 'new_m_meta_tokens': 'new_m', 'new_m_s5_lambda_re': 'new_m', 'new_m_s5_lambda_im': 'new_m', 'new_m_s5_log_dt': 'new_m', 'new_m_s5_b_re': 'new_m', 'new_m_s5_b_im': 'new_m', 'new_m_s5_c_re': 'new_m', 'new_m_s5_c_im': 'new_m', 'new_m_s5_d': 'new_m', 'new_m_s5_w_glu': 'new_m', 'new_m_s5_w_out': 'new_m', 'new_m_attn_w_qkv': 'new_m', 'new_m_attn_q_gain': 'new_m', 'new_m_attn_k_gain': 'new_m', 'new_m_attn_w_out': 'new_m', 'new_m_ffn_w_gate': 'new_m', 'new_m_ffn_w_up': 'new_m', 'new_m_ffn_w_down': 'new_m', 'new_m_ln_gain': 'new_m', 'new_m_ln_bias': 'new_m', 'new_v_meta_tokens': 'new_v', 'new_v_s5_lambda_re': 'new_v', 'new_v_s5_lambda_im': 'new_v', 'new_v_s5_log_dt': 'new_v', 'new_v_s5_b_re': 'new_v', 'new_v_s5_b_im': 'new_v', 'new_v_s5_c_re': 'new_v', 'new_v_s5_c_im': 'new_v', 'new_v_s5_d': 'new_v', 'new_v_s5_w_glu': 'new_v', 'new_v_s5_w_out': 'new_v', 'new_v_attn_w_qkv': 'new_v', 'new_v_attn_q_gain': 'new_v', 'new_v_attn_k_gain': 'new_v', 'new_v_attn_w_out': 'new_v', 'new_v_ffn_w_gate': 'new_v', 'new_v_ffn_w_up': 'new_v', 'new_v_ffn_w_down': 'new_v', 'new_v_ln_gain': 'new_v', 'new_v_ln_bias': 'new_v'}


def _forward(args):
    return _fwd_reference(*[args[k] for k in FWD_PARAMS])


def _output_shape():
    def fwd():
        inp = _fwd_setup_inputs(0)
        return _fwd_reference(*[inp[k] for k in FWD_PARAMS])
    out = _jax.eval_shape(fwd)
    return out.shape, out.dtype

N_MICROBATCH = 1
ADAM_LR = 0.001
ADAM_B1 = 0.9
ADAM_B2 = 0.999
ADAM_EPS = 1e-08
ADAM_WD = 0.01
ADAM_STEP = 10
PER_EXAMPLE_BATCH_AXIS = {'x': 0, 'loss_target': 0}
SHARED_INPUTS = []
_WEIGHT_DTYPES = {'meta_tokens': _jnp.float32, 's5_lambda_re': _jnp.float32, 's5_lambda_im': _jnp.float32, 's5_log_dt': _jnp.float32, 's5_b_re': _jnp.float32, 's5_b_im': _jnp.float32, 's5_c_re': _jnp.float32, 's5_c_im': _jnp.float32, 's5_d': _jnp.float32, 's5_w_glu': _jnp.float32, 's5_w_out': _jnp.float32, 'attn_w_qkv': _jnp.float32, 'attn_q_gain': _jnp.float32, 'attn_k_gain': _jnp.float32, 'attn_w_out': _jnp.float32, 'ffn_w_gate': _jnp.float32, 'ffn_w_up': _jnp.float32, 'ffn_w_down': _jnp.float32, 'ln_gain': _jnp.float32, 'ln_bias': _jnp.float32}
MOMENT_SCALE = {'meta_tokens': 3.560464e-04, 's5_lambda_re': 1.385607e-03, 's5_lambda_im': 1.294615e-03, 's5_log_dt': 1.065702e+00, 's5_b_re': 8.003871e-04, 's5_b_im': 8.123092e-04, 's5_c_re': 1.616227e-03, 's5_c_im': 1.631288e-03, 's5_d': 4.275601e-02, 's5_w_glu': 8.338959e-03, 's5_w_out': 9.347702e-02, 'attn_w_qkv': 7.849054e-03, 'attn_q_gain': 1.071654e-02, 'attn_k_gain': 1.051232e-02, 'attn_w_out': 1.024453e-02, 'ffn_w_gate': 2.345569e-02, 'ffn_w_up': 2.280547e-02, 'ffn_w_down': 8.986518e-02, 'ln_gain': 2.285091e+01, 'ln_bias': 1.900878e+00}


def _to_microbatches(a, axis):
    t = _jnp.moveaxis(a, axis, 0)
    t = t.reshape((N_MICROBATCH, t.shape[0] // N_MICROBATCH) + t.shape[1:])
    return _jnp.moveaxis(t, 1, axis + 1)


def setup_inputs(seed: int = 0) -> dict:
    inp = _fwd_setup_inputs(seed)
    key = _jax.random.fold_in(_jax.random.key(seed), 7919)
    shape, _ = _output_shape()
    out = dict(inp)
    out["loss_target"] = _jax.random.normal(_jax.random.fold_in(key, 0), shape, _jnp.float32)
    for i, name in enumerate(TWIN_WEIGHTS):
        w = inp[name].astype(_jnp.float32)
        if MOMENT_SCALE is None:
            s = _jnp.sqrt(_jnp.mean(_jnp.square(w)) + 1e-30)
        else:
            s = MOMENT_SCALE[name]
        km, kv = _jax.random.split(_jax.random.fold_in(key, i + 1))
        out[name] = w
        out["m_" + name] = s * _jax.random.normal(km, w.shape, _jnp.float32)
        out["v_" + name] = (s * s) * _jax.random.uniform(kv, w.shape, _jnp.float32, 0.5, 1.5)
    if N_MICROBATCH > 1:
        for name, axis in PER_EXAMPLE_BATCH_AXIS.items():
            out[name] = _to_microbatches(out[name], axis)
    return {'x': out['x'], 'meta_tokens': out['meta_tokens'], 's5_lambda_re': out['s5_lambda_re'], 's5_lambda_im': out['s5_lambda_im'], 's5_log_dt': out['s5_log_dt'], 's5_b_re': out['s5_b_re'], 's5_b_im': out['s5_b_im'], 's5_c_re': out['s5_c_re'], 's5_c_im': out['s5_c_im'], 's5_d': out['s5_d'], 's5_w_glu': out['s5_w_glu'], 's5_w_out': out['s5_w_out'], 'attn_w_qkv': out['attn_w_qkv'], 'attn_q_gain': out['attn_q_gain'], 'attn_k_gain': out['attn_k_gain'], 'attn_w_out': out['attn_w_out'], 'ffn_w_gate': out['ffn_w_gate'], 'ffn_w_up': out['ffn_w_up'], 'ffn_w_down': out['ffn_w_down'], 'ln_gain': out['ln_gain'], 'ln_bias': out['ln_bias'], 'loss_target': out['loss_target'], 'm_meta_tokens': out['m_meta_tokens'], 'm_s5_lambda_re': out['m_s5_lambda_re'], 'm_s5_lambda_im': out['m_s5_lambda_im'], 'm_s5_log_dt': out['m_s5_log_dt'], 'm_s5_b_re': out['m_s5_b_re'], 'm_s5_b_im': out['m_s5_b_im'], 'm_s5_c_re': out['m_s5_c_re'], 'm_s5_c_im': out['m_s5_c_im'], 'm_s5_d': out['m_s5_d'], 'm_s5_w_glu': out['m_s5_w_glu'], 'm_s5_w_out': out['m_s5_w_out'], 'm_attn_w_qkv': out['m_attn_w_qkv'], 'm_attn_q_gain': out['m_attn_q_gain'], 'm_attn_k_gain': out['m_attn_k_gain'], 'm_attn_w_out': out['m_attn_w_out'], 'm_ffn_w_gate': out['m_ffn_w_gate'], 'm_ffn_w_up': out['m_ffn_w_up'], 'm_ffn_w_down': out['m_ffn_w_down'], 'm_ln_gain': out['m_ln_gain'], 'm_ln_bias': out['m_ln_bias'], 'v_meta_tokens': out['v_meta_tokens'], 'v_s5_lambda_re': out['v_s5_lambda_re'], 'v_s5_lambda_im': out['v_s5_lambda_im'], 'v_s5_log_dt': out['v_s5_log_dt'], 'v_s5_b_re': out['v_s5_b_re'], 'v_s5_b_im': out['v_s5_b_im'], 'v_s5_c_re': out['v_s5_c_re'], 'v_s5_c_im': out['v_s5_c_im'], 'v_s5_d': out['v_s5_d'], 'v_s5_w_glu': out['v_s5_w_glu'], 'v_s5_w_out': out['v_s5_w_out'], 'v_attn_w_qkv': out['v_attn_w_qkv'], 'v_attn_q_gain': out['v_attn_q_gain'], 'v_attn_k_gain': out['v_attn_k_gain'], 'v_attn_w_out': out['v_attn_w_out'], 'v_ffn_w_gate': out['v_ffn_w_gate'], 'v_ffn_w_up': out['v_ffn_w_up'], 'v_ffn_w_down': out['v_ffn_w_down'], 'v_ln_gain': out['v_ln_gain'], 'v_ln_bias': out['v_ln_bias']}


def _loss(weights, diff, rest, loss_target):
    with _jax.named_scope("forward"):
        args = {**rest, TWIN_DIFF_INPUT: diff, **{k: w.astype(_WEIGHT_DTYPES[k]) for k, w in weights.items()}}
        y = _forward(args)
    with _jax.named_scope("loss_head"):
        err = _jnp.square(y.astype(_jnp.float32) - loss_target)
        return 0.5 * _jnp.sum(_jnp.mean(err, axis=-1)) if err.ndim else 0.5 * err


def _adamw(w, g, m, v):
    m = ADAM_B1 * m + (1.0 - ADAM_B1) * g
    v = ADAM_B2 * v + (1.0 - ADAM_B2) * _jnp.square(g)
    m_hat = m / (1.0 - ADAM_B1 ** ADAM_STEP)
    v_hat = v / (1.0 - ADAM_B2 ** ADAM_STEP)
    delta = -ADAM_LR * (m_hat / (_jnp.sqrt(v_hat) + ADAM_EPS) + ADAM_WD * w)
    return delta, m, v


def reference(x, meta_tokens, s5_lambda_re, s5_lambda_im, s5_log_dt, s5_b_re, s5_b_im, s5_c_re, s5_c_im, s5_d, s5_w_glu, s5_w_out, attn_w_qkv, attn_q_gain, attn_k_gain, attn_w_out, ffn_w_gate, ffn_w_up, ffn_w_down, ln_gain, ln_bias, loss_target, m_meta_tokens, m_s5_lambda_re, m_s5_lambda_im, m_s5_log_dt, m_s5_b_re, m_s5_b_im, m_s5_c_re, m_s5_c_im, m_s5_d, m_s5_w_glu, m_s5_w_out, m_attn_w_qkv, m_attn_q_gain, m_attn_k_gain, m_attn_w_out, m_ffn_w_gate, m_ffn_w_up, m_ffn_w_down, m_ln_gain, m_ln_bias, v_meta_tokens, v_s5_lambda_re, v_s5_lambda_im, v_s5_log_dt, v_s5_b_re, v_s5_b_im, v_s5_c_re, v_s5_c_im, v_s5_d, v_s5_w_glu, v_s5_w_out, v_attn_w_qkv, v_attn_q_gain, v_attn_k_gain, v_attn_w_out, v_ffn_w_gate, v_ffn_w_up, v_ffn_w_down, v_ln_gain, v_ln_bias):
    given = dict(x=x, meta_tokens=meta_tokens, s5_lambda_re=s5_lambda_re, s5_lambda_im=s5_lambda_im, s5_log_dt=s5_log_dt, s5_b_re=s5_b_re, s5_b_im=s5_b_im, s5_c_re=s5_c_re, s5_c_im=s5_c_im, s5_d=s5_d, s5_w_glu=s5_w_glu, s5_w_out=s5_w_out, attn_w_qkv=attn_w_qkv, attn_q_gain=attn_q_gain, attn_k_gain=attn_k_gain, attn_w_out=attn_w_out, ffn_w_gate=ffn_w_gate, ffn_w_up=ffn_w_up, ffn_w_down=ffn_w_down, ln_gain=ln_gain, ln_bias=ln_bias, loss_target=loss_target, m_meta_tokens=m_meta_tokens, m_s5_lambda_re=m_s5_lambda_re, m_s5_lambda_im=m_s5_lambda_im, m_s5_log_dt=m_s5_log_dt, m_s5_b_re=m_s5_b_re, m_s5_b_im=m_s5_b_im, m_s5_c_re=m_s5_c_re, m_s5_c_im=m_s5_c_im, m_s5_d=m_s5_d, m_s5_w_glu=m_s5_w_glu, m_s5_w_out=m_s5_w_out, m_attn_w_qkv=m_attn_w_qkv, m_attn_q_gain=m_attn_q_gain, m_attn_k_gain=m_attn_k_gain, m_attn_w_out=m_attn_w_out, m_ffn_w_gate=m_ffn_w_gate, m_ffn_w_up=m_ffn_w_up, m_ffn_w_down=m_ffn_w_down, m_ln_gain=m_ln_gain, m_ln_bias=m_ln_bias, v_meta_tokens=v_meta_tokens, v_s5_lambda_re=v_s5_lambda_re, v_s5_lambda_im=v_s5_lambda_im, v_s5_log_dt=v_s5_log_dt, v_s5_b_re=v_s5_b_re, v_s5_b_im=v_s5_b_im, v_s5_c_re=v_s5_c_re, v_s5_c_im=v_s5_c_im, v_s5_d=v_s5_d, v_s5_w_glu=v_s5_w_glu, v_s5_w_out=v_s5_w_out, v_attn_w_qkv=v_attn_w_qkv, v_attn_q_gain=v_attn_q_gain, v_attn_k_gain=v_attn_k_gain, v_attn_w_out=v_attn_w_out, v_ffn_w_gate=v_ffn_w_gate, v_ffn_w_up=v_ffn_w_up, v_ffn_w_down=v_ffn_w_down, v_ln_gain=v_ln_gain, v_ln_bias=v_ln_bias)
    weights = {n: given[n] for n in TWIN_WEIGHTS}
    shared = {n: given[n] for n in SHARED_INPUTS}
    per_example = {n: given[n] for n in ['x']}
    grad_fn = _jax.value_and_grad(_loss, argnums=(0, 1))

    def one_microbatch(ex, loss_target):
        ex = dict(ex)
        diff = ex.pop(TWIN_DIFF_INPUT)
        return grad_fn(weights, diff, {**shared, **ex}, loss_target)

    if N_MICROBATCH == 1:
        loss, (grad_w, grad_x) = one_microbatch(per_example, given["loss_target"])
    else:
        def body(carry, xs):
            loss_sum, grad_sum = carry
            l_k, (gw_k, gx_k) = one_microbatch(xs[0], xs[1])
            with _jax.named_scope("update"):
                return (loss_sum + l_k, _jax.tree.map(_jnp.add, grad_sum, gw_k)), gx_k

        init = (_jnp.zeros((), _jnp.float32), _jax.tree.map(_jnp.zeros_like, weights))
        (loss, grad_w), grad_x = _jax.lax.scan(body, init, (per_example, given["loss_target"]))
    with _jax.named_scope("update"):
        delta_w, new_m, new_v = {}, {}, {}
        for n in TWIN_WEIGHTS:
            delta_w[n], new_m[n], new_v[n] = _adamw(weights[n], grad_w[n], given["m_" + n], given["v_" + n])
    return (loss, grad_x, *[grad_w[n] for n in TWIN_WEIGHTS], *[delta_w[n] for n in TWIN_WEIGHTS],
            *[new_m[n] for n in TWIN_WEIGHTS], *[new_v[n] for n in TWIN_WEIGHTS])
```

```python
import math

import jax
import jax.numpy as jnp
from jax import lax
from jax.experimental import pallas as pl
from jax.experimental.pallas import tpu as pltpu

F32 = jnp.float32
BF16 = jnp.bfloat16
MESH = pl.DeviceIdType.MESH

D_MODEL = 1024
N_META = 16
GRID_W = 64
HEAD_DIM = 64
N_Q_HEADS = 16
N_KV_HEADS = 4
ROPE_THETA = 10000.0
QK_EPS = 1e-6
S5_CH = 16
S5_GROUPS = 64
S5_STATE = 64
D_FF = 2816
LN_EPS = 1e-5
DEPTH = 4
ALPHA = (2.0 * DEPTH) ** 0.25
ADAM_LR, ADAM_B1, ADAM_B2, ADAM_EPS, ADAM_WD, ADAM_STEP = 0.001, 0.9, 0.999, 1e-08, 0.01, 10

LANES = 128
SUBLANES = 8
VMEM_LIMIT = 56 * 1024 * 1024

S5_T = 8
S5_GB = LANES // S5_CH
S5_NJ = S5_GROUPS // S5_GB
S5_W = S5_T * LANES
S5_SW = 2 * S5_GB * S5_STATE
S5_HALF = S5_SW // 2


def _round_up(a, b):
    return -(-a // b) * b


def _pick_tile(n, prefs):
    for t in prefs:
        if n % t == 0:
            return t
    return n


def _params(sem=None):
    kw = dict(vmem_limit_bytes=VMEM_LIMIT)
    if sem is not None:
        kw["dimension_semantics"] = sem
    return pltpu.CompilerParams(**kw)


def _dot(a, b, dims):
    return lax.dot_general(a, b, (dims, ((), ())), preferred_element_type=F32)


def _nn(a, b):
    return _dot(a, b, ((1,), (0,)))


def _nt(a, b):
    return _dot(a, b, ((1,), (1,)))


def _tn(a, b):
    return _dot(a, b, ((0,), (0,)))


def _blockdiag(w):
    g, a0, a1, b0, b1 = w.shape
    w = w.reshape(S5_NJ, S5_GB, a0, a1, b0, b1)
    eye = jnp.eye(S5_GB, dtype=w.dtype)
    out = jnp.einsum("jgabcd,gh->jagbchd", w, eye)
    return out.reshape(S5_NJ, a0 * S5_GB * a1, b0 * S5_GB * b1)


def s5_prep(lam_re, lam_im, log_dt, b_re, b_im, c_re, c_im):
    hi = lax.Precision.HIGHEST
    t = S5_T
    dt = jnp.exp(log_dt)[..., None]
    taus = jnp.arange(t + 1, dtype=F32)[:, None, None, None]
    mag = jnp.exp(lam_re * dt)
    ang = lam_im * dt
    pr = jnp.concatenate([jnp.ones_like(mag)[None], (mag * jnp.cos(ang))[None],
                          jnp.exp(lam_re * dt * taus[2:]) * jnp.cos(ang * taus[2:])], axis=0)
    pi = jnp.concatenate([jnp.zeros_like(mag)[None], (mag * jnp.sin(ang))[None],
                          jnp.exp(lam_re * dt * taus[2:]) * jnp.sin(ang * taus[2:])], axis=0)
    abr, abi = pr[1], pi[1]
    nr, ni = abr - 1.0, abi
    den = lam_re * lam_re + lam_im * lam_im
    cr = (nr * lam_re + ni * lam_im) / den
    ci_ = (ni * lam_re - nr * lam_im) / den
    bbr = cr[..., None] * b_re - ci_[..., None] * b_im
    bbi = cr[..., None] * b_im + ci_[..., None] * b_re
    er = c_re[None] * pr[:, :, :, None, :] - c_im[None] * pi[:, :, :, None, :]
    ei = c_re[None] * pi[:, :, :, None, :] + c_im[None] * pr[:, :, :, None, :]
    kk = (jnp.einsum("tdgop,dgpc->tdgoc", er[:t], bbr, precision=hi)
          - jnp.einsum("tdgop,dgpc->tdgoc", ei[:t], bbi, precision=hi))
    s_idx = jnp.arange(t)[:, None]
    i_idx = jnp.arange(t)[None, :]
    kf = jnp.concatenate([kk[:, 0], jnp.zeros_like(kk[:1, 0])], axis=0)
    kr = jnp.concatenate([kk[:, 1], jnp.zeros_like(kk[:1, 1])], axis=0)
    idx_f = jnp.where(i_idx >= s_idx, i_idx - s_idx, t)
    idx_r = jnp.where(s_idx >= i_idx, s_idx - i_idx, t)
    mg = kf[idx_f] + kr[idx_r]
    mg = jnp.transpose(mg, (2, 0, 4, 1, 3))
    m = _blockdiag(mg)
    pw_f = jnp.stack([pr[t - 1 - s, 0] for s in range(t)]), jnp.stack([pi[t - 1 - s, 0] for s in range(t)])
    pw_r = jnp.stack([pr[s, 1] for s in range(t)]), jnp.stack([pi[s, 1] for s in range(t)])
    wx = []
    for d, (qr, qi) in enumerate((pw_f, pw_r)):
        wr = qr[..., None] * bbr[d][None] - qi[..., None] * bbi[d][None]
        wi = qr[..., None] * bbi[d][None] + qi[..., None] * bbr[d][None]
        w = jnp.stack([wr, wi], axis=0)
        w = jnp.transpose(w, (2, 1, 4, 0, 3))
        wx.append(_blockdiag(w))
    ci = []
    for d in range(2):
        exps = [i + 1 for i in range(t)] if d == 0 else [t - i for i in range(t)]
        e_r = jnp.stack([er[e, d] for e in exps])
        e_i = jnp.stack([ei[e, d] for e in exps])
        w = jnp.stack([e_r, -e_i], axis=0)
        w = jnp.transpose(w, (2, 0, 4, 1, 3))
        ci.append(_blockdiag(w))
    at = jnp.stack([pr[t], pi[t]], axis=1)
    at = at.reshape(2, 2, S5_NJ, S5_GB * S5_STATE)
    at = jnp.transpose(at, (0, 2, 1, 3)).reshape(2, S5_NJ, 1, S5_SW)
    return m, jnp.stack(wx), jnp.stack(ci), at


def _chunk_rows(ref, nc):
    return jnp.concatenate([ref[pl.ds(s, nc, stride=S5_T), :] for s in range(S5_T)], axis=1)


def _cmul(ar, ai, sr, si):
    return ar * sr - ai * si, ar * si + ai * sr


def _scan_tiles(nc, reverse, step):
    nt = nc // SUBLANES

    def body(it, carry):
        tix = (nt - 1 - it) if reverse else it
        k0 = pl.multiple_of(tix * SUBLANES, SUBLANES)
        return step(k0, carry)

    return body, nt


def _s5_specs(nc):
    hbm = pl.BlockSpec(memory_space=pl.ANY)
    aspec = pl.BlockSpec((1, 1, S5_SW), lambda j: (j, 0, 0))
    cspec = pl.BlockSpec((1, nc, S5_W), lambda j: (j, 0, 0))
    return hbm, aspec, cspec


def _s5_fetch(j, tok_hbm, w_hbms, tok_s, w_s, sems):
    cols = pl.ds(pl.multiple_of(j * LANES, LANES), LANES)
    cps = [pltpu.make_async_copy(tok_hbm.at[:, cols], tok_s, sems.at[0])]
    for i, w in enumerate(w_hbms):
        cps.append(pltpu.make_async_copy(w.at[j], w_s.at[i], sems.at[1 + i]))
    for cp in cps:
        cp.start()
    return cols, cps


def s5_forward(u, m, wx, ci, at, n_valid):
    lp = u.shape[0]
    nc = lp // S5_T
    nvc = n_valid // S5_T

    def body(u_hbm, m_hbm, wxf_hbm, wxr_hbm, cif_hbm, cir_hbm, atf_ref, atr_ref,
             y_hbm, lhs_ref, sp_ref, sn_ref, tok_s, w_s, xf_s, xr_s, sems):
        j = pl.program_id(0)
        cols, cps = _s5_fetch(j, u_hbm, (m_hbm, wxf_hbm, wxr_hbm, cif_hbm, cir_hbm), tok_s, w_s, sems)
        cps[0].wait()
        lhs = _chunk_rows(tok_s, nc)
        rows = lax.broadcasted_iota(jnp.int32, lhs.shape, 0)
        lhs = jnp.where(rows < nvc, lhs, 0.0).astype(BF16)
        lhs_ref[0] = lhs
        cps[2].wait()
        cps[3].wait()
        xf_s[...] = _nn(lhs, w_s[1])
        xr_s[...] = _nn(lhs, w_s[2])
        afr, afi = atf_ref[0, :, :S5_HALF], atf_ref[0, :, S5_HALF:]
        arr, ari = atr_ref[0, :, :S5_HALF], atr_ref[0, :, S5_HALF:]

        def scan_step(x_s, ar, ai, descending):
            def step(k0, carry):
                sr, si = carry
                x = x_s[pl.ds(k0, SUBLANES), :]
                outs = [None] * SUBLANES
                order = reversed(range(SUBLANES)) if descending else range(SUBLANES)
                for r in order:
                    outs[r] = jnp.concatenate([sr, si], axis=1)
                    nr, ni = _cmul(ar, ai, sr, si)
                    sr = nr + x[r:r + 1, :S5_HALF]
                    si = ni + x[r:r + 1, S5_HALF:]
                x_s[pl.ds(k0, SUBLANES), :] = jnp.concatenate(outs, axis=0)
                return sr, si
            return step

        zero = jnp.zeros((1, S5_HALF), F32)
        fb, nt = _scan_tiles(nc, False, scan_step(xf_s, afr, afi, False))
        lax.fori_loop(0, nt, fb, (zero, zero))
        rb, nt = _scan_tiles(nc, True, scan_step(xr_s, arr, ari, True))
        lax.fori_loop(0, nt, rb, (zero, zero))
        sp = xf_s[...].astype(BF16)
        sn = xr_s[...].astype(BF16)
        sp_ref[0] = sp
        sn_ref[0] = sn
        cps[1].wait()
        cps[4].wait()
        cps[5].wait()
        y = _nn(lhs, w_s[0]) + _nn(sp, w_s[3]) + _nn(sn, w_s[4])
        for i in range(S5_T):
            tok_s[pl.ds(i, nc, stride=S5_T), :] = y[:, i * LANES:(i + 1) * LANES]
        pltpu.sync_copy(tok_s, y_hbm.at[:, cols])

    hbm, aspec, cspec = _s5_specs(nc)
    return pl.pallas_call(
        body, name="s5_forward", grid=(S5_NJ,),
        in_specs=[hbm] * 6 + [aspec, aspec],
        out_specs=[hbm, cspec, cspec, cspec],
        out_shape=[jax.ShapeDtypeStruct((lp, D_MODEL), F32)] + [jax.ShapeDtypeStruct((S5_NJ, nc, S5_W), BF16)] * 3,
        scratch_shapes=[pltpu.VMEM((lp, LANES), F32), pltpu.VMEM((5, S5_W, S5_W), BF16),
                        pltpu.VMEM((nc, S5_SW), F32), pltpu.VMEM((nc, S5_SW), F32), pltpu.SemaphoreType.DMA((6,))],
        compiler_params=_params(("arbitrary",)),
    )(u, m, wx[0], wx[1], ci[0], ci[1], at[0], at[1])


def s5_backward(dy, dhs, m, wx, ci, at, sp, sn, n_valid):
    lp = dy.shape[0]
    nc = lp // S5_T
    nvc = n_valid // S5_T

    def body(dy_hbm, dhs_hbm, m_hbm, wxf_hbm, wxr_hbm, cif_hbm, cir_hbm, atf_ref, atr_ref, sp_ref, sn_ref,
             dh_hbm, ldy_ref, dxf_ref, dxr_ref, daf_ref, dar_ref, tok_s, w_s, gf_s, gr_s, sems):
        j = pl.program_id(0)
        cols, cps = _s5_fetch(j, dy_hbm, (m_hbm, wxf_hbm, wxr_hbm, cif_hbm, cir_hbm), tok_s, w_s, sems)
        cps[0].wait()
        ldy = _chunk_rows(tok_s, nc)
        rows = lax.broadcasted_iota(jnp.int32, ldy.shape, 0)
        ldy = jnp.where(rows < nvc, ldy, 0.0).astype(BF16)
        ldy_ref[0] = ldy
        resid = pltpu.make_async_copy(dhs_hbm.at[:, cols], tok_s, sems.at[0])
        resid.start()
        cps[4].wait()
        cps[5].wait()
        gf_s[...] = _nt(ldy, w_s[3])
        gr_s[...] = _nt(ldy, w_s[4])
        afr, afi = atf_ref[0, :, :S5_HALF], atf_ref[0, :, S5_HALF:]
        arr, ari = atr_ref[0, :, :S5_HALF], atr_ref[0, :, S5_HALF:]

        def adj_step(g_s, s_ref, ar, ai, descending):
            def step(k0, carry):
                gr_, gi_, dr_, di_ = carry
                g = g_s[pl.ds(k0, SUBLANES), :]
                p = s_ref[0, pl.ds(k0, SUBLANES), :].astype(F32)
                outs = [None] * SUBLANES
                order = reversed(range(SUBLANES)) if descending else range(SUBLANES)
                for r in order:
                    outs[r] = jnp.concatenate([gr_, gi_], axis=1)
                    pr_, pi_ = p[r:r + 1, :S5_HALF], p[r:r + 1, S5_HALF:]
                    dr_ = dr_ + gr_ * pr_ + gi_ * pi_
                    di_ = di_ + gi_ * pr_ - gr_ * pi_
                    nr, ni = _cmul(ar, -ai, gr_, gi_)
                    gr_ = nr + g[r:r + 1, :S5_HALF]
                    gi_ = ni + g[r:r + 1, S5_HALF:]
                g_s[pl.ds(k0, SUBLANES), :] = jnp.concatenate(outs, axis=0)
                return gr_, gi_, dr_, di_
            return step

        zero = jnp.zeros((1, S5_HALF), F32)
        fb, nt = _scan_tiles(nc, True, adj_step(gf_s, sp_ref, afr, afi, True))
        _, _, dr_, di_ = lax.fori_loop(0, nt, fb, (zero,) * 4)
        daf_ref[0] = jnp.concatenate([dr_, di_], axis=1)
        rb, nt = _scan_tiles(nc, False, adj_step(gr_s, sn_ref, arr, ari, False))
        _, _, dr_, di_ = lax.fori_loop(0, nt, rb, (zero,) * 4)
        dar_ref[0] = jnp.concatenate([dr_, di_], axis=1)
        dxf = gf_s[...].astype(BF16)
        dxr = gr_s[...].astype(BF16)
        dxf_ref[0] = dxf
        dxr_ref[0] = dxr
        cps[1].wait()
        cps[2].wait()
        cps[3].wait()
        du = _nt(ldy, w_s[0]) + _nt(dxf, w_s[1]) + _nt(dxr, w_s[2])
        rows = lax.broadcasted_iota(jnp.int32, du.shape, 0)
        du = jnp.where(rows < nvc, du, 0.0)
        resid.wait()
        for s in range(S5_T):
            tok_s[pl.ds(s, nc, stride=S5_T), :] += du[:, s * LANES:(s + 1) * LANES]
        pltpu.sync_copy(tok_s, dh_hbm.at[:, cols])

    hbm, aspec, cspec = _s5_specs(nc)
    return pl.pallas_call(
        body, name="s5_backward", grid=(S5_NJ,),
        in_specs=[hbm] * 7 + [aspec, aspec, cspec, cspec],
        out_specs=[hbm, cspec, cspec, cspec, aspec, aspec],
        out_shape=[jax.ShapeDtypeStruct((lp, D_MODEL), F32)] + [jax.ShapeDtypeStruct((S5_NJ, nc, S5_W), BF16)] * 3
        + [jax.ShapeDtypeStruct((S5_NJ, 1, S5_SW), F32)] * 2,
        scratch_shapes=[pltpu.VMEM((lp, LANES), F32), pltpu.VMEM((5, S5_W, S5_W), BF16),
                        pltpu.VMEM((nc, S5_SW), F32), pltpu.VMEM((nc, S5_SW), F32), pltpu.SemaphoreType.DMA((6,))],
        compiler_params=_params(("arbitrary",)),
    )(dy, dhs, m, wx[0], wx[1], ci[0], ci[1], at[0], at[1], sp, sn)


def bmm_tn(a, b, name):
    nj, k, wa = a.shape
    wb = b.shape[2]

    def body(a_ref, b_ref, o_ref):
        o_ref[0] = _tn(a_ref[0], b_ref[0])

    return pl.pallas_call(
        body, name=name, grid=(nj,),
        in_specs=[pl.BlockSpec((1, k, wa), lambda j: (j, 0, 0)), pl.BlockSpec((1, k, wb), lambda j: (j, 0, 0))],
        out_specs=pl.BlockSpec((1, wa, wb), lambda j: (j, 0, 0)),
        out_shape=jax.ShapeDtypeStruct((nj, wa, wb), F32),
        compiler_params=_params(("arbitrary",)),
    )(a, b)


def _tm(lp):
    return _pick_tile(lp, (640, 128))


def _row_spec(tm, width):
    return pl.BlockSpec((tm, width), lambda i: (i, 0))


def _full_spec(shape):
    return pl.BlockSpec(shape, lambda *_: (0,) * len(shape))


def _gelu(v):
    return 0.5 * v * (1.0 + lax.erf(v * (2.0 ** -0.5)))


def _gelu_grad(v):
    return 0.5 * (1.0 + lax.erf(v * (2.0 ** -0.5))) + v * jnp.exp(-0.5 * v * v) * (2.0 * math.pi) ** -0.5


def _layer_norm(r, gain, bias):
    mean = jnp.mean(r, axis=-1, keepdims=True)
    c = r - mean
    var = jnp.mean(c * c, axis=-1, keepdims=True)
    return c * lax.rsqrt(var + LN_EPS) * gain + bias


def glu_forward(y, h, dvec, wglu):
    lp, d = y.shape
    tm = _tm(lp)

    def body(y_ref, h_ref, d_ref, w_ref, v_ref, t_ref, g_ref, z_ref):
        v = y_ref[...] + d_ref[...] * h_ref[...]
        g = _gelu(v)
        gb = g.astype(BF16)
        t = _nn(gb, w_ref[...])
        v_ref[...] = v
        t_ref[...] = t
        g_ref[...] = gb
        z_ref[...] = (g * jax.nn.sigmoid(t)).astype(BF16)

    rs = _row_spec(tm, d)
    return pl.pallas_call(
        body, name="glu_forward", grid=(lp // tm,),
        in_specs=[rs, rs, _full_spec((1, d)), _full_spec((d, d))],
        out_specs=[rs, rs, rs, rs],
        out_shape=[jax.ShapeDtypeStruct((lp, d), F32)] * 2 + [jax.ShapeDtypeStruct((lp, d), BF16)] * 2,
        compiler_params=_params(("parallel",)),
    )(y, h, dvec, wglu)


def proj_ln_forward(z, w, h, gain, bias, name):
    lp, k = z.shape
    d = w.shape[1]
    tm = _tm(lp)

    def body(z_ref, w_ref, h_ref, g_ref, b_ref, r_ref, o_ref):
        r = ALPHA * h_ref[...] + _nn(z_ref[...], w_ref[...])
        r_ref[...] = r
        o_ref[...] = _layer_norm(r, g_ref[...], b_ref[...])

    rs = _row_spec(tm, d)
    return pl.pallas_call(
        body, name=name, grid=(lp // tm,),
        in_specs=[_row_spec(tm, k), _full_spec((k, d)), rs, _full_spec((1, d)), _full_spec((1, d))],
        out_specs=[rs, rs],
        out_shape=[jax.ShapeDtypeStruct((lp, d), F32)] * 2,
        compiler_params=_params(("parallel",)),
    )(z, w, h, gain, bias)


FFN_NB = 1408


def ffn_up_forward(h, wg, wu):
    lp, d = h.shape
    dff = wg.shape[1]
    tm = _tm(lp)

    def body(h_ref, wg_ref, wu_ref, a_ref, b_ref, f_ref):
        hb = h_ref[...].astype(BF16)
        a = _nn(hb, wg_ref[...])
        b = _nn(hb, wu_ref[...])
        a_ref[...] = a.astype(BF16)
        b_ref[...] = b.astype(BF16)
        f_ref[...] = (a * jax.nn.sigmoid(a) * b).astype(BF16)

    ws = pl.BlockSpec((d, FFN_NB), lambda n, i: (0, n))
    os_ = pl.BlockSpec((tm, FFN_NB), lambda n, i: (i, n))
    return pl.pallas_call(
        body, name="ffn_up_forward", grid=(dff // FFN_NB, lp // tm),
        in_specs=[pl.BlockSpec((tm, d), lambda n, i: (i, 0)), ws, ws],
        out_specs=[os_, os_, os_],
        out_shape=[jax.ShapeDtypeStruct((lp, dff), BF16)] * 3,
        compiler_params=_params(("parallel", "parallel")),
    )(h, wg, wu)


def ln_backward(dh, r, gain):
    lp, d = dh.shape
    tm = _tm(lp)

    def body(dh_ref, r_ref, g_ref, dr_ref, s_ref):
        r_ = r_ref[...]
        dh_ = dh_ref[...]
        mean = jnp.mean(r_, axis=-1, keepdims=True)
        c = r_ - mean
        var = jnp.mean(c * c, axis=-1, keepdims=True)
        rstd = lax.rsqrt(var + LN_EPS)
        xh = c * rstd
        dxh = dh_ * g_ref[...]
        m1 = jnp.mean(dxh, axis=-1, keepdims=True)
        m2 = jnp.mean(dxh * xh, axis=-1, keepdims=True)
        dr_ref[...] = rstd * (dxh - m1 - xh * m2)

        @pl.when(pl.program_id(0) == 0)
        def _():
            s_ref[...] = jnp.zeros_like(s_ref)

        s_ref[0:1, :] += jnp.sum(dh_ * xh, axis=0, keepdims=True)
        s_ref[1:2, :] += jnp.sum(dh_, axis=0, keepdims=True)

    rs = _row_spec(tm, d)
    return pl.pallas_call(
        body, name="ln_backward", grid=(lp // tm,),
        in_specs=[rs, rs, _full_spec((1, d))],
        out_specs=[rs, _full_spec((SUBLANES, d))],
        out_shape=[jax.ShapeDtypeStruct((lp, d), F32), jax.ShapeDtypeStruct((SUBLANES, d), F32)],
        compiler_params=_params(("arbitrary",)),
    )(dh, r, gain)


def ffn_backward_act(dr, wd, a, b):
    lp, d = dr.shape
    dff = wd.shape[0]
    tm = _tm(lp)

    def body(dr_ref, wd_ref, a_ref, b_ref, da_ref, db_ref):
        df = _nt(dr_ref[...].astype(BF16), wd_ref[...])
        a_ = a_ref[...].astype(F32)
        b_ = b_ref[...].astype(F32)
        sg = jax.nn.sigmoid(a_)
        da_ref[...] = (df * b_ * sg * (1.0 + a_ * (1.0 - sg))).astype(BF16)
        db_ref[...] = (df * a_ * sg).astype(BF16)

    os_ = pl.BlockSpec((tm, FFN_NB), lambda n, i: (i, n))
    return pl.pallas_call(
        body, name="ffn_backward_act", grid=(dff // FFN_NB, lp // tm),
        in_specs=[pl.BlockSpec((tm, d), lambda n, i: (i, 0)), pl.BlockSpec((FFN_NB, d), lambda n, i: (n, 0)), os_, os_],
        out_specs=[os_, os_],
        out_shape=[jax.ShapeDtypeStruct((lp, dff), BF16)] * 2,
        compiler_params=_params(("parallel", "parallel")),
    )(dr, wd, a, b)


def resid_nt(dr, xs, ws, name):
    lp, d = dr.shape
    tm = _tm(lp)
    n = len(xs)

    def body(*refs):
        acc = ALPHA * refs[0][...]
        for i in range(n):
            acc = acc + _nt(refs[1 + i][...], refs[1 + n + i][...])
        refs[-1][...] = acc

    rs = _row_spec(tm, d)
    in_specs = [rs] + [_row_spec(tm, x.shape[1]) for x in xs] + [_full_spec(w.shape) for w in ws]
    return pl.pallas_call(
        body, name=name, grid=(lp // tm,),
        in_specs=in_specs, out_specs=rs,
        out_shape=jax.ShapeDtypeStruct((lp, d), F32),
        compiler_params=_params(("parallel",)),
    )(dr, *xs, *ws)


def mm_tn(x, y, name):
    lp, k = x.shape
    n = y.shape[1]
    tm = _tm(lp)
    nb = _pick_tile(n, (512, 1408))

    def body(x_ref, y_ref, o_ref):
        @pl.when(pl.program_id(1) == 0)
        def _():
            o_ref[...] = jnp.zeros_like(o_ref)

        o_ref[...] += _tn(x_ref[...].astype(BF16), y_ref[...].astype(BF16))

    return pl.pallas_call(
        body, name=name, grid=(n // nb, lp // tm),
        in_specs=[pl.BlockSpec((tm, k), lambda j, i: (i, 0)), pl.BlockSpec((tm, nb), lambda j, i: (i, j))],
        out_specs=pl.BlockSpec((k, nb), lambda j, i: (0, j)),
        out_shape=jax.ShapeDtypeStruct((k, n), F32),
        compiler_params=_params(("parallel", "arbitrary")),
    )(x, y)


def glu_backward1(dr, wout, g, t):
    lp, d = dr.shape
    tm = _tm(lp)

    def body(dr_ref, w_ref, g_ref, t_ref, dt_ref, dgd_ref):
        dz = _nt(dr_ref[...].astype(BF16), w_ref[...])
        s = jax.nn.sigmoid(t_ref[...])
        dgd_ref[...] = dz * s
        dt_ref[...] = (dz * g_ref[...].astype(F32) * s * (1.0 - s)).astype(BF16)

    rs = _row_spec(tm, d)
    return pl.pallas_call(
        body, name="glu_backward1", grid=(lp // tm,),
        in_specs=[rs, _full_spec((d, d)), rs, rs],
        out_specs=[rs, rs],
        out_shape=[jax.ShapeDtypeStruct((lp, d), BF16), jax.ShapeDtypeStruct((lp, d), F32)],
        compiler_params=_params(("parallel",)),
    )(dr, wout, g, t)


def glu_backward2(dt, dgd, wglu, v, h, dvec, dr):
    lp, d = dt.shape
    tm = _tm(lp)

    def body(dt_ref, dgd_ref, w_ref, v_ref, h_ref, d_ref, dr_ref, dv_ref, dhs_ref, s_ref):
        dg = dgd_ref[...] + _nt(dt_ref[...], w_ref[...])
        dv = dg * _gelu_grad(v_ref[...])
        dv_ref[...] = dv
        dhs_ref[...] = ALPHA * dr_ref[...] + dv * d_ref[...]

        @pl.when(pl.program_id(0) == 0)
        def _():
            s_ref[...] = jnp.zeros_like(s_ref)

        s_ref[0:1, :] += jnp.sum(dv * h_ref[...], axis=0, keepdims=True)

    rs = _row_spec(tm, d)
    return pl.pallas_call(
        body, name="glu_backward2", grid=(lp // tm,),
        in_specs=[rs, rs, _full_spec((d, d)), rs, rs, _full_spec((1, d)), rs],
        out_specs=[rs, rs, _full_spec((SUBLANES, d))],
        out_shape=[jax.ShapeDtypeStruct((lp, d), F32)] * 2 + [jax.ShapeDtypeStruct((SUBLANES, d), F32)],
        compiler_params=_params(("arbitrary",)),
    )(dt, dgd, wglu, v, h, dvec, dr)


def loss_backward(hf, tgt, n_valid):
    lp, d = hf.shape
    tm = _tm(lp)

    def body(h_ref, t_ref, dh_ref, s_ref):
        rows = pl.program_id(0) * tm + lax.broadcasted_iota(jnp.int32, (tm, d), 0)
        ok = (rows >= N_META) & (rows < n_valid)
        e = jnp.where(ok, h_ref[...] - t_ref[...], 0.0)
        dh_ref[...] = e * (1.0 / d)

        @pl.when(pl.program_id(0) == 0)
        def _():
            s_ref[...] = jnp.zeros_like(s_ref)

        sq = e * e
        part = sq[:, 0:LANES]
        for c in range(1, d // LANES):
            part = part + sq[:, c * LANES:(c + 1) * LANES]
        acc = part[0:SUBLANES]
        for r in range(1, tm // SUBLANES):
            acc = acc + part[r * SUBLANES:(r + 1) * SUBLANES]
        s_ref[...] += acc * (0.5 / d)

    rs = _row_spec(tm, d)
    return pl.pallas_call(
        body, name="loss_backward", grid=(lp // tm,),
        in_specs=[rs, rs], out_specs=[rs, _full_spec((SUBLANES, LANES))],
        out_shape=[jax.ShapeDtypeStruct((lp, d), F32), jax.ShapeDtypeStruct((SUBLANES, LANES), F32)],
        compiler_params=_params(("arbitrary",)),
    )(hf, tgt)


N_QB = N_Q_HEADS // 2
N_KB = N_KV_HEADS
QKV_W = (N_QB + 2 * N_KB) * LANES


def rope_tables(lp, n_valid):
    t = jnp.arange(lp, dtype=jnp.int32)
    real = (t >= N_META) & (t < n_valid)
    pos = jnp.where(real, t - N_META, 0)
    row = (pos // GRID_W).astype(F32)
    col = (pos % GRID_W).astype(F32)
    axis_dim = HEAD_DIM // 2
    inv = ROPE_THETA ** (-jnp.arange(0, axis_dim, 2, dtype=F32) / axis_dim)
    ar = row[:, None] * inv[None, :]
    ac = col[:, None] * inv[None, :]
    cos = jnp.concatenate([jnp.cos(ar), jnp.cos(ar), jnp.cos(ac), jnp.cos(ac)], axis=1)
    sin = jnp.concatenate([-jnp.sin(ar), jnp.sin(ar), -jnp.sin(ac), jnp.sin(ac)], axis=1)
    return jnp.tile(cos, (1, 2)), jnp.tile(sin, (1, 2))


def head_sum_matrix():
    return jnp.kron(jnp.eye(2, dtype=F32), jnp.ones((HEAD_DIM, HEAD_DIM), F32)).astype(BF16)


def _segsum(x, e):
    hi = x.astype(BF16)
    lo = (x - hi.astype(F32)).astype(BF16)
    return _nn(hi, e) + _nn(lo, e)


def _swap_halves(x):
    lane = lax.broadcasted_iota(jnp.int32, x.shape, 1)
    quarter = HEAD_DIM // 4
    return jnp.where(lane % (2 * quarter) < quarter, pltpu.roll(x, LANES - quarter, 1), pltpu.roll(x, quarter, 1))


def qkv_forward(h, w2, gq, gk, cos, sin, e):
    lp, d = h.shape
    tm = _tm(lp)
    kw, vw = N_KB * LANES, N_KB * LANES

    def body(h_ref, w_ref, gq_ref, gk_ref, cos_ref, sin_ref, e_ref, raw_ref, q_ref, k_ref, v_ref):
        raw = _nn(h_ref[...].astype(BF16), w_ref[...])
        raw_ref[...] = raw
        c, s_, em = cos_ref[...], sin_ref[...], e_ref[...]
        for cb in range(N_QB + N_KB):
            t = raw[:, cb * LANES:(cb + 1) * LANES]
            rstd = lax.rsqrt(_segsum(t * t, em) * (1.0 / HEAD_DIM) + QK_EPS)
            n = t * rstd * (gq_ref[...] if cb < N_QB else gk_ref[...])
            rot = n * c + _swap_halves(n) * s_
            if cb < N_QB:
                q_ref[:, cb * LANES:(cb + 1) * LANES] = (rot * (HEAD_DIM ** -0.5)).astype(BF16)
            else:
                k_ref[:, (cb - N_QB) * LANES:(cb - N_QB + 1) * LANES] = rot.astype(BF16)
        v_ref[...] = raw[:, (N_QB + N_KB) * LANES:].astype(BF16)

    return pl.pallas_call(
        body, name="qkv_forward", grid=(lp // tm,),
        in_specs=[_row_spec(tm, d), _full_spec((d, QKV_W)), _full_spec((1, LANES)), _full_spec((1, LANES)),
                  _row_spec(tm, LANES), _row_spec(tm, LANES), _full_spec((LANES, LANES))],
        out_specs=[_row_spec(tm, QKV_W), _row_spec(tm, N_QB * LANES), _row_spec(tm, kw), _row_spec(tm, vw)],
        out_shape=[jax.ShapeDtypeStruct((lp, QKV_W), F32), jax.ShapeDtypeStruct((lp, N_QB * LANES), BF16),
                   jax.ShapeDtypeStruct((lp, kw), BF16), jax.ShapeDtypeStruct((lp, vw), BF16)],
        compiler_params=_params(("parallel",)),
    )(h, w2, gq, gk, cos, sin, e)


def qkv_backward(dqs, dk2, dv2, raw, gq, gk, cos, sin, e):
    lp = raw.shape[0]
    tm = _tm(lp)

    def body(dq_ref, dk_ref, dv_ref, raw_ref, gq_ref, gk_ref, cos_ref, sin_ref, e_ref, d_ref, s_ref):
        @pl.when(pl.program_id(0) == 0)
        def _():
            s_ref[...] = jnp.zeros_like(s_ref)

        c, s_, em = cos_ref[...], sin_ref[...], e_ref[...]
        gsum = [jnp.zeros((1, LANES), F32), jnp.zeros((1, LANES), F32)]
        for cb in range(N_QB + N_KB):
            isq = cb < N_QB
            t = raw_ref[:, cb * LANES:(cb + 1) * LANES]
            if isq:
                drot = dq_ref[:, cb * LANES:(cb + 1) * LANES] * (HEAD_DIM ** -0.5)
            else:
                drot = dk_ref[:, (cb - N_QB) * LANES:(cb - N_QB + 1) * LANES]
            gain = gq_ref[...] if isq else gk_ref[...]
            rstd = lax.rsqrt(_segsum(t * t, em) * (1.0 / HEAD_DIM) + QK_EPS)
            dn = drot * c + _swap_halves(drot * s_)
            xh = t * rstd
            gsum[0 if isq else 1] = gsum[0 if isq else 1] + jnp.sum(dn * xh, axis=0, keepdims=True)
            w = dn * gain
            mw = _segsum(w * xh, em) * (1.0 / HEAD_DIM)
            d_ref[:, cb * LANES:(cb + 1) * LANES] = (rstd * (w - xh * mw)).astype(BF16)
        d_ref[:, (N_QB + N_KB) * LANES:] = dv_ref[...].astype(BF16)
        s_ref[0:1, :] += gsum[0]
        s_ref[1:2, :] += gsum[1]

    kw = N_KB * LANES
    return pl.pallas_call(
        body, name="qkv_backward", grid=(lp // tm,),
        in_specs=[_row_spec(tm, N_QB * LANES), _row_spec(tm, kw), _row_spec(tm, kw), _row_spec(tm, QKV_W),
                  _full_spec((1, LANES)), _full_spec((1, LANES)), _row_spec(tm, LANES), _row_spec(tm, LANES),
                  _full_spec((LANES, LANES))],
        out_specs=[_row_spec(tm, QKV_W), _full_spec((SUBLANES, LANES))],
        out_shape=[jax.ShapeDtypeStruct((lp, QKV_W), BF16), jax.ShapeDtypeStruct((SUBLANES, LANES), F32)],
        compiler_params=_params(("arbitrary",)),
    )(dqs, dk2, dv2, raw, gq, gk, cos, sin, e)


NEG = -1e30
Q_PER_KV = N_Q_HEADS // N_KV_HEADS


def _attn_tiles(lp):
    t = _pick_tile(lp, (640, 128))
    return t, t


def _half_masks(x):
    lane = lax.broadcasted_iota(jnp.int32, x.shape, 1)
    zero = jnp.zeros_like(x)
    return jnp.where(lane < HEAD_DIM, x, zero), jnp.where(lane >= HEAD_DIM, x, zero)


def attn_forward(qs, k2, v2, n_valid):
    lp = qs.shape[0]
    tq, kb = _attn_tiles(lp)
    nk = lp // kb
    gw = 2 * LANES

    def body(q_ref, k_ref, v_ref, o_ref, lse_ref, m_s, l_s, acc_s):
        j = pl.program_id(2)

        @pl.when(j == 0)
        def _():
            m_s[...] = jnp.full_like(m_s, NEG)
            l_s[...] = jnp.zeros_like(l_s)
            acc_s[...] = jnp.zeros_like(acc_s)

        ks = _half_masks(k_ref[...])
        vs = _half_masks(v_ref[...])
        col = j * kb + lax.broadcasted_iota(jnp.int32, (1, kb), 1)
        bias = jnp.where(col < n_valid, 0.0, NEG)
        low = lax.broadcasted_iota(jnp.int32, (tq, LANES), 1) < HEAD_DIM
        for pair in range(2):
            qp = q_ref[:, pair * LANES:(pair + 1) * LANES]
            for half in range(2):
                hh = 2 * pair + half
                s = _nt(qp, ks[half]) + bias
                m_prev = m_s[hh]
                m_new = jnp.maximum(m_prev, jnp.max(s, axis=1, keepdims=True))
                alpha = jnp.exp(m_prev - m_new)
                p = jnp.exp(s - m_new)
                l_s[hh] = alpha * l_s[hh] + jnp.sum(p, axis=1, keepdims=True)
                m_s[hh] = m_new
                pv = _nn(p.astype(BF16), vs[half])
                keep = low if half == 0 else jnp.logical_not(low)
                acc_s[pair] = jnp.where(keep, alpha, 1.0) * acc_s[pair] + pv

        @pl.when(j == nk - 1)
        def _():
            for pair in range(2):
                inv = jnp.where(low, 1.0 / l_s[2 * pair], 1.0 / l_s[2 * pair + 1])
                o_ref[:, pair * LANES:(pair + 1) * LANES] = (acc_s[pair] * inv).astype(BF16)
            for hh in range(Q_PER_KV):
                lse_ref[0, hh] = m_s[hh] + jnp.log(l_s[hh])

    return pl.pallas_call(
        body, name="attn_forward", grid=(N_KV_HEADS, lp // tq, nk),
        in_specs=[pl.BlockSpec((tq, gw), lambda g, i, j: (i, g)), pl.BlockSpec((kb, LANES), lambda g, i, j: (j, g)),
                  pl.BlockSpec((kb, LANES), lambda g, i, j: (j, g))],
        out_specs=[pl.BlockSpec((tq, gw), lambda g, i, j: (i, g)),
                   pl.BlockSpec((1, Q_PER_KV, tq, 1), lambda g, i, j: (g, 0, i, 0))],
        out_shape=[jax.ShapeDtypeStruct((lp, N_QB * LANES), BF16),
                   jax.ShapeDtypeStruct((N_KV_HEADS, Q_PER_KV, lp, 1), F32)],
        scratch_shapes=[pltpu.VMEM((Q_PER_KV, tq, 1), F32), pltpu.VMEM((Q_PER_KV, tq, 1), F32),
                        pltpu.VMEM((2, tq, LANES), F32)],
        compiler_params=_params(("parallel", "parallel", "arbitrary")),
    )(qs, k2, v2)


def attn_backward_q(qs, k2, v2, do, lse, delta, n_valid):
    lp = qs.shape[0]
    tq, kb = _attn_tiles(lp)
    nk = lp // kb
    gw = 2 * LANES

    def body(q_ref, k_ref, v_ref, do_ref, lse_ref, dl_ref, dq_ref, acc_s):
        j = pl.program_id(2)

        @pl.when(j == 0)
        def _():
            acc_s[...] = jnp.zeros_like(acc_s)

        ks = _half_masks(k_ref[...])
        vs = _half_masks(v_ref[...])
        col = j * kb + lax.broadcasted_iota(jnp.int32, (1, kb), 1)
        bias = jnp.where(col < n_valid, 0.0, NEG)
        for pair in range(2):
            qp = q_ref[:, pair * LANES:(pair + 1) * LANES]
            dop = do_ref[:, pair * LANES:(pair + 1) * LANES]
            for half in range(2):
                hh = 2 * pair + half
                s = _nt(qp, ks[half]) + bias
                p = jnp.exp(s - lse_ref[0, hh])
                dp = _nt(dop, vs[half])
                ds = p * (dp - dl_ref[0, hh])
                acc_s[pair] += _nn(ds.astype(BF16), ks[half])

        @pl.when(j == nk - 1)
        def _():
            for pair in range(2):
                dq_ref[:, pair * LANES:(pair + 1) * LANES] = acc_s[pair]

    cspec = pl.BlockSpec((1, Q_PER_KV, tq, 1), lambda g, i, j: (g, 0, i, 0))
    qspec = pl.BlockSpec((tq, gw), lambda g, i, j: (i, g))
    kspec = pl.BlockSpec((kb, LANES), lambda g, i, j: (j, g))
    return pl.pallas_call(
        body, name="attn_backward_q", grid=(N_KV_HEADS, lp // tq, nk),
        in_specs=[qspec, kspec, kspec, qspec, cspec, cspec],
        out_specs=qspec,
        out_shape=jax.ShapeDtypeStruct((lp, N_QB * LANES), F32),
        scratch_shapes=[pltpu.VMEM((2, tq, LANES), F32)],
        compiler_params=_params(("parallel", "parallel", "arbitrary")),
    )(qs, k2, v2, do, lse, delta)


def attn_backward_kv(qs, k2, v2, do, lse_row, delta_row, n_valid):
    lp = qs.shape[0]
    tq, kb = _attn_tiles(lp)
    nq = lp // tq
    gw = 2 * LANES

    def body(q_ref, k_ref, v_ref, do_ref, lse_ref, dl_ref, dk_ref, dv_ref, dk_s, dv_s):
        j = pl.program_id(1)
        i = pl.program_id(2)

        @pl.when(i == 0)
        def _():
            dk_s[...] = jnp.zeros_like(dk_s)
            dv_s[...] = jnp.zeros_like(dv_s)

        ks = _half_masks(k_ref[...])
        vs = _half_masks(v_ref[...])
        row = j * kb + lax.broadcasted_iota(jnp.int32, (kb, 1), 0)
        bias = jnp.where(row < n_valid, 0.0, NEG)
        for pair in range(2):
            qp = q_ref[:, pair * LANES:(pair + 1) * LANES]
            dop = do_ref[:, pair * LANES:(pair + 1) * LANES]
            qh = _half_masks(qp)
            doh = _half_masks(dop)
            for half in range(2):
                hh = 2 * pair + half
                st = _nt(ks[half], qp) + bias
                pt = jnp.exp(st - lse_ref[0, hh])
                dv_s[...] += _nn(pt.astype(BF16), doh[half])
                dpt = _nt(vs[half], dop)
                dst = pt * (dpt - dl_ref[0, hh])
                dk_s[...] += _nn(dst.astype(BF16), qh[half])

        @pl.when(i == nq - 1)
        def _():
            dk_ref[...] = dk_s[...]
            dv_ref[...] = dv_s[...]

    rspec = pl.BlockSpec((1, Q_PER_KV, 1, tq), lambda g, j, i: (g, 0, 0, i))
    qspec = pl.BlockSpec((tq, gw), lambda g, j, i: (i, g))
    kspec = pl.BlockSpec((kb, LANES), lambda g, j, i: (j, g))
    return pl.pallas_call(
        body, name="attn_backward_kv", grid=(N_KV_HEADS, lp // kb, nq),
        in_specs=[qspec, kspec, kspec, qspec, rspec, rspec],
        out_specs=[kspec, kspec],
        out_shape=[jax.ShapeDtypeStruct((lp, N_KB * LANES), F32)] * 2,
        scratch_shapes=[pltpu.VMEM((kb, LANES), F32)] * 2,
        compiler_params=_params(("parallel", "parallel", "arbitrary")),
    )(qs, k2, v2, do, lse_row, delta_row)


def attn_out_backward(dr, wout, o, e16):
    lp, d = dr.shape
    tm = _tm(lp)

    def body(dr_ref, w_ref, o_ref, e_ref, do_ref, dl_ref):
        do = _nt(dr_ref[...].astype(BF16), w_ref[...]).astype(BF16)
        do_ref[...] = do
        dl_ref[...] = _segsum(do.astype(F32) * o_ref[...].astype(F32), e_ref[...])

    rs = _row_spec(tm, d)
    return pl.pallas_call(
        body, name="attn_out_backward", grid=(lp // tm,),
        in_specs=[rs, _full_spec((d, d)), rs, _full_spec((d, N_Q_HEADS))],
        out_specs=[rs, _row_spec(tm, N_Q_HEADS)],
        out_shape=[jax.ShapeDtypeStruct((lp, d), BF16), jax.ShapeDtypeStruct((lp, N_Q_HEADS), F32)],
        compiler_params=_params(("parallel",)),
    )(dr, wout, o, e16)


N_CHIPS = 4


def _mesh_pos():
    return lax.axis_index("x"), lax.axis_index("y"), lax.axis_index("c")


def chip_exchange(arrs, scatter, name):
    n = len(arrs)
    hbm = pl.BlockSpec(memory_space=pl.ANY)

    def body(*refs):
        ins, outs = refs[:n], refs[n:2 * n]
        send_sems, recv_sems, loc_sems = refs[2 * n:]
        x, y, c = _mesh_pos()
        me = 2 * x + y
        chips = [(1 - x, y), (x, 1 - y), (1 - x, 1 - y)]
        started = []
        for a in range(n):
            loc = pltpu.make_async_copy(ins[a].at[me] if scatter else ins[a], outs[a].at[me], loc_sems.at[a])
            loc.start()
            started.append(loc)
            for k, (px, py) in enumerate(chips):
                src = ins[a].at[2 * px + py] if scatter else ins[a]
                cp = pltpu.make_async_remote_copy(
                    src_ref=src, dst_ref=outs[a].at[me], send_sem=send_sems.at[3 * a + k], recv_sem=recv_sems.at[3 * a + k],
                    device_id=(px, py, c), device_id_type=MESH)
                cp.start()
                started.append(cp)
        for cp in started:
            cp.wait()

    out_shape = [jax.ShapeDtypeStruct(a.shape if scatter else (N_CHIPS,) + a.shape, a.dtype) for a in arrs]
    return pl.pallas_call(
        body, name=name, in_specs=[hbm] * n, out_specs=[hbm] * n, out_shape=out_shape,
        scratch_shapes=[pltpu.SemaphoreType.DMA((3 * n,)), pltpu.SemaphoreType.DMA((3 * n,)), pltpu.SemaphoreType.DMA((n,))],
    )(*arrs)


def sibling_exchange(arrs, name):
    n = len(arrs)
    hbm = pl.BlockSpec(memory_space=pl.ANY)

    def body(*refs):
        ins, outs = refs[:n], refs[n:2 * n]
        send_sems, recv_sems = refs[2 * n:]
        x, y, c = _mesh_pos()
        started = []
        for a in range(n):
            cp = pltpu.make_async_remote_copy(
                src_ref=ins[a], dst_ref=outs[a], send_sem=send_sems.at[a], recv_sem=recv_sems.at[a],
                device_id=(x, y, 1 - c), device_id_type=MESH)
            cp.start()
            started.append(cp)
        for cp in started:
            cp.wait()

    return pl.pallas_call(
        body, name=name, in_specs=[hbm] * n, out_specs=[hbm] * n,
        out_shape=[jax.ShapeDtypeStruct(a.shape, a.dtype) for a in arrs],
        scratch_shapes=[pltpu.SemaphoreType.DMA((n,)), pltpu.SemaphoreType.DMA((n,))],
    )(*arrs)


def _rows_tile(r, c):
    return _pick_tile(r, tuple(t for t in (512, 256, 128, 64, 32, 16, 8) if t * c * 4 <= 2 * 1024 * 1024))


def chip_sum(recv, name):
    _, r, c = recv.shape
    tr = _rows_tile(r, c)

    def body(r_ref, o_ref):
        acc = r_ref[0].astype(F32)
        for q in range(1, N_CHIPS):
            acc = acc + r_ref[q].astype(F32)
        o_ref[...] = acc

    return pl.pallas_call(
        body, name=name, grid=(r // tr,),
        in_specs=[pl.BlockSpec((N_CHIPS, tr, c), lambda i: (0, i, 0))],
        out_specs=pl.BlockSpec((tr, c), lambda i: (i, 0)),
        out_shape=jax.ShapeDtypeStruct((r, c), F32),
        compiler_params=_params(("parallel",)),
    )(recv)


def pair_sum(part, sib, name):
    r, c = part.shape
    tr = _rows_tile(r, c)

    def body(p_ref, s_ref, o_ref):
        o_ref[...] = p_ref[...] + s_ref[...]

    rs = pl.BlockSpec((tr, c), lambda i: (i, 0))
    return pl.pallas_call(
        body, name=name, grid=(r // tr,), in_specs=[rs] * 2, out_specs=rs,
        out_shape=jax.ShapeDtypeStruct((r, c), F32), compiler_params=_params(("parallel",)),
    )(part, sib)


def adamw(part, sib, w, m, v, name):
    r, c = w.shape
    tr = _rows_tile(r, c)

    def body(p_ref, s_ref, w_ref, m_ref, v_ref, g_ref, d_ref, nm_ref, nv_ref):
        g = p_ref[...] + s_ref[...]
        m_ = ADAM_B1 * m_ref[...] + (1.0 - ADAM_B1) * g
        v_ = ADAM_B2 * v_ref[...] + (1.0 - ADAM_B2) * (g * g)
        m_hat = m_ / (1.0 - ADAM_B1 ** ADAM_STEP)
        v_hat = v_ / (1.0 - ADAM_B2 ** ADAM_STEP)
        g_ref[...] = g
        d_ref[...] = -ADAM_LR * (m_hat / (jnp.sqrt(v_hat) + ADAM_EPS) + ADAM_WD * w_ref[...])
        nm_ref[...] = m_
        nv_ref[...] = v_

    rs = pl.BlockSpec((tr, c), lambda i: (i, 0))
    return pl.pallas_call(
        body, name=name, grid=(r // tr,), in_specs=[rs] * 5, out_specs=[rs] * 4,
        out_shape=[jax.ShapeDtypeStruct((r, c), F32)] * 4,
        compiler_params=_params(("parallel",)),
    )(part, sib, w, m, v)


WEIGHTS = ['meta_tokens', 's5_lambda_re', 's5_lambda_im', 's5_log_dt', 's5_b_re', 's5_b_im', 's5_c_re', 's5_c_im', 's5_d',
           's5_w_glu', 's5_w_out', 'attn_w_qkv', 'attn_q_gain', 'attn_k_gain', 'attn_w_out', 'ffn_w_gate', 'ffn_w_up',
           'ffn_w_down', 'ln_gain', 'ln_bias']
BIG = ['s5_w_glu', 's5_w_out', 'attn_w_qkv', 'attn_w_out', 'ffn_w_gate', 'ffn_w_up', 'ffn_w_down']
ROW_SHARDED = {'s5_w_glu', 's5_w_out', 'attn_w_out', 'ffn_w_down'}
SMALL_SHARDED = ['meta_tokens', 'ln_gain', 'ln_bias']
REPLICATED = ['s5_lambda_re', 's5_lambda_im', 's5_log_dt', 's5_b_re', 's5_b_im', 's5_c_re', 's5_c_im', 's5_d',
              'attn_q_gain', 'attn_k_gain']
REP_ALIGN = N_CHIPS * LANES * LANES


def _natural(gathered, row_sharded):
    p, n, a, b = gathered.shape
    if row_sharded:
        return jnp.transpose(gathered, (1, 0, 2, 3)).reshape(n, p * a, b)
    return jnp.transpose(gathered, (1, 2, 0, 3)).reshape(n, a, p * b)


def _shard_major(full, row_sharded):
    n, a, b = full.shape
    if row_sharded:
        return jnp.transpose(full.reshape(n, N_CHIPS, a // N_CHIPS, b), (1, 0, 2, 3))
    return jnp.transpose(full.reshape(n, a, N_CHIPS, b // N_CHIPS), (2, 0, 1, 3))


def _dup_heads(w):
    lead = w.shape[:-1]
    w = w.reshape(lead + (N_KV_HEADS, 1, HEAD_DIM))
    return jnp.broadcast_to(w, lead + (N_KV_HEADS, 2, HEAD_DIM)).reshape(lead + (N_KV_HEADS * 2 * HEAD_DIM,))


def _fold_heads(d):
    lead = d.shape[:-1]
    return d.reshape(lead + (N_KV_HEADS, 2, HEAD_DIM)).sum(axis=-2).reshape(lead + (N_KV_HEADS * HEAD_DIM,))


def _pack_rep(tree):
    flat = jnp.concatenate([tree[n].reshape(-1) for n in REPLICATED])
    pad = _round_up(flat.shape[0], REP_ALIGN) - flat.shape[0]
    return jnp.pad(flat, (0, pad))


def _unpack_rep(flat, like):
    out, off = {}, 0
    for n in REPLICATED:
        size = math.prod(like[n].shape)
        out[n] = flat[off:off + size].reshape(like[n].shape)
        off += size
    return out


def _train_step(x, loss_target, w, mom, vel):
    s = x.shape[1]
    n_valid = N_META + s
    lp = _round_up(n_valid, LANES)
    nq = N_Q_HEADS * HEAD_DIM
    nkv = N_KV_HEADS * HEAD_DIM

    small = jnp.concatenate([w[n].reshape(-1, w[n].shape[-1]) for n in SMALL_SHARDED], axis=0)
    gathered = chip_exchange([w[n].astype(BF16) for n in BIG] + [small], False, "gather_weights")
    full = {n: _natural(g, n in ROW_SHARDED) for n, g in zip(BIG, gathered[:-1])}
    small_full = jnp.transpose(gathered[-1], (1, 0, 2)).reshape(small.shape[0], D_MODEL)
    meta_full = small_full[:N_META]
    ln_gain = small_full[N_META:N_META + 2 * DEPTH].reshape(DEPTH, 2, 1, D_MODEL)
    ln_bias = small_full[N_META + 2 * DEPTH:].reshape(DEPTH, 2, 1, D_MODEL)
    wqkv = full['attn_w_qkv']
    w2 = jnp.concatenate([wqkv[..., :nq], _dup_heads(wqkv[..., nq:nq + nkv]), _dup_heads(wqkv[..., nq + nkv:])], axis=-1)

    cos, sin = rope_tables(lp, n_valid)
    e128 = head_sum_matrix()
    e16 = jnp.kron(jnp.eye(N_Q_HEADS, dtype=F32), jnp.ones((HEAD_DIM, 1), F32)).astype(BF16)
    gq = jnp.tile(w['attn_q_gain'], (1, 2))[:, None, :]
    gk = jnp.tile(w['attn_k_gain'], (1, 2))[:, None, :]

    pad_rows = jnp.zeros((lp - n_valid, D_MODEL), F32)
    h = jnp.concatenate([meta_full, x[0], pad_rows], axis=0)
    tgt = jnp.concatenate([jnp.zeros((N_META, D_MODEL), F32), loss_target[0], pad_rows], axis=0)

    saved = []
    s5_names = ['s5_lambda_re', 's5_lambda_im', 's5_log_dt', 's5_b_re', 's5_b_im', 's5_c_re', 's5_c_im']
    for i in range(DEPTH):
        j = i // 2
        sv = {'h': h}
        if i % 2 == 0:
            ops, sv['prep_vjp'] = jax.vjp(s5_prep, *[w[n][j] for n in s5_names])
            m_, wx_, ci_, at_ = ops
            sv['ops'] = (m_.astype(BF16), wx_.astype(BF16), ci_.astype(BF16), at_)
            y, sv['lhs'], sv['sp'], sv['sn'] = s5_forward(h, *sv['ops'], n_valid)
            sv['v'], sv['t'], sv['g'], sv['z'] = glu_forward(y, h, w['s5_d'][j][None], full['s5_w_glu'][j])
            sv['r1'], h1 = proj_ln_forward(sv['z'], full['s5_w_out'][j], h, ln_gain[i, 0], ln_bias[i, 0], "s5_out_ln")
        else:
            sv['raw'], sv['qs'], sv['k2'], sv['v2'] = qkv_forward(h, w2[j], gq[j], gk[j], cos, sin, e128)
            sv['o'], sv['lse'] = attn_forward(sv['qs'], sv['k2'], sv['v2'], n_valid)
            sv['r1'], h1 = proj_ln_forward(sv['o'], full['attn_w_out'][j], h, ln_gain[i, 0], ln_bias[i, 0], "attn_out_ln")
        sv['h1'] = h1
        sv['a'], sv['b'], sv['f'] = ffn_up_forward(h1, full['ffn_w_gate'][i], full['ffn_w_up'][i])
        sv['r2'], h = proj_ln_forward(sv['f'], full['ffn_w_down'][i], h1, ln_gain[i, 1], ln_bias[i, 1], "ffn_down_ln")
        saved.append(sv)

    dh, loss_part = loss_backward(h, tgt, n_valid)
    loss = lax.psum(jnp.sum(loss_part), ("x", "y", "c"))

    gfull = {n: [None] * w[n].shape[0] for n in BIG}
    d_ln_gain = [[None, None] for _ in range(DEPTH)]
    d_ln_bias = [[None, None] for _ in range(DEPTH)]
    grep = {n: [None] * w[n].shape[0] for n in REPLICATED}
    for i in reversed(range(DEPTH)):
        j = i // 2
        sv = saved[i]
        dr2, s2 = ln_backward(dh, sv['r2'], ln_gain[i, 1])
        d_ln_gain[i][1], d_ln_bias[i][1] = s2[0], s2[1]
        da, db = ffn_backward_act(dr2, full['ffn_w_down'][i], sv['a'], sv['b'])
        gfull['ffn_w_down'][i] = mm_tn(sv['f'], dr2, "grad_ffn_down")
        dh1 = resid_nt(dr2, [da, db], [full['ffn_w_gate'][i], full['ffn_w_up'][i]], "ffn_backward_x")
        gfull['ffn_w_gate'][i] = mm_tn(sv['h1'], da, "grad_ffn_gate")
        gfull['ffn_w_up'][i] = mm_tn(sv['h1'], db, "grad_ffn_up")
        dr1, s1 = ln_backward(dh1, sv['r1'], ln_gain[i, 0])
        d_ln_gain[i][0], d_ln_bias[i][0] = s1[0], s1[1]
        if i % 2 == 0:
            dt, dgd = glu_backward1(dr1, full['s5_w_out'][j], sv['g'], sv['t'])
            gfull['s5_w_out'][j] = mm_tn(sv['z'], dr1, "grad_s5_out")
            dv, dhs, sd = glu_backward2(dt, dgd, full['s5_w_glu'][j], sv['v'], sv['h'], w['s5_d'][j][None], dr1)
            grep['s5_d'][j] = sd[0]
            gfull['s5_w_glu'][j] = mm_tn(sv['g'], dt, "grad_s5_glu")
            dh, ldy, dxf, dxr, daf, dar = s5_backward(dv, dhs, *sv['ops'], sv['sp'], sv['sn'], n_valid)
            dm = bmm_tn(sv['lhs'], ldy, "grad_s5_m")
            dwx = jnp.stack([bmm_tn(sv['lhs'], dxf, "grad_s5_wxf"), bmm_tn(sv['lhs'], dxr, "grad_s5_wxr")])
            dci = jnp.stack([bmm_tn(sv['sp'], ldy, "grad_s5_cif"), bmm_tn(sv['sn'], ldy, "grad_s5_cir")])
            dps = sv['prep_vjp']((dm, dwx, dci, jnp.stack([daf, dar])))
            for n, g in zip(s5_names, dps):
                grep[n][j] = g
        else:
            do, delta = attn_out_backward(dr1, full['attn_w_out'][j], sv['o'], e16)
            gfull['attn_w_out'][j] = mm_tn(sv['o'], dr1, "grad_attn_out")
            dl = delta.T.reshape(N_KV_HEADS, Q_PER_KV, lp)
            dq = attn_backward_q(sv['qs'], sv['k2'], sv['v2'], do, sv['lse'], dl[..., None], n_valid)
            dk2, dv2 = attn_backward_kv(sv['qs'], sv['k2'], sv['v2'], do,
                                        sv['lse'].reshape(N_KV_HEADS, Q_PER_KV, 1, lp), dl[:, :, None, :], n_valid)
            draw, gs = qkv_backward(dq, dk2, dv2, sv['raw'], gq[j], gk[j], cos, sin, e128)
            grep['attn_q_gain'][j] = gs[0, :HEAD_DIM] + gs[0, HEAD_DIM:]
            grep['attn_k_gain'][j] = gs[1, :HEAD_DIM] + gs[1, HEAD_DIM:]
            dh = resid_nt(dr1, [draw], [w2[j]], "attn_backward_x")
            dw2 = mm_tn(sv['h'], draw, "grad_attn_qkv")
            kq = N_QB * LANES
            kk = N_KB * LANES
            gfull['attn_w_qkv'][j] = jnp.concatenate(
                [dw2[:, :kq], _fold_heads(dw2[:, kq:kq + kk]), _fold_heads(dw2[:, kq + kk:])], axis=1)
    grad_x = dh[N_META:n_valid][None]

    contrib = [_shard_major(jnp.stack(gfull[n]), n in ROW_SHARDED).astype(BF16) for n in BIG]
    small_g = jnp.concatenate([dh[:N_META], jnp.stack([g for pair in d_ln_gain for g in pair]),
                               jnp.stack([g for pair in d_ln_bias for g in pair])], axis=0)
    contrib.append(jnp.transpose(small_g.reshape(-1, N_CHIPS, D_MODEL // N_CHIPS), (1, 0, 2)))
    rep_g = _pack_rep({n: jnp.stack(grep[n]) for n in REPLICATED})
    contrib.append(rep_g.reshape(N_CHIPS, -1, LANES))
    recv = chip_exchange(contrib, True, "scatter_grads")
    names = BIG + ['small', 'rep']
    parts = [chip_sum(r.reshape(N_CHIPS, -1, r.shape[-1]), "chip_sum_" + n) for n, r in zip(names, recv)]
    sibs = sibling_exchange(parts, "sibling_grads")

    out = {}

    def update(n, part, sib):
        shape = w[n].shape
        two_d = (-1, shape[-1])
        g, d, nm, nv = adamw(part, sib, w[n].reshape(two_d), mom[n].reshape(two_d), vel[n].reshape(two_d), "adamw_" + n)
        out[n] = tuple(t.reshape(shape) for t in (g, d, nm, nv))

    for n, part, sib in zip(BIG, parts, sibs):
        update(n, part, sib)
    sm_w = jnp.concatenate([w[n].reshape(-1, w[n].shape[-1]) for n in SMALL_SHARDED], axis=0)
    sm_m = jnp.concatenate([mom[n].reshape(-1, mom[n].shape[-1]) for n in SMALL_SHARDED], axis=0)
    sm_v = jnp.concatenate([vel[n].reshape(-1, vel[n].shape[-1]) for n in SMALL_SHARDED], axis=0)
    sm = adamw(parts[-2], sibs[-2], sm_w, sm_m, sm_v, "adamw_small")
    off = 0
    for n in SMALL_SHARDED:
        rows = math.prod(w[n].shape[:-1])
        out[n] = tuple(t[off:off + rows].reshape(w[n].shape) for t in sm)
        off += rows
    rep_quarter = pair_sum(parts[-1], sibs[-1], "rep_sum")
    rep_all = chip_exchange([rep_quarter], False, "gather_rep")[0].reshape(-1, LANES)
    rp = adamw(rep_all, jnp.zeros_like(rep_all), _pack_rep(w).reshape(-1, LANES), _pack_rep(mom).reshape(-1, LANES),
               _pack_rep(vel).reshape(-1, LANES), "adamw_rep")
    unpacked = [_unpack_rep(t.reshape(-1), w) for t in rp]
    for n in REPLICATED:
        out[n] = tuple(u[n] for u in unpacked)

    return (loss, grad_x, *[out[n][0] for n in WEIGHTS], *[out[n][1] for n in WEIGHTS],
            *[out[n][2] for n in WEIGHTS], *[out[n][3] for n in WEIGHTS])


def kernel(x, meta_tokens, s5_lambda_re, s5_lambda_im, s5_log_dt, s5_b_re, s5_b_im, s5_c_re, s5_c_im, s5_d, s5_w_glu, s5_w_out, attn_w_qkv, attn_q_gain, attn_k_gain, attn_w_out, ffn_w_gate, ffn_w_up, ffn_w_down, ln_gain, ln_bias, loss_target, m_meta_tokens, m_s5_lambda_re, m_s5_lambda_im, m_s5_log_dt, m_s5_b_re, m_s5_b_im, m_s5_c_re, m_s5_c_im, m_s5_d, m_s5_w_glu, m_s5_w_out, m_attn_w_qkv, m_attn_q_gain, m_attn_k_gain, m_attn_w_out, m_ffn_w_gate, m_ffn_w_up, m_ffn_w_down, m_ln_gain, m_ln_bias, v_meta_tokens, v_s5_lambda_re, v_s5_lambda_im, v_s5_log_dt, v_s5_b_re, v_s5_b_im, v_s5_c_re, v_s5_c_im, v_s5_d, v_s5_w_glu, v_s5_w_out, v_attn_w_qkv, v_attn_q_gain, v_attn_k_gain, v_attn_w_out, v_ffn_w_gate, v_ffn_w_up, v_ffn_w_down, v_ln_gain, v_ln_bias):
    given = locals()
    w = {n: given[n] for n in WEIGHTS}
    mom = {n: given["m_" + n] for n in WEIGHTS}
    vel = {n: given["v_" + n] for n in WEIGHTS}
    return _train_step(x, loss_target, w, mom, vel)
```

```python
import math

import jax
import jax.numpy as jnp
from jax import lax
from jax.experimental import pallas as pl
from jax.experimental.pallas import tpu as pltpu

F32 = jnp.float32
BF16 = jnp.bfloat16
MESH = pl.DeviceIdType.MESH

D_MODEL = 1024
N_META = 16
GRID_W = 64
HEAD_DIM = 64
N_Q_HEADS = 16
N_KV_HEADS = 4
ROPE_THETA = 10000.0
QK_EPS = 1e-6
S5_CH = 16
S5_GROUPS = 64
S5_STATE = 64
D_FF = 2816
LN_EPS = 1e-5
DEPTH = 4
ALPHA = (2.0 * DEPTH) ** 0.25
ADAM_LR, ADAM_B1, ADAM_B2, ADAM_EPS, ADAM_WD, ADAM_STEP = 0.001, 0.9, 0.999, 1e-08, 0.01, 10

LANES = 128
SUBLANES = 8
VMEM_LIMIT = 56 * 1024 * 1024

S5_T = 8
S5_GB = LANES // S5_CH
S5_NJ = S5_GROUPS // S5_GB
S5_W = S5_T * LANES
S5_SW = 2 * S5_GB * S5_STATE
S5_HALF = S5_SW // 2


def _round_up(a, b):
    return -(-a // b) * b


def _pick_tile(n, prefs):
    for t in prefs:
        if n % t == 0:
            return t
    return n


def _params(sem=None):
    kw = dict(vmem_limit_bytes=VMEM_LIMIT)
    if sem is not None:
        kw["dimension_semantics"] = sem
    return pltpu.CompilerParams(**kw)


def _dot(a, b, dims):
    return lax.dot_general(a, b, (dims, ((), ())), preferred_element_type=F32)


def _nn(a, b):
    return _dot(a, b, ((1,), (0,)))


def _nt(a, b):
    return _dot(a, b, ((1,), (1,)))


def _tn(a, b):
    return _dot(a, b, ((0,), (0,)))


def _blockdiag(w):
    g, a0, a1, b0, b1 = w.shape
    w = jnp.transpose(w.reshape(S5_NJ, S5_GB, a0, a1, b0, b1), (0, 2, 1, 3, 4, 5))[:, :, :, :, :, None, :]
    gi = jnp.arange(S5_GB)
    same = gi[:, None, None, None, None] == gi[None, None, None, :, None]
    out = jnp.where(same, w, jnp.zeros((), w.dtype))
    return out.reshape(S5_NJ, a0 * S5_GB * a1, b0 * S5_GB * b1)


def s5_prep(lam_re, lam_im, log_dt, b_re, b_im, c_re, c_im):
    hi = lax.Precision.HIGHEST
    t = S5_T
    dt = jnp.exp(log_dt)[..., None]
    taus = jnp.arange(t + 1, dtype=F32)[:, None, None, None]
    mag = jnp.exp(lam_re * dt)
    ang = lam_im * dt
    pr = jnp.concatenate([jnp.ones_like(mag)[None], (mag * jnp.cos(ang))[None],
                          jnp.exp(lam_re * dt * taus[2:]) * jnp.cos(ang * taus[2:])], axis=0)
    pi = jnp.concatenate([jnp.zeros_like(mag)[None], (mag * jnp.sin(ang))[None],
                          jnp.exp(lam_re * dt * taus[2:]) * jnp.sin(ang * taus[2:])], axis=0)
    abr, abi = pr[1], pi[1]
    nr, ni = abr - 1.0, abi
    den = lam_re * lam_re + lam_im * lam_im
    cr = (nr * lam_re + ni * lam_im) / den
    ci_ = (ni * lam_re - nr * lam_im) / den
    bbr = cr[..., None] * b_re - ci_[..., None] * b_im
    bbi = cr[..., None] * b_im + ci_[..., None] * b_re
    er = c_re[None] * pr[:, :, :, None, :] - c_im[None] * pi[:, :, :, None, :]
    ei = c_re[None] * pi[:, :, :, None, :] + c_im[None] * pr[:, :, :, None, :]
    kk = (jnp.einsum("tdgop,dgpc->tdgoc", er[:t], bbr, precision=hi)
          - jnp.einsum("tdgop,dgpc->tdgoc", ei[:t], bbi, precision=hi))
    zero = jnp.zeros_like(kk[0, 0])
    mg = jnp.stack([jnp.stack([(kk[i - s, 0] if i > s else zero) + (kk[s - i, 1] if s > i else zero)
                               + ((kk[0, 0] + kk[0, 1]) if i == s else zero) for i in range(t)])
                    for s in range(t)])
    mg = jnp.transpose(mg, (2, 0, 4, 1, 3))
    m = _blockdiag(mg)
    pw_f = jnp.stack([pr[t - 1 - s, 0] for s in range(t)]), jnp.stack([pi[t - 1 - s, 0] for s in range(t)])
    pw_r = jnp.stack([pr[s, 1] for s in range(t)]), jnp.stack([pi[s, 1] for s in range(t)])
    wx = []
    for d, (qr, qi) in enumerate((pw_f, pw_r)):
        wr = qr[..., None] * bbr[d][None] - qi[..., None] * bbi[d][None]
        wi = qr[..., None] * bbi[d][None] + qi[..., None] * bbr[d][None]
        w = jnp.stack([wr, wi], axis=0)
        w = jnp.transpose(w, (2, 1, 4, 0, 3))
        wx.append(_blockdiag(w))
    ci = []
    for d in range(2):
        exps = [i + 1 for i in range(t)] if d == 0 else [t - i for i in range(t)]
        e_r = jnp.stack([er[e, d] for e in exps])
        e_i = jnp.stack([ei[e, d] for e in exps])
        w = jnp.stack([e_r, -e_i], axis=0)
        w = jnp.transpose(w, (2, 0, 4, 1, 3))
        ci.append(_blockdiag(w))
    at = jnp.stack([pr[t], pi[t]], axis=1)
    at = at.reshape(2, 2, S5_NJ, S5_GB * S5_STATE)
    at = jnp.transpose(at, (0, 2, 1, 3)).reshape(2, S5_NJ, 1, S5_SW)
    return m, jnp.stack(wx), jnp.stack(ci), at


def _chunk_rows(ref, nc):
    return jnp.concatenate([ref[pl.ds(s, nc, stride=S5_T), :] for s in range(S5_T)], axis=1)


def _cmul(ar, ai, sr, si):
    return ar * sr - ai * si, ar * si + ai * sr


def _scan_tiles(nc, reverse, step):
    nt = nc // SUBLANES

    def body(it, carry):
        tix = (nt - 1 - it) if reverse else it
        k0 = pl.multiple_of(tix * SUBLANES, SUBLANES)
        return step(k0, carry)

    return body, nt


def _s5_specs(nc):
    hbm = pl.BlockSpec(memory_space=pl.ANY)
    aspec = pl.BlockSpec((1, 1, S5_SW), lambda j: (j, 0, 0))
    cspec = pl.BlockSpec((1, nc, S5_W), lambda j: (j, 0, 0))
    return hbm, aspec, cspec


def _s5_fetch(j, tok_hbm, w_hbms, tok_s, w_s, sems):
    cols = pl.ds(pl.multiple_of(j * LANES, LANES), LANES)
    cps = [pltpu.make_async_copy(tok_hbm.at[:, cols], tok_s, sems.at[0])]
    for i, w in enumerate(w_hbms):
        cps.append(pltpu.make_async_copy(w.at[j], w_s.at[i], sems.at[1 + i]))
    for cp in cps:
        cp.start()
    return cols, cps


def s5_forward(u, m, wx, ci, at, n_valid):
    lp = u.shape[0]
    nc = lp // S5_T
    nvc = n_valid // S5_T

    def body(u_hbm, m_hbm, wxf_hbm, wxr_hbm, cif_hbm, cir_hbm, atf_ref, atr_ref,
             y_hbm, lhs_ref, sp_ref, sn_ref, tok_s, w_s, xf_s, xr_s, sems):
        j = pl.program_id(0)
        cols, cps = _s5_fetch(j, u_hbm, (m_hbm, wxf_hbm, wxr_hbm, cif_hbm, cir_hbm), tok_s, w_s, sems)
        cps[0].wait()
        lhs = _chunk_rows(tok_s, nc)
        rows = lax.broadcasted_iota(jnp.int32, lhs.shape, 0)
        lhs = jnp.where(rows < nvc, lhs, 0.0).astype(BF16)
        lhs_ref[0] = lhs
        cps[2].wait()
        cps[3].wait()
        xf_s[...] = _nn(lhs, w_s[1])
        xr_s[...] = _nn(lhs, w_s[2])
        afr, afi = atf_ref[0, :, :S5_HALF], atf_ref[0, :, S5_HALF:]
        arr, ari = atr_ref[0, :, :S5_HALF], atr_ref[0, :, S5_HALF:]

        def scan_step(x_s, ar, ai, descending):
            def step(k0, carry):
                sr, si = carry
                x = x_s[pl.ds(k0, SUBLANES), :]
                outs = [None] * SUBLANES
                order = reversed(range(SUBLANES)) if descending else range(SUBLANES)
                for r in order:
                    outs[r] = jnp.concatenate([sr, si], axis=1)
                    nr, ni = _cmul(ar, ai, sr, si)
                    sr = nr + x[r:r + 1, :S5_HALF]
                    si = ni + x[r:r + 1, S5_HALF:]
                x_s[pl.ds(k0, SUBLANES), :] = jnp.concatenate(outs, axis=0)
                return sr, si
            return step

        zero = jnp.zeros((1, S5_HALF), F32)
        fb, nt = _scan_tiles(nc, False, scan_step(xf_s, afr, afi, False))
        lax.fori_loop(0, nt, fb, (zero, zero))
        rb, nt = _scan_tiles(nc, True, scan_step(xr_s, arr, ari, True))
        lax.fori_loop(0, nt, rb, (zero, zero))
        sp = xf_s[...].astype(BF16)
        sn = xr_s[...].astype(BF16)
        sp_ref[0] = sp
        sn_ref[0] = sn
        cps[1].wait()
        cps[4].wait()
        cps[5].wait()
        y = _nn(lhs, w_s[0]) + _nn(sp, w_s[3]) + _nn(sn, w_s[4])
        for i in range(S5_T):
            tok_s[pl.ds(i, nc, stride=S5_T), :] = y[:, i * LANES:(i + 1) * LANES]
        pltpu.sync_copy(tok_s, y_hbm.at[:, cols])

    hbm, aspec, cspec = _s5_specs(nc)
    return pl.pallas_call(
        body, name="s5_forward", grid=(S5_NJ,),
        in_specs=[hbm] * 6 + [aspec, aspec],
        out_specs=[hbm, cspec, cspec, cspec],
        out_shape=[jax.ShapeDtypeStruct((lp, D_MODEL), F32)] + [jax.ShapeDtypeStruct((S5_NJ, nc, S5_W), BF16)] * 3,
        scratch_shapes=[pltpu.VMEM((lp, LANES), F32), pltpu.VMEM((5, S5_W, S5_W), BF16),
                        pltpu.VMEM((nc, S5_SW), F32), pltpu.VMEM((nc, S5_SW), F32), pltpu.SemaphoreType.DMA((6,))],
        compiler_params=_params(("arbitrary",)),
    )(u, m, wx[0], wx[1], ci[0], ci[1], at[0], at[1])


def s5_backward(dy, dhs, m, wx, ci, at, sp, sn, n_valid):
    lp = dy.shape[0]
    nc = lp // S5_T
    nvc = n_valid // S5_T

    def body(dy_hbm, dhs_hbm, m_hbm, wxf_hbm, wxr_hbm, cif_hbm, cir_hbm, atf_ref, atr_ref, sp_ref, sn_ref,
             dh_hbm, ldy_ref, dxf_ref, dxr_ref, daf_ref, dar_ref, tok_s, w_s, gf_s, gr_s, sems):
        j = pl.program_id(0)
        cols, cps = _s5_fetch(j, dy_hbm, (m_hbm, wxf_hbm, wxr_hbm, cif_hbm, cir_hbm), tok_s, w_s, sems)
        cps[0].wait()
        ldy = _chunk_rows(tok_s, nc)
        rows = lax.broadcasted_iota(jnp.int32, ldy.shape, 0)
        ldy = jnp.where(rows < nvc, ldy, 0.0).astype(BF16)
        ldy_ref[0] = ldy
        resid = pltpu.make_async_copy(dhs_hbm.at[:, cols], tok_s, sems.at[0])
        resid.start()
        cps[4].wait()
        cps[5].wait()
        gf_s[...] = _nt(ldy, w_s[3])
        gr_s[...] = _nt(ldy, w_s[4])
        afr, afi = atf_ref[0, :, :S5_HALF], atf_ref[0, :, S5_HALF:]
        arr, ari = atr_ref[0, :, :S5_HALF], atr_ref[0, :, S5_HALF:]

        def adj_step(g_s, s_ref, ar, ai, descending):
            def step(k0, carry):
                gr_, gi_, dr_, di_ = carry
                g = g_s[pl.ds(k0, SUBLANES), :]
                p = s_ref[0, pl.ds(k0, SUBLANES), :].astype(F32)
                outs = [None] * SUBLANES
                order = reversed(range(SUBLANES)) if descending else range(SUBLANES)
                for r in order:
                    outs[r] = jnp.concatenate([gr_, gi_], axis=1)
                    pr_, pi_ = p[r:r + 1, :S5_HALF], p[r:r + 1, S5_HALF:]
                    dr_ = dr_ + gr_ * pr_ + gi_ * pi_
                    di_ = di_ + gi_ * pr_ - gr_ * pi_
                    nr, ni = _cmul(ar, -ai, gr_, gi_)
                    gr_ = nr + g[r:r + 1, :S5_HALF]
                    gi_ = ni + g[r:r + 1, S5_HALF:]
                g_s[pl.ds(k0, SUBLANES), :] = jnp.concatenate(outs, axis=0)
                return gr_, gi_, dr_, di_
            return step

        zero = jnp.zeros((1, S5_HALF), F32)
        fb, nt = _scan_tiles(nc, True, adj_step(gf_s, sp_ref, afr, afi, True))
        _, _, dr_, di_ = lax.fori_loop(0, nt, fb, (zero,) * 4)
        daf_ref[0] = jnp.concatenate([dr_, di_], axis=1)
        rb, nt = _scan_tiles(nc, False, adj_step(gr_s, sn_ref, arr, ari, False))
        _, _, dr_, di_ = lax.fori_loop(0, nt, rb, (zero,) * 4)
        dar_ref[0] = jnp.concatenate([dr_, di_], axis=1)
        dxf = gf_s[...].astype(BF16)
        dxr = gr_s[...].astype(BF16)
        dxf_ref[0] = dxf
        dxr_ref[0] = dxr
        cps[1].wait()
        cps[2].wait()
        cps[3].wait()
        du = _nt(ldy, w_s[0]) + _nt(dxf, w_s[1]) + _nt(dxr, w_s[2])
        rows = lax.broadcasted_iota(jnp.int32, du.shape, 0)
        du = jnp.where(rows < nvc, du, 0.0)
        resid.wait()
        for s in range(S5_T):
            tok_s[pl.ds(s, nc, stride=S5_T), :] += du[:, s * LANES:(s + 1) * LANES]
        pltpu.sync_copy(tok_s, dh_hbm.at[:, cols])

    hbm, aspec, cspec = _s5_specs(nc)
    return pl.pallas_call(
        body, name="s5_backward", grid=(S5_NJ,),
        in_specs=[hbm] * 7 + [aspec, aspec, cspec, cspec],
        out_specs=[hbm, cspec, cspec, cspec, aspec, aspec],
        out_shape=[jax.ShapeDtypeStruct((lp, D_MODEL), F32)] + [jax.ShapeDtypeStruct((S5_NJ, nc, S5_W), BF16)] * 3
        + [jax.ShapeDtypeStruct((S5_NJ, 1, S5_SW), F32)] * 2,
        scratch_shapes=[pltpu.VMEM((lp, LANES), F32), pltpu.VMEM((5, S5_W, S5_W), BF16),
                        pltpu.VMEM((nc, S5_SW), F32), pltpu.VMEM((nc, S5_SW), F32), pltpu.SemaphoreType.DMA((6,))],
        compiler_params=_params(("arbitrary",)),
    )(dy, dhs, m, wx[0], wx[1], ci[0], ci[1], at[0], at[1], sp, sn)


def bmm_tn(a, b, name):
    nj, k, wa = a.shape
    wb = b.shape[2]

    def body(a_ref, b_ref, o_ref):
        o_ref[0] = _tn(a_ref[0], b_ref[0])

    return pl.pallas_call(
        body, name=name, grid=(nj,),
        in_specs=[pl.BlockSpec((1, k, wa), lambda j: (j, 0, 0)), pl.BlockSpec((1, k, wb), lambda j: (j, 0, 0))],
        out_specs=pl.BlockSpec((1, wa, wb), lambda j: (j, 0, 0)),
        out_shape=jax.ShapeDtypeStruct((nj, wa, wb), F32),
        compiler_params=_params(("arbitrary",)),
    )(a, b)


def _tm(lp):
    return _pick_tile(lp, (768, 256))


def _row_spec(tm, width):
    return pl.BlockSpec((tm, width), lambda i: (i, 0))


def _full_spec(shape):
    return pl.BlockSpec(shape, lambda *_: (0,) * len(shape))


def _gelu(v):
    return 0.5 * v * (1.0 + lax.erf(v * (2.0 ** -0.5)))


def _gelu_grad(v):
    return 0.5 * (1.0 + lax.erf(v * (2.0 ** -0.5))) + v * jnp.exp(-0.5 * v * v) * (2.0 * math.pi) ** -0.5


def _layer_norm(r, gain, bias):
    mean = jnp.mean(r, axis=-1, keepdims=True)
    c = r - mean
    var = jnp.mean(c * c, axis=-1, keepdims=True)
    return c * lax.rsqrt(var + LN_EPS) * gain + bias


def glu_forward(y, h, dvec, wglu):
    lp, d = y.shape
    tm = _tm(lp)

    def body(y_ref, h_ref, d_ref, w_ref, v_ref, t_ref, g_ref, z_ref):
        v = y_ref[...] + d_ref[...] * h_ref[...]
        g = _gelu(v)
        gb = g.astype(BF16)
        t = _nn(gb, w_ref[...])
        v_ref[...] = v
        t_ref[...] = t
        g_ref[...] = gb
        z_ref[...] = (g * jax.nn.sigmoid(t)).astype(BF16)

    rs = _row_spec(tm, d)
    return pl.pallas_call(
        body, name="glu_forward", grid=(lp // tm,),
        in_specs=[rs, rs, _full_spec((1, d)), _full_spec((d, d))],
        out_specs=[rs, rs, rs, rs],
        out_shape=[jax.ShapeDtypeStruct((lp, d), F32)] * 2 + [jax.ShapeDtypeStruct((lp, d), BF16)] * 2,
        compiler_params=_params(("parallel",)),
    )(y, h, dvec, wglu)


def proj_ln_forward(z, w, h, gain, bias, name):
    lp, k = z.shape
    d = w.shape[1]
    tm = _tm(lp)

    def body(z_ref, w_ref, h_ref, g_ref, b_ref, r_ref, o_ref):
        r = ALPHA * h_ref[...] + _nn(z_ref[...], w_ref[...])
        r_ref[...] = r
        o_ref[...] = _layer_norm(r, g_ref[...], b_ref[...])

    rs = _row_spec(tm, d)
    return pl.pallas_call(
        body, name=name, grid=(lp // tm,),
        in_specs=[_row_spec(tm, k), _full_spec((k, d)), rs, _full_spec((1, d)), _full_spec((1, d))],
        out_specs=[rs, rs],
        out_shape=[jax.ShapeDtypeStruct((lp, d), F32)] * 2,
        compiler_params=_params(("parallel",)),
    )(z, w, h, gain, bias)


FFN_NB = 1408


def ffn_up_forward(h, wg, wu):
    lp, d = h.shape
    dff = wg.shape[1]
    tm = _tm(lp)

    def body(h_ref, wg_ref, wu_ref, a_ref, b_ref, f_ref):
        hb = h_ref[...].astype(BF16)
        a = _nn(hb, wg_ref[...])
        b = _nn(hb, wu_ref[...])
        a_ref[...] = a.astype(BF16)
        b_ref[...] = b.astype(BF16)
        f_ref[...] = (a * jax.nn.sigmoid(a) * b).astype(BF16)

    ws = pl.BlockSpec((d, FFN_NB), lambda n, i: (0, n))
    os_ = pl.BlockSpec((tm, FFN_NB), lambda n, i: (i, n))
    return pl.pallas_call(
        body, name="ffn_up_forward", grid=(dff // FFN_NB, lp // tm),
        in_specs=[pl.BlockSpec((tm, d), lambda n, i: (i, 0)), ws, ws],
        out_specs=[os_, os_, os_],
        out_shape=[jax.ShapeDtypeStruct((lp, dff), BF16)] * 3,
        compiler_params=_params(("parallel", "parallel")),
    )(h, wg, wu)


def ln_backward(dh, r, gain):
    lp, d = dh.shape
    tm = _tm(lp)

    def body(dh_ref, r_ref, g_ref, dr_ref, s_ref):
        r_ = r_ref[...]
        dh_ = dh_ref[...]
        mean = jnp.mean(r_, axis=-1, keepdims=True)
        c = r_ - mean
        var = jnp.mean(c * c, axis=-1, keepdims=True)
        rstd = lax.rsqrt(var + LN_EPS)
        xh = c * rstd
        dxh = dh_ * g_ref[...]
        m1 = jnp.mean(dxh, axis=-1, keepdims=True)
        m2 = jnp.mean(dxh * xh, axis=-1, keepdims=True)
        dr_ref[...] = rstd * (dxh - m1 - xh * m2)

        @pl.when(pl.program_id(0) == 0)
        def _():
            s_ref[...] = jnp.zeros_like(s_ref)

        s_ref[0:1, :] += jnp.sum(dh_ * xh, axis=0, keepdims=True)
        s_ref[1:2, :] += jnp.sum(dh_, axis=0, keepdims=True)

    rs = _row_spec(tm, d)
    return pl.pallas_call(
        body, name="ln_backward", grid=(lp // tm,),
        in_specs=[rs, rs, _full_spec((1, d))],
        out_specs=[rs, _full_spec((SUBLANES, d))],
        out_shape=[jax.ShapeDtypeStruct((lp, d), F32), jax.ShapeDtypeStruct((SUBLANES, d), F32)],
        compiler_params=_params(("arbitrary",)),
    )(dh, r, gain)


def ffn_backward_act(dr, wd, a, b):
    lp, d = dr.shape
    dff = wd.shape[0]
    tm = _tm(lp)

    def body(dr_ref, wd_ref, a_ref, b_ref, da_ref, db_ref):
        df = _nt(dr_ref[...].astype(BF16), wd_ref[...])
        a_ = a_ref[...].astype(F32)
        b_ = b_ref[...].astype(F32)
        sg = jax.nn.sigmoid(a_)
        da_ref[...] = (df * b_ * sg * (1.0 + a_ * (1.0 - sg))).astype(BF16)
        db_ref[...] = (df * a_ * sg).astype(BF16)

    os_ = pl.BlockSpec((tm, FFN_NB), lambda n, i: (i, n))
    return pl.pallas_call(
        body, name="ffn_backward_act", grid=(dff // FFN_NB, lp // tm),
        in_specs=[pl.BlockSpec((tm, d), lambda n, i: (i, 0)), pl.BlockSpec((FFN_NB, d), lambda n, i: (n, 0)), os_, os_],
        out_specs=[os_, os_],
        out_shape=[jax.ShapeDtypeStruct((lp, dff), BF16)] * 2,
        compiler_params=_params(("parallel", "parallel")),
    )(dr, wd, a, b)


def resid_nt(dr, xs, ws, name):
    lp, d = dr.shape
    tm = _tm(lp)
    n = len(xs)

    def body(*refs):
        acc = ALPHA * refs[0][...]
        for i in range(n):
            acc = acc + _nt(refs[1 + i][...], refs[1 + n + i][...])
        refs[-1][...] = acc

    rs = _row_spec(tm, d)
    in_specs = [rs] + [_row_spec(tm, x.shape[1]) for x in xs] + [_full_spec(w.shape) for w in ws]
    return pl.pallas_call(
        body, name=name, grid=(lp // tm,),
        in_specs=in_specs, out_specs=rs,
        out_shape=jax.ShapeDtypeStruct((lp, d), F32),
        compiler_params=_params(("parallel",)),
    )(dr, *xs, *ws)


def mm_tn(x, y, name):
    lp, k = x.shape
    n = y.shape[1]
    tm = _tm(lp)
    nb = _pick_tile(n, (512, 1408))

    def body(x_ref, y_ref, o_ref):
        @pl.when(pl.program_id(1) == 0)
        def _():
            o_ref[...] = jnp.zeros_like(o_ref)

        o_ref[...] += _tn(x_ref[...].astype(BF16), y_ref[...].astype(BF16))

    return pl.pallas_call(
        body, name=name, grid=(n // nb, lp // tm),
        in_specs=[pl.BlockSpec((tm, k), lambda j, i: (i, 0)), pl.BlockSpec((tm, nb), lambda j, i: (i, j))],
        out_specs=pl.BlockSpec((k, nb), lambda j, i: (0, j)),
        out_shape=jax.ShapeDtypeStruct((k, n), F32),
        compiler_params=_params(("parallel", "arbitrary")),
    )(x, y)


def glu_backward1(dr, wout, g, t):
    lp, d = dr.shape
    tm = _tm(lp)

    def body(dr_ref, w_ref, g_ref, t_ref, dt_ref, dgd_ref):
        dz = _nt(dr_ref[...].astype(BF16), w_ref[...])
        s = jax.nn.sigmoid(t_ref[...])
        dgd_ref[...] = dz * s
        dt_ref[...] = (dz * g_ref[...].astype(F32) * s * (1.0 - s)).astype(BF16)

    rs = _row_spec(tm, d)
    return pl.pallas_call(
        body, name="glu_backward1", grid=(lp // tm,),
        in_specs=[rs, _full_spec((d, d)), rs, rs],
        out_specs=[rs, rs],
        out_shape=[jax.ShapeDtypeStruct((lp, d), BF16), jax.ShapeDtypeStruct((lp, d), F32)],
        compiler_params=_params(("parallel",)),
    )(dr, wout, g, t)


def glu_backward2(dt, dgd, wglu, v, h, dvec, dr):
    lp, d = dt.shape
    tm = _tm(lp)

    def body(dt_ref, dgd_ref, w_ref, v_ref, h_ref, d_ref, dr_ref, dv_ref, dhs_ref, s_ref):
        dg = dgd_ref[...] + _nt(dt_ref[...], w_ref[...])
        dv = dg * _gelu_grad(v_ref[...])
        dv_ref[...] = dv
        dhs_ref[...] = ALPHA * dr_ref[...] + dv * d_ref[...]

        @pl.when(pl.program_id(0) == 0)
        def _():
            s_ref[...] = jnp.zeros_like(s_ref)

        s_ref[0:1, :] += jnp.sum(dv * h_ref[...], axis=0, keepdims=True)

    rs = _row_spec(tm, d)
    return pl.pallas_call(
        body, name="glu_backward2", grid=(lp // tm,),
        in_specs=[rs, rs, _full_spec((d, d)), rs, rs, _full_spec((1, d)), rs],
        out_specs=[rs, rs, _full_spec((SUBLANES, d))],
        out_shape=[jax.ShapeDtypeStruct((lp, d), F32)] * 2 + [jax.ShapeDtypeStruct((SUBLANES, d), F32)],
        compiler_params=_params(("arbitrary",)),
    )(dt, dgd, wglu, v, h, dvec, dr)


def loss_backward(hf, tgt, n_valid):
    lp, d = hf.shape
    tm = _tm(lp)

    def body(h_ref, t_ref, dh_ref, s_ref):
        rows = pl.program_id(0) * tm + lax.broadcasted_iota(jnp.int32, (tm, d), 0)
        ok = (rows >= N_META) & (rows < n_valid)
        e = jnp.where(ok, h_ref[...] - t_ref[...], 0.0)
        dh_ref[...] = e * (1.0 / d)

        @pl.when(pl.program_id(0) == 0)
        def _():
            s_ref[...] = jnp.zeros_like(s_ref)

        sq = e * e
        part = sq[:, 0:LANES]
        for c in range(1, d // LANES):
            part = part + sq[:, c * LANES:(c + 1) * LANES]
        acc = part[0:SUBLANES]
        for r in range(1, tm // SUBLANES):
            acc = acc + part[r * SUBLANES:(r + 1) * SUBLANES]
        s_ref[...] += acc * (0.5 / d)

    rs = _row_spec(tm, d)
    return pl.pallas_call(
        body, name="loss_backward", grid=(lp // tm,),
        in_specs=[rs, rs], out_specs=[rs, _full_spec((SUBLANES, LANES))],
        out_shape=[jax.ShapeDtypeStruct((lp, d), F32), jax.ShapeDtypeStruct((SUBLANES, LANES), F32)],
        compiler_params=_params(("arbitrary",)),
    )(hf, tgt)


N_QB = N_Q_HEADS // 2
N_KB = N_KV_HEADS
QKV_W = (N_QB + 2 * N_KB) * LANES
Q_SCALE = HEAD_DIM ** -0.5 * math.log2(math.e)


def rope_tables(lp, n_valid):
    t = jnp.arange(lp, dtype=jnp.int32)
    real = (t >= N_META) & (t < n_valid)
    pos = jnp.where(real, t - N_META, 0)
    row = (pos // GRID_W).astype(F32)
    col = (pos % GRID_W).astype(F32)
    axis_dim = HEAD_DIM // 2
    inv = ROPE_THETA ** (-jnp.arange(0, axis_dim, 2, dtype=F32) / axis_dim)
    ar = row[:, None] * inv[None, :]
    ac = col[:, None] * inv[None, :]
    cos = jnp.concatenate([jnp.cos(ar), jnp.cos(ar), jnp.cos(ac), jnp.cos(ac)], axis=1)
    sin = jnp.concatenate([-jnp.sin(ar), jnp.sin(ar), -jnp.sin(ac), jnp.sin(ac)], axis=1)
    return jnp.tile(cos, (1, 2)), jnp.tile(sin, (1, 2))


def head_sum_matrix():
    return jnp.kron(jnp.eye(2, dtype=F32), jnp.ones((HEAD_DIM, HEAD_DIM), F32)).astype(BF16)


def _segsum(x, e):
    hi = x.astype(BF16)
    lo = (x - hi.astype(F32)).astype(BF16)
    return _nn(hi, e) + _nn(lo, e)


def _swap_halves(x):
    lane = lax.broadcasted_iota(jnp.int32, x.shape, 1)
    quarter = HEAD_DIM // 4
    return jnp.where(lane % (2 * quarter) < quarter, pltpu.roll(x, LANES - quarter, 1), pltpu.roll(x, quarter, 1))


def qkv_forward(h, w2, gq, gk, cos, sin, e):
    lp, d = h.shape
    tm = _tm(lp)
    kw, vw = N_KB * LANES, N_KB * LANES

    def body(h_ref, w_ref, gq_ref, gk_ref, cos_ref, sin_ref, e_ref, raw_ref, q_ref, k_ref, v_ref):
        raw = _nn(h_ref[...].astype(BF16), w_ref[...])
        raw_ref[...] = raw
        c, s_, em = cos_ref[...], sin_ref[...], e_ref[...]
        for cb in range(N_QB + N_KB):
            t = raw[:, cb * LANES:(cb + 1) * LANES]
            rstd = lax.rsqrt(_segsum(t * t, em) * (1.0 / HEAD_DIM) + QK_EPS)
            n = t * rstd * (gq_ref[...] if cb < N_QB else gk_ref[...])
            rot = n * c + _swap_halves(n) * s_
            if cb < N_QB:
                q_ref[:, cb * LANES:(cb + 1) * LANES] = (rot * Q_SCALE).astype(BF16)
            else:
                k_ref[:, (cb - N_QB) * LANES:(cb - N_QB + 1) * LANES] = rot.astype(BF16)
        v_ref[...] = raw[:, (N_QB + N_KB) * LANES:].astype(BF16)

    return pl.pallas_call(
        body, name="qkv_forward", grid=(lp // tm,),
        in_specs=[_row_spec(tm, d), _full_spec((d, QKV_W)), _full_spec((1, LANES)), _full_spec((1, LANES)),
                  _row_spec(tm, LANES), _row_spec(tm, LANES), _full_spec((LANES, LANES))],
        out_specs=[_row_spec(tm, QKV_W), _row_spec(tm, N_QB * LANES), _row_spec(tm, kw), _row_spec(tm, vw)],
        out_shape=[jax.ShapeDtypeStruct((lp, QKV_W), F32), jax.ShapeDtypeStruct((lp, N_QB * LANES), BF16),
                   jax.ShapeDtypeStruct((lp, kw), BF16), jax.ShapeDtypeStruct((lp, vw), BF16)],
        compiler_params=_params(("parallel",)),
    )(h, w2, gq, gk, cos, sin, e)


def qkv_backward(dqs, dk2, dv2, raw, gq, gk, cos, sin, e):
    lp = raw.shape[0]
    tm = _tm(lp)

    def body(dq_ref, dk_ref, dv_ref, raw_ref, gq_ref, gk_ref, cos_ref, sin_ref, e_ref, d_ref, s_ref):
        @pl.when(pl.program_id(0) == 0)
        def _():
            s_ref[...] = jnp.zeros_like(s_ref)

        c, s_, em = cos_ref[...], sin_ref[...], e_ref[...]
        gsum = [jnp.zeros((1, LANES), F32), jnp.zeros((1, LANES), F32)]
        for cb in range(N_QB + N_KB):
            isq = cb < N_QB
            t = raw_ref[:, cb * LANES:(cb + 1) * LANES]
            if isq:
                drot = dq_ref[:, cb * LANES:(cb + 1) * LANES] * (HEAD_DIM ** -0.5)
            else:
                drot = dk_ref[:, (cb - N_QB) * LANES:(cb - N_QB + 1) * LANES] * math.log(2.0)
            gain = gq_ref[...] if isq else gk_ref[...]
            rstd = lax.rsqrt(_segsum(t * t, em) * (1.0 / HEAD_DIM) + QK_EPS)
            dn = drot * c + _swap_halves(drot * s_)
            xh = t * rstd
            gsum[0 if isq else 1] = gsum[0 if isq else 1] + jnp.sum(dn * xh, axis=0, keepdims=True)
            w = dn * gain
            mw = _segsum(w * xh, em) * (1.0 / HEAD_DIM)
            d_ref[:, cb * LANES:(cb + 1) * LANES] = (rstd * (w - xh * mw)).astype(BF16)
        d_ref[:, (N_QB + N_KB) * LANES:] = dv_ref[...].astype(BF16)
        s_ref[0:1, :] += gsum[0]
        s_ref[1:2, :] += gsum[1]

    kw = N_KB * LANES
    return pl.pallas_call(
        body, name="qkv_backward", grid=(lp // tm,),
        in_specs=[_row_spec(tm, N_QB * LANES), _row_spec(tm, kw), _row_spec(tm, kw), _row_spec(tm, QKV_W),
                  _full_spec((1, LANES)), _full_spec((1, LANES)), _row_spec(tm, LANES), _row_spec(tm, LANES),
                  _full_spec((LANES, LANES))],
        out_specs=[_row_spec(tm, QKV_W), _full_spec((SUBLANES, LANES))],
        out_shape=[jax.ShapeDtypeStruct((lp, QKV_W), BF16), jax.ShapeDtypeStruct((SUBLANES, LANES), F32)],
        compiler_params=_params(("arbitrary",)),
    )(dqs, dk2, dv2, raw, gq, gk, cos, sin, e)


NEG = -1e30
Q_PER_KV = N_Q_HEADS // N_KV_HEADS


def _half_masks(x):
    lane = lax.broadcasted_iota(jnp.int32, x.shape, 1)
    zero = jnp.zeros_like(x)
    return jnp.where(lane < HEAD_DIM, x, zero), jnp.where(lane >= HEAD_DIM, x, zero)


ATTN_TR = 16


def _attn_tiles(lp):
    t = _pick_tile(lp, (768, 256))
    return t, t


def attn_forward_t(qs, k2, v2t, n_valid):
    lp = qs.shape[0]
    tq, kb = _attn_tiles(lp)
    nk = lp // kb
    gw = 2 * LANES
    nr = kb // ATTN_TR
    pad0 = n_valid - (nk - 1) * kb

    def body(q_ref, k_ref, vt_ref, ot_ref, lse_ref, m_s, l_s, acc_s, s_s, p_s):
        j = pl.program_id(2)

        @pl.when(j == 0)
        def _():
            m_s[...] = jnp.full_like(m_s, NEG)
            l_s[...] = jnp.zeros_like(l_s)
            acc_s[...] = jnp.zeros_like(acc_s)

        ks = _half_masks(k_ref[...])
        for pair in range(2):
            qp = q_ref[:, pair * LANES:(pair + 1) * LANES]
            for half in range(2):
                hh = 2 * pair + half
                s_s[...] = _nt(ks[half], qp)

                if pad0 < kb:
                    @pl.when(j == nk - 1)
                    def _():
                        s_s[pad0:, :] = jnp.full((kb - pad0, tq), NEG, F32)

                def max_step(r, run):
                    rows = pl.ds(r * ATTN_TR, ATTN_TR)
                    blk = s_s[rows, :]
                    for t in range(ATTN_TR // SUBLANES):
                        run = jnp.maximum(run, blk[t * SUBLANES:(t + 1) * SUBLANES])
                    return run

                run = jnp.full((SUBLANES, tq), NEG, F32)
                for r in range(nr):
                    run = max_step(r, run)
                m_prev = m_s[hh:hh + 1, :]
                m_new = jnp.maximum(m_prev, jnp.max(run, axis=0, keepdims=True))
                alpha = jnp.exp2(m_prev - m_new)
                m_s[hh:hh + 1, :] = m_new

                def exp_step(r, run):
                    rows = pl.ds(r * ATTN_TR, ATTN_TR)
                    p = jnp.exp2(s_s[rows, :] - m_new)
                    p_s[rows, :] = p.astype(BF16)
                    for t in range(ATTN_TR // SUBLANES):
                        run = run + p[t * SUBLANES:(t + 1) * SUBLANES]
                    return run

                run = jnp.zeros((SUBLANES, tq), F32)
                for r in range(nr):
                    run = exp_step(r, run)
                l_s[hh:hh + 1, :] = alpha * l_s[hh:hh + 1, :] + jnp.sum(run, axis=0, keepdims=True)
                vt = vt_ref[half * HEAD_DIM:(half + 1) * HEAD_DIM, :]
                pv = _nn(vt, p_s[...])
                rs = slice(half * HEAD_DIM, (half + 1) * HEAD_DIM)
                acc_s[pair, rs, :] = alpha * acc_s[pair, rs, :] + pv

        @pl.when(j == nk - 1)
        def _():
            for pair in range(2):
                for half in range(2):
                    hh = 2 * pair + half
                    rs = slice(half * HEAD_DIM, (half + 1) * HEAD_DIM)
                    inv = 1.0 / l_s[hh:hh + 1, :]
                    ot_ref[pair * LANES + half * HEAD_DIM:pair * LANES + (half + 1) * HEAD_DIM, :] = (
                        acc_s[pair, rs, :] * inv).astype(BF16)
            for hh in range(Q_PER_KV):
                lse_ref[0, hh] = m_s[hh:hh + 1, :] + jnp.log2(l_s[hh:hh + 1, :])

    return pl.pallas_call(
        body, name="attn_forward", grid=(N_KV_HEADS, lp // tq, nk),
        in_specs=[pl.BlockSpec((tq, gw), lambda g, i, j: (i, g)), pl.BlockSpec((kb, LANES), lambda g, i, j: (j, g)),
                  pl.BlockSpec((LANES, kb), lambda g, i, j: (g, j))],
        out_specs=[pl.BlockSpec((gw, tq), lambda g, i, j: (g, i)),
                   pl.BlockSpec((1, Q_PER_KV, 1, tq), lambda g, i, j: (g, 0, 0, i))],
        out_shape=[jax.ShapeDtypeStruct((N_QB * LANES, lp), BF16),
                   jax.ShapeDtypeStruct((N_KV_HEADS, Q_PER_KV, 1, lp), F32)],
        scratch_shapes=[pltpu.VMEM((SUBLANES, tq), F32), pltpu.VMEM((SUBLANES, tq), F32),
                        pltpu.VMEM((2, LANES, tq), F32), pltpu.VMEM((kb, tq), F32), pltpu.VMEM((kb, tq), BF16)],
        compiler_params=_params(("parallel", "parallel", "arbitrary")),
    )(qs, k2, v2t)


def attn_backward(qs, qst, k2, v2, do, dot, lse, delta, n_valid):
    lp = qs.shape[0]
    tq, kb = _attn_tiles(lp)
    nk = lp // kb
    gw = 2 * LANES
    pad0 = n_valid - (nk - 1) * kb

    def body(q_ref, qt_ref, k_ref, v_ref, do_ref, dot_ref, lse_ref, dl_ref, dq_ref, dkt_ref, dvt_ref, acc_s):
        i = pl.program_id(1)
        j = pl.program_id(2)
        cols = pl.ds(pl.multiple_of(j * kb, kb), kb)

        @pl.when(j == 0)
        def _():
            acc_s[...] = jnp.zeros_like(acc_s)

        @pl.when(i == 0)
        def _():
            dkt_ref[0, :, cols] = jnp.zeros((LANES, kb), F32)
            dvt_ref[0, :, cols] = jnp.zeros((LANES, kb), F32)

        def step(masked):
            ks = _half_masks(k_ref[...])
            vs = _half_masks(v_ref[...])
            if masked:
                col = lax.broadcasted_iota(jnp.int32, (1, kb), 1)
                bias = jnp.where(col < pad0, 0.0, NEG)
            for pair in range(2):
                qp = q_ref[:, pair * LANES:(pair + 1) * LANES]
                dop = do_ref[:, pair * LANES:(pair + 1) * LANES]
                for half in range(2):
                    hh = 2 * pair + half
                    rs = slice(half * HEAD_DIM, (half + 1) * HEAD_DIM)
                    rt = slice(pair * LANES + half * HEAD_DIM, pair * LANES + (half + 1) * HEAD_DIM)
                    s = _nt(qp, ks[half])
                    if masked:
                        s = s + bias
                    p = jnp.exp2(s - lse_ref[0, hh])
                    dp = _nt(dop, vs[half])
                    ds = (p * (dp - dl_ref[0, hh])).astype(BF16)
                    pb = p.astype(BF16)
                    acc_s[pair] += _nn(ds, ks[half])
                    dvt_ref[0, rs, cols] += _nn(dot_ref[rt, :], pb)
                    dkt_ref[0, rs, cols] += _nn(qt_ref[rt, :], ds)

        if pad0 < kb:
            pl.when(j < nk - 1)(lambda: step(False))
            pl.when(j == nk - 1)(lambda: step(True))
        else:
            step(False)

        @pl.when(j == nk - 1)
        def _():
            for pair in range(2):
                dq_ref[:, pair * LANES:(pair + 1) * LANES] = acc_s[pair]

    cspec = pl.BlockSpec((1, Q_PER_KV, tq, 1), lambda g, i, j: (g, 0, i, 0))
    qspec = pl.BlockSpec((tq, gw), lambda g, i, j: (i, g))
    tspec = pl.BlockSpec((gw, tq), lambda g, i, j: (g, i))
    kspec = pl.BlockSpec((kb, LANES), lambda g, i, j: (j, g))
    gspec = pl.BlockSpec((1, LANES, lp), lambda g, i, j: (g, 0, 0))
    return pl.pallas_call(
        body, name="attn_backward", grid=(N_KV_HEADS, lp // tq, nk),
        in_specs=[qspec, tspec, kspec, kspec, qspec, tspec, cspec, cspec],
        out_specs=[qspec, gspec, gspec],
        out_shape=[jax.ShapeDtypeStruct((lp, N_QB * LANES), F32),
                   jax.ShapeDtypeStruct((N_KV_HEADS, LANES, lp), F32), jax.ShapeDtypeStruct((N_KV_HEADS, LANES, lp), F32)],
        scratch_shapes=[pltpu.VMEM((2, tq, LANES), F32)],
        compiler_params=_params(("parallel", "arbitrary", "arbitrary")),
    )(qs, qst, k2, v2, do, dot, lse, delta)


def attn_out_backward(dr, wout, o, e16):
    lp, d = dr.shape
    tm = _tm(lp)

    def body(dr_ref, w_ref, o_ref, e_ref, do_ref, dl_ref):
        do = _nt(dr_ref[...].astype(BF16), w_ref[...]).astype(BF16)
        do_ref[...] = do
        dl_ref[...] = _segsum(do.astype(F32) * o_ref[...].astype(F32), e_ref[...])

    rs = _row_spec(tm, d)
    return pl.pallas_call(
        body, name="attn_out_backward", grid=(lp // tm,),
        in_specs=[rs, _full_spec((d, d)), rs, _full_spec((d, N_Q_HEADS))],
        out_specs=[rs, _row_spec(tm, N_Q_HEADS)],
        out_shape=[jax.ShapeDtypeStruct((lp, d), BF16), jax.ShapeDtypeStruct((lp, N_Q_HEADS), F32)],
        compiler_params=_params(("parallel",)),
    )(dr, wout, o, e16)


N_CHIPS = 4


def _mesh_pos():
    return lax.axis_index("x"), lax.axis_index("y"), lax.axis_index("c")


def chip_exchange(arrs, scatter, name):
    n = len(arrs)
    hbm = pl.BlockSpec(memory_space=pl.ANY)

    def body(*refs):
        ins, outs = refs[:n], refs[n:2 * n]
        send_sems, recv_sems, loc_sems = refs[2 * n:]
        x, y, c = _mesh_pos()
        me = 2 * x + y
        chips = [(1 - x, y), (x, 1 - y), (1 - x, 1 - y)]
        started = []
        for a in range(n):
            loc = pltpu.make_async_copy(ins[a].at[me] if scatter else ins[a], outs[a].at[me], loc_sems.at[a])
            loc.start()
            started.append(loc)
            for k, (px, py) in enumerate(chips):
                src = ins[a].at[2 * px + py] if scatter else ins[a]
                cp = pltpu.make_async_remote_copy(
                    src_ref=src, dst_ref=outs[a].at[me], send_sem=send_sems.at[3 * a + k], recv_sem=recv_sems.at[3 * a + k],
                    device_id=(px, py, c), device_id_type=MESH)
                cp.start()
                started.append(cp)
        for cp in started:
            cp.wait()

    out_shape = [jax.ShapeDtypeStruct(a.shape if scatter else (N_CHIPS,) + a.shape, a.dtype) for a in arrs]
    return pl.pallas_call(
        body, name=name, in_specs=[hbm] * n, out_specs=[hbm] * n, out_shape=out_shape,
        scratch_shapes=[pltpu.SemaphoreType.DMA((3 * n,)), pltpu.SemaphoreType.DMA((3 * n,)), pltpu.SemaphoreType.DMA((n,))],
    )(*arrs)


def gather_two_level(arrs, name):
    n = len(arrs)
    hbm = pl.BlockSpec(memory_space=pl.ANY)

    def body(*refs):
        ins, outs = refs[:n], refs[n:2 * n]
        ici_send, ici_recv, d2d_send, d2d_recv, loc_sems = refs[2 * n:]
        x, y, c = _mesh_pos()
        me = 2 * x + y
        chips = [(1 - x, y), (x, 1 - y), (1 - x, 1 - y)]
        started = []
        for a in range(n):
            hn = arrs[a].shape[0] // 2
            mine = pl.ds(c * hn, hn)
            loc = pltpu.make_async_copy(ins[a], outs[a].at[me], loc_sems.at[a])
            loc.start()
            started.append(loc)
            first = []
            for k, (px, py) in enumerate(chips):
                cp = pltpu.make_async_remote_copy(
                    src_ref=ins[a].at[mine], dst_ref=outs[a].at[me, mine], send_sem=ici_send.at[3 * a + k],
                    recv_sem=ici_recv.at[3 * a + k], device_id=(px, py, c), device_id_type=MESH)
                cp.start()
                first.append(cp)
            for k, (px, py) in enumerate(chips):
                q = 2 * px + py
                first[k].wait_recv()
                fw = pltpu.make_async_remote_copy(
                    src_ref=outs[a].at[q, mine], dst_ref=outs[a].at[q, mine], send_sem=d2d_send.at[3 * a + k],
                    recv_sem=d2d_recv.at[3 * a + k], device_id=(x, y, 1 - c), device_id_type=MESH)
                fw.start()
                started.append(fw)
            for cp in first:
                cp.wait_send()
        for cp in started:
            cp.wait()

    out_shape = [jax.ShapeDtypeStruct((N_CHIPS,) + a.shape, a.dtype) for a in arrs]
    return pl.pallas_call(
        body, name=name, in_specs=[hbm] * n, out_specs=[hbm] * n, out_shape=out_shape,
        scratch_shapes=[pltpu.SemaphoreType.DMA((3 * n,))] * 4 + [pltpu.SemaphoreType.DMA((n,))],
    )(*arrs)


def sibling_exchange(arrs, name):
    n = len(arrs)
    hbm = pl.BlockSpec(memory_space=pl.ANY)

    def body(*refs):
        ins, outs = refs[:n], refs[n:2 * n]
        send_sems, recv_sems = refs[2 * n:]
        x, y, c = _mesh_pos()
        started = []
        for a in range(n):
            cp = pltpu.make_async_remote_copy(
                src_ref=ins[a], dst_ref=outs[a], send_sem=send_sems.at[a], recv_sem=recv_sems.at[a],
                device_id=(x, y, 1 - c), device_id_type=MESH)
            cp.start()
            started.append(cp)
        for cp in started:
            cp.wait()

    return pl.pallas_call(
        body, name=name, in_specs=[hbm] * n, out_specs=[hbm] * n,
        out_shape=[jax.ShapeDtypeStruct(a.shape, a.dtype) for a in arrs],
        scratch_shapes=[pltpu.SemaphoreType.DMA((n,)), pltpu.SemaphoreType.DMA((n,))],
    )(*arrs)


def _rows_tile(r, c):
    return _pick_tile(r, tuple(t for t in (512, 256, 128, 64, 32, 16, 8) if t * c * 4 <= 2 * 1024 * 1024))


def chip_sum(recv, name):
    _, r, c = recv.shape
    tr = _rows_tile(r, c)

    def body(r_ref, o_ref):
        acc = r_ref[0].astype(F32)
        for q in range(1, N_CHIPS):
            acc = acc + r_ref[q].astype(F32)
        o_ref[...] = acc

    return pl.pallas_call(
        body, name=name, grid=(r // tr,),
        in_specs=[pl.BlockSpec((N_CHIPS, tr, c), lambda i: (0, i, 0))],
        out_specs=pl.BlockSpec((tr, c), lambda i: (i, 0)),
        out_shape=jax.ShapeDtypeStruct((r, c), F32),
        compiler_params=_params(("parallel",)),
    )(recv)


def pair_sum(part, sib, name, dtype=F32):
    r, c = part.shape
    tr = _rows_tile(r, c)

    def body(p_ref, s_ref, o_ref):
        o_ref[...] = (p_ref[...].astype(F32) + s_ref[...].astype(F32)).astype(dtype)

    rs = pl.BlockSpec((tr, c), lambda i: (i, 0))
    return pl.pallas_call(
        body, name=name, grid=(r // tr,), in_specs=[rs] * 2, out_specs=rs,
        out_shape=jax.ShapeDtypeStruct((r, c), dtype), compiler_params=_params(("parallel",)),
    )(part, sib)


def adamw(g, w, m, v, name):
    r, c = w.shape
    tr = _rows_tile(r, c)

    def body(g_ref, w_ref, m_ref, v_ref, d_ref, nm_ref, nv_ref):
        g_ = g_ref[...]
        m_ = ADAM_B1 * m_ref[...] + (1.0 - ADAM_B1) * g_
        v_ = ADAM_B2 * v_ref[...] + (1.0 - ADAM_B2) * (g_ * g_)
        m_hat = m_ / (1.0 - ADAM_B1 ** ADAM_STEP)
        v_hat = v_ / (1.0 - ADAM_B2 ** ADAM_STEP)
        d_ref[...] = -ADAM_LR * (m_hat / (jnp.sqrt(v_hat) + ADAM_EPS) + ADAM_WD * w_ref[...])
        nm_ref[...] = m_
        nv_ref[...] = v_

    rs = pl.BlockSpec((tr, c), lambda i: (i, 0))
    return pl.pallas_call(
        body, name=name, grid=(r // tr,), in_specs=[rs] * 4, out_specs=[rs] * 3,
        out_shape=[jax.ShapeDtypeStruct((r, c), F32)] * 3,
        compiler_params=_params(("parallel",)),
    )(g, w, m, v)


WEIGHTS = ['meta_tokens', 's5_lambda_re', 's5_lambda_im', 's5_log_dt', 's5_b_re', 's5_b_im', 's5_c_re', 's5_c_im', 's5_d',
           's5_w_glu', 's5_w_out', 'attn_w_qkv', 'attn_q_gain', 'attn_k_gain', 'attn_w_out', 'ffn_w_gate', 'ffn_w_up',
           'ffn_w_down', 'ln_gain', 'ln_bias']
BIG = ['s5_w_glu', 's5_w_out', 'attn_w_qkv', 'attn_w_out', 'ffn_w_gate', 'ffn_w_up', 'ffn_w_down']
ROW_SHARDED = {'s5_w_glu', 's5_w_out', 'attn_w_out', 'ffn_w_down'}
SMALL_SHARDED = ['meta_tokens', 'ln_gain', 'ln_bias']
REPLICATED = ['s5_lambda_re', 's5_lambda_im', 's5_log_dt', 's5_b_re', 's5_b_im', 's5_c_re', 's5_c_im', 's5_d',
              'attn_q_gain', 'attn_k_gain']
REP_ALIGN = N_CHIPS * LANES * LANES


def _natural(gathered, row_sharded):
    p, n, a, b = gathered.shape
    if row_sharded:
        return jnp.transpose(gathered, (1, 0, 2, 3)).reshape(n, p * a, b)
    return jnp.transpose(gathered, (1, 2, 0, 3)).reshape(n, a, p * b)


def _shard_major(full, row_sharded):
    n, a, b = full.shape
    if row_sharded:
        return jnp.transpose(full.reshape(n, N_CHIPS, a // N_CHIPS, b), (1, 0, 2, 3))
    return jnp.transpose(full.reshape(n, a, N_CHIPS, b // N_CHIPS), (2, 0, 1, 3))


def _dup_heads(w):
    lead = w.shape[:-1]
    w = w.reshape(lead + (N_KV_HEADS, 1, HEAD_DIM))
    return jnp.broadcast_to(w, lead + (N_KV_HEADS, 2, HEAD_DIM)).reshape(lead + (N_KV_HEADS * 2 * HEAD_DIM,))


def _fold_heads(d):
    lead = d.shape[:-1]
    return d.reshape(lead + (N_KV_HEADS, 2, HEAD_DIM)).sum(axis=-2).reshape(lead + (N_KV_HEADS * HEAD_DIM,))


def _pack_rep(tree):
    flat = jnp.concatenate([tree[n].reshape(-1) for n in REPLICATED])
    pad = _round_up(flat.shape[0], REP_ALIGN) - flat.shape[0]
    return jnp.pad(flat, (0, pad))


def _unpack_rep(flat, like):
    out, off = {}, 0
    for n in REPLICATED:
        size = math.prod(like[n].shape)
        out[n] = flat[off:off + size].reshape(like[n].shape)
        off += size
    return out


def _train_step(x, loss_target, w, mom, vel):
    s = x.shape[1]
    n_valid = N_META + s
    lp = _round_up(n_valid, 2 * LANES)
    nq = N_Q_HEADS * HEAD_DIM
    nkv = N_KV_HEADS * HEAD_DIM

    small = jnp.concatenate([w[n].reshape(-1, w[n].shape[-1]) for n in SMALL_SHARDED], axis=0)
    gathered = gather_two_level([w[n].astype(BF16) for n in BIG] + [small], "gather_weights")
    full = {n: _natural(g, n in ROW_SHARDED) for n, g in zip(BIG, gathered[:-1])}
    small_full = jnp.transpose(gathered[-1], (1, 0, 2)).reshape(small.shape[0], D_MODEL)
    meta_full = small_full[:N_META]
    ln_gain = small_full[N_META:N_META + 2 * DEPTH].reshape(DEPTH, 2, 1, D_MODEL)
    ln_bias = small_full[N_META + 2 * DEPTH:].reshape(DEPTH, 2, 1, D_MODEL)
    wqkv = full['attn_w_qkv']
    w2 = jnp.concatenate([wqkv[..., :nq], _dup_heads(wqkv[..., nq:nq + nkv]), _dup_heads(wqkv[..., nq + nkv:])], axis=-1)

    cos, sin = rope_tables(lp, n_valid)
    e128 = head_sum_matrix()
    e16 = jnp.kron(jnp.eye(N_Q_HEADS, dtype=F32), jnp.ones((HEAD_DIM, 1), F32)).astype(BF16)
    gq = jnp.tile(w['attn_q_gain'], (1, 2))[:, None, :]
    gk = jnp.tile(w['attn_k_gain'], (1, 2))[:, None, :]

    pad_rows = jnp.zeros((lp - n_valid, D_MODEL), F32)
    h = jnp.concatenate([meta_full, x[0], pad_rows], axis=0)
    tgt = jnp.concatenate([jnp.zeros((N_META, D_MODEL), F32), loss_target[0], pad_rows], axis=0)

    saved = []
    s5_names = ['s5_lambda_re', 's5_lambda_im', 's5_log_dt', 's5_b_re', 's5_b_im', 's5_c_re', 's5_c_im']
    for i in range(DEPTH):
        j = i // 2
        sv = {'h': h}
        if i % 2 == 0:
            ops, sv['prep_vjp'] = jax.vjp(s5_prep, *[w[n][j] for n in s5_names])
            m_, wx_, ci_, at_ = ops
            sv['ops'] = (m_.astype(BF16), wx_.astype(BF16), ci_.astype(BF16), at_)
            y, sv['lhs'], sv['sp'], sv['sn'] = s5_forward(h, *sv['ops'], n_valid)
            sv['v'], sv['t'], sv['g'], sv['z'] = glu_forward(y, h, w['s5_d'][j][None], full['s5_w_glu'][j])
            sv['r1'], h1 = proj_ln_forward(sv['z'], full['s5_w_out'][j], h, ln_gain[i, 0], ln_bias[i, 0], "s5_out_ln")
        else:
            sv['raw'], sv['qs'], sv['k2'], sv['v2'] = qkv_forward(h, w2[j], gq[j], gk[j], cos, sin, e128)
            ot, sv['lse'] = attn_forward_t(sv['qs'], sv['k2'], sv['v2'].T, n_valid)
            sv['o'] = ot.T
            sv['r1'], h1 = proj_ln_forward(sv['o'], full['attn_w_out'][j], h, ln_gain[i, 0], ln_bias[i, 0], "attn_out_ln")
        sv['h1'] = h1
        sv['a'], sv['b'], sv['f'] = ffn_up_forward(h1, full['ffn_w_gate'][i], full['ffn_w_up'][i])
        sv['r2'], h = proj_ln_forward(sv['f'], full['ffn_w_down'][i], h1, ln_gain[i, 1], ln_bias[i, 1], "ffn_down_ln")
        saved.append(sv)

    dh, loss_part = loss_backward(h, tgt, n_valid)
    loss = lax.psum(jnp.sum(loss_part), ("x", "y", "c"))

    gfull = {n: [None] * w[n].shape[0] for n in BIG}
    d_ln_gain = [[None, None] for _ in range(DEPTH)]
    d_ln_bias = [[None, None] for _ in range(DEPTH)]
    grep = {n: [None] * w[n].shape[0] for n in REPLICATED}
    for i in reversed(range(DEPTH)):
        j = i // 2
        sv = saved[i]
        dr2, s2 = ln_backward(dh, sv['r2'], ln_gain[i, 1])
        d_ln_gain[i][1], d_ln_bias[i][1] = s2[0], s2[1]
        da, db = ffn_backward_act(dr2, full['ffn_w_down'][i], sv['a'], sv['b'])
        gfull['ffn_w_down'][i] = mm_tn(sv['f'], dr2, "grad_ffn_down")
        dh1 = resid_nt(dr2, [da, db], [full['ffn_w_gate'][i], full['ffn_w_up'][i]], "ffn_backward_x")
        gfull['ffn_w_gate'][i] = mm_tn(sv['h1'], da, "grad_ffn_gate")
        gfull['ffn_w_up'][i] = mm_tn(sv['h1'], db, "grad_ffn_up")
        dr1, s1 = ln_backward(dh1, sv['r1'], ln_gain[i, 0])
        d_ln_gain[i][0], d_ln_bias[i][0] = s1[0], s1[1]
        if i % 2 == 0:
            dt, dgd = glu_backward1(dr1, full['s5_w_out'][j], sv['g'], sv['t'])
            gfull['s5_w_out'][j] = mm_tn(sv['z'], dr1, "grad_s5_out")
            dv, dhs, sd = glu_backward2(dt, dgd, full['s5_w_glu'][j], sv['v'], sv['h'], w['s5_d'][j][None], dr1)
            grep['s5_d'][j] = sd[0]
            gfull['s5_w_glu'][j] = mm_tn(sv['g'], dt, "grad_s5_glu")
            dh, ldy, dxf, dxr, daf, dar = s5_backward(dv, dhs, *sv['ops'], sv['sp'], sv['sn'], n_valid)
            dm = bmm_tn(sv['lhs'], ldy, "grad_s5_m")
            dwx = jnp.stack([bmm_tn(sv['lhs'], dxf, "grad_s5_wxf"), bmm_tn(sv['lhs'], dxr, "grad_s5_wxr")])
            dci = jnp.stack([bmm_tn(sv['sp'], ldy, "grad_s5_cif"), bmm_tn(sv['sn'], ldy, "grad_s5_cir")])
            dps = sv['prep_vjp']((dm, dwx, dci, jnp.stack([daf, dar])))
            for n, g in zip(s5_names, dps):
                grep[n][j] = g
        else:
            do, delta = attn_out_backward(dr1, full['attn_w_out'][j], sv['o'], e16)
            gfull['attn_w_out'][j] = mm_tn(sv['o'], dr1, "grad_attn_out")
            heads = (N_KV_HEADS, Q_PER_KV, lp, 1)
            dq, dkt, dvt = attn_backward(sv['qs'], sv['qs'].T, sv['k2'], sv['v2'], do, do.T,
                                         sv['lse'].reshape(heads), delta.T.reshape(heads), n_valid)
            dk2 = jnp.transpose(dkt, (2, 0, 1)).reshape(lp, N_KB * LANES)
            dv2 = jnp.transpose(dvt, (2, 0, 1)).reshape(lp, N_KB * LANES)
            draw, gs = qkv_backward(dq, dk2, dv2, sv['raw'], gq[j], gk[j], cos, sin, e128)
            grep['attn_q_gain'][j] = gs[0, :HEAD_DIM] + gs[0, HEAD_DIM:]
            grep['attn_k_gain'][j] = gs[1, :HEAD_DIM] + gs[1, HEAD_DIM:]
            dh = resid_nt(dr1, [draw], [w2[j]], "attn_backward_x")
            dw2 = mm_tn(sv['h'], draw, "grad_attn_qkv")
            kq = N_QB * LANES
            kk = N_KB * LANES
            gfull['attn_w_qkv'][j] = jnp.concatenate(
                [dw2[:, :kq], _fold_heads(dw2[:, kq:kq + kk]), _fold_heads(dw2[:, kq + kk:])], axis=1)
    grad_x = dh[N_META:n_valid][None]

    core = lax.axis_index("c")
    contrib = [_shard_major(jnp.stack(gfull[n]), n in ROW_SHARDED) for n in BIG]
    small_g = jnp.concatenate([dh[:N_META], jnp.stack([g for pair in d_ln_gain for g in pair]),
                               jnp.stack([g for pair in d_ln_bias for g in pair])], axis=0)
    contrib.append(jnp.transpose(small_g.reshape(-1, N_CHIPS, D_MODEL // N_CHIPS), (1, 0, 2)))
    rep_g = _pack_rep({n: jnp.stack(grep[n]) for n in REPLICATED})
    contrib.append(rep_g.reshape(N_CHIPS, -1, LANES))
    names = BIG + ['small', 'rep']
    wire = [BF16] * len(BIG) + [F32, F32]
    keep, give = [], []
    for t, dt in zip(contrib, wire):
        hn = t.shape[1] // 2
        keep.append(lax.dynamic_slice_in_dim(t, core * hn, hn, axis=1))
        give.append(lax.dynamic_slice_in_dim(t, (1 - core) * hn, hn, axis=1).astype(dt))
    got = sibling_exchange(give, "sibling_contrib")
    two_d = lambda t: t.reshape(-1, t.shape[-1])
    pair = [pair_sum(two_d(a), two_d(b), "pair_sum_" + n, dt).reshape(a.shape)
            for n, a, b, dt in zip(names, keep, got, wire)]
    recv = chip_exchange(pair, True, "scatter_grads")
    halves = [chip_sum(r.reshape(N_CHIPS, -1, r.shape[-1]), "chip_sum_" + n) for n, r in zip(names, recv)]
    others = sibling_exchange(halves, "sibling_halves")
    grads = [jnp.where(core == 0, jnp.concatenate([a, b], axis=0), jnp.concatenate([b, a], axis=0))
             for a, b in zip(halves, others)]

    out = {}

    def update(n, g, wn, mn, vn):
        shape = wn.shape
        flat = (-1, shape[-1])
        d, nm, nv = adamw(g, wn.reshape(flat), mn.reshape(flat), vn.reshape(flat), "adamw_" + n)
        return tuple(t.reshape(shape) for t in (g, d, nm, nv))

    for n, g in zip(BIG, grads):
        out[n] = update(n, g, w[n], mom[n], vel[n])
    cat = lambda tree: jnp.concatenate([tree[n].reshape(-1, tree[n].shape[-1]) for n in SMALL_SHARDED], axis=0)
    sm = update("small", grads[-2], cat(w), cat(mom), cat(vel))
    off = 0
    for n in SMALL_SHARDED:
        rows = math.prod(w[n].shape[:-1])
        out[n] = tuple(t[off:off + rows].reshape(w[n].shape) for t in sm)
        off += rows
    rep_all = chip_exchange([grads[-1]], False, "gather_rep")[0].reshape(-1, LANES)
    rp = update("rep", rep_all, _pack_rep(w).reshape(-1, LANES), _pack_rep(mom).reshape(-1, LANES),
                _pack_rep(vel).reshape(-1, LANES))
    unpacked = [_unpack_rep(t.reshape(-1), w) for t in rp]
    for n in REPLICATED:
        out[n] = tuple(u[n] for u in unpacked)

    return (loss, grad_x, *[out[n][0] for n in WEIGHTS], *[out[n][1] for n in WEIGHTS],
            *[out[n][2] for n in WEIGHTS], *[out[n][3] for n in WEIGHTS])


def kernel(x, meta_tokens, s5_lambda_re, s5_lambda_im, s5_log_dt, s5_b_re, s5_b_im, s5_c_re, s5_c_im, s5_d, s5_w_glu, s5_w_out, attn_w_qkv, attn_q_gain, attn_k_gain, attn_w_out, ffn_w_gate, ffn_w_up, ffn_w_down, ln_gain, ln_bias, loss_target, m_meta_tokens, m_s5_lambda_re, m_s5_lambda_im, m_s5_log_dt, m_s5_b_re, m_s5_b_im, m_s5_c_re, m_s5_c_im, m_s5_d, m_s5_w_glu, m_s5_w_out, m_attn_w_qkv, m_attn_q_gain, m_attn_k_gain, m_attn_w_out, m_ffn_w_gate, m_ffn_w_up, m_ffn_w_down, m_ln_gain, m_ln_bias, v_meta_tokens, v_s5_lambda_re, v_s5_lambda_im, v_s5_log_dt, v_s5_b_re, v_s5_b_im, v_s5_c_re, v_s5_c_im, v_s5_d, v_s5_w_glu, v_s5_w_out, v_attn_w_qkv, v_attn_q_gain, v_attn_k_gain, v_attn_w_out, v_ffn_w_gate, v_ffn_w_up, v_ffn_w_down, v_ln_gain, v_ln_bias):
    given = locals()
    w = {n: given[n] for n in WEIGHTS}
    mom = {n: given["m_" + n] for n in WEIGHTS}
    vel = {n: given["v_" + n] for n in WEIGHTS}
    return _train_step(x, loss_target, w, mom, vel)
```

```python
import math

import jax
import jax.numpy as jnp
from jax import lax
from jax.experimental import pallas as pl
from jax.experimental.pallas import tpu as pltpu

F32 = jnp.float32
BF16 = jnp.bfloat16
MESH = pl.DeviceIdType.MESH

D_MODEL = 1024
N_META = 16
GRID_W = 64
HEAD_DIM = 64
N_Q_HEADS = 16
N_KV_HEADS = 4
ROPE_THETA = 10000.0
QK_EPS = 1e-6
S5_CH = 16
S5_GROUPS = 64
S5_STATE = 64
D_FF = 2816
LN_EPS = 1e-5
DEPTH = 4
ALPHA = (2.0 * DEPTH) ** 0.25
ADAM_LR, ADAM_B1, ADAM_B2, ADAM_EPS, ADAM_WD, ADAM_STEP = 0.001, 0.9, 0.999, 1e-08, 0.01, 10

LANES = 128
SUBLANES = 8
VMEM_LIMIT = 56 * 1024 * 1024

S5_T = 8
S5_GB = LANES // S5_CH
S5_NJ = S5_GROUPS // S5_GB
S5_W = S5_T * LANES
S5_SW = 2 * S5_GB * S5_STATE
S5_HALF = S5_SW // 2


def _round_up(a, b):
    return -(-a // b) * b


def _pick_tile(n, prefs):
    for t in prefs:
        if n % t == 0:
            return t
    return n


def _params(sem=None):
    kw = dict(vmem_limit_bytes=VMEM_LIMIT)
    if sem is not None:
        kw["dimension_semantics"] = sem
    return pltpu.CompilerParams(**kw)


def _dot(a, b, dims):
    return lax.dot_general(a, b, (dims, ((), ())), preferred_element_type=F32)


def _nn(a, b):
    return _dot(a, b, ((1,), (0,)))


def _nt(a, b):
    return _dot(a, b, ((1,), (1,)))


def _tn(a, b):
    return _dot(a, b, ((0,), (0,)))


def _compact(w):
    g, a0, a1, b0, b1 = w.shape
    w = jnp.transpose(w.reshape(S5_NJ, S5_GB, a0, a1, b0, b1), (0, 2, 1, 3, 4, 5))
    return w.reshape(S5_NJ, a0 * S5_GB * a1, b0 * b1)


def s5_prep(lam_re, lam_im, log_dt, b_re, b_im, c_re, c_im):
    hi = lax.Precision.HIGHEST
    t = S5_T
    dt = jnp.exp(log_dt)[..., None]
    taus = jnp.arange(t + 1, dtype=F32)[:, None, None, None]
    mag = jnp.exp(lam_re * dt)
    ang = lam_im * dt
    pr = jnp.concatenate([jnp.ones_like(mag)[None], (mag * jnp.cos(ang))[None],
                          jnp.exp(lam_re * dt * taus[2:]) * jnp.cos(ang * taus[2:])], axis=0)
    pi = jnp.concatenate([jnp.zeros_like(mag)[None], (mag * jnp.sin(ang))[None],
                          jnp.exp(lam_re * dt * taus[2:]) * jnp.sin(ang * taus[2:])], axis=0)
    abr, abi = pr[1], pi[1]
    nr, ni = abr - 1.0, abi
    den = lam_re * lam_re + lam_im * lam_im
    cr = (nr * lam_re + ni * lam_im) / den
    ci_ = (ni * lam_re - nr * lam_im) / den
    bbr = cr[..., None] * b_re - ci_[..., None] * b_im
    bbi = cr[..., None] * b_im + ci_[..., None] * b_re
    er = c_re[None] * pr[:, :, :, None, :] - c_im[None] * pi[:, :, :, None, :]
    ei = c_re[None] * pi[:, :, :, None, :] + c_im[None] * pr[:, :, :, None, :]
    kk = (jnp.einsum("tdgop,dgpc->tdgoc", er[:t], bbr, precision=hi)
          - jnp.einsum("tdgop,dgpc->tdgoc", ei[:t], bbi, precision=hi))
    zero = jnp.zeros_like(kk[0, 0])
    mg = jnp.stack([jnp.stack([(kk[i - s, 0] if i > s else zero) + (kk[s - i, 1] if s > i else zero)
                               + ((kk[0, 0] + kk[0, 1]) if i == s else zero) for i in range(t)])
                    for s in range(t)])
    mg = jnp.transpose(mg, (2, 0, 4, 1, 3))
    m = _compact(mg)
    pw_f = jnp.stack([pr[t - 1 - s, 0] for s in range(t)]), jnp.stack([pi[t - 1 - s, 0] for s in range(t)])
    pw_r = jnp.stack([pr[s, 1] for s in range(t)]), jnp.stack([pi[s, 1] for s in range(t)])
    wx = []
    for d, (qr, qi) in enumerate((pw_f, pw_r)):
        wr = qr[..., None] * bbr[d][None] - qi[..., None] * bbi[d][None]
        wi = qr[..., None] * bbi[d][None] + qi[..., None] * bbr[d][None]
        w = jnp.stack([wr, wi], axis=0)
        w = jnp.transpose(w, (2, 1, 4, 0, 3))
        wx.append(_compact(w))
    ci = []
    for d in range(2):
        exps = [i + 1 for i in range(t)] if d == 0 else [t - i for i in range(t)]
        e_r = jnp.stack([er[e, d] for e in exps])
        e_i = jnp.stack([ei[e, d] for e in exps])
        w = jnp.stack([e_r, -e_i], axis=0)
        w = jnp.transpose(w, (2, 0, 4, 1, 3))
        ci.append(_compact(w))
    at = jnp.stack([pr[t], pi[t]], axis=1)
    at = at.reshape(2, 2, S5_NJ, S5_GB * S5_STATE)
    at = jnp.transpose(at, (0, 2, 1, 3)).reshape(2, S5_NJ, 1, S5_SW)
    return m, jnp.stack(wx), jnp.stack(ci), at


def _chunk_rows(ref, nc):
    return jnp.concatenate([ref[pl.ds(s, nc, stride=S5_T), :] for s in range(S5_T)], axis=1)


def _cmul(ar, ai, sr, si):
    return ar * sr - ai * si, ar * si + ai * sr


def _scan_tiles(nc, reverse, step):
    nt = nc // SUBLANES

    def body(it, carry):
        tix = (nt - 1 - it) if reverse else it
        k0 = pl.multiple_of(tix * SUBLANES, SUBLANES)
        return step(k0, carry)

    return body, nt


def _s5_specs(nc):
    hbm = pl.BlockSpec(memory_space=pl.ANY)
    aspec = pl.BlockSpec((1, 1, S5_SW), lambda j: (j, 0, 0))
    cspec = pl.BlockSpec((1, nc, S5_W), lambda j: (j, 0, 0))
    return hbm, aspec, cspec


def _s5_fetch(j, tok_hbm, w_hbms, tok_s, w_s, sems):
    cols = pl.ds(pl.multiple_of(j * LANES, LANES), LANES)
    cps = [pltpu.make_async_copy(tok_hbm.at[:, cols], tok_s, sems.at[0])]
    for i, w in enumerate(w_hbms):
        cps.append(pltpu.make_async_copy(w.at[j], w_s.at[i], sems.at[1 + i]))
    for cp in cps:
        cp.start()
    return cols, cps


def s5_forward(u, m, wx, ci, at, n_valid):
    lp = u.shape[0]
    nc = lp // S5_T
    nvc = n_valid // S5_T

    def body(u_hbm, m_hbm, wxf_hbm, wxr_hbm, cif_hbm, cir_hbm, atf_ref, atr_ref,
             y_hbm, lhs_ref, sp_ref, sn_ref, tok_s, w_s, xf_s, xr_s, sems):
        j = pl.program_id(0)
        cols, cps = _s5_fetch(j, u_hbm, (m_hbm, wxf_hbm, wxr_hbm, cif_hbm, cir_hbm), tok_s, w_s, sems)
        cps[0].wait()
        lhs = _chunk_rows(tok_s, nc)
        rows = lax.broadcasted_iota(jnp.int32, lhs.shape, 0)
        lhs = jnp.where(rows < nvc, lhs, 0.0).astype(BF16)
        lhs_ref[0] = lhs
        cps[2].wait()
        cps[3].wait()
        xf_s[...] = _nn(lhs, w_s[1])
        xr_s[...] = _nn(lhs, w_s[2])
        afr, afi = atf_ref[0, :, :S5_HALF], atf_ref[0, :, S5_HALF:]
        arr, ari = atr_ref[0, :, :S5_HALF], atr_ref[0, :, S5_HALF:]

        def scan_step(x_s, ar, ai, descending):
            def step(k0, carry):
                sr, si = carry
                x = x_s[pl.ds(k0, SUBLANES), :]
                outs = [None] * SUBLANES
                order = reversed(range(SUBLANES)) if descending else range(SUBLANES)
                for r in order:
                    outs[r] = jnp.concatenate([sr, si], axis=1)
                    nr, ni = _cmul(ar, ai, sr, si)
                    sr = nr + x[r:r + 1, :S5_HALF]
                    si = ni + x[r:r + 1, S5_HALF:]
                x_s[pl.ds(k0, SUBLANES), :] = jnp.concatenate(outs, axis=0)
                return sr, si
            return step

        zero = jnp.zeros((1, S5_HALF), F32)
        fb, nt = _scan_tiles(nc, False, scan_step(xf_s, afr, afi, False))
        lax.fori_loop(0, nt, fb, (zero, zero))
        rb, nt = _scan_tiles(nc, True, scan_step(xr_s, arr, ari, True))
        lax.fori_loop(0, nt, rb, (zero, zero))
        sp = xf_s[...].astype(BF16)
        sn = xr_s[...].astype(BF16)
        sp_ref[0] = sp
        sn_ref[0] = sn
        cps[1].wait()
        cps[4].wait()
        cps[5].wait()
        y = _nn(lhs, w_s[0]) + _nn(sp, w_s[3]) + _nn(sn, w_s[4])
        for i in range(S5_T):
            tok_s[pl.ds(i, nc, stride=S5_T), :] = y[:, i * LANES:(i + 1) * LANES]
        pltpu.sync_copy(tok_s, y_hbm.at[:, cols])

    hbm, aspec, cspec = _s5_specs(nc)
    return pl.pallas_call(
        body, name="s5_forward", grid=(S5_NJ,),
        in_specs=[hbm] * 6 + [aspec, aspec],
        out_specs=[hbm, cspec, cspec, cspec],
        out_shape=[jax.ShapeDtypeStruct((lp, D_MODEL), F32)] + [jax.ShapeDtypeStruct((S5_NJ, nc, S5_W), BF16)] * 3,
        scratch_shapes=[pltpu.VMEM((lp, LANES), F32), pltpu.VMEM((5, S5_W, S5_W), BF16),
                        pltpu.VMEM((nc, S5_SW), F32), pltpu.VMEM((nc, S5_SW), F32), pltpu.SemaphoreType.DMA((6,))],
        compiler_params=_params(("arbitrary",)),
    )(u, m, wx[0], wx[1], ci[0], ci[1], at[0], at[1])


def s5_backward(dy, dhs, m, wx, ci, at, sp, sn, n_valid):
    lp = dy.shape[0]
    nc = lp // S5_T
    nvc = n_valid // S5_T

    def body(dy_hbm, dhs_hbm, m_hbm, wxf_hbm, wxr_hbm, cif_hbm, cir_hbm, atf_ref, atr_ref, sp_ref, sn_ref,
             dh_hbm, ldy_ref, dxf_ref, dxr_ref, daf_ref, dar_ref, tok_s, w_s, gf_s, gr_s, sems):
        j = pl.program_id(0)
        cols, cps = _s5_fetch(j, dy_hbm, (m_hbm, wxf_hbm, wxr_hbm, cif_hbm, cir_hbm), tok_s, w_s, sems)
        cps[0].wait()
        ldy = _chunk_rows(tok_s, nc)
        rows = lax.broadcasted_iota(jnp.int32, ldy.shape, 0)
        ldy = jnp.where(rows < nvc, ldy, 0.0).astype(BF16)
        ldy_ref[0] = ldy
        resid = pltpu.make_async_copy(dhs_hbm.at[:, cols], tok_s, sems.at[0])
        resid.start()
        cps[4].wait()
        cps[5].wait()
        gf_s[...] = _nt(ldy, w_s[3])
        gr_s[...] = _nt(ldy, w_s[4])
        afr, afi = atf_ref[0, :, :S5_HALF], atf_ref[0, :, S5_HALF:]
        arr, ari = atr_ref[0, :, :S5_HALF], atr_ref[0, :, S5_HALF:]

        def adj_step(g_s, s_ref, ar, ai, descending):
            def step(k0, carry):
                gr_, gi_, dr_, di_ = carry
                g = g_s[pl.ds(k0, SUBLANES), :]
                p = s_ref[0, pl.ds(k0, SUBLANES), :].astype(F32)
                outs = [None] * SUBLANES
                order = reversed(range(SUBLANES)) if descending else range(SUBLANES)
                for r in order:
                    outs[r] = jnp.concatenate([gr_, gi_], axis=1)
                    pr_, pi_ = p[r:r + 1, :S5_HALF], p[r:r + 1, S5_HALF:]
                    dr_ = dr_ + gr_ * pr_ + gi_ * pi_
                    di_ = di_ + gi_ * pr_ - gr_ * pi_
                    nr, ni = _cmul(ar, -ai, gr_, gi_)
                    gr_ = nr + g[r:r + 1, :S5_HALF]
                    gi_ = ni + g[r:r + 1, S5_HALF:]
                g_s[pl.ds(k0, SUBLANES), :] = jnp.concatenate(outs, axis=0)
                return gr_, gi_, dr_, di_
            return step

        zero = jnp.zeros((1, S5_HALF), F32)
        fb, nt = _scan_tiles(nc, True, adj_step(gf_s, sp_ref, afr, afi, True))
        _, _, dr_, di_ = lax.fori_loop(0, nt, fb, (zero,) * 4)
        daf_ref[0] = jnp.concatenate([dr_, di_], axis=1)
        rb, nt = _scan_tiles(nc, False, adj_step(gr_s, sn_ref, arr, ari, False))
        _, _, dr_, di_ = lax.fori_loop(0, nt, rb, (zero,) * 4)
        dar_ref[0] = jnp.concatenate([dr_, di_], axis=1)
        dxf = gf_s[...].astype(BF16)
        dxr = gr_s[...].astype(BF16)
        dxf_ref[0] = dxf
        dxr_ref[0] = dxr
        cps[1].wait()
        cps[2].wait()
        cps[3].wait()
        du = _nt(ldy, w_s[0]) + _nt(dxf, w_s[1]) + _nt(dxr, w_s[2])
        rows = lax.broadcasted_iota(jnp.int32, du.shape, 0)
        du = jnp.where(rows < nvc, du, 0.0)
        resid.wait()
        for s in range(S5_T):
            tok_s[pl.ds(s, nc, stride=S5_T), :] += du[:, s * LANES:(s + 1) * LANES]
        pltpu.sync_copy(tok_s, dh_hbm.at[:, cols])

    hbm, aspec, cspec = _s5_specs(nc)
    return pl.pallas_call(
        body, name="s5_backward", grid=(S5_NJ,),
        in_specs=[hbm] * 7 + [aspec, aspec, cspec, cspec],
        out_specs=[hbm, cspec, cspec, cspec, aspec, aspec],
        out_shape=[jax.ShapeDtypeStruct((lp, D_MODEL), F32)] + [jax.ShapeDtypeStruct((S5_NJ, nc, S5_W), BF16)] * 3
        + [jax.ShapeDtypeStruct((S5_NJ, 1, S5_SW), F32)] * 2,
        scratch_shapes=[pltpu.VMEM((lp, LANES), F32), pltpu.VMEM((5, S5_W, S5_W), BF16),
                        pltpu.VMEM((nc, S5_SW), F32), pltpu.VMEM((nc, S5_SW), F32), pltpu.SemaphoreType.DMA((6,))],
        compiler_params=_params(("arbitrary",)),
    )(dy, dhs, m, wx[0], wx[1], ci[0], ci[1], at[0], at[1], sp, sn)


S5_CW = LANES


def _replicate_matrix(b1):
    b0n = S5_CW // b1
    eye0 = jnp.eye(b0n, dtype=F32)
    eye1 = jnp.eye(b1, dtype=F32)
    r = jnp.einsum("ab,cd->acbd", eye0, eye1)[:, :, :, None, :]
    r = jnp.broadcast_to(r, (b0n, b1, b0n, S5_GB, b1))
    return r.reshape(S5_CW, b0n * S5_GB * b1).astype(BF16)


def _same_group(a1, b1):
    rg = (lax.broadcasted_iota(jnp.int32, (S5_W, S5_W), 0) // a1) % S5_GB
    cg = (lax.broadcasted_iota(jnp.int32, (S5_W, S5_W), 1) // b1) % S5_GB
    return rg == cg


def blockdiag_expand(compact, a1, b1, name):
    nj = compact.shape[0]

    def body(c_ref, r_ref, o_ref):
        rep = _nn(c_ref[0].astype(BF16), r_ref[...])
        o_ref[0] = jnp.where(_same_group(a1, b1), rep, 0.0).astype(BF16)

    return pl.pallas_call(
        body, name=name, grid=(nj,),
        in_specs=[pl.BlockSpec((1, S5_W, S5_CW), lambda j: (j, 0, 0)), _full_spec((S5_CW, S5_W))],
        out_specs=pl.BlockSpec((1, S5_W, S5_W), lambda j: (j, 0, 0)),
        out_shape=jax.ShapeDtypeStruct((nj, S5_W, S5_W), BF16),
        compiler_params=_params(("parallel",)),
    )(compact, _replicate_matrix(b1))


def bmm_tn_compact(a, b, a1, b1, name):
    nj, k, wa = a.shape
    wb = b.shape[2]

    def body(a_ref, b_ref, r_ref, o_ref):
        prod = jnp.where(_same_group(a1, b1), _tn(a_ref[0], b_ref[0]), 0.0)
        hi = prod.astype(BF16)
        lo = (prod - hi.astype(F32)).astype(BF16)
        o_ref[0] = _nt(hi, r_ref[...]) + _nt(lo, r_ref[...])

    return pl.pallas_call(
        body, name=name, grid=(nj,),
        in_specs=[pl.BlockSpec((1, k, wa), lambda j: (j, 0, 0)), pl.BlockSpec((1, k, wb), lambda j: (j, 0, 0)),
                  _full_spec((S5_CW, S5_W))],
        out_specs=pl.BlockSpec((1, wa, S5_CW), lambda j: (j, 0, 0)),
        out_shape=jax.ShapeDtypeStruct((nj, wa, S5_CW), F32),
        compiler_params=_params(("parallel",)),
    )(a, b, _replicate_matrix(b1))


def _tm(lp):
    return _pick_tile(lp, (768, 256))


def _row_spec(tm, width):
    return pl.BlockSpec((tm, width), lambda i: (i, 0))


def _full_spec(shape):
    return pl.BlockSpec(shape, lambda *_: (0,) * len(shape))


def _gelu(v):
    return 0.5 * v * (1.0 + lax.erf(v * (2.0 ** -0.5)))


def _gelu_grad(v):
    return 0.5 * (1.0 + lax.erf(v * (2.0 ** -0.5))) + v * jnp.exp(-0.5 * v * v) * (2.0 * math.pi) ** -0.5


def _layer_norm(r, gain, bias):
    mean = jnp.mean(r, axis=-1, keepdims=True)
    c = r - mean
    var = jnp.mean(c * c, axis=-1, keepdims=True)
    return c * lax.rsqrt(var + LN_EPS) * gain + bias


def glu_forward(y, h, dvec, wglu):
    lp, d = y.shape
    tm = _tm(lp)

    def body(y_ref, h_ref, d_ref, w_ref, v_ref, t_ref, g_ref, z_ref):
        v = y_ref[...] + d_ref[...] * h_ref[...]
        g = _gelu(v)
        gb = g.astype(BF16)
        t = _nn(gb, w_ref[...])
        v_ref[...] = v
        t_ref[...] = t
        g_ref[...] = gb
        z_ref[...] = (g * jax.nn.sigmoid(t)).astype(BF16)

    rs = _row_spec(tm, d)
    return pl.pallas_call(
        body, name="glu_forward", grid=(lp // tm,),
        in_specs=[rs, rs, _full_spec((1, d)), _full_spec((d, d))],
        out_specs=[rs, rs, rs, rs],
        out_shape=[jax.ShapeDtypeStruct((lp, d), F32)] * 2 + [jax.ShapeDtypeStruct((lp, d), BF16)] * 2,
        compiler_params=_params(("parallel",)),
    )(y, h, dvec, wglu)


def proj_ln_forward(z, w, h, gain, bias, name):
    lp, k = z.shape
    d = w.shape[1]
    tm = _tm(lp)

    def body(z_ref, w_ref, h_ref, g_ref, b_ref, r_ref, o_ref):
        r = ALPHA * h_ref[...] + _nn(z_ref[...], w_ref[...])
        r_ref[...] = r
        o_ref[...] = _layer_norm(r, g_ref[...], b_ref[...])

    rs = _row_spec(tm, d)
    return pl.pallas_call(
        body, name=name, grid=(lp // tm,),
        in_specs=[_row_spec(tm, k), _full_spec((k, d)), rs, _full_spec((1, d)), _full_spec((1, d))],
        out_specs=[rs, rs],
        out_shape=[jax.ShapeDtypeStruct((lp, d), F32)] * 2,
        compiler_params=_params(("parallel",)),
    )(z, w, h, gain, bias)


FFN_NB = 1408


def ffn_up_forward(h, wg, wu):
    lp, d = h.shape
    dff = wg.shape[1]
    tm = _tm(lp)

    def body(h_ref, wg_ref, wu_ref, a_ref, b_ref, f_ref):
        hb = h_ref[...].astype(BF16)
        a = _nn(hb, wg_ref[...])
        b = _nn(hb, wu_ref[...])
        a_ref[...] = a.astype(BF16)
        b_ref[...] = b.astype(BF16)
        f_ref[...] = (a * jax.nn.sigmoid(a) * b).astype(BF16)

    ws = pl.BlockSpec((d, FFN_NB), lambda n, i: (0, n))
    os_ = pl.BlockSpec((tm, FFN_NB), lambda n, i: (i, n))
    return pl.pallas_call(
        body, name="ffn_up_forward", grid=(dff // FFN_NB, lp // tm),
        in_specs=[pl.BlockSpec((tm, d), lambda n, i: (i, 0)), ws, ws],
        out_specs=[os_, os_, os_],
        out_shape=[jax.ShapeDtypeStruct((lp, dff), BF16)] * 3,
        compiler_params=_params(("parallel", "parallel")),
    )(h, wg, wu)


def ln_backward(dh, r, gain):
    lp, d = dh.shape
    tm = _tm(lp)

    def body(dh_ref, r_ref, g_ref, dr_ref, s_ref):
        r_ = r_ref[...]
        dh_ = dh_ref[...]
        mean = jnp.mean(r_, axis=-1, keepdims=True)
        c = r_ - mean
        var = jnp.mean(c * c, axis=-1, keepdims=True)
        rstd = lax.rsqrt(var + LN_EPS)
        xh = c * rstd
        dxh = dh_ * g_ref[...]
        m1 = jnp.mean(dxh, axis=-1, keepdims=True)
        m2 = jnp.mean(dxh * xh, axis=-1, keepdims=True)
        dr_ref[...] = rstd * (dxh - m1 - xh * m2)

        @pl.when(pl.program_id(0) == 0)
        def _():
            s_ref[...] = jnp.zeros_like(s_ref)

        s_ref[0:1, :] += jnp.sum(dh_ * xh, axis=0, keepdims=True)
        s_ref[1:2, :] += jnp.sum(dh_, axis=0, keepdims=True)

    rs = _row_spec(tm, d)
    return pl.pallas_call(
        body, name="ln_backward", grid=(lp // tm,),
        in_specs=[rs, rs, _full_spec((1, d))],
        out_specs=[rs, _full_spec((SUBLANES, d))],
        out_shape=[jax.ShapeDtypeStruct((lp, d), F32), jax.ShapeDtypeStruct((SUBLANES, d), F32)],
        compiler_params=_params(("arbitrary",)),
    )(dh, r, gain)


def ffn_backward_act(dr, wd, a, b):
    lp, d = dr.shape
    dff = wd.shape[0]
    tm = _tm(lp)

    def body(dr_ref, wd_ref, a_ref, b_ref, da_ref, db_ref):
        df = _nt(dr_ref[...].astype(BF16), wd_ref[...])
        a_ = a_ref[...].astype(F32)
        b_ = b_ref[...].astype(F32)
        sg = jax.nn.sigmoid(a_)
        da_ref[...] = (df * b_ * sg * (1.0 + a_ * (1.0 - sg))).astype(BF16)
        db_ref[...] = (df * a_ * sg).astype(BF16)

    os_ = pl.BlockSpec((tm, FFN_NB), lambda n, i: (i, n))
    return pl.pallas_call(
        body, name="ffn_backward_act", grid=(dff // FFN_NB, lp // tm),
        in_specs=[pl.BlockSpec((tm, d), lambda n, i: (i, 0)), pl.BlockSpec((FFN_NB, d), lambda n, i: (n, 0)), os_, os_],
        out_specs=[os_, os_],
        out_shape=[jax.ShapeDtypeStruct((lp, dff), BF16)] * 2,
        compiler_params=_params(("parallel", "parallel")),
    )(dr, wd, a, b)


def resid_nt(dr, xs, ws, name):
    lp, d = dr.shape
    tm = _tm(lp)
    n = len(xs)

    def body(*refs):
        acc = ALPHA * refs[0][...]
        for i in range(n):
            acc = acc + _nt(refs[1 + i][...], refs[1 + n + i][...])
        refs[-1][...] = acc

    rs = _row_spec(tm, d)
    in_specs = [rs] + [_row_spec(tm, x.shape[1]) for x in xs] + [_full_spec(w.shape) for w in ws]
    return pl.pallas_call(
        body, name=name, grid=(lp // tm,),
        in_specs=in_specs, out_specs=rs,
        out_shape=jax.ShapeDtypeStruct((lp, d), F32),
        compiler_params=_params(("parallel",)),
    )(dr, *xs, *ws)


def mm_tn(x, y, name):
    lp, k = x.shape
    n = y.shape[1]
    tm = _tm(lp)
    nb = _pick_tile(n, (512, 1408))

    def body(x_ref, y_ref, o_ref):
        @pl.when(pl.program_id(1) == 0)
        def _():
            o_ref[...] = jnp.zeros_like(o_ref)

        o_ref[...] += _tn(x_ref[...].astype(BF16), y_ref[...].astype(BF16))

    return pl.pallas_call(
        body, name=name, grid=(n // nb, lp // tm),
        in_specs=[pl.BlockSpec((tm, k), lambda j, i: (i, 0)), pl.BlockSpec((tm, nb), lambda j, i: (i, j))],
        out_specs=pl.BlockSpec((k, nb), lambda j, i: (0, j)),
        out_shape=jax.ShapeDtypeStruct((k, n), F32),
        compiler_params=_params(("parallel", "arbitrary")),
    )(x, y)


def glu_backward1(dr, wout, g, t):
    lp, d = dr.shape
    tm = _tm(lp)

    def body(dr_ref, w_ref, g_ref, t_ref, dt_ref, dgd_ref):
        dz = _nt(dr_ref[...].astype(BF16), w_ref[...])
        s = jax.nn.sigmoid(t_ref[...])
        dgd_ref[...] = dz * s
        dt_ref[...] = (dz * g_ref[...].astype(F32) * s * (1.0 - s)).astype(BF16)

    rs = _row_spec(tm, d)
    return pl.pallas_call(
        body, name="glu_backward1", grid=(lp // tm,),
        in_specs=[rs, _full_spec((d, d)), rs, rs],
        out_specs=[rs, rs],
        out_shape=[jax.ShapeDtypeStruct((lp, d), BF16), jax.ShapeDtypeStruct((lp, d), F32)],
        compiler_params=_params(("parallel",)),
    )(dr, wout, g, t)


def glu_backward2(dt, dgd, wglu, v, h, dvec, dr):
    lp, d = dt.shape
    tm = _tm(lp)

    def body(dt_ref, dgd_ref, w_ref, v_ref, h_ref, d_ref, dr_ref, dv_ref, dhs_ref, s_ref):
        dg = dgd_ref[...] + _nt(dt_ref[...], w_ref[...])
        dv = dg * _gelu_grad(v_ref[...])
        dv_ref[...] = dv
        dhs_ref[...] = ALPHA * dr_ref[...] + dv * d_ref[...]

        @pl.when(pl.program_id(0) == 0)
        def _():
            s_ref[...] = jnp.zeros_like(s_ref)

        s_ref[0:1, :] += jnp.sum(dv * h_ref[...], axis=0, keepdims=True)

    rs = _row_spec(tm, d)
    return pl.pallas_call(
        body, name="glu_backward2", grid=(lp // tm,),
        in_specs=[rs, rs, _full_spec((d, d)), rs, rs, _full_spec((1, d)), rs],
        out_specs=[rs, rs, _full_spec((SUBLANES, d))],
        out_shape=[jax.ShapeDtypeStruct((lp, d), F32)] * 2 + [jax.ShapeDtypeStruct((SUBLANES, d), F32)],
        compiler_params=_params(("arbitrary",)),
    )(dt, dgd, wglu, v, h, dvec, dr)


def loss_backward(hf, tgt, n_valid):
    lp, d = hf.shape
    tm = _tm(lp)

    def body(h_ref, t_ref, dh_ref, s_ref):
        rows = pl.program_id(0) * tm + lax.broadcasted_iota(jnp.int32, (tm, d), 0)
        ok = (rows >= N_META) & (rows < n_valid)
        e = jnp.where(ok, h_ref[...] - t_ref[...], 0.0)
        dh_ref[...] = e * (1.0 / d)

        @pl.when(pl.program_id(0) == 0)
        def _():
            s_ref[...] = jnp.zeros_like(s_ref)

        sq = e * e
        part = sq[:, 0:LANES]
        for c in range(1, d // LANES):
            part = part + sq[:, c * LANES:(c + 1) * LANES]
        acc = part[0:SUBLANES]
        for r in range(1, tm // SUBLANES):
            acc = acc + part[r * SUBLANES:(r + 1) * SUBLANES]
        s_ref[...] += acc * (0.5 / d)

    rs = _row_spec(tm, d)
    return pl.pallas_call(
        body, name="loss_backward", grid=(lp // tm,),
        in_specs=[rs, rs], out_specs=[rs, _full_spec((SUBLANES, LANES))],
        out_shape=[jax.ShapeDtypeStruct((lp, d), F32), jax.ShapeDtypeStruct((SUBLANES, LANES), F32)],
        compiler_params=_params(("arbitrary",)),
    )(hf, tgt)


N_QB = N_Q_HEADS // 2
N_KB = N_KV_HEADS
QKV_W = (N_QB + 2 * N_KB) * LANES
Q_SCALE = HEAD_DIM ** -0.5 * math.log2(math.e)


def rope_tables(lp, n_valid):
    t = jnp.arange(lp, dtype=jnp.int32)
    real = (t >= N_META) & (t < n_valid)
    pos = jnp.where(real, t - N_META, 0)
    row = (pos // GRID_W).astype(F32)
    col = (pos % GRID_W).astype(F32)
    axis_dim = HEAD_DIM // 2
    inv = ROPE_THETA ** (-jnp.arange(0, axis_dim, 2, dtype=F32) / axis_dim)
    ar = row[:, None] * inv[None, :]
    ac = col[:, None] * inv[None, :]
    cos = jnp.concatenate([jnp.cos(ar), jnp.cos(ar), jnp.cos(ac), jnp.cos(ac)], axis=1)
    sin = jnp.concatenate([-jnp.sin(ar), jnp.sin(ar), -jnp.sin(ac), jnp.sin(ac)], axis=1)
    return jnp.tile(cos, (1, 2)), jnp.tile(sin, (1, 2))


def head_sum_matrix():
    return jnp.kron(jnp.eye(2, dtype=F32), jnp.ones((HEAD_DIM, HEAD_DIM), F32)).astype(BF16)


def _segsum(x, e):
    hi = x.astype(BF16)
    lo = (x - hi.astype(F32)).astype(BF16)
    return _nn(hi, e) + _nn(lo, e)


def _swap_halves(x):
    lane = lax.broadcasted_iota(jnp.int32, x.shape, 1)
    quarter = HEAD_DIM // 4
    return jnp.where(lane % (2 * quarter) < quarter, pltpu.roll(x, LANES - quarter, 1), pltpu.roll(x, quarter, 1))


def qkv_forward(h, w2, gq, gk, cos, sin, e):
    lp, d = h.shape
    tm = _tm(lp)
    kw, vw = N_KB * LANES, N_KB * LANES

    def body(h_ref, w_ref, gq_ref, gk_ref, cos_ref, sin_ref, e_ref, raw_ref, q_ref, k_ref, v_ref):
        raw = _nn(h_ref[...].astype(BF16), w_ref[...])
        raw_ref[...] = raw
        c, s_, em = cos_ref[...], sin_ref[...], e_ref[...]
        for cb in range(N_QB + N_KB):
            t = raw[:, cb * LANES:(cb + 1) * LANES]
            rstd = lax.rsqrt(_segsum(t * t, em) * (1.0 / HEAD_DIM) + QK_EPS)
            n = t * rstd * (gq_ref[...] if cb < N_QB else gk_ref[...])
            rot = n * c + _swap_halves(n) * s_
            if cb < N_QB:
                q_ref[:, cb * LANES:(cb + 1) * LANES] = (rot * Q_SCALE).astype(BF16)
            else:
                k_ref[:, (cb - N_QB) * LANES:(cb - N_QB + 1) * LANES] = rot.astype(BF16)
        v_ref[...] = raw[:, (N_QB + N_KB) * LANES:].astype(BF16)

    return pl.pallas_call(
        body, name="qkv_forward", grid=(lp // tm,),
        in_specs=[_row_spec(tm, d), _full_spec((d, QKV_W)), _full_spec((1, LANES)), _full_spec((1, LANES)),
                  _row_spec(tm, LANES), _row_spec(tm, LANES), _full_spec((LANES, LANES))],
        out_specs=[_row_spec(tm, QKV_W), _row_spec(tm, N_QB * LANES), _row_spec(tm, kw), _row_spec(tm, vw)],
        out_shape=[jax.ShapeDtypeStruct((lp, QKV_W), F32), jax.ShapeDtypeStruct((lp, N_QB * LANES), BF16),
                   jax.ShapeDtypeStruct((lp, kw), BF16), jax.ShapeDtypeStruct((lp, vw), BF16)],
        compiler_params=_params(("parallel",)),
    )(h, w2, gq, gk, cos, sin, e)


def qkv_backward(dqs, dk2, dv2, raw, gq, gk, cos, sin, e):
    lp = raw.shape[0]
    tm = _tm(lp)

    def body(dq_ref, dk_ref, dv_ref, raw_ref, gq_ref, gk_ref, cos_ref, sin_ref, e_ref, d_ref, s_ref):
        @pl.when(pl.program_id(0) == 0)
        def _():
            s_ref[...] = jnp.zeros_like(s_ref)

        c, s_, em = cos_ref[...], sin_ref[...], e_ref[...]
        gsum = [jnp.zeros((1, LANES), F32), jnp.zeros((1, LANES), F32)]
        for cb in range(N_QB + N_KB):
            isq = cb < N_QB
            t = raw_ref[:, cb * LANES:(cb + 1) * LANES]
            if isq:
                drot = dq_ref[:, cb * LANES:(cb + 1) * LANES] * (HEAD_DIM ** -0.5)
            else:
                drot = dk_ref[:, (cb - N_QB) * LANES:(cb - N_QB + 1) * LANES] * math.log(2.0)
            gain = gq_ref[...] if isq else gk_ref[...]
            rstd = lax.rsqrt(_segsum(t * t, em) * (1.0 / HEAD_DIM) + QK_EPS)
            dn = drot * c + _swap_halves(drot * s_)
            xh = t * rstd
            gsum[0 if isq else 1] = gsum[0 if isq else 1] + jnp.sum(dn * xh, axis=0, keepdims=True)
            w = dn * gain
            mw = _segsum(w * xh, em) * (1.0 / HEAD_DIM)
            d_ref[:, cb * LANES:(cb + 1) * LANES] = (rstd * (w - xh * mw)).astype(BF16)
        d_ref[:, (N_QB + N_KB) * LANES:] = dv_ref[...].astype(BF16)
        s_ref[0:1, :] += gsum[0]
        s_ref[1:2, :] += gsum[1]

    kw = N_KB * LANES
    return pl.pallas_call(
        body, name="qkv_backward", grid=(lp // tm,),
        in_specs=[_row_spec(tm, N_QB * LANES), _row_spec(tm, kw), _row_spec(tm, kw), _row_spec(tm, QKV_W),
                  _full_spec((1, LANES)), _full_spec((1, LANES)), _row_spec(tm, LANES), _row_spec(tm, LANES),
                  _full_spec((LANES, LANES))],
        out_specs=[_row_spec(tm, QKV_W), _full_spec((SUBLANES, LANES))],
        out_shape=[jax.ShapeDtypeStruct((lp, QKV_W), BF16), jax.ShapeDtypeStruct((SUBLANES, LANES), F32)],
        compiler_params=_params(("arbitrary",)),
    )(dqs, dk2, dv2, raw, gq, gk, cos, sin, e)


NEG = -1e30
Q_PER_KV = N_Q_HEADS // N_KV_HEADS


def _half_masks(x):
    lane = lax.broadcasted_iota(jnp.int32, x.shape, 1)
    zero = jnp.zeros_like(x)
    return jnp.where(lane < HEAD_DIM, x, zero), jnp.where(lane >= HEAD_DIM, x, zero)


ATTN_TR = 16


def _attn_tiles(lp):
    t = _pick_tile(lp, (768, 256))
    return t, t


def attn_forward_t(qs, k2, v2t, n_valid):
    lp = qs.shape[0]
    tq, kb = _attn_tiles(lp)
    nk = lp // kb
    gw = 2 * LANES
    nr = kb // ATTN_TR
    pad0 = n_valid - (nk - 1) * kb

    def body(q_ref, k_ref, vt_ref, ot_ref, lse_ref, m_s, l_s, acc_s, s_s, p_s):
        j = pl.program_id(2)

        @pl.when(j == 0)
        def _():
            m_s[...] = jnp.full_like(m_s, NEG)
            l_s[...] = jnp.zeros_like(l_s)
            acc_s[...] = jnp.zeros_like(acc_s)

        ks = _half_masks(k_ref[...])
        for pair in range(2):
            qp = q_ref[:, pair * LANES:(pair + 1) * LANES]
            for half in range(2):
                hh = 2 * pair + half
                s_s[...] = _nt(ks[half], qp)

                if pad0 < kb:
                    @pl.when(j == nk - 1)
                    def _():
                        s_s[pad0:, :] = jnp.full((kb - pad0, tq), NEG, F32)

                def max_step(r, run):
                    rows = pl.ds(r * ATTN_TR, ATTN_TR)
                    blk = s_s[rows, :]
                    for t in range(ATTN_TR // SUBLANES):
                        run = jnp.maximum(run, blk[t * SUBLANES:(t + 1) * SUBLANES])
                    return run

                run = jnp.full((SUBLANES, tq), NEG, F32)
                for r in range(nr):
                    run = max_step(r, run)
                m_prev = m_s[hh:hh + 1, :]
                m_new = jnp.maximum(m_prev, jnp.max(run, axis=0, keepdims=True))
                alpha = jnp.exp2(m_prev - m_new)
                m_s[hh:hh + 1, :] = m_new

                def exp_step(r, run):
                    rows = pl.ds(r * ATTN_TR, ATTN_TR)
                    p = jnp.exp2(s_s[rows, :] - m_new)
                    p_s[rows, :] = p.astype(BF16)
                    for t in range(ATTN_TR // SUBLANES):
                        run = run + p[t * SUBLANES:(t + 1) * SUBLANES]
                    return run

                run = jnp.zeros((SUBLANES, tq), F32)
                for r in range(nr):
                    run = exp_step(r, run)
                l_s[hh:hh + 1, :] = alpha * l_s[hh:hh + 1, :] + jnp.sum(run, axis=0, keepdims=True)
                vt = vt_ref[half * HEAD_DIM:(half + 1) * HEAD_DIM, :]
                pv = _nn(vt, p_s[...])
                rs = slice(half * HEAD_DIM, (half + 1) * HEAD_DIM)
                acc_s[pair, rs, :] = alpha * acc_s[pair, rs, :] + pv

        @pl.when(j == nk - 1)
        def _():
            for pair in range(2):
                for half in range(2):
                    hh = 2 * pair + half
                    rs = slice(half * HEAD_DIM, (half + 1) * HEAD_DIM)
                    inv = 1.0 / l_s[hh:hh + 1, :]
                    ot_ref[pair * LANES + half * HEAD_DIM:pair * LANES + (half + 1) * HEAD_DIM, :] = (
                        acc_s[pair, rs, :] * inv).astype(BF16)
            for hh in range(Q_PER_KV):
                lse_ref[0, hh] = m_s[hh:hh + 1, :] + jnp.log2(l_s[hh:hh + 1, :])

    return pl.pallas_call(
        body, name="attn_forward", grid=(N_KV_HEADS, lp // tq, nk),
        in_specs=[pl.BlockSpec((tq, gw), lambda g, i, j: (i, g)), pl.BlockSpec((kb, LANES), lambda g, i, j: (j, g)),
                  pl.BlockSpec((LANES, kb), lambda g, i, j: (g, j))],
        out_specs=[pl.BlockSpec((gw, tq), lambda g, i, j: (g, i)),
                   pl.BlockSpec((1, Q_PER_KV, 1, tq), lambda g, i, j: (g, 0, 0, i))],
        out_shape=[jax.ShapeDtypeStruct((N_QB * LANES, lp), BF16),
                   jax.ShapeDtypeStruct((N_KV_HEADS, Q_PER_KV, 1, lp), F32)],
        scratch_shapes=[pltpu.VMEM((SUBLANES, tq), F32), pltpu.VMEM((SUBLANES, tq), F32),
                        pltpu.VMEM((2, LANES, tq), F32), pltpu.VMEM((kb, tq), F32), pltpu.VMEM((kb, tq), BF16)],
        compiler_params=_params(("parallel", "parallel", "arbitrary")),
    )(qs, k2, v2t)


def attn_backward(qs, qst, k2, v2, do, dot, lse, delta, n_valid):
    lp = qs.shape[0]
    tq, kb = _attn_tiles(lp)
    nk = lp // kb
    gw = 2 * LANES
    pad0 = n_valid - (nk - 1) * kb

    def body(q_ref, qt_ref, k_ref, v_ref, do_ref, dot_ref, lse_ref, dl_ref, dq_ref, dkt_ref, dvt_ref, acc_s):
        i = pl.program_id(1)
        j = pl.program_id(2)
        cols = pl.ds(pl.multiple_of(j * kb, kb), kb)

        @pl.when(j == 0)
        def _():
            acc_s[...] = jnp.zeros_like(acc_s)

        @pl.when(i == 0)
        def _():
            dkt_ref[0, :, cols] = jnp.zeros((LANES, kb), F32)
            dvt_ref[0, :, cols] = jnp.zeros((LANES, kb), F32)

        def step(masked):
            ks = _half_masks(k_ref[...])
            vs = _half_masks(v_ref[...])
            if masked:
                col = lax.broadcasted_iota(jnp.int32, (1, kb), 1)
                bias = jnp.where(col < pad0, 0.0, NEG)
            for pair in range(2):
                qp = q_ref[:, pair * LANES:(pair + 1) * LANES]
                dop = do_ref[:, pair * LANES:(pair + 1) * LANES]
                for half in range(2):
                    hh = 2 * pair + half
                    rs = slice(half * HEAD_DIM, (half + 1) * HEAD_DIM)
                    rt = slice(pair * LANES + half * HEAD_DIM, pair * LANES + (half + 1) * HEAD_DIM)
                    s = _nt(qp, ks[half])
                    if masked:
                        s = s + bias
                    p = jnp.exp2(s - lse_ref[0, hh])
                    dp = _nt(dop, vs[half])
                    ds = (p * (dp - dl_ref[0, hh])).astype(BF16)
                    pb = p.astype(BF16)
                    acc_s[pair] += _nn(ds, ks[half])
                    dvt_ref[0, rs, cols] += _nn(dot_ref[rt, :], pb)
                    dkt_ref[0, rs, cols] += _nn(qt_ref[rt, :], ds)

        if pad0 < kb:
            pl.when(j < nk - 1)(lambda: step(False))
            pl.when(j == nk - 1)(lambda: step(True))
        else:
            step(False)

        @pl.when(j == nk - 1)
        def _():
            for pair in range(2):
                dq_ref[:, pair * LANES:(pair + 1) * LANES] = acc_s[pair]

    cspec = pl.BlockSpec((1, Q_PER_KV, tq, 1), lambda g, i, j: (g, 0, i, 0))
    qspec = pl.BlockSpec((tq, gw), lambda g, i, j: (i, g))
    tspec = pl.BlockSpec((gw, tq), lambda g, i, j: (g, i))
    kspec = pl.BlockSpec((kb, LANES), lambda g, i, j: (j, g))
    gspec = pl.BlockSpec((1, LANES, lp), lambda g, i, j: (g, 0, 0))
    return pl.pallas_call(
        body, name="attn_backward", grid=(N_KV_HEADS, lp // tq, nk),
        in_specs=[qspec, tspec, kspec, kspec, qspec, tspec, cspec, cspec],
        out_specs=[qspec, gspec, gspec],
        out_shape=[jax.ShapeDtypeStruct((lp, N_QB * LANES), F32),
                   jax.ShapeDtypeStruct((N_KV_HEADS, LANES, lp), F32), jax.ShapeDtypeStruct((N_KV_HEADS, LANES, lp), F32)],
        scratch_shapes=[pltpu.VMEM((2, tq, LANES), F32)],
        compiler_params=_params(("parallel", "arbitrary", "arbitrary")),
    )(qs, qst, k2, v2, do, dot, lse, delta)


def attn_out_backward(dr, wout, o, e16):
    lp, d = dr.shape
    tm = _tm(lp)

    def body(dr_ref, w_ref, o_ref, e_ref, do_ref, dl_ref):
        do = _nt(dr_ref[...].astype(BF16), w_ref[...]).astype(BF16)
        do_ref[...] = do
        dl_ref[...] = _segsum(do.astype(F32) * o_ref[...].astype(F32), e_ref[...])

    rs = _row_spec(tm, d)
    return pl.pallas_call(
        body, name="attn_out_backward", grid=(lp // tm,),
        in_specs=[rs, _full_spec((d, d)), rs, _full_spec((d, N_Q_HEADS))],
        out_specs=[rs, _row_spec(tm, N_Q_HEADS)],
        out_shape=[jax.ShapeDtypeStruct((lp, d), BF16), jax.ShapeDtypeStruct((lp, N_Q_HEADS), F32)],
        compiler_params=_params(("parallel",)),
    )(dr, wout, o, e16)


N_CHIPS = 4


def _mesh_pos():
    return lax.axis_index("x"), lax.axis_index("y"), lax.axis_index("c")


def chip_exchange(arrs, scatter, name):
    n = len(arrs)
    hbm = pl.BlockSpec(memory_space=pl.ANY)

    def body(*refs):
        ins, outs = refs[:n], refs[n:2 * n]
        send_sems, recv_sems, loc_sems = refs[2 * n:]
        x, y, c = _mesh_pos()
        me = 2 * x + y
        chips = [(1 - x, y), (x, 1 - y), (1 - x, 1 - y)]
        started = []
        for a in range(n):
            loc = pltpu.make_async_copy(ins[a].at[me] if scatter else ins[a], outs[a].at[me], loc_sems.at[a])
            loc.start()
            started.append(loc)
            for k, (px, py) in enumerate(chips):
                src = ins[a].at[2 * px + py] if scatter else ins[a]
                cp = pltpu.make_async_remote_copy(
                    src_ref=src, dst_ref=outs[a].at[me], send_sem=send_sems.at[3 * a + k], recv_sem=recv_sems.at[3 * a + k],
                    device_id=(px, py, c), device_id_type=MESH)
                cp.start()
                started.append(cp)
        for cp in started:
            cp.wait()

    out_shape = [jax.ShapeDtypeStruct(a.shape if scatter else (N_CHIPS,) + a.shape, a.dtype) for a in arrs]
    return pl.pallas_call(
        body, name=name, in_specs=[hbm] * n, out_specs=[hbm] * n, out_shape=out_shape,
        scratch_shapes=[pltpu.SemaphoreType.DMA((3 * n,)), pltpu.SemaphoreType.DMA((3 * n,)), pltpu.SemaphoreType.DMA((n,))],
    )(*arrs)


def gather_two_level(arrs, name):
    n = len(arrs)
    hbm = pl.BlockSpec(memory_space=pl.ANY)

    def body(*refs):
        ins, outs = refs[:n], refs[n:2 * n]
        ici_send, ici_recv, d2d_send, d2d_recv, loc_sems = refs[2 * n:]
        x, y, c = _mesh_pos()
        me = 2 * x + y
        chips = [(1 - x, y), (x, 1 - y), (1 - x, 1 - y)]
        started = []
        for a in range(n):
            hn = arrs[a].shape[0] // 2
            mine = pl.ds(c * hn, hn)
            loc = pltpu.make_async_copy(ins[a], outs[a].at[me], loc_sems.at[a])
            loc.start()
            started.append(loc)
            first = []
            for k, (px, py) in enumerate(chips):
                cp = pltpu.make_async_remote_copy(
                    src_ref=ins[a].at[mine], dst_ref=outs[a].at[me, mine], send_sem=ici_send.at[3 * a + k],
                    recv_sem=ici_recv.at[3 * a + k], device_id=(px, py, c), device_id_type=MESH)
                cp.start()
                first.append(cp)
            for k, (px, py) in enumerate(chips):
                q = 2 * px + py
                first[k].wait_recv()
                fw = pltpu.make_async_remote_copy(
                    src_ref=outs[a].at[q, mine], dst_ref=outs[a].at[q, mine], send_sem=d2d_send.at[3 * a + k],
                    recv_sem=d2d_recv.at[3 * a + k], device_id=(x, y, 1 - c), device_id_type=MESH)
                fw.start()
                started.append(fw)
            for cp in first:
                cp.wait_send()
        for cp in started:
            cp.wait()

    out_shape = [jax.ShapeDtypeStruct((N_CHIPS,) + a.shape, a.dtype) for a in arrs]
    return pl.pallas_call(
        body, name=name, in_specs=[hbm] * n, out_specs=[hbm] * n, out_shape=out_shape,
        scratch_shapes=[pltpu.SemaphoreType.DMA((3 * n,))] * 4 + [pltpu.SemaphoreType.DMA((n,))],
    )(*arrs)


def sibling_exchange(arrs, name):
    n = len(arrs)
    hbm = pl.BlockSpec(memory_space=pl.ANY)

    def body(*refs):
        ins, outs = refs[:n], refs[n:2 * n]
        send_sems, recv_sems = refs[2 * n:]
        x, y, c = _mesh_pos()
        started = []
        for a in range(n):
            cp = pltpu.make_async_remote_copy(
                src_ref=ins[a], dst_ref=outs[a], send_sem=send_sems.at[a], recv_sem=recv_sems.at[a],
                device_id=(x, y, 1 - c), device_id_type=MESH)
            cp.start()
            started.append(cp)
        for cp in started:
            cp.wait()

    return pl.pallas_call(
        body, name=name, in_specs=[hbm] * n, out_specs=[hbm] * n,
        out_shape=[jax.ShapeDtypeStruct(a.shape, a.dtype) for a in arrs],
        scratch_shapes=[pltpu.SemaphoreType.DMA((n,)), pltpu.SemaphoreType.DMA((n,))],
    )(*arrs)


def _rows_tile(r, c):
    return _pick_tile(r, tuple(t for t in (512, 256, 128, 64, 32, 16, 8) if t * c * 4 <= 2 * 1024 * 1024))


def chip_sum(recv, name):
    _, r, c = recv.shape
    tr = _rows_tile(r, c)

    def body(r_ref, o_ref):
        acc = r_ref[0].astype(F32)
        for q in range(1, N_CHIPS):
            acc = acc + r_ref[q].astype(F32)
        o_ref[...] = acc

    return pl.pallas_call(
        body, name=name, grid=(r // tr,),
        in_specs=[pl.BlockSpec((N_CHIPS, tr, c), lambda i: (0, i, 0))],
        out_specs=pl.BlockSpec((tr, c), lambda i: (i, 0)),
        out_shape=jax.ShapeDtypeStruct((r, c), F32),
        compiler_params=_params(("parallel",)),
    )(recv)


def pair_sum(part, sib, name, dtype=F32):
    r, c = part.shape
    tr = _rows_tile(r, c)

    def body(p_ref, s_ref, o_ref):
        o_ref[...] = (p_ref[...].astype(F32) + s_ref[...].astype(F32)).astype(dtype)

    rs = pl.BlockSpec((tr, c), lambda i: (i, 0))
    return pl.pallas_call(
        body, name=name, grid=(r // tr,), in_specs=[rs] * 2, out_specs=rs,
        out_shape=jax.ShapeDtypeStruct((r, c), dtype), compiler_params=_params(("parallel",)),
    )(part, sib)


def adamw(g, w, m, v, name):
    r, c = w.shape
    tr = _rows_tile(r, c)

    def body(g_ref, w_ref, m_ref, v_ref, d_ref, nm_ref, nv_ref):
        g_ = g_ref[...]
        m_ = ADAM_B1 * m_ref[...] + (1.0 - ADAM_B1) * g_
        v_ = ADAM_B2 * v_ref[...] + (1.0 - ADAM_B2) * (g_ * g_)
        m_hat = m_ / (1.0 - ADAM_B1 ** ADAM_STEP)
        v_hat = v_ / (1.0 - ADAM_B2 ** ADAM_STEP)
        d_ref[...] = -ADAM_LR * (m_hat / (jnp.sqrt(v_hat) + ADAM_EPS) + ADAM_WD * w_ref[...])
        nm_ref[...] = m_
        nv_ref[...] = v_

    rs = pl.BlockSpec((tr, c), lambda i: (i, 0))
    return pl.pallas_call(
        body, name=name, grid=(r // tr,), in_specs=[rs] * 4, out_specs=[rs] * 3,
        out_shape=[jax.ShapeDtypeStruct((r, c), F32)] * 3,
        compiler_params=_params(("parallel",)),
    )(g, w, m, v)


WEIGHTS = ['meta_tokens', 's5_lambda_re', 's5_lambda_im', 's5_log_dt', 's5_b_re', 's5_b_im', 's5_c_re', 's5_c_im', 's5_d',
           's5_w_glu', 's5_w_out', 'attn_w_qkv', 'attn_q_gain', 'attn_k_gain', 'attn_w_out', 'ffn_w_gate', 'ffn_w_up',
           'ffn_w_down', 'ln_gain', 'ln_bias']
BIG = ['s5_w_glu', 's5_w_out', 'attn_w_qkv', 'attn_w_out', 'ffn_w_gate', 'ffn_w_up', 'ffn_w_down']
ROW_SHARDED = {'s5_w_glu', 's5_w_out', 'attn_w_out', 'ffn_w_down'}
SMALL_SHARDED = ['meta_tokens', 'ln_gain', 'ln_bias']
REPLICATED = ['s5_lambda_re', 's5_lambda_im', 's5_log_dt', 's5_b_re', 's5_b_im', 's5_c_re', 's5_c_im', 's5_d',
              'attn_q_gain', 'attn_k_gain']
REP_ALIGN = N_CHIPS * LANES * LANES


def _natural(gathered, row_sharded):
    p, n, a, b = gathered.shape
    if row_sharded:
        return jnp.transpose(gathered, (1, 0, 2, 3)).reshape(n, p * a, b)
    return jnp.transpose(gathered, (1, 2, 0, 3)).reshape(n, a, p * b)


def _shard_major(full, row_sharded):
    n, a, b = full.shape
    if row_sharded:
        return jnp.transpose(full.reshape(n, N_CHIPS, a // N_CHIPS, b), (1, 0, 2, 3))
    return jnp.transpose(full.reshape(n, a, N_CHIPS, b // N_CHIPS), (2, 0, 1, 3))


def _dup_heads(w):
    lead = w.shape[:-1]
    w = w.reshape(lead + (N_KV_HEADS, 1, HEAD_DIM))
    return jnp.broadcast_to(w, lead + (N_KV_HEADS, 2, HEAD_DIM)).reshape(lead + (N_KV_HEADS * 2 * HEAD_DIM,))


def _fold_heads(d):
    lead = d.shape[:-1]
    return d.reshape(lead + (N_KV_HEADS, 2, HEAD_DIM)).sum(axis=-2).reshape(lead + (N_KV_HEADS * HEAD_DIM,))


def _pack_rep(tree):
    flat = jnp.concatenate([tree[n].reshape(-1) for n in REPLICATED])
    pad = _round_up(flat.shape[0], REP_ALIGN) - flat.shape[0]
    return jnp.pad(flat, (0, pad))


def _unpack_rep(flat, like):
    out, off = {}, 0
    for n in REPLICATED:
        size = math.prod(like[n].shape)
        out[n] = flat[off:off + size].reshape(like[n].shape)
        off += size
    return out


def _train_step(x, loss_target, w, mom, vel):
    s = x.shape[1]
    n_valid = N_META + s
    lp = _round_up(n_valid, 2 * LANES)
    nq = N_Q_HEADS * HEAD_DIM
    nkv = N_KV_HEADS * HEAD_DIM

    small = jnp.concatenate([w[n].reshape(-1, w[n].shape[-1]) for n in SMALL_SHARDED], axis=0)
    gathered = gather_two_level([w[n].astype(BF16) for n in BIG] + [small], "gather_weights")
    full = {n: _natural(g, n in ROW_SHARDED) for n, g in zip(BIG, gathered[:-1])}
    small_full = jnp.transpose(gathered[-1], (1, 0, 2)).reshape(small.shape[0], D_MODEL)
    meta_full = small_full[:N_META]
    ln_gain = small_full[N_META:N_META + 2 * DEPTH].reshape(DEPTH, 2, 1, D_MODEL)
    ln_bias = small_full[N_META + 2 * DEPTH:].reshape(DEPTH, 2, 1, D_MODEL)
    wqkv = full['attn_w_qkv']
    w2 = jnp.concatenate([wqkv[..., :nq], _dup_heads(wqkv[..., nq:nq + nkv]), _dup_heads(wqkv[..., nq + nkv:])], axis=-1)

    cos, sin = rope_tables(lp, n_valid)
    e128 = head_sum_matrix()
    e16 = jnp.kron(jnp.eye(N_Q_HEADS, dtype=F32), jnp.ones((HEAD_DIM, 1), F32)).astype(BF16)
    gq = jnp.tile(w['attn_q_gain'], (1, 2))[:, None, :]
    gk = jnp.tile(w['attn_k_gain'], (1, 2))[:, None, :]

    pad_rows = jnp.zeros((lp - n_valid, D_MODEL), F32)
    h = jnp.concatenate([meta_full, x[0], pad_rows], axis=0)
    tgt = jnp.concatenate([jnp.zeros((N_META, D_MODEL), F32), loss_target[0], pad_rows], axis=0)

    saved = []
    s5_names = ['s5_lambda_re', 's5_lambda_im', 's5_log_dt', 's5_b_re', 's5_b_im', 's5_c_re', 's5_c_im']
    for i in range(DEPTH):
        j = i // 2
        sv = {'h': h}
        if i % 2 == 0:
            ops, sv['prep_vjp'] = jax.vjp(s5_prep, *[w[n][j] for n in s5_names])
            m_, wx_, ci_, at_ = ops
            two = lambda t: t.reshape((2 * S5_NJ,) + t.shape[2:])
            sv['ops'] = (blockdiag_expand(m_, S5_CH, S5_CH, "s5_expand_m"),
                         blockdiag_expand(two(wx_), S5_CH, S5_STATE, "s5_expand_wx").reshape(2, S5_NJ, S5_W, S5_W),
                         blockdiag_expand(two(ci_), S5_STATE, S5_CH, "s5_expand_ci").reshape(2, S5_NJ, S5_W, S5_W), at_)
            y, sv['lhs'], sv['sp'], sv['sn'] = s5_forward(h, *sv['ops'], n_valid)
            sv['v'], sv['t'], sv['g'], sv['z'] = glu_forward(y, h, w['s5_d'][j][None], full['s5_w_glu'][j])
            sv['r1'], h1 = proj_ln_forward(sv['z'], full['s5_w_out'][j], h, ln_gain[i, 0], ln_bias[i, 0], "s5_out_ln")
        else:
            sv['raw'], sv['qs'], sv['k2'], sv['v2'] = qkv_forward(h, w2[j], gq[j], gk[j], cos, sin, e128)
            ot, sv['lse'] = attn_forward_t(sv['qs'], sv['k2'], sv['v2'].T, n_valid)
            sv['o'] = ot.T
            sv['r1'], h1 = proj_ln_forward(sv['o'], full['attn_w_out'][j], h, ln_gain[i, 0], ln_bias[i, 0], "attn_out_ln")
        sv['h1'] = h1
        sv['a'], sv['b'], sv['f'] = ffn_up_forward(h1, full['ffn_w_gate'][i], full['ffn_w_up'][i])
        sv['r2'], h = proj_ln_forward(sv['f'], full['ffn_w_down'][i], h1, ln_gain[i, 1], ln_bias[i, 1], "ffn_down_ln")
        saved.append(sv)

    dh, loss_part = loss_backward(h, tgt, n_valid)
    loss = lax.psum(jnp.sum(loss_part), ("x", "y", "c"))

    gfull = {n: [None] * w[n].shape[0] for n in BIG}
    d_ln_gain = [[None, None] for _ in range(DEPTH)]
    d_ln_bias = [[None, None] for _ in range(DEPTH)]
    grep = {n: [None] * w[n].shape[0] for n in REPLICATED}
    for i in reversed(range(DEPTH)):
        j = i // 2
        sv = saved[i]
        dr2, s2 = ln_backward(dh, sv['r2'], ln_gain[i, 1])
        d_ln_gain[i][1], d_ln_bias[i][1] = s2[0], s2[1]
        da, db = ffn_backward_act(dr2, full['ffn_w_down'][i], sv['a'], sv['b'])
        gfull['ffn_w_down'][i] = mm_tn(sv['f'], dr2, "grad_ffn_down")
        dh1 = resid_nt(dr2, [da, db], [full['ffn_w_gate'][i], full['ffn_w_up'][i]], "ffn_backward_x")
        gfull['ffn_w_gate'][i] = mm_tn(sv['h1'], da, "grad_ffn_gate")
        gfull['ffn_w_up'][i] = mm_tn(sv['h1'], db, "grad_ffn_up")
        dr1, s1 = ln_backward(dh1, sv['r1'], ln_gain[i, 0])
        d_ln_gain[i][0], d_ln_bias[i][0] = s1[0], s1[1]
        if i % 2 == 0:
            dt, dgd = glu_backward1(dr1, full['s5_w_out'][j], sv['g'], sv['t'])
            gfull['s5_w_out'][j] = mm_tn(sv['z'], dr1, "grad_s5_out")
            dv, dhs, sd = glu_backward2(dt, dgd, full['s5_w_glu'][j], sv['v'], sv['h'], w['s5_d'][j][None], dr1)
            grep['s5_d'][j] = sd[0]
            gfull['s5_w_glu'][j] = mm_tn(sv['g'], dt, "grad_s5_glu")
            dh, ldy, dxf, dxr, daf, dar = s5_backward(dv, dhs, *sv['ops'], sv['sp'], sv['sn'], n_valid)
            dm = bmm_tn_compact(sv['lhs'], ldy, S5_CH, S5_CH, "grad_s5_m")
            dwx = jnp.stack([bmm_tn_compact(sv['lhs'], dxf, S5_CH, S5_STATE, "grad_s5_wxf"),
                             bmm_tn_compact(sv['lhs'], dxr, S5_CH, S5_STATE, "grad_s5_wxr")])
            dci = jnp.stack([bmm_tn_compact(sv['sp'], ldy, S5_STATE, S5_CH, "grad_s5_cif"),
                             bmm_tn_compact(sv['sn'], ldy, S5_STATE, S5_CH, "grad_s5_cir")])
            dps = sv['prep_vjp']((dm, dwx, dci, jnp.stack([daf, dar])))
            for n, g in zip(s5_names, dps):
                grep[n][j] = g
        else:
            do, delta = attn_out_backward(dr1, full['attn_w_out'][j], sv['o'], e16)
            gfull['attn_w_out'][j] = mm_tn(sv['o'], dr1, "grad_attn_out")
            heads = (N_KV_HEADS, Q_PER_KV, lp, 1)
            dq, dkt, dvt = attn_backward(sv['qs'], sv['qs'].T, sv['k2'], sv['v2'], do, do.T,
                                         sv['lse'].reshape(heads), delta.T.reshape(heads), n_valid)
            dk2 = jnp.transpose(dkt, (2, 0, 1)).reshape(lp, N_KB * LANES)
            dv2 = jnp.transpose(dvt, (2, 0, 1)).reshape(lp, N_KB * LANES)
            draw, gs = qkv_backward(dq, dk2, dv2, sv['raw'], gq[j], gk[j], cos, sin, e128)
            grep['attn_q_gain'][j] = gs[0, :HEAD_DIM] + gs[0, HEAD_DIM:]
            grep['attn_k_gain'][j] = gs[1, :HEAD_DIM] + gs[1, HEAD_DIM:]
            dh = resid_nt(dr1, [draw], [w2[j]], "attn_backward_x")
            dw2 = mm_tn(sv['h'], draw, "grad_attn_qkv")
            kq = N_QB * LANES
            kk = N_KB * LANES
            gfull['attn_w_qkv'][j] = jnp.concatenate(
                [dw2[:, :kq], _fold_heads(dw2[:, kq:kq + kk]), _fold_heads(dw2[:, kq + kk:])], axis=1)
    grad_x = dh[N_META:n_valid][None]

    core = lax.axis_index("c")
    contrib = [_shard_major(jnp.stack(gfull[n]), n in ROW_SHARDED) for n in BIG]
    small_g = jnp.concatenate([dh[:N_META], jnp.stack([g for pair in d_ln_gain for g in pair]),
                               jnp.stack([g for pair in d_ln_bias for g in pair])], axis=0)
    contrib.append(jnp.transpose(small_g.reshape(-1, N_CHIPS, D_MODEL // N_CHIPS), (1, 0, 2)))
    rep_g = _pack_rep({n: jnp.stack(grep[n]) for n in REPLICATED})
    contrib.append(rep_g.reshape(N_CHIPS, -1, LANES))
    names = BIG + ['small', 'rep']
    wire = [BF16] * len(BIG) + [F32, F32]
    keep, give = [], []
    for t, dt in zip(contrib, wire):
        hn = t.shape[1] // 2
        keep.append(lax.dynamic_slice_in_dim(t, core * hn, hn, axis=1))
        give.append(lax.dynamic_slice_in_dim(t, (1 - core) * hn, hn, axis=1).astype(dt))
    got = sibling_exchange(give, "sibling_contrib")
    two_d = lambda t: t.reshape(-1, t.shape[-1])
    pair = [pair_sum(two_d(a), two_d(b), "pair_sum_" + n, dt).reshape(a.shape)
            for n, a, b, dt in zip(names, keep, got, wire)]
    recv = chip_exchange(pair, True, "scatter_grads")
    halves = [chip_sum(r.reshape(N_CHIPS, -1, r.shape[-1]), "chip_sum_" + n) for n, r in zip(names, recv)]
    others = sibling_exchange(halves, "sibling_halves")
    grads = [jnp.where(core == 0, jnp.concatenate([a, b], axis=0), jnp.concatenate([b, a], axis=0))
             for a, b in zip(halves, others)]

    out = {}

    def update(n, g, wn, mn, vn):
        shape = wn.shape
        flat = (-1, shape[-1])
        d, nm, nv = adamw(g, wn.reshape(flat), mn.reshape(flat), vn.reshape(flat), "adamw_" + n)
        return tuple(t.reshape(shape) for t in (g, d, nm, nv))

    for n, g in zip(BIG, grads):
        out[n] = update(n, g, w[n], mom[n], vel[n])
    cat = lambda tree: jnp.concatenate([tree[n].reshape(-1, tree[n].shape[-1]) for n in SMALL_SHARDED], axis=0)
    sm = update("small", grads[-2], cat(w), cat(mom), cat(vel))
    off = 0
    for n in SMALL_SHARDED:
        rows = math.prod(w[n].shape[:-1])
        out[n] = tuple(t[off:off + rows].reshape(w[n].shape) for t in sm)
        off += rows
    rep_all = chip_exchange([grads[-1]], False, "gather_rep")[0].reshape(-1, LANES)
    rp = update("rep", rep_all, _pack_rep(w).reshape(-1, LANES), _pack_rep(mom).reshape(-1, LANES),
                _pack_rep(vel).reshape(-1, LANES))
    unpacked = [_unpack_rep(t.reshape(-1), w) for t in rp]
    for n in REPLICATED:
        out[n] = tuple(u[n] for u in unpacked)

    return (loss, grad_x, *[out[n][0] for n in WEIGHTS], *[out[n][1] for n in WEIGHTS],
            *[out[n][2] for n in WEIGHTS], *[out[n][3] for n in WEIGHTS])


def kernel(x, meta_tokens, s5_lambda_re, s5_lambda_im, s5_log_dt, s5_b_re, s5_b_im, s5_c_re, s5_c_im, s5_d, s5_w_glu, s5_w_out, attn_w_qkv, attn_q_gain, attn_k_gain, attn_w_out, ffn_w_gate, ffn_w_up, ffn_w_down, ln_gain, ln_bias, loss_target, m_meta_tokens, m_s5_lambda_re, m_s5_lambda_im, m_s5_log_dt, m_s5_b_re, m_s5_b_im, m_s5_c_re, m_s5_c_im, m_s5_d, m_s5_w_glu, m_s5_w_out, m_attn_w_qkv, m_attn_q_gain, m_attn_k_gain, m_attn_w_out, m_ffn_w_gate, m_ffn_w_up, m_ffn_w_down, m_ln_gain, m_ln_bias, v_meta_tokens, v_s5_lambda_re, v_s5_lambda_im, v_s5_log_dt, v_s5_b_re, v_s5_b_im, v_s5_c_re, v_s5_c_im, v_s5_d, v_s5_w_glu, v_s5_w_out, v_attn_w_qkv, v_attn_q_gain, v_attn_k_gain, v_attn_w_out, v_ffn_w_gate, v_ffn_w_up, v_ffn_w_down, v_ln_gain, v_ln_bias):
    given = locals()
    w = {n: given[n] for n in WEIGHTS}
    mom = {n: given["m_" + n] for n in WEIGHTS}
    vel = {n: given["v_" + n] for n in WEIGHTS}
    return _train_step(x, loss_target, w, mom, vel)
```

```python
import math

import jax
import jax.numpy as jnp
from jax import lax
from jax.experimental import pallas as pl
from jax.experimental.pallas import tpu as pltpu

F32 = jnp.float32
BF16 = jnp.bfloat16
MESH = pl.DeviceIdType.MESH

D_MODEL = 1024
N_META = 16
GRID_W = 64
HEAD_DIM = 64
N_Q_HEADS = 16
N_KV_HEADS = 4
ROPE_THETA = 10000.0
QK_EPS = 1e-6
S5_CH = 16
S5_GROUPS = 64
S5_STATE = 64
D_FF = 2816
LN_EPS = 1e-5
DEPTH = 4
ALPHA = (2.0 * DEPTH) ** 0.25
ADAM_LR, ADAM_B1, ADAM_B2, ADAM_EPS, ADAM_WD, ADAM_STEP = 0.001, 0.9, 0.999, 1e-08, 0.01, 10

LANES = 128
SUBLANES = 8
VMEM_LIMIT = 56 * 1024 * 1024

S5_T = 8
S5_GB = LANES // S5_CH
S5_NJ = S5_GROUPS // S5_GB
S5_W = S5_T * LANES
S5_SW = 2 * S5_GB * S5_STATE
S5_HALF = S5_SW // 2


def _round_up(a, b):
    return -(-a // b) * b


def _pick_tile(n, prefs):
    for t in prefs:
        if n % t == 0:
            return t
    return n


def _params(sem=None):
    kw = dict(vmem_limit_bytes=VMEM_LIMIT)
    if sem is not None:
        kw["dimension_semantics"] = sem
    return pltpu.CompilerParams(**kw)


def _dot(a, b, dims):
    return lax.dot_general(a, b, (dims, ((), ())), preferred_element_type=F32)


def _nn(a, b):
    return _dot(a, b, ((1,), (0,)))


def _nt(a, b):
    return _dot(a, b, ((1,), (1,)))


def _tn(a, b):
    return _dot(a, b, ((0,), (0,)))


def _compact(w):
    g, a0, a1, b0, b1 = w.shape
    w = jnp.transpose(w.reshape(S5_NJ, S5_GB, a0, a1, b0, b1), (0, 2, 1, 3, 4, 5))
    return w.reshape(S5_NJ, a0 * S5_GB * a1, b0 * b1)


def s5_prep(lam_re, lam_im, log_dt, b_re, b_im, c_re, c_im):
    hi = lax.Precision.HIGHEST
    t = S5_T
    dt = jnp.exp(log_dt)[..., None]
    taus = jnp.arange(t + 1, dtype=F32)[:, None, None, None]
    mag = jnp.exp(lam_re * dt)
    ang = lam_im * dt
    pr = jnp.concatenate([jnp.ones_like(mag)[None], (mag * jnp.cos(ang))[None],
                          jnp.exp(lam_re * dt * taus[2:]) * jnp.cos(ang * taus[2:])], axis=0)
    pi = jnp.concatenate([jnp.zeros_like(mag)[None], (mag * jnp.sin(ang))[None],
                          jnp.exp(lam_re * dt * taus[2:]) * jnp.sin(ang * taus[2:])], axis=0)
    abr, abi = pr[1], pi[1]
    nr, ni = abr - 1.0, abi
    den = lam_re * lam_re + lam_im * lam_im
    cr = (nr * lam_re + ni * lam_im) / den
    ci_ = (ni * lam_re - nr * lam_im) / den
    bbr = cr[..., None] * b_re - ci_[..., None] * b_im
    bbi = cr[..., None] * b_im + ci_[..., None] * b_re
    er = c_re[None] * pr[:, :, :, None, :] - c_im[None] * pi[:, :, :, None, :]
    ei = c_re[None] * pi[:, :, :, None, :] + c_im[None] * pr[:, :, :, None, :]
    kk = (jnp.einsum("tdgop,dgpc->tdgoc", er[:t], bbr, precision=hi)
          - jnp.einsum("tdgop,dgpc->tdgoc", ei[:t], bbi, precision=hi))
    zero = jnp.zeros_like(kk[0, 0])
    mg = jnp.stack([jnp.stack([(kk[i - s, 0] if i > s else zero) + (kk[s - i, 1] if s > i else zero)
                               + ((kk[0, 0] + kk[0, 1]) if i == s else zero) for i in range(t)])
                    for s in range(t)])
    mg = jnp.transpose(mg, (2, 0, 4, 1, 3))
    m = _compact(mg)
    pw_f = jnp.stack([pr[t - 1 - s, 0] for s in range(t)]), jnp.stack([pi[t - 1 - s, 0] for s in range(t)])
    pw_r = jnp.stack([pr[s, 1] for s in range(t)]), jnp.stack([pi[s, 1] for s in range(t)])
    wx = []
    for d, (qr, qi) in enumerate((pw_f, pw_r)):
        wr = qr[..., None] * bbr[d][None] - qi[..., None] * bbi[d][None]
        wi = qr[..., None] * bbi[d][None] + qi[..., None] * bbr[d][None]
        w = jnp.stack([wr, wi], axis=0)
        w = jnp.transpose(w, (2, 1, 4, 0, 3))
        wx.append(_compact(w))
    ci = []
    for d in range(2):
        exps = [i + 1 for i in range(t)] if d == 0 else [t - i for i in range(t)]
        e_r = jnp.stack([er[e, d] for e in exps])
        e_i = jnp.stack([ei[e, d] for e in exps])
        w = jnp.stack([e_r, -e_i], axis=0)
        w = jnp.transpose(w, (2, 0, 4, 1, 3))
        ci.append(_compact(w))
    at = jnp.stack([pr[t], pi[t]], axis=1)
    at = at.reshape(2, 2, S5_NJ, S5_GB * S5_STATE)
    at = jnp.transpose(at, (0, 2, 1, 3)).reshape(2, S5_NJ, 1, S5_SW)
    return m, jnp.stack(wx), jnp.stack(ci), at


def _chunk_rows(ref, nc):
    return jnp.concatenate([ref[pl.ds(s, nc, stride=S5_T), :] for s in range(S5_T)], axis=1)


def _cmul(ar, ai, sr, si):
    return ar * sr - ai * si, ar * si + ai * sr


def _scan_tiles(nc, reverse, step):
    nt = nc // SUBLANES

    def body(it, carry):
        tix = (nt - 1 - it) if reverse else it
        k0 = pl.multiple_of(tix * SUBLANES, SUBLANES)
        return step(k0, carry)

    return body, nt


def _s5_specs(nc):
    hbm = pl.BlockSpec(memory_space=pl.ANY)
    aspec = pl.BlockSpec((1, 1, S5_SW), lambda j: (j, 0, 0))
    cspec = pl.BlockSpec((1, nc, S5_W), lambda j: (j, 0, 0))
    return hbm, aspec, cspec


def _s5_fetch(j, tok_hbm, w_hbms, tok_s, w_s, sems):
    cols = pl.ds(pl.multiple_of(j * LANES, LANES), LANES)
    cps = [pltpu.make_async_copy(tok_hbm.at[:, cols], tok_s, sems.at[0])]
    for i, w in enumerate(w_hbms):
        cps.append(pltpu.make_async_copy(w.at[j], w_s.at[i], sems.at[1 + i]))
    for cp in cps:
        cp.start()
    return cols, cps


def s5_forward(u, m, wx, ci, at, n_valid):
    lp = u.shape[0]
    nc = lp // S5_T
    nvc = n_valid // S5_T

    def body(u_hbm, m_hbm, wxf_hbm, wxr_hbm, cif_hbm, cir_hbm, atf_ref, atr_ref,
             y_hbm, lhs_ref, sp_ref, sn_ref, tok_s, w_s, xf_s, xr_s, sems):
        j = pl.program_id(0)
        cols, cps = _s5_fetch(j, u_hbm, (m_hbm, wxf_hbm, wxr_hbm, cif_hbm, cir_hbm), tok_s, w_s, sems)
        cps[0].wait()
        lhs = _chunk_rows(tok_s, nc)
        rows = lax.broadcasted_iota(jnp.int32, lhs.shape, 0)
        lhs = jnp.where(rows < nvc, lhs, 0.0).astype(BF16)
        lhs_ref[0] = lhs
        cps[2].wait()
        cps[3].wait()
        xf_s[...] = _nn(lhs, w_s[1])
        xr_s[...] = _nn(lhs, w_s[2])
        afr, afi = atf_ref[0, :, :S5_HALF], atf_ref[0, :, S5_HALF:]
        arr, ari = atr_ref[0, :, :S5_HALF], atr_ref[0, :, S5_HALF:]

        def scan_step(x_s, ar, ai, descending):
            def step(k0, carry):
                sr, si = carry
                x = x_s[pl.ds(k0, SUBLANES), :]
                outs = [None] * SUBLANES
                order = reversed(range(SUBLANES)) if descending else range(SUBLANES)
                for r in order:
                    outs[r] = jnp.concatenate([sr, si], axis=1)
                    nr, ni = _cmul(ar, ai, sr, si)
                    sr = nr + x[r:r + 1, :S5_HALF]
                    si = ni + x[r:r + 1, S5_HALF:]
                x_s[pl.ds(k0, SUBLANES), :] = jnp.concatenate(outs, axis=0)
                return sr, si
            return step

        zero = jnp.zeros((1, S5_HALF), F32)
        fb, nt = _scan_tiles(nc, False, scan_step(xf_s, afr, afi, False))
        lax.fori_loop(0, nt, fb, (zero, zero))
        rb, nt = _scan_tiles(nc, True, scan_step(xr_s, arr, ari, True))
        lax.fori_loop(0, nt, rb, (zero, zero))
        sp = xf_s[...].astype(BF16)
        sn = xr_s[...].astype(BF16)
        sp_ref[0] = sp
        sn_ref[0] = sn
        cps[1].wait()
        cps[4].wait()
        cps[5].wait()
        y = _nn(lhs, w_s[0]) + _nn(sp, w_s[3]) + _nn(sn, w_s[4])
        for i in range(S5_T):
            tok_s[pl.ds(i, nc, stride=S5_T), :] = y[:, i * LANES:(i + 1) * LANES]
        pltpu.sync_copy(tok_s, y_hbm.at[:, cols])

    hbm, aspec, cspec = _s5_specs(nc)
    return pl.pallas_call(
        body, name="s5_forward", grid=(S5_NJ,),
        in_specs=[hbm] * 6 + [aspec, aspec],
        out_specs=[hbm, cspec, cspec, cspec],
        out_shape=[jax.ShapeDtypeStruct((lp, D_MODEL), F32)] + [jax.ShapeDtypeStruct((S5_NJ, nc, S5_W), BF16)] * 3,
        scratch_shapes=[pltpu.VMEM((lp, LANES), F32), pltpu.VMEM((5, S5_W, S5_W), BF16),
                        pltpu.VMEM((nc, S5_SW), F32), pltpu.VMEM((nc, S5_SW), F32), pltpu.SemaphoreType.DMA((6,))],
        compiler_params=_params(("arbitrary",)),
    )(u, m, wx[0], wx[1], ci[0], ci[1], at[0], at[1])


def s5_backward(dy, dhs, m, wx, ci, at, sp, sn, n_valid):
    lp = dy.shape[0]
    nc = lp // S5_T
    nvc = n_valid // S5_T

    def body(dy_hbm, dhs_hbm, m_hbm, wxf_hbm, wxr_hbm, cif_hbm, cir_hbm, atf_ref, atr_ref, sp_ref, sn_ref,
             dh_hbm, ldy_ref, dxf_ref, dxr_ref, daf_ref, dar_ref, tok_s, w_s, gf_s, gr_s, sems):
        j = pl.program_id(0)
        cols, cps = _s5_fetch(j, dy_hbm, (m_hbm, wxf_hbm, wxr_hbm, cif_hbm, cir_hbm), tok_s, w_s, sems)
        cps[0].wait()
        ldy = _chunk_rows(tok_s, nc)
        rows = lax.broadcasted_iota(jnp.int32, ldy.shape, 0)
        ldy = jnp.where(rows < nvc, ldy, 0.0).astype(BF16)
        ldy_ref[0] = ldy
        resid = pltpu.make_async_copy(dhs_hbm.at[:, cols], tok_s, sems.at[0])
        resid.start()
        cps[4].wait()
        cps[5].wait()
        gf_s[...] = _nt(ldy, w_s[3])
        gr_s[...] = _nt(ldy, w_s[4])
        afr, afi = atf_ref[0, :, :S5_HALF], atf_ref[0, :, S5_HALF:]
        arr, ari = atr_ref[0, :, :S5_HALF], atr_ref[0, :, S5_HALF:]

        def adj_step(g_s, s_ref, ar, ai, descending):
            def step(k0, carry):
                gr_, gi_, dr_, di_ = carry
                g = g_s[pl.ds(k0, SUBLANES), :]
                p = s_ref[0, pl.ds(k0, SUBLANES), :].astype(F32)
                outs = [None] * SUBLANES
                order = reversed(range(SUBLANES)) if descending else range(SUBLANES)
                for r in order:
                    outs[r] = jnp.concatenate([gr_, gi_], axis=1)
                    pr_, pi_ = p[r:r + 1, :S5_HALF], p[r:r + 1, S5_HALF:]
                    dr_ = dr_ + gr_ * pr_ + gi_ * pi_
                    di_ = di_ + gi_ * pr_ - gr_ * pi_
                    nr, ni = _cmul(ar, -ai, gr_, gi_)
                    gr_ = nr + g[r:r + 1, :S5_HALF]
                    gi_ = ni + g[r:r + 1, S5_HALF:]
                g_s[pl.ds(k0, SUBLANES), :] = jnp.concatenate(outs, axis=0)
                return gr_, gi_, dr_, di_
            return step

        zero = jnp.zeros((1, S5_HALF), F32)
        fb, nt = _scan_tiles(nc, True, adj_step(gf_s, sp_ref, afr, afi, True))
        _, _, dr_, di_ = lax.fori_loop(0, nt, fb, (zero,) * 4)
        daf_ref[0] = jnp.concatenate([dr_, di_], axis=1)
        rb, nt = _scan_tiles(nc, False, adj_step(gr_s, sn_ref, arr, ari, False))
        _, _, dr_, di_ = lax.fori_loop(0, nt, rb, (zero,) * 4)
        dar_ref[0] = jnp.concatenate([dr_, di_], axis=1)
        dxf = gf_s[...].astype(BF16)
        dxr = gr_s[...].astype(BF16)
        dxf_ref[0] = dxf
        dxr_ref[0] = dxr
        cps[1].wait()
        cps[2].wait()
        cps[3].wait()
        du = _nt(ldy, w_s[0]) + _nt(dxf, w_s[1]) + _nt(dxr, w_s[2])
        rows = lax.broadcasted_iota(jnp.int32, du.shape, 0)
        du = jnp.where(rows < nvc, du, 0.0)
        resid.wait()
        for s in range(S5_T):
            tok_s[pl.ds(s, nc, stride=S5_T), :] += du[:, s * LANES:(s + 1) * LANES]
        pltpu.sync_copy(tok_s, dh_hbm.at[:, cols])

    hbm, aspec, cspec = _s5_specs(nc)
    return pl.pallas_call(
        body, name="s5_backward", grid=(S5_NJ,),
        in_specs=[hbm] * 7 + [aspec, aspec, cspec, cspec],
        out_specs=[hbm, cspec, cspec, cspec, aspec, aspec],
        out_shape=[jax.ShapeDtypeStruct((lp, D_MODEL), F32)] + [jax.ShapeDtypeStruct((S5_NJ, nc, S5_W), BF16)] * 3
        + [jax.ShapeDtypeStruct((S5_NJ, 1, S5_SW), F32)] * 2,
        scratch_shapes=[pltpu.VMEM((lp, LANES), F32), pltpu.VMEM((5, S5_W, S5_W), BF16),
                        pltpu.VMEM((nc, S5_SW), F32), pltpu.VMEM((nc, S5_SW), F32), pltpu.SemaphoreType.DMA((6,))],
        compiler_params=_params(("arbitrary",)),
    )(dy, dhs, m, wx[0], wx[1], ci[0], ci[1], at[0], at[1], sp, sn)


S5_CW = LANES


def _replicate_matrix(b1):
    b0n = S5_CW // b1
    eye0 = jnp.eye(b0n, dtype=F32)
    eye1 = jnp.eye(b1, dtype=F32)
    r = jnp.einsum("ab,cd->acbd", eye0, eye1)[:, :, :, None, :]
    r = jnp.broadcast_to(r, (b0n, b1, b0n, S5_GB, b1))
    return r.reshape(S5_CW, b0n * S5_GB * b1).astype(BF16)


def _same_group(a1, b1):
    rg = (lax.broadcasted_iota(jnp.int32, (S5_W, S5_W), 0) // a1) % S5_GB
    cg = (lax.broadcasted_iota(jnp.int32, (S5_W, S5_W), 1) // b1) % S5_GB
    return rg == cg


def blockdiag_expand(compact, a1, b1, name):
    nj = compact.shape[0]

    def body(c_ref, r_ref, o_ref):
        rep = _nn(c_ref[0].astype(BF16), r_ref[...])
        o_ref[0] = jnp.where(_same_group(a1, b1), rep, 0.0).astype(BF16)

    return pl.pallas_call(
        body, name=name, grid=(nj,),
        in_specs=[pl.BlockSpec((1, S5_W, S5_CW), lambda j: (j, 0, 0)), _full_spec((S5_CW, S5_W))],
        out_specs=pl.BlockSpec((1, S5_W, S5_W), lambda j: (j, 0, 0)),
        out_shape=jax.ShapeDtypeStruct((nj, S5_W, S5_W), BF16),
        compiler_params=_params(("parallel",)),
    )(compact, _replicate_matrix(b1))


def bmm_tn_compact(a, b, a1, b1, name):
    nj, k, wa = a.shape
    wb = b.shape[2]

    def body(a_ref, b_ref, r_ref, o_ref):
        prod = jnp.where(_same_group(a1, b1), _tn(a_ref[0], b_ref[0]), 0.0)
        hi = prod.astype(BF16)
        lo = (prod - hi.astype(F32)).astype(BF16)
        o_ref[0] = _nt(hi, r_ref[...]) + _nt(lo, r_ref[...])

    return pl.pallas_call(
        body, name=name, grid=(nj,),
        in_specs=[pl.BlockSpec((1, k, wa), lambda j: (j, 0, 0)), pl.BlockSpec((1, k, wb), lambda j: (j, 0, 0)),
                  _full_spec((S5_CW, S5_W))],
        out_specs=pl.BlockSpec((1, wa, S5_CW), lambda j: (j, 0, 0)),
        out_shape=jax.ShapeDtypeStruct((nj, wa, S5_CW), F32),
        compiler_params=_params(("parallel",)),
    )(a, b, _replicate_matrix(b1))


def _tm(lp):
    return _pick_tile(lp, (768, 256))


def _row_spec(tm, width):
    return pl.BlockSpec((tm, width), lambda i: (i, 0))


def _full_spec(shape):
    return pl.BlockSpec(shape, lambda *_: (0,) * len(shape))


def _gelu(v):
    return 0.5 * v * (1.0 + lax.erf(v * (2.0 ** -0.5)))


def _gelu_grad(v):
    return 0.5 * (1.0 + lax.erf(v * (2.0 ** -0.5))) + v * jnp.exp(-0.5 * v * v) * (2.0 * math.pi) ** -0.5


def _layer_norm(r, gain, bias):
    mean = jnp.mean(r, axis=-1, keepdims=True)
    c = r - mean
    var = jnp.mean(c * c, axis=-1, keepdims=True)
    return c * lax.rsqrt(var + LN_EPS) * gain + bias


def glu_forward(y, h, dvec, wglu):
    lp, d = y.shape
    tm = _tm(lp)

    def body(y_ref, h_ref, d_ref, w_ref, v_ref, t_ref, g_ref, z_ref):
        v = y_ref[...] + d_ref[...] * h_ref[...]
        g = _gelu(v)
        gb = g.astype(BF16)
        t = _nn(gb, w_ref[...])
        v_ref[...] = v
        t_ref[...] = t
        g_ref[...] = gb
        z_ref[...] = (g * jax.nn.sigmoid(t)).astype(BF16)

    rs = _row_spec(tm, d)
    return pl.pallas_call(
        body, name="glu_forward", grid=(lp // tm,),
        in_specs=[rs, rs, _full_spec((1, d)), _full_spec((d, d))],
        out_specs=[rs, rs, rs, rs],
        out_shape=[jax.ShapeDtypeStruct((lp, d), F32)] * 2 + [jax.ShapeDtypeStruct((lp, d), BF16)] * 2,
        compiler_params=_params(("parallel",)),
    )(y, h, dvec, wglu)


def proj_ln_forward(z, w, h, gain, bias, name):
    lp, k = z.shape
    d = w.shape[1]
    tm = _tm(lp)

    def body(z_ref, w_ref, h_ref, g_ref, b_ref, r_ref, o_ref):
        r = ALPHA * h_ref[...] + _nn(z_ref[...], w_ref[...])
        r_ref[...] = r
        o_ref[...] = _layer_norm(r, g_ref[...], b_ref[...])

    rs = _row_spec(tm, d)
    return pl.pallas_call(
        body, name=name, grid=(lp // tm,),
        in_specs=[_row_spec(tm, k), _full_spec((k, d)), rs, _full_spec((1, d)), _full_spec((1, d))],
        out_specs=[rs, rs],
        out_shape=[jax.ShapeDtypeStruct((lp, d), F32)] * 2,
        compiler_params=_params(("parallel",)),
    )(z, w, h, gain, bias)


FFN_NB = 1408


def ffn_up_forward(h, wg, wu):
    lp, d = h.shape
    dff = wg.shape[1]
    tm = _tm(lp)

    def body(h_ref, wg_ref, wu_ref, a_ref, b_ref, f_ref):
        hb = h_ref[...].astype(BF16)
        a = _nn(hb, wg_ref[...])
        b = _nn(hb, wu_ref[...])
        a_ref[...] = a.astype(BF16)
        b_ref[...] = b.astype(BF16)
        f_ref[...] = (a * jax.nn.sigmoid(a) * b).astype(BF16)

    ws = pl.BlockSpec((d, FFN_NB), lambda n, i: (0, n))
    os_ = pl.BlockSpec((tm, FFN_NB), lambda n, i: (i, n))
    return pl.pallas_call(
        body, name="ffn_up_forward", grid=(dff // FFN_NB, lp // tm),
        in_specs=[pl.BlockSpec((tm, d), lambda n, i: (i, 0)), ws, ws],
        out_specs=[os_, os_, os_],
        out_shape=[jax.ShapeDtypeStruct((lp, dff), BF16)] * 3,
        compiler_params=_params(("parallel", "parallel")),
    )(h, wg, wu)


def ln_backward(dh, r, gain):
    lp, d = dh.shape
    tm = _tm(lp)

    def body(dh_ref, r_ref, g_ref, dr_ref, s_ref):
        r_ = r_ref[...]
        dh_ = dh_ref[...]
        mean = jnp.mean(r_, axis=-1, keepdims=True)
        c = r_ - mean
        var = jnp.mean(c * c, axis=-1, keepdims=True)
        rstd = lax.rsqrt(var + LN_EPS)
        xh = c * rstd
        dxh = dh_ * g_ref[...]
        m1 = jnp.mean(dxh, axis=-1, keepdims=True)
        m2 = jnp.mean(dxh * xh, axis=-1, keepdims=True)
        dr_ref[...] = rstd * (dxh - m1 - xh * m2)

        @pl.when(pl.program_id(0) == 0)
        def _():
            s_ref[...] = jnp.zeros_like(s_ref)

        s_ref[0:1, :] += jnp.sum(dh_ * xh, axis=0, keepdims=True)
        s_ref[1:2, :] += jnp.sum(dh_, axis=0, keepdims=True)

    rs = _row_spec(tm, d)
    return pl.pallas_call(
        body, name="ln_backward", grid=(lp // tm,),
        in_specs=[rs, rs, _full_spec((1, d))],
        out_specs=[rs, _full_spec((SUBLANES, d))],
        out_shape=[jax.ShapeDtypeStruct((lp, d), F32), jax.ShapeDtypeStruct((SUBLANES, d), F32)],
        compiler_params=_params(("arbitrary",)),
    )(dh, r, gain)


def ffn_backward_act(dr, wd, a, b):
    lp, d = dr.shape
    dff = wd.shape[0]
    tm = _tm(lp)

    def body(dr_ref, wd_ref, a_ref, b_ref, da_ref, db_ref):
        df = _nt(dr_ref[...].astype(BF16), wd_ref[...])
        a_ = a_ref[...].astype(F32)
        b_ = b_ref[...].astype(F32)
        sg = jax.nn.sigmoid(a_)
        da_ref[...] = (df * b_ * sg * (1.0 + a_ * (1.0 - sg))).astype(BF16)
        db_ref[...] = (df * a_ * sg).astype(BF16)

    os_ = pl.BlockSpec((tm, FFN_NB), lambda n, i: (i, n))
    return pl.pallas_call(
        body, name="ffn_backward_act", grid=(dff // FFN_NB, lp // tm),
        in_specs=[pl.BlockSpec((tm, d), lambda n, i: (i, 0)), pl.BlockSpec((FFN_NB, d), lambda n, i: (n, 0)), os_, os_],
        out_specs=[os_, os_],
        out_shape=[jax.ShapeDtypeStruct((lp, dff), BF16)] * 2,
        compiler_params=_params(("parallel", "parallel")),
    )(dr, wd, a, b)


def resid_nt(dr, xs, ws, name):
    lp, d = dr.shape
    tm = _tm(lp)
    n = len(xs)

    def body(*refs):
        acc = ALPHA * refs[0][...]
        for i in range(n):
            acc = acc + _nt(refs[1 + i][...], refs[1 + n + i][...])
        refs[-1][...] = acc

    rs = _row_spec(tm, d)
    in_specs = [rs] + [_row_spec(tm, x.shape[1]) for x in xs] + [_full_spec(w.shape) for w in ws]
    return pl.pallas_call(
        body, name=name, grid=(lp // tm,),
        in_specs=in_specs, out_specs=rs,
        out_shape=jax.ShapeDtypeStruct((lp, d), F32),
        compiler_params=_params(("parallel",)),
    )(dr, *xs, *ws)


def mm_tn(x, y, name):
    lp, k = x.shape
    n = y.shape[1]
    tm = _tm(lp)
    nb = _pick_tile(n, (512, 1408))

    def body(x_ref, y_ref, o_ref):
        @pl.when(pl.program_id(1) == 0)
        def _():
            o_ref[...] = jnp.zeros_like(o_ref)

        o_ref[...] += _tn(x_ref[...].astype(BF16), y_ref[...].astype(BF16))

    return pl.pallas_call(
        body, name=name, grid=(n // nb, lp // tm),
        in_specs=[pl.BlockSpec((tm, k), lambda j, i: (i, 0)), pl.BlockSpec((tm, nb), lambda j, i: (i, j))],
        out_specs=pl.BlockSpec((k, nb), lambda j, i: (0, j)),
        out_shape=jax.ShapeDtypeStruct((k, n), F32),
        compiler_params=_params(("parallel", "arbitrary")),
    )(x, y)


def glu_backward1(dr, wout, g, t):
    lp, d = dr.shape
    tm = _tm(lp)

    def body(dr_ref, w_ref, g_ref, t_ref, dt_ref, dgd_ref):
        dz = _nt(dr_ref[...].astype(BF16), w_ref[...])
        s = jax.nn.sigmoid(t_ref[...])
        dgd_ref[...] = dz * s
        dt_ref[...] = (dz * g_ref[...].astype(F32) * s * (1.0 - s)).astype(BF16)

    rs = _row_spec(tm, d)
    return pl.pallas_call(
        body, name="glu_backward1", grid=(lp // tm,),
        in_specs=[rs, _full_spec((d, d)), rs, rs],
        out_specs=[rs, rs],
        out_shape=[jax.ShapeDtypeStruct((lp, d), BF16), jax.ShapeDtypeStruct((lp, d), F32)],
        compiler_params=_params(("parallel",)),
    )(dr, wout, g, t)


def glu_backward2(dt, dgd, wglu, v, h, dvec, dr):
    lp, d = dt.shape
    tm = _tm(lp)

    def body(dt_ref, dgd_ref, w_ref, v_ref, h_ref, d_ref, dr_ref, dv_ref, dhs_ref, s_ref):
        dg = dgd_ref[...] + _nt(dt_ref[...], w_ref[...])
        dv = dg * _gelu_grad(v_ref[...])
        dv_ref[...] = dv
        dhs_ref[...] = ALPHA * dr_ref[...] + dv * d_ref[...]

        @pl.when(pl.program_id(0) == 0)
        def _():
            s_ref[...] = jnp.zeros_like(s_ref)

        s_ref[0:1, :] += jnp.sum(dv * h_ref[...], axis=0, keepdims=True)

    rs = _row_spec(tm, d)
    return pl.pallas_call(
        body, name="glu_backward2", grid=(lp // tm,),
        in_specs=[rs, rs, _full_spec((d, d)), rs, rs, _full_spec((1, d)), rs],
        out_specs=[rs, rs, _full_spec((SUBLANES, d))],
        out_shape=[jax.ShapeDtypeStruct((lp, d), F32)] * 2 + [jax.ShapeDtypeStruct((SUBLANES, d), F32)],
        compiler_params=_params(("arbitrary",)),
    )(dt, dgd, wglu, v, h, dvec, dr)


def loss_backward(hf, tgt, n_valid):
    lp, d = hf.shape
    tm = _tm(lp)

    def body(h_ref, t_ref, dh_ref, s_ref):
        rows = pl.program_id(0) * tm + lax.broadcasted_iota(jnp.int32, (tm, d), 0)
        ok = (rows >= N_META) & (rows < n_valid)
        e = jnp.where(ok, h_ref[...] - t_ref[...], 0.0)
        dh_ref[...] = e * (1.0 / d)

        @pl.when(pl.program_id(0) == 0)
        def _():
            s_ref[...] = jnp.zeros_like(s_ref)

        sq = e * e
        part = sq[:, 0:LANES]
        for c in range(1, d // LANES):
            part = part + sq[:, c * LANES:(c + 1) * LANES]
        acc = part[0:SUBLANES]
        for r in range(1, tm // SUBLANES):
            acc = acc + part[r * SUBLANES:(r + 1) * SUBLANES]
        s_ref[...] += acc * (0.5 / d)

    rs = _row_spec(tm, d)
    return pl.pallas_call(
        body, name="loss_backward", grid=(lp // tm,),
        in_specs=[rs, rs], out_specs=[rs, _full_spec((SUBLANES, LANES))],
        out_shape=[jax.ShapeDtypeStruct((lp, d), F32), jax.ShapeDtypeStruct((SUBLANES, LANES), F32)],
        compiler_params=_params(("arbitrary",)),
    )(hf, tgt)


N_QB = N_Q_HEADS // 2
N_KB = N_KV_HEADS
QKV_W = (N_QB + 2 * N_KB) * LANES
Q_SCALE = HEAD_DIM ** -0.5 * math.log2(math.e)


def rope_tables(lp, n_valid):
    t = jnp.arange(lp, dtype=jnp.int32)
    real = (t >= N_META) & (t < n_valid)
    pos = jnp.where(real, t - N_META, 0)
    row = (pos // GRID_W).astype(F32)
    col = (pos % GRID_W).astype(F32)
    axis_dim = HEAD_DIM // 2
    inv = ROPE_THETA ** (-jnp.arange(0, axis_dim, 2, dtype=F32) / axis_dim)
    ar = row[:, None] * inv[None, :]
    ac = col[:, None] * inv[None, :]
    cos = jnp.concatenate([jnp.cos(ar), jnp.cos(ar), jnp.cos(ac), jnp.cos(ac)], axis=1)
    sin = jnp.concatenate([-jnp.sin(ar), jnp.sin(ar), -jnp.sin(ac), jnp.sin(ac)], axis=1)
    return jnp.tile(cos, (1, 2)), jnp.tile(sin, (1, 2))


def head_sum_matrix():
    return jnp.kron(jnp.eye(2, dtype=F32), jnp.ones((HEAD_DIM, HEAD_DIM), F32)).astype(BF16)


def _segsum(x, e):
    hi = x.astype(BF16)
    lo = (x - hi.astype(F32)).astype(BF16)
    return _nn(hi, e) + _nn(lo, e)


def _swap_halves(x):
    lane = lax.broadcasted_iota(jnp.int32, x.shape, 1)
    quarter = HEAD_DIM // 4
    return jnp.where(lane % (2 * quarter) < quarter, pltpu.roll(x, LANES - quarter, 1), pltpu.roll(x, quarter, 1))


def qkv_forward(h, w2, gq, gk, cos, sin, e):
    lp, d = h.shape
    tm = _tm(lp)
    kw, vw = N_KB * LANES, N_KB * LANES

    def body(h_ref, w_ref, gq_ref, gk_ref, cos_ref, sin_ref, e_ref, raw_ref, q_ref, k_ref, v_ref, qt_ref, vt_ref):
        raw = _nn(h_ref[...].astype(BF16), w_ref[...])
        raw_ref[...] = raw
        c, s_, em = cos_ref[...], sin_ref[...], e_ref[...]
        for cb in range(N_QB + N_KB):
            t = raw[:, cb * LANES:(cb + 1) * LANES]
            rstd = lax.rsqrt(_segsum(t * t, em) * (1.0 / HEAD_DIM) + QK_EPS)
            n = t * rstd * (gq_ref[...] if cb < N_QB else gk_ref[...])
            rot = n * c + _swap_halves(n) * s_
            if cb < N_QB:
                qs = rot * Q_SCALE
                q_ref[:, cb * LANES:(cb + 1) * LANES] = qs.astype(BF16)
                qt_ref[cb * LANES:(cb + 1) * LANES, :] = qs.T.astype(BF16)
            else:
                k_ref[:, (cb - N_QB) * LANES:(cb - N_QB + 1) * LANES] = rot.astype(BF16)
        v_ref[...] = raw[:, (N_QB + N_KB) * LANES:].astype(BF16)
        for cb in range(N_KB):
            lo = (N_QB + N_KB + cb) * LANES
            vt_ref[cb * LANES:(cb + 1) * LANES, :] = raw[:, lo:lo + LANES].T.astype(BF16)

    col_spec = lambda rows: pl.BlockSpec((rows, tm), lambda i: (0, i))
    return pl.pallas_call(
        body, name="qkv_forward", grid=(lp // tm,),
        in_specs=[_row_spec(tm, d), _full_spec((d, QKV_W)), _full_spec((1, LANES)), _full_spec((1, LANES)),
                  _row_spec(tm, LANES), _row_spec(tm, LANES), _full_spec((LANES, LANES))],
        out_specs=[_row_spec(tm, QKV_W), _row_spec(tm, N_QB * LANES), _row_spec(tm, kw), _row_spec(tm, vw),
                   col_spec(N_QB * LANES), col_spec(vw)],
        out_shape=[jax.ShapeDtypeStruct((lp, QKV_W), F32), jax.ShapeDtypeStruct((lp, N_QB * LANES), BF16),
                   jax.ShapeDtypeStruct((lp, kw), BF16), jax.ShapeDtypeStruct((lp, vw), BF16),
                   jax.ShapeDtypeStruct((N_QB * LANES, lp), BF16), jax.ShapeDtypeStruct((vw, lp), BF16)],
        compiler_params=_params(("parallel",)),
    )(h, w2, gq, gk, cos, sin, e)


def qkv_backward(dqs, dk2, dv2, raw, gq, gk, cos, sin, e):
    lp = raw.shape[0]
    tm = _tm(lp)

    def body(dq_ref, dk_ref, dv_ref, raw_ref, gq_ref, gk_ref, cos_ref, sin_ref, e_ref, d_ref, s_ref):
        @pl.when(pl.program_id(0) == 0)
        def _():
            s_ref[...] = jnp.zeros_like(s_ref)

        c, s_, em = cos_ref[...], sin_ref[...], e_ref[...]
        gsum = [jnp.zeros((1, LANES), F32), jnp.zeros((1, LANES), F32)]
        for cb in range(N_QB + N_KB):
            isq = cb < N_QB
            t = raw_ref[:, cb * LANES:(cb + 1) * LANES]
            if isq:
                drot = dq_ref[:, cb * LANES:(cb + 1) * LANES] * (HEAD_DIM ** -0.5)
            else:
                drot = dk_ref[:, (cb - N_QB) * LANES:(cb - N_QB + 1) * LANES] * math.log(2.0)
            gain = gq_ref[...] if isq else gk_ref[...]
            rstd = lax.rsqrt(_segsum(t * t, em) * (1.0 / HEAD_DIM) + QK_EPS)
            dn = drot * c + _swap_halves(drot * s_)
            xh = t * rstd
            gsum[0 if isq else 1] = gsum[0 if isq else 1] + jnp.sum(dn * xh, axis=0, keepdims=True)
            w = dn * gain
            mw = _segsum(w * xh, em) * (1.0 / HEAD_DIM)
            d_ref[:, cb * LANES:(cb + 1) * LANES] = (rstd * (w - xh * mw)).astype(BF16)
        d_ref[:, (N_QB + N_KB) * LANES:] = dv_ref[...].astype(BF16)
        s_ref[0:1, :] += gsum[0]
        s_ref[1:2, :] += gsum[1]

    kw = N_KB * LANES
    return pl.pallas_call(
        body, name="qkv_backward", grid=(lp // tm,),
        in_specs=[_row_spec(tm, N_QB * LANES), _row_spec(tm, kw), _row_spec(tm, kw), _row_spec(tm, QKV_W),
                  _full_spec((1, LANES)), _full_spec((1, LANES)), _row_spec(tm, LANES), _row_spec(tm, LANES),
                  _full_spec((LANES, LANES))],
        out_specs=[_row_spec(tm, QKV_W), _full_spec((SUBLANES, LANES))],
        out_shape=[jax.ShapeDtypeStruct((lp, QKV_W), BF16), jax.ShapeDtypeStruct((SUBLANES, LANES), F32)],
        compiler_params=_params(("arbitrary",)),
    )(dqs, dk2, dv2, raw, gq, gk, cos, sin, e)


NEG = -1e30
Q_PER_KV = N_Q_HEADS // N_KV_HEADS


def _half_masks(x):
    lane = lax.broadcasted_iota(jnp.int32, x.shape, 1)
    zero = jnp.zeros_like(x)
    return jnp.where(lane < HEAD_DIM, x, zero), jnp.where(lane >= HEAD_DIM, x, zero)


ATTN_TR = 16


def _attn_tiles(lp):
    t = _pick_tile(lp, (1408, 256))
    return t, t


def attn_forward_t(qs, k2, v2t, n_valid):
    lp = qs.shape[0]
    tq, kb = _attn_tiles(lp)
    nk = lp // kb
    gw = 2 * LANES
    nr = kb // ATTN_TR
    pad0 = n_valid - (nk - 1) * kb

    def body(q_ref, k_ref, vt_ref, o_ref, lse_ref, m_s, l_s, acc_s, s_s, p_s):
        j = pl.program_id(2)

        @pl.when(j == 0)
        def _():
            m_s[...] = jnp.full_like(m_s, NEG)
            l_s[...] = jnp.zeros_like(l_s)
            acc_s[...] = jnp.zeros_like(acc_s)

        ks = _half_masks(k_ref[...])
        for pair in range(2):
            qp = q_ref[:, pair * LANES:(pair + 1) * LANES]
            for half in range(2):
                hh = 2 * pair + half
                s_s[...] = _nt(ks[half], qp)

                if pad0 < kb:
                    @pl.when(j == nk - 1)
                    def _():
                        s_s[pad0:, :] = jnp.full((kb - pad0, tq), NEG, F32)

                def max_step(r, run):
                    rows = pl.ds(r * ATTN_TR, ATTN_TR)
                    blk = s_s[rows, :]
                    for t in range(ATTN_TR // SUBLANES):
                        run = jnp.maximum(run, blk[t * SUBLANES:(t + 1) * SUBLANES])
                    return run

                run = jnp.full((SUBLANES, tq), NEG, F32)
                for r in range(nr):
                    run = max_step(r, run)
                m_prev = m_s[hh:hh + 1, :]
                m_new = jnp.maximum(m_prev, jnp.max(run, axis=0, keepdims=True))
                alpha = jnp.exp2(m_prev - m_new)
                m_s[hh:hh + 1, :] = m_new

                def exp_step(r, run):
                    rows = pl.ds(r * ATTN_TR, ATTN_TR)
                    p = jnp.exp2(s_s[rows, :] - m_new)
                    p_s[rows, :] = p.astype(BF16)
                    for t in range(ATTN_TR // SUBLANES):
                        run = run + p[t * SUBLANES:(t + 1) * SUBLANES]
                    return run

                run = jnp.zeros((SUBLANES, tq), F32)
                for r in range(nr):
                    run = exp_step(r, run)
                l_s[hh:hh + 1, :] = alpha * l_s[hh:hh + 1, :] + jnp.sum(run, axis=0, keepdims=True)
                vt = vt_ref[half * HEAD_DIM:(half + 1) * HEAD_DIM, :]
                pv = _nn(vt, p_s[...])
                rs = slice(half * HEAD_DIM, (half + 1) * HEAD_DIM)
                acc_s[pair, rs, :] = alpha * acc_s[pair, rs, :] + pv

        @pl.when(j == nk - 1)
        def _():
            for pair in range(2):
                for half in range(2):
                    hh = 2 * pair + half
                    rs = slice(half * HEAD_DIM, (half + 1) * HEAD_DIM)
                    acc_s[pair, rs, :] = acc_s[pair, rs, :] * (1.0 / l_s[hh:hh + 1, :])
                o_ref[:, pair * LANES:(pair + 1) * LANES] = acc_s[pair].T.astype(BF16)
            for hh in range(Q_PER_KV):
                lse_ref[0, hh] = m_s[hh:hh + 1, :] + jnp.log2(l_s[hh:hh + 1, :])

    return pl.pallas_call(
        body, name="attn_forward", grid=(N_KV_HEADS, lp // tq, nk),
        in_specs=[pl.BlockSpec((tq, gw), lambda g, i, j: (i, g)), pl.BlockSpec((kb, LANES), lambda g, i, j: (j, g)),
                  pl.BlockSpec((LANES, kb), lambda g, i, j: (g, j))],
        out_specs=[pl.BlockSpec((tq, gw), lambda g, i, j: (i, g)),
                   pl.BlockSpec((1, Q_PER_KV, 1, tq), lambda g, i, j: (g, 0, 0, i))],
        out_shape=[jax.ShapeDtypeStruct((lp, N_QB * LANES), BF16),
                   jax.ShapeDtypeStruct((N_KV_HEADS, Q_PER_KV, 1, lp), F32)],
        scratch_shapes=[pltpu.VMEM((SUBLANES, tq), F32), pltpu.VMEM((SUBLANES, tq), F32),
                        pltpu.VMEM((2, LANES, tq), F32), pltpu.VMEM((kb, tq), F32), pltpu.VMEM((kb, tq), BF16)],
        compiler_params=_params(("parallel", "parallel", "arbitrary")),
    )(qs, k2, v2t)


def attn_backward(qs, qst, k2, v2, do, dot, lse, delta, n_valid):
    lp = qs.shape[0]
    tq, kb = _attn_tiles(lp)
    nq, nk = lp // tq, lp // kb
    gw = 2 * LANES
    pad0 = n_valid - (nk - 1) * kb

    def body(q_ref, qt_ref, k_ref, v_ref, do_ref, dot_ref, lse_ref, dl_ref, dq_ref, dk_ref, dv_ref, acc_s, dkt_s, dvt_s):
        g = pl.program_id(0)
        i = pl.program_id(1)
        j = pl.program_id(2)
        cols = pl.ds(pl.multiple_of(j * kb, kb), kb)

        @pl.when(j == 0)
        def _():
            acc_s[...] = jnp.zeros_like(acc_s)

        @pl.when(i == 0)
        def _():
            dkt_s[:, cols] = jnp.zeros((LANES, kb), F32)
            dvt_s[:, cols] = jnp.zeros((LANES, kb), F32)

        head = lax.broadcasted_iota(jnp.int32, (tq, N_Q_HEADS), 1)

        def column(ref, hh):
            return jnp.sum(jnp.where(head == Q_PER_KV * g + hh, ref[...], 0.0), axis=1, keepdims=True)

        def step(masked):
            ks = _half_masks(k_ref[...])
            vs = _half_masks(v_ref[...])
            if masked:
                col = lax.broadcasted_iota(jnp.int32, (1, kb), 1)
                bias = jnp.where(col < pad0, 0.0, NEG)
            for pair in range(2):
                qp = q_ref[:, pair * LANES:(pair + 1) * LANES]
                dop = do_ref[:, pair * LANES:(pair + 1) * LANES]
                for half in range(2):
                    hh = 2 * pair + half
                    rs = slice(half * HEAD_DIM, (half + 1) * HEAD_DIM)
                    rt = slice(pair * LANES + half * HEAD_DIM, pair * LANES + (half + 1) * HEAD_DIM)
                    s = _nt(qp, ks[half])
                    if masked:
                        s = s + bias
                    p = jnp.exp2(s - column(lse_ref, hh))
                    dp = _nt(dop, vs[half])
                    ds = (p * (dp - column(dl_ref, hh))).astype(BF16)
                    pb = p.astype(BF16)
                    acc_s[pair] += _nn(ds, ks[half])
                    dvt_s[rs, cols] += _nn(dot_ref[rt, :], pb)
                    dkt_s[rs, cols] += _nn(qt_ref[rt, :], ds)

        if pad0 < kb:
            pl.when(j < nk - 1)(lambda: step(False))
            pl.when(j == nk - 1)(lambda: step(True))
        else:
            step(False)

        @pl.when(j == nk - 1)
        def _():
            for pair in range(2):
                dq_ref[:, pair * LANES:(pair + 1) * LANES] = acc_s[pair]

        @pl.when(i == nq - 1)
        def _():
            dk_ref[cols, :] = dkt_s[:, cols].T
            dv_ref[cols, :] = dvt_s[:, cols].T

    cspec = pl.BlockSpec((tq, N_Q_HEADS), lambda g, i, j: (i, 0))
    qspec = pl.BlockSpec((tq, gw), lambda g, i, j: (i, g))
    tspec = pl.BlockSpec((gw, tq), lambda g, i, j: (g, i))
    kspec = pl.BlockSpec((kb, LANES), lambda g, i, j: (j, g))
    gspec = pl.BlockSpec((lp, LANES), lambda g, i, j: (0, g))
    return pl.pallas_call(
        body, name="attn_backward", grid=(N_KV_HEADS, nq, nk),
        in_specs=[qspec, tspec, kspec, kspec, qspec, tspec, cspec, cspec],
        out_specs=[qspec, gspec, gspec],
        out_shape=[jax.ShapeDtypeStruct((lp, N_QB * LANES), F32),
                   jax.ShapeDtypeStruct((lp, N_KB * LANES), F32), jax.ShapeDtypeStruct((lp, N_KB * LANES), F32)],
        scratch_shapes=[pltpu.VMEM((2, tq, LANES), F32), pltpu.VMEM((LANES, lp), F32), pltpu.VMEM((LANES, lp), F32)],
        compiler_params=_params(("parallel", "arbitrary", "arbitrary")),
    )(qs, qst, k2, v2, do, dot, lse, delta)


def attn_out_backward(dr, wout, o, e16):
    lp, d = dr.shape
    tm = _tm(lp)

    def body(dr_ref, w_ref, o_ref, e_ref, do_ref, dl_ref, dot_ref):
        do32 = _nt(dr_ref[...].astype(BF16), w_ref[...])
        do = do32.astype(BF16)
        do_ref[...] = do
        for cb in range(d // LANES):
            dot_ref[cb * LANES:(cb + 1) * LANES, :] = do32[:, cb * LANES:(cb + 1) * LANES].T.astype(BF16)
        dl_ref[...] = _segsum(do.astype(F32) * o_ref[...].astype(F32), e_ref[...])

    rs = _row_spec(tm, d)
    return pl.pallas_call(
        body, name="attn_out_backward", grid=(lp // tm,),
        in_specs=[rs, _full_spec((d, d)), rs, _full_spec((d, N_Q_HEADS))],
        out_specs=[rs, _row_spec(tm, N_Q_HEADS), pl.BlockSpec((d, tm), lambda i: (0, i))],
        out_shape=[jax.ShapeDtypeStruct((lp, d), BF16), jax.ShapeDtypeStruct((lp, N_Q_HEADS), F32),
                   jax.ShapeDtypeStruct((d, lp), BF16)],
        compiler_params=_params(("parallel",)),
    )(dr, wout, o, e16)


N_CHIPS = 4


def _mesh_pos():
    return lax.axis_index("x"), lax.axis_index("y"), lax.axis_index("c")


def chip_exchange(arrs, scatter, name):
    n = len(arrs)
    hbm = pl.BlockSpec(memory_space=pl.ANY)

    def body(*refs):
        ins, outs = refs[:n], refs[n:2 * n]
        send_sems, recv_sems, loc_sems = refs[2 * n:]
        x, y, c = _mesh_pos()
        me = 2 * x + y
        chips = [(1 - x, y), (x, 1 - y), (1 - x, 1 - y)]
        started = []
        for a in range(n):
            loc = pltpu.make_async_copy(ins[a].at[me] if scatter else ins[a], outs[a].at[me], loc_sems.at[a])
            loc.start()
            started.append(loc)
            for k, (px, py) in enumerate(chips):
                src = ins[a].at[2 * px + py] if scatter else ins[a]
                cp = pltpu.make_async_remote_copy(
                    src_ref=src, dst_ref=outs[a].at[me], send_sem=send_sems.at[3 * a + k], recv_sem=recv_sems.at[3 * a + k],
                    device_id=(px, py, c), device_id_type=MESH)
                cp.start()
                started.append(cp)
        for cp in started:
            cp.wait()

    out_shape = [jax.ShapeDtypeStruct(a.shape if scatter else (N_CHIPS,) + a.shape, a.dtype) for a in arrs]
    return pl.pallas_call(
        body, name=name, in_specs=[hbm] * n, out_specs=[hbm] * n, out_shape=out_shape,
        scratch_shapes=[pltpu.SemaphoreType.DMA((3 * n,)), pltpu.SemaphoreType.DMA((3 * n,)), pltpu.SemaphoreType.DMA((n,))],
    )(*arrs)


def gather_two_level(arrs, name):
    n = len(arrs)
    hbm = pl.BlockSpec(memory_space=pl.ANY)

    def body(*refs):
        ins, outs = refs[:n], refs[n:2 * n]
        ici_send, ici_recv, d2d_send, d2d_recv, loc_sems = refs[2 * n:]
        x, y, c = _mesh_pos()
        me = 2 * x + y
        chips = [(1 - x, y), (x, 1 - y), (1 - x, 1 - y)]
        started = []
        for a in range(n):
            hn = arrs[a].shape[0] // 2
            mine = pl.ds(c * hn, hn)
            loc = pltpu.make_async_copy(ins[a], outs[a].at[me], loc_sems.at[a])
            loc.start()
            started.append(loc)
            first = []
            for k, (px, py) in enumerate(chips):
                cp = pltpu.make_async_remote_copy(
                    src_ref=ins[a].at[mine], dst_ref=outs[a].at[me, mine], send_sem=ici_send.at[3 * a + k],
                    recv_sem=ici_recv.at[3 * a + k], device_id=(px, py, c), device_id_type=MESH)
                cp.start()
                first.append(cp)
            for k, (px, py) in enumerate(chips):
                q = 2 * px + py
                first[k].wait_recv()
                fw = pltpu.make_async_remote_copy(
                    src_ref=outs[a].at[q, mine], dst_ref=outs[a].at[q, mine], send_sem=d2d_send.at[3 * a + k],
                    recv_sem=d2d_recv.at[3 * a + k], device_id=(x, y, 1 - c), device_id_type=MESH)
                fw.start()
                started.append(fw)
            for cp in first:
                cp.wait_send()
        for cp in started:
            cp.wait()

    out_shape = [jax.ShapeDtypeStruct((N_CHIPS,) + a.shape, a.dtype) for a in arrs]
    return pl.pallas_call(
        body, name=name, in_specs=[hbm] * n, out_specs=[hbm] * n, out_shape=out_shape,
        scratch_shapes=[pltpu.SemaphoreType.DMA((3 * n,))] * 4 + [pltpu.SemaphoreType.DMA((n,))],
    )(*arrs)


def sibling_exchange(arrs, name):
    n = len(arrs)
    hbm = pl.BlockSpec(memory_space=pl.ANY)

    def body(*refs):
        ins, outs = refs[:n], refs[n:2 * n]
        send_sems, recv_sems = refs[2 * n:]
        x, y, c = _mesh_pos()
        started = []
        for a in range(n):
            cp = pltpu.make_async_remote_copy(
                src_ref=ins[a], dst_ref=outs[a], send_sem=send_sems.at[a], recv_sem=recv_sems.at[a],
                device_id=(x, y, 1 - c), device_id_type=MESH)
            cp.start()
            started.append(cp)
        for cp in started:
            cp.wait()

    return pl.pallas_call(
        body, name=name, in_specs=[hbm] * n, out_specs=[hbm] * n,
        out_shape=[jax.ShapeDtypeStruct(a.shape, a.dtype) for a in arrs],
        scratch_shapes=[pltpu.SemaphoreType.DMA((n,)), pltpu.SemaphoreType.DMA((n,))],
    )(*arrs)


def _rows_tile(r, c):
    return _pick_tile(r, tuple(t for t in (512, 256, 128, 64, 32, 16, 8) if t * c * 4 <= 2 * 1024 * 1024))


def chip_sum(recv, name):
    _, r, c = recv.shape
    tr = _rows_tile(r, c)

    def body(r_ref, o_ref):
        acc = r_ref[0].astype(F32)
        for q in range(1, N_CHIPS):
            acc = acc + r_ref[q].astype(F32)
        o_ref[...] = acc

    return pl.pallas_call(
        body, name=name, grid=(r // tr,),
        in_specs=[pl.BlockSpec((N_CHIPS, tr, c), lambda i: (0, i, 0))],
        out_specs=pl.BlockSpec((tr, c), lambda i: (i, 0)),
        out_shape=jax.ShapeDtypeStruct((r, c), F32),
        compiler_params=_params(("parallel",)),
    )(recv)


def pair_sum(part, sib, name, dtype=F32):
    r, c = part.shape
    tr = _rows_tile(r, c)

    def body(p_ref, s_ref, o_ref):
        o_ref[...] = (p_ref[...].astype(F32) + s_ref[...].astype(F32)).astype(dtype)

    rs = pl.BlockSpec((tr, c), lambda i: (i, 0))
    return pl.pallas_call(
        body, name=name, grid=(r // tr,), in_specs=[rs] * 2, out_specs=rs,
        out_shape=jax.ShapeDtypeStruct((r, c), dtype), compiler_params=_params(("parallel",)),
    )(part, sib)


def adamw(g, w, m, v, name):
    r, c = w.shape
    tr = _rows_tile(r, c)

    def body(g_ref, w_ref, m_ref, v_ref, d_ref, nm_ref, nv_ref):
        g_ = g_ref[...]
        m_ = ADAM_B1 * m_ref[...] + (1.0 - ADAM_B1) * g_
        v_ = ADAM_B2 * v_ref[...] + (1.0 - ADAM_B2) * (g_ * g_)
        m_hat = m_ / (1.0 - ADAM_B1 ** ADAM_STEP)
        v_hat = v_ / (1.0 - ADAM_B2 ** ADAM_STEP)
        d_ref[...] = -ADAM_LR * (m_hat / (jnp.sqrt(v_hat) + ADAM_EPS) + ADAM_WD * w_ref[...])
        nm_ref[...] = m_
        nv_ref[...] = v_

    rs = pl.BlockSpec((tr, c), lambda i: (i, 0))
    return pl.pallas_call(
        body, name=name, grid=(r // tr,), in_specs=[rs] * 4, out_specs=[rs] * 3,
        out_shape=[jax.ShapeDtypeStruct((r, c), F32)] * 3,
        compiler_params=_params(("parallel",)),
    )(g, w, m, v)


WEIGHTS = ['meta_tokens', 's5_lambda_re', 's5_lambda_im', 's5_log_dt', 's5_b_re', 's5_b_im', 's5_c_re', 's5_c_im', 's5_d',
           's5_w_glu', 's5_w_out', 'attn_w_qkv', 'attn_q_gain', 'attn_k_gain', 'attn_w_out', 'ffn_w_gate', 'ffn_w_up',
           'ffn_w_down', 'ln_gain', 'ln_bias']
BIG = ['s5_w_glu', 's5_w_out', 'attn_w_qkv', 'attn_w_out', 'ffn_w_gate', 'ffn_w_up', 'ffn_w_down']
ROW_SHARDED = {'s5_w_glu', 's5_w_out', 'attn_w_out', 'ffn_w_down'}
SMALL_SHARDED = ['meta_tokens', 'ln_gain', 'ln_bias']
REPLICATED = ['s5_lambda_re', 's5_lambda_im', 's5_log_dt', 's5_b_re', 's5_b_im', 's5_c_re', 's5_c_im', 's5_d',
              'attn_q_gain', 'attn_k_gain']
REP_ALIGN = N_CHIPS * LANES * LANES


def _natural(gathered, row_sharded):
    p, n, a, b = gathered.shape
    if row_sharded:
        return jnp.transpose(gathered, (1, 0, 2, 3)).reshape(n, p * a, b)
    return jnp.transpose(gathered, (1, 2, 0, 3)).reshape(n, a, p * b)


def _shard_major(full, row_sharded):
    n, a, b = full.shape
    if row_sharded:
        return jnp.transpose(full.reshape(n, N_CHIPS, a // N_CHIPS, b), (1, 0, 2, 3))
    return jnp.transpose(full.reshape(n, a, N_CHIPS, b // N_CHIPS), (2, 0, 1, 3))


def _dup_heads(w):
    lead = w.shape[:-1]
    w = w.reshape(lead + (N_KV_HEADS, 1, HEAD_DIM))
    return jnp.broadcast_to(w, lead + (N_KV_HEADS, 2, HEAD_DIM)).reshape(lead + (N_KV_HEADS * 2 * HEAD_DIM,))


def _fold_heads(d):
    lead = d.shape[:-1]
    return d.reshape(lead + (N_KV_HEADS, 2, HEAD_DIM)).sum(axis=-2).reshape(lead + (N_KV_HEADS * HEAD_DIM,))


def _pack_rep(tree):
    flat = jnp.concatenate([tree[n].reshape(-1) for n in REPLICATED])
    pad = _round_up(flat.shape[0], REP_ALIGN) - flat.shape[0]
    return jnp.pad(flat, (0, pad))


def _unpack_rep(flat, like):
    out, off = {}, 0
    for n in REPLICATED:
        size = math.prod(like[n].shape)
        out[n] = flat[off:off + size].reshape(like[n].shape)
        off += size
    return out


def _train_step(x, loss_target, w, mom, vel):
    s = x.shape[1]
    n_valid = N_META + s
    lp = _round_up(n_valid, 2 * LANES)
    nq = N_Q_HEADS * HEAD_DIM
    nkv = N_KV_HEADS * HEAD_DIM

    small = jnp.concatenate([w[n].reshape(-1, w[n].shape[-1]) for n in SMALL_SHARDED], axis=0)
    gathered = gather_two_level([w[n].astype(BF16) for n in BIG] + [small], "gather_weights")
    full = {n: _natural(g, n in ROW_SHARDED) for n, g in zip(BIG, gathered[:-1])}
    small_full = jnp.transpose(gathered[-1], (1, 0, 2)).reshape(small.shape[0], D_MODEL)
    meta_full = small_full[:N_META]
    ln_gain = small_full[N_META:N_META + 2 * DEPTH].reshape(DEPTH, 2, 1, D_MODEL)
    ln_bias = small_full[N_META + 2 * DEPTH:].reshape(DEPTH, 2, 1, D_MODEL)
    wqkv = full['attn_w_qkv']
    w2 = jnp.concatenate([wqkv[..., :nq], _dup_heads(wqkv[..., nq:nq + nkv]), _dup_heads(wqkv[..., nq + nkv:])], axis=-1)

    cos, sin = rope_tables(lp, n_valid)
    e128 = head_sum_matrix()
    e16 = jnp.kron(jnp.eye(N_Q_HEADS, dtype=F32), jnp.ones((HEAD_DIM, 1), F32)).astype(BF16)
    gq = jnp.tile(w['attn_q_gain'], (1, 2))[:, None, :]
    gk = jnp.tile(w['attn_k_gain'], (1, 2))[:, None, :]

    pad_rows = jnp.zeros((lp - n_valid, D_MODEL), F32)
    h = jnp.concatenate([meta_full, x[0], pad_rows], axis=0)
    tgt = jnp.concatenate([jnp.zeros((N_META, D_MODEL), F32), loss_target[0], pad_rows], axis=0)

    saved = []
    s5_names = ['s5_lambda_re', 's5_lambda_im', 's5_log_dt', 's5_b_re', 's5_b_im', 's5_c_re', 's5_c_im']
    for i in range(DEPTH):
        j = i // 2
        sv = {'h': h}
        if i % 2 == 0:
            ops, sv['prep_vjp'] = jax.vjp(s5_prep, *[w[n][j] for n in s5_names])
            m_, wx_, ci_, at_ = ops
            two = lambda t: t.reshape((2 * S5_NJ,) + t.shape[2:])
            sv['ops'] = (blockdiag_expand(m_, S5_CH, S5_CH, "s5_expand_m"),
                         blockdiag_expand(two(wx_), S5_CH, S5_STATE, "s5_expand_wx").reshape(2, S5_NJ, S5_W, S5_W),
                         blockdiag_expand(two(ci_), S5_STATE, S5_CH, "s5_expand_ci").reshape(2, S5_NJ, S5_W, S5_W), at_)
            y, sv['lhs'], sv['sp'], sv['sn'] = s5_forward(h, *sv['ops'], n_valid)
            sv['v'], sv['t'], sv['g'], sv['z'] = glu_forward(y, h, w['s5_d'][j][None], full['s5_w_glu'][j])
            sv['r1'], h1 = proj_ln_forward(sv['z'], full['s5_w_out'][j], h, ln_gain[i, 0], ln_bias[i, 0], "s5_out_ln")
        else:
            sv['raw'], sv['qs'], sv['k2'], sv['v2'], sv['qst'], v2t = qkv_forward(h, w2[j], gq[j], gk[j], cos, sin, e128)
            sv['o'], lse = attn_forward_t(sv['qs'], sv['k2'], v2t, n_valid)
            sv['lse'] = lse.reshape(N_Q_HEADS, lp).T
            sv['r1'], h1 = proj_ln_forward(sv['o'], full['attn_w_out'][j], h, ln_gain[i, 0], ln_bias[i, 0], "attn_out_ln")
        sv['h1'] = h1
        sv['a'], sv['b'], sv['f'] = ffn_up_forward(h1, full['ffn_w_gate'][i], full['ffn_w_up'][i])
        sv['r2'], h = proj_ln_forward(sv['f'], full['ffn_w_down'][i], h1, ln_gain[i, 1], ln_bias[i, 1], "ffn_down_ln")
        saved.append(sv)

    dh, loss_part = loss_backward(h, tgt, n_valid)
    loss = lax.psum(jnp.sum(loss_part), ("x", "y", "c"))

    gfull = {n: [None] * w[n].shape[0] for n in BIG}
    d_ln_gain = [[None, None] for _ in range(DEPTH)]
    d_ln_bias = [[None, None] for _ in range(DEPTH)]
    grep = {n: [None] * w[n].shape[0] for n in REPLICATED}
    for i in reversed(range(DEPTH)):
        j = i // 2
        sv = saved[i]
        dr2, s2 = ln_backward(dh, sv['r2'], ln_gain[i, 1])
        d_ln_gain[i][1], d_ln_bias[i][1] = s2[0], s2[1]
        da, db = ffn_backward_act(dr2, full['ffn_w_down'][i], sv['a'], sv['b'])
        gfull['ffn_w_down'][i] = mm_tn(sv['f'], dr2, "grad_ffn_down")
        dh1 = resid_nt(dr2, [da, db], [full['ffn_w_gate'][i], full['ffn_w_up'][i]], "ffn_backward_x")
        gfull['ffn_w_gate'][i] = mm_tn(sv['h1'], da, "grad_ffn_gate")
        gfull['ffn_w_up'][i] = mm_tn(sv['h1'], db, "grad_ffn_up")
        dr1, s1 = ln_backward(dh1, sv['r1'], ln_gain[i, 0])
        d_ln_gain[i][0], d_ln_bias[i][0] = s1[0], s1[1]
        if i % 2 == 0:
            dt, dgd = glu_backward1(dr1, full['s5_w_out'][j], sv['g'], sv['t'])
            gfull['s5_w_out'][j] = mm_tn(sv['z'], dr1, "grad_s5_out")
            dv, dhs, sd = glu_backward2(dt, dgd, full['s5_w_glu'][j], sv['v'], sv['h'], w['s5_d'][j][None], dr1)
            grep['s5_d'][j] = sd[0]
            gfull['s5_w_glu'][j] = mm_tn(sv['g'], dt, "grad_s5_glu")
            dh, ldy, dxf, dxr, daf, dar = s5_backward(dv, dhs, *sv['ops'], sv['sp'], sv['sn'], n_valid)
            dm = bmm_tn_compact(sv['lhs'], ldy, S5_CH, S5_CH, "grad_s5_m")
            dwx = jnp.stack([bmm_tn_compact(sv['lhs'], dxf, S5_CH, S5_STATE, "grad_s5_wxf"),
                             bmm_tn_compact(sv['lhs'], dxr, S5_CH, S5_STATE, "grad_s5_wxr")])
            dci = jnp.stack([bmm_tn_compact(sv['sp'], ldy, S5_STATE, S5_CH, "grad_s5_cif"),
                             bmm_tn_compact(sv['sn'], ldy, S5_STATE, S5_CH, "grad_s5_cir")])
            dps = sv['prep_vjp']((dm, dwx, dci, jnp.stack([daf, dar])))
            for n, g in zip(s5_names, dps):
                grep[n][j] = g
        else:
            do, delta, dot = attn_out_backward(dr1, full['attn_w_out'][j], sv['o'], e16)
            gfull['attn_w_out'][j] = mm_tn(sv['o'], dr1, "grad_attn_out")
            dq, dk2, dv2 = attn_backward(sv['qs'], sv['qst'], sv['k2'], sv['v2'], do, dot, sv['lse'], delta, n_valid)
            draw, gs = qkv_backward(dq, dk2, dv2, sv['raw'], gq[j], gk[j], cos, sin, e128)
            grep['attn_q_gain'][j] = gs[0, :HEAD_DIM] + gs[0, HEAD_DIM:]
            grep['attn_k_gain'][j] = gs[1, :HEAD_DIM] + gs[1, HEAD_DIM:]
            dh = resid_nt(dr1, [draw], [w2[j]], "attn_backward_x")
            dw2 = mm_tn(sv['h'], draw, "grad_attn_qkv")
            kq = N_QB * LANES
            kk = N_KB * LANES
            gfull['attn_w_qkv'][j] = jnp.concatenate(
                [dw2[:, :kq], _fold_heads(dw2[:, kq:kq + kk]), _fold_heads(dw2[:, kq + kk:])], axis=1)
    grad_x = dh[N_META:n_valid][None]

    core = lax.axis_index("c")
    contrib = [_shard_major(jnp.stack(gfull[n]), n in ROW_SHARDED) for n in BIG]
    small_g = jnp.concatenate([dh[:N_META], jnp.stack([g for pair in d_ln_gain for g in pair]),
                               jnp.stack([g for pair in d_ln_bias for g in pair])], axis=0)
    contrib.append(jnp.transpose(small_g.reshape(-1, N_CHIPS, D_MODEL // N_CHIPS), (1, 0, 2)))
    rep_g = _pack_rep({n: jnp.stack(grep[n]) for n in REPLICATED})
    contrib.append(rep_g.reshape(N_CHIPS, -1, LANES))
    names = BIG + ['small', 'rep']
    wire = [BF16] * len(BIG) + [F32, F32]
    keep, give = [], []
    for t, dt in zip(contrib, wire):
        hn = t.shape[1] // 2
        keep.append(lax.dynamic_slice_in_dim(t, core * hn, hn, axis=1))
        give.append(lax.dynamic_slice_in_dim(t, (1 - core) * hn, hn, axis=1).astype(dt))
    got = sibling_exchange(give, "sibling_contrib")
    two_d = lambda t: t.reshape(-1, t.shape[-1])
    pair = [pair_sum(two_d(a), two_d(b), "pair_sum_" + n, dt).reshape(a.shape)
            for n, a, b, dt in zip(names, keep, got, wire)]
    recv = chip_exchange(pair, True, "scatter_grads")
    halves = [chip_sum(r.reshape(N_CHIPS, -1, r.shape[-1]), "chip_sum_" + n) for n, r in zip(names, recv)]
    others = sibling_exchange(halves, "sibling_halves")
    grads = [jnp.where(core == 0, jnp.concatenate([a, b], axis=0), jnp.concatenate([b, a], axis=0))
             for a, b in zip(halves, others)]

    out = {}

    def update(n, g, wn, mn, vn):
        shape = wn.shape
        flat = (-1, shape[-1])
        d, nm, nv = adamw(g, wn.reshape(flat), mn.reshape(flat), vn.reshape(flat), "adamw_" + n)
        return tuple(t.reshape(shape) for t in (g, d, nm, nv))

    for n, g in zip(BIG, grads):
        out[n] = update(n, g, w[n], mom[n], vel[n])
    cat = lambda tree: jnp.concatenate([tree[n].reshape(-1, tree[n].shape[-1]) for n in SMALL_SHARDED], axis=0)
    sm = update("small", grads[-2], cat(w), cat(mom), cat(vel))
    off = 0
    for n in SMALL_SHARDED:
        rows = math.prod(w[n].shape[:-1])
        out[n] = tuple(t[off:off + rows].reshape(w[n].shape) for t in sm)
        off += rows
    rep_all = chip_exchange([grads[-1]], False, "gather_rep")[0].reshape(-1, LANES)
    rp = update("rep", rep_all, _pack_rep(w).reshape(-1, LANES), _pack_rep(mom).reshape(-1, LANES),
                _pack_rep(vel).reshape(-1, LANES))
    unpacked = [_unpack_rep(t.reshape(-1), w) for t in rp]
    for n in REPLICATED:
        out[n] = tuple(u[n] for u in unpacked)

    return (loss, grad_x, *[out[n][0] for n in WEIGHTS], *[out[n][1] for n in WEIGHTS],
            *[out[n][2] for n in WEIGHTS], *[out[n][3] for n in WEIGHTS])


def kernel(x, meta_tokens, s5_lambda_re, s5_lambda_im, s5_log_dt, s5_b_re, s5_b_im, s5_c_re, s5_c_im, s5_d, s5_w_glu, s5_w_out, attn_w_qkv, attn_q_gain, attn_k_gain, attn_w_out, ffn_w_gate, ffn_w_up, ffn_w_down, ln_gain, ln_bias, loss_target, m_meta_tokens, m_s5_lambda_re, m_s5_lambda_im, m_s5_log_dt, m_s5_b_re, m_s5_b_im, m_s5_c_re, m_s5_c_im, m_s5_d, m_s5_w_glu, m_s5_w_out, m_attn_w_qkv, m_attn_q_gain, m_attn_k_gain, m_attn_w_out, m_ffn_w_gate, m_ffn_w_up, m_ffn_w_down, m_ln_gain, m_ln_bias, v_meta_tokens, v_s5_lambda_re, v_s5_lambda_im, v_s5_log_dt, v_s5_b_re, v_s5_b_im, v_s5_c_re, v_s5_c_im, v_s5_d, v_s5_w_glu, v_s5_w_out, v_attn_w_qkv, v_attn_q_gain, v_attn_k_gain, v_attn_w_out, v_ffn_w_gate, v_ffn_w_up, v_ffn_w_down, v_ln_gain, v_ln_bias):
    given = locals()
    w = {n: given[n] for n in WEIGHTS}
    mom = {n: given["m_" + n] for n in WEIGHTS}
    vel = {n: given["v_" + n] for n in WEIGHTS}
    return _train_step(x, loss_target, w, mom, vel)
```

```python
import math

import jax
import jax.numpy as jnp
from jax import lax
from jax.experimental import pallas as pl
from jax.experimental.pallas import tpu as pltpu

F32 = jnp.float32
BF16 = jnp.bfloat16
MESH = pl.DeviceIdType.MESH

D_MODEL = 1024
N_META = 16
GRID_W = 64
HEAD_DIM = 64
N_Q_HEADS = 16
N_KV_HEADS = 4
ROPE_THETA = 10000.0
QK_EPS = 1e-6
S5_CH = 16
S5_GROUPS = 64
S5_STATE = 64
D_FF = 2816
LN_EPS = 1e-5
DEPTH = 4
ALPHA = (2.0 * DEPTH) ** 0.25
ADAM_LR, ADAM_B1, ADAM_B2, ADAM_EPS, ADAM_WD, ADAM_STEP = 0.001, 0.9, 0.999, 1e-08, 0.01, 10

LANES = 128
SUBLANES = 8
VMEM_LIMIT = 56 * 1024 * 1024

S5_T = 8
S5_GB = LANES // S5_CH
S5_NJ = S5_GROUPS // S5_GB
S5_W = S5_T * LANES
S5_SW = 2 * S5_GB * S5_STATE
S5_HALF = S5_SW // 2


def _round_up(a, b):
    return -(-a // b) * b


def _pick_tile(n, prefs):
    for t in prefs:
        if n % t == 0:
            return t
    return n


def _params(sem=None):
    kw = dict(vmem_limit_bytes=VMEM_LIMIT)
    if sem is not None:
        kw["dimension_semantics"] = sem
    return pltpu.CompilerParams(**kw)


def _dot(a, b, dims):
    return lax.dot_general(a, b, (dims, ((), ())), preferred_element_type=F32)


def _nn(a, b):
    return _dot(a, b, ((1,), (0,)))


def _nt(a, b):
    return _dot(a, b, ((1,), (1,)))


def _tn(a, b):
    return _dot(a, b, ((0,), (0,)))


def _compact(w):
    g, a0, a1, b0, b1 = w.shape
    w = jnp.transpose(w.reshape(S5_NJ, S5_GB, a0, a1, b0, b1), (0, 2, 1, 3, 4, 5))
    return w.reshape(S5_NJ, a0 * S5_GB * a1, b0 * b1)


def s5_prep(lam_re, lam_im, log_dt, b_re, b_im, c_re, c_im):
    hi = lax.Precision.HIGHEST
    t = S5_T
    dt = jnp.exp(log_dt)[..., None]
    taus = jnp.arange(t + 1, dtype=F32)[:, None, None, None]
    mag = jnp.exp(lam_re * dt)
    ang = lam_im * dt
    pr = jnp.concatenate([jnp.ones_like(mag)[None], (mag * jnp.cos(ang))[None],
                          jnp.exp(lam_re * dt * taus[2:]) * jnp.cos(ang * taus[2:])], axis=0)
    pi = jnp.concatenate([jnp.zeros_like(mag)[None], (mag * jnp.sin(ang))[None],
                          jnp.exp(lam_re * dt * taus[2:]) * jnp.sin(ang * taus[2:])], axis=0)
    abr, abi = pr[1], pi[1]
    nr, ni = abr - 1.0, abi
    den = lam_re * lam_re + lam_im * lam_im
    cr = (nr * lam_re + ni * lam_im) / den
    ci_ = (ni * lam_re - nr * lam_im) / den
    bbr = cr[..., None] * b_re - ci_[..., None] * b_im
    bbi = cr[..., None] * b_im + ci_[..., None] * b_re
    er = c_re[None] * pr[:, :, :, None, :] - c_im[None] * pi[:, :, :, None, :]
    ei = c_re[None] * pi[:, :, :, None, :] + c_im[None] * pr[:, :, :, None, :]
    kk = (jnp.einsum("tdgop,dgpc->tdgoc", er[:t], bbr, precision=hi)
          - jnp.einsum("tdgop,dgpc->tdgoc", ei[:t], bbi, precision=hi))
    zero = jnp.zeros_like(kk[0, 0])
    mg = jnp.stack([jnp.stack([(kk[i - s, 0] if i > s else zero) + (kk[s - i, 1] if s > i else zero)
                               + ((kk[0, 0] + kk[0, 1]) if i == s else zero) for i in range(t)])
                    for s in range(t)])
    mg = jnp.transpose(mg, (2, 0, 4, 1, 3))
    m = _compact(mg)
    pw_f = jnp.stack([pr[t - 1 - s, 0] for s in range(t)]), jnp.stack([pi[t - 1 - s, 0] for s in range(t)])
    pw_r = jnp.stack([pr[s, 1] for s in range(t)]), jnp.stack([pi[s, 1] for s in range(t)])
    wx = []
    for d, (qr, qi) in enumerate((pw_f, pw_r)):
        wr = qr[..., None] * bbr[d][None] - qi[..., None] * bbi[d][None]
        wi = qr[..., None] * bbi[d][None] + qi[..., None] * bbr[d][None]
        w = jnp.stack([wr, wi], axis=0)
        w = jnp.transpose(w, (2, 1, 4, 0, 3))
        wx.append(_compact(w))
    ci = []
    for d in range(2):
        exps = [i + 1 for i in range(t)] if d == 0 else [t - i for i in range(t)]
        e_r = jnp.stack([er[e, d] for e in exps])
        e_i = jnp.stack([ei[e, d] for e in exps])
        w = jnp.stack([e_r, -e_i], axis=0)
        w = jnp.transpose(w, (2, 0, 4, 1, 3))
        ci.append(_compact(w))
    at = jnp.stack([pr[t], pi[t]], axis=1)
    at = at.reshape(2, 2, S5_NJ, S5_GB * S5_STATE)
    at = jnp.transpose(at, (0, 2, 1, 3)).reshape(2, S5_NJ, 1, S5_SW)
    return m, jnp.stack(wx), jnp.stack(ci), at


def _chunk_rows(ref, nc):
    return jnp.concatenate([ref[pl.ds(s, nc, stride=S5_T), :] for s in range(S5_T)], axis=1)


def _cmul(ar, ai, sr, si):
    return ar * sr - ai * si, ar * si + ai * sr


def _scan_tiles(nc, reverse, step):
    nt = nc // SUBLANES

    def body(it, carry):
        tix = (nt - 1 - it) if reverse else it
        k0 = pl.multiple_of(tix * SUBLANES, SUBLANES)
        return step(k0, carry)

    return body, nt


def _s5_specs(nc):
    hbm = pl.BlockSpec(memory_space=pl.ANY)
    aspec = pl.BlockSpec((1, 1, S5_SW), lambda j: (j, 0, 0))
    cspec = pl.BlockSpec((1, nc, S5_W), lambda j: (j, 0, 0))
    return hbm, aspec, cspec


def _s5_fetch(j, tok_hbm, w_hbms, tok_s, w_s, sems):
    cols = pl.ds(pl.multiple_of(j * LANES, LANES), LANES)
    cps = [pltpu.make_async_copy(tok_hbm.at[:, cols], tok_s, sems.at[0])]
    for i, w in enumerate(w_hbms):
        cps.append(pltpu.make_async_copy(w.at[j], w_s.at[i], sems.at[1 + i]))
    for cp in cps:
        cp.start()
    return cols, cps


def s5_forward(u, m, wx, ci, at, n_valid):
    lp = u.shape[0]
    nc = lp // S5_T
    nvc = n_valid // S5_T

    def body(u_hbm, m_hbm, wxf_hbm, wxr_hbm, cif_hbm, cir_hbm, atf_ref, atr_ref,
             y_hbm, lhs_ref, sp_ref, sn_ref, tok_s, w_s, xf_s, xr_s, sems):
        j = pl.program_id(0)
        cols, cps = _s5_fetch(j, u_hbm, (m_hbm, wxf_hbm, wxr_hbm, cif_hbm, cir_hbm), tok_s, w_s, sems)
        cps[0].wait()
        lhs = _chunk_rows(tok_s, nc)
        rows = lax.broadcasted_iota(jnp.int32, lhs.shape, 0)
        lhs = jnp.where(rows < nvc, lhs, 0.0).astype(BF16)
        lhs_ref[0] = lhs
        cps[2].wait()
        cps[3].wait()
        xf_s[...] = _nn(lhs, w_s[1])
        xr_s[...] = _nn(lhs, w_s[2])
        afr, afi = atf_ref[0, :, :S5_HALF], atf_ref[0, :, S5_HALF:]
        arr, ari = atr_ref[0, :, :S5_HALF], atr_ref[0, :, S5_HALF:]

        def scan_step(x_s, ar, ai, descending):
            def step(k0, carry):
                sr, si = carry
                x = x_s[pl.ds(k0, SUBLANES), :]
                outs = [None] * SUBLANES
                order = reversed(range(SUBLANES)) if descending else range(SUBLANES)
                for r in order:
                    outs[r] = jnp.concatenate([sr, si], axis=1)
                    nr, ni = _cmul(ar, ai, sr, si)
                    sr = nr + x[r:r + 1, :S5_HALF]
                    si = ni + x[r:r + 1, S5_HALF:]
                x_s[pl.ds(k0, SUBLANES), :] = jnp.concatenate(outs, axis=0)
                return sr, si
            return step

        zero = jnp.zeros((1, S5_HALF), F32)
        fb, nt = _scan_tiles(nc, False, scan_step(xf_s, afr, afi, False))
        lax.fori_loop(0, nt, fb, (zero, zero))
        rb, nt = _scan_tiles(nc, True, scan_step(xr_s, arr, ari, True))
        lax.fori_loop(0, nt, rb, (zero, zero))
        sp = xf_s[...].astype(BF16)
        sn = xr_s[...].astype(BF16)
        sp_ref[0] = sp
        sn_ref[0] = sn
        cps[1].wait()
        cps[4].wait()
        cps[5].wait()
        y = _nn(lhs, w_s[0]) + _nn(sp, w_s[3]) + _nn(sn, w_s[4])
        for i in range(S5_T):
            tok_s[pl.ds(i, nc, stride=S5_T), :] = y[:, i * LANES:(i + 1) * LANES]
        pltpu.sync_copy(tok_s, y_hbm.at[:, cols])

    hbm, aspec, cspec = _s5_specs(nc)
    return pl.pallas_call(
        body, name="s5_forward", grid=(S5_NJ,),
        in_specs=[hbm] * 6 + [aspec, aspec],
        out_specs=[hbm, cspec, cspec, cspec],
        out_shape=[jax.ShapeDtypeStruct((lp, D_MODEL), F32)] + [jax.ShapeDtypeStruct((S5_NJ, nc, S5_W), BF16)] * 3,
        scratch_shapes=[pltpu.VMEM((lp, LANES), F32), pltpu.VMEM((5, S5_W, S5_W), BF16),
                        pltpu.VMEM((nc, S5_SW), F32), pltpu.VMEM((nc, S5_SW), F32), pltpu.SemaphoreType.DMA((6,))],
        compiler_params=_params(("arbitrary",)),
    )(u, m, wx[0], wx[1], ci[0], ci[1], at[0], at[1])


def s5_backward(dy, dhs, m, wx, ci, at, sp, sn, n_valid):
    lp = dy.shape[0]
    nc = lp // S5_T
    nvc = n_valid // S5_T

    def body(dy_hbm, dhs_hbm, m_hbm, wxf_hbm, wxr_hbm, cif_hbm, cir_hbm, atf_ref, atr_ref, sp_ref, sn_ref,
             dh_hbm, ldy_ref, dxf_ref, dxr_ref, daf_ref, dar_ref, tok_s, w_s, gf_s, gr_s, sems):
        j = pl.program_id(0)
        cols, cps = _s5_fetch(j, dy_hbm, (m_hbm, wxf_hbm, wxr_hbm, cif_hbm, cir_hbm), tok_s, w_s, sems)
        cps[0].wait()
        ldy = _chunk_rows(tok_s, nc)
        rows = lax.broadcasted_iota(jnp.int32, ldy.shape, 0)
        ldy = jnp.where(rows < nvc, ldy, 0.0).astype(BF16)
        ldy_ref[0] = ldy
        resid = pltpu.make_async_copy(dhs_hbm.at[:, cols], tok_s, sems.at[0])
        resid.start()
        cps[4].wait()
        cps[5].wait()
        gf_s[...] = _nt(ldy, w_s[3])
        gr_s[...] = _nt(ldy, w_s[4])
        afr, afi = atf_ref[0, :, :S5_HALF], atf_ref[0, :, S5_HALF:]
        arr, ari = atr_ref[0, :, :S5_HALF], atr_ref[0, :, S5_HALF:]

        def adj_step(g_s, s_ref, ar, ai, descending):
            def step(k0, carry):
                gr_, gi_, dr_, di_ = carry
                g = g_s[pl.ds(k0, SUBLANES), :]
                p = s_ref[0, pl.ds(k0, SUBLANES), :].astype(F32)
                outs = [None] * SUBLANES
                order = reversed(range(SUBLANES)) if descending else range(SUBLANES)
                for r in order:
                    outs[r] = jnp.concatenate([gr_, gi_], axis=1)
                    pr_, pi_ = p[r:r + 1, :S5_HALF], p[r:r + 1, S5_HALF:]
                    dr_ = dr_ + gr_ * pr_ + gi_ * pi_
                    di_ = di_ + gi_ * pr_ - gr_ * pi_
                    nr, ni = _cmul(ar, -ai, gr_, gi_)
                    gr_ = nr + g[r:r + 1, :S5_HALF]
                    gi_ = ni + g[r:r + 1, S5_HALF:]
                g_s[pl.ds(k0, SUBLANES), :] = jnp.concatenate(outs, axis=0)
                return gr_, gi_, dr_, di_
            return step

        zero = jnp.zeros((1, S5_HALF), F32)
        fb, nt = _scan_tiles(nc, True, adj_step(gf_s, sp_ref, afr, afi, True))
        _, _, dr_, di_ = lax.fori_loop(0, nt, fb, (zero,) * 4)
        daf_ref[0] = jnp.concatenate([dr_, di_], axis=1)
        rb, nt = _scan_tiles(nc, False, adj_step(gr_s, sn_ref, arr, ari, False))
        _, _, dr_, di_ = lax.fori_loop(0, nt, rb, (zero,) * 4)
        dar_ref[0] = jnp.concatenate([dr_, di_], axis=1)
        dxf = gf_s[...].astype(BF16)
        dxr = gr_s[...].astype(BF16)
        dxf_ref[0] = dxf
        dxr_ref[0] = dxr
        cps[1].wait()
        cps[2].wait()
        cps[3].wait()
        du = _nt(ldy, w_s[0]) + _nt(dxf, w_s[1]) + _nt(dxr, w_s[2])
        rows = lax.broadcasted_iota(jnp.int32, du.shape, 0)
        du = jnp.where(rows < nvc, du, 0.0)
        resid.wait()
        for s in range(S5_T):
            tok_s[pl.ds(s, nc, stride=S5_T), :] += du[:, s * LANES:(s + 1) * LANES]
        pltpu.sync_copy(tok_s, dh_hbm.at[:, cols])

    hbm, aspec, cspec = _s5_specs(nc)
    return pl.pallas_call(
        body, name="s5_backward", grid=(S5_NJ,),
        in_specs=[hbm] * 7 + [aspec, aspec, cspec, cspec],
        out_specs=[hbm, cspec, cspec, cspec, aspec, aspec],
        out_shape=[jax.ShapeDtypeStruct((lp, D_MODEL), F32)] + [jax.ShapeDtypeStruct((S5_NJ, nc, S5_W), BF16)] * 3
        + [jax.ShapeDtypeStruct((S5_NJ, 1, S5_SW), F32)] * 2,
        scratch_shapes=[pltpu.VMEM((lp, LANES), F32), pltpu.VMEM((5, S5_W, S5_W), BF16),
                        pltpu.VMEM((nc, S5_SW), F32), pltpu.VMEM((nc, S5_SW), F32), pltpu.SemaphoreType.DMA((6,))],
        compiler_params=_params(("arbitrary",)),
    )(dy, dhs, m, wx[0], wx[1], ci[0], ci[1], at[0], at[1], sp, sn)


S5_CW = LANES


def _replicate_matrix(b1):
    b0n = S5_CW // b1
    eye0 = jnp.eye(b0n, dtype=F32)
    eye1 = jnp.eye(b1, dtype=F32)
    r = jnp.einsum("ab,cd->acbd", eye0, eye1)[:, :, :, None, :]
    r = jnp.broadcast_to(r, (b0n, b1, b0n, S5_GB, b1))
    return r.reshape(S5_CW, b0n * S5_GB * b1).astype(BF16)


def _same_group(a1, b1):
    rg = (lax.broadcasted_iota(jnp.int32, (S5_W, S5_W), 0) // a1) % S5_GB
    cg = (lax.broadcasted_iota(jnp.int32, (S5_W, S5_W), 1) // b1) % S5_GB
    return rg == cg


def blockdiag_expand(compact, a1, b1, name):
    nj = compact.shape[0]

    def body(c_ref, r_ref, o_ref):
        rep = _nn(c_ref[0].astype(BF16), r_ref[...])
        o_ref[0] = jnp.where(_same_group(a1, b1), rep, 0.0).astype(BF16)

    return pl.pallas_call(
        body, name=name, grid=(nj,),
        in_specs=[pl.BlockSpec((1, S5_W, S5_CW), lambda j: (j, 0, 0)), _full_spec((S5_CW, S5_W))],
        out_specs=pl.BlockSpec((1, S5_W, S5_W), lambda j: (j, 0, 0)),
        out_shape=jax.ShapeDtypeStruct((nj, S5_W, S5_W), BF16),
        compiler_params=_params(("parallel",)),
    )(compact, _replicate_matrix(b1))


def bmm_tn_compact(a, b, a1, b1, name):
    nj, k, wa = a.shape
    wb = b.shape[2]

    def body(a_ref, b_ref, r_ref, o_ref):
        prod = jnp.where(_same_group(a1, b1), _tn(a_ref[0], b_ref[0]), 0.0)
        hi = prod.astype(BF16)
        lo = (prod - hi.astype(F32)).astype(BF16)
        o_ref[0] = _nt(hi, r_ref[...]) + _nt(lo, r_ref[...])

    return pl.pallas_call(
        body, name=name, grid=(nj,),
        in_specs=[pl.BlockSpec((1, k, wa), lambda j: (j, 0, 0)), pl.BlockSpec((1, k, wb), lambda j: (j, 0, 0)),
                  _full_spec((S5_CW, S5_W))],
        out_specs=pl.BlockSpec((1, wa, S5_CW), lambda j: (j, 0, 0)),
        out_shape=jax.ShapeDtypeStruct((nj, wa, S5_CW), F32),
        compiler_params=_params(("parallel",)),
    )(a, b, _replicate_matrix(b1))


def _tm(lp):
    return _pick_tile(lp, (768, 256))


def _row_spec(tm, width):
    return pl.BlockSpec((tm, width), lambda i: (i, 0))


def _full_spec(shape):
    return pl.BlockSpec(shape, lambda *_: (0,) * len(shape))


def _gelu(v):
    return 0.5 * v * (1.0 + lax.erf(v * (2.0 ** -0.5)))


def _gelu_grad(v):
    return 0.5 * (1.0 + lax.erf(v * (2.0 ** -0.5))) + v * jnp.exp(-0.5 * v * v) * (2.0 * math.pi) ** -0.5


def _layer_norm(r, gain, bias):
    mean = jnp.mean(r, axis=-1, keepdims=True)
    c = r - mean
    var = jnp.mean(c * c, axis=-1, keepdims=True)
    return c * lax.rsqrt(var + LN_EPS) * gain + bias


def glu_forward(y, h, dvec, wglu):
    lp, d = y.shape
    tm = _tm(lp)

    def body(y_ref, h_ref, d_ref, w_ref, v_ref, t_ref, g_ref, z_ref):
        v = y_ref[...] + d_ref[...] * h_ref[...]
        g = _gelu(v)
        gb = g.astype(BF16)
        t = _nn(gb, w_ref[...])
        v_ref[...] = v
        t_ref[...] = t
        g_ref[...] = gb
        z_ref[...] = (g * jax.nn.sigmoid(t)).astype(BF16)

    rs = _row_spec(tm, d)
    return pl.pallas_call(
        body, name="glu_forward", grid=(lp // tm,),
        in_specs=[rs, rs, _full_spec((1, d)), _full_spec((d, d))],
        out_specs=[rs, rs, rs, rs],
        out_shape=[jax.ShapeDtypeStruct((lp, d), F32)] * 2 + [jax.ShapeDtypeStruct((lp, d), BF16)] * 2,
        compiler_params=_params(("parallel",)),
    )(y, h, dvec, wglu)


def proj_ln_forward(z, w, h, gain, bias, name):
    lp, k = z.shape
    d = w.shape[1]
    tm = _tm(lp)

    def body(z_ref, w_ref, h_ref, g_ref, b_ref, r_ref, o_ref):
        r = ALPHA * h_ref[...] + _nn(z_ref[...], w_ref[...])
        r_ref[...] = r
        o_ref[...] = _layer_norm(r, g_ref[...], b_ref[...])

    rs = _row_spec(tm, d)
    return pl.pallas_call(
        body, name=name, grid=(lp // tm,),
        in_specs=[_row_spec(tm, k), _full_spec((k, d)), rs, _full_spec((1, d)), _full_spec((1, d))],
        out_specs=[rs, rs],
        out_shape=[jax.ShapeDtypeStruct((lp, d), F32)] * 2,
        compiler_params=_params(("parallel",)),
    )(z, w, h, gain, bias)


FFN_NB = 1408


def ffn_up_forward(h, wg, wu):
    lp, d = h.shape
    dff = wg.shape[1]
    tm = _tm(lp)

    def body(h_ref, wg_ref, wu_ref, a_ref, b_ref, f_ref):
        hb = h_ref[...].astype(BF16)
        a = _nn(hb, wg_ref[...])
        b = _nn(hb, wu_ref[...])
        a_ref[...] = a.astype(BF16)
        b_ref[...] = b.astype(BF16)
        f_ref[...] = (a * jax.nn.sigmoid(a) * b).astype(BF16)

    ws = pl.BlockSpec((d, FFN_NB), lambda n, i: (0, n))
    os_ = pl.BlockSpec((tm, FFN_NB), lambda n, i: (i, n))
    return pl.pallas_call(
        body, name="ffn_up_forward", grid=(dff // FFN_NB, lp // tm),
        in_specs=[pl.BlockSpec((tm, d), lambda n, i: (i, 0)), ws, ws],
        out_specs=[os_, os_, os_],
        out_shape=[jax.ShapeDtypeStruct((lp, dff), BF16)] * 3,
        compiler_params=_params(("parallel", "parallel")),
    )(h, wg, wu)


def ln_backward(dh, r, gain):
    lp, d = dh.shape
    tm = _tm(lp)

    def body(dh_ref, r_ref, g_ref, dr_ref, s_ref):
        r_ = r_ref[...]
        dh_ = dh_ref[...]
        mean = jnp.mean(r_, axis=-1, keepdims=True)
        c = r_ - mean
        var = jnp.mean(c * c, axis=-1, keepdims=True)
        rstd = lax.rsqrt(var + LN_EPS)
        xh = c * rstd
        dxh = dh_ * g_ref[...]
        m1 = jnp.mean(dxh, axis=-1, keepdims=True)
        m2 = jnp.mean(dxh * xh, axis=-1, keepdims=True)
        dr_ref[...] = rstd * (dxh - m1 - xh * m2)

        @pl.when(pl.program_id(0) == 0)
        def _():
            s_ref[...] = jnp.zeros_like(s_ref)

        s_ref[0:1, :] += jnp.sum(dh_ * xh, axis=0, keepdims=True)
        s_ref[1:2, :] += jnp.sum(dh_, axis=0, keepdims=True)

    rs = _row_spec(tm, d)
    return pl.pallas_call(
        body, name="ln_backward", grid=(lp // tm,),
        in_specs=[rs, rs, _full_spec((1, d))],
        out_specs=[rs, _full_spec((SUBLANES, d))],
        out_shape=[jax.ShapeDtypeStruct((lp, d), F32), jax.ShapeDtypeStruct((SUBLANES, d), F32)],
        compiler_params=_params(("arbitrary",)),
    )(dh, r, gain)


def ffn_backward_act(dr, wd, a, b):
    lp, d = dr.shape
    dff = wd.shape[0]
    tm = _tm(lp)

    def body(dr_ref, wd_ref, a_ref, b_ref, da_ref, db_ref):
        df = _nt(dr_ref[...].astype(BF16), wd_ref[...])
        a_ = a_ref[...].astype(F32)
        b_ = b_ref[...].astype(F32)
        sg = jax.nn.sigmoid(a_)
        da_ref[...] = (df * b_ * sg * (1.0 + a_ * (1.0 - sg))).astype(BF16)
        db_ref[...] = (df * a_ * sg).astype(BF16)

    os_ = pl.BlockSpec((tm, FFN_NB), lambda n, i: (i, n))
    return pl.pallas_call(
        body, name="ffn_backward_act", grid=(dff // FFN_NB, lp // tm),
        in_specs=[pl.BlockSpec((tm, d), lambda n, i: (i, 0)), pl.BlockSpec((FFN_NB, d), lambda n, i: (n, 0)), os_, os_],
        out_specs=[os_, os_],
        out_shape=[jax.ShapeDtypeStruct((lp, dff), BF16)] * 2,
        compiler_params=_params(("parallel", "parallel")),
    )(dr, wd, a, b)


def resid_nt(dr, xs, ws, name):
    lp, d = dr.shape
    tm = _tm(lp)
    n = len(xs)

    def body(*refs):
        acc = ALPHA * refs[0][...]
        for i in range(n):
            acc = acc + _nt(refs[1 + i][...], refs[1 + n + i][...])
        refs[-1][...] = acc

    rs = _row_spec(tm, d)
    in_specs = [rs] + [_row_spec(tm, x.shape[1]) for x in xs] + [_full_spec(w.shape) for w in ws]
    return pl.pallas_call(
        body, name=name, grid=(lp // tm,),
        in_specs=in_specs, out_specs=rs,
        out_shape=jax.ShapeDtypeStruct((lp, d), F32),
        compiler_params=_params(("parallel",)),
    )(dr, *xs, *ws)


def mm_tn(x, y, name):
    lp, k = x.shape
    n = y.shape[1]
    tm = _tm(lp)
    nb = _pick_tile(n, (512, 1408))

    def body(x_ref, y_ref, o_ref):
        @pl.when(pl.program_id(1) == 0)
        def _():
            o_ref[...] = jnp.zeros_like(o_ref)

        o_ref[...] += _tn(x_ref[...].astype(BF16), y_ref[...].astype(BF16))

    return pl.pallas_call(
        body, name=name, grid=(n // nb, lp // tm),
        in_specs=[pl.BlockSpec((tm, k), lambda j, i: (i, 0)), pl.BlockSpec((tm, nb), lambda j, i: (i, j))],
        out_specs=pl.BlockSpec((k, nb), lambda j, i: (0, j)),
        out_shape=jax.ShapeDtypeStruct((k, n), F32),
        compiler_params=_params(("parallel", "arbitrary")),
    )(x, y)


def glu_backward1(dr, wout, g, t):
    lp, d = dr.shape
    tm = _tm(lp)

    def body(dr_ref, w_ref, g_ref, t_ref, dt_ref, dgd_ref):
        dz = _nt(dr_ref[...].astype(BF16), w_ref[...])
        s = jax.nn.sigmoid(t_ref[...])
        dgd_ref[...] = dz * s
        dt_ref[...] = (dz * g_ref[...].astype(F32) * s * (1.0 - s)).astype(BF16)

    rs = _row_spec(tm, d)
    return pl.pallas_call(
        body, name="glu_backward1", grid=(lp // tm,),
        in_specs=[rs, _full_spec((d, d)), rs, rs],
        out_specs=[rs, rs],
        out_shape=[jax.ShapeDtypeStruct((lp, d), BF16), jax.ShapeDtypeStruct((lp, d), F32)],
        compiler_params=_params(("parallel",)),
    )(dr, wout, g, t)


def glu_backward2(dt, dgd, wglu, v, h, dvec, dr):
    lp, d = dt.shape
    tm = _tm(lp)

    def body(dt_ref, dgd_ref, w_ref, v_ref, h_ref, d_ref, dr_ref, dv_ref, dhs_ref, s_ref):
        dg = dgd_ref[...] + _nt(dt_ref[...], w_ref[...])
        dv = dg * _gelu_grad(v_ref[...])
        dv_ref[...] = dv
        dhs_ref[...] = ALPHA * dr_ref[...] + dv * d_ref[...]

        @pl.when(pl.program_id(0) == 0)
        def _():
            s_ref[...] = jnp.zeros_like(s_ref)

        s_ref[0:1, :] += jnp.sum(dv * h_ref[...], axis=0, keepdims=True)

    rs = _row_spec(tm, d)
    return pl.pallas_call(
        body, name="glu_backward2", grid=(lp // tm,),
        in_specs=[rs, rs, _full_spec((d, d)), rs, rs, _full_spec((1, d)), rs],
        out_specs=[rs, rs, _full_spec((SUBLANES, d))],
        out_shape=[jax.ShapeDtypeStruct((lp, d), F32)] * 2 + [jax.ShapeDtypeStruct((SUBLANES, d), F32)],
        compiler_params=_params(("arbitrary",)),
    )(dt, dgd, wglu, v, h, dvec, dr)


def loss_backward(hf, tgt, n_valid):
    lp, d = hf.shape
    tm = _tm(lp)

    def body(h_ref, t_ref, dh_ref, s_ref):
        rows = pl.program_id(0) * tm + lax.broadcasted_iota(jnp.int32, (tm, d), 0)
        ok = (rows >= N_META) & (rows < n_valid)
        e = jnp.where(ok, h_ref[...] - t_ref[...], 0.0)
        dh_ref[...] = e * (1.0 / d)

        @pl.when(pl.program_id(0) == 0)
        def _():
            s_ref[...] = jnp.zeros_like(s_ref)

        sq = e * e
        part = sq[:, 0:LANES]
        for c in range(1, d // LANES):
            part = part + sq[:, c * LANES:(c + 1) * LANES]
        acc = part[0:SUBLANES]
        for r in range(1, tm // SUBLANES):
            acc = acc + part[r * SUBLANES:(r + 1) * SUBLANES]
        s_ref[...] += acc * (0.5 / d)

    rs = _row_spec(tm, d)
    return pl.pallas_call(
        body, name="loss_backward", grid=(lp // tm,),
        in_specs=[rs, rs], out_specs=[rs, _full_spec((SUBLANES, LANES))],
        out_shape=[jax.ShapeDtypeStruct((lp, d), F32), jax.ShapeDtypeStruct((SUBLANES, LANES), F32)],
        compiler_params=_params(("arbitrary",)),
    )(hf, tgt)


N_QB = N_Q_HEADS // 2
N_KB = N_KV_HEADS
QKV_W = (N_QB + 2 * N_KB) * LANES
Q_SCALE = HEAD_DIM ** -0.5 * math.log2(math.e)


def rope_tables(lp, n_valid):
    t = jnp.arange(lp, dtype=jnp.int32)
    real = (t >= N_META) & (t < n_valid)
    pos = jnp.where(real, t - N_META, 0)
    row = (pos // GRID_W).astype(F32)
    col = (pos % GRID_W).astype(F32)
    axis_dim = HEAD_DIM // 2
    inv = ROPE_THETA ** (-jnp.arange(0, axis_dim, 2, dtype=F32) / axis_dim)
    ar = row[:, None] * inv[None, :]
    ac = col[:, None] * inv[None, :]
    cos = jnp.concatenate([jnp.cos(ar), jnp.cos(ar), jnp.cos(ac), jnp.cos(ac)], axis=1)
    sin = jnp.concatenate([-jnp.sin(ar), jnp.sin(ar), -jnp.sin(ac), jnp.sin(ac)], axis=1)
    return jnp.tile(cos, (1, 2)), jnp.tile(sin, (1, 2))


def head_sum_matrix():
    return jnp.kron(jnp.eye(2, dtype=F32), jnp.ones((HEAD_DIM, HEAD_DIM), F32)).astype(BF16)


def _segsum(x, e):
    hi = x.astype(BF16)
    lo = (x - hi.astype(F32)).astype(BF16)
    return _nn(hi, e) + _nn(lo, e)


def _swap_halves(x):
    lane = lax.broadcasted_iota(jnp.int32, x.shape, 1)
    quarter = HEAD_DIM // 4
    return jnp.where(lane % (2 * quarter) < quarter, pltpu.roll(x, LANES - quarter, 1), pltpu.roll(x, quarter, 1))


def qkv_forward(h, w2, gq, gk, cos, sin, e):
    lp, d = h.shape
    tm = _tm(lp)
    kw, vw = N_KB * LANES, N_KB * LANES

    def body(h_ref, w_ref, gq_ref, gk_ref, cos_ref, sin_ref, e_ref, raw_ref, q_ref, k_ref, v_ref, qt_ref, vt_ref):
        raw = _nn(h_ref[...].astype(BF16), w_ref[...])
        raw_ref[...] = raw
        c, s_, em = cos_ref[...], sin_ref[...], e_ref[...]
        for cb in range(N_QB + N_KB):
            t = raw[:, cb * LANES:(cb + 1) * LANES]
            rstd = lax.rsqrt(_segsum(t * t, em) * (1.0 / HEAD_DIM) + QK_EPS)
            n = t * rstd * (gq_ref[...] if cb < N_QB else gk_ref[...])
            rot = n * c + _swap_halves(n) * s_
            if cb < N_QB:
                qs = rot * Q_SCALE
                q_ref[:, cb * LANES:(cb + 1) * LANES] = qs.astype(BF16)
                qt_ref[cb * LANES:(cb + 1) * LANES, :] = qs.T.astype(BF16)
            else:
                k_ref[:, (cb - N_QB) * LANES:(cb - N_QB + 1) * LANES] = rot.astype(BF16)
        v_ref[...] = raw[:, (N_QB + N_KB) * LANES:].astype(BF16)
        for cb in range(N_KB):
            lo = (N_QB + N_KB + cb) * LANES
            vt_ref[cb * LANES:(cb + 1) * LANES, :] = raw[:, lo:lo + LANES].T.astype(BF16)

    col_spec = lambda rows: pl.BlockSpec((rows, tm), lambda i: (0, i))
    return pl.pallas_call(
        body, name="qkv_forward", grid=(lp // tm,),
        in_specs=[_row_spec(tm, d), _full_spec((d, QKV_W)), _full_spec((1, LANES)), _full_spec((1, LANES)),
                  _row_spec(tm, LANES), _row_spec(tm, LANES), _full_spec((LANES, LANES))],
        out_specs=[_row_spec(tm, QKV_W), _row_spec(tm, N_QB * LANES), _row_spec(tm, kw), _row_spec(tm, vw),
                   col_spec(N_QB * LANES), col_spec(vw)],
        out_shape=[jax.ShapeDtypeStruct((lp, QKV_W), F32), jax.ShapeDtypeStruct((lp, N_QB * LANES), BF16),
                   jax.ShapeDtypeStruct((lp, kw), BF16), jax.ShapeDtypeStruct((lp, vw), BF16),
                   jax.ShapeDtypeStruct((N_QB * LANES, lp), BF16), jax.ShapeDtypeStruct((vw, lp), BF16)],
        compiler_params=_params(("parallel",)),
    )(h, w2, gq, gk, cos, sin, e)


def qkv_backward(dqs, dk2, dv2, raw, gq, gk, cos, sin, e):
    lp = raw.shape[0]
    tm = _tm(lp)

    def body(dq_ref, dk_ref, dv_ref, raw_ref, gq_ref, gk_ref, cos_ref, sin_ref, e_ref, d_ref, s_ref):
        @pl.when(pl.program_id(0) == 0)
        def _():
            s_ref[...] = jnp.zeros_like(s_ref)

        c, s_, em = cos_ref[...], sin_ref[...], e_ref[...]
        gsum = [jnp.zeros((1, LANES), F32), jnp.zeros((1, LANES), F32)]
        for cb in range(N_QB + N_KB):
            isq = cb < N_QB
            t = raw_ref[:, cb * LANES:(cb + 1) * LANES]
            if isq:
                drot = dq_ref[:, cb * LANES:(cb + 1) * LANES] * (HEAD_DIM ** -0.5)
            else:
                drot = dk_ref[:, (cb - N_QB) * LANES:(cb - N_QB + 1) * LANES] * math.log(2.0)
            gain = gq_ref[...] if isq else gk_ref[...]
            rstd = lax.rsqrt(_segsum(t * t, em) * (1.0 / HEAD_DIM) + QK_EPS)
            dn = drot * c + _swap_halves(drot * s_)
            xh = t * rstd
            gsum[0 if isq else 1] = gsum[0 if isq else 1] + jnp.sum(dn * xh, axis=0, keepdims=True)
            w = dn * gain
            mw = _segsum(w * xh, em) * (1.0 / HEAD_DIM)
            d_ref[:, cb * LANES:(cb + 1) * LANES] = (rstd * (w - xh * mw)).astype(BF16)
        d_ref[:, (N_QB + N_KB) * LANES:] = dv_ref[...].astype(BF16)
        s_ref[0:1, :] += gsum[0]
        s_ref[1:2, :] += gsum[1]

    kw = N_KB * LANES
    return pl.pallas_call(
        body, name="qkv_backward", grid=(lp // tm,),
        in_specs=[_row_spec(tm, N_QB * LANES), _row_spec(tm, kw), _row_spec(tm, kw), _row_spec(tm, QKV_W),
                  _full_spec((1, LANES)), _full_spec((1, LANES)), _row_spec(tm, LANES), _row_spec(tm, LANES),
                  _full_spec((LANES, LANES))],
        out_specs=[_row_spec(tm, QKV_W), _full_spec((SUBLANES, LANES))],
        out_shape=[jax.ShapeDtypeStruct((lp, QKV_W), BF16), jax.ShapeDtypeStruct((SUBLANES, LANES), F32)],
        compiler_params=_params(("arbitrary",)),
    )(dqs, dk2, dv2, raw, gq, gk, cos, sin, e)


NEG = -1e30
Q_PER_KV = N_Q_HEADS // N_KV_HEADS


def _half_masks(x):
    lane = lax.broadcasted_iota(jnp.int32, x.shape, 1)
    zero = jnp.zeros_like(x)
    return jnp.where(lane < HEAD_DIM, x, zero), jnp.where(lane >= HEAD_DIM, x, zero)


ATTN_TR = 16


def _attn_tiles(lp):
    t = _pick_tile(lp, (1408, 256))
    return t, t


def _attn_tiles_bwd(lp):
    t = _pick_tile(lp, (768, 256))
    return t, t


def attn_forward_t(qs, k2, v2t, n_valid):
    lp = qs.shape[0]
    tq, kb = _attn_tiles(lp)
    nk = lp // kb
    gw = 2 * LANES
    nr = kb // ATTN_TR
    pad0 = n_valid - (nk - 1) * kb

    def body(q_ref, k_ref, vt_ref, o_ref, lse_ref, m_s, l_s, acc_s, s_s, p_s):
        j = pl.program_id(2)

        @pl.when(j == 0)
        def _():
            m_s[...] = jnp.full_like(m_s, NEG)
            l_s[...] = jnp.zeros_like(l_s)
            acc_s[...] = jnp.zeros_like(acc_s)

        ks = _half_masks(k_ref[...])
        for pair in range(2):
            qp = q_ref[:, pair * LANES:(pair + 1) * LANES]
            for half in range(2):
                hh = 2 * pair + half
                s_s[...] = _nt(ks[half], qp)

                if pad0 < kb:
                    @pl.when(j == nk - 1)
                    def _():
                        s_s[pad0:, :] = jnp.full((kb - pad0, tq), NEG, F32)

                def max_step(r, run):
                    rows = pl.ds(r * ATTN_TR, ATTN_TR)
                    blk = s_s[rows, :]
                    for t in range(ATTN_TR // SUBLANES):
                        run = jnp.maximum(run, blk[t * SUBLANES:(t + 1) * SUBLANES])
                    return run

                run = jnp.full((SUBLANES, tq), NEG, F32)
                for r in range(nr):
                    run = max_step(r, run)
                m_prev = m_s[hh:hh + 1, :]
                m_new = jnp.maximum(m_prev, jnp.max(run, axis=0, keepdims=True))
                alpha = jnp.exp2(m_prev - m_new)
                m_s[hh:hh + 1, :] = m_new

                def exp_step(r, run):
                    rows = pl.ds(r * ATTN_TR, ATTN_TR)
                    p = jnp.exp2(s_s[rows, :] - m_new)
                    p_s[rows, :] = p.astype(BF16)
                    for t in range(ATTN_TR // SUBLANES):
                        run = run + p[t * SUBLANES:(t + 1) * SUBLANES]
                    return run

                run = jnp.zeros((SUBLANES, tq), F32)
                for r in range(nr):
                    run = exp_step(r, run)
                l_s[hh:hh + 1, :] = alpha * l_s[hh:hh + 1, :] + jnp.sum(run, axis=0, keepdims=True)
                vt = vt_ref[half * HEAD_DIM:(half + 1) * HEAD_DIM, :]
                pv = _nn(vt, p_s[...])
                rs = slice(half * HEAD_DIM, (half + 1) * HEAD_DIM)
                acc_s[pair, rs, :] = alpha * acc_s[pair, rs, :] + pv

        @pl.when(j == nk - 1)
        def _():
            for pair in range(2):
                for half in range(2):
                    hh = 2 * pair + half
                    rs = slice(half * HEAD_DIM, (half + 1) * HEAD_DIM)
                    acc_s[pair, rs, :] = acc_s[pair, rs, :] * (1.0 / l_s[hh:hh + 1, :])
                o_ref[:, pair * LANES:(pair + 1) * LANES] = acc_s[pair].T.astype(BF16)
            for hh in range(Q_PER_KV):
                lse_ref[0, hh] = m_s[hh:hh + 1, :] + jnp.log2(l_s[hh:hh + 1, :])

    return pl.pallas_call(
        body, name="attn_forward", grid=(N_KV_HEADS, lp // tq, nk),
        in_specs=[pl.BlockSpec((tq, gw), lambda g, i, j: (i, g)), pl.BlockSpec((kb, LANES), lambda g, i, j: (j, g)),
                  pl.BlockSpec((LANES, kb), lambda g, i, j: (g, j))],
        out_specs=[pl.BlockSpec((tq, gw), lambda g, i, j: (i, g)),
                   pl.BlockSpec((1, Q_PER_KV, 1, tq), lambda g, i, j: (g, 0, 0, i))],
        out_shape=[jax.ShapeDtypeStruct((lp, N_QB * LANES), BF16),
                   jax.ShapeDtypeStruct((N_KV_HEADS, Q_PER_KV, 1, lp), F32)],
        scratch_shapes=[pltpu.VMEM((SUBLANES, tq), F32), pltpu.VMEM((SUBLANES, tq), F32),
                        pltpu.VMEM((2, LANES, tq), F32), pltpu.VMEM((kb, tq), F32), pltpu.VMEM((kb, tq), BF16)],
        compiler_params=_params(("parallel", "parallel", "arbitrary")),
    )(qs, k2, v2t)


def attn_backward(qs, qst, k2, v2, do, dot, lse, delta, n_valid):
    lp = qs.shape[0]
    tq, kb = _attn_tiles_bwd(lp)
    nq, nk = lp // tq, lp // kb
    gw = 2 * LANES
    pad0 = n_valid - (nk - 1) * kb

    def body(q_ref, qt_ref, k_ref, v_ref, do_ref, dot_ref, lse_ref, dl_ref, dq_ref, dk_ref, dv_ref, acc_s, dkt_s, dvt_s):
        g = pl.program_id(0)
        i = pl.program_id(1)
        j = pl.program_id(2)
        cols = pl.ds(pl.multiple_of(j * kb, kb), kb)

        @pl.when(j == 0)
        def _():
            acc_s[...] = jnp.zeros_like(acc_s)

        @pl.when(i == 0)
        def _():
            dkt_s[:, cols] = jnp.zeros((LANES, kb), F32)
            dvt_s[:, cols] = jnp.zeros((LANES, kb), F32)

        head = lax.broadcasted_iota(jnp.int32, (tq, N_Q_HEADS), 1)

        def column(ref, hh):
            return jnp.sum(jnp.where(head == Q_PER_KV * g + hh, ref[...], 0.0), axis=1, keepdims=True)

        def step(masked):
            ks = _half_masks(k_ref[...])
            vs = _half_masks(v_ref[...])
            if masked:
                col = lax.broadcasted_iota(jnp.int32, (1, kb), 1)
                bias = jnp.where(col < pad0, 0.0, NEG)
            for pair in range(2):
                qp = q_ref[:, pair * LANES:(pair + 1) * LANES]
                dop = do_ref[:, pair * LANES:(pair + 1) * LANES]
                for half in range(2):
                    hh = 2 * pair + half
                    rs = slice(half * HEAD_DIM, (half + 1) * HEAD_DIM)
                    rt = slice(pair * LANES + half * HEAD_DIM, pair * LANES + (half + 1) * HEAD_DIM)
                    s = _nt(qp, ks[half])
                    if masked:
                        s = s + bias
                    p = jnp.exp2(s - column(lse_ref, hh))
                    dp = _nt(dop, vs[half])
                    ds = (p * (dp - column(dl_ref, hh))).astype(BF16)
                    pb = p.astype(BF16)
                    acc_s[pair] += _nn(ds, ks[half])
                    dvt_s[rs, cols] += _nn(dot_ref[rt, :], pb)
                    dkt_s[rs, cols] += _nn(qt_ref[rt, :], ds)

        if pad0 < kb:
            pl.when(j < nk - 1)(lambda: step(False))
            pl.when(j == nk - 1)(lambda: step(True))
        else:
            step(False)

        @pl.when(j == nk - 1)
        def _():
            for pair in range(2):
                dq_ref[:, pair * LANES:(pair + 1) * LANES] = acc_s[pair]

        @pl.when(i == nq - 1)
        def _():
            dk_ref[cols, :] = dkt_s[:, cols].T
            dv_ref[cols, :] = dvt_s[:, cols].T

    cspec = pl.BlockSpec((tq, N_Q_HEADS), lambda g, i, j: (i, 0))
    qspec = pl.BlockSpec((tq, gw), lambda g, i, j: (i, g))
    tspec = pl.BlockSpec((gw, tq), lambda g, i, j: (g, i))
    kspec = pl.BlockSpec((kb, LANES), lambda g, i, j: (j, g))
    gspec = pl.BlockSpec((lp, LANES), lambda g, i, j: (0, g))
    return pl.pallas_call(
        body, name="attn_backward", grid=(N_KV_HEADS, nq, nk),
        in_specs=[qspec, tspec, kspec, kspec, qspec, tspec, cspec, cspec],
        out_specs=[qspec, gspec, gspec],
        out_shape=[jax.ShapeDtypeStruct((lp, N_QB * LANES), F32),
                   jax.ShapeDtypeStruct((lp, N_KB * LANES), F32), jax.ShapeDtypeStruct((lp, N_KB * LANES), F32)],
        scratch_shapes=[pltpu.VMEM((2, tq, LANES), F32), pltpu.VMEM((LANES, lp), F32), pltpu.VMEM((LANES, lp), F32)],
        compiler_params=_params(("parallel", "arbitrary", "arbitrary")),
    )(qs, qst, k2, v2, do, dot, lse, delta)


def attn_out_backward(dr, wout, o, e16):
    lp, d = dr.shape
    tm = _tm(lp)

    def body(dr_ref, w_ref, o_ref, e_ref, do_ref, dl_ref, dot_ref):
        do32 = _nt(dr_ref[...].astype(BF16), w_ref[...])
        do = do32.astype(BF16)
        do_ref[...] = do
        for cb in range(d // LANES):
            dot_ref[cb * LANES:(cb + 1) * LANES, :] = do32[:, cb * LANES:(cb + 1) * LANES].T.astype(BF16)
        dl_ref[...] = _segsum(do.astype(F32) * o_ref[...].astype(F32), e_ref[...])

    rs = _row_spec(tm, d)
    return pl.pallas_call(
        body, name="attn_out_backward", grid=(lp // tm,),
        in_specs=[rs, _full_spec((d, d)), rs, _full_spec((d, N_Q_HEADS))],
        out_specs=[rs, _row_spec(tm, N_Q_HEADS), pl.BlockSpec((d, tm), lambda i: (0, i))],
        out_shape=[jax.ShapeDtypeStruct((lp, d), BF16), jax.ShapeDtypeStruct((lp, N_Q_HEADS), F32),
                   jax.ShapeDtypeStruct((d, lp), BF16)],
        compiler_params=_params(("parallel",)),
    )(dr, wout, o, e16)


N_CHIPS = 4


def _mesh_pos():
    return lax.axis_index("x"), lax.axis_index("y"), lax.axis_index("c")


def chip_exchange(arrs, scatter, name):
    n = len(arrs)
    hbm = pl.BlockSpec(memory_space=pl.ANY)

    def body(*refs):
        ins, outs = refs[:n], refs[n:2 * n]
        send_sems, recv_sems, loc_sems = refs[2 * n:]
        x, y, c = _mesh_pos()
        me = 2 * x + y
        chips = [(1 - x, y), (x, 1 - y), (1 - x, 1 - y)]
        started = []
        for a in range(n):
            loc = pltpu.make_async_copy(ins[a].at[me] if scatter else ins[a], outs[a].at[me], loc_sems.at[a])
            loc.start()
            started.append(loc)
            for k, (px, py) in enumerate(chips):
                src = ins[a].at[2 * px + py] if scatter else ins[a]
                cp = pltpu.make_async_remote_copy(
                    src_ref=src, dst_ref=outs[a].at[me], send_sem=send_sems.at[3 * a + k], recv_sem=recv_sems.at[3 * a + k],
                    device_id=(px, py, c), device_id_type=MESH)
                cp.start()
                started.append(cp)
        for cp in started:
            cp.wait()

    out_shape = [jax.ShapeDtypeStruct(a.shape if scatter else (N_CHIPS,) + a.shape, a.dtype) for a in arrs]
    return pl.pallas_call(
        body, name=name, in_specs=[hbm] * n, out_specs=[hbm] * n, out_shape=out_shape,
        scratch_shapes=[pltpu.SemaphoreType.DMA((3 * n,)), pltpu.SemaphoreType.DMA((3 * n,)), pltpu.SemaphoreType.DMA((n,))],
    )(*arrs)


def gather_two_level(arrs, name):
    n = len(arrs)
    hbm = pl.BlockSpec(memory_space=pl.ANY)

    def body(*refs):
        ins, outs = refs[:n], refs[n:2 * n]
        ici_send, ici_recv, d2d_send, d2d_recv, loc_sems = refs[2 * n:]
        x, y, c = _mesh_pos()
        me = 2 * x + y
        chips = [(1 - x, y), (x, 1 - y), (1 - x, 1 - y)]
        started = []
        for a in range(n):
            hn = arrs[a].shape[0] // 2
            mine = pl.ds(c * hn, hn)
            loc = pltpu.make_async_copy(ins[a], outs[a].at[me], loc_sems.at[a])
            loc.start()
            started.append(loc)
            first = []
            for k, (px, py) in enumerate(chips):
                cp = pltpu.make_async_remote_copy(
                    src_ref=ins[a].at[mine], dst_ref=outs[a].at[me, mine], send_sem=ici_send.at[3 * a + k],
                    recv_sem=ici_recv.at[3 * a + k], device_id=(px, py, c), device_id_type=MESH)
                cp.start()
                first.append(cp)
            for k, (px, py) in enumerate(chips):
                q = 2 * px + py
                first[k].wait_recv()
                fw = pltpu.make_async_remote_copy(
                    src_ref=outs[a].at[q, mine], dst_ref=outs[a].at[q, mine], send_sem=d2d_send.at[3 * a + k],
                    recv_sem=d2d_recv.at[3 * a + k], device_id=(x, y, 1 - c), device_id_type=MESH)
                fw.start()
                started.append(fw)
            for cp in first:
                cp.wait_send()
        for cp in started:
            cp.wait()

    out_shape = [jax.ShapeDtypeStruct((N_CHIPS,) + a.shape, a.dtype) for a in arrs]
    return pl.pallas_call(
        body, name=name, in_specs=[hbm] * n, out_specs=[hbm] * n, out_shape=out_shape,
        scratch_shapes=[pltpu.SemaphoreType.DMA((3 * n,))] * 4 + [pltpu.SemaphoreType.DMA((n,))],
    )(*arrs)


def sibling_exchange(arrs, name):
    n = len(arrs)
    hbm = pl.BlockSpec(memory_space=pl.ANY)

    def body(*refs):
        ins, outs = refs[:n], refs[n:2 * n]
        send_sems, recv_sems = refs[2 * n:]
        x, y, c = _mesh_pos()
        started = []
        for a in range(n):
            cp = pltpu.make_async_remote_copy(
                src_ref=ins[a], dst_ref=outs[a], send_sem=send_sems.at[a], recv_sem=recv_sems.at[a],
                device_id=(x, y, 1 - c), device_id_type=MESH)
            cp.start()
            started.append(cp)
        for cp in started:
            cp.wait()

    return pl.pallas_call(
        body, name=name, in_specs=[hbm] * n, out_specs=[hbm] * n,
        out_shape=[jax.ShapeDtypeStruct(a.shape, a.dtype) for a in arrs],
        scratch_shapes=[pltpu.SemaphoreType.DMA((n,)), pltpu.SemaphoreType.DMA((n,))],
    )(*arrs)


def _rows_tile(r, c):
    return _pick_tile(r, tuple(t for t in (512, 256, 128, 64, 32, 16, 8) if t * c * 4 <= 2 * 1024 * 1024))


def chip_sum(recv, name):
    _, r, c = recv.shape
    tr = _rows_tile(r, c)

    def body(r_ref, o_ref):
        acc = r_ref[0].astype(F32)
        for q in range(1, N_CHIPS):
            acc = acc + r_ref[q].astype(F32)
        o_ref[...] = acc

    return pl.pallas_call(
        body, name=name, grid=(r // tr,),
        in_specs=[pl.BlockSpec((N_CHIPS, tr, c), lambda i: (0, i, 0))],
        out_specs=pl.BlockSpec((tr, c), lambda i: (i, 0)),
        out_shape=jax.ShapeDtypeStruct((r, c), F32),
        compiler_params=_params(("parallel",)),
    )(recv)


def pair_sum(part, sib, name, dtype=F32):
    r, c = part.shape
    tr = _rows_tile(r, c)

    def body(p_ref, s_ref, o_ref):
        o_ref[...] = (p_ref[...].astype(F32) + s_ref[...].astype(F32)).astype(dtype)

    rs = pl.BlockSpec((tr, c), lambda i: (i, 0))
    return pl.pallas_call(
        body, name=name, grid=(r // tr,), in_specs=[rs] * 2, out_specs=rs,
        out_shape=jax.ShapeDtypeStruct((r, c), dtype), compiler_params=_params(("parallel",)),
    )(part, sib)


def adamw(g, w, m, v, name):
    r, c = w.shape
    tr = _rows_tile(r, c)

    def body(g_ref, w_ref, m_ref, v_ref, d_ref, nm_ref, nv_ref):
        g_ = g_ref[...]
        m_ = ADAM_B1 * m_ref[...] + (1.0 - ADAM_B1) * g_
        v_ = ADAM_B2 * v_ref[...] + (1.0 - ADAM_B2) * (g_ * g_)
        m_hat = m_ / (1.0 - ADAM_B1 ** ADAM_STEP)
        v_hat = v_ / (1.0 - ADAM_B2 ** ADAM_STEP)
        d_ref[...] = -ADAM_LR * (m_hat / (jnp.sqrt(v_hat) + ADAM_EPS) + ADAM_WD * w_ref[...])
        nm_ref[...] = m_
        nv_ref[...] = v_

    rs = pl.BlockSpec((tr, c), lambda i: (i, 0))
    return pl.pallas_call(
        body, name=name, grid=(r // tr,), in_specs=[rs] * 4, out_specs=[rs] * 3,
        out_shape=[jax.ShapeDtypeStruct((r, c), F32)] * 3,
        compiler_params=_params(("parallel",)),
    )(g, w, m, v)


WEIGHTS = ['meta_tokens', 's5_lambda_re', 's5_lambda_im', 's5_log_dt', 's5_b_re', 's5_b_im', 's5_c_re', 's5_c_im', 's5_d',
           's5_w_glu', 's5_w_out', 'attn_w_qkv', 'attn_q_gain', 'attn_k_gain', 'attn_w_out', 'ffn_w_gate', 'ffn_w_up',
           'ffn_w_down', 'ln_gain', 'ln_bias']
BIG = ['s5_w_glu', 's5_w_out', 'attn_w_qkv', 'attn_w_out', 'ffn_w_gate', 'ffn_w_up', 'ffn_w_down']
ROW_SHARDED = {'s5_w_glu', 's5_w_out', 'attn_w_out', 'ffn_w_down'}
SMALL_SHARDED = ['meta_tokens', 'ln_gain', 'ln_bias']
REPLICATED = ['s5_lambda_re', 's5_lambda_im', 's5_log_dt', 's5_b_re', 's5_b_im', 's5_c_re', 's5_c_im', 's5_d',
              'attn_q_gain', 'attn_k_gain']
REP_ALIGN = N_CHIPS * LANES * LANES


def _natural(gathered, row_sharded):
    p, n, a, b = gathered.shape
    if row_sharded:
        return jnp.transpose(gathered, (1, 0, 2, 3)).reshape(n, p * a, b)
    return jnp.transpose(gathered, (1, 2, 0, 3)).reshape(n, a, p * b)


def _shard_major(full, row_sharded):
    n, a, b = full.shape
    if row_sharded:
        return jnp.transpose(full.reshape(n, N_CHIPS, a // N_CHIPS, b), (1, 0, 2, 3))
    return jnp.transpose(full.reshape(n, a, N_CHIPS, b // N_CHIPS), (2, 0, 1, 3))


def _dup_heads(w):
    lead = w.shape[:-1]
    w = w.reshape(lead + (N_KV_HEADS, 1, HEAD_DIM))
    return jnp.broadcast_to(w, lead + (N_KV_HEADS, 2, HEAD_DIM)).reshape(lead + (N_KV_HEADS * 2 * HEAD_DIM,))


def _fold_heads(d):
    lead = d.shape[:-1]
    return d.reshape(lead + (N_KV_HEADS, 2, HEAD_DIM)).sum(axis=-2).reshape(lead + (N_KV_HEADS * HEAD_DIM,))


def _pack_rep(tree):
    flat = jnp.concatenate([tree[n].reshape(-1) for n in REPLICATED])
    pad = _round_up(flat.shape[0], REP_ALIGN) - flat.shape[0]
    return jnp.pad(flat, (0, pad))


def _unpack_rep(flat, like):
    out, off = {}, 0
    for n in REPLICATED:
        size = math.prod(like[n].shape)
        out[n] = flat[off:off + size].reshape(like[n].shape)
        off += size
    return out


def _train_step(x, loss_target, w, mom, vel):
    s = x.shape[1]
    n_valid = N_META + s
    lp = _round_up(n_valid, 2 * LANES)
    nq = N_Q_HEADS * HEAD_DIM
    nkv = N_KV_HEADS * HEAD_DIM

    small = jnp.concatenate([w[n].reshape(-1, w[n].shape[-1]) for n in SMALL_SHARDED], axis=0)
    gathered = gather_two_level([w[n].astype(BF16) for n in BIG] + [small], "gather_weights")
    full = {n: _natural(g, n in ROW_SHARDED) for n, g in zip(BIG, gathered[:-1])}
    small_full = jnp.transpose(gathered[-1], (1, 0, 2)).reshape(small.shape[0], D_MODEL)
    meta_full = small_full[:N_META]
    ln_gain = small_full[N_META:N_META + 2 * DEPTH].reshape(DEPTH, 2, 1, D_MODEL)
    ln_bias = small_full[N_META + 2 * DEPTH:].reshape(DEPTH, 2, 1, D_MODEL)
    wqkv = full['attn_w_qkv']
    w2 = jnp.concatenate([wqkv[..., :nq], _dup_heads(wqkv[..., nq:nq + nkv]), _dup_heads(wqkv[..., nq + nkv:])], axis=-1)

    cos, sin = rope_tables(lp, n_valid)
    e128 = head_sum_matrix()
    e16 = jnp.kron(jnp.eye(N_Q_HEADS, dtype=F32), jnp.ones((HEAD_DIM, 1), F32)).astype(BF16)
    gq = jnp.tile(w['attn_q_gain'], (1, 2))[:, None, :]
    gk = jnp.tile(w['attn_k_gain'], (1, 2))[:, None, :]

    pad_rows = jnp.zeros((lp - n_valid, D_MODEL), F32)
    h = jnp.concatenate([meta_full, x[0], pad_rows], axis=0)
    tgt = jnp.concatenate([jnp.zeros((N_META, D_MODEL), F32), loss_target[0], pad_rows], axis=0)

    saved = []
    s5_names = ['s5_lambda_re', 's5_lambda_im', 's5_log_dt', 's5_b_re', 's5_b_im', 's5_c_re', 's5_c_im']
    for i in range(DEPTH):
        j = i // 2
        sv = {'h': h}
        if i % 2 == 0:
            ops, sv['prep_vjp'] = jax.vjp(s5_prep, *[w[n][j] for n in s5_names])
            m_, wx_, ci_, at_ = ops
            two = lambda t: t.reshape((2 * S5_NJ,) + t.shape[2:])
            sv['ops'] = (blockdiag_expand(m_, S5_CH, S5_CH, "s5_expand_m"),
                         blockdiag_expand(two(wx_), S5_CH, S5_STATE, "s5_expand_wx").reshape(2, S5_NJ, S5_W, S5_W),
                         blockdiag_expand(two(ci_), S5_STATE, S5_CH, "s5_expand_ci").reshape(2, S5_NJ, S5_W, S5_W), at_)
            y, sv['lhs'], sv['sp'], sv['sn'] = s5_forward(h, *sv['ops'], n_valid)
            sv['v'], sv['t'], sv['g'], sv['z'] = glu_forward(y, h, w['s5_d'][j][None], full['s5_w_glu'][j])
            sv['r1'], h1 = proj_ln_forward(sv['z'], full['s5_w_out'][j], h, ln_gain[i, 0], ln_bias[i, 0], "s5_out_ln")
        else:
            sv['raw'], sv['qs'], sv['k2'], sv['v2'], sv['qst'], v2t = qkv_forward(h, w2[j], gq[j], gk[j], cos, sin, e128)
            sv['o'], lse = attn_forward_t(sv['qs'], sv['k2'], v2t, n_valid)
            sv['lse'] = lse.reshape(N_Q_HEADS, lp).T
            sv['r1'], h1 = proj_ln_forward(sv['o'], full['attn_w_out'][j], h, ln_gain[i, 0], ln_bias[i, 0], "attn_out_ln")
        sv['h1'] = h1
        sv['a'], sv['b'], sv['f'] = ffn_up_forward(h1, full['ffn_w_gate'][i], full['ffn_w_up'][i])
        sv['r2'], h = proj_ln_forward(sv['f'], full['ffn_w_down'][i], h1, ln_gain[i, 1], ln_bias[i, 1], "ffn_down_ln")
        saved.append(sv)

    dh, loss_part = loss_backward(h, tgt, n_valid)
    loss = lax.psum(jnp.sum(loss_part), ("x", "y", "c"))

    gfull = {n: [None] * w[n].shape[0] for n in BIG}
    d_ln_gain = [[None, None] for _ in range(DEPTH)]
    d_ln_bias = [[None, None] for _ in range(DEPTH)]
    grep = {n: [None] * w[n].shape[0] for n in REPLICATED}
    for i in reversed(range(DEPTH)):
        j = i // 2
        sv = saved[i]
        dr2, s2 = ln_backward(dh, sv['r2'], ln_gain[i, 1])
        d_ln_gain[i][1], d_ln_bias[i][1] = s2[0], s2[1]
        da, db = ffn_backward_act(dr2, full['ffn_w_down'][i], sv['a'], sv['b'])
        gfull['ffn_w_down'][i] = mm_tn(sv['f'], dr2, "grad_ffn_down")
        dh1 = resid_nt(dr2, [da, db], [full['ffn_w_gate'][i], full['ffn_w_up'][i]], "ffn_backward_x")
        gfull['ffn_w_gate'][i] = mm_tn(sv['h1'], da, "grad_ffn_gate")
        gfull['ffn_w_up'][i] = mm_tn(sv['h1'], db, "grad_ffn_up")
        dr1, s1 = ln_backward(dh1, sv['r1'], ln_gain[i, 0])
        d_ln_gain[i][0], d_ln_bias[i][0] = s1[0], s1[1]
        if i % 2 == 0:
            dt, dgd = glu_backward1(dr1, full['s5_w_out'][j], sv['g'], sv['t'])
            gfull['s5_w_out'][j] = mm_tn(sv['z'], dr1, "grad_s5_out")
            dv, dhs, sd = glu_backward2(dt, dgd, full['s5_w_glu'][j], sv['v'], sv['h'], w['s5_d'][j][None], dr1)
            grep['s5_d'][j] = sd[0]
            gfull['s5_w_glu'][j] = mm_tn(sv['g'], dt, "grad_s5_glu")
            dh, ldy, dxf, dxr, daf, dar = s5_backward(dv, dhs, *sv['ops'], sv['sp'], sv['sn'], n_valid)
            dm = bmm_tn_compact(sv['lhs'], ldy, S5_CH, S5_CH, "grad_s5_m")
            dwx = jnp.stack([bmm_tn_compact(sv['lhs'], dxf, S5_CH, S5_STATE, "grad_s5_wxf"),
                             bmm_tn_compact(sv['lhs'], dxr, S5_CH, S5_STATE, "grad_s5_wxr")])
            dci = jnp.stack([bmm_tn_compact(sv['sp'], ldy, S5_STATE, S5_CH, "grad_s5_cif"),
                             bmm_tn_compact(sv['sn'], ldy, S5_STATE, S5_CH, "grad_s5_cir")])
            dps = sv['prep_vjp']((dm, dwx, dci, jnp.stack([daf, dar])))
            for n, g in zip(s5_names, dps):
                grep[n][j] = g
        else:
            do, delta, dot = attn_out_backward(dr1, full['attn_w_out'][j], sv['o'], e16)
            gfull['attn_w_out'][j] = mm_tn(sv['o'], dr1, "grad_attn_out")
            dq, dk2, dv2 = attn_backward(sv['qs'], sv['qst'], sv['k2'], sv['v2'], do, dot, sv['lse'], delta, n_valid)
            draw, gs = qkv_backward(dq, dk2, dv2, sv['raw'], gq[j], gk[j], cos, sin, e128)
            grep['attn_q_gain'][j] = gs[0, :HEAD_DIM] + gs[0, HEAD_DIM:]
            grep['attn_k_gain'][j] = gs[1, :HEAD_DIM] + gs[1, HEAD_DIM:]
            dh = resid_nt(dr1, [draw], [w2[j]], "attn_backward_x")
            dw2 = mm_tn(sv['h'], draw, "grad_attn_qkv")
            kq = N_QB * LANES
            kk = N_KB * LANES
            gfull['attn_w_qkv'][j] = jnp.concatenate(
                [dw2[:, :kq], _fold_heads(dw2[:, kq:kq + kk]), _fold_heads(dw2[:, kq + kk:])], axis=1)
    grad_x = dh[N_META:n_valid][None]

    core = lax.axis_index("c")
    contrib = [_shard_major(jnp.stack(gfull[n]), n in ROW_SHARDED) for n in BIG]
    small_g = jnp.concatenate([dh[:N_META], jnp.stack([g for pair in d_ln_gain for g in pair]),
                               jnp.stack([g for pair in d_ln_bias for g in pair])], axis=0)
    contrib.append(jnp.transpose(small_g.reshape(-1, N_CHIPS, D_MODEL // N_CHIPS), (1, 0, 2)))
    rep_g = _pack_rep({n: jnp.stack(grep[n]) for n in REPLICATED})
    contrib.append(rep_g.reshape(N_CHIPS, -1, LANES))
    names = BIG + ['small', 'rep']
    wire = [BF16] * len(BIG) + [F32, F32]
    keep, give = [], []
    for t, dt in zip(contrib, wire):
        hn = t.shape[1] // 2
        keep.append(lax.dynamic_slice_in_dim(t, core * hn, hn, axis=1))
        give.append(lax.dynamic_slice_in_dim(t, (1 - core) * hn, hn, axis=1).astype(dt))
    got = sibling_exchange(give, "sibling_contrib")
    two_d = lambda t: t.reshape(-1, t.shape[-1])
    pair = [pair_sum(two_d(a), two_d(b), "pair_sum_" + n, dt).reshape(a.shape)
            for n, a, b, dt in zip(names, keep, got, wire)]
    recv = chip_exchange(pair, True, "scatter_grads")
    halves = [chip_sum(r.reshape(N_CHIPS, -1, r.shape[-1]), "chip_sum_" + n) for n, r in zip(names, recv)]
    others = sibling_exchange(halves, "sibling_halves")
    grads = [jnp.where(core == 0, jnp.concatenate([a, b], axis=0), jnp.concatenate([b, a], axis=0))
             for a, b in zip(halves, others)]

    out = {}

    def update(n, g, wn, mn, vn):
        shape = wn.shape
        flat = (-1, shape[-1])
        d, nm, nv = adamw(g, wn.reshape(flat), mn.reshape(flat), vn.reshape(flat), "adamw_" + n)
        return tuple(t.reshape(shape) for t in (g, d, nm, nv))

    for n, g in zip(BIG, grads):
        out[n] = update(n, g, w[n], mom[n], vel[n])
    cat = lambda tree: jnp.concatenate([tree[n].reshape(-1, tree[n].shape[-1]) for n in SMALL_SHARDED], axis=0)
    sm = update("small", grads[-2], cat(w), cat(mom), cat(vel))
    off = 0
    for n in SMALL_SHARDED:
        rows = math.prod(w[n].shape[:-1])
        out[n] = tuple(t[off:off + rows].reshape(w[n].shape) for t in sm)
        off += rows
    rep_all = chip_exchange([grads[-1]], False, "gather_rep")[0].reshape(-1, LANES)
    rp = update("rep", rep_all, _pack_rep(w).reshape(-1, LANES), _pack_rep(mom).reshape(-1, LANES),
                _pack_rep(vel).reshape(-1, LANES))
    unpacked = [_unpack_rep(t.reshape(-1), w) for t in rp]
    for n in REPLICATED:
        out[n] = tuple(u[n] for u in unpacked)

    return (loss, grad_x, *[out[n][0] for n in WEIGHTS], *[out[n][1] for n in WEIGHTS],
            *[out[n][2] for n in WEIGHTS], *[out[n][3] for n in WEIGHTS])


def kernel(x, meta_tokens, s5_lambda_re, s5_lambda_im, s5_log_dt, s5_b_re, s5_b_im, s5_c_re, s5_c_im, s5_d, s5_w_glu, s5_w_out, attn_w_qkv, attn_q_gain, attn_k_gain, attn_w_out, ffn_w_gate, ffn_w_up, ffn_w_down, ln_gain, ln_bias, loss_target, m_meta_tokens, m_s5_lambda_re, m_s5_lambda_im, m_s5_log_dt, m_s5_b_re, m_s5_b_im, m_s5_c_re, m_s5_c_im, m_s5_d, m_s5_w_glu, m_s5_w_out, m_attn_w_qkv, m_attn_q_gain, m_attn_k_gain, m_attn_w_out, m_ffn_w_gate, m_ffn_w_up, m_ffn_w_down, m_ln_gain, m_ln_bias, v_meta_tokens, v_s5_lambda_re, v_s5_lambda_im, v_s5_log_dt, v_s5_b_re, v_s5_b_im, v_s5_c_re, v_s5_c_im, v_s5_d, v_s5_w_glu, v_s5_w_out, v_attn_w_qkv, v_attn_q_gain, v_attn_k_gain, v_attn_w_out, v_ffn_w_gate, v_ffn_w_up, v_ffn_w_down, v_ln_gain, v_ln_bias):
    given = locals()
    w = {n: given[n] for n in WEIGHTS}
    mom = {n: given["m_" + n] for n in WEIGHTS}
    vel = {n: given["v_" + n] for n in WEIGHTS}
    return _train_step(x, loss_target, w, mom, vel)
```

```python
import math

import jax
import jax.numpy as jnp
from jax import lax
from jax.experimental import pallas as pl
from jax.experimental.pallas import tpu as pltpu

F32 = jnp.float32
BF16 = jnp.bfloat16
MESH = pl.DeviceIdType.MESH

D_MODEL = 1024
N_META = 16
GRID_W = 64
HEAD_DIM = 64
N_Q_HEADS = 16
N_KV_HEADS = 4
ROPE_THETA = 10000.0
QK_EPS = 1e-6
S5_CH = 16
S5_GROUPS = 64
S5_STATE = 64
D_FF = 2816
LN_EPS = 1e-5
DEPTH = 4
ALPHA = (2.0 * DEPTH) ** 0.25
ADAM_LR, ADAM_B1, ADAM_B2, ADAM_EPS, ADAM_WD, ADAM_STEP = 0.001, 0.9, 0.999, 1e-08, 0.01, 10

LANES = 128
SUBLANES = 8
VMEM_LIMIT = 56 * 1024 * 1024

S5_T = 8
S5_GB = LANES // S5_CH
S5_NJ = S5_GROUPS // S5_GB
S5_W = S5_T * LANES
S5_SW = 2 * S5_GB * S5_STATE
S5_HALF = S5_SW // 2


def _round_up(a, b):
    return -(-a // b) * b


def _pick_tile(n, prefs):
    for t in prefs:
        if n % t == 0:
            return t
    return n


def _params(sem=None):
    kw = dict(vmem_limit_bytes=VMEM_LIMIT)
    if sem is not None:
        kw["dimension_semantics"] = sem
    return pltpu.CompilerParams(**kw)


def _dot(a, b, dims):
    return lax.dot_general(a, b, (dims, ((), ())), preferred_element_type=F32)


def _nn(a, b):
    return _dot(a, b, ((1,), (0,)))


def _nt(a, b):
    return _dot(a, b, ((1,), (1,)))


def _tn(a, b):
    return _dot(a, b, ((0,), (0,)))


def _compact(w):
    g, a0, a1, b0, b1 = w.shape
    w = jnp.transpose(w.reshape(S5_NJ, S5_GB, a0, a1, b0, b1), (0, 2, 1, 3, 4, 5))
    return w.reshape(S5_NJ, a0 * S5_GB * a1, b0 * b1)


def s5_prep(lam_re, lam_im, log_dt, b_re, b_im, c_re, c_im):
    hi = lax.Precision.HIGHEST
    t = S5_T
    dt = jnp.exp(log_dt)[..., None]
    taus = jnp.arange(t + 1, dtype=F32)[:, None, None, None]
    mag = jnp.exp(lam_re * dt)
    ang = lam_im * dt
    pr = jnp.concatenate([jnp.ones_like(mag)[None], (mag * jnp.cos(ang))[None],
                          jnp.exp(lam_re * dt * taus[2:]) * jnp.cos(ang * taus[2:])], axis=0)
    pi = jnp.concatenate([jnp.zeros_like(mag)[None], (mag * jnp.sin(ang))[None],
                          jnp.exp(lam_re * dt * taus[2:]) * jnp.sin(ang * taus[2:])], axis=0)
    abr, abi = pr[1], pi[1]
    nr, ni = abr - 1.0, abi
    den = lam_re * lam_re + lam_im * lam_im
    cr = (nr * lam_re + ni * lam_im) / den
    ci_ = (ni * lam_re - nr * lam_im) / den
    bbr = cr[..., None] * b_re - ci_[..., None] * b_im
    bbi = cr[..., None] * b_im + ci_[..., None] * b_re
    er = c_re[None] * pr[:, :, :, None, :] - c_im[None] * pi[:, :, :, None, :]
    ei = c_re[None] * pi[:, :, :, None, :] + c_im[None] * pr[:, :, :, None, :]
    nd, ng, ch = er.shape[1], er.shape[2], er.shape[3]
    lhs = jnp.concatenate([er[:t], -ei[:t]], axis=-1)
    lhs = jnp.transpose(lhs, (1, 2, 0, 3, 4)).reshape(nd, ng, t * ch, 2 * S5_STATE)
    rhs = jnp.concatenate([bbr, bbi], axis=-2)
    kk = jnp.einsum("dgmp,dgpc->dgmc", lhs, rhs, precision=hi)
    kk = jnp.transpose(kk.reshape(nd, ng, t, ch, ch), (2, 0, 1, 3, 4))
    zero = jnp.zeros_like(kk[0, 0])
    mg = jnp.stack([jnp.stack([(kk[i - s, 0] if i > s else zero) + (kk[s - i, 1] if s > i else zero)
                               + ((kk[0, 0] + kk[0, 1]) if i == s else zero) for i in range(t)])
                    for s in range(t)])
    mg = jnp.transpose(mg, (2, 0, 4, 1, 3))
    m = _compact(mg)
    pw_f = jnp.stack([pr[t - 1 - s, 0] for s in range(t)]), jnp.stack([pi[t - 1 - s, 0] for s in range(t)])
    pw_r = jnp.stack([pr[s, 1] for s in range(t)]), jnp.stack([pi[s, 1] for s in range(t)])
    wx = []
    for d, (qr, qi) in enumerate((pw_f, pw_r)):
        wr = qr[..., None] * bbr[d][None] - qi[..., None] * bbi[d][None]
        wi = qr[..., None] * bbi[d][None] + qi[..., None] * bbr[d][None]
        w = jnp.stack([wr, wi], axis=0)
        w = jnp.transpose(w, (2, 1, 4, 0, 3))
        wx.append(_compact(w))
    ci = []
    for d in range(2):
        exps = [i + 1 for i in range(t)] if d == 0 else [t - i for i in range(t)]
        e_r = jnp.stack([er[e, d] for e in exps])
        e_i = jnp.stack([ei[e, d] for e in exps])
        w = jnp.stack([e_r, -e_i], axis=0)
        w = jnp.transpose(w, (2, 0, 4, 1, 3))
        ci.append(_compact(w))
    at = jnp.stack([pr[t], pi[t]], axis=1)
    at = at.reshape(2, 2, S5_NJ, S5_GB * S5_STATE)
    at = jnp.transpose(at, (0, 2, 1, 3)).reshape(2, S5_NJ, 1, S5_SW)
    return m, jnp.stack(wx), jnp.stack(ci), at


def _chunk_rows(ref, nc):
    return jnp.concatenate([ref[pl.ds(s, nc, stride=S5_T), :] for s in range(S5_T)], axis=1)


def _cmul(ar, ai, sr, si):
    return ar * sr - ai * si, ar * si + ai * sr


def _scan_tiles(nc, reverse, step):
    nt = nc // SUBLANES

    def body(it, carry):
        tix = (nt - 1 - it) if reverse else it
        k0 = pl.multiple_of(tix * SUBLANES, SUBLANES)
        return step(k0, carry)

    return body, nt


def _s5_specs(nc):
    hbm = pl.BlockSpec(memory_space=pl.ANY)
    aspec = pl.BlockSpec((1, 1, S5_SW), lambda j: (j, 0, 0))
    cspec = pl.BlockSpec((1, nc, S5_W), lambda j: (j, 0, 0))
    return hbm, aspec, cspec


def _s5_fetch(j, tok_hbm, w_hbms, tok_s, w_s, sems):
    cols = pl.ds(pl.multiple_of(j * LANES, LANES), LANES)
    cps = [pltpu.make_async_copy(tok_hbm.at[:, cols], tok_s, sems.at[0])]
    for i, w in enumerate(w_hbms):
        cps.append(pltpu.make_async_copy(w.at[j], w_s.at[i], sems.at[1 + i]))
    for cp in cps:
        cp.start()
    return cols, cps


def s5_forward(u, m, wx, ci, at, n_valid, prefetch=()):
    lp = u.shape[0]
    nc = lp // S5_T
    nvc = n_valid // S5_T
    npf = len(prefetch)
    pf_specs, pf_shapes, pf_sems = _prefetch_specs(prefetch)

    def body(*refs):
        u_hbm, m_hbm, wxf_hbm, wxr_hbm, cif_hbm, cir_hbm, atf_ref, atr_ref = refs[:8]
        pf_ins = refs[8:8 + npf]
        y_hbm, lhs_ref, sp_ref, sn_ref = refs[8 + npf:12 + npf]
        pf_outs = refs[12 + npf:12 + 2 * npf]
        tok_s, w_s, xf_s, xr_s, sems = refs[12 + 2 * npf:17 + 2 * npf]
        j = pl.program_id(0)
        _prefetch_run(j == 0, j == S5_NJ - 1, pf_ins, pf_outs, refs[17 + 2 * npf:])
        cols, cps = _s5_fetch(j, u_hbm, (m_hbm, wxf_hbm, wxr_hbm, cif_hbm, cir_hbm), tok_s, w_s, sems)
        cps[0].wait()
        lhs = _chunk_rows(tok_s, nc)
        rows = lax.broadcasted_iota(jnp.int32, lhs.shape, 0)
        lhs = jnp.where(rows < nvc, lhs, 0.0).astype(BF16)
        lhs_ref[0] = lhs
        cps[2].wait()
        cps[3].wait()
        xf_s[...] = _nn(lhs, w_s[1])
        xr_s[...] = _nn(lhs, w_s[2])
        afr, afi = atf_ref[0, :, :S5_HALF], atf_ref[0, :, S5_HALF:]
        arr, ari = atr_ref[0, :, :S5_HALF], atr_ref[0, :, S5_HALF:]

        def scan_step(x_s, ar, ai, descending):
            def step(k0, carry):
                sr, si = carry
                x = x_s[pl.ds(k0, SUBLANES), :]
                outs = [None] * SUBLANES
                order = reversed(range(SUBLANES)) if descending else range(SUBLANES)
                for r in order:
                    outs[r] = jnp.concatenate([sr, si], axis=1)
                    nr, ni = _cmul(ar, ai, sr, si)
                    sr = nr + x[r:r + 1, :S5_HALF]
                    si = ni + x[r:r + 1, S5_HALF:]
                x_s[pl.ds(k0, SUBLANES), :] = jnp.concatenate(outs, axis=0)
                return sr, si
            return step

        zero = jnp.zeros((1, S5_HALF), F32)
        fb, nt = _scan_tiles(nc, False, scan_step(xf_s, afr, afi, False))
        lax.fori_loop(0, nt, fb, (zero, zero))
        rb, nt = _scan_tiles(nc, True, scan_step(xr_s, arr, ari, True))
        lax.fori_loop(0, nt, rb, (zero, zero))
        sp = xf_s[...].astype(BF16)
        sn = xr_s[...].astype(BF16)
        sp_ref[0] = sp
        sn_ref[0] = sn
        cps[1].wait()
        cps[4].wait()
        cps[5].wait()
        y = _nn(lhs, w_s[0]) + _nn(sp, w_s[3]) + _nn(sn, w_s[4])
        for i in range(S5_T):
            tok_s[pl.ds(i, nc, stride=S5_T), :] = y[:, i * LANES:(i + 1) * LANES]
        pltpu.sync_copy(tok_s, y_hbm.at[:, cols])

    hbm, aspec, cspec = _s5_specs(nc)
    return pl.pallas_call(
        body, name="s5_forward", grid=(S5_NJ,),
        in_specs=[hbm] * 6 + [aspec, aspec] + pf_specs,
        out_specs=[hbm, cspec, cspec, cspec] + pf_specs,
        out_shape=[jax.ShapeDtypeStruct((lp, D_MODEL), F32)] + [jax.ShapeDtypeStruct((S5_NJ, nc, S5_W), BF16)] * 3 + pf_shapes,
        scratch_shapes=[pltpu.VMEM((lp, LANES), F32), pltpu.VMEM((5, S5_W, S5_W), BF16),
                        pltpu.VMEM((nc, S5_SW), F32), pltpu.VMEM((nc, S5_SW), F32), pltpu.SemaphoreType.DMA((6,))] + pf_sems,
        compiler_params=_params(("arbitrary",)),
    )(u, m, wx[0], wx[1], ci[0], ci[1], at[0], at[1], *prefetch)


def s5_backward(dy, dhs, m, wx, ci, at, sp, sn, n_valid):
    lp = dy.shape[0]
    nc = lp // S5_T
    nvc = n_valid // S5_T

    def body(dy_hbm, dhs_hbm, m_hbm, wxf_hbm, wxr_hbm, cif_hbm, cir_hbm, atf_ref, atr_ref, sp_ref, sn_ref,
             dh_hbm, ldy_ref, dxf_ref, dxr_ref, daf_ref, dar_ref, tok_s, w_s, gf_s, gr_s, sems):
        j = pl.program_id(0)
        cols, cps = _s5_fetch(j, dy_hbm, (m_hbm, wxf_hbm, wxr_hbm, cif_hbm, cir_hbm), tok_s, w_s, sems)
        cps[0].wait()
        ldy = _chunk_rows(tok_s, nc)
        rows = lax.broadcasted_iota(jnp.int32, ldy.shape, 0)
        ldy = jnp.where(rows < nvc, ldy, 0.0).astype(BF16)
        ldy_ref[0] = ldy
        resid = pltpu.make_async_copy(dhs_hbm.at[:, cols], tok_s, sems.at[0])
        resid.start()
        cps[4].wait()
        cps[5].wait()
        gf_s[...] = _nt(ldy, w_s[3])
        gr_s[...] = _nt(ldy, w_s[4])
        afr, afi = atf_ref[0, :, :S5_HALF], atf_ref[0, :, S5_HALF:]
        arr, ari = atr_ref[0, :, :S5_HALF], atr_ref[0, :, S5_HALF:]

        def adj_step(g_s, s_ref, ar, ai, descending):
            def step(k0, carry):
                gr_, gi_, dr_, di_ = carry
                g = g_s[pl.ds(k0, SUBLANES), :]
                p = s_ref[0, pl.ds(k0, SUBLANES), :].astype(F32)
                outs = [None] * SUBLANES
                order = reversed(range(SUBLANES)) if descending else range(SUBLANES)
                for r in order:
                    outs[r] = jnp.concatenate([gr_, gi_], axis=1)
                    pr_, pi_ = p[r:r + 1, :S5_HALF], p[r:r + 1, S5_HALF:]
                    dr_ = dr_ + gr_ * pr_ + gi_ * pi_
                    di_ = di_ + gi_ * pr_ - gr_ * pi_
                    nr, ni = _cmul(ar, -ai, gr_, gi_)
                    gr_ = nr + g[r:r + 1, :S5_HALF]
                    gi_ = ni + g[r:r + 1, S5_HALF:]
                g_s[pl.ds(k0, SUBLANES), :] = jnp.concatenate(outs, axis=0)
                return gr_, gi_, dr_, di_
            return step

        zero = jnp.zeros((1, S5_HALF), F32)
        fb, nt = _scan_tiles(nc, True, adj_step(gf_s, sp_ref, afr, afi, True))
        _, _, dr_, di_ = lax.fori_loop(0, nt, fb, (zero,) * 4)
        daf_ref[0] = jnp.concatenate([dr_, di_], axis=1)
        rb, nt = _scan_tiles(nc, False, adj_step(gr_s, sn_ref, arr, ari, False))
        _, _, dr_, di_ = lax.fori_loop(0, nt, rb, (zero,) * 4)
        dar_ref[0] = jnp.concatenate([dr_, di_], axis=1)
        dxf = gf_s[...].astype(BF16)
        dxr = gr_s[...].astype(BF16)
        dxf_ref[0] = dxf
        dxr_ref[0] = dxr
        cps[1].wait()
        cps[2].wait()
        cps[3].wait()
        du = _nt(ldy, w_s[0]) + _nt(dxf, w_s[1]) + _nt(dxr, w_s[2])
        rows = lax.broadcasted_iota(jnp.int32, du.shape, 0)
        du = jnp.where(rows < nvc, du, 0.0)
        resid.wait()
        for s in range(S5_T):
            tok_s[pl.ds(s, nc, stride=S5_T), :] += du[:, s * LANES:(s + 1) * LANES]
        pltpu.sync_copy(tok_s, dh_hbm.at[:, cols])

    hbm, aspec, cspec = _s5_specs(nc)
    return pl.pallas_call(
        body, name="s5_backward", grid=(S5_NJ,),
        in_specs=[hbm] * 7 + [aspec, aspec, cspec, cspec],
        out_specs=[hbm, cspec, cspec, cspec, aspec, aspec],
        out_shape=[jax.ShapeDtypeStruct((lp, D_MODEL), F32)] + [jax.ShapeDtypeStruct((S5_NJ, nc, S5_W), BF16)] * 3
        + [jax.ShapeDtypeStruct((S5_NJ, 1, S5_SW), F32)] * 2,
        scratch_shapes=[pltpu.VMEM((lp, LANES), F32), pltpu.VMEM((5, S5_W, S5_W), BF16),
                        pltpu.VMEM((nc, S5_SW), F32), pltpu.VMEM((nc, S5_SW), F32), pltpu.SemaphoreType.DMA((6,))],
        compiler_params=_params(("arbitrary",)),
    )(dy, dhs, m, wx[0], wx[1], ci[0], ci[1], at[0], at[1], sp, sn)


S5_CW = LANES


def _replicate_matrix(b1):
    b0n = S5_CW // b1
    eye0 = jnp.eye(b0n, dtype=F32)
    eye1 = jnp.eye(b1, dtype=F32)
    r = jnp.einsum("ab,cd->acbd", eye0, eye1)[:, :, :, None, :]
    r = jnp.broadcast_to(r, (b0n, b1, b0n, S5_GB, b1))
    return r.reshape(S5_CW, b0n * S5_GB * b1).astype(BF16)


def _same_group(a1, b1):
    rg = (lax.broadcasted_iota(jnp.int32, (S5_W, S5_W), 0) // a1) % S5_GB
    cg = (lax.broadcasted_iota(jnp.int32, (S5_W, S5_W), 1) // b1) % S5_GB
    return rg == cg


def blockdiag_expand(compact, a1, b1, name):
    nj = compact.shape[0]

    def body(c_ref, r_ref, o_ref):
        rep = _nn(c_ref[0].astype(BF16), r_ref[...])
        o_ref[0] = jnp.where(_same_group(a1, b1), rep, 0.0).astype(BF16)

    return pl.pallas_call(
        body, name=name, grid=(nj,),
        in_specs=[pl.BlockSpec((1, S5_W, S5_CW), lambda j: (j, 0, 0)), _full_spec((S5_CW, S5_W))],
        out_specs=pl.BlockSpec((1, S5_W, S5_W), lambda j: (j, 0, 0)),
        out_shape=jax.ShapeDtypeStruct((nj, S5_W, S5_W), BF16),
        compiler_params=_params(("parallel",)),
    )(compact, _replicate_matrix(b1))


def bmm_tn_compact(a, b, a1, b1, name):
    nj, k, wa = a.shape
    wb = b.shape[2]

    def body(a_ref, b_ref, r_ref, o_ref):
        prod = jnp.where(_same_group(a1, b1), _tn(a_ref[0], b_ref[0]), 0.0)
        hi = prod.astype(BF16)
        lo = (prod - hi.astype(F32)).astype(BF16)
        o_ref[0] = _nt(hi, r_ref[...]) + _nt(lo, r_ref[...])

    return pl.pallas_call(
        body, name=name, grid=(nj,),
        in_specs=[pl.BlockSpec((1, k, wa), lambda j: (j, 0, 0)), pl.BlockSpec((1, k, wb), lambda j: (j, 0, 0)),
                  _full_spec((S5_CW, S5_W))],
        out_specs=pl.BlockSpec((1, wa, S5_CW), lambda j: (j, 0, 0)),
        out_shape=jax.ShapeDtypeStruct((nj, wa, S5_CW), F32),
        compiler_params=_params(("parallel",)),
    )(a, b, _replicate_matrix(b1))


def _tm(lp):
    return _pick_tile(lp, (768, 256))


def _row_spec(tm, width):
    return pl.BlockSpec((tm, width), lambda i: (i, 0))


def _full_spec(shape):
    return pl.BlockSpec(shape, lambda *_: (0,) * len(shape))


def _gelu(v):
    return 0.5 * v * (1.0 + lax.erf(v * (2.0 ** -0.5)))


def _gelu_grad(v):
    return 0.5 * (1.0 + lax.erf(v * (2.0 ** -0.5))) + v * jnp.exp(-0.5 * v * v) * (2.0 * math.pi) ** -0.5


def _layer_norm(r, gain, bias):
    mean = jnp.mean(r, axis=-1, keepdims=True)
    c = r - mean
    var = jnp.mean(c * c, axis=-1, keepdims=True)
    return c * lax.rsqrt(var + LN_EPS) * gain + bias


def glu_forward(y, h, dvec, wglu):
    lp, d = y.shape
    tm = _tm(lp)

    def body(y_ref, h_ref, d_ref, w_ref, v_ref, t_ref, g_ref, z_ref):
        v = y_ref[...] + d_ref[...] * h_ref[...]
        g = _gelu(v)
        gb = g.astype(BF16)
        t = _nn(gb, w_ref[...])
        v_ref[...] = v
        t_ref[...] = t
        g_ref[...] = gb
        z_ref[...] = (g * jax.nn.sigmoid(t)).astype(BF16)

    rs = _row_spec(tm, d)
    return pl.pallas_call(
        body, name="glu_forward", grid=(lp // tm,),
        in_specs=[rs, rs, _full_spec((1, d)), _full_spec((d, d))],
        out_specs=[rs, rs, rs, rs],
        out_shape=[jax.ShapeDtypeStruct((lp, d), F32)] * 2 + [jax.ShapeDtypeStruct((lp, d), BF16)] * 2,
        compiler_params=_params(("parallel",)),
    )(y, h, dvec, wglu)


def proj_ln_forward(z, w, h, gain, bias, name):
    lp, k = z.shape
    d = w.shape[1]
    tm = _tm(lp)

    def body(z_ref, w_ref, h_ref, g_ref, b_ref, r_ref, o_ref):
        r = ALPHA * h_ref[...] + _nn(z_ref[...], w_ref[...])
        r_ref[...] = r
        o_ref[...] = _layer_norm(r, g_ref[...], b_ref[...])

    rs = _row_spec(tm, d)
    return pl.pallas_call(
        body, name=name, grid=(lp // tm,),
        in_specs=[_row_spec(tm, k), _full_spec((k, d)), rs, _full_spec((1, d)), _full_spec((1, d))],
        out_specs=[rs, rs],
        out_shape=[jax.ShapeDtypeStruct((lp, d), F32)] * 2,
        compiler_params=_params(("parallel",)),
    )(z, w, h, gain, bias)


FFN_NB = 1408


def ffn_up_forward(h, wg, wu):
    lp, d = h.shape
    dff = wg.shape[1]
    tm = _tm(lp)

    def body(h_ref, wg_ref, wu_ref, a_ref, b_ref, f_ref):
        hb = h_ref[...].astype(BF16)
        a = _nn(hb, wg_ref[...])
        b = _nn(hb, wu_ref[...])
        a_ref[...] = a.astype(BF16)
        b_ref[...] = b.astype(BF16)
        f_ref[...] = (a * jax.nn.sigmoid(a) * b).astype(BF16)

    ws = pl.BlockSpec((d, FFN_NB), lambda n, i: (0, n))
    os_ = pl.BlockSpec((tm, FFN_NB), lambda n, i: (i, n))
    return pl.pallas_call(
        body, name="ffn_up_forward", grid=(dff // FFN_NB, lp // tm),
        in_specs=[pl.BlockSpec((tm, d), lambda n, i: (i, 0)), ws, ws],
        out_specs=[os_, os_, os_],
        out_shape=[jax.ShapeDtypeStruct((lp, dff), BF16)] * 3,
        compiler_params=_params(("parallel", "parallel")),
    )(h, wg, wu)


def ln_backward(dh, r, gain):
    lp, d = dh.shape
    tm = _tm(lp)

    def body(dh_ref, r_ref, g_ref, dr_ref, s_ref):
        r_ = r_ref[...]
        dh_ = dh_ref[...]
        mean = jnp.mean(r_, axis=-1, keepdims=True)
        c = r_ - mean
        var = jnp.mean(c * c, axis=-1, keepdims=True)
        rstd = lax.rsqrt(var + LN_EPS)
        xh = c * rstd
        dxh = dh_ * g_ref[...]
        m1 = jnp.mean(dxh, axis=-1, keepdims=True)
        m2 = jnp.mean(dxh * xh, axis=-1, keepdims=True)
        dr_ref[...] = rstd * (dxh - m1 - xh * m2)

        @pl.when(pl.program_id(0) == 0)
        def _():
            s_ref[...] = jnp.zeros_like(s_ref)

        s_ref[0:1, :] += jnp.sum(dh_ * xh, axis=0, keepdims=True)
        s_ref[1:2, :] += jnp.sum(dh_, axis=0, keepdims=True)

    rs = _row_spec(tm, d)
    return pl.pallas_call(
        body, name="ln_backward", grid=(lp // tm,),
        in_specs=[rs, rs, _full_spec((1, d))],
        out_specs=[rs, _full_spec((SUBLANES, d))],
        out_shape=[jax.ShapeDtypeStruct((lp, d), F32), jax.ShapeDtypeStruct((SUBLANES, d), F32)],
        compiler_params=_params(("arbitrary",)),
    )(dh, r, gain)


def ffn_backward_act(dr, wd, a, b):
    lp, d = dr.shape
    dff = wd.shape[0]
    tm = _tm(lp)

    def body(dr_ref, wd_ref, a_ref, b_ref, da_ref, db_ref):
        df = _nt(dr_ref[...].astype(BF16), wd_ref[...])
        a_ = a_ref[...].astype(F32)
        b_ = b_ref[...].astype(F32)
        sg = jax.nn.sigmoid(a_)
        da_ref[...] = (df * b_ * sg * (1.0 + a_ * (1.0 - sg))).astype(BF16)
        db_ref[...] = (df * a_ * sg).astype(BF16)

    os_ = pl.BlockSpec((tm, FFN_NB), lambda n, i: (i, n))
    return pl.pallas_call(
        body, name="ffn_backward_act", grid=(dff // FFN_NB, lp // tm),
        in_specs=[pl.BlockSpec((tm, d), lambda n, i: (i, 0)), pl.BlockSpec((FFN_NB, d), lambda n, i: (n, 0)), os_, os_],
        out_specs=[os_, os_],
        out_shape=[jax.ShapeDtypeStruct((lp, dff), BF16)] * 2,
        compiler_params=_params(("parallel", "parallel")),
    )(dr, wd, a, b)


def resid_nt(dr, xs, ws, name):
    lp, d = dr.shape
    tm = _tm(lp)
    n = len(xs)

    def body(*refs):
        acc = ALPHA * refs[0][...]
        for i in range(n):
            acc = acc + _nt(refs[1 + i][...], refs[1 + n + i][...])
        refs[-1][...] = acc

    rs = _row_spec(tm, d)
    in_specs = [rs] + [_row_spec(tm, x.shape[1]) for x in xs] + [_full_spec(w.shape) for w in ws]
    return pl.pallas_call(
        body, name=name, grid=(lp // tm,),
        in_specs=in_specs, out_specs=rs,
        out_shape=jax.ShapeDtypeStruct((lp, d), F32),
        compiler_params=_params(("parallel",)),
    )(dr, *xs, *ws)


def mm_tn(x, y, name):
    lp, k = x.shape
    n = y.shape[1]
    tm = _tm(lp)
    nb = _pick_tile(n, (512, 1408))

    def body(x_ref, y_ref, o_ref):
        @pl.when(pl.program_id(1) == 0)
        def _():
            o_ref[...] = jnp.zeros_like(o_ref)

        o_ref[...] += _tn(x_ref[...].astype(BF16), y_ref[...].astype(BF16))

    return pl.pallas_call(
        body, name=name, grid=(n // nb, lp // tm),
        in_specs=[pl.BlockSpec((tm, k), lambda j, i: (i, 0)), pl.BlockSpec((tm, nb), lambda j, i: (i, j))],
        out_specs=pl.BlockSpec((k, nb), lambda j, i: (0, j)),
        out_shape=jax.ShapeDtypeStruct((k, n), F32),
        compiler_params=_params(("parallel", "arbitrary")),
    )(x, y)


def glu_backward1(dr, wout, g, t):
    lp, d = dr.shape
    tm = _tm(lp)

    def body(dr_ref, w_ref, g_ref, t_ref, dt_ref, dgd_ref):
        dz = _nt(dr_ref[...].astype(BF16), w_ref[...])
        s = jax.nn.sigmoid(t_ref[...])
        dgd_ref[...] = dz * s
        dt_ref[...] = (dz * g_ref[...].astype(F32) * s * (1.0 - s)).astype(BF16)

    rs = _row_spec(tm, d)
    return pl.pallas_call(
        body, name="glu_backward1", grid=(lp // tm,),
        in_specs=[rs, _full_spec((d, d)), rs, rs],
        out_specs=[rs, rs],
        out_shape=[jax.ShapeDtypeStruct((lp, d), BF16), jax.ShapeDtypeStruct((lp, d), F32)],
        compiler_params=_params(("parallel",)),
    )(dr, wout, g, t)


def glu_backward2(dt, dgd, wglu, v, h, dvec, dr):
    lp, d = dt.shape
    tm = _tm(lp)

    def body(dt_ref, dgd_ref, w_ref, v_ref, h_ref, d_ref, dr_ref, dv_ref, dhs_ref, s_ref):
        dg = dgd_ref[...] + _nt(dt_ref[...], w_ref[...])
        dv = dg * _gelu_grad(v_ref[...])
        dv_ref[...] = dv
        dhs_ref[...] = ALPHA * dr_ref[...] + dv * d_ref[...]

        @pl.when(pl.program_id(0) == 0)
        def _():
            s_ref[...] = jnp.zeros_like(s_ref)

        s_ref[0:1, :] += jnp.sum(dv * h_ref[...], axis=0, keepdims=True)

    rs = _row_spec(tm, d)
    return pl.pallas_call(
        body, name="glu_backward2", grid=(lp // tm,),
        in_specs=[rs, rs, _full_spec((d, d)), rs, rs, _full_spec((1, d)), rs],
        out_specs=[rs, rs, _full_spec((SUBLANES, d))],
        out_shape=[jax.ShapeDtypeStruct((lp, d), F32)] * 2 + [jax.ShapeDtypeStruct((SUBLANES, d), F32)],
        compiler_params=_params(("arbitrary",)),
    )(dt, dgd, wglu, v, h, dvec, dr)


def loss_backward(hf, tgt, n_valid):
    lp, d = hf.shape
    tm = _tm(lp)

    def body(h_ref, t_ref, dh_ref, s_ref):
        rows = pl.program_id(0) * tm + lax.broadcasted_iota(jnp.int32, (tm, d), 0)
        ok = (rows >= N_META) & (rows < n_valid)
        e = jnp.where(ok, h_ref[...] - t_ref[...], 0.0)
        dh_ref[...] = e * (1.0 / d)

        @pl.when(pl.program_id(0) == 0)
        def _():
            s_ref[...] = jnp.zeros_like(s_ref)

        sq = e * e
        part = sq[:, 0:LANES]
        for c in range(1, d // LANES):
            part = part + sq[:, c * LANES:(c + 1) * LANES]
        acc = part[0:SUBLANES]
        for r in range(1, tm // SUBLANES):
            acc = acc + part[r * SUBLANES:(r + 1) * SUBLANES]
        s_ref[...] += acc * (0.5 / d)

    rs = _row_spec(tm, d)
    return pl.pallas_call(
        body, name="loss_backward", grid=(lp // tm,),
        in_specs=[rs, rs], out_specs=[rs, _full_spec((SUBLANES, LANES))],
        out_shape=[jax.ShapeDtypeStruct((lp, d), F32), jax.ShapeDtypeStruct((SUBLANES, LANES), F32)],
        compiler_params=_params(("arbitrary",)),
    )(hf, tgt)


N_QB = N_Q_HEADS // 2
N_KB = N_KV_HEADS
QKV_W = (N_QB + 2 * N_KB) * LANES
Q_SCALE = HEAD_DIM ** -0.5 * math.log2(math.e)


def rope_tables(lp, n_valid):
    t = jnp.arange(lp, dtype=jnp.int32)
    real = (t >= N_META) & (t < n_valid)
    pos = jnp.where(real, t - N_META, 0)
    row = (pos // GRID_W).astype(F32)
    col = (pos % GRID_W).astype(F32)
    axis_dim = HEAD_DIM // 2
    inv = ROPE_THETA ** (-jnp.arange(0, axis_dim, 2, dtype=F32) / axis_dim)
    ar = row[:, None] * inv[None, :]
    ac = col[:, None] * inv[None, :]
    cos = jnp.concatenate([jnp.cos(ar), jnp.cos(ar), jnp.cos(ac), jnp.cos(ac)], axis=1)
    sin = jnp.concatenate([-jnp.sin(ar), jnp.sin(ar), -jnp.sin(ac), jnp.sin(ac)], axis=1)
    return jnp.tile(cos, (1, 2)), jnp.tile(sin, (1, 2))


def head_sum_matrix():
    return jnp.kron(jnp.eye(2, dtype=F32), jnp.ones((HEAD_DIM, HEAD_DIM), F32)).astype(BF16)


def _segsum(x, e):
    hi = x.astype(BF16)
    lo = (x - hi.astype(F32)).astype(BF16)
    return _nn(hi, e) + _nn(lo, e)


def _swap_halves(x):
    lane = lax.broadcasted_iota(jnp.int32, x.shape, 1)
    quarter = HEAD_DIM // 4
    return jnp.where(lane % (2 * quarter) < quarter, pltpu.roll(x, LANES - quarter, 1), pltpu.roll(x, quarter, 1))


def qkv_forward(h, w2, gq, gk, cos, sin, e):
    lp, d = h.shape
    tm = _tm(lp)
    kw, vw = N_KB * LANES, N_KB * LANES

    def body(h_ref, w_ref, gq_ref, gk_ref, cos_ref, sin_ref, e_ref, raw_ref, q_ref, k_ref, v_ref, qt_ref, vt_ref):
        raw = _nn(h_ref[...].astype(BF16), w_ref[...])
        raw_ref[...] = raw
        c, s_, em = cos_ref[...], sin_ref[...], e_ref[...]
        for cb in range(N_QB + N_KB):
            t = raw[:, cb * LANES:(cb + 1) * LANES]
            rstd = lax.rsqrt(_segsum(t * t, em) * (1.0 / HEAD_DIM) + QK_EPS)
            n = t * rstd * (gq_ref[...] if cb < N_QB else gk_ref[...])
            rot = n * c + _swap_halves(n) * s_
            if cb < N_QB:
                qs = rot * Q_SCALE
                q_ref[:, cb * LANES:(cb + 1) * LANES] = qs.astype(BF16)
                qt_ref[cb * LANES:(cb + 1) * LANES, :] = qs.T.astype(BF16)
            else:
                k_ref[:, (cb - N_QB) * LANES:(cb - N_QB + 1) * LANES] = rot.astype(BF16)
        v_ref[...] = raw[:, (N_QB + N_KB) * LANES:].astype(BF16)
        for cb in range(N_KB):
            lo = (N_QB + N_KB + cb) * LANES
            vt_ref[cb * LANES:(cb + 1) * LANES, :] = raw[:, lo:lo + LANES].T.astype(BF16)

    col_spec = lambda rows: pl.BlockSpec((rows, tm), lambda i: (0, i))
    return pl.pallas_call(
        body, name="qkv_forward", grid=(lp // tm,),
        in_specs=[_row_spec(tm, d), _full_spec((d, QKV_W)), _full_spec((1, LANES)), _full_spec((1, LANES)),
                  _row_spec(tm, LANES), _row_spec(tm, LANES), _full_spec((LANES, LANES))],
        out_specs=[_row_spec(tm, QKV_W), _row_spec(tm, N_QB * LANES), _row_spec(tm, kw), _row_spec(tm, vw),
                   col_spec(N_QB * LANES), col_spec(vw)],
        out_shape=[jax.ShapeDtypeStruct((lp, QKV_W), F32), jax.ShapeDtypeStruct((lp, N_QB * LANES), BF16),
                   jax.ShapeDtypeStruct((lp, kw), BF16), jax.ShapeDtypeStruct((lp, vw), BF16),
                   jax.ShapeDtypeStruct((N_QB * LANES, lp), BF16), jax.ShapeDtypeStruct((vw, lp), BF16)],
        compiler_params=_params(("parallel",)),
    )(h, w2, gq, gk, cos, sin, e)


def qkv_backward(dqs, dk2, dv2, raw, gq, gk, cos, sin, e):
    lp = raw.shape[0]
    tm = _tm(lp)

    def body(dq_ref, dk_ref, dv_ref, raw_ref, gq_ref, gk_ref, cos_ref, sin_ref, e_ref, d_ref, s_ref):
        @pl.when(pl.program_id(0) == 0)
        def _():
            s_ref[...] = jnp.zeros_like(s_ref)

        c, s_, em = cos_ref[...], sin_ref[...], e_ref[...]
        gsum = [jnp.zeros((1, LANES), F32), jnp.zeros((1, LANES), F32)]
        for cb in range(N_QB + N_KB):
            isq = cb < N_QB
            t = raw_ref[:, cb * LANES:(cb + 1) * LANES]
            if isq:
                drot = dq_ref[:, cb * LANES:(cb + 1) * LANES] * (HEAD_DIM ** -0.5)
            else:
                drot = dk_ref[:, (cb - N_QB) * LANES:(cb - N_QB + 1) * LANES] * math.log(2.0)
            gain = gq_ref[...] if isq else gk_ref[...]
            rstd = lax.rsqrt(_segsum(t * t, em) * (1.0 / HEAD_DIM) + QK_EPS)
            dn = drot * c + _swap_halves(drot * s_)
            xh = t * rstd
            gsum[0 if isq else 1] = gsum[0 if isq else 1] + jnp.sum(dn * xh, axis=0, keepdims=True)
            w = dn * gain
            mw = _segsum(w * xh, em) * (1.0 / HEAD_DIM)
            d_ref[:, cb * LANES:(cb + 1) * LANES] = (rstd * (w - xh * mw)).astype(BF16)
        d_ref[:, (N_QB + N_KB) * LANES:] = dv_ref[...].astype(BF16)
        s_ref[0:1, :] += gsum[0]
        s_ref[1:2, :] += gsum[1]

    kw = N_KB * LANES
    return pl.pallas_call(
        body, name="qkv_backward", grid=(lp // tm,),
        in_specs=[_row_spec(tm, N_QB * LANES), _row_spec(tm, kw), _row_spec(tm, kw), _row_spec(tm, QKV_W),
                  _full_spec((1, LANES)), _full_spec((1, LANES)), _row_spec(tm, LANES), _row_spec(tm, LANES),
                  _full_spec((LANES, LANES))],
        out_specs=[_row_spec(tm, QKV_W), _full_spec((SUBLANES, LANES))],
        out_shape=[jax.ShapeDtypeStruct((lp, QKV_W), BF16), jax.ShapeDtypeStruct((SUBLANES, LANES), F32)],
        compiler_params=_params(("arbitrary",)),
    )(dqs, dk2, dv2, raw, gq, gk, cos, sin, e)


NEG = -1e30
Q_PER_KV = N_Q_HEADS // N_KV_HEADS


def _half_masks(x):
    lane = lax.broadcasted_iota(jnp.int32, x.shape, 1)
    zero = jnp.zeros_like(x)
    return jnp.where(lane < HEAD_DIM, x, zero), jnp.where(lane >= HEAD_DIM, x, zero)


ATTN_TR = 16


def _attn_tiles(lp):
    t = _pick_tile(lp, (1408, 256))
    return t, t


def _attn_tiles_bwd(lp):
    t = _pick_tile(lp, (768, 256))
    return t, t


def attn_forward_t(qs, k2, v2t, n_valid, prefetch=()):
    lp = qs.shape[0]
    tq, kb = _attn_tiles(lp)
    nk = lp // kb
    gw = 2 * LANES
    nr = kb // ATTN_TR
    pad0 = n_valid - (nk - 1) * kb
    npf = len(prefetch)
    pf_specs, pf_shapes, pf_sems = _prefetch_specs(prefetch)
    nq = lp // tq

    def body(*refs):
        q_ref, k_ref, vt_ref = refs[:3]
        pf_ins = refs[3:3 + npf]
        o_ref, lse_ref = refs[3 + npf:5 + npf]
        pf_outs = refs[5 + npf:5 + 2 * npf]
        m_s, l_s, acc_s, s_s, p_s = refs[5 + 2 * npf:10 + 2 * npf]
        j = pl.program_id(2)
        first = (pl.program_id(0) == 0) & (pl.program_id(1) == 0) & (j == 0)
        last = (pl.program_id(0) == N_KV_HEADS - 1) & (pl.program_id(1) == nq - 1) & (j == nk - 1)
        _prefetch_run(first, last, pf_ins, pf_outs, refs[10 + 2 * npf:])

        @pl.when(j == 0)
        def _():
            m_s[...] = jnp.full_like(m_s, NEG)
            l_s[...] = jnp.zeros_like(l_s)
            acc_s[...] = jnp.zeros_like(acc_s)

        ks = _half_masks(k_ref[...])
        for pair in range(2):
            qp = q_ref[:, pair * LANES:(pair + 1) * LANES]
            for half in range(2):
                hh = 2 * pair + half
                s_s[...] = _nt(ks[half], qp)

                if pad0 < kb:
                    @pl.when(j == nk - 1)
                    def _():
                        s_s[pad0:, :] = jnp.full((kb - pad0, tq), NEG, F32)

                def max_step(r, run):
                    rows = pl.ds(r * ATTN_TR, ATTN_TR)
                    blk = s_s[rows, :]
                    for t in range(ATTN_TR // SUBLANES):
                        run = jnp.maximum(run, blk[t * SUBLANES:(t + 1) * SUBLANES])
                    return run

                run = jnp.full((SUBLANES, tq), NEG, F32)
                for r in range(nr):
                    run = max_step(r, run)
                m_prev = m_s[hh:hh + 1, :]
                m_new = jnp.maximum(m_prev, jnp.max(run, axis=0, keepdims=True))
                alpha = jnp.exp2(m_prev - m_new)
                m_s[hh:hh + 1, :] = m_new

                def exp_step(r, run):
                    rows = pl.ds(r * ATTN_TR, ATTN_TR)
                    p = jnp.exp2(s_s[rows, :] - m_new)
                    p_s[rows, :] = p.astype(BF16)
                    for t in range(ATTN_TR // SUBLANES):
                        run = run + p[t * SUBLANES:(t + 1) * SUBLANES]
                    return run

                run = jnp.zeros((SUBLANES, tq), F32)
                for r in range(nr):
                    run = exp_step(r, run)
                l_s[hh:hh + 1, :] = alpha * l_s[hh:hh + 1, :] + jnp.sum(run, axis=0, keepdims=True)
                vt = vt_ref[half * HEAD_DIM:(half + 1) * HEAD_DIM, :]
                pv = _nn(vt, p_s[...])
                rs = slice(half * HEAD_DIM, (half + 1) * HEAD_DIM)
                acc_s[pair, rs, :] = alpha * acc_s[pair, rs, :] + pv

        @pl.when(j == nk - 1)
        def _():
            for pair in range(2):
                for half in range(2):
                    hh = 2 * pair + half
                    rs = slice(half * HEAD_DIM, (half + 1) * HEAD_DIM)
                    acc_s[pair, rs, :] = acc_s[pair, rs, :] * (1.0 / l_s[hh:hh + 1, :])
                o_ref[:, pair * LANES:(pair + 1) * LANES] = acc_s[pair].T.astype(BF16)
            for hh in range(Q_PER_KV):
                lse_ref[0, hh] = m_s[hh:hh + 1, :] + jnp.log2(l_s[hh:hh + 1, :])

    return pl.pallas_call(
        body, name="attn_forward", grid=(N_KV_HEADS, lp // tq, nk),
        in_specs=[pl.BlockSpec((tq, gw), lambda g, i, j: (i, g)), pl.BlockSpec((kb, LANES), lambda g, i, j: (j, g)),
                  pl.BlockSpec((LANES, kb), lambda g, i, j: (g, j))] + pf_specs,
        out_specs=[pl.BlockSpec((tq, gw), lambda g, i, j: (i, g)),
                   pl.BlockSpec((1, Q_PER_KV, 1, tq), lambda g, i, j: (g, 0, 0, i))] + pf_specs,
        out_shape=[jax.ShapeDtypeStruct((lp, N_QB * LANES), BF16),
                   jax.ShapeDtypeStruct((N_KV_HEADS, Q_PER_KV, 1, lp), F32)] + pf_shapes,
        scratch_shapes=[pltpu.VMEM((SUBLANES, tq), F32), pltpu.VMEM((SUBLANES, tq), F32),
                        pltpu.VMEM((2, LANES, tq), F32), pltpu.VMEM((kb, tq), F32), pltpu.VMEM((kb, tq), BF16)] + pf_sems,
        compiler_params=_params(("arbitrary",) * 3 if npf else ("parallel", "parallel", "arbitrary")),
    )(qs, k2, v2t, *prefetch)


def attn_backward(qs, qst, k2, v2, do, dot, lse, delta, n_valid):
    lp = qs.shape[0]
    tq, kb = _attn_tiles_bwd(lp)
    nq, nk = lp // tq, lp // kb
    gw = 2 * LANES
    pad0 = n_valid - (nk - 1) * kb

    def body(q_ref, qt_ref, k_ref, v_ref, do_ref, dot_ref, lse_ref, dl_ref, dq_ref, dk_ref, dv_ref, acc_s, dkt_s, dvt_s):
        g = pl.program_id(0)
        i = pl.program_id(1)
        j = pl.program_id(2)
        cols = pl.ds(pl.multiple_of(j * kb, kb), kb)

        @pl.when(j == 0)
        def _():
            acc_s[...] = jnp.zeros_like(acc_s)

        @pl.when(i == 0)
        def _():
            dkt_s[:, cols] = jnp.zeros((LANES, kb), F32)
            dvt_s[:, cols] = jnp.zeros((LANES, kb), F32)

        head = lax.broadcasted_iota(jnp.int32, (tq, N_Q_HEADS), 1)

        def column(ref, hh):
            return jnp.sum(jnp.where(head == Q_PER_KV * g + hh, ref[...], 0.0), axis=1, keepdims=True)

        def step(masked):
            ks = _half_masks(k_ref[...])
            vs = _half_masks(v_ref[...])
            if masked:
                col = lax.broadcasted_iota(jnp.int32, (1, kb), 1)
                bias = jnp.where(col < pad0, 0.0, NEG)
            for pair in range(2):
                qp = q_ref[:, pair * LANES:(pair + 1) * LANES]
                dop = do_ref[:, pair * LANES:(pair + 1) * LANES]
                for half in range(2):
                    hh = 2 * pair + half
                    rs = slice(half * HEAD_DIM, (half + 1) * HEAD_DIM)
                    rt = slice(pair * LANES + half * HEAD_DIM, pair * LANES + (half + 1) * HEAD_DIM)
                    s = _nt(qp, ks[half])
                    if masked:
                        s = s + bias
                    p = jnp.exp2(s - column(lse_ref, hh))
                    dp = _nt(dop, vs[half])
                    ds = (p * (dp - column(dl_ref, hh))).astype(BF16)
                    pb = p.astype(BF16)
                    acc_s[pair] += _nn(ds, ks[half])
                    dvt_s[rs, cols] += _nn(dot_ref[rt, :], pb)
                    dkt_s[rs, cols] += _nn(qt_ref[rt, :], ds)

        if pad0 < kb:
            pl.when(j < nk - 1)(lambda: step(False))
            pl.when(j == nk - 1)(lambda: step(True))
        else:
            step(False)

        @pl.when(j == nk - 1)
        def _():
            for pair in range(2):
                dq_ref[:, pair * LANES:(pair + 1) * LANES] = acc_s[pair]

        @pl.when(i == nq - 1)
        def _():
            dk_ref[cols, :] = dkt_s[:, cols].T
            dv_ref[cols, :] = dvt_s[:, cols].T

    cspec = pl.BlockSpec((tq, N_Q_HEADS), lambda g, i, j: (i, 0))
    qspec = pl.BlockSpec((tq, gw), lambda g, i, j: (i, g))
    tspec = pl.BlockSpec((gw, tq), lambda g, i, j: (g, i))
    kspec = pl.BlockSpec((kb, LANES), lambda g, i, j: (j, g))
    gspec = pl.BlockSpec((lp, LANES), lambda g, i, j: (0, g))
    return pl.pallas_call(
        body, name="attn_backward", grid=(N_KV_HEADS, nq, nk),
        in_specs=[qspec, tspec, kspec, kspec, qspec, tspec, cspec, cspec],
        out_specs=[qspec, gspec, gspec],
        out_shape=[jax.ShapeDtypeStruct((lp, N_QB * LANES), F32),
                   jax.ShapeDtypeStruct((lp, N_KB * LANES), F32), jax.ShapeDtypeStruct((lp, N_KB * LANES), F32)],
        scratch_shapes=[pltpu.VMEM((2, tq, LANES), F32), pltpu.VMEM((LANES, lp), F32), pltpu.VMEM((LANES, lp), F32)],
        compiler_params=_params(("parallel", "arbitrary", "arbitrary")),
    )(qs, qst, k2, v2, do, dot, lse, delta)


def attn_out_backward(dr, wout, o, e16):
    lp, d = dr.shape
    tm = _tm(lp)

    def body(dr_ref, w_ref, o_ref, e_ref, do_ref, dl_ref, dot_ref):
        do32 = _nt(dr_ref[...].astype(BF16), w_ref[...])
        do = do32.astype(BF16)
        do_ref[...] = do
        for cb in range(d // LANES):
            dot_ref[cb * LANES:(cb + 1) * LANES, :] = do32[:, cb * LANES:(cb + 1) * LANES].T.astype(BF16)
        dl_ref[...] = _segsum(do.astype(F32) * o_ref[...].astype(F32), e_ref[...])

    rs = _row_spec(tm, d)
    return pl.pallas_call(
        body, name="attn_out_backward", grid=(lp // tm,),
        in_specs=[rs, _full_spec((d, d)), rs, _full_spec((d, N_Q_HEADS))],
        out_specs=[rs, _row_spec(tm, N_Q_HEADS), pl.BlockSpec((d, tm), lambda i: (0, i))],
        out_shape=[jax.ShapeDtypeStruct((lp, d), BF16), jax.ShapeDtypeStruct((lp, N_Q_HEADS), F32),
                   jax.ShapeDtypeStruct((d, lp), BF16)],
        compiler_params=_params(("parallel",)),
    )(dr, wout, o, e16)


N_CHIPS = 4


def _mesh_pos():
    return lax.axis_index("x"), lax.axis_index("y"), lax.axis_index("c")


def chip_exchange(arrs, scatter, name):
    n = len(arrs)
    hbm = pl.BlockSpec(memory_space=pl.ANY)

    def body(*refs):
        ins, outs = refs[:n], refs[n:2 * n]
        send_sems, recv_sems, loc_sems = refs[2 * n:]
        x, y, c = _mesh_pos()
        me = 2 * x + y
        chips = [(1 - x, y), (x, 1 - y), (1 - x, 1 - y)]
        started = []
        for a in range(n):
            loc = pltpu.make_async_copy(ins[a].at[me] if scatter else ins[a], outs[a].at[me], loc_sems.at[a])
            loc.start()
            started.append(loc)
            for k, (px, py) in enumerate(chips):
                src = ins[a].at[2 * px + py] if scatter else ins[a]
                cp = pltpu.make_async_remote_copy(
                    src_ref=src, dst_ref=outs[a].at[me], send_sem=send_sems.at[3 * a + k], recv_sem=recv_sems.at[3 * a + k],
                    device_id=(px, py, c), device_id_type=MESH)
                cp.start()
                started.append(cp)
        for cp in started:
            cp.wait()

    out_shape = [jax.ShapeDtypeStruct(a.shape if scatter else (N_CHIPS,) + a.shape, a.dtype) for a in arrs]
    return pl.pallas_call(
        body, name=name, in_specs=[hbm] * n, out_specs=[hbm] * n, out_shape=out_shape,
        scratch_shapes=[pltpu.SemaphoreType.DMA((3 * n,)), pltpu.SemaphoreType.DMA((3 * n,)), pltpu.SemaphoreType.DMA((n,))],
    )(*arrs)


def _same_core_copies(ins, outs, send_sems, recv_sems, loc_sems):
    x, y, c = _mesh_pos()
    me = 2 * x + y
    chips = [(1 - x, y), (x, 1 - y), (1 - x, 1 - y)]
    cps = []
    for a in range(len(ins)):
        cps.append(pltpu.make_async_copy(ins[a], outs[a].at[me], loc_sems.at[a]))
        for k, (px, py) in enumerate(chips):
            cps.append(pltpu.make_async_remote_copy(
                src_ref=ins[a], dst_ref=outs[a].at[me], send_sem=send_sems.at[3 * a + k], recv_sem=recv_sems.at[3 * a + k],
                device_id=(px, py, c), device_id_type=MESH))
    return cps


def _prefetch_specs(arrs):
    n = len(arrs)
    hbm = pl.BlockSpec(memory_space=pl.ANY)
    shapes = [jax.ShapeDtypeStruct((N_CHIPS,) + a.shape, a.dtype) for a in arrs]
    sems = [pltpu.SemaphoreType.DMA((3 * n,)), pltpu.SemaphoreType.DMA((3 * n,)), pltpu.SemaphoreType.DMA((n,))] if n else []
    return [hbm] * n, shapes, sems


def _prefetch_run(first, last, ins, outs, sems):
    if not ins:
        return

    @pl.when(first)
    def _():
        for cp in _same_core_copies(ins, outs, *sems):
            cp.start()

    @pl.when(last)
    def _():
        for cp in _same_core_copies(ins, outs, *sems):
            cp.wait()


def gather_two_level(arrs, name):
    n = len(arrs)
    hbm = pl.BlockSpec(memory_space=pl.ANY)

    def body(*refs):
        ins, outs = refs[:n], refs[n:2 * n]
        ici_send, ici_recv, d2d_send, d2d_recv, loc_sems = refs[2 * n:]
        x, y, c = _mesh_pos()
        me = 2 * x + y
        chips = [(1 - x, y), (x, 1 - y), (1 - x, 1 - y)]
        started = []
        for a in range(n):
            hn = arrs[a].shape[0] // 2
            mine = pl.ds(c * hn, hn)
            loc = pltpu.make_async_copy(ins[a], outs[a].at[me], loc_sems.at[a])
            loc.start()
            started.append(loc)
            first = []
            for k, (px, py) in enumerate(chips):
                cp = pltpu.make_async_remote_copy(
                    src_ref=ins[a].at[mine], dst_ref=outs[a].at[me, mine], send_sem=ici_send.at[3 * a + k],
                    recv_sem=ici_recv.at[3 * a + k], device_id=(px, py, c), device_id_type=MESH)
                cp.start()
                first.append(cp)
            for k, (px, py) in enumerate(chips):
                q = 2 * px + py
                first[k].wait_recv()
                fw = pltpu.make_async_remote_copy(
                    src_ref=outs[a].at[q, mine], dst_ref=outs[a].at[q, mine], send_sem=d2d_send.at[3 * a + k],
                    recv_sem=d2d_recv.at[3 * a + k], device_id=(x, y, 1 - c), device_id_type=MESH)
                fw.start()
                started.append(fw)
            for cp in first:
                cp.wait_send()
        for cp in started:
            cp.wait()

    out_shape = [jax.ShapeDtypeStruct((N_CHIPS,) + a.shape, a.dtype) for a in arrs]
    return pl.pallas_call(
        body, name=name, in_specs=[hbm] * n, out_specs=[hbm] * n, out_shape=out_shape,
        scratch_shapes=[pltpu.SemaphoreType.DMA((3 * n,))] * 4 + [pltpu.SemaphoreType.DMA((n,))],
    )(*arrs)


def sibling_exchange(arrs, name):
    n = len(arrs)
    hbm = pl.BlockSpec(memory_space=pl.ANY)

    def body(*refs):
        ins, outs = refs[:n], refs[n:2 * n]
        send_sems, recv_sems = refs[2 * n:]
        x, y, c = _mesh_pos()
        started = []
        for a in range(n):
            cp = pltpu.make_async_remote_copy(
                src_ref=ins[a], dst_ref=outs[a], send_sem=send_sems.at[a], recv_sem=recv_sems.at[a],
                device_id=(x, y, 1 - c), device_id_type=MESH)
            cp.start()
            started.append(cp)
        for cp in started:
            cp.wait()

    return pl.pallas_call(
        body, name=name, in_specs=[hbm] * n, out_specs=[hbm] * n,
        out_shape=[jax.ShapeDtypeStruct(a.shape, a.dtype) for a in arrs],
        scratch_shapes=[pltpu.SemaphoreType.DMA((n,)), pltpu.SemaphoreType.DMA((n,))],
    )(*arrs)


def _rows_tile(r, c):
    return _pick_tile(r, tuple(t for t in (512, 256, 128, 64, 32, 16, 8) if t * c * 4 <= 2 * 1024 * 1024))


def chip_sum(recv, name):
    _, r, c = recv.shape
    tr = _rows_tile(r, c)

    def body(r_ref, o_ref):
        acc = r_ref[0].astype(F32)
        for q in range(1, N_CHIPS):
            acc = acc + r_ref[q].astype(F32)
        o_ref[...] = acc

    return pl.pallas_call(
        body, name=name, grid=(r // tr,),
        in_specs=[pl.BlockSpec((N_CHIPS, tr, c), lambda i: (0, i, 0))],
        out_specs=pl.BlockSpec((tr, c), lambda i: (i, 0)),
        out_shape=jax.ShapeDtypeStruct((r, c), F32),
        compiler_params=_params(("parallel",)),
    )(recv)


def pair_sum(part, sib, name, dtype=F32):
    r, c = part.shape
    tr = _rows_tile(r, c)

    def body(p_ref, s_ref, o_ref):
        o_ref[...] = (p_ref[...].astype(F32) + s_ref[...].astype(F32)).astype(dtype)

    rs = pl.BlockSpec((tr, c), lambda i: (i, 0))
    return pl.pallas_call(
        body, name=name, grid=(r // tr,), in_specs=[rs] * 2, out_specs=rs,
        out_shape=jax.ShapeDtypeStruct((r, c), dtype), compiler_params=_params(("parallel",)),
    )(part, sib)


def adamw(g, w, m, v, name):
    r, c = w.shape
    tr = _rows_tile(r, c)

    def body(g_ref, w_ref, m_ref, v_ref, d_ref, nm_ref, nv_ref):
        g_ = g_ref[...]
        m_ = ADAM_B1 * m_ref[...] + (1.0 - ADAM_B1) * g_
        v_ = ADAM_B2 * v_ref[...] + (1.0 - ADAM_B2) * (g_ * g_)
        m_hat = m_ / (1.0 - ADAM_B1 ** ADAM_STEP)
        v_hat = v_ / (1.0 - ADAM_B2 ** ADAM_STEP)
        d_ref[...] = -ADAM_LR * (m_hat / (jnp.sqrt(v_hat) + ADAM_EPS) + ADAM_WD * w_ref[...])
        nm_ref[...] = m_
        nv_ref[...] = v_

    rs = pl.BlockSpec((tr, c), lambda i: (i, 0))
    return pl.pallas_call(
        body, name=name, grid=(r // tr,), in_specs=[rs] * 4, out_specs=[rs] * 3,
        out_shape=[jax.ShapeDtypeStruct((r, c), F32)] * 3,
        compiler_params=_params(("parallel",)),
    )(g, w, m, v)


WEIGHTS = ['meta_tokens', 's5_lambda_re', 's5_lambda_im', 's5_log_dt', 's5_b_re', 's5_b_im', 's5_c_re', 's5_c_im', 's5_d',
           's5_w_glu', 's5_w_out', 'attn_w_qkv', 'attn_q_gain', 'attn_k_gain', 'attn_w_out', 'ffn_w_gate', 'ffn_w_up',
           'ffn_w_down', 'ln_gain', 'ln_bias']
BIG = ['s5_w_glu', 's5_w_out', 'attn_w_qkv', 'attn_w_out', 'ffn_w_gate', 'ffn_w_up', 'ffn_w_down']
ROW_SHARDED = {'s5_w_glu', 's5_w_out', 'attn_w_out', 'ffn_w_down'}
SMALL_SHARDED = ['meta_tokens', 'ln_gain', 'ln_bias']
REPLICATED = ['s5_lambda_re', 's5_lambda_im', 's5_log_dt', 's5_b_re', 's5_b_im', 's5_c_re', 's5_c_im', 's5_d',
              'attn_q_gain', 'attn_k_gain']
REP_ALIGN = N_CHIPS * LANES * LANES


def _natural(gathered, row_sharded):
    p, n, a, b = gathered.shape
    if row_sharded:
        return jnp.transpose(gathered, (1, 0, 2, 3)).reshape(n, p * a, b)
    return jnp.transpose(gathered, (1, 2, 0, 3)).reshape(n, a, p * b)


def _shard_major(full, row_sharded):
    n, a, b = full.shape
    if row_sharded:
        return jnp.transpose(full.reshape(n, N_CHIPS, a // N_CHIPS, b), (1, 0, 2, 3))
    return jnp.transpose(full.reshape(n, a, N_CHIPS, b // N_CHIPS), (2, 0, 1, 3))


def _dup_heads(w):
    lead = w.shape[:-1]
    w = w.reshape(lead + (N_KV_HEADS, 1, HEAD_DIM))
    return jnp.broadcast_to(w, lead + (N_KV_HEADS, 2, HEAD_DIM)).reshape(lead + (N_KV_HEADS * 2 * HEAD_DIM,))


def _fold_heads(d):
    lead = d.shape[:-1]
    return d.reshape(lead + (N_KV_HEADS, 2, HEAD_DIM)).sum(axis=-2).reshape(lead + (N_KV_HEADS * HEAD_DIM,))


def _pack_rep(tree):
    flat = jnp.concatenate([tree[n].reshape(-1) for n in REPLICATED])
    pad = _round_up(flat.shape[0], REP_ALIGN) - flat.shape[0]
    return jnp.pad(flat, (0, pad))


def _unpack_rep(flat, like):
    out, off = {}, 0
    for n in REPLICATED:
        size = math.prod(like[n].shape)
        out[n] = flat[off:off + size].reshape(like[n].shape)
        off += size
    return out


def _train_step(x, loss_target, w, mom, vel):
    s = x.shape[1]
    n_valid = N_META + s
    lp = _round_up(n_valid, 2 * LANES)
    nq = N_Q_HEADS * HEAD_DIM
    nkv = N_KV_HEADS * HEAD_DIM

    small = jnp.concatenate([w[n].reshape(-1, w[n].shape[-1]) for n in SMALL_SHARDED], axis=0)
    shard = {n: w[n].astype(BF16) for n in BIG}
    uses = [[('s5_w_glu', 0), ('s5_w_out', 0), ('ffn_w_gate', 0), ('ffn_w_up', 0), ('ffn_w_down', 0)],
            [('attn_w_qkv', 0), ('attn_w_out', 0), ('ffn_w_gate', 1), ('ffn_w_up', 1), ('ffn_w_down', 1)],
            [('s5_w_glu', 1), ('s5_w_out', 1), ('ffn_w_gate', 2), ('ffn_w_up', 2), ('ffn_w_down', 2),
             ('attn_w_qkv', 1), ('attn_w_out', 1), ('ffn_w_gate', 3), ('ffn_w_up', 3), ('ffn_w_down', 3)]]
    full = {}

    def unpack(stage, gathered):
        for (n, l), g in zip(uses[stage], gathered):
            full[(n, l)] = _natural(g[:, None], n in ROW_SHARDED)[0]

    first = gather_two_level([shard[n][l] for n, l in uses[0]] + [small], "gather_weights")
    unpack(0, first[:-1])
    small_full = jnp.transpose(first[-1], (1, 0, 2)).reshape(small.shape[0], D_MODEL)
    meta_full = small_full[:N_META]
    ln_gain = small_full[N_META:N_META + 2 * DEPTH].reshape(DEPTH, 2, 1, D_MODEL)
    ln_bias = small_full[N_META + 2 * DEPTH:].reshape(DEPTH, 2, 1, D_MODEL)

    def qkv_dup(wqkv):
        return jnp.concatenate([wqkv[..., :nq], _dup_heads(wqkv[..., nq:nq + nkv]), _dup_heads(wqkv[..., nq + nkv:])], axis=-1)

    w2 = {}

    cos, sin = rope_tables(lp, n_valid)
    e128 = head_sum_matrix()
    e16 = jnp.kron(jnp.eye(N_Q_HEADS, dtype=F32), jnp.ones((HEAD_DIM, 1), F32)).astype(BF16)
    gq = jnp.tile(w['attn_q_gain'], (1, 2))[:, None, :]
    gk = jnp.tile(w['attn_k_gain'], (1, 2))[:, None, :]

    pad_rows = jnp.zeros((lp - n_valid, D_MODEL), F32)
    h = jnp.concatenate([meta_full, x[0], pad_rows], axis=0)
    tgt = jnp.concatenate([jnp.zeros((N_META, D_MODEL), F32), loss_target[0], pad_rows], axis=0)

    saved = []
    s5_names = ['s5_lambda_re', 's5_lambda_im', 's5_log_dt', 's5_b_re', 's5_b_im', 's5_c_re', 's5_c_im']
    for i in range(DEPTH):
        j = i // 2
        sv = {'h': h}
        if i % 2 == 0:
            ops, sv['prep_vjp'] = jax.vjp(s5_prep, *[w[n][j] for n in s5_names])
            m_, wx_, ci_, at_ = ops
            two = lambda t: t.reshape((2 * S5_NJ,) + t.shape[2:])
            sv['ops'] = (blockdiag_expand(m_, S5_CH, S5_CH, "s5_expand_m"),
                         blockdiag_expand(two(wx_), S5_CH, S5_STATE, "s5_expand_wx").reshape(2, S5_NJ, S5_W, S5_W),
                         blockdiag_expand(two(ci_), S5_STATE, S5_CH, "s5_expand_ci").reshape(2, S5_NJ, S5_W, S5_W), at_)
            pf = [shard[n][l] for n, l in uses[1]] if i == 0 else []
            y, sv['lhs'], sv['sp'], sv['sn'], *got = s5_forward(h, *sv['ops'], n_valid, prefetch=pf)
            if i == 0:
                unpack(1, got)
            sv['v'], sv['t'], sv['g'], sv['z'] = glu_forward(y, h, w['s5_d'][j][None], full['s5_w_glu', j])
            sv['r1'], h1 = proj_ln_forward(sv['z'], full['s5_w_out', j], h, ln_gain[i, 0], ln_bias[i, 0], "s5_out_ln")
        else:
            w2[j] = qkv_dup(full['attn_w_qkv', j])
            sv['raw'], sv['qs'], sv['k2'], sv['v2'], sv['qst'], v2t = qkv_forward(h, w2[j], gq[j], gk[j], cos, sin, e128)
            pf = [shard[n][l] for n, l in uses[2]] if i == 1 else []
            sv['o'], lse, *got = attn_forward_t(sv['qs'], sv['k2'], v2t, n_valid, prefetch=pf)
            if i == 1:
                unpack(2, got)
            sv['lse'] = lse.reshape(N_Q_HEADS, lp).T
            sv['r1'], h1 = proj_ln_forward(sv['o'], full['attn_w_out', j], h, ln_gain[i, 0], ln_bias[i, 0], "attn_out_ln")
        sv['h1'] = h1
        sv['a'], sv['b'], sv['f'] = ffn_up_forward(h1, full['ffn_w_gate', i], full['ffn_w_up', i])
        sv['r2'], h = proj_ln_forward(sv['f'], full['ffn_w_down', i], h1, ln_gain[i, 1], ln_bias[i, 1], "ffn_down_ln")
        saved.append(sv)

    dh, loss_part = loss_backward(h, tgt, n_valid)
    loss = lax.psum(jnp.sum(loss_part), ("x", "y", "c"))

    gfull = {n: [None] * w[n].shape[0] for n in BIG}
    d_ln_gain = [[None, None] for _ in range(DEPTH)]
    d_ln_bias = [[None, None] for _ in range(DEPTH)]
    grep = {n: [None] * w[n].shape[0] for n in REPLICATED}
    for i in reversed(range(DEPTH)):
        j = i // 2
        sv = saved[i]
        dr2, s2 = ln_backward(dh, sv['r2'], ln_gain[i, 1])
        d_ln_gain[i][1], d_ln_bias[i][1] = s2[0], s2[1]
        da, db = ffn_backward_act(dr2, full['ffn_w_down', i], sv['a'], sv['b'])
        gfull['ffn_w_down'][i] = mm_tn(sv['f'], dr2, "grad_ffn_down")
        dh1 = resid_nt(dr2, [da, db], [full['ffn_w_gate', i], full['ffn_w_up', i]], "ffn_backward_x")
        gfull['ffn_w_gate'][i] = mm_tn(sv['h1'], da, "grad_ffn_gate")
        gfull['ffn_w_up'][i] = mm_tn(sv['h1'], db, "grad_ffn_up")
        dr1, s1 = ln_backward(dh1, sv['r1'], ln_gain[i, 0])
        d_ln_gain[i][0], d_ln_bias[i][0] = s1[0], s1[1]
        if i % 2 == 0:
            dt, dgd = glu_backward1(dr1, full['s5_w_out', j], sv['g'], sv['t'])
            gfull['s5_w_out'][j] = mm_tn(sv['z'], dr1, "grad_s5_out")
            dv, dhs, sd = glu_backward2(dt, dgd, full['s5_w_glu', j], sv['v'], sv['h'], w['s5_d'][j][None], dr1)
            grep['s5_d'][j] = sd[0]
            gfull['s5_w_glu'][j] = mm_tn(sv['g'], dt, "grad_s5_glu")
            dh, ldy, dxf, dxr, daf, dar = s5_backward(dv, dhs, *sv['ops'], sv['sp'], sv['sn'], n_valid)
            dm = bmm_tn_compact(sv['lhs'], ldy, S5_CH, S5_CH, "grad_s5_m")
            dwx = jnp.stack([bmm_tn_compact(sv['lhs'], dxf, S5_CH, S5_STATE, "grad_s5_wxf"),
                             bmm_tn_compact(sv['lhs'], dxr, S5_CH, S5_STATE, "grad_s5_wxr")])
            dci = jnp.stack([bmm_tn_compact(sv['sp'], ldy, S5_STATE, S5_CH, "grad_s5_cif"),
                             bmm_tn_compact(sv['sn'], ldy, S5_STATE, S5_CH, "grad_s5_cir")])
            dps = sv['prep_vjp']((dm, dwx, dci, jnp.stack([daf, dar])))
            for n, g in zip(s5_names, dps):
                grep[n][j] = g
        else:
            do, delta, dot = attn_out_backward(dr1, full['attn_w_out', j], sv['o'], e16)
            gfull['attn_w_out'][j] = mm_tn(sv['o'], dr1, "grad_attn_out")
            dq, dk2, dv2 = attn_backward(sv['qs'], sv['qst'], sv['k2'], sv['v2'], do, dot, sv['lse'], delta, n_valid)
            draw, gs = qkv_backward(dq, dk2, dv2, sv['raw'], gq[j], gk[j], cos, sin, e128)
            grep['attn_q_gain'][j] = gs[0, :HEAD_DIM] + gs[0, HEAD_DIM:]
            grep['attn_k_gain'][j] = gs[1, :HEAD_DIM] + gs[1, HEAD_DIM:]
            dh = resid_nt(dr1, [draw], [w2[j]], "attn_backward_x")
            dw2 = mm_tn(sv['h'], draw, "grad_attn_qkv")
            kq = N_QB * LANES
            kk = N_KB * LANES
            gfull['attn_w_qkv'][j] = jnp.concatenate(
                [dw2[:, :kq], _fold_heads(dw2[:, kq:kq + kk]), _fold_heads(dw2[:, kq + kk:])], axis=1)
    grad_x = dh[N_META:n_valid][None]

    core = lax.axis_index("c")
    contrib = [_shard_major(jnp.stack(gfull[n]), n in ROW_SHARDED) for n in BIG]
    small_g = jnp.concatenate([dh[:N_META], jnp.stack([g for pair in d_ln_gain for g in pair]),
                               jnp.stack([g for pair in d_ln_bias for g in pair])], axis=0)
    contrib.append(jnp.transpose(small_g.reshape(-1, N_CHIPS, D_MODEL // N_CHIPS), (1, 0, 2)))
    rep_g = _pack_rep({n: jnp.stack(grep[n]) for n in REPLICATED})
    contrib.append(rep_g.reshape(N_CHIPS, -1, LANES))
    names = BIG + ['small', 'rep']
    wire = [BF16] * len(BIG) + [F32, F32]
    keep, give = [], []
    for t, dt in zip(contrib, wire):
        hn = t.shape[1] // 2
        keep.append(lax.dynamic_slice_in_dim(t, core * hn, hn, axis=1))
        give.append(lax.dynamic_slice_in_dim(t, (1 - core) * hn, hn, axis=1).astype(dt))
    got = sibling_exchange(give, "sibling_contrib")
    two_d = lambda t: t.reshape(-1, t.shape[-1])
    pair = [pair_sum(two_d(a), two_d(b), "pair_sum_" + n, dt).reshape(a.shape)
            for n, a, b, dt in zip(names, keep, got, wire)]
    recv = chip_exchange(pair, True, "scatter_grads")
    halves = [chip_sum(r.reshape(N_CHIPS, -1, r.shape[-1]), "chip_sum_" + n) for n, r in zip(names, recv)]
    others = sibling_exchange(halves, "sibling_halves")
    grads = [jnp.where(core == 0, jnp.concatenate([a, b], axis=0), jnp.concatenate([b, a], axis=0))
             for a, b in zip(halves, others)]

    out = {}

    def update(n, g, wn, mn, vn):
        shape = wn.shape
        flat = (-1, shape[-1])
        d, nm, nv = adamw(g, wn.reshape(flat), mn.reshape(flat), vn.reshape(flat), "adamw_" + n)
        return tuple(t.reshape(shape) for t in (g, d, nm, nv))

    for n, g in zip(BIG, grads):
        out[n] = update(n, g, w[n], mom[n], vel[n])
    cat = lambda tree: jnp.concatenate([tree[n].reshape(-1, tree[n].shape[-1]) for n in SMALL_SHARDED], axis=0)
    sm = update("small", grads[-2], cat(w), cat(mom), cat(vel))
    off = 0
    for n in SMALL_SHARDED:
        rows = math.prod(w[n].shape[:-1])
        out[n] = tuple(t[off:off + rows].reshape(w[n].shape) for t in sm)
        off += rows
    rep_all = chip_exchange([grads[-1]], False, "gather_rep")[0].reshape(-1, LANES)
    rp = update("rep", rep_all, _pack_rep(w).reshape(-1, LANES), _pack_rep(mom).reshape(-1, LANES),
                _pack_rep(vel).reshape(-1, LANES))
    unpacked = [_unpack_rep(t.reshape(-1), w) for t in rp]
    for n in REPLICATED:
        out[n] = tuple(u[n] for u in unpacked)

    return (loss, grad_x, *[out[n][0] for n in WEIGHTS], *[out[n][1] for n in WEIGHTS],
            *[out[n][2] for n in WEIGHTS], *[out[n][3] for n in WEIGHTS])


def kernel(x, meta_tokens, s5_lambda_re, s5_lambda_im, s5_log_dt, s5_b_re, s5_b_im, s5_c_re, s5_c_im, s5_d, s5_w_glu, s5_w_out, attn_w_qkv, attn_q_gain, attn_k_gain, attn_w_out, ffn_w_gate, ffn_w_up, ffn_w_down, ln_gain, ln_bias, loss_target, m_meta_tokens, m_s5_lambda_re, m_s5_lambda_im, m_s5_log_dt, m_s5_b_re, m_s5_b_im, m_s5_c_re, m_s5_c_im, m_s5_d, m_s5_w_glu, m_s5_w_out, m_attn_w_qkv, m_attn_q_gain, m_attn_k_gain, m_attn_w_out, m_ffn_w_gate, m_ffn_w_up, m_ffn_w_down, m_ln_gain, m_ln_bias, v_meta_tokens, v_s5_lambda_re, v_s5_lambda_im, v_s5_log_dt, v_s5_b_re, v_s5_b_im, v_s5_c_re, v_s5_c_im, v_s5_d, v_s5_w_glu, v_s5_w_out, v_attn_w_qkv, v_attn_q_gain, v_attn_k_gain, v_attn_w_out, v_ffn_w_gate, v_ffn_w_up, v_ffn_w_down, v_ln_gain, v_ln_bias):
    given = locals()
    w = {n: given[n] for n in WEIGHTS}
    mom = {n: given["m_" + n] for n in WEIGHTS}
    vel = {n: given["v_" + n] for n in WEIGHTS}
    return _train_step(x, loss_target, w, mom, vel)
```

```python
import math

import jax
import jax.numpy as jnp
from jax import lax
from jax.experimental import pallas as pl
from jax.experimental.pallas import tpu as pltpu

F32 = jnp.float32
BF16 = jnp.bfloat16
MESH = pl.DeviceIdType.MESH

D_MODEL = 1024
N_META = 16
GRID_W = 64
HEAD_DIM = 64
N_Q_HEADS = 16
N_KV_HEADS = 4
ROPE_THETA = 10000.0
QK_EPS = 1e-6
S5_CH = 16
S5_GROUPS = 64
S5_STATE = 64
D_FF = 2816
LN_EPS = 1e-5
DEPTH = 4
ALPHA = (2.0 * DEPTH) ** 0.25
ADAM_LR, ADAM_B1, ADAM_B2, ADAM_EPS, ADAM_WD, ADAM_STEP = 0.001, 0.9, 0.999, 1e-08, 0.01, 10

LANES = 128
SUBLANES = 8
VMEM_LIMIT = 56 * 1024 * 1024

S5_T = 8
S5_GB = LANES // S5_CH
S5_NJ = S5_GROUPS // S5_GB
S5_W = S5_T * LANES
S5_SW = 2 * S5_GB * S5_STATE
S5_HALF = S5_SW // 2


def _round_up(a, b):
    return -(-a // b) * b


def _pick_tile(n, prefs):
    for t in prefs:
        if n % t == 0:
            return t
    return n


def _params(sem=None):
    kw = dict(vmem_limit_bytes=VMEM_LIMIT)
    if sem is not None:
        kw["dimension_semantics"] = sem
    return pltpu.CompilerParams(**kw)


def _dot(a, b, dims):
    return lax.dot_general(a, b, (dims, ((), ())), preferred_element_type=F32)


def _nn(a, b):
    return _dot(a, b, ((1,), (0,)))


def _nt(a, b):
    return _dot(a, b, ((1,), (1,)))


def _tn(a, b):
    return _dot(a, b, ((0,), (0,)))


def _compact(w):
    g, a0, a1, b0, b1 = w.shape
    w = jnp.transpose(w.reshape(S5_NJ, S5_GB, a0, a1, b0, b1), (0, 2, 1, 3, 4, 5))
    return w.reshape(S5_NJ, a0 * S5_GB * a1, b0 * b1)


def s5_prep(lam_re, lam_im, log_dt, b_re, b_im, c_re, c_im):
    hi = lax.Precision.HIGHEST
    t = S5_T
    dt = jnp.exp(log_dt)[..., None]
    taus = jnp.arange(t + 1, dtype=F32)[:, None, None, None]
    mag = jnp.exp(lam_re * dt)
    ang = lam_im * dt
    pr = jnp.concatenate([jnp.ones_like(mag)[None], (mag * jnp.cos(ang))[None],
                          jnp.exp(lam_re * dt * taus[2:]) * jnp.cos(ang * taus[2:])], axis=0)
    pi = jnp.concatenate([jnp.zeros_like(mag)[None], (mag * jnp.sin(ang))[None],
                          jnp.exp(lam_re * dt * taus[2:]) * jnp.sin(ang * taus[2:])], axis=0)
    abr, abi = pr[1], pi[1]
    nr, ni = abr - 1.0, abi
    den = lam_re * lam_re + lam_im * lam_im
    cr = (nr * lam_re + ni * lam_im) / den
    ci_ = (ni * lam_re - nr * lam_im) / den
    bbr = cr[..., None] * b_re - ci_[..., None] * b_im
    bbi = cr[..., None] * b_im + ci_[..., None] * b_re
    er = c_re[None] * pr[:, :, :, None, :] - c_im[None] * pi[:, :, :, None, :]
    ei = c_re[None] * pi[:, :, :, None, :] + c_im[None] * pr[:, :, :, None, :]
    nd, ng, ch = er.shape[1], er.shape[2], er.shape[3]
    lhs = jnp.concatenate([er[:t], -ei[:t]], axis=-1)
    lhs = jnp.transpose(lhs, (1, 2, 0, 3, 4)).reshape(nd, ng, t * ch, 2 * S5_STATE)
    rhs = jnp.concatenate([bbr, bbi], axis=-2)
    kk = jnp.einsum("dgmp,dgpc->dgmc", lhs, rhs, precision=hi)
    kk = jnp.transpose(kk.reshape(nd, ng, t, ch, ch), (2, 0, 1, 3, 4))
    zero = jnp.zeros_like(kk[0, 0])
    mg = jnp.stack([jnp.stack([(kk[i - s, 0] if i > s else zero) + (kk[s - i, 1] if s > i else zero)
                               + ((kk[0, 0] + kk[0, 1]) if i == s else zero) for i in range(t)])
                    for s in range(t)])
    mg = jnp.transpose(mg, (2, 0, 4, 1, 3))
    m = _compact(mg)
    pw_f = jnp.stack([pr[t - 1 - s, 0] for s in range(t)]), jnp.stack([pi[t - 1 - s, 0] for s in range(t)])
    pw_r = jnp.stack([pr[s, 1] for s in range(t)]), jnp.stack([pi[s, 1] for s in range(t)])
    wx = []
    for d, (qr, qi) in enumerate((pw_f, pw_r)):
        wr = qr[..., None] * bbr[d][None] - qi[..., None] * bbi[d][None]
        wi = qr[..., None] * bbi[d][None] + qi[..., None] * bbr[d][None]
        w = jnp.stack([wr, wi], axis=0)
        w = jnp.transpose(w, (2, 1, 4, 0, 3))
        wx.append(_compact(w))
    ci = []
    for d in range(2):
        exps = [i + 1 for i in range(t)] if d == 0 else [t - i for i in range(t)]
        e_r = jnp.stack([er[e, d] for e in exps])
        e_i = jnp.stack([ei[e, d] for e in exps])
        w = jnp.stack([e_r, -e_i], axis=0)
        w = jnp.transpose(w, (2, 0, 4, 1, 3))
        ci.append(_compact(w))
    at = jnp.stack([pr[t], pi[t]], axis=1)
    at = at.reshape(2, 2, S5_NJ, S5_GB * S5_STATE)
    at = jnp.transpose(at, (0, 2, 1, 3)).reshape(2, S5_NJ, 1, S5_SW)
    return m, jnp.stack(wx), jnp.stack(ci), at


def _chunk_rows(ref, nc):
    return jnp.concatenate([ref[pl.ds(s, nc, stride=S5_T), :] for s in range(S5_T)], axis=1)


def _cmul(ar, ai, sr, si):
    return ar * sr - ai * si, ar * si + ai * sr


def _scan_tiles(nc, reverse, step):
    nt = nc // SUBLANES

    def body(it, carry):
        tix = (nt - 1 - it) if reverse else it
        k0 = pl.multiple_of(tix * SUBLANES, SUBLANES)
        return step(k0, carry)

    return body, nt


def _s5_specs(nc):
    hbm = pl.BlockSpec(memory_space=pl.ANY)
    aspec = pl.BlockSpec((1, 1, S5_SW), lambda j: (j, 0, 0))
    cspec = pl.BlockSpec((1, nc, S5_W), lambda j: (j, 0, 0))
    return hbm, aspec, cspec


def _s5_fetch(j, tok_hbm, w_hbms, tok_s, w_s, sems):
    cols = pl.ds(pl.multiple_of(j * LANES, LANES), LANES)
    cps = [pltpu.make_async_copy(tok_hbm.at[:, cols], tok_s, sems.at[0])]
    for i, w in enumerate(w_hbms):
        cps.append(pltpu.make_async_copy(w.at[j], w_s.at[i], sems.at[1 + i]))
    for cp in cps:
        cp.start()
    return cols, cps


def s5_forward(u, m, wx, ci, at, n_valid, prefetch=()):
    lp = u.shape[0]
    nc = lp // S5_T
    nvc = n_valid // S5_T
    npf = len(prefetch)
    pf_specs, pf_shapes, pf_sems = _prefetch_specs(prefetch)

    def body(*refs):
        u_hbm, m_hbm, wxf_hbm, wxr_hbm, cif_hbm, cir_hbm, atf_ref, atr_ref = refs[:8]
        pf_ins = refs[8:8 + npf]
        y_hbm, lhs_ref, sp_ref, sn_ref = refs[8 + npf:12 + npf]
        pf_outs = refs[12 + npf:12 + 2 * npf]
        tok_s, w_s, xf_s, xr_s, sems = refs[12 + 2 * npf:17 + 2 * npf]
        j = pl.program_id(0)
        _prefetch_run(j == 0, j == S5_NJ - 1, pf_ins, pf_outs, refs[17 + 2 * npf:])
        cols, cps = _s5_fetch(j, u_hbm, (m_hbm, wxf_hbm, wxr_hbm, cif_hbm, cir_hbm), tok_s, w_s, sems)
        cps[0].wait()
        lhs = _chunk_rows(tok_s, nc)
        rows = lax.broadcasted_iota(jnp.int32, lhs.shape, 0)
        lhs = jnp.where(rows < nvc, lhs, 0.0).astype(BF16)
        lhs_ref[0] = lhs
        cps[2].wait()
        cps[3].wait()
        xf_s[...] = _nn(lhs, w_s[1])
        xr_s[...] = _nn(lhs, w_s[2])
        afr, afi = atf_ref[0, :, :S5_HALF], atf_ref[0, :, S5_HALF:]
        arr, ari = atr_ref[0, :, :S5_HALF], atr_ref[0, :, S5_HALF:]

        def scan_step(x_s, ar, ai, descending):
            def step(k0, carry):
                sr, si = carry
                x = x_s[pl.ds(k0, SUBLANES), :]
                outs = [None] * SUBLANES
                order = reversed(range(SUBLANES)) if descending else range(SUBLANES)
                for r in order:
                    outs[r] = jnp.concatenate([sr, si], axis=1)
                    nr, ni = _cmul(ar, ai, sr, si)
                    sr = nr + x[r:r + 1, :S5_HALF]
                    si = ni + x[r:r + 1, S5_HALF:]
                x_s[pl.ds(k0, SUBLANES), :] = jnp.concatenate(outs, axis=0)
                return sr, si
            return step

        zero = jnp.zeros((1, S5_HALF), F32)
        fb, nt = _scan_tiles(nc, False, scan_step(xf_s, afr, afi, False))
        lax.fori_loop(0, nt, fb, (zero, zero))
        rb, nt = _scan_tiles(nc, True, scan_step(xr_s, arr, ari, True))
        lax.fori_loop(0, nt, rb, (zero, zero))
        sp = xf_s[...].astype(BF16)
        sn = xr_s[...].astype(BF16)
        sp_ref[0] = sp
        sn_ref[0] = sn
        cps[1].wait()
        cps[4].wait()
        cps[5].wait()
        y = _nn(lhs, w_s[0]) + _nn(sp, w_s[3]) + _nn(sn, w_s[4])
        for i in range(S5_T):
            tok_s[pl.ds(i, nc, stride=S5_T), :] = y[:, i * LANES:(i + 1) * LANES]
        pltpu.sync_copy(tok_s, y_hbm.at[:, cols])

    hbm, aspec, cspec = _s5_specs(nc)
    return pl.pallas_call(
        body, name="s5_forward", grid=(S5_NJ,),
        in_specs=[hbm] * 6 + [aspec, aspec] + pf_specs,
        out_specs=[hbm, cspec, cspec, cspec] + pf_specs,
        out_shape=[jax.ShapeDtypeStruct((lp, D_MODEL), F32)] + [jax.ShapeDtypeStruct((S5_NJ, nc, S5_W), BF16)] * 3 + pf_shapes,
        scratch_shapes=[pltpu.VMEM((lp, LANES), F32), pltpu.VMEM((5, S5_W, S5_W), BF16),
                        pltpu.VMEM((nc, S5_SW), F32), pltpu.VMEM((nc, S5_SW), F32), pltpu.SemaphoreType.DMA((6,))] + pf_sems,
        compiler_params=_params(("arbitrary",)),
    )(u, m, wx[0], wx[1], ci[0], ci[1], at[0], at[1], *prefetch)


def s5_backward(dy, dhs, m, wx, ci, at, sp, sn, n_valid):
    lp = dy.shape[0]
    nc = lp // S5_T
    nvc = n_valid // S5_T

    def body(dy_hbm, dhs_hbm, m_hbm, wxf_hbm, wxr_hbm, cif_hbm, cir_hbm, atf_ref, atr_ref, sp_ref, sn_ref,
             dh_hbm, ldy_ref, dxf_ref, dxr_ref, daf_ref, dar_ref, tok_s, w_s, gf_s, gr_s, sems):
        j = pl.program_id(0)
        cols, cps = _s5_fetch(j, dy_hbm, (m_hbm, wxf_hbm, wxr_hbm, cif_hbm, cir_hbm), tok_s, w_s, sems)
        cps[0].wait()
        ldy = _chunk_rows(tok_s, nc)
        rows = lax.broadcasted_iota(jnp.int32, ldy.shape, 0)
        ldy = jnp.where(rows < nvc, ldy, 0.0).astype(BF16)
        ldy_ref[0] = ldy
        resid = pltpu.make_async_copy(dhs_hbm.at[:, cols], tok_s, sems.at[0])
        resid.start()
        cps[4].wait()
        cps[5].wait()
        gf_s[...] = _nt(ldy, w_s[3])
        gr_s[...] = _nt(ldy, w_s[4])
        afr, afi = atf_ref[0, :, :S5_HALF], atf_ref[0, :, S5_HALF:]
        arr, ari = atr_ref[0, :, :S5_HALF], atr_ref[0, :, S5_HALF:]

        def adj_step(g_s, s_ref, ar, ai, descending):
            def step(k0, carry):
                gr_, gi_, dr_, di_ = carry
                g = g_s[pl.ds(k0, SUBLANES), :]
                p = s_ref[0, pl.ds(k0, SUBLANES), :].astype(F32)
                outs = [None] * SUBLANES
                order = reversed(range(SUBLANES)) if descending else range(SUBLANES)
                for r in order:
                    outs[r] = jnp.concatenate([gr_, gi_], axis=1)
                    pr_, pi_ = p[r:r + 1, :S5_HALF], p[r:r + 1, S5_HALF:]
                    dr_ = dr_ + gr_ * pr_ + gi_ * pi_
                    di_ = di_ + gi_ * pr_ - gr_ * pi_
                    nr, ni = _cmul(ar, -ai, gr_, gi_)
                    gr_ = nr + g[r:r + 1, :S5_HALF]
                    gi_ = ni + g[r:r + 1, S5_HALF:]
                g_s[pl.ds(k0, SUBLANES), :] = jnp.concatenate(outs, axis=0)
                return gr_, gi_, dr_, di_
            return step

        zero = jnp.zeros((1, S5_HALF), F32)
        fb, nt = _scan_tiles(nc, True, adj_step(gf_s, sp_ref, afr, afi, True))
        _, _, dr_, di_ = lax.fori_loop(0, nt, fb, (zero,) * 4)
        daf_ref[0] = jnp.concatenate([dr_, di_], axis=1)
        rb, nt = _scan_tiles(nc, False, adj_step(gr_s, sn_ref, arr, ari, False))
        _, _, dr_, di_ = lax.fori_loop(0, nt, rb, (zero,) * 4)
        dar_ref[0] = jnp.concatenate([dr_, di_], axis=1)
        dxf = gf_s[...].astype(BF16)
        dxr = gr_s[...].astype(BF16)
        dxf_ref[0] = dxf
        dxr_ref[0] = dxr
        cps[1].wait()
        cps[2].wait()
        cps[3].wait()
        du = _nt(ldy, w_s[0]) + _nt(dxf, w_s[1]) + _nt(dxr, w_s[2])
        rows = lax.broadcasted_iota(jnp.int32, du.shape, 0)
        du = jnp.where(rows < nvc, du, 0.0)
        resid.wait()
        for s in range(S5_T):
            tok_s[pl.ds(s, nc, stride=S5_T), :] += du[:, s * LANES:(s + 1) * LANES]
        pltpu.sync_copy(tok_s, dh_hbm.at[:, cols])

    hbm, aspec, cspec = _s5_specs(nc)
    return pl.pallas_call(
        body, name="s5_backward", grid=(S5_NJ,),
        in_specs=[hbm] * 7 + [aspec, aspec, cspec, cspec],
        out_specs=[hbm, cspec, cspec, cspec, aspec, aspec],
        out_shape=[jax.ShapeDtypeStruct((lp, D_MODEL), F32)] + [jax.ShapeDtypeStruct((S5_NJ, nc, S5_W), BF16)] * 3
        + [jax.ShapeDtypeStruct((S5_NJ, 1, S5_SW), F32)] * 2,
        scratch_shapes=[pltpu.VMEM((lp, LANES), F32), pltpu.VMEM((5, S5_W, S5_W), BF16),
                        pltpu.VMEM((nc, S5_SW), F32), pltpu.VMEM((nc, S5_SW), F32), pltpu.SemaphoreType.DMA((6,))],
        compiler_params=_params(("arbitrary",)),
    )(dy, dhs, m, wx[0], wx[1], ci[0], ci[1], at[0], at[1], sp, sn)


S5_CW = LANES


def _replicate_matrix(b1):
    b0n = S5_CW // b1
    eye0 = jnp.eye(b0n, dtype=F32)
    eye1 = jnp.eye(b1, dtype=F32)
    r = jnp.einsum("ab,cd->acbd", eye0, eye1)[:, :, :, None, :]
    r = jnp.broadcast_to(r, (b0n, b1, b0n, S5_GB, b1))
    return r.reshape(S5_CW, b0n * S5_GB * b1).astype(BF16)


def _same_group(a1, b1):
    rg = (lax.broadcasted_iota(jnp.int32, (S5_W, S5_W), 0) // a1) % S5_GB
    cg = (lax.broadcasted_iota(jnp.int32, (S5_W, S5_W), 1) // b1) % S5_GB
    return rg == cg


def blockdiag_expand(compact, a1, b1, name):
    nj = compact.shape[0]

    def body(c_ref, r_ref, o_ref):
        rep = _nn(c_ref[0].astype(BF16), r_ref[...])
        o_ref[0] = jnp.where(_same_group(a1, b1), rep, 0.0).astype(BF16)

    return pl.pallas_call(
        body, name=name, grid=(nj,),
        in_specs=[pl.BlockSpec((1, S5_W, S5_CW), lambda j: (j, 0, 0)), _full_spec((S5_CW, S5_W))],
        out_specs=pl.BlockSpec((1, S5_W, S5_W), lambda j: (j, 0, 0)),
        out_shape=jax.ShapeDtypeStruct((nj, S5_W, S5_W), BF16),
        compiler_params=_params(("parallel",)),
    )(compact, _replicate_matrix(b1))


def bmm_tn_compact(a, b, a1, b1, name):
    nj, k, wa = a.shape
    wb = b.shape[2]

    def body(a_ref, b_ref, r_ref, o_ref):
        prod = jnp.where(_same_group(a1, b1), _tn(a_ref[0], b_ref[0]), 0.0)
        hi = prod.astype(BF16)
        lo = (prod - hi.astype(F32)).astype(BF16)
        o_ref[0] = _nt(hi, r_ref[...]) + _nt(lo, r_ref[...])

    return pl.pallas_call(
        body, name=name, grid=(nj,),
        in_specs=[pl.BlockSpec((1, k, wa), lambda j: (j, 0, 0)), pl.BlockSpec((1, k, wb), lambda j: (j, 0, 0)),
                  _full_spec((S5_CW, S5_W))],
        out_specs=pl.BlockSpec((1, wa, S5_CW), lambda j: (j, 0, 0)),
        out_shape=jax.ShapeDtypeStruct((nj, wa, S5_CW), F32),
        compiler_params=_params(("parallel",)),
    )(a, b, _replicate_matrix(b1))


def _tm(lp):
    return _pick_tile(lp, (768, 256))


def _row_spec(tm, width):
    return pl.BlockSpec((tm, width), lambda i: (i, 0))


def _full_spec(shape):
    return pl.BlockSpec(shape, lambda *_: (0,) * len(shape))


def _gelu(v):
    return 0.5 * v * (1.0 + lax.erf(v * (2.0 ** -0.5)))


def _gelu_grad(v):
    return 0.5 * (1.0 + lax.erf(v * (2.0 ** -0.5))) + v * jnp.exp(-0.5 * v * v) * (2.0 * math.pi) ** -0.5


def _layer_norm(r, gain, bias):
    mean = jnp.mean(r, axis=-1, keepdims=True)
    c = r - mean
    var = jnp.mean(c * c, axis=-1, keepdims=True)
    return c * lax.rsqrt(var + LN_EPS) * gain + bias


def glu_forward(y, h, dvec, wglu):
    lp, d = y.shape
    tm = _tm(lp)

    def body(y_ref, h_ref, d_ref, w_ref, v_ref, t_ref, g_ref, z_ref):
        v = y_ref[...] + d_ref[...] * h_ref[...]
        g = _gelu(v)
        gb = g.astype(BF16)
        t = _nn(gb, w_ref[...])
        v_ref[...] = v
        t_ref[...] = t
        g_ref[...] = gb
        z_ref[...] = (g * jax.nn.sigmoid(t)).astype(BF16)

    rs = _row_spec(tm, d)
    return pl.pallas_call(
        body, name="glu_forward", grid=(lp // tm,),
        in_specs=[rs, rs, _full_spec((1, d)), _full_spec((d, d))],
        out_specs=[rs, rs, rs, rs],
        out_shape=[jax.ShapeDtypeStruct((lp, d), F32)] * 2 + [jax.ShapeDtypeStruct((lp, d), BF16)] * 2,
        compiler_params=_params(("parallel",)),
    )(y, h, dvec, wglu)


def proj_ln_forward(z, w, h, gain, bias, name):
    lp, k = z.shape
    d = w.shape[1]
    tm = _tm(lp)

    def body(z_ref, w_ref, h_ref, g_ref, b_ref, r_ref, o_ref):
        r = ALPHA * h_ref[...] + _nn(z_ref[...], w_ref[...])
        r_ref[...] = r
        o_ref[...] = _layer_norm(r, g_ref[...], b_ref[...])

    rs = _row_spec(tm, d)
    return pl.pallas_call(
        body, name=name, grid=(lp // tm,),
        in_specs=[_row_spec(tm, k), _full_spec((k, d)), rs, _full_spec((1, d)), _full_spec((1, d))],
        out_specs=[rs, rs],
        out_shape=[jax.ShapeDtypeStruct((lp, d), F32)] * 2,
        compiler_params=_params(("parallel",)),
    )(z, w, h, gain, bias)


FFN_NB = 1408


def ffn_up_forward(h, wg, wu):
    lp, d = h.shape
    dff = wg.shape[1]
    tm = _tm(lp)

    def body(h_ref, wg_ref, wu_ref, a_ref, b_ref, f_ref):
        hb = h_ref[...].astype(BF16)
        a = _nn(hb, wg_ref[...])
        b = _nn(hb, wu_ref[...])
        a_ref[...] = a.astype(BF16)
        b_ref[...] = b.astype(BF16)
        f_ref[...] = (a * jax.nn.sigmoid(a) * b).astype(BF16)

    ws = pl.BlockSpec((d, FFN_NB), lambda n, i: (0, n))
    os_ = pl.BlockSpec((tm, FFN_NB), lambda n, i: (i, n))
    return pl.pallas_call(
        body, name="ffn_up_forward", grid=(dff // FFN_NB, lp // tm),
        in_specs=[pl.BlockSpec((tm, d), lambda n, i: (i, 0)), ws, ws],
        out_specs=[os_, os_, os_],
        out_shape=[jax.ShapeDtypeStruct((lp, dff), BF16)] * 3,
        compiler_params=_params(("parallel", "parallel")),
    )(h, wg, wu)


def ln_backward(dh, r, gain):
    lp, d = dh.shape
    tm = _tm(lp)

    def body(dh_ref, r_ref, g_ref, dr_ref, s_ref):
        r_ = r_ref[...]
        dh_ = dh_ref[...]
        mean = jnp.mean(r_, axis=-1, keepdims=True)
        c = r_ - mean
        var = jnp.mean(c * c, axis=-1, keepdims=True)
        rstd = lax.rsqrt(var + LN_EPS)
        xh = c * rstd
        dxh = dh_ * g_ref[...]
        m1 = jnp.mean(dxh, axis=-1, keepdims=True)
        m2 = jnp.mean(dxh * xh, axis=-1, keepdims=True)
        dr_ref[...] = rstd * (dxh - m1 - xh * m2)

        @pl.when(pl.program_id(0) == 0)
        def _():
            s_ref[...] = jnp.zeros_like(s_ref)

        s_ref[0:1, :] += jnp.sum(dh_ * xh, axis=0, keepdims=True)
        s_ref[1:2, :] += jnp.sum(dh_, axis=0, keepdims=True)

    rs = _row_spec(tm, d)
    return pl.pallas_call(
        body, name="ln_backward", grid=(lp // tm,),
        in_specs=[rs, rs, _full_spec((1, d))],
        out_specs=[rs, _full_spec((SUBLANES, d))],
        out_shape=[jax.ShapeDtypeStruct((lp, d), F32), jax.ShapeDtypeStruct((SUBLANES, d), F32)],
        compiler_params=_params(("arbitrary",)),
    )(dh, r, gain)


def ffn_backward_act(dr, wd, a, b):
    lp, d = dr.shape
    dff = wd.shape[0]
    tm = _tm(lp)

    def body(dr_ref, wd_ref, a_ref, b_ref, da_ref, db_ref):
        df = _nt(dr_ref[...].astype(BF16), wd_ref[...])
        a_ = a_ref[...].astype(F32)
        b_ = b_ref[...].astype(F32)
        sg = jax.nn.sigmoid(a_)
        da_ref[...] = (df * b_ * sg * (1.0 + a_ * (1.0 - sg))).astype(BF16)
        db_ref[...] = (df * a_ * sg).astype(BF16)

    os_ = pl.BlockSpec((tm, FFN_NB), lambda n, i: (i, n))
    return pl.pallas_call(
        body, name="ffn_backward_act", grid=(dff // FFN_NB, lp // tm),
        in_specs=[pl.BlockSpec((tm, d), lambda n, i: (i, 0)), pl.BlockSpec((FFN_NB, d), lambda n, i: (n, 0)), os_, os_],
        out_specs=[os_, os_],
        out_shape=[jax.ShapeDtypeStruct((lp, dff), BF16)] * 2,
        compiler_params=_params(("parallel", "parallel")),
    )(dr, wd, a, b)


def resid_nt(dr, xs, ws, name):
    lp, d = dr.shape
    tm = _tm(lp)
    n = len(xs)

    def body(*refs):
        acc = ALPHA * refs[0][...]
        for i in range(n):
            acc = acc + _nt(refs[1 + i][...], refs[1 + n + i][...])
        refs[-1][...] = acc

    rs = _row_spec(tm, d)
    in_specs = [rs] + [_row_spec(tm, x.shape[1]) for x in xs] + [_full_spec(w.shape) for w in ws]
    return pl.pallas_call(
        body, name=name, grid=(lp // tm,),
        in_specs=in_specs, out_specs=rs,
        out_shape=jax.ShapeDtypeStruct((lp, d), F32),
        compiler_params=_params(("parallel",)),
    )(dr, *xs, *ws)


def mm_tn(x, y, name):
    lp, k = x.shape
    n = y.shape[1]
    tm = _tm(lp)
    nb = _pick_tile(n, (512, 1408))

    def body(x_ref, y_ref, o_ref):
        @pl.when(pl.program_id(1) == 0)
        def _():
            o_ref[...] = jnp.zeros_like(o_ref)

        o_ref[...] += _tn(x_ref[...].astype(BF16), y_ref[...].astype(BF16))

    return pl.pallas_call(
        body, name=name, grid=(n // nb, lp // tm),
        in_specs=[pl.BlockSpec((tm, k), lambda j, i: (i, 0)), pl.BlockSpec((tm, nb), lambda j, i: (i, j))],
        out_specs=pl.BlockSpec((k, nb), lambda j, i: (0, j)),
        out_shape=jax.ShapeDtypeStruct((k, n), F32),
        compiler_params=_params(("parallel", "arbitrary")),
    )(x, y)


def glu_backward1(dr, wout, g, t):
    lp, d = dr.shape
    tm = _tm(lp)

    def body(dr_ref, w_ref, g_ref, t_ref, dt_ref, dgd_ref):
        dz = _nt(dr_ref[...].astype(BF16), w_ref[...])
        s = jax.nn.sigmoid(t_ref[...])
        dgd_ref[...] = dz * s
        dt_ref[...] = (dz * g_ref[...].astype(F32) * s * (1.0 - s)).astype(BF16)

    rs = _row_spec(tm, d)
    return pl.pallas_call(
        body, name="glu_backward1", grid=(lp // tm,),
        in_specs=[rs, _full_spec((d, d)), rs, rs],
        out_specs=[rs, rs],
        out_shape=[jax.ShapeDtypeStruct((lp, d), BF16), jax.ShapeDtypeStruct((lp, d), F32)],
        compiler_params=_params(("parallel",)),
    )(dr, wout, g, t)


def glu_backward2(dt, dgd, wglu, v, h, dvec, dr):
    lp, d = dt.shape
    tm = _tm(lp)

    def body(dt_ref, dgd_ref, w_ref, v_ref, h_ref, d_ref, dr_ref, dv_ref, dhs_ref, s_ref):
        dg = dgd_ref[...] + _nt(dt_ref[...], w_ref[...])
        dv = dg * _gelu_grad(v_ref[...])
        dv_ref[...] = dv
        dhs_ref[...] = ALPHA * dr_ref[...] + dv * d_ref[...]

        @pl.when(pl.program_id(0) == 0)
        def _():
            s_ref[...] = jnp.zeros_like(s_ref)

        s_ref[0:1, :] += jnp.sum(dv * h_ref[...], axis=0, keepdims=True)

    rs = _row_spec(tm, d)
    return pl.pallas_call(
        body, name="glu_backward2", grid=(lp // tm,),
        in_specs=[rs, rs, _full_spec((d, d)), rs, rs, _full_spec((1, d)), rs],
        out_specs=[rs, rs, _full_spec((SUBLANES, d))],
        out_shape=[jax.ShapeDtypeStruct((lp, d), F32)] * 2 + [jax.ShapeDtypeStruct((SUBLANES, d), F32)],
        compiler_params=_params(("arbitrary",)),
    )(dt, dgd, wglu, v, h, dvec, dr)


def loss_backward(hf, tgt, n_valid):
    lp, d = hf.shape
    tm = _tm(lp)

    def body(h_ref, t_ref, dh_ref, s_ref):
        rows = pl.program_id(0) * tm + lax.broadcasted_iota(jnp.int32, (tm, d), 0)
        ok = (rows >= N_META) & (rows < n_valid)
        e = jnp.where(ok, h_ref[...] - t_ref[...], 0.0)
        dh_ref[...] = e * (1.0 / d)

        @pl.when(pl.program_id(0) == 0)
        def _():
            s_ref[...] = jnp.zeros_like(s_ref)

        sq = e * e
        part = sq[:, 0:LANES]
        for c in range(1, d // LANES):
            part = part + sq[:, c * LANES:(c + 1) * LANES]
        acc = part[0:SUBLANES]
        for r in range(1, tm // SUBLANES):
            acc = acc + part[r * SUBLANES:(r + 1) * SUBLANES]
        s_ref[...] += acc * (0.5 / d)

    rs = _row_spec(tm, d)
    return pl.pallas_call(
        body, name="loss_backward", grid=(lp // tm,),
        in_specs=[rs, rs], out_specs=[rs, _full_spec((SUBLANES, LANES))],
        out_shape=[jax.ShapeDtypeStruct((lp, d), F32), jax.ShapeDtypeStruct((SUBLANES, LANES), F32)],
        compiler_params=_params(("arbitrary",)),
    )(hf, tgt)


N_QB = N_Q_HEADS // 2
N_KB = N_KV_HEADS
QKV_W = (N_QB + 2 * N_KB) * LANES
Q_SCALE = HEAD_DIM ** -0.5 * math.log2(math.e)


def rope_tables(lp, n_valid):
    t = jnp.arange(lp, dtype=jnp.int32)
    real = (t >= N_META) & (t < n_valid)
    pos = jnp.where(real, t - N_META, 0)
    row = (pos // GRID_W).astype(F32)
    col = (pos % GRID_W).astype(F32)
    axis_dim = HEAD_DIM // 2
    inv = ROPE_THETA ** (-jnp.arange(0, axis_dim, 2, dtype=F32) / axis_dim)
    ar = row[:, None] * inv[None, :]
    ac = col[:, None] * inv[None, :]
    cos = jnp.concatenate([jnp.cos(ar), jnp.cos(ar), jnp.cos(ac), jnp.cos(ac)], axis=1)
    sin = jnp.concatenate([-jnp.sin(ar), jnp.sin(ar), -jnp.sin(ac), jnp.sin(ac)], axis=1)
    return jnp.tile(cos, (1, 2)), jnp.tile(sin, (1, 2))


def head_sum_matrix():
    return jnp.kron(jnp.eye(2, dtype=F32), jnp.ones((HEAD_DIM, HEAD_DIM), F32)).astype(BF16)


def _segsum(x, e):
    hi = x.astype(BF16)
    lo = (x - hi.astype(F32)).astype(BF16)
    return _nn(hi, e) + _nn(lo, e)


def _swap_halves(x):
    lane = lax.broadcasted_iota(jnp.int32, x.shape, 1)
    quarter = HEAD_DIM // 4
    return jnp.where(lane % (2 * quarter) < quarter, pltpu.roll(x, LANES - quarter, 1), pltpu.roll(x, quarter, 1))


def qkv_forward(h, w2, gq, gk, cos, sin, e):
    lp, d = h.shape
    tm = _tm(lp)
    kw, vw = N_KB * LANES, N_KB * LANES

    def body(h_ref, w_ref, gq_ref, gk_ref, cos_ref, sin_ref, e_ref, raw_ref, q_ref, k_ref, v_ref, qt_ref, vt_ref):
        raw = _nn(h_ref[...].astype(BF16), w_ref[...])
        raw_ref[...] = raw
        c, s_, em = cos_ref[...], sin_ref[...], e_ref[...]
        for cb in range(N_QB + N_KB):
            t = raw[:, cb * LANES:(cb + 1) * LANES]
            rstd = lax.rsqrt(_segsum(t * t, em) * (1.0 / HEAD_DIM) + QK_EPS)
            n = t * rstd * (gq_ref[...] if cb < N_QB else gk_ref[...])
            rot = n * c + _swap_halves(n) * s_
            if cb < N_QB:
                qs = rot * Q_SCALE
                q_ref[:, cb * LANES:(cb + 1) * LANES] = qs.astype(BF16)
                qt_ref[cb * LANES:(cb + 1) * LANES, :] = qs.T.astype(BF16)
            else:
                k_ref[:, (cb - N_QB) * LANES:(cb - N_QB + 1) * LANES] = rot.astype(BF16)
        v_ref[...] = raw[:, (N_QB + N_KB) * LANES:].astype(BF16)
        for cb in range(N_KB):
            lo = (N_QB + N_KB + cb) * LANES
            vt_ref[cb * LANES:(cb + 1) * LANES, :] = raw[:, lo:lo + LANES].T.astype(BF16)

    col_spec = lambda rows: pl.BlockSpec((rows, tm), lambda i: (0, i))
    return pl.pallas_call(
        body, name="qkv_forward", grid=(lp // tm,),
        in_specs=[_row_spec(tm, d), _full_spec((d, QKV_W)), _full_spec((1, LANES)), _full_spec((1, LANES)),
                  _row_spec(tm, LANES), _row_spec(tm, LANES), _full_spec((LANES, LANES))],
        out_specs=[_row_spec(tm, QKV_W), _row_spec(tm, N_QB * LANES), _row_spec(tm, kw), _row_spec(tm, vw),
                   col_spec(N_QB * LANES), col_spec(vw)],
        out_shape=[jax.ShapeDtypeStruct((lp, QKV_W), F32), jax.ShapeDtypeStruct((lp, N_QB * LANES), BF16),
                   jax.ShapeDtypeStruct((lp, kw), BF16), jax.ShapeDtypeStruct((lp, vw), BF16),
                   jax.ShapeDtypeStruct((N_QB * LANES, lp), BF16), jax.ShapeDtypeStruct((vw, lp), BF16)],
        compiler_params=_params(("parallel",)),
    )(h, w2, gq, gk, cos, sin, e)


def qkv_backward(dqs, dk2, dv2, raw, gq, gk, cos, sin, e):
    lp = raw.shape[0]
    tm = _tm(lp)

    def body(dq_ref, dk_ref, dv_ref, raw_ref, gq_ref, gk_ref, cos_ref, sin_ref, e_ref, d_ref, s_ref):
        @pl.when(pl.program_id(0) == 0)
        def _():
            s_ref[...] = jnp.zeros_like(s_ref)

        c, s_, em = cos_ref[...], sin_ref[...], e_ref[...]
        gsum = [jnp.zeros((1, LANES), F32), jnp.zeros((1, LANES), F32)]
        for cb in range(N_QB + N_KB):
            isq = cb < N_QB
            t = raw_ref[:, cb * LANES:(cb + 1) * LANES]
            if isq:
                drot = dq_ref[:, cb * LANES:(cb + 1) * LANES] * (HEAD_DIM ** -0.5)
            else:
                drot = dk_ref[:, (cb - N_QB) * LANES:(cb - N_QB + 1) * LANES] * math.log(2.0)
            gain = gq_ref[...] if isq else gk_ref[...]
            rstd = lax.rsqrt(_segsum(t * t, em) * (1.0 / HEAD_DIM) + QK_EPS)
            dn = drot * c + _swap_halves(drot * s_)
            xh = t * rstd
            gsum[0 if isq else 1] = gsum[0 if isq else 1] + jnp.sum(dn * xh, axis=0, keepdims=True)
            w = dn * gain
            mw = _segsum(w * xh, em) * (1.0 / HEAD_DIM)
            d_ref[:, cb * LANES:(cb + 1) * LANES] = (rstd * (w - xh * mw)).astype(BF16)
        d_ref[:, (N_QB + N_KB) * LANES:] = dv_ref[...].astype(BF16)
        s_ref[0:1, :] += gsum[0]
        s_ref[1:2, :] += gsum[1]

    kw = N_KB * LANES
    return pl.pallas_call(
        body, name="qkv_backward", grid=(lp // tm,),
        in_specs=[_row_spec(tm, N_QB * LANES), _row_spec(tm, kw), _row_spec(tm, kw), _row_spec(tm, QKV_W),
                  _full_spec((1, LANES)), _full_spec((1, LANES)), _row_spec(tm, LANES), _row_spec(tm, LANES),
                  _full_spec((LANES, LANES))],
        out_specs=[_row_spec(tm, QKV_W), _full_spec((SUBLANES, LANES))],
        out_shape=[jax.ShapeDtypeStruct((lp, QKV_W), BF16), jax.ShapeDtypeStruct((SUBLANES, LANES), F32)],
        compiler_params=_params(("arbitrary",)),
    )(dqs, dk2, dv2, raw, gq, gk, cos, sin, e)


NEG = -1e30
Q_PER_KV = N_Q_HEADS // N_KV_HEADS


def _half_masks(x):
    lane = lax.broadcasted_iota(jnp.int32, x.shape, 1)
    zero = jnp.zeros_like(x)
    return jnp.where(lane < HEAD_DIM, x, zero), jnp.where(lane >= HEAD_DIM, x, zero)


ATTN_TR = 16


def _attn_tiles(lp):
    t = _pick_tile(lp, (1408, 256))
    return t, t


def _attn_tiles_bwd(lp):
    return _pick_tile(lp, (768, 256)), _pick_tile(lp, (1408, 256))


def attn_forward_t(qs, k2, v2t, n_valid, prefetch=()):
    lp = qs.shape[0]
    tq, kb = _attn_tiles(lp)
    nk = lp // kb
    gw = 2 * LANES
    nr = kb // ATTN_TR
    pad0 = n_valid - (nk - 1) * kb
    npf = len(prefetch)
    pf_specs, pf_shapes, pf_sems = _prefetch_specs(prefetch)
    nq = lp // tq

    def body(*refs):
        q_ref, k_ref, vt_ref = refs[:3]
        pf_ins = refs[3:3 + npf]
        o_ref, lse_ref = refs[3 + npf:5 + npf]
        pf_outs = refs[5 + npf:5 + 2 * npf]
        m_s, l_s, acc_s, s_s, p_s = refs[5 + 2 * npf:10 + 2 * npf]
        j = pl.program_id(2)
        first = (pl.program_id(0) == 0) & (pl.program_id(1) == 0) & (j == 0)
        last = (pl.program_id(0) == N_KV_HEADS - 1) & (pl.program_id(1) == nq - 1) & (j == nk - 1)
        _prefetch_run(first, last, pf_ins, pf_outs, refs[10 + 2 * npf:])

        @pl.when(j == 0)
        def _():
            m_s[...] = jnp.full_like(m_s, NEG)
            l_s[...] = jnp.zeros_like(l_s)
            acc_s[...] = jnp.zeros_like(acc_s)

        ks = _half_masks(k_ref[...])
        for pair in range(2):
            qp = q_ref[:, pair * LANES:(pair + 1) * LANES]
            for half in range(2):
                hh = 2 * pair + half
                s_s[...] = _nt(ks[half], qp)

                if pad0 < kb:
                    @pl.when(j == nk - 1)
                    def _():
                        s_s[pad0:, :] = jnp.full((kb - pad0, tq), NEG, F32)

                def max_step(r, run):
                    rows = pl.ds(r * ATTN_TR, ATTN_TR)
                    blk = s_s[rows, :]
                    for t in range(ATTN_TR // SUBLANES):
                        run = jnp.maximum(run, blk[t * SUBLANES:(t + 1) * SUBLANES])
                    return run

                run = jnp.full((SUBLANES, tq), NEG, F32)
                for r in range(nr):
                    run = max_step(r, run)
                m_prev = m_s[hh:hh + 1, :]
                m_new = jnp.maximum(m_prev, jnp.max(run, axis=0, keepdims=True))
                alpha = jnp.exp2(m_prev - m_new)
                m_s[hh:hh + 1, :] = m_new

                def exp_step(r, run):
                    rows = pl.ds(r * ATTN_TR, ATTN_TR)
                    p = jnp.exp2(s_s[rows, :] - m_new)
                    p_s[rows, :] = p.astype(BF16)
                    for t in range(ATTN_TR // SUBLANES):
                        run = run + p[t * SUBLANES:(t + 1) * SUBLANES]
                    return run

                run = jnp.zeros((SUBLANES, tq), F32)
                for r in range(nr):
                    run = exp_step(r, run)
                l_s[hh:hh + 1, :] = alpha * l_s[hh:hh + 1, :] + jnp.sum(run, axis=0, keepdims=True)
                vt = vt_ref[half * HEAD_DIM:(half + 1) * HEAD_DIM, :]
                pv = _nn(vt, p_s[...])
                rs = slice(half * HEAD_DIM, (half + 1) * HEAD_DIM)
                acc_s[pair, rs, :] = alpha * acc_s[pair, rs, :] + pv

        @pl.when(j == nk - 1)
        def _():
            for pair in range(2):
                for half in range(2):
                    hh = 2 * pair + half
                    rs = slice(half * HEAD_DIM, (half + 1) * HEAD_DIM)
                    acc_s[pair, rs, :] = acc_s[pair, rs, :] * (1.0 / l_s[hh:hh + 1, :])
                o_ref[:, pair * LANES:(pair + 1) * LANES] = acc_s[pair].T.astype(BF16)
            for hh in range(Q_PER_KV):
                lse_ref[0, hh] = m_s[hh:hh + 1, :] + jnp.log2(l_s[hh:hh + 1, :])

    return pl.pallas_call(
        body, name="attn_forward", grid=(N_KV_HEADS, lp // tq, nk),
        in_specs=[pl.BlockSpec((tq, gw), lambda g, i, j: (i, g)), pl.BlockSpec((kb, LANES), lambda g, i, j: (j, g)),
                  pl.BlockSpec((LANES, kb), lambda g, i, j: (g, j))] + pf_specs,
        out_specs=[pl.BlockSpec((tq, gw), lambda g, i, j: (i, g)),
                   pl.BlockSpec((1, Q_PER_KV, 1, tq), lambda g, i, j: (g, 0, 0, i))] + pf_specs,
        out_shape=[jax.ShapeDtypeStruct((lp, N_QB * LANES), BF16),
                   jax.ShapeDtypeStruct((N_KV_HEADS, Q_PER_KV, 1, lp), F32)] + pf_shapes,
        scratch_shapes=[pltpu.VMEM((SUBLANES, tq), F32), pltpu.VMEM((SUBLANES, tq), F32),
                        pltpu.VMEM((2, LANES, tq), F32), pltpu.VMEM((kb, tq), F32), pltpu.VMEM((kb, tq), BF16)] + pf_sems,
        compiler_params=_params(("arbitrary",) * 3 if npf else ("parallel", "parallel", "arbitrary")),
    )(qs, k2, v2t, *prefetch)


def attn_backward(qs, qst, k2, v2, do, dot, lse, delta, n_valid):
    lp = qs.shape[0]
    tq, kb = _attn_tiles_bwd(lp)
    nq, nk = lp // tq, lp // kb
    gw = 2 * LANES
    pad0 = n_valid - (nk - 1) * kb

    def body(q_ref, qt_ref, k_ref, v_ref, do_ref, dot_ref, lse_ref, dl_ref, dq_ref, dk_ref, dv_ref, acc_s, dkt_s, dvt_s):
        g = pl.program_id(0)
        i = pl.program_id(1)
        j = pl.program_id(2)
        cols = pl.ds(pl.multiple_of(j * kb, kb), kb)

        @pl.when(j == 0)
        def _():
            acc_s[...] = jnp.zeros_like(acc_s)

        @pl.when(i == 0)
        def _():
            dkt_s[:, cols] = jnp.zeros((LANES, kb), F32)
            dvt_s[:, cols] = jnp.zeros((LANES, kb), F32)

        head = lax.broadcasted_iota(jnp.int32, (tq, N_Q_HEADS), 1)

        def column(ref, hh):
            return jnp.sum(jnp.where(head == Q_PER_KV * g + hh, ref[...], 0.0), axis=1, keepdims=True)

        def step(masked):
            ks = _half_masks(k_ref[...])
            vs = _half_masks(v_ref[...])
            if masked:
                col = lax.broadcasted_iota(jnp.int32, (1, kb), 1)
                bias = jnp.where(col < pad0, 0.0, NEG)
            for pair in range(2):
                qp = q_ref[:, pair * LANES:(pair + 1) * LANES]
                dop = do_ref[:, pair * LANES:(pair + 1) * LANES]
                for half in range(2):
                    hh = 2 * pair + half
                    rs = slice(half * HEAD_DIM, (half + 1) * HEAD_DIM)
                    rt = slice(pair * LANES + half * HEAD_DIM, pair * LANES + (half + 1) * HEAD_DIM)
                    s = _nt(qp, ks[half])
                    if masked:
                        s = s + bias
                    p = jnp.exp2(s - column(lse_ref, hh))
                    dp = _nt(dop, vs[half])
                    ds = (p * (dp - column(dl_ref, hh))).astype(BF16)
                    pb = p.astype(BF16)
                    acc_s[pair] += _nn(ds, ks[half])
                    dvt_s[rs, cols] += _nn(dot_ref[rt, :], pb)
                    dkt_s[rs, cols] += _nn(qt_ref[rt, :], ds)

        if pad0 < kb:
            pl.when(j < nk - 1)(lambda: step(False))
            pl.when(j == nk - 1)(lambda: step(True))
        else:
            step(False)

        @pl.when(j == nk - 1)
        def _():
            for pair in range(2):
                dq_ref[:, pair * LANES:(pair + 1) * LANES] = acc_s[pair]

        @pl.when(i == nq - 1)
        def _():
            dk_ref[cols, :] = dkt_s[:, cols].T
            dv_ref[cols, :] = dvt_s[:, cols].T

    cspec = pl.BlockSpec((tq, N_Q_HEADS), lambda g, i, j: (i, 0))
    qspec = pl.BlockSpec((tq, gw), lambda g, i, j: (i, g))
    tspec = pl.BlockSpec((gw, tq), lambda g, i, j: (g, i))
    kspec = pl.BlockSpec((kb, LANES), lambda g, i, j: (j, g))
    gspec = pl.BlockSpec((lp, LANES), lambda g, i, j: (0, g))
    return pl.pallas_call(
        body, name="attn_backward", grid=(N_KV_HEADS, nq, nk),
        in_specs=[qspec, tspec, kspec, kspec, qspec, tspec, cspec, cspec],
        out_specs=[qspec, gspec, gspec],
        out_shape=[jax.ShapeDtypeStruct((lp, N_QB * LANES), F32),
                   jax.ShapeDtypeStruct((lp, N_KB * LANES), F32), jax.ShapeDtypeStruct((lp, N_KB * LANES), F32)],
        scratch_shapes=[pltpu.VMEM((2, tq, LANES), F32), pltpu.VMEM((LANES, lp), F32), pltpu.VMEM((LANES, lp), F32)],
        compiler_params=_params(("parallel", "arbitrary", "arbitrary")),
    )(qs, qst, k2, v2, do, dot, lse, delta)


def attn_out_backward(dr, wout, o, e16):
    lp, d = dr.shape
    tm = _tm(lp)

    def body(dr_ref, w_ref, o_ref, e_ref, do_ref, dl_ref, dot_ref):
        do32 = _nt(dr_ref[...].astype(BF16), w_ref[...])
        do = do32.astype(BF16)
        do_ref[...] = do
        for cb in range(d // LANES):
            dot_ref[cb * LANES:(cb + 1) * LANES, :] = do32[:, cb * LANES:(cb + 1) * LANES].T.astype(BF16)
        dl_ref[...] = _segsum(do.astype(F32) * o_ref[...].astype(F32), e_ref[...])

    rs = _row_spec(tm, d)
    return pl.pallas_call(
        body, name="attn_out_backward", grid=(lp // tm,),
        in_specs=[rs, _full_spec((d, d)), rs, _full_spec((d, N_Q_HEADS))],
        out_specs=[rs, _row_spec(tm, N_Q_HEADS), pl.BlockSpec((d, tm), lambda i: (0, i))],
        out_shape=[jax.ShapeDtypeStruct((lp, d), BF16), jax.ShapeDtypeStruct((lp, N_Q_HEADS), F32),
                   jax.ShapeDtypeStruct((d, lp), BF16)],
        compiler_params=_params(("parallel",)),
    )(dr, wout, o, e16)


N_CHIPS = 4


def _mesh_pos():
    return lax.axis_index("x"), lax.axis_index("y"), lax.axis_index("c")


def chip_exchange(arrs, scatter, name):
    n = len(arrs)
    hbm = pl.BlockSpec(memory_space=pl.ANY)

    def body(*refs):
        ins, outs = refs[:n], refs[n:2 * n]
        send_sems, recv_sems, loc_sems = refs[2 * n:]
        x, y, c = _mesh_pos()
        me = 2 * x + y
        chips = [(1 - x, y), (x, 1 - y), (1 - x, 1 - y)]
        started = []
        for a in range(n):
            loc = pltpu.make_async_copy(ins[a].at[me] if scatter else ins[a], outs[a].at[me], loc_sems.at[a])
            loc.start()
            started.append(loc)
            for k, (px, py) in enumerate(chips):
                src = ins[a].at[2 * px + py] if scatter else ins[a]
                cp = pltpu.make_async_remote_copy(
                    src_ref=src, dst_ref=outs[a].at[me], send_sem=send_sems.at[3 * a + k], recv_sem=recv_sems.at[3 * a + k],
                    device_id=(px, py, c), device_id_type=MESH)
                cp.start()
                started.append(cp)
        for cp in started:
            cp.wait()

    out_shape = [jax.ShapeDtypeStruct(a.shape if scatter else (N_CHIPS,) + a.shape, a.dtype) for a in arrs]
    return pl.pallas_call(
        body, name=name, in_specs=[hbm] * n, out_specs=[hbm] * n, out_shape=out_shape,
        scratch_shapes=[pltpu.SemaphoreType.DMA((3 * n,)), pltpu.SemaphoreType.DMA((3 * n,)), pltpu.SemaphoreType.DMA((n,))],
    )(*arrs)


def _same_core_copies(ins, outs, send_sems, recv_sems, loc_sems):
    x, y, c = _mesh_pos()
    me = 2 * x + y
    chips = [(1 - x, y), (x, 1 - y), (1 - x, 1 - y)]
    cps = []
    for a in range(len(ins)):
        cps.append(pltpu.make_async_copy(ins[a], outs[a].at[me], loc_sems.at[a]))
        for k, (px, py) in enumerate(chips):
            cps.append(pltpu.make_async_remote_copy(
                src_ref=ins[a], dst_ref=outs[a].at[me], send_sem=send_sems.at[3 * a + k], recv_sem=recv_sems.at[3 * a + k],
                device_id=(px, py, c), device_id_type=MESH))
    return cps


def _prefetch_specs(arrs):
    n = len(arrs)
    hbm = pl.BlockSpec(memory_space=pl.ANY)
    shapes = [jax.ShapeDtypeStruct((N_CHIPS,) + a.shape, a.dtype) for a in arrs]
    sems = [pltpu.SemaphoreType.DMA((3 * n,)), pltpu.SemaphoreType.DMA((3 * n,)), pltpu.SemaphoreType.DMA((n,))] if n else []
    return [hbm] * n, shapes, sems


def _prefetch_run(first, last, ins, outs, sems):
    if not ins:
        return

    @pl.when(first)
    def _():
        for cp in _same_core_copies(ins, outs, *sems):
            cp.start()

    @pl.when(last)
    def _():
        for cp in _same_core_copies(ins, outs, *sems):
            cp.wait()


def gather_two_level(arrs, name):
    n = len(arrs)
    hbm = pl.BlockSpec(memory_space=pl.ANY)

    def body(*refs):
        ins, outs = refs[:n], refs[n:2 * n]
        ici_send, ici_recv, d2d_send, d2d_recv, loc_sems = refs[2 * n:]
        x, y, c = _mesh_pos()
        me = 2 * x + y
        chips = [(1 - x, y), (x, 1 - y), (1 - x, 1 - y)]
        started = []
        for a in range(n):
            hn = arrs[a].shape[0] // 2
            mine = pl.ds(c * hn, hn)
            loc = pltpu.make_async_copy(ins[a], outs[a].at[me], loc_sems.at[a])
            loc.start()
            started.append(loc)
            first = []
            for k, (px, py) in enumerate(chips):
                cp = pltpu.make_async_remote_copy(
                    src_ref=ins[a].at[mine], dst_ref=outs[a].at[me, mine], send_sem=ici_send.at[3 * a + k],
                    recv_sem=ici_recv.at[3 * a + k], device_id=(px, py, c), device_id_type=MESH)
                cp.start()
                first.append(cp)
            for k, (px, py) in enumerate(chips):
                q = 2 * px + py
                first[k].wait_recv()
                fw = pltpu.make_async_remote_copy(
                    src_ref=outs[a].at[q, mine], dst_ref=outs[a].at[q, mine], send_sem=d2d_send.at[3 * a + k],
                    recv_sem=d2d_recv.at[3 * a + k], device_id=(x, y, 1 - c), device_id_type=MESH)
                fw.start()
                started.append(fw)
            for cp in first:
                cp.wait_send()
        for cp in started:
            cp.wait()

    out_shape = [jax.ShapeDtypeStruct((N_CHIPS,) + a.shape, a.dtype) for a in arrs]
    return pl.pallas_call(
        body, name=name, in_specs=[hbm] * n, out_specs=[hbm] * n, out_shape=out_shape,
        scratch_shapes=[pltpu.SemaphoreType.DMA((3 * n,))] * 4 + [pltpu.SemaphoreType.DMA((n,))],
    )(*arrs)


def sibling_exchange(arrs, name):
    n = len(arrs)
    hbm = pl.BlockSpec(memory_space=pl.ANY)

    def body(*refs):
        ins, outs = refs[:n], refs[n:2 * n]
        send_sems, recv_sems = refs[2 * n:]
        x, y, c = _mesh_pos()
        started = []
        for a in range(n):
            cp = pltpu.make_async_remote_copy(
                src_ref=ins[a], dst_ref=outs[a], send_sem=send_sems.at[a], recv_sem=recv_sems.at[a],
                device_id=(x, y, 1 - c), device_id_type=MESH)
            cp.start()
            started.append(cp)
        for cp in started:
            cp.wait()

    return pl.pallas_call(
        body, name=name, in_specs=[hbm] * n, out_specs=[hbm] * n,
        out_shape=[jax.ShapeDtypeStruct(a.shape, a.dtype) for a in arrs],
        scratch_shapes=[pltpu.SemaphoreType.DMA((n,)), pltpu.SemaphoreType.DMA((n,))],
    )(*arrs)


def _rows_tile(r, c):
    return _pick_tile(r, tuple(t for t in (512, 256, 128, 64, 32, 16, 8) if t * c * 4 <= 2 * 1024 * 1024))


def chip_sum(recv, name):
    _, r, c = recv.shape
    tr = _rows_tile(r, c)

    def body(r_ref, o_ref):
        acc = r_ref[0].astype(F32)
        for q in range(1, N_CHIPS):
            acc = acc + r_ref[q].astype(F32)
        o_ref[...] = acc

    return pl.pallas_call(
        body, name=name, grid=(r // tr,),
        in_specs=[pl.BlockSpec((N_CHIPS, tr, c), lambda i: (0, i, 0))],
        out_specs=pl.BlockSpec((tr, c), lambda i: (i, 0)),
        out_shape=jax.ShapeDtypeStruct((r, c), F32),
        compiler_params=_params(("parallel",)),
    )(recv)


def pair_sum(part, sib, name, dtype=F32):
    r, c = part.shape
    tr = _rows_tile(r, c)

    def body(p_ref, s_ref, o_ref):
        o_ref[...] = (p_ref[...].astype(F32) + s_ref[...].astype(F32)).astype(dtype)

    rs = pl.BlockSpec((tr, c), lambda i: (i, 0))
    return pl.pallas_call(
        body, name=name, grid=(r // tr,), in_specs=[rs] * 2, out_specs=rs,
        out_shape=jax.ShapeDtypeStruct((r, c), dtype), compiler_params=_params(("parallel",)),
    )(part, sib)


def adamw(g, w, m, v, name):
    r, c = w.shape
    tr = _rows_tile(r, c)

    def body(g_ref, w_ref, m_ref, v_ref, d_ref, nm_ref, nv_ref):
        g_ = g_ref[...]
        m_ = ADAM_B1 * m_ref[...] + (1.0 - ADAM_B1) * g_
        v_ = ADAM_B2 * v_ref[...] + (1.0 - ADAM_B2) * (g_ * g_)
        m_hat = m_ / (1.0 - ADAM_B1 ** ADAM_STEP)
        v_hat = v_ / (1.0 - ADAM_B2 ** ADAM_STEP)
        d_ref[...] = -ADAM_LR * (m_hat / (jnp.sqrt(v_hat) + ADAM_EPS) + ADAM_WD * w_ref[...])
        nm_ref[...] = m_
        nv_ref[...] = v_

    rs = pl.BlockSpec((tr, c), lambda i: (i, 0))
    return pl.pallas_call(
        body, name=name, grid=(r // tr,), in_specs=[rs] * 4, out_specs=[rs] * 3,
        out_shape=[jax.ShapeDtypeStruct((r, c), F32)] * 3,
        compiler_params=_params(("parallel",)),
    )(g, w, m, v)


WEIGHTS = ['meta_tokens', 's5_lambda_re', 's5_lambda_im', 's5_log_dt', 's5_b_re', 's5_b_im', 's5_c_re', 's5_c_im', 's5_d',
           's5_w_glu', 's5_w_out', 'attn_w_qkv', 'attn_q_gain', 'attn_k_gain', 'attn_w_out', 'ffn_w_gate', 'ffn_w_up',
           'ffn_w_down', 'ln_gain', 'ln_bias']
BIG = ['s5_w_glu', 's5_w_out', 'attn_w_qkv', 'attn_w_out', 'ffn_w_gate', 'ffn_w_up', 'ffn_w_down']
ROW_SHARDED = {'s5_w_glu', 's5_w_out', 'attn_w_out', 'ffn_w_down'}
SMALL_SHARDED = ['meta_tokens', 'ln_gain', 'ln_bias']
REPLICATED = ['s5_lambda_re', 's5_lambda_im', 's5_log_dt', 's5_b_re', 's5_b_im', 's5_c_re', 's5_c_im', 's5_d',
              'attn_q_gain', 'attn_k_gain']
REP_ALIGN = N_CHIPS * LANES * LANES


def _natural(gathered, row_sharded):
    p, n, a, b = gathered.shape
    if row_sharded:
        return jnp.transpose(gathered, (1, 0, 2, 3)).reshape(n, p * a, b)
    return jnp.transpose(gathered, (1, 2, 0, 3)).reshape(n, a, p * b)


def _shard_major(full, row_sharded):
    n, a, b = full.shape
    if row_sharded:
        return jnp.transpose(full.reshape(n, N_CHIPS, a // N_CHIPS, b), (1, 0, 2, 3))
    return jnp.transpose(full.reshape(n, a, N_CHIPS, b // N_CHIPS), (2, 0, 1, 3))


def _dup_heads(w):
    lead = w.shape[:-1]
    w = w.reshape(lead + (N_KV_HEADS, 1, HEAD_DIM))
    return jnp.broadcast_to(w, lead + (N_KV_HEADS, 2, HEAD_DIM)).reshape(lead + (N_KV_HEADS * 2 * HEAD_DIM,))


def _fold_heads(d):
    lead = d.shape[:-1]
    return d.reshape(lead + (N_KV_HEADS, 2, HEAD_DIM)).sum(axis=-2).reshape(lead + (N_KV_HEADS * HEAD_DIM,))


def _pack_rep(tree):
    flat = jnp.concatenate([tree[n].reshape(-1) for n in REPLICATED])
    pad = _round_up(flat.shape[0], REP_ALIGN) - flat.shape[0]
    return jnp.pad(flat, (0, pad))


def _unpack_rep(flat, like):
    out, off = {}, 0
    for n in REPLICATED:
        size = math.prod(like[n].shape)
        out[n] = flat[off:off + size].reshape(like[n].shape)
        off += size
    return out


def _train_step(x, loss_target, w, mom, vel):
    s = x.shape[1]
    n_valid = N_META + s
    lp = _round_up(n_valid, 2 * LANES)
    nq = N_Q_HEADS * HEAD_DIM
    nkv = N_KV_HEADS * HEAD_DIM

    small = jnp.concatenate([w[n].reshape(-1, w[n].shape[-1]) for n in SMALL_SHARDED], axis=0)
    shard = {n: w[n].astype(BF16) for n in BIG}
    uses = [[('s5_w_glu', 0), ('s5_w_out', 0), ('ffn_w_gate', 0), ('ffn_w_up', 0), ('ffn_w_down', 0)],
            [('attn_w_qkv', 0), ('attn_w_out', 0), ('ffn_w_gate', 1), ('ffn_w_up', 1), ('ffn_w_down', 1)],
            [('s5_w_glu', 1), ('s5_w_out', 1), ('ffn_w_gate', 2), ('ffn_w_up', 2), ('ffn_w_down', 2),
             ('attn_w_qkv', 1), ('attn_w_out', 1), ('ffn_w_gate', 3), ('ffn_w_up', 3), ('ffn_w_down', 3)]]
    full = {}

    def unpack(stage, gathered):
        for (n, l), g in zip(uses[stage], gathered):
            full[(n, l)] = _natural(g[:, None], n in ROW_SHARDED)[0]

    first = gather_two_level([shard[n][l] for n, l in uses[0]] + [small], "gather_weights")
    unpack(0, first[:-1])
    small_full = jnp.transpose(first[-1], (1, 0, 2)).reshape(small.shape[0], D_MODEL)
    meta_full = small_full[:N_META]
    ln_gain = small_full[N_META:N_META + 2 * DEPTH].reshape(DEPTH, 2, 1, D_MODEL)
    ln_bias = small_full[N_META + 2 * DEPTH:].reshape(DEPTH, 2, 1, D_MODEL)

    def qkv_dup(wqkv):
        return jnp.concatenate([wqkv[..., :nq], _dup_heads(wqkv[..., nq:nq + nkv]), _dup_heads(wqkv[..., nq + nkv:])], axis=-1)

    w2 = {}

    cos, sin = rope_tables(lp, n_valid)
    e128 = head_sum_matrix()
    e16 = jnp.kron(jnp.eye(N_Q_HEADS, dtype=F32), jnp.ones((HEAD_DIM, 1), F32)).astype(BF16)
    gq = jnp.tile(w['attn_q_gain'], (1, 2))[:, None, :]
    gk = jnp.tile(w['attn_k_gain'], (1, 2))[:, None, :]

    pad_rows = jnp.zeros((lp - n_valid, D_MODEL), F32)
    h = jnp.concatenate([meta_full, x[0], pad_rows], axis=0)
    tgt = jnp.concatenate([jnp.zeros((N_META, D_MODEL), F32), loss_target[0], pad_rows], axis=0)

    saved = []
    s5_names = ['s5_lambda_re', 's5_lambda_im', 's5_log_dt', 's5_b_re', 's5_b_im', 's5_c_re', 's5_c_im']
    for i in range(DEPTH):
        j = i // 2
        sv = {'h': h}
        if i % 2 == 0:
            ops, sv['prep_vjp'] = jax.vjp(s5_prep, *[w[n][j] for n in s5_names])
            m_, wx_, ci_, at_ = ops
            two = lambda t: t.reshape((2 * S5_NJ,) + t.shape[2:])
            sv['ops'] = (blockdiag_expand(m_, S5_CH, S5_CH, "s5_expand_m"),
                         blockdiag_expand(two(wx_), S5_CH, S5_STATE, "s5_expand_wx").reshape(2, S5_NJ, S5_W, S5_W),
                         blockdiag_expand(two(ci_), S5_STATE, S5_CH, "s5_expand_ci").reshape(2, S5_NJ, S5_W, S5_W), at_)
            pf = [shard[n][l] for n, l in uses[1]] if i == 0 else []
            y, sv['lhs'], sv['sp'], sv['sn'], *got = s5_forward(h, *sv['ops'], n_valid, prefetch=pf)
            if i == 0:
                unpack(1, got)
            sv['v'], sv['t'], sv['g'], sv['z'] = glu_forward(y, h, w['s5_d'][j][None], full['s5_w_glu', j])
            sv['r1'], h1 = proj_ln_forward(sv['z'], full['s5_w_out', j], h, ln_gain[i, 0], ln_bias[i, 0], "s5_out_ln")
        else:
            w2[j] = qkv_dup(full['attn_w_qkv', j])
            sv['raw'], sv['qs'], sv['k2'], sv['v2'], sv['qst'], v2t = qkv_forward(h, w2[j], gq[j], gk[j], cos, sin, e128)
            pf = [shard[n][l] for n, l in uses[2]] if i == 1 else []
            sv['o'], lse, *got = attn_forward_t(sv['qs'], sv['k2'], v2t, n_valid, prefetch=pf)
            if i == 1:
                unpack(2, got)
            sv['lse'] = lse.reshape(N_Q_HEADS, lp).T
            sv['r1'], h1 = proj_ln_forward(sv['o'], full['attn_w_out', j], h, ln_gain[i, 0], ln_bias[i, 0], "attn_out_ln")
        sv['h1'] = h1
        sv['a'], sv['b'], sv['f'] = ffn_up_forward(h1, full['ffn_w_gate', i], full['ffn_w_up', i])
        sv['r2'], h = proj_ln_forward(sv['f'], full['ffn_w_down', i], h1, ln_gain[i, 1], ln_bias[i, 1], "ffn_down_ln")
        saved.append(sv)

    dh, loss_part = loss_backward(h, tgt, n_valid)
    loss = lax.psum(jnp.sum(loss_part), ("x", "y", "c"))

    gfull = {n: [None] * w[n].shape[0] for n in BIG}
    d_ln_gain = [[None, None] for _ in range(DEPTH)]
    d_ln_bias = [[None, None] for _ in range(DEPTH)]
    grep = {n: [None] * w[n].shape[0] for n in REPLICATED}
    for i in reversed(range(DEPTH)):
        j = i // 2
        sv = saved[i]
        dr2, s2 = ln_backward(dh, sv['r2'], ln_gain[i, 1])
        d_ln_gain[i][1], d_ln_bias[i][1] = s2[0], s2[1]
        da, db = ffn_backward_act(dr2, full['ffn_w_down', i], sv['a'], sv['b'])
        gfull['ffn_w_down'][i] = mm_tn(sv['f'], dr2, "grad_ffn_down")
        dh1 = resid_nt(dr2, [da, db], [full['ffn_w_gate', i], full['ffn_w_up', i]], "ffn_backward_x")
        gfull['ffn_w_gate'][i] = mm_tn(sv['h1'], da, "grad_ffn_gate")
        gfull['ffn_w_up'][i] = mm_tn(sv['h1'], db, "grad_ffn_up")
        dr1, s1 = ln_backward(dh1, sv['r1'], ln_gain[i, 0])
        d_ln_gain[i][0], d_ln_bias[i][0] = s1[0], s1[1]
        if i % 2 == 0:
            dt, dgd = glu_backward1(dr1, full['s5_w_out', j], sv['g'], sv['t'])
            gfull['s5_w_out'][j] = mm_tn(sv['z'], dr1, "grad_s5_out")
            dv, dhs, sd = glu_backward2(dt, dgd, full['s5_w_glu', j], sv['v'], sv['h'], w['s5_d'][j][None], dr1)
            grep['s5_d'][j] = sd[0]
            gfull['s5_w_glu'][j] = mm_tn(sv['g'], dt, "grad_s5_glu")
            dh, ldy, dxf, dxr, daf, dar = s5_backward(dv, dhs, *sv['ops'], sv['sp'], sv['sn'], n_valid)
            dm = bmm_tn_compact(sv['lhs'], ldy, S5_CH, S5_CH, "grad_s5_m")
            dwx = jnp.stack([bmm_tn_compact(sv['lhs'], dxf, S5_CH, S5_STATE, "grad_s5_wxf"),
                             bmm_tn_compact(sv['lhs'], dxr, S5_CH, S5_STATE, "grad_s5_wxr")])
            dci = jnp.stack([bmm_tn_compact(sv['sp'], ldy, S5_STATE, S5_CH, "grad_s5_cif"),
                             bmm_tn_compact(sv['sn'], ldy, S5_STATE, S5_CH, "grad_s5_cir")])
            dps = sv['prep_vjp']((dm, dwx, dci, jnp.stack([daf, dar])))
            for n, g in zip(s5_names, dps):
                grep[n][j] = g
        else:
            do, delta, dot = attn_out_backward(dr1, full['attn_w_out', j], sv['o'], e16)
            gfull['attn_w_out'][j] = mm_tn(sv['o'], dr1, "grad_attn_out")
            dq, dk2, dv2 = attn_backward(sv['qs'], sv['qst'], sv['k2'], sv['v2'], do, dot, sv['lse'], delta, n_valid)
            draw, gs = qkv_backward(dq, dk2, dv2, sv['raw'], gq[j], gk[j], cos, sin, e128)
            grep['attn_q_gain'][j] = gs[0, :HEAD_DIM] + gs[0, HEAD_DIM:]
            grep['attn_k_gain'][j] = gs[1, :HEAD_DIM] + gs[1, HEAD_DIM:]
            dh = resid_nt(dr1, [draw], [w2[j]], "attn_backward_x")
            dw2 = mm_tn(sv['h'], draw, "grad_attn_qkv")
            kq = N_QB * LANES
            kk = N_KB * LANES
            gfull['attn_w_qkv'][j] = jnp.concatenate(
                [dw2[:, :kq], _fold_heads(dw2[:, kq:kq + kk]), _fold_heads(dw2[:, kq + kk:])], axis=1)
    grad_x = dh[N_META:n_valid][None]

    core = lax.axis_index("c")
    contrib = [_shard_major(jnp.stack(gfull[n]), n in ROW_SHARDED) for n in BIG]
    small_g = jnp.concatenate([dh[:N_META], jnp.stack([g for pair in d_ln_gain for g in pair]),
                               jnp.stack([g for pair in d_ln_bias for g in pair])], axis=0)
    contrib.append(jnp.transpose(small_g.reshape(-1, N_CHIPS, D_MODEL // N_CHIPS), (1, 0, 2)))
    rep_g = _pack_rep({n: jnp.stack(grep[n]) for n in REPLICATED})
    contrib.append(rep_g.reshape(N_CHIPS, -1, LANES))
    names = BIG + ['small', 'rep']
    wire = [BF16] * len(BIG) + [F32, F32]
    keep, give = [], []
    for t, dt in zip(contrib, wire):
        hn = t.shape[1] // 2
        keep.append(lax.dynamic_slice_in_dim(t, core * hn, hn, axis=1))
        give.append(lax.dynamic_slice_in_dim(t, (1 - core) * hn, hn, axis=1).astype(dt))
    got = sibling_exchange(give, "sibling_contrib")
    two_d = lambda t: t.reshape(-1, t.shape[-1])
    pair = [pair_sum(two_d(a), two_d(b), "pair_sum_" + n, dt).reshape(a.shape)
            for n, a, b, dt in zip(names, keep, got, wire)]
    recv = chip_exchange(pair, True, "scatter_grads")
    halves = [chip_sum(r.reshape(N_CHIPS, -1, r.shape[-1]), "chip_sum_" + n) for n, r in zip(names, recv)]
    others = sibling_exchange(halves, "sibling_halves")
    grads = [jnp.where(core == 0, jnp.concatenate([a, b], axis=0), jnp.concatenate([b, a], axis=0))
             for a, b in zip(halves, others)]

    out = {}

    def update(n, g, wn, mn, vn):
        shape = wn.shape
        flat = (-1, shape[-1])
        d, nm, nv = adamw(g, wn.reshape(flat), mn.reshape(flat), vn.reshape(flat), "adamw_" + n)
        return tuple(t.reshape(shape) for t in (g, d, nm, nv))

    for n, g in zip(BIG, grads):
        out[n] = update(n, g, w[n], mom[n], vel[n])
    cat = lambda tree: jnp.concatenate([tree[n].reshape(-1, tree[n].shape[-1]) for n in SMALL_SHARDED], axis=0)
    sm = update("small", grads[-2], cat(w), cat(mom), cat(vel))
    off = 0
    for n in SMALL_SHARDED:
        rows = math.prod(w[n].shape[:-1])
        out[n] = tuple(t[off:off + rows].reshape(w[n].shape) for t in sm)
        off += rows
    rep_all = chip_exchange([grads[-1]], False, "gather_rep")[0].reshape(-1, LANES)
    rp = update("rep", rep_all, _pack_rep(w).reshape(-1, LANES), _pack_rep(mom).reshape(-1, LANES),
                _pack_rep(vel).reshape(-1, LANES))
    unpacked = [_unpack_rep(t.reshape(-1), w) for t in rp]
    for n in REPLICATED:
        out[n] = tuple(u[n] for u in unpacked)

    return (loss, grad_x, *[out[n][0] for n in WEIGHTS], *[out[n][1] for n in WEIGHTS],
            *[out[n][2] for n in WEIGHTS], *[out[n][3] for n in WEIGHTS])


def kernel(x, meta_tokens, s5_lambda_re, s5_lambda_im, s5_log_dt, s5_b_re, s5_b_im, s5_c_re, s5_c_im, s5_d, s5_w_glu, s5_w_out, attn_w_qkv, attn_q_gain, attn_k_gain, attn_w_out, ffn_w_gate, ffn_w_up, ffn_w_down, ln_gain, ln_bias, loss_target, m_meta_tokens, m_s5_lambda_re, m_s5_lambda_im, m_s5_log_dt, m_s5_b_re, m_s5_b_im, m_s5_c_re, m_s5_c_im, m_s5_d, m_s5_w_glu, m_s5_w_out, m_attn_w_qkv, m_attn_q_gain, m_attn_k_gain, m_attn_w_out, m_ffn_w_gate, m_ffn_w_up, m_ffn_w_down, m_ln_gain, m_ln_bias, v_meta_tokens, v_s5_lambda_re, v_s5_lambda_im, v_s5_log_dt, v_s5_b_re, v_s5_b_im, v_s5_c_re, v_s5_c_im, v_s5_d, v_s5_w_glu, v_s5_w_out, v_attn_w_qkv, v_attn_q_gain, v_attn_k_gain, v_attn_w_out, v_ffn_w_gate, v_ffn_w_up, v_ffn_w_down, v_ln_gain, v_ln_bias):
    given = locals()
    w = {n: given[n] for n in WEIGHTS}
    mom = {n: given["m_" + n] for n in WEIGHTS}
    vel = {n: given["v_" + n] for n in WEIGHTS}
    return _train_step(x, loss_target, w, mom, vel)
```

```python
import math

import jax
import jax.numpy as jnp
from jax import lax
from jax.experimental import pallas as pl
from jax.experimental.pallas import tpu as pltpu

F32 = jnp.float32
BF16 = jnp.bfloat16
MESH = pl.DeviceIdType.MESH

D_MODEL = 1024
N_META = 16
GRID_W = 64
HEAD_DIM = 64
N_Q_HEADS = 16
N_KV_HEADS = 4
ROPE_THETA = 10000.0
QK_EPS = 1e-6
S5_CH = 16
S5_GROUPS = 64
S5_STATE = 64
D_FF = 2816
LN_EPS = 1e-5
DEPTH = 4
ALPHA = (2.0 * DEPTH) ** 0.25
ADAM_LR, ADAM_B1, ADAM_B2, ADAM_EPS, ADAM_WD, ADAM_STEP = 0.001, 0.9, 0.999, 1e-08, 0.01, 10

LANES = 128
SUBLANES = 8
VMEM_LIMIT = 56 * 1024 * 1024

S5_T = 8
S5_GB = LANES // S5_CH
S5_NJ = S5_GROUPS // S5_GB
S5_W = S5_T * LANES
S5_SW = 2 * S5_GB * S5_STATE
S5_HALF = S5_SW // 2


def _round_up(a, b):
    return -(-a // b) * b


def _pick_tile(n, prefs):
    for t in prefs:
        if n % t == 0:
            return t
    return n


def _params(sem=None):
    kw = dict(vmem_limit_bytes=VMEM_LIMIT)
    if sem is not None:
        kw["dimension_semantics"] = sem
    return pltpu.CompilerParams(**kw)


def _dot(a, b, dims):
    return lax.dot_general(a, b, (dims, ((), ())), preferred_element_type=F32)


def _nn(a, b):
    return _dot(a, b, ((1,), (0,)))


def _nt(a, b):
    return _dot(a, b, ((1,), (1,)))


def _tn(a, b):
    return _dot(a, b, ((0,), (0,)))


def _compact(w):
    g, a0, a1, b0, b1 = w.shape
    w = jnp.transpose(w.reshape(S5_NJ, S5_GB, a0, a1, b0, b1), (0, 2, 1, 3, 4, 5))
    return w.reshape(S5_NJ, a0 * S5_GB * a1, b0 * b1)


def s5_prep(lam_re, lam_im, log_dt, b_re, b_im, c_re, c_im):
    hi = lax.Precision.HIGHEST
    t = S5_T
    dt = jnp.exp(log_dt)[..., None]
    taus = jnp.arange(t + 1, dtype=F32)[:, None, None, None]
    mag = jnp.exp(lam_re * dt)
    ang = lam_im * dt
    pr = jnp.concatenate([jnp.ones_like(mag)[None], (mag * jnp.cos(ang))[None],
                          jnp.exp(lam_re * dt * taus[2:]) * jnp.cos(ang * taus[2:])], axis=0)
    pi = jnp.concatenate([jnp.zeros_like(mag)[None], (mag * jnp.sin(ang))[None],
                          jnp.exp(lam_re * dt * taus[2:]) * jnp.sin(ang * taus[2:])], axis=0)
    abr, abi = pr[1], pi[1]
    nr, ni = abr - 1.0, abi
    den = lam_re * lam_re + lam_im * lam_im
    cr = (nr * lam_re + ni * lam_im) / den
    ci_ = (ni * lam_re - nr * lam_im) / den
    bbr = cr[..., None] * b_re - ci_[..., None] * b_im
    bbi = cr[..., None] * b_im + ci_[..., None] * b_re
    er = c_re[None] * pr[:, :, :, None, :] - c_im[None] * pi[:, :, :, None, :]
    ei = c_re[None] * pi[:, :, :, None, :] + c_im[None] * pr[:, :, :, None, :]
    nd, ng, ch = er.shape[1], er.shape[2], er.shape[3]
    lhs = jnp.concatenate([er[:t], -ei[:t]], axis=-1)
    lhs = jnp.transpose(lhs, (1, 2, 0, 3, 4)).reshape(nd, ng, t * ch, 2 * S5_STATE)
    rhs = jnp.concatenate([bbr, bbi], axis=-2)
    kk = jnp.einsum("dgmp,dgpc->dgmc", lhs, rhs, precision=hi)
    kk = jnp.transpose(kk.reshape(nd, ng, t, ch, ch), (2, 0, 1, 3, 4))
    zero = jnp.zeros_like(kk[0, 0])
    mg = jnp.stack([jnp.stack([(kk[i - s, 0] if i > s else zero) + (kk[s - i, 1] if s > i else zero)
                               + ((kk[0, 0] + kk[0, 1]) if i == s else zero) for i in range(t)])
                    for s in range(t)])
    mg = jnp.transpose(mg, (2, 0, 4, 1, 3))
    m = _compact(mg)
    pw_f = jnp.stack([pr[t - 1 - s, 0] for s in range(t)]), jnp.stack([pi[t - 1 - s, 0] for s in range(t)])
    pw_r = jnp.stack([pr[s, 1] for s in range(t)]), jnp.stack([pi[s, 1] for s in range(t)])
    wx = []
    for d, (qr, qi) in enumerate((pw_f, pw_r)):
        wr = qr[..., None] * bbr[d][None] - qi[..., None] * bbi[d][None]
        wi = qr[..., None] * bbi[d][None] + qi[..., None] * bbr[d][None]
        w = jnp.stack([wr, wi], axis=0)
        w = jnp.transpose(w, (2, 1, 4, 0, 3))
        wx.append(_compact(w))
    ci = []
    for d in range(2):
        exps = [i + 1 for i in range(t)] if d == 0 else [t - i for i in range(t)]
        e_r = jnp.stack([er[e, d] for e in exps])
        e_i = jnp.stack([ei[e, d] for e in exps])
        w = jnp.stack([e_r, -e_i], axis=0)
        w = jnp.transpose(w, (2, 0, 4, 1, 3))
        ci.append(_compact(w))
    at = jnp.stack([pr[t], pi[t]], axis=1)
    at = at.reshape(2, 2, S5_NJ, S5_GB * S5_STATE)
    at = jnp.transpose(at, (0, 2, 1, 3)).reshape(2, S5_NJ, 1, S5_SW)
    return m, jnp.stack(wx), jnp.stack(ci), at


def _chunk_rows(ref, nc):
    return jnp.concatenate([ref[pl.ds(s, nc, stride=S5_T), :] for s in range(S5_T)], axis=1)


def _cmul(ar, ai, sr, si):
    return ar * sr - ai * si, ar * si + ai * sr


def _scan_tiles(nc, reverse, step):
    nt = nc // SUBLANES

    def body(it, carry):
        tix = (nt - 1 - it) if reverse else it
        k0 = pl.multiple_of(tix * SUBLANES, SUBLANES)
        return step(k0, carry)

    return body, nt


def _s5_specs(nc):
    hbm = pl.BlockSpec(memory_space=pl.ANY)
    aspec = pl.BlockSpec((1, 1, S5_SW), lambda j: (j, 0, 0))
    cspec = pl.BlockSpec((1, nc, S5_W), lambda j: (j, 0, 0))
    return hbm, aspec, cspec


def _s5_fetch(j, tok_hbm, w_hbms, tok_s, w_s, sems):
    cols = pl.ds(pl.multiple_of(j * LANES, LANES), LANES)
    cps = [pltpu.make_async_copy(tok_hbm.at[:, cols], tok_s, sems.at[0])]
    for i, w in enumerate(w_hbms):
        cps.append(pltpu.make_async_copy(w.at[j], w_s.at[i], sems.at[1 + i]))
    for cp in cps:
        cp.start()
    return cols, cps


def s5_forward(u, m, wx, ci, at, n_valid, prefetch=()):
    lp = u.shape[0]
    nc = lp // S5_T
    nvc = n_valid // S5_T
    npf = len(prefetch)
    pf_specs, pf_shapes, pf_sems = _prefetch_specs(prefetch)

    def body(*refs):
        u_hbm, m_hbm, wxf_hbm, wxr_hbm, cif_hbm, cir_hbm, atf_ref, atr_ref = refs[:8]
        pf_ins = refs[8:8 + npf]
        y_hbm, lhs_ref, sp_ref, sn_ref = refs[8 + npf:12 + npf]
        pf_outs = refs[12 + npf:12 + 2 * npf]
        tok_s, w_s, xf_s, xr_s, sems = refs[12 + 2 * npf:17 + 2 * npf]
        j = pl.program_id(0)
        _prefetch_run(j == 0, j == S5_NJ - 1, pf_ins, pf_outs, refs[17 + 2 * npf:])
        cols, cps = _s5_fetch(j, u_hbm, (m_hbm, wxf_hbm, wxr_hbm, cif_hbm, cir_hbm), tok_s, w_s, sems)
        cps[0].wait()
        lhs = _chunk_rows(tok_s, nc)
        rows = lax.broadcasted_iota(jnp.int32, lhs.shape, 0)
        lhs = jnp.where(rows < nvc, lhs, 0.0).astype(BF16)
        lhs_ref[0] = lhs
        cps[2].wait()
        cps[3].wait()
        xf_s[...] = _nn(lhs, w_s[1])
        xr_s[...] = _nn(lhs, w_s[2])
        afr, afi = atf_ref[0, :, :S5_HALF], atf_ref[0, :, S5_HALF:]
        arr, ari = atr_ref[0, :, :S5_HALF], atr_ref[0, :, S5_HALF:]

        def scan_step(x_s, ar, ai, descending):
            def step(k0, carry):
                sr, si = carry
                x = x_s[pl.ds(k0, SUBLANES), :]
                outs = [None] * SUBLANES
                order = reversed(range(SUBLANES)) if descending else range(SUBLANES)
                for r in order:
                    outs[r] = jnp.concatenate([sr, si], axis=1)
                    nr, ni = _cmul(ar, ai, sr, si)
                    sr = nr + x[r:r + 1, :S5_HALF]
                    si = ni + x[r:r + 1, S5_HALF:]
                x_s[pl.ds(k0, SUBLANES), :] = jnp.concatenate(outs, axis=0)
                return sr, si
            return step

        zero = jnp.zeros((1, S5_HALF), F32)
        fb, nt = _scan_tiles(nc, False, scan_step(xf_s, afr, afi, False))
        lax.fori_loop(0, nt, fb, (zero, zero))
        rb, nt = _scan_tiles(nc, True, scan_step(xr_s, arr, ari, True))
        lax.fori_loop(0, nt, rb, (zero, zero))
        sp = xf_s[...].astype(BF16)
        sn = xr_s[...].astype(BF16)
        sp_ref[0] = sp
        sn_ref[0] = sn
        cps[1].wait()
        cps[4].wait()
        cps[5].wait()
        y = _nn(lhs, w_s[0]) + _nn(sp, w_s[3]) + _nn(sn, w_s[4])
        for i in range(S5_T):
            tok_s[pl.ds(i, nc, stride=S5_T), :] = y[:, i * LANES:(i + 1) * LANES]
        pltpu.sync_copy(tok_s, y_hbm.at[:, cols])

    hbm, aspec, cspec = _s5_specs(nc)
    return pl.pallas_call(
        body, name="s5_forward", grid=(S5_NJ,),
        in_specs=[hbm] * 6 + [aspec, aspec] + pf_specs,
        out_specs=[hbm, cspec, cspec, cspec] + pf_specs,
        out_shape=[jax.ShapeDtypeStruct((lp, D_MODEL), F32)] + [jax.ShapeDtypeStruct((S5_NJ, nc, S5_W), BF16)] * 3 + pf_shapes,
        scratch_shapes=[pltpu.VMEM((lp, LANES), F32), pltpu.VMEM((5, S5_W, S5_W), BF16),
                        pltpu.VMEM((nc, S5_SW), F32), pltpu.VMEM((nc, S5_SW), F32), pltpu.SemaphoreType.DMA((6,))] + pf_sems,
        compiler_params=_params(("arbitrary",)),
    )(u, m, wx[0], wx[1], ci[0], ci[1], at[0], at[1], *prefetch)


def s5_backward(dy, dhs, m, wx, ci, at, sp, sn, n_valid):
    lp = dy.shape[0]
    nc = lp // S5_T
    nvc = n_valid // S5_T

    def body(dy_hbm, dhs_hbm, m_hbm, wxf_hbm, wxr_hbm, cif_hbm, cir_hbm, atf_ref, atr_ref, sp_ref, sn_ref,
             dh_hbm, ldy_ref, dxf_ref, dxr_ref, daf_ref, dar_ref, tok_s, w_s, gf_s, gr_s, sems):
        j = pl.program_id(0)
        cols, cps = _s5_fetch(j, dy_hbm, (m_hbm, wxf_hbm, wxr_hbm, cif_hbm, cir_hbm), tok_s, w_s, sems)
        cps[0].wait()
        ldy = _chunk_rows(tok_s, nc)
        rows = lax.broadcasted_iota(jnp.int32, ldy.shape, 0)
        ldy = jnp.where(rows < nvc, ldy, 0.0).astype(BF16)
        ldy_ref[0] = ldy
        resid = pltpu.make_async_copy(dhs_hbm.at[:, cols], tok_s, sems.at[0])
        resid.start()
        cps[4].wait()
        cps[5].wait()
        gf_s[...] = _nt(ldy, w_s[3])
        gr_s[...] = _nt(ldy, w_s[4])
        afr, afi = atf_ref[0, :, :S5_HALF], atf_ref[0, :, S5_HALF:]
        arr, ari = atr_ref[0, :, :S5_HALF], atr_ref[0, :, S5_HALF:]

        def adj_step(g_s, s_ref, ar, ai, descending):
            def step(k0, carry):
                gr_, gi_, dr_, di_ = carry
                g = g_s[pl.ds(k0, SUBLANES), :]
                p = s_ref[0, pl.ds(k0, SUBLANES), :].astype(F32)
                outs = [None] * SUBLANES
                order = reversed(range(SUBLANES)) if descending else range(SUBLANES)
                for r in order:
                    outs[r] = jnp.concatenate([gr_, gi_], axis=1)
                    pr_, pi_ = p[r:r + 1, :S5_HALF], p[r:r + 1, S5_HALF:]
                    dr_ = dr_ + gr_ * pr_ + gi_ * pi_
                    di_ = di_ + gi_ * pr_ - gr_ * pi_
                    nr, ni = _cmul(ar, -ai, gr_, gi_)
                    gr_ = nr + g[r:r + 1, :S5_HALF]
                    gi_ = ni + g[r:r + 1, S5_HALF:]
                g_s[pl.ds(k0, SUBLANES), :] = jnp.concatenate(outs, axis=0)
                return gr_, gi_, dr_, di_
            return step

        zero = jnp.zeros((1, S5_HALF), F32)
        fb, nt = _scan_tiles(nc, True, adj_step(gf_s, sp_ref, afr, afi, True))
        _, _, dr_, di_ = lax.fori_loop(0, nt, fb, (zero,) * 4)
        daf_ref[0] = jnp.concatenate([dr_, di_], axis=1)
        rb, nt = _scan_tiles(nc, False, adj_step(gr_s, sn_ref, arr, ari, False))
        _, _, dr_, di_ = lax.fori_loop(0, nt, rb, (zero,) * 4)
        dar_ref[0] = jnp.concatenate([dr_, di_], axis=1)
        dxf = gf_s[...].astype(BF16)
        dxr = gr_s[...].astype(BF16)
        dxf_ref[0] = dxf
        dxr_ref[0] = dxr
        cps[1].wait()
        cps[2].wait()
        cps[3].wait()
        du = _nt(ldy, w_s[0]) + _nt(dxf, w_s[1]) + _nt(dxr, w_s[2])
        rows = lax.broadcasted_iota(jnp.int32, du.shape, 0)
        du = jnp.where(rows < nvc, du, 0.0)
        resid.wait()
        for s in range(S5_T):
            tok_s[pl.ds(s, nc, stride=S5_T), :] += du[:, s * LANES:(s + 1) * LANES]
        pltpu.sync_copy(tok_s, dh_hbm.at[:, cols])

    hbm, aspec, cspec = _s5_specs(nc)
    return pl.pallas_call(
        body, name="s5_backward", grid=(S5_NJ,),
        in_specs=[hbm] * 7 + [aspec, aspec, cspec, cspec],
        out_specs=[hbm, cspec, cspec, cspec, aspec, aspec],
        out_shape=[jax.ShapeDtypeStruct((lp, D_MODEL), F32)] + [jax.ShapeDtypeStruct((S5_NJ, nc, S5_W), BF16)] * 3
        + [jax.ShapeDtypeStruct((S5_NJ, 1, S5_SW), F32)] * 2,
        scratch_shapes=[pltpu.VMEM((lp, LANES), F32), pltpu.VMEM((5, S5_W, S5_W), BF16),
                        pltpu.VMEM((nc, S5_SW), F32), pltpu.VMEM((nc, S5_SW), F32), pltpu.SemaphoreType.DMA((6,))],
        compiler_params=_params(("arbitrary",)),
    )(dy, dhs, m, wx[0], wx[1], ci[0], ci[1], at[0], at[1], sp, sn)


S5_CW = LANES


def _replicate_matrix(b1):
    b0n = S5_CW // b1
    eye0 = jnp.eye(b0n, dtype=F32)
    eye1 = jnp.eye(b1, dtype=F32)
    r = jnp.einsum("ab,cd->acbd", eye0, eye1)[:, :, :, None, :]
    r = jnp.broadcast_to(r, (b0n, b1, b0n, S5_GB, b1))
    return r.reshape(S5_CW, b0n * S5_GB * b1).astype(BF16)


def _same_group(a1, b1):
    rg = (lax.broadcasted_iota(jnp.int32, (S5_W, S5_W), 0) // a1) % S5_GB
    cg = (lax.broadcasted_iota(jnp.int32, (S5_W, S5_W), 1) // b1) % S5_GB
    return rg == cg


def blockdiag_expand(compact, a1, b1, name):
    nj = compact.shape[0]

    def body(c_ref, r_ref, o_ref):
        rep = _nn(c_ref[0].astype(BF16), r_ref[...])
        o_ref[0] = jnp.where(_same_group(a1, b1), rep, 0.0).astype(BF16)

    return pl.pallas_call(
        body, name=name, grid=(nj,),
        in_specs=[pl.BlockSpec((1, S5_W, S5_CW), lambda j: (j, 0, 0)), _full_spec((S5_CW, S5_W))],
        out_specs=pl.BlockSpec((1, S5_W, S5_W), lambda j: (j, 0, 0)),
        out_shape=jax.ShapeDtypeStruct((nj, S5_W, S5_W), BF16),
        compiler_params=_params(("parallel",)),
    )(compact, _replicate_matrix(b1))


def bmm_tn_compact(a, b, a1, b1, name):
    nj, k, wa = a.shape
    wb = b.shape[2]

    def body(a_ref, b_ref, r_ref, o_ref):
        prod = jnp.where(_same_group(a1, b1), _tn(a_ref[0], b_ref[0]), 0.0)
        hi = prod.astype(BF16)
        lo = (prod - hi.astype(F32)).astype(BF16)
        o_ref[0] = _nt(hi, r_ref[...]) + _nt(lo, r_ref[...])

    return pl.pallas_call(
        body, name=name, grid=(nj,),
        in_specs=[pl.BlockSpec((1, k, wa), lambda j: (j, 0, 0)), pl.BlockSpec((1, k, wb), lambda j: (j, 0, 0)),
                  _full_spec((S5_CW, S5_W))],
        out_specs=pl.BlockSpec((1, wa, S5_CW), lambda j: (j, 0, 0)),
        out_shape=jax.ShapeDtypeStruct((nj, wa, S5_CW), F32),
        compiler_params=_params(("parallel",)),
    )(a, b, _replicate_matrix(b1))


def _tm(lp):
    return _pick_tile(lp, (768, 256))


def _row_spec(tm, width):
    return pl.BlockSpec((tm, width), lambda i: (i, 0))


def _full_spec(shape):
    return pl.BlockSpec(shape, lambda *_: (0,) * len(shape))


def _gelu(v):
    return 0.5 * v * (1.0 + lax.erf(v * (2.0 ** -0.5)))


def _gelu_grad(v):
    return 0.5 * (1.0 + lax.erf(v * (2.0 ** -0.5))) + v * jnp.exp(-0.5 * v * v) * (2.0 * math.pi) ** -0.5


def _layer_norm(r, gain, bias):
    mean = jnp.mean(r, axis=-1, keepdims=True)
    c = r - mean
    var = jnp.mean(c * c, axis=-1, keepdims=True)
    return c * lax.rsqrt(var + LN_EPS) * gain + bias


def glu_forward(y, h, dvec, wglu):
    lp, d = y.shape
    tm = _tm(lp)

    def body(y_ref, h_ref, d_ref, w_ref, v_ref, t_ref, g_ref, z_ref):
        v = y_ref[...] + d_ref[...] * h_ref[...]
        g = _gelu(v)
        gb = g.astype(BF16)
        t = _nn(gb, w_ref[...])
        v_ref[...] = v
        t_ref[...] = t
        g_ref[...] = gb
        z_ref[...] = (g * jax.nn.sigmoid(t)).astype(BF16)

    rs = _row_spec(tm, d)
    return pl.pallas_call(
        body, name="glu_forward", grid=(lp // tm,),
        in_specs=[rs, rs, _full_spec((1, d)), _full_spec((d, d))],
        out_specs=[rs, rs, rs, rs],
        out_shape=[jax.ShapeDtypeStruct((lp, d), F32)] * 2 + [jax.ShapeDtypeStruct((lp, d), BF16)] * 2,
        compiler_params=_params(("parallel",)),
    )(y, h, dvec, wglu)


def proj_ln_forward(z, w, h, gain, bias, name):
    lp, k = z.shape
    d = w.shape[1]
    tm = _tm(lp)

    def body(z_ref, w_ref, h_ref, g_ref, b_ref, r_ref, o_ref):
        r = ALPHA * h_ref[...] + _nn(z_ref[...], w_ref[...])
        r_ref[...] = r
        o_ref[...] = _layer_norm(r, g_ref[...], b_ref[...])

    rs = _row_spec(tm, d)
    return pl.pallas_call(
        body, name=name, grid=(lp // tm,),
        in_specs=[_row_spec(tm, k), _full_spec((k, d)), rs, _full_spec((1, d)), _full_spec((1, d))],
        out_specs=[rs, rs],
        out_shape=[jax.ShapeDtypeStruct((lp, d), F32)] * 2,
        compiler_params=_params(("parallel",)),
    )(z, w, h, gain, bias)


FFN_TM = 384


def ffn_up_forward(h, wg, wu):
    lp, d = h.shape
    dff = wg.shape[1]
    tm = _pick_tile(lp, (FFN_TM, 256))

    def body(h_ref, wg_ref, wu_ref, a_ref, b_ref, f_ref):
        hb = h_ref[...].astype(BF16)
        a = _nn(hb, wg_ref[...])
        b = _nn(hb, wu_ref[...])
        a_ref[...] = a.astype(BF16)
        b_ref[...] = b.astype(BF16)
        f_ref[...] = (a * jax.nn.sigmoid(a) * b).astype(BF16)

    ws = _full_spec((d, dff))
    os_ = _row_spec(tm, dff)
    return pl.pallas_call(
        body, name="ffn_up_forward", grid=(lp // tm,),
        in_specs=[_row_spec(tm, d), ws, ws],
        out_specs=[os_, os_, os_],
        out_shape=[jax.ShapeDtypeStruct((lp, dff), BF16)] * 3,
        compiler_params=_params(("parallel",)),
    )(h, wg, wu)


def ln_backward(dh, r, gain):
    lp, d = dh.shape
    tm = _tm(lp)

    def body(dh_ref, r_ref, g_ref, dr_ref, s_ref):
        r_ = r_ref[...]
        dh_ = dh_ref[...]
        mean = jnp.mean(r_, axis=-1, keepdims=True)
        c = r_ - mean
        var = jnp.mean(c * c, axis=-1, keepdims=True)
        rstd = lax.rsqrt(var + LN_EPS)
        xh = c * rstd
        dxh = dh_ * g_ref[...]
        m1 = jnp.mean(dxh, axis=-1, keepdims=True)
        m2 = jnp.mean(dxh * xh, axis=-1, keepdims=True)
        dr_ref[...] = rstd * (dxh - m1 - xh * m2)

        @pl.when(pl.program_id(0) == 0)
        def _():
            s_ref[...] = jnp.zeros_like(s_ref)

        s_ref[0:1, :] += jnp.sum(dh_ * xh, axis=0, keepdims=True)
        s_ref[1:2, :] += jnp.sum(dh_, axis=0, keepdims=True)

    rs = _row_spec(tm, d)
    return pl.pallas_call(
        body, name="ln_backward", grid=(lp // tm,),
        in_specs=[rs, rs, _full_spec((1, d))],
        out_specs=[rs, _full_spec((SUBLANES, d))],
        out_shape=[jax.ShapeDtypeStruct((lp, d), F32), jax.ShapeDtypeStruct((SUBLANES, d), F32)],
        compiler_params=_params(("arbitrary",)),
    )(dh, r, gain)


def ffn_backward_act(dr, wd, a, b):
    lp, d = dr.shape
    dff = wd.shape[0]
    tm = _pick_tile(lp, (FFN_TM, 256))

    def body(dr_ref, wd_ref, a_ref, b_ref, da_ref, db_ref):
        df = _nt(dr_ref[...].astype(BF16), wd_ref[...])
        a_ = a_ref[...].astype(F32)
        b_ = b_ref[...].astype(F32)
        sg = jax.nn.sigmoid(a_)
        da_ref[...] = (df * b_ * sg * (1.0 + a_ * (1.0 - sg))).astype(BF16)
        db_ref[...] = (df * a_ * sg).astype(BF16)

    os_ = _row_spec(tm, dff)
    return pl.pallas_call(
        body, name="ffn_backward_act", grid=(lp // tm,),
        in_specs=[_row_spec(tm, d), _full_spec((dff, d)), os_, os_],
        out_specs=[os_, os_],
        out_shape=[jax.ShapeDtypeStruct((lp, dff), BF16)] * 2,
        compiler_params=_params(("parallel",)),
    )(dr, wd, a, b)


def resid_nt(dr, xs, ws, name):
    lp, d = dr.shape
    tm = _tm(lp)
    n = len(xs)

    def body(*refs):
        acc = ALPHA * refs[0][...]
        for i in range(n):
            acc = acc + _nt(refs[1 + i][...], refs[1 + n + i][...])
        refs[-1][...] = acc

    rs = _row_spec(tm, d)
    in_specs = [rs] + [_row_spec(tm, x.shape[1]) for x in xs] + [_full_spec(w.shape) for w in ws]
    return pl.pallas_call(
        body, name=name, grid=(lp // tm,),
        in_specs=in_specs, out_specs=rs,
        out_shape=jax.ShapeDtypeStruct((lp, d), F32),
        compiler_params=_params(("parallel",)),
    )(dr, *xs, *ws)


def mm_tn(x, y, name):
    lp, k = x.shape
    n = y.shape[1]
    tm = _tm(lp)
    nb = _pick_tile(n, (512, 2816))

    def body(x_ref, y_ref, o_ref):
        @pl.when(pl.program_id(1) == 0)
        def _():
            o_ref[...] = jnp.zeros_like(o_ref)

        o_ref[...] += _tn(x_ref[...].astype(BF16), y_ref[...].astype(BF16))

    return pl.pallas_call(
        body, name=name, grid=(n // nb, lp // tm),
        in_specs=[pl.BlockSpec((tm, k), lambda j, i: (i, 0)), pl.BlockSpec((tm, nb), lambda j, i: (i, j))],
        out_specs=pl.BlockSpec((k, nb), lambda j, i: (0, j)),
        out_shape=jax.ShapeDtypeStruct((k, n), F32),
        compiler_params=_params(("parallel", "arbitrary")),
    )(x, y)


def glu_backward1(dr, wout, g, t):
    lp, d = dr.shape
    tm = _tm(lp)

    def body(dr_ref, w_ref, g_ref, t_ref, dt_ref, dgd_ref):
        dz = _nt(dr_ref[...].astype(BF16), w_ref[...])
        s = jax.nn.sigmoid(t_ref[...])
        dgd_ref[...] = dz * s
        dt_ref[...] = (dz * g_ref[...].astype(F32) * s * (1.0 - s)).astype(BF16)

    rs = _row_spec(tm, d)
    return pl.pallas_call(
        body, name="glu_backward1", grid=(lp // tm,),
        in_specs=[rs, _full_spec((d, d)), rs, rs],
        out_specs=[rs, rs],
        out_shape=[jax.ShapeDtypeStruct((lp, d), BF16), jax.ShapeDtypeStruct((lp, d), F32)],
        compiler_params=_params(("parallel",)),
    )(dr, wout, g, t)


def glu_backward2(dt, dgd, wglu, v, h, dvec, dr):
    lp, d = dt.shape
    tm = _tm(lp)

    def body(dt_ref, dgd_ref, w_ref, v_ref, h_ref, d_ref, dr_ref, dv_ref, dhs_ref, s_ref):
        dg = dgd_ref[...] + _nt(dt_ref[...], w_ref[...])
        dv = dg * _gelu_grad(v_ref[...])
        dv_ref[...] = dv
        dhs_ref[...] = ALPHA * dr_ref[...] + dv * d_ref[...]

        @pl.when(pl.program_id(0) == 0)
        def _():
            s_ref[...] = jnp.zeros_like(s_ref)

        s_ref[0:1, :] += jnp.sum(dv * h_ref[...], axis=0, keepdims=True)

    rs = _row_spec(tm, d)
    return pl.pallas_call(
        body, name="glu_backward2", grid=(lp // tm,),
        in_specs=[rs, rs, _full_spec((d, d)), rs, rs, _full_spec((1, d)), rs],
        out_specs=[rs, rs, _full_spec((SUBLANES, d))],
        out_shape=[jax.ShapeDtypeStruct((lp, d), F32)] * 2 + [jax.ShapeDtypeStruct((SUBLANES, d), F32)],
        compiler_params=_params(("arbitrary",)),
    )(dt, dgd, wglu, v, h, dvec, dr)


def loss_backward(hf, tgt, n_valid):
    lp, d = hf.shape
    tm = _tm(lp)

    def body(h_ref, t_ref, dh_ref, s_ref):
        rows = pl.program_id(0) * tm + lax.broadcasted_iota(jnp.int32, (tm, d), 0)
        ok = (rows >= N_META) & (rows < n_valid)
        e = jnp.where(ok, h_ref[...] - t_ref[...], 0.0)
        dh_ref[...] = e * (1.0 / d)

        @pl.when(pl.program_id(0) == 0)
        def _():
            s_ref[...] = jnp.zeros_like(s_ref)

        sq = e * e
        part = sq[:, 0:LANES]
        for c in range(1, d // LANES):
            part = part + sq[:, c * LANES:(c + 1) * LANES]
        acc = part[0:SUBLANES]
        for r in range(1, tm // SUBLANES):
            acc = acc + part[r * SUBLANES:(r + 1) * SUBLANES]
        s_ref[...] += acc * (0.5 / d)

    rs = _row_spec(tm, d)
    return pl.pallas_call(
        body, name="loss_backward", grid=(lp // tm,),
        in_specs=[rs, rs], out_specs=[rs, _full_spec((SUBLANES, LANES))],
        out_shape=[jax.ShapeDtypeStruct((lp, d), F32), jax.ShapeDtypeStruct((SUBLANES, LANES), F32)],
        compiler_params=_params(("arbitrary",)),
    )(hf, tgt)


N_QB = N_Q_HEADS // 2
N_KB = N_KV_HEADS
QKV_W = (N_QB + 2 * N_KB) * LANES
Q_SCALE = HEAD_DIM ** -0.5 * math.log2(math.e)


def rope_tables(lp, n_valid):
    t = jnp.arange(lp, dtype=jnp.int32)
    real = (t >= N_META) & (t < n_valid)
    pos = jnp.where(real, t - N_META, 0)
    row = (pos // GRID_W).astype(F32)
    col = (pos % GRID_W).astype(F32)
    axis_dim = HEAD_DIM // 2
    inv = ROPE_THETA ** (-jnp.arange(0, axis_dim, 2, dtype=F32) / axis_dim)
    ar = row[:, None] * inv[None, :]
    ac = col[:, None] * inv[None, :]
    cos = jnp.concatenate([jnp.cos(ar), jnp.cos(ar), jnp.cos(ac), jnp.cos(ac)], axis=1)
    sin = jnp.concatenate([-jnp.sin(ar), jnp.sin(ar), -jnp.sin(ac), jnp.sin(ac)], axis=1)
    return jnp.tile(cos, (1, 2)), jnp.tile(sin, (1, 2))


def head_sum_matrix():
    return jnp.kron(jnp.eye(2, dtype=F32), jnp.ones((HEAD_DIM, HEAD_DIM), F32)).astype(BF16)


def _segsum(x, e):
    hi = x.astype(BF16)
    lo = (x - hi.astype(F32)).astype(BF16)
    return _nn(hi, e) + _nn(lo, e)


def _swap_halves(x):
    lane = lax.broadcasted_iota(jnp.int32, x.shape, 1)
    quarter = HEAD_DIM // 4
    return jnp.where(lane % (2 * quarter) < quarter, pltpu.roll(x, LANES - quarter, 1), pltpu.roll(x, quarter, 1))


def qkv_forward(h, w2, gq, gk, cos, sin, e):
    lp, d = h.shape
    tm = _tm(lp)
    kw, vw = N_KB * LANES, N_KB * LANES

    def body(h_ref, w_ref, gq_ref, gk_ref, cos_ref, sin_ref, e_ref, raw_ref, q_ref, k_ref, v_ref, qt_ref, vt_ref):
        raw = _nn(h_ref[...].astype(BF16), w_ref[...])
        raw_ref[...] = raw
        c, s_, em = cos_ref[...], sin_ref[...], e_ref[...]
        for cb in range(N_QB + N_KB):
            t = raw[:, cb * LANES:(cb + 1) * LANES]
            rstd = lax.rsqrt(_segsum(t * t, em) * (1.0 / HEAD_DIM) + QK_EPS)
            n = t * rstd * (gq_ref[...] if cb < N_QB else gk_ref[...])
            rot = n * c + _swap_halves(n) * s_
            if cb < N_QB:
                qs = rot * Q_SCALE
                q_ref[:, cb * LANES:(cb + 1) * LANES] = qs.astype(BF16)
                qt_ref[cb * LANES:(cb + 1) * LANES, :] = qs.T.astype(BF16)
            else:
                k_ref[:, (cb - N_QB) * LANES:(cb - N_QB + 1) * LANES] = rot.astype(BF16)
        v_ref[...] = raw[:, (N_QB + N_KB) * LANES:].astype(BF16)
        for cb in range(N_KB):
            lo = (N_QB + N_KB + cb) * LANES
            vt_ref[cb * LANES:(cb + 1) * LANES, :] = raw[:, lo:lo + LANES].T.astype(BF16)

    col_spec = lambda rows: pl.BlockSpec((rows, tm), lambda i: (0, i))
    return pl.pallas_call(
        body, name="qkv_forward", grid=(lp // tm,),
        in_specs=[_row_spec(tm, d), _full_spec((d, QKV_W)), _full_spec((1, LANES)), _full_spec((1, LANES)),
                  _row_spec(tm, LANES), _row_spec(tm, LANES), _full_spec((LANES, LANES))],
        out_specs=[_row_spec(tm, QKV_W), _row_spec(tm, N_QB * LANES), _row_spec(tm, kw), _row_spec(tm, vw),
                   col_spec(N_QB * LANES), col_spec(vw)],
        out_shape=[jax.ShapeDtypeStruct((lp, QKV_W), F32), jax.ShapeDtypeStruct((lp, N_QB * LANES), BF16),
                   jax.ShapeDtypeStruct((lp, kw), BF16), jax.ShapeDtypeStruct((lp, vw), BF16),
                   jax.ShapeDtypeStruct((N_QB * LANES, lp), BF16), jax.ShapeDtypeStruct((vw, lp), BF16)],
        compiler_params=_params(("parallel",)),
    )(h, w2, gq, gk, cos, sin, e)


def qkv_backward(dqs, dk2, dv2, raw, gq, gk, cos, sin, e):
    lp = raw.shape[0]
    tm = _tm(lp)

    def body(dq_ref, dk_ref, dv_ref, raw_ref, gq_ref, gk_ref, cos_ref, sin_ref, e_ref, d_ref, s_ref):
        @pl.when(pl.program_id(0) == 0)
        def _():
            s_ref[...] = jnp.zeros_like(s_ref)

        c, s_, em = cos_ref[...], sin_ref[...], e_ref[...]
        gsum = [jnp.zeros((1, LANES), F32), jnp.zeros((1, LANES), F32)]
        for cb in range(N_QB + N_KB):
            isq = cb < N_QB
            t = raw_ref[:, cb * LANES:(cb + 1) * LANES]
            if isq:
                drot = dq_ref[:, cb * LANES:(cb + 1) * LANES] * (HEAD_DIM ** -0.5)
            else:
                drot = dk_ref[:, (cb - N_QB) * LANES:(cb - N_QB + 1) * LANES] * math.log(2.0)
            gain = gq_ref[...] if isq else gk_ref[...]
            rstd = lax.rsqrt(_segsum(t * t, em) * (1.0 / HEAD_DIM) + QK_EPS)
            dn = drot * c + _swap_halves(drot * s_)
            xh = t * rstd
            gsum[0 if isq else 1] = gsum[0 if isq else 1] + jnp.sum(dn * xh, axis=0, keepdims=True)
            w = dn * gain
            mw = _segsum(w * xh, em) * (1.0 / HEAD_DIM)
            d_ref[:, cb * LANES:(cb + 1) * LANES] = (rstd * (w - xh * mw)).astype(BF16)
        d_ref[:, (N_QB + N_KB) * LANES:] = dv_ref[...].astype(BF16)
        s_ref[0:1, :] += gsum[0]
        s_ref[1:2, :] += gsum[1]

    kw = N_KB * LANES
    return pl.pallas_call(
        body, name="qkv_backward", grid=(lp // tm,),
        in_specs=[_row_spec(tm, N_QB * LANES), _row_spec(tm, kw), _row_spec(tm, kw), _row_spec(tm, QKV_W),
                  _full_spec((1, LANES)), _full_spec((1, LANES)), _row_spec(tm, LANES), _row_spec(tm, LANES),
                  _full_spec((LANES, LANES))],
        out_specs=[_row_spec(tm, QKV_W), _full_spec((SUBLANES, LANES))],
        out_shape=[jax.ShapeDtypeStruct((lp, QKV_W), BF16), jax.ShapeDtypeStruct((SUBLANES, LANES), F32)],
        compiler_params=_params(("arbitrary",)),
    )(dqs, dk2, dv2, raw, gq, gk, cos, sin, e)


NEG = -1e30
Q_PER_KV = N_Q_HEADS // N_KV_HEADS


def _half_masks(x):
    lane = lax.broadcasted_iota(jnp.int32, x.shape, 1)
    zero = jnp.zeros_like(x)
    return jnp.where(lane < HEAD_DIM, x, zero), jnp.where(lane >= HEAD_DIM, x, zero)


ATTN_TR = 16


def _attn_tiles(lp):
    t = _pick_tile(lp, (1408, 256))
    return t, t


def _attn_tiles_bwd(lp):
    return _pick_tile(lp, (768, 256)), _pick_tile(lp, (1408, 256))


def attn_forward_t(qs, k2, v2t, n_valid, prefetch=()):
    lp = qs.shape[0]
    tq, kb = _attn_tiles(lp)
    nk = lp // kb
    gw = 2 * LANES
    nr = kb // ATTN_TR
    pad0 = n_valid - (nk - 1) * kb
    npf = len(prefetch)
    pf_specs, pf_shapes, pf_sems = _prefetch_specs(prefetch)
    nq = lp // tq

    def body(*refs):
        q_ref, k_ref, vt_ref = refs[:3]
        pf_ins = refs[3:3 + npf]
        o_ref, lse_ref = refs[3 + npf:5 + npf]
        pf_outs = refs[5 + npf:5 + 2 * npf]
        m_s, l_s, acc_s, s_s, p_s = refs[5 + 2 * npf:10 + 2 * npf]
        j = pl.program_id(2)
        first = (pl.program_id(0) == 0) & (pl.program_id(1) == 0) & (j == 0)
        last = (pl.program_id(0) == N_KV_HEADS - 1) & (pl.program_id(1) == nq - 1) & (j == nk - 1)
        _prefetch_run(first, last, pf_ins, pf_outs, refs[10 + 2 * npf:])

        @pl.when(j == 0)
        def _():
            m_s[...] = jnp.full_like(m_s, NEG)
            l_s[...] = jnp.zeros_like(l_s)
            acc_s[...] = jnp.zeros_like(acc_s)

        ks = _half_masks(k_ref[...])
        for pair in range(2):
            qp = q_ref[:, pair * LANES:(pair + 1) * LANES]
            for half in range(2):
                hh = 2 * pair + half
                s_s[...] = _nt(ks[half], qp)

                if pad0 < kb:
                    @pl.when(j == nk - 1)
                    def _():
                        s_s[pad0:, :] = jnp.full((kb - pad0, tq), NEG, F32)

                run = s_s[pl.ds(0, ATTN_TR), :]
                for r in range(1, nr):
                    run = jnp.maximum(run, s_s[pl.ds(r * ATTN_TR, ATTN_TR), :])
                m_prev = m_s[hh:hh + 1, :]
                m_new = jnp.maximum(m_prev, jnp.max(run, axis=0, keepdims=True))
                alpha = jnp.exp2(m_prev - m_new)
                m_s[hh:hh + 1, :] = m_new
                for r in range(nr):
                    rows = pl.ds(r * ATTN_TR, ATTN_TR)
                    p_s[rows, :] = jnp.exp2(s_s[rows, :] - m_new).astype(BF16)
                vt = jnp.concatenate([vt_ref[half * HEAD_DIM:(half + 1) * HEAD_DIM, :],
                                      jnp.ones((ATTN_TR, kb), BF16)], axis=0)
                pv = _nn(vt, p_s[...])
                l_s[hh:hh + 1, :] = alpha * l_s[hh:hh + 1, :] + pv[HEAD_DIM:HEAD_DIM + 1, :]
                pv = pv[:HEAD_DIM, :]
                rs = slice(half * HEAD_DIM, (half + 1) * HEAD_DIM)
                acc_s[pair, rs, :] = alpha * acc_s[pair, rs, :] + pv

        @pl.when(j == nk - 1)
        def _():
            for pair in range(2):
                for half in range(2):
                    hh = 2 * pair + half
                    rs = slice(half * HEAD_DIM, (half + 1) * HEAD_DIM)
                    acc_s[pair, rs, :] = acc_s[pair, rs, :] * (1.0 / l_s[hh:hh + 1, :])
                o_ref[:, pair * LANES:(pair + 1) * LANES] = acc_s[pair].T.astype(BF16)
            for hh in range(Q_PER_KV):
                lse_ref[0, hh] = m_s[hh:hh + 1, :] + jnp.log2(l_s[hh:hh + 1, :])

    return pl.pallas_call(
        body, name="attn_forward", grid=(N_KV_HEADS, lp // tq, nk),
        in_specs=[pl.BlockSpec((tq, gw), lambda g, i, j: (i, g)), pl.BlockSpec((kb, LANES), lambda g, i, j: (j, g)),
                  pl.BlockSpec((LANES, kb), lambda g, i, j: (g, j))] + pf_specs,
        out_specs=[pl.BlockSpec((tq, gw), lambda g, i, j: (i, g)),
                   pl.BlockSpec((1, Q_PER_KV, 1, tq), lambda g, i, j: (g, 0, 0, i))] + pf_specs,
        out_shape=[jax.ShapeDtypeStruct((lp, N_QB * LANES), BF16),
                   jax.ShapeDtypeStruct((N_KV_HEADS, Q_PER_KV, 1, lp), F32)] + pf_shapes,
        scratch_shapes=[pltpu.VMEM((SUBLANES, tq), F32), pltpu.VMEM((SUBLANES, tq), F32),
                        pltpu.VMEM((2, LANES, tq), F32), pltpu.VMEM((kb, tq), F32), pltpu.VMEM((kb, tq), BF16)] + pf_sems,
        compiler_params=_params(("arbitrary",) * 3 if npf else ("parallel", "parallel", "arbitrary")),
    )(qs, k2, v2t, *prefetch)


def attn_backward(qs, qst, k2, v2, do, dot, lse, delta, n_valid):
    lp = qs.shape[0]
    tq, kb = _attn_tiles_bwd(lp)
    nq, nk = lp // tq, lp // kb
    gw = 2 * LANES
    pad0 = n_valid - (nk - 1) * kb

    def body(q_ref, qt_ref, k_ref, v_ref, do_ref, dot_ref, lse_ref, dl_ref, dq_ref, dk_ref, dv_ref, acc_s, dkt_s, dvt_s):
        g = pl.program_id(0)
        i = pl.program_id(1)
        j = pl.program_id(2)
        cols = pl.ds(pl.multiple_of(j * kb, kb), kb)

        @pl.when(j == 0)
        def _():
            acc_s[...] = jnp.zeros_like(acc_s)

        @pl.when(i == 0)
        def _():
            dkt_s[:, cols] = jnp.zeros((LANES, kb), F32)
            dvt_s[:, cols] = jnp.zeros((LANES, kb), F32)

        head = lax.broadcasted_iota(jnp.int32, (tq, N_Q_HEADS), 1)

        def column(ref, hh):
            return jnp.sum(jnp.where(head == Q_PER_KV * g + hh, ref[...], 0.0), axis=1, keepdims=True)

        def step(masked):
            ks = _half_masks(k_ref[...])
            vs = _half_masks(v_ref[...])
            if masked:
                col = lax.broadcasted_iota(jnp.int32, (1, kb), 1)
                bias = jnp.where(col < pad0, 0.0, NEG)
            for pair in range(2):
                qp = q_ref[:, pair * LANES:(pair + 1) * LANES]
                dop = do_ref[:, pair * LANES:(pair + 1) * LANES]
                for half in range(2):
                    hh = 2 * pair + half
                    rs = slice(half * HEAD_DIM, (half + 1) * HEAD_DIM)
                    rt = slice(pair * LANES + half * HEAD_DIM, pair * LANES + (half + 1) * HEAD_DIM)
                    s = _nt(qp, ks[half])
                    if masked:
                        s = s + bias
                    p = jnp.exp2(s - column(lse_ref, hh))
                    dp = _nt(dop, vs[half])
                    ds = (p * (dp - column(dl_ref, hh))).astype(BF16)
                    pb = p.astype(BF16)
                    acc_s[pair] += _nn(ds, ks[half])
                    dvt_s[rs, cols] += _nn(dot_ref[rt, :], pb)
                    dkt_s[rs, cols] += _nn(qt_ref[rt, :], ds)

        if pad0 < kb:
            pl.when(j < nk - 1)(lambda: step(False))
            pl.when(j == nk - 1)(lambda: step(True))
        else:
            step(False)

        @pl.when(j == nk - 1)
        def _():
            for pair in range(2):
                dq_ref[:, pair * LANES:(pair + 1) * LANES] = acc_s[pair]

        @pl.when(i == nq - 1)
        def _():
            dk_ref[cols, :] = dkt_s[:, cols].T
            dv_ref[cols, :] = dvt_s[:, cols].T

    cspec = pl.BlockSpec((tq, N_Q_HEADS), lambda g, i, j: (i, 0))
    qspec = pl.BlockSpec((tq, gw), lambda g, i, j: (i, g))
    tspec = pl.BlockSpec((gw, tq), lambda g, i, j: (g, i))
    kspec = pl.BlockSpec((kb, LANES), lambda g, i, j: (j, g))
    gspec = pl.BlockSpec((lp, LANES), lambda g, i, j: (0, g))
    return pl.pallas_call(
        body, name="attn_backward", grid=(N_KV_HEADS, nq, nk),
        in_specs=[qspec, tspec, kspec, kspec, qspec, tspec, cspec, cspec],
        out_specs=[qspec, gspec, gspec],
        out_shape=[jax.ShapeDtypeStruct((lp, N_QB * LANES), F32),
                   jax.ShapeDtypeStruct((lp, N_KB * LANES), F32), jax.ShapeDtypeStruct((lp, N_KB * LANES), F32)],
        scratch_shapes=[pltpu.VMEM((2, tq, LANES), F32), pltpu.VMEM((LANES, lp), F32), pltpu.VMEM((LANES, lp), F32)],
        compiler_params=_params(("parallel", "arbitrary", "arbitrary")),
    )(qs, qst, k2, v2, do, dot, lse, delta)


def attn_out_backward(dr, wout, o, e16):
    lp, d = dr.shape
    tm = _tm(lp)

    def body(dr_ref, w_ref, o_ref, e_ref, do_ref, dl_ref, dot_ref):
        do32 = _nt(dr_ref[...].astype(BF16), w_ref[...])
        do = do32.astype(BF16)
        do_ref[...] = do
        for cb in range(d // LANES):
            dot_ref[cb * LANES:(cb + 1) * LANES, :] = do32[:, cb * LANES:(cb + 1) * LANES].T.astype(BF16)
        dl_ref[...] = _segsum(do.astype(F32) * o_ref[...].astype(F32), e_ref[...])

    rs = _row_spec(tm, d)
    return pl.pallas_call(
        body, name="attn_out_backward", grid=(lp // tm,),
        in_specs=[rs, _full_spec((d, d)), rs, _full_spec((d, N_Q_HEADS))],
        out_specs=[rs, _row_spec(tm, N_Q_HEADS), pl.BlockSpec((d, tm), lambda i: (0, i))],
        out_shape=[jax.ShapeDtypeStruct((lp, d), BF16), jax.ShapeDtypeStruct((lp, N_Q_HEADS), F32),
                   jax.ShapeDtypeStruct((d, lp), BF16)],
        compiler_params=_params(("parallel",)),
    )(dr, wout, o, e16)


N_CHIPS = 4


def _mesh_pos():
    return lax.axis_index("x"), lax.axis_index("y"), lax.axis_index("c")


def chip_exchange(arrs, scatter, name):
    n = len(arrs)
    hbm = pl.BlockSpec(memory_space=pl.ANY)

    def body(*refs):
        ins, outs = refs[:n], refs[n:2 * n]
        send_sems, recv_sems, loc_sems = refs[2 * n:]
        x, y, c = _mesh_pos()
        me = 2 * x + y
        chips = [(1 - x, y), (x, 1 - y), (1 - x, 1 - y)]
        started = []
        for a in range(n):
            loc = pltpu.make_async_copy(ins[a].at[me] if scatter else ins[a], outs[a].at[me], loc_sems.at[a])
            loc.start()
            started.append(loc)
            for k, (px, py) in enumerate(chips):
                src = ins[a].at[2 * px + py] if scatter else ins[a]
                cp = pltpu.make_async_remote_copy(
                    src_ref=src, dst_ref=outs[a].at[me], send_sem=send_sems.at[3 * a + k], recv_sem=recv_sems.at[3 * a + k],
                    device_id=(px, py, c), device_id_type=MESH)
                cp.start()
                started.append(cp)
        for cp in started:
            cp.wait()

    out_shape = [jax.ShapeDtypeStruct(a.shape if scatter else (N_CHIPS,) + a.shape, a.dtype) for a in arrs]
    return pl.pallas_call(
        body, name=name, in_specs=[hbm] * n, out_specs=[hbm] * n, out_shape=out_shape,
        scratch_shapes=[pltpu.SemaphoreType.DMA((3 * n,)), pltpu.SemaphoreType.DMA((3 * n,)), pltpu.SemaphoreType.DMA((n,))],
    )(*arrs)


def _same_core_copies(ins, outs, send_sems, recv_sems, loc_sems):
    x, y, c = _mesh_pos()
    me = 2 * x + y
    chips = [(1 - x, y), (x, 1 - y), (1 - x, 1 - y)]
    cps = []
    for a in range(len(ins)):
        cps.append(pltpu.make_async_copy(ins[a], outs[a].at[me], loc_sems.at[a]))
        for k, (px, py) in enumerate(chips):
            cps.append(pltpu.make_async_remote_copy(
                src_ref=ins[a], dst_ref=outs[a].at[me], send_sem=send_sems.at[3 * a + k], recv_sem=recv_sems.at[3 * a + k],
                device_id=(px, py, c), device_id_type=MESH))
    return cps


def _prefetch_specs(arrs):
    n = len(arrs)
    hbm = pl.BlockSpec(memory_space=pl.ANY)
    shapes = [jax.ShapeDtypeStruct((N_CHIPS,) + a.shape, a.dtype) for a in arrs]
    sems = [pltpu.SemaphoreType.DMA((3 * n,)), pltpu.SemaphoreType.DMA((3 * n,)), pltpu.SemaphoreType.DMA((n,))] if n else []
    return [hbm] * n, shapes, sems


def _prefetch_run(first, last, ins, outs, sems):
    if not ins:
        return

    @pl.when(first)
    def _():
        for cp in _same_core_copies(ins, outs, *sems):
            cp.start()

    @pl.when(last)
    def _():
        for cp in _same_core_copies(ins, outs, *sems):
            cp.wait()


def gather_two_level(arrs, name):
    n = len(arrs)
    hbm = pl.BlockSpec(memory_space=pl.ANY)

    def body(*refs):
        ins, outs = refs[:n], refs[n:2 * n]
        ici_send, ici_recv, d2d_send, d2d_recv, loc_sems = refs[2 * n:]
        x, y, c = _mesh_pos()
        me = 2 * x + y
        chips = [(1 - x, y), (x, 1 - y), (1 - x, 1 - y)]
        started = []
        for a in range(n):
            hn = arrs[a].shape[0] // 2
            mine = pl.ds(c * hn, hn)
            loc = pltpu.make_async_copy(ins[a], outs[a].at[me], loc_sems.at[a])
            loc.start()
            started.append(loc)
            first = []
            for k, (px, py) in enumerate(chips):
                cp = pltpu.make_async_remote_copy(
                    src_ref=ins[a].at[mine], dst_ref=outs[a].at[me, mine], send_sem=ici_send.at[3 * a + k],
                    recv_sem=ici_recv.at[3 * a + k], device_id=(px, py, c), device_id_type=MESH)
                cp.start()
                first.append(cp)
            for k, (px, py) in enumerate(chips):
                q = 2 * px + py
                first[k].wait_recv()
                fw = pltpu.make_async_remote_copy(
                    src_ref=outs[a].at[q, mine], dst_ref=outs[a].at[q, mine], send_sem=d2d_send.at[3 * a + k],
                    recv_sem=d2d_recv.at[3 * a + k], device_id=(x, y, 1 - c), device_id_type=MESH)
                fw.start()
                started.append(fw)
            for cp in first:
                cp.wait_send()
        for cp in started:
            cp.wait()

    out_shape = [jax.ShapeDtypeStruct((N_CHIPS,) + a.shape, a.dtype) for a in arrs]
    return pl.pallas_call(
        body, name=name, in_specs=[hbm] * n, out_specs=[hbm] * n, out_shape=out_shape,
        scratch_shapes=[pltpu.SemaphoreType.DMA((3 * n,))] * 4 + [pltpu.SemaphoreType.DMA((n,))],
    )(*arrs)


def sibling_exchange(arrs, name):
    n = len(arrs)
    hbm = pl.BlockSpec(memory_space=pl.ANY)

    def body(*refs):
        ins, outs = refs[:n], refs[n:2 * n]
        send_sems, recv_sems = refs[2 * n:]
        x, y, c = _mesh_pos()
        started = []
        for a in range(n):
            cp = pltpu.make_async_remote_copy(
                src_ref=ins[a], dst_ref=outs[a], send_sem=send_sems.at[a], recv_sem=recv_sems.at[a],
                device_id=(x, y, 1 - c), device_id_type=MESH)
            cp.start()
            started.append(cp)
        for cp in started:
            cp.wait()

    return pl.pallas_call(
        body, name=name, in_specs=[hbm] * n, out_specs=[hbm] * n,
        out_shape=[jax.ShapeDtypeStruct(a.shape, a.dtype) for a in arrs],
        scratch_shapes=[pltpu.SemaphoreType.DMA((n,)), pltpu.SemaphoreType.DMA((n,))],
    )(*arrs)


def _rows_tile(r, c):
    return _pick_tile(r, tuple(t for t in (512, 256, 128, 64, 32, 16, 8) if t * c * 4 <= 2 * 1024 * 1024))


def chip_sum(recv, name):
    _, r, c = recv.shape
    tr = _rows_tile(r, c)

    def body(r_ref, o_ref):
        acc = r_ref[0].astype(F32)
        for q in range(1, N_CHIPS):
            acc = acc + r_ref[q].astype(F32)
        o_ref[...] = acc

    return pl.pallas_call(
        body, name=name, grid=(r // tr,),
        in_specs=[pl.BlockSpec((N_CHIPS, tr, c), lambda i: (0, i, 0))],
        out_specs=pl.BlockSpec((tr, c), lambda i: (i, 0)),
        out_shape=jax.ShapeDtypeStruct((r, c), F32),
        compiler_params=_params(("parallel",)),
    )(recv)


def pair_sum(part, sib, name, dtype=F32):
    r, c = part.shape
    tr = _rows_tile(r, c)

    def body(p_ref, s_ref, o_ref):
        o_ref[...] = (p_ref[...].astype(F32) + s_ref[...].astype(F32)).astype(dtype)

    rs = pl.BlockSpec((tr, c), lambda i: (i, 0))
    return pl.pallas_call(
        body, name=name, grid=(r // tr,), in_specs=[rs] * 2, out_specs=rs,
        out_shape=jax.ShapeDtypeStruct((r, c), dtype), compiler_params=_params(("parallel",)),
    )(part, sib)


def adamw(g, w, m, v, name):
    r, c = w.shape
    tr = _rows_tile(r, c)

    def body(g_ref, w_ref, m_ref, v_ref, d_ref, nm_ref, nv_ref):
        g_ = g_ref[...]
        m_ = ADAM_B1 * m_ref[...] + (1.0 - ADAM_B1) * g_
        v_ = ADAM_B2 * v_ref[...] + (1.0 - ADAM_B2) * (g_ * g_)
        m_hat = m_ / (1.0 - ADAM_B1 ** ADAM_STEP)
        v_hat = v_ / (1.0 - ADAM_B2 ** ADAM_STEP)
        d_ref[...] = -ADAM_LR * (m_hat / (jnp.sqrt(v_hat) + ADAM_EPS) + ADAM_WD * w_ref[...])
        nm_ref[...] = m_
        nv_ref[...] = v_

    rs = pl.BlockSpec((tr, c), lambda i: (i, 0))
    return pl.pallas_call(
        body, name=name, grid=(r // tr,), in_specs=[rs] * 4, out_specs=[rs] * 3,
        out_shape=[jax.ShapeDtypeStruct((r, c), F32)] * 3,
        compiler_params=_params(("parallel",)),
    )(g, w, m, v)


def adamw_halves(mine, other, core, w, m, v, name):
    r, c = w.shape
    tr = _rows_tile(r // 2, c)
    th = (r // 2) // tr

    def body(core_ref, a_ref, b_ref, w_ref, m_ref, v_ref, g_ref, d_ref, nm_ref, nv_ref):
        g_ = jnp.where(pl.program_id(0) // th == core_ref[0], a_ref[...], b_ref[...])
        m_ = ADAM_B1 * m_ref[...] + (1.0 - ADAM_B1) * g_
        v_ = ADAM_B2 * v_ref[...] + (1.0 - ADAM_B2) * (g_ * g_)
        m_hat = m_ / (1.0 - ADAM_B1 ** ADAM_STEP)
        v_hat = v_ / (1.0 - ADAM_B2 ** ADAM_STEP)
        g_ref[...] = g_
        d_ref[...] = -ADAM_LR * (m_hat / (jnp.sqrt(v_hat) + ADAM_EPS) + ADAM_WD * w_ref[...])
        nm_ref[...] = m_
        nv_ref[...] = v_

    half = pl.BlockSpec((tr, c), lambda i, core_ref: (i % th, 0))
    rows = pl.BlockSpec((tr, c), lambda i, core_ref: (i, 0))
    return pl.pallas_call(
        body, name=name,
        grid_spec=pltpu.PrefetchScalarGridSpec(num_scalar_prefetch=1, grid=(r // tr,), in_specs=[half, half, rows, rows, rows],
                                               out_specs=[rows] * 4),
        out_shape=[jax.ShapeDtypeStruct((r, c), F32)] * 4,
        compiler_params=_params(("parallel",)),
    )(core, mine, other, w, m, v)


WEIGHTS = ['meta_tokens', 's5_lambda_re', 's5_lambda_im', 's5_log_dt', 's5_b_re', 's5_b_im', 's5_c_re', 's5_c_im', 's5_d',
           's5_w_glu', 's5_w_out', 'attn_w_qkv', 'attn_q_gain', 'attn_k_gain', 'attn_w_out', 'ffn_w_gate', 'ffn_w_up',
           'ffn_w_down', 'ln_gain', 'ln_bias']
BIG = ['s5_w_glu', 's5_w_out', 'attn_w_qkv', 'attn_w_out', 'ffn_w_gate', 'ffn_w_up', 'ffn_w_down']
ROW_SHARDED = {'s5_w_glu', 's5_w_out', 'attn_w_out', 'ffn_w_down'}
SMALL_SHARDED = ['meta_tokens', 'ln_gain', 'ln_bias']
REPLICATED = ['s5_lambda_re', 's5_lambda_im', 's5_log_dt', 's5_b_re', 's5_b_im', 's5_c_re', 's5_c_im', 's5_d',
              'attn_q_gain', 'attn_k_gain']
REP_ALIGN = N_CHIPS * LANES * LANES


def _natural(gathered, row_sharded):
    p, n, a, b = gathered.shape
    if row_sharded:
        return jnp.transpose(gathered, (1, 0, 2, 3)).reshape(n, p * a, b)
    return jnp.transpose(gathered, (1, 2, 0, 3)).reshape(n, a, p * b)


def _shard_major(full, row_sharded):
    n, a, b = full.shape
    if row_sharded:
        return jnp.transpose(full.reshape(n, N_CHIPS, a // N_CHIPS, b), (1, 0, 2, 3))
    return jnp.transpose(full.reshape(n, a, N_CHIPS, b // N_CHIPS), (2, 0, 1, 3))


def _dup_heads(w):
    lead = w.shape[:-1]
    w = w.reshape(lead + (N_KV_HEADS, 1, HEAD_DIM))
    return jnp.broadcast_to(w, lead + (N_KV_HEADS, 2, HEAD_DIM)).reshape(lead + (N_KV_HEADS * 2 * HEAD_DIM,))


def _fold_heads(d):
    lead = d.shape[:-1]
    return d.reshape(lead + (N_KV_HEADS, 2, HEAD_DIM)).sum(axis=-2).reshape(lead + (N_KV_HEADS * HEAD_DIM,))


def _pack_rep(tree):
    flat = jnp.concatenate([tree[n].reshape(-1) for n in REPLICATED])
    pad = _round_up(flat.shape[0], REP_ALIGN) - flat.shape[0]
    return jnp.pad(flat, (0, pad))


def _unpack_rep(flat, like):
    out, off = {}, 0
    for n in REPLICATED:
        size = math.prod(like[n].shape)
        out[n] = flat[off:off + size].reshape(like[n].shape)
        off += size
    return out


def _train_step(x, loss_target, w, mom, vel):
    s = x.shape[1]
    n_valid = N_META + s
    lp = _round_up(n_valid, 2 * LANES)
    nq = N_Q_HEADS * HEAD_DIM
    nkv = N_KV_HEADS * HEAD_DIM

    small = jnp.concatenate([w[n].reshape(-1, w[n].shape[-1]) for n in SMALL_SHARDED], axis=0)
    shard = {n: w[n].astype(BF16) for n in BIG}
    uses = [[('s5_w_glu', 0), ('s5_w_out', 0), ('ffn_w_gate', 0), ('ffn_w_up', 0), ('ffn_w_down', 0)],
            [('attn_w_qkv', 0), ('attn_w_out', 0), ('ffn_w_gate', 1), ('ffn_w_up', 1), ('ffn_w_down', 1)],
            [('s5_w_glu', 1), ('s5_w_out', 1), ('ffn_w_gate', 2), ('ffn_w_up', 2), ('ffn_w_down', 2),
             ('attn_w_qkv', 1), ('attn_w_out', 1), ('ffn_w_gate', 3), ('ffn_w_up', 3), ('ffn_w_down', 3)]]
    full = {}

    def unpack(stage, gathered):
        for (n, l), g in zip(uses[stage], gathered):
            full[(n, l)] = _natural(g[:, None], n in ROW_SHARDED)[0]

    first = gather_two_level([shard[n][l] for n, l in uses[0]] + [small], "gather_weights")
    unpack(0, first[:-1])
    small_full = jnp.transpose(first[-1], (1, 0, 2)).reshape(small.shape[0], D_MODEL)
    meta_full = small_full[:N_META]
    ln_gain = small_full[N_META:N_META + 2 * DEPTH].reshape(DEPTH, 2, 1, D_MODEL)
    ln_bias = small_full[N_META + 2 * DEPTH:].reshape(DEPTH, 2, 1, D_MODEL)

    def qkv_dup(wqkv):
        return jnp.concatenate([wqkv[..., :nq], _dup_heads(wqkv[..., nq:nq + nkv]), _dup_heads(wqkv[..., nq + nkv:])], axis=-1)

    w2 = {}

    cos, sin = rope_tables(lp, n_valid)
    e128 = head_sum_matrix()
    e16 = jnp.kron(jnp.eye(N_Q_HEADS, dtype=F32), jnp.ones((HEAD_DIM, 1), F32)).astype(BF16)
    gq = jnp.tile(w['attn_q_gain'], (1, 2))[:, None, :]
    gk = jnp.tile(w['attn_k_gain'], (1, 2))[:, None, :]

    pad_rows = jnp.zeros((lp - n_valid, D_MODEL), F32)
    h = jnp.concatenate([meta_full, x[0], pad_rows], axis=0)
    tgt = jnp.concatenate([jnp.zeros((N_META, D_MODEL), F32), loss_target[0], pad_rows], axis=0)

    saved = []
    s5_names = ['s5_lambda_re', 's5_lambda_im', 's5_log_dt', 's5_b_re', 's5_b_im', 's5_c_re', 's5_c_im']
    for i in range(DEPTH):
        j = i // 2
        sv = {'h': h}
        if i % 2 == 0:
            ops, sv['prep_vjp'] = jax.vjp(s5_prep, *[w[n][j] for n in s5_names])
            m_, wx_, ci_, at_ = ops
            two = lambda t: t.reshape((2 * S5_NJ,) + t.shape[2:])
            sv['ops'] = (blockdiag_expand(m_, S5_CH, S5_CH, "s5_expand_m"),
                         blockdiag_expand(two(wx_), S5_CH, S5_STATE, "s5_expand_wx").reshape(2, S5_NJ, S5_W, S5_W),
                         blockdiag_expand(two(ci_), S5_STATE, S5_CH, "s5_expand_ci").reshape(2, S5_NJ, S5_W, S5_W), at_)
            pf = [shard[n][l] for n, l in uses[1]] if i == 0 else []
            y, sv['lhs'], sv['sp'], sv['sn'], *got = s5_forward(h, *sv['ops'], n_valid, prefetch=pf)
            if i == 0:
                unpack(1, got)
            sv['v'], sv['t'], sv['g'], sv['z'] = glu_forward(y, h, w['s5_d'][j][None], full['s5_w_glu', j])
            sv['r1'], h1 = proj_ln_forward(sv['z'], full['s5_w_out', j], h, ln_gain[i, 0], ln_bias[i, 0], "s5_out_ln")
        else:
            w2[j] = qkv_dup(full['attn_w_qkv', j])
            sv['raw'], sv['qs'], sv['k2'], sv['v2'], sv['qst'], v2t = qkv_forward(h, w2[j], gq[j], gk[j], cos, sin, e128)
            pf = [shard[n][l] for n, l in uses[2]] if i == 1 else []
            sv['o'], lse, *got = attn_forward_t(sv['qs'], sv['k2'], v2t, n_valid, prefetch=pf)
            if i == 1:
                unpack(2, got)
            sv['lse'] = lse.reshape(N_Q_HEADS, lp).T
            sv['r1'], h1 = proj_ln_forward(sv['o'], full['attn_w_out', j], h, ln_gain[i, 0], ln_bias[i, 0], "attn_out_ln")
        sv['h1'] = h1
        sv['a'], sv['b'], sv['f'] = ffn_up_forward(h1, full['ffn_w_gate', i], full['ffn_w_up', i])
        sv['r2'], h = proj_ln_forward(sv['f'], full['ffn_w_down', i], h1, ln_gain[i, 1], ln_bias[i, 1], "ffn_down_ln")
        saved.append(sv)

    dh, loss_part = loss_backward(h, tgt, n_valid)
    loss = lax.psum(jnp.sum(loss_part), ("x", "y", "c"))

    gfull = {n: [None] * w[n].shape[0] for n in BIG}
    d_ln_gain = [[None, None] for _ in range(DEPTH)]
    d_ln_bias = [[None, None] for _ in range(DEPTH)]
    grep = {n: [None] * w[n].shape[0] for n in REPLICATED}
    for i in reversed(range(DEPTH)):
        j = i // 2
        sv = saved[i]
        dr2, s2 = ln_backward(dh, sv['r2'], ln_gain[i, 1])
        d_ln_gain[i][1], d_ln_bias[i][1] = s2[0], s2[1]
        da, db = ffn_backward_act(dr2, full['ffn_w_down', i], sv['a'], sv['b'])
        gfull['ffn_w_down'][i] = mm_tn(sv['f'], dr2, "grad_ffn_down")
        dh1 = resid_nt(dr2, [da, db], [full['ffn_w_gate', i], full['ffn_w_up', i]], "ffn_backward_x")
        gfull['ffn_w_gate'][i] = mm_tn(sv['h1'], da, "grad_ffn_gate")
        gfull['ffn_w_up'][i] = mm_tn(sv['h1'], db, "grad_ffn_up")
        dr1, s1 = ln_backward(dh1, sv['r1'], ln_gain[i, 0])
        d_ln_gain[i][0], d_ln_bias[i][0] = s1[0], s1[1]
        if i % 2 == 0:
            dt, dgd = glu_backward1(dr1, full['s5_w_out', j], sv['g'], sv['t'])
            gfull['s5_w_out'][j] = mm_tn(sv['z'], dr1, "grad_s5_out")
            dv, dhs, sd = glu_backward2(dt, dgd, full['s5_w_glu', j], sv['v'], sv['h'], w['s5_d'][j][None], dr1)
            grep['s5_d'][j] = sd[0]
            gfull['s5_w_glu'][j] = mm_tn(sv['g'], dt, "grad_s5_glu")
            dh, ldy, dxf, dxr, daf, dar = s5_backward(dv, dhs, *sv['ops'], sv['sp'], sv['sn'], n_valid)
            dm = bmm_tn_compact(sv['lhs'], ldy, S5_CH, S5_CH, "grad_s5_m")
            dwx = jnp.stack([bmm_tn_compact(sv['lhs'], dxf, S5_CH, S5_STATE, "grad_s5_wxf"),
                             bmm_tn_compact(sv['lhs'], dxr, S5_CH, S5_STATE, "grad_s5_wxr")])
            dci = jnp.stack([bmm_tn_compact(sv['sp'], ldy, S5_STATE, S5_CH, "grad_s5_cif"),
                             bmm_tn_compact(sv['sn'], ldy, S5_STATE, S5_CH, "grad_s5_cir")])
            dps = sv['prep_vjp']((dm, dwx, dci, jnp.stack([daf, dar])))
            for n, g in zip(s5_names, dps):
                grep[n][j] = g
        else:
            do, delta, dot = attn_out_backward(dr1, full['attn_w_out', j], sv['o'], e16)
            gfull['attn_w_out'][j] = mm_tn(sv['o'], dr1, "grad_attn_out")
            dq, dk2, dv2 = attn_backward(sv['qs'], sv['qst'], sv['k2'], sv['v2'], do, dot, sv['lse'], delta, n_valid)
            draw, gs = qkv_backward(dq, dk2, dv2, sv['raw'], gq[j], gk[j], cos, sin, e128)
            grep['attn_q_gain'][j] = gs[0, :HEAD_DIM] + gs[0, HEAD_DIM:]
            grep['attn_k_gain'][j] = gs[1, :HEAD_DIM] + gs[1, HEAD_DIM:]
            dh = resid_nt(dr1, [draw], [w2[j]], "attn_backward_x")
            dw2 = mm_tn(sv['h'], draw, "grad_attn_qkv")
            kq = N_QB * LANES
            kk = N_KB * LANES
            gfull['attn_w_qkv'][j] = jnp.concatenate(
                [dw2[:, :kq], _fold_heads(dw2[:, kq:kq + kk]), _fold_heads(dw2[:, kq + kk:])], axis=1)
    grad_x = dh[N_META:n_valid][None]

    core = lax.axis_index("c")
    contrib = [_shard_major(jnp.stack(gfull[n]), n in ROW_SHARDED) for n in BIG]
    small_g = jnp.concatenate([dh[:N_META], jnp.stack([g for pair in d_ln_gain for g in pair]),
                               jnp.stack([g for pair in d_ln_bias for g in pair])], axis=0)
    contrib.append(jnp.transpose(small_g.reshape(-1, N_CHIPS, D_MODEL // N_CHIPS), (1, 0, 2)))
    rep_g = _pack_rep({n: jnp.stack(grep[n]) for n in REPLICATED})
    contrib.append(rep_g.reshape(N_CHIPS, -1, LANES))
    names = BIG + ['small', 'rep']
    wire = [BF16] * len(BIG) + [F32, F32]
    keep, give = [], []
    for t, dt in zip(contrib, wire):
        hn = t.shape[1] // 2
        keep.append(lax.dynamic_slice_in_dim(t, core * hn, hn, axis=1))
        give.append(lax.dynamic_slice_in_dim(t, (1 - core) * hn, hn, axis=1).astype(dt))
    got = sibling_exchange(give, "sibling_contrib")
    two_d = lambda t: t.reshape(-1, t.shape[-1])
    pair = [pair_sum(two_d(a), two_d(b), "pair_sum_" + n, dt).reshape(a.shape)
            for n, a, b, dt in zip(names, keep, got, wire)]
    recv = chip_exchange(pair, True, "scatter_grads")
    halves = [chip_sum(r.reshape(N_CHIPS, -1, r.shape[-1]), "chip_sum_" + n) for n, r in zip(names, recv)]
    others = sibling_exchange(halves, "sibling_halves")
    core_arr = jnp.reshape(core, (1,)).astype(jnp.int32)

    out = {}

    def update_halves(n, a, b, wn, mn, vn):
        shape = wn.shape
        flat = (-1, shape[-1])
        res = adamw_halves(a, b, core_arr, wn.reshape(flat), mn.reshape(flat), vn.reshape(flat), "adamw_" + n)
        return tuple(t.reshape(shape) for t in res)

    for n, a, b in zip(BIG, halves, others):
        out[n] = update_halves(n, a, b, w[n], mom[n], vel[n])
    cat = lambda tree: jnp.concatenate([tree[n].reshape(-1, tree[n].shape[-1]) for n in SMALL_SHARDED], axis=0)
    sm = update_halves("small", halves[-2], others[-2], cat(w), cat(mom), cat(vel))
    off = 0
    for n in SMALL_SHARDED:
        rows = math.prod(w[n].shape[:-1])
        out[n] = tuple(t[off:off + rows].reshape(w[n].shape) for t in sm)
        off += rows
    rep_quarter = jnp.where(core == 0, jnp.concatenate([halves[-1], others[-1]], axis=0),
                            jnp.concatenate([others[-1], halves[-1]], axis=0))

    def update(n, g, wn, mn, vn):
        shape = wn.shape
        flat = (-1, shape[-1])
        d, nm, nv = adamw(g, wn.reshape(flat), mn.reshape(flat), vn.reshape(flat), "adamw_" + n)
        return tuple(t.reshape(shape) for t in (g, d, nm, nv))

    rep_all = chip_exchange([rep_quarter], False, "gather_rep")[0].reshape(-1, LANES)
    rp = update("rep", rep_all, _pack_rep(w).reshape(-1, LANES), _pack_rep(mom).reshape(-1, LANES),
                _pack_rep(vel).reshape(-1, LANES))
    unpacked = [_unpack_rep(t.reshape(-1), w) for t in rp]
    for n in REPLICATED:
        out[n] = tuple(u[n] for u in unpacked)

    return (loss, grad_x, *[out[n][0] for n in WEIGHTS], *[out[n][1] for n in WEIGHTS],
            *[out[n][2] for n in WEIGHTS], *[out[n][3] for n in WEIGHTS])


def kernel(x, meta_tokens, s5_lambda_re, s5_lambda_im, s5_log_dt, s5_b_re, s5_b_im, s5_c_re, s5_c_im, s5_d, s5_w_glu, s5_w_out, attn_w_qkv, attn_q_gain, attn_k_gain, attn_w_out, ffn_w_gate, ffn_w_up, ffn_w_down, ln_gain, ln_bias, loss_target, m_meta_tokens, m_s5_lambda_re, m_s5_lambda_im, m_s5_log_dt, m_s5_b_re, m_s5_b_im, m_s5_c_re, m_s5_c_im, m_s5_d, m_s5_w_glu, m_s5_w_out, m_attn_w_qkv, m_attn_q_gain, m_attn_k_gain, m_attn_w_out, m_ffn_w_gate, m_ffn_w_up, m_ffn_w_down, m_ln_gain, m_ln_bias, v_meta_tokens, v_s5_lambda_re, v_s5_lambda_im, v_s5_log_dt, v_s5_b_re, v_s5_b_im, v_s5_c_re, v_s5_c_im, v_s5_d, v_s5_w_glu, v_s5_w_out, v_attn_w_qkv, v_attn_q_gain, v_attn_k_gain, v_attn_w_out, v_ffn_w_gate, v_ffn_w_up, v_ffn_w_down, v_ln_gain, v_ln_bias):
    given = locals()
    w = {n: given[n] for n in WEIGHTS}
    mom = {n: given["m_" + n] for n in WEIGHTS}
    vel = {n: given["v_" + n] for n in WEIGHTS}
    return _train_step(x, loss_target, w, mom, vel)
```

```python
import math

import jax
import jax.numpy as jnp
from jax import lax
from jax.experimental import pallas as pl
from jax.experimental.pallas import tpu as pltpu

F32 = jnp.float32
BF16 = jnp.bfloat16
MESH = pl.DeviceIdType.MESH

D_MODEL = 1024
N_META = 16
GRID_W = 64
HEAD_DIM = 64
N_Q_HEADS = 16
N_KV_HEADS = 4
ROPE_THETA = 10000.0
QK_EPS = 1e-6
S5_CH = 16
S5_GROUPS = 64
S5_STATE = 64
D_FF = 2816
LN_EPS = 1e-5
DEPTH = 4
ALPHA = (2.0 * DEPTH) ** 0.25
ADAM_LR, ADAM_B1, ADAM_B2, ADAM_EPS, ADAM_WD, ADAM_STEP = 0.001, 0.9, 0.999, 1e-08, 0.01, 10

LANES = 128
SUBLANES = 8
VMEM_LIMIT = 56 * 1024 * 1024

S5_T = 8
S5_GB = LANES // S5_CH
S5_NJ = S5_GROUPS // S5_GB
S5_W = S5_T * LANES
S5_SW = 2 * S5_GB * S5_STATE
S5_HALF = S5_SW // 2


def _round_up(a, b):
    return -(-a // b) * b


def _pick_tile(n, prefs):
    for t in prefs:
        if n % t == 0:
            return t
    return n


def _params(sem=None):
    kw = dict(vmem_limit_bytes=VMEM_LIMIT)
    if sem is not None:
        kw["dimension_semantics"] = sem
    return pltpu.CompilerParams(**kw)


def _dot(a, b, dims):
    return lax.dot_general(a, b, (dims, ((), ())), preferred_element_type=F32)


def _nn(a, b):
    return _dot(a, b, ((1,), (0,)))


def _nt(a, b):
    return _dot(a, b, ((1,), (1,)))


def _tn(a, b):
    return _dot(a, b, ((0,), (0,)))


def _compact(w):
    g, a0, a1, b0, b1 = w.shape
    w = jnp.transpose(w.reshape(S5_NJ, S5_GB, a0, a1, b0, b1), (0, 2, 1, 3, 4, 5))
    return w.reshape(S5_NJ, a0 * S5_GB * a1, b0 * b1)


def s5_prep(lam_re, lam_im, log_dt, b_re, b_im, c_re, c_im):
    hi = lax.Precision.HIGHEST
    t = S5_T
    dt = jnp.exp(log_dt)[..., None]
    taus = jnp.arange(t + 1, dtype=F32)[:, None, None, None]
    mag = jnp.exp(lam_re * dt)
    ang = lam_im * dt
    pr = jnp.concatenate([jnp.ones_like(mag)[None], (mag * jnp.cos(ang))[None],
                          jnp.exp(lam_re * dt * taus[2:]) * jnp.cos(ang * taus[2:])], axis=0)
    pi = jnp.concatenate([jnp.zeros_like(mag)[None], (mag * jnp.sin(ang))[None],
                          jnp.exp(lam_re * dt * taus[2:]) * jnp.sin(ang * taus[2:])], axis=0)
    abr, abi = pr[1], pi[1]
    nr, ni = abr - 1.0, abi
    den = lam_re * lam_re + lam_im * lam_im
    cr = (nr * lam_re + ni * lam_im) / den
    ci_ = (ni * lam_re - nr * lam_im) / den
    bbr = cr[..., None] * b_re - ci_[..., None] * b_im
    bbi = cr[..., None] * b_im + ci_[..., None] * b_re
    er = c_re[None] * pr[:, :, :, None, :] - c_im[None] * pi[:, :, :, None, :]
    ei = c_re[None] * pi[:, :, :, None, :] + c_im[None] * pr[:, :, :, None, :]
    nd, ng, ch = er.shape[1], er.shape[2], er.shape[3]
    lhs = jnp.concatenate([er[:t], -ei[:t]], axis=-1)
    lhs = jnp.transpose(lhs, (1, 2, 0, 3, 4)).reshape(nd, ng, t * ch, 2 * S5_STATE)
    rhs = jnp.concatenate([bbr, bbi], axis=-2)
    kk = jnp.einsum("dgmp,dgpc->dgmc", lhs, rhs, precision=hi)
    kk = jnp.transpose(kk.reshape(nd, ng, t, ch, ch), (2, 0, 1, 3, 4))
    zero = jnp.zeros_like(kk[0, 0])
    mg = jnp.stack([jnp.stack([(kk[i - s, 0] if i > s else zero) + (kk[s - i, 1] if s > i else zero)
                               + ((kk[0, 0] + kk[0, 1]) if i == s else zero) for i in range(t)])
                    for s in range(t)])
    mg = jnp.transpose(mg, (2, 0, 4, 1, 3))
    m = _compact(mg)
    pw_f = jnp.stack([pr[t - 1 - s, 0] for s in range(t)]), jnp.stack([pi[t - 1 - s, 0] for s in range(t)])
    pw_r = jnp.stack([pr[s, 1] for s in range(t)]), jnp.stack([pi[s, 1] for s in range(t)])
    wx = []
    for d, (qr, qi) in enumerate((pw_f, pw_r)):
        wr = qr[..., None] * bbr[d][None] - qi[..., None] * bbi[d][None]
        wi = qr[..., None] * bbi[d][None] + qi[..., None] * bbr[d][None]
        w = jnp.stack([wr, wi], axis=0)
        w = jnp.transpose(w, (2, 1, 4, 0, 3))
        wx.append(_compact(w))
    ci = []
    for d in range(2):
        exps = [i + 1 for i in range(t)] if d == 0 else [t - i for i in range(t)]
        e_r = jnp.stack([er[e, d] for e in exps])
        e_i = jnp.stack([ei[e, d] for e in exps])
        w = jnp.stack([e_r, -e_i], axis=0)
        w = jnp.transpose(w, (2, 0, 4, 1, 3))
        ci.append(_compact(w))
    at = jnp.stack([pr[t], pi[t]], axis=1)
    at = at.reshape(2, 2, S5_NJ, S5_GB * S5_STATE)
    at = jnp.transpose(at, (0, 2, 1, 3)).reshape(2, S5_NJ, 1, S5_SW)
    return m, jnp.stack(wx), jnp.stack(ci), at


def _chunk_rows(ref, nc):
    return jnp.concatenate([ref[pl.ds(s, nc, stride=S5_T), :] for s in range(S5_T)], axis=1)


def _cmul(ar, ai, sr, si):
    return ar * sr - ai * si, ar * si + ai * sr


def _scan_tiles(nc, reverse, step):
    nt = nc // SUBLANES

    def body(it, carry):
        tix = (nt - 1 - it) if reverse else it
        k0 = pl.multiple_of(tix * SUBLANES, SUBLANES)
        return step(k0, carry)

    return body, nt


def _s5_specs(nc):
    hbm = pl.BlockSpec(memory_space=pl.ANY)
    aspec = pl.BlockSpec((1, 1, S5_SW), lambda j: (j, 0, 0))
    cspec = pl.BlockSpec((1, nc, S5_W), lambda j: (j, 0, 0))
    return hbm, aspec, cspec


def _s5_fetch(j, tok_hbm, w_hbms, tok_s, w_s, sems):
    cols = pl.ds(pl.multiple_of(j * LANES, LANES), LANES)
    cps = [pltpu.make_async_copy(tok_hbm.at[:, cols], tok_s, sems.at[0])]
    for i, w in enumerate(w_hbms):
        cps.append(pltpu.make_async_copy(w.at[j], w_s.at[i], sems.at[1 + i]))
    for cp in cps:
        cp.start()
    return cols, cps


def s5_forward(u, m, wx, ci, at, n_valid, prefetch=()):
    lp = u.shape[0]
    nc = lp // S5_T
    nvc = n_valid // S5_T
    npf = len(prefetch)
    pf_specs, pf_shapes, pf_sems = _prefetch_specs(prefetch)

    def body(*refs):
        u_hbm, m_hbm, wxf_hbm, wxr_hbm, cif_hbm, cir_hbm, atf_ref, atr_ref = refs[:8]
        pf_ins = refs[8:8 + npf]
        y_hbm, lhs_ref, sp_ref, sn_ref = refs[8 + npf:12 + npf]
        pf_outs = refs[12 + npf:12 + 2 * npf]
        tok_s, w_s, xf_s, xr_s, sems = refs[12 + 2 * npf:17 + 2 * npf]
        j = pl.program_id(0)
        _prefetch_run(j == 0, j == S5_NJ - 1, pf_ins, pf_outs, refs[17 + 2 * npf:])
        cols, cps = _s5_fetch(j, u_hbm, (m_hbm, wxf_hbm, wxr_hbm, cif_hbm, cir_hbm), tok_s, w_s, sems)
        cps[0].wait()
        lhs = _chunk_rows(tok_s, nc)
        rows = lax.broadcasted_iota(jnp.int32, lhs.shape, 0)
        lhs = jnp.where(rows < nvc, lhs, 0.0).astype(BF16)
        lhs_ref[0] = lhs
        cps[2].wait()
        cps[3].wait()
        xf_s[...] = _nn(lhs, w_s[1])
        xr_s[...] = _nn(lhs, w_s[2])
        afr, afi = atf_ref[0, :, :S5_HALF], atf_ref[0, :, S5_HALF:]
        arr, ari = atr_ref[0, :, :S5_HALF], atr_ref[0, :, S5_HALF:]

        def scan_step(x_s, ar, ai, descending):
            def step(k0, carry):
                sr, si = carry
                x = x_s[pl.ds(k0, SUBLANES), :]
                outs = [None] * SUBLANES
                order = reversed(range(SUBLANES)) if descending else range(SUBLANES)
                for r in order:
                    outs[r] = jnp.concatenate([sr, si], axis=1)
                    nr, ni = _cmul(ar, ai, sr, si)
                    sr = nr + x[r:r + 1, :S5_HALF]
                    si = ni + x[r:r + 1, S5_HALF:]
                x_s[pl.ds(k0, SUBLANES), :] = jnp.concatenate(outs, axis=0)
                return sr, si
            return step

        zero = jnp.zeros((1, S5_HALF), F32)
        fb, nt = _scan_tiles(nc, False, scan_step(xf_s, afr, afi, False))
        lax.fori_loop(0, nt, fb, (zero, zero))
        rb, nt = _scan_tiles(nc, True, scan_step(xr_s, arr, ari, True))
        lax.fori_loop(0, nt, rb, (zero, zero))
        sp = xf_s[...].astype(BF16)
        sn = xr_s[...].astype(BF16)
        sp_ref[0] = sp
        sn_ref[0] = sn
        cps[1].wait()
        cps[4].wait()
        cps[5].wait()
        y = _nn(lhs, w_s[0]) + _nn(sp, w_s[3]) + _nn(sn, w_s[4])
        for i in range(S5_T):
            tok_s[pl.ds(i, nc, stride=S5_T), :] = y[:, i * LANES:(i + 1) * LANES]
        pltpu.sync_copy(tok_s, y_hbm.at[:, cols])

    hbm, aspec, cspec = _s5_specs(nc)
    return pl.pallas_call(
        body, name="s5_forward", grid=(S5_NJ,),
        in_specs=[hbm] * 6 + [aspec, aspec] + pf_specs,
        out_specs=[hbm, cspec, cspec, cspec] + pf_specs,
        out_shape=[jax.ShapeDtypeStruct((lp, D_MODEL), F32)] + [jax.ShapeDtypeStruct((S5_NJ, nc, S5_W), BF16)] * 3 + pf_shapes,
        scratch_shapes=[pltpu.VMEM((lp, LANES), F32), pltpu.VMEM((5, S5_W, S5_W), BF16),
                        pltpu.VMEM((nc, S5_SW), F32), pltpu.VMEM((nc, S5_SW), F32), pltpu.SemaphoreType.DMA((6,))] + pf_sems,
        compiler_params=_params(("arbitrary",)),
    )(u, m, wx[0], wx[1], ci[0], ci[1], at[0], at[1], *prefetch)


def s5_backward(dy, dhs, m, wx, ci, at, sp, sn, n_valid):
    lp = dy.shape[0]
    nc = lp // S5_T
    nvc = n_valid // S5_T

    def body(dy_hbm, dhs_hbm, m_hbm, wxf_hbm, wxr_hbm, cif_hbm, cir_hbm, atf_ref, atr_ref, sp_ref, sn_ref,
             dh_hbm, ldy_ref, dxf_ref, dxr_ref, daf_ref, dar_ref, tok_s, w_s, gf_s, gr_s, sems):
        j = pl.program_id(0)
        cols, cps = _s5_fetch(j, dy_hbm, (m_hbm, wxf_hbm, wxr_hbm, cif_hbm, cir_hbm), tok_s, w_s, sems)
        cps[0].wait()
        ldy = _chunk_rows(tok_s, nc)
        rows = lax.broadcasted_iota(jnp.int32, ldy.shape, 0)
        ldy = jnp.where(rows < nvc, ldy, 0.0).astype(BF16)
        ldy_ref[0] = ldy
        resid = pltpu.make_async_copy(dhs_hbm.at[:, cols], tok_s, sems.at[0])
        resid.start()
        cps[4].wait()
        cps[5].wait()
        gf_s[...] = _nt(ldy, w_s[3])
        gr_s[...] = _nt(ldy, w_s[4])
        afr, afi = atf_ref[0, :, :S5_HALF], atf_ref[0, :, S5_HALF:]
        arr, ari = atr_ref[0, :, :S5_HALF], atr_ref[0, :, S5_HALF:]

        def adj_step(g_s, s_ref, ar, ai, descending):
            def step(k0, carry):
                gr_, gi_, dr_, di_ = carry
                g = g_s[pl.ds(k0, SUBLANES), :]
                p = s_ref[0, pl.ds(k0, SUBLANES), :].astype(F32)
                outs = [None] * SUBLANES
                order = reversed(range(SUBLANES)) if descending else range(SUBLANES)
                for r in order:
                    outs[r] = jnp.concatenate([gr_, gi_], axis=1)
                    pr_, pi_ = p[r:r + 1, :S5_HALF], p[r:r + 1, S5_HALF:]
                    dr_ = dr_ + gr_ * pr_ + gi_ * pi_
                    di_ = di_ + gi_ * pr_ - gr_ * pi_
                    nr, ni = _cmul(ar, -ai, gr_, gi_)
                    gr_ = nr + g[r:r + 1, :S5_HALF]
                    gi_ = ni + g[r:r + 1, S5_HALF:]
                g_s[pl.ds(k0, SUBLANES), :] = jnp.concatenate(outs, axis=0)
                return gr_, gi_, dr_, di_
            return step

        zero = jnp.zeros((1, S5_HALF), F32)
        fb, nt = _scan_tiles(nc, True, adj_step(gf_s, sp_ref, afr, afi, True))
        _, _, dr_, di_ = lax.fori_loop(0, nt, fb, (zero,) * 4)
        daf_ref[0] = jnp.concatenate([dr_, di_], axis=1)
        rb, nt = _scan_tiles(nc, False, adj_step(gr_s, sn_ref, arr, ari, False))
        _, _, dr_, di_ = lax.fori_loop(0, nt, rb, (zero,) * 4)
        dar_ref[0] = jnp.concatenate([dr_, di_], axis=1)
        dxf = gf_s[...].astype(BF16)
        dxr = gr_s[...].astype(BF16)
        dxf_ref[0] = dxf
        dxr_ref[0] = dxr
        cps[1].wait()
        cps[2].wait()
        cps[3].wait()
        du = _nt(ldy, w_s[0]) + _nt(dxf, w_s[1]) + _nt(dxr, w_s[2])
        rows = lax.broadcasted_iota(jnp.int32, du.shape, 0)
        du = jnp.where(rows < nvc, du, 0.0)
        resid.wait()
        for s in range(S5_T):
            tok_s[pl.ds(s, nc, stride=S5_T), :] += du[:, s * LANES:(s + 1) * LANES]
        pltpu.sync_copy(tok_s, dh_hbm.at[:, cols])

    hbm, aspec, cspec = _s5_specs(nc)
    return pl.pallas_call(
        body, name="s5_backward", grid=(S5_NJ,),
        in_specs=[hbm] * 7 + [aspec, aspec, cspec, cspec],
        out_specs=[hbm, cspec, cspec, cspec, aspec, aspec],
        out_shape=[jax.ShapeDtypeStruct((lp, D_MODEL), F32)] + [jax.ShapeDtypeStruct((S5_NJ, nc, S5_W), BF16)] * 3
        + [jax.ShapeDtypeStruct((S5_NJ, 1, S5_SW), F32)] * 2,
        scratch_shapes=[pltpu.VMEM((lp, LANES), F32), pltpu.VMEM((5, S5_W, S5_W), BF16),
                        pltpu.VMEM((nc, S5_SW), F32), pltpu.VMEM((nc, S5_SW), F32), pltpu.SemaphoreType.DMA((6,))],
        compiler_params=_params(("arbitrary",)),
    )(dy, dhs, m, wx[0], wx[1], ci[0], ci[1], at[0], at[1], sp, sn)


S5_CW = LANES


def _replicate_matrix(b1):
    b0n = S5_CW // b1
    eye0 = jnp.eye(b0n, dtype=F32)
    eye1 = jnp.eye(b1, dtype=F32)
    r = jnp.einsum("ab,cd->acbd", eye0, eye1)[:, :, :, None, :]
    r = jnp.broadcast_to(r, (b0n, b1, b0n, S5_GB, b1))
    return r.reshape(S5_CW, b0n * S5_GB * b1).astype(BF16)


def _same_group(a1, b1):
    rg = (lax.broadcasted_iota(jnp.int32, (S5_W, S5_W), 0) // a1) % S5_GB
    cg = (lax.broadcasted_iota(jnp.int32, (S5_W, S5_W), 1) // b1) % S5_GB
    return rg == cg


def blockdiag_expand(compact, a1, b1, name):
    nj = compact.shape[0]

    def body(c_ref, r_ref, o_ref):
        rep = _nn(c_ref[0].astype(BF16), r_ref[...])
        o_ref[0] = jnp.where(_same_group(a1, b1), rep, 0.0).astype(BF16)

    return pl.pallas_call(
        body, name=name, grid=(nj,),
        in_specs=[pl.BlockSpec((1, S5_W, S5_CW), lambda j: (j, 0, 0)), _full_spec((S5_CW, S5_W))],
        out_specs=pl.BlockSpec((1, S5_W, S5_W), lambda j: (j, 0, 0)),
        out_shape=jax.ShapeDtypeStruct((nj, S5_W, S5_W), BF16),
        compiler_params=_params(("parallel",)),
    )(compact, _replicate_matrix(b1))


def bmm_tn_compact(a, b, a1, b1, name):
    nj, k, wa = a.shape
    wb = b.shape[2]

    def body(a_ref, b_ref, r_ref, o_ref):
        prod = jnp.where(_same_group(a1, b1), _tn(a_ref[0], b_ref[0]), 0.0)
        hi = prod.astype(BF16)
        lo = (prod - hi.astype(F32)).astype(BF16)
        o_ref[0] = _nt(hi, r_ref[...]) + _nt(lo, r_ref[...])

    return pl.pallas_call(
        body, name=name, grid=(nj,),
        in_specs=[pl.BlockSpec((1, k, wa), lambda j: (j, 0, 0)), pl.BlockSpec((1, k, wb), lambda j: (j, 0, 0)),
                  _full_spec((S5_CW, S5_W))],
        out_specs=pl.BlockSpec((1, wa, S5_CW), lambda j: (j, 0, 0)),
        out_shape=jax.ShapeDtypeStruct((nj, wa, S5_CW), F32),
        compiler_params=_params(("parallel",)),
    )(a, b, _replicate_matrix(b1))


def _tm(lp):
    return _pick_tile(lp, (768, 256))


def _row_spec(tm, width):
    return pl.BlockSpec((tm, width), lambda i: (i, 0))


def _full_spec(shape):
    return pl.BlockSpec(shape, lambda *_: (0,) * len(shape))


def _gelu(v):
    return 0.5 * v * (1.0 + lax.erf(v * (2.0 ** -0.5)))


def _gelu_grad(v):
    return 0.5 * (1.0 + lax.erf(v * (2.0 ** -0.5))) + v * jnp.exp(-0.5 * v * v) * (2.0 * math.pi) ** -0.5


def _layer_norm(r, gain, bias):
    mean = jnp.mean(r, axis=-1, keepdims=True)
    c = r - mean
    var = jnp.mean(c * c, axis=-1, keepdims=True)
    return c * lax.rsqrt(var + LN_EPS) * gain + bias


def glu_forward(y, h, dvec, wglu):
    lp, d = y.shape
    tm = _tm(lp)

    def body(y_ref, h_ref, d_ref, w_ref, v_ref, t_ref, g_ref, z_ref):
        v = y_ref[...] + d_ref[...] * h_ref[...]
        g = _gelu(v)
        gb = g.astype(BF16)
        t = _nn(gb, w_ref[...])
        v_ref[...] = v
        t_ref[...] = t
        g_ref[...] = gb
        z_ref[...] = (g * jax.nn.sigmoid(t)).astype(BF16)

    rs = _row_spec(tm, d)
    return pl.pallas_call(
        body, name="glu_forward", grid=(lp // tm,),
        in_specs=[rs, rs, _full_spec((1, d)), _full_spec((d, d))],
        out_specs=[rs, rs, rs, rs],
        out_shape=[jax.ShapeDtypeStruct((lp, d), F32)] * 2 + [jax.ShapeDtypeStruct((lp, d), BF16)] * 2,
        compiler_params=_params(("parallel",)),
    )(y, h, dvec, wglu)


def proj_ln_forward(z, w, h, gain, bias, name):
    lp, k = z.shape
    d = w.shape[1]
    tm = _tm(lp)

    def body(z_ref, w_ref, h_ref, g_ref, b_ref, r_ref, o_ref):
        r = ALPHA * h_ref[...] + _nn(z_ref[...], w_ref[...])
        r_ref[...] = r
        o_ref[...] = _layer_norm(r, g_ref[...], b_ref[...])

    rs = _row_spec(tm, d)
    return pl.pallas_call(
        body, name=name, grid=(lp // tm,),
        in_specs=[_row_spec(tm, k), _full_spec((k, d)), rs, _full_spec((1, d)), _full_spec((1, d))],
        out_specs=[rs, rs],
        out_shape=[jax.ShapeDtypeStruct((lp, d), F32)] * 2,
        compiler_params=_params(("parallel",)),
    )(z, w, h, gain, bias)


FFN_TM = 384


def ffn_up_forward(h, wg, wu):
    lp, d = h.shape
    dff = wg.shape[1]
    tm = _pick_tile(lp, (FFN_TM, 256))

    def body(h_ref, wg_ref, wu_ref, a_ref, b_ref, f_ref):
        hb = h_ref[...].astype(BF16)
        a = _nn(hb, wg_ref[...])
        b = _nn(hb, wu_ref[...])
        a_ref[...] = a.astype(BF16)
        b_ref[...] = b.astype(BF16)
        f_ref[...] = (a * jax.nn.sigmoid(a) * b).astype(BF16)

    ws = _full_spec((d, dff))
    os_ = _row_spec(tm, dff)
    return pl.pallas_call(
        body, name="ffn_up_forward", grid=(lp // tm,),
        in_specs=[_row_spec(tm, d), ws, ws],
        out_specs=[os_, os_, os_],
        out_shape=[jax.ShapeDtypeStruct((lp, dff), BF16)] * 3,
        compiler_params=_params(("parallel",)),
    )(h, wg, wu)


def _ln_backward_rows(dh_, r_, gain):
    mean = jnp.mean(r_, axis=-1, keepdims=True)
    c = r_ - mean
    var = jnp.mean(c * c, axis=-1, keepdims=True)
    rstd = lax.rsqrt(var + LN_EPS)
    xh = c * rstd
    dxh = dh_ * gain
    m1 = jnp.mean(dxh, axis=-1, keepdims=True)
    m2 = jnp.mean(dxh * xh, axis=-1, keepdims=True)
    return rstd * (dxh - m1 - xh * m2), jnp.sum(dh_ * xh, axis=0, keepdims=True), jnp.sum(dh_, axis=0, keepdims=True)


def _accumulate_ln_sums(s_ref, sg, sb):
    @pl.when(pl.program_id(0) == 0)
    def _():
        s_ref[...] = jnp.zeros_like(s_ref)

    s_ref[0:1, :] += sg
    s_ref[1:2, :] += sb


def ln_backward(dh, r, gain):
    lp, d = dh.shape
    tm = _tm(lp)

    def body(dh_ref, r_ref, g_ref, dr_ref, s_ref):
        dr, sg, sb = _ln_backward_rows(dh_ref[...], r_ref[...], g_ref[...])
        dr_ref[...] = dr
        _accumulate_ln_sums(s_ref, sg, sb)

    rs = _row_spec(tm, d)
    return pl.pallas_call(
        body, name="ln_backward", grid=(lp // tm,),
        in_specs=[rs, rs, _full_spec((1, d))],
        out_specs=[rs, _full_spec((SUBLANES, d))],
        out_shape=[jax.ShapeDtypeStruct((lp, d), F32), jax.ShapeDtypeStruct((SUBLANES, d), F32)],
        compiler_params=_params(("arbitrary",)),
    )(dh, r, gain)


def ffn_backward_act(dr, wd, a, b):
    lp, d = dr.shape
    dff = wd.shape[0]
    tm = _pick_tile(lp, (FFN_TM, 256))

    def body(dr_ref, wd_ref, a_ref, b_ref, da_ref, db_ref):
        df = _nt(dr_ref[...].astype(BF16), wd_ref[...])
        a_ = a_ref[...].astype(F32)
        b_ = b_ref[...].astype(F32)
        sg = jax.nn.sigmoid(a_)
        da_ref[...] = (df * b_ * sg * (1.0 + a_ * (1.0 - sg))).astype(BF16)
        db_ref[...] = (df * a_ * sg).astype(BF16)

    os_ = _row_spec(tm, dff)
    return pl.pallas_call(
        body, name="ffn_backward_act", grid=(lp // tm,),
        in_specs=[_row_spec(tm, d), _full_spec((dff, d)), os_, os_],
        out_specs=[os_, os_],
        out_shape=[jax.ShapeDtypeStruct((lp, dff), BF16)] * 2,
        compiler_params=_params(("parallel",)),
    )(dr, wd, a, b)


def resid_nt(dr, xs, ws, name, ln=None):
    lp, d = dr.shape
    tm = _pick_tile(lp, (FFN_TM, 256))
    n = len(xs)

    def body(*refs):
        acc = ALPHA * refs[0][...]
        for i in range(n):
            acc = acc + _nt(refs[1 + i][...], refs[1 + n + i][...])
        if ln is None:
            refs[-1][...] = acc
        else:
            r_ref, g_ref, o_ref, s_ref = refs[1 + 2 * n:]
            dr_, sg, sb = _ln_backward_rows(acc, r_ref[...], g_ref[...])
            o_ref[...] = dr_
            _accumulate_ln_sums(s_ref, sg, sb)

    rs = _row_spec(tm, d)
    in_specs = [rs] + [_row_spec(tm, x.shape[1]) for x in xs] + [_full_spec(w.shape) for w in ws]
    if ln is None:
        return pl.pallas_call(
            body, name=name, grid=(lp // tm,),
            in_specs=in_specs, out_specs=rs,
            out_shape=jax.ShapeDtypeStruct((lp, d), F32),
            compiler_params=_params(("parallel",)),
        )(dr, *xs, *ws)
    return pl.pallas_call(
        body, name=name, grid=(lp // tm,),
        in_specs=in_specs + [rs, _full_spec((1, d))], out_specs=[rs, _full_spec((SUBLANES, d))],
        out_shape=[jax.ShapeDtypeStruct((lp, d), F32), jax.ShapeDtypeStruct((SUBLANES, d), F32)],
        compiler_params=_params(("arbitrary",)),
    )(dr, *xs, *ws, *ln)


def mm_tn(x, y, name):
    lp, k = x.shape
    n = y.shape[1]
    tm = _tm(lp)
    nb = _pick_tile(n, (512, 2816))

    def body(x_ref, y_ref, o_ref):
        @pl.when(pl.program_id(1) == 0)
        def _():
            o_ref[...] = jnp.zeros_like(o_ref)

        o_ref[...] += _tn(x_ref[...].astype(BF16), y_ref[...].astype(BF16))

    return pl.pallas_call(
        body, name=name, grid=(n // nb, lp // tm),
        in_specs=[pl.BlockSpec((tm, k), lambda j, i: (i, 0)), pl.BlockSpec((tm, nb), lambda j, i: (i, j))],
        out_specs=pl.BlockSpec((k, nb), lambda j, i: (0, j)),
        out_shape=jax.ShapeDtypeStruct((k, n), F32),
        compiler_params=_params(("parallel", "arbitrary")),
    )(x, y)


def glu_backward1(dr, wout, g, t):
    lp, d = dr.shape
    tm = _tm(lp)

    def body(dr_ref, w_ref, g_ref, t_ref, dt_ref, dgd_ref):
        dz = _nt(dr_ref[...].astype(BF16), w_ref[...])
        s = jax.nn.sigmoid(t_ref[...])
        dgd_ref[...] = dz * s
        dt_ref[...] = (dz * g_ref[...].astype(F32) * s * (1.0 - s)).astype(BF16)

    rs = _row_spec(tm, d)
    return pl.pallas_call(
        body, name="glu_backward1", grid=(lp // tm,),
        in_specs=[rs, _full_spec((d, d)), rs, rs],
        out_specs=[rs, rs],
        out_shape=[jax.ShapeDtypeStruct((lp, d), BF16), jax.ShapeDtypeStruct((lp, d), F32)],
        compiler_params=_params(("parallel",)),
    )(dr, wout, g, t)


def glu_backward2(dt, dgd, wglu, v, h, dvec, dr):
    lp, d = dt.shape
    tm = _tm(lp)

    def body(dt_ref, dgd_ref, w_ref, v_ref, h_ref, d_ref, dr_ref, dv_ref, dhs_ref, s_ref):
        dg = dgd_ref[...] + _nt(dt_ref[...], w_ref[...])
        dv = dg * _gelu_grad(v_ref[...])
        dv_ref[...] = dv
        dhs_ref[...] = ALPHA * dr_ref[...] + dv * d_ref[...]

        @pl.when(pl.program_id(0) == 0)
        def _():
            s_ref[...] = jnp.zeros_like(s_ref)

        s_ref[0:1, :] += jnp.sum(dv * h_ref[...], axis=0, keepdims=True)

    rs = _row_spec(tm, d)
    return pl.pallas_call(
        body, name="glu_backward2", grid=(lp // tm,),
        in_specs=[rs, rs, _full_spec((d, d)), rs, rs, _full_spec((1, d)), rs],
        out_specs=[rs, rs, _full_spec((SUBLANES, d))],
        out_shape=[jax.ShapeDtypeStruct((lp, d), F32)] * 2 + [jax.ShapeDtypeStruct((SUBLANES, d), F32)],
        compiler_params=_params(("arbitrary",)),
    )(dt, dgd, wglu, v, h, dvec, dr)


def loss_backward(hf, tgt, n_valid):
    lp, d = hf.shape
    tm = _tm(lp)

    def body(h_ref, t_ref, dh_ref, s_ref):
        rows = pl.program_id(0) * tm + lax.broadcasted_iota(jnp.int32, (tm, d), 0)
        ok = (rows >= N_META) & (rows < n_valid)
        e = jnp.where(ok, h_ref[...] - t_ref[...], 0.0)
        dh_ref[...] = e * (1.0 / d)

        @pl.when(pl.program_id(0) == 0)
        def _():
            s_ref[...] = jnp.zeros_like(s_ref)

        sq = e * e
        part = sq[:, 0:LANES]
        for c in range(1, d // LANES):
            part = part + sq[:, c * LANES:(c + 1) * LANES]
        acc = part[0:SUBLANES]
        for r in range(1, tm // SUBLANES):
            acc = acc + part[r * SUBLANES:(r + 1) * SUBLANES]
        s_ref[...] += acc * (0.5 / d)

    rs = _row_spec(tm, d)
    return pl.pallas_call(
        body, name="loss_backward", grid=(lp // tm,),
        in_specs=[rs, rs], out_specs=[rs, _full_spec((SUBLANES, LANES))],
        out_shape=[jax.ShapeDtypeStruct((lp, d), F32), jax.ShapeDtypeStruct((SUBLANES, LANES), F32)],
        compiler_params=_params(("arbitrary",)),
    )(hf, tgt)


N_QB = N_Q_HEADS // 2
N_KB = N_KV_HEADS
QKV_W = (N_QB + 2 * N_KB) * LANES
Q_SCALE = HEAD_DIM ** -0.5 * math.log2(math.e)


def rope_tables(lp, n_valid):
    t = jnp.arange(lp, dtype=jnp.int32)
    real = (t >= N_META) & (t < n_valid)
    pos = jnp.where(real, t - N_META, 0)
    row = (pos // GRID_W).astype(F32)
    col = (pos % GRID_W).astype(F32)
    axis_dim = HEAD_DIM // 2
    inv = ROPE_THETA ** (-jnp.arange(0, axis_dim, 2, dtype=F32) / axis_dim)
    ar = row[:, None] * inv[None, :]
    ac = col[:, None] * inv[None, :]
    cos = jnp.concatenate([jnp.cos(ar), jnp.cos(ar), jnp.cos(ac), jnp.cos(ac)], axis=1)
    sin = jnp.concatenate([-jnp.sin(ar), jnp.sin(ar), -jnp.sin(ac), jnp.sin(ac)], axis=1)
    return jnp.tile(cos, (1, 2)), jnp.tile(sin, (1, 2))


def head_sum_matrix():
    return jnp.kron(jnp.eye(2, dtype=F32), jnp.ones((HEAD_DIM, HEAD_DIM), F32)).astype(BF16)


def _segsum(x, e):
    hi = x.astype(BF16)
    lo = (x - hi.astype(F32)).astype(BF16)
    return _nn(hi, e) + _nn(lo, e)


def _swap_halves(x):
    lane = lax.broadcasted_iota(jnp.int32, x.shape, 1)
    quarter = HEAD_DIM // 4
    return jnp.where(lane % (2 * quarter) < quarter, pltpu.roll(x, LANES - quarter, 1), pltpu.roll(x, quarter, 1))


def qkv_forward(h, w2, gq, gk, cos, sin, e):
    lp, d = h.shape
    tm = _tm(lp)
    kw, vw = N_KB * LANES, N_KB * LANES

    def body(h_ref, w_ref, gq_ref, gk_ref, cos_ref, sin_ref, e_ref, raw_ref, q_ref, k_ref, v_ref, qt_ref, vt_ref):
        raw = _nn(h_ref[...].astype(BF16), w_ref[...])
        raw_ref[...] = raw
        c, s_, em = cos_ref[...], sin_ref[...], e_ref[...]
        for cb in range(N_QB + N_KB):
            t = raw[:, cb * LANES:(cb + 1) * LANES]
            rstd = lax.rsqrt(_segsum(t * t, em) * (1.0 / HEAD_DIM) + QK_EPS)
            n = t * rstd * (gq_ref[...] if cb < N_QB else gk_ref[...])
            rot = n * c + _swap_halves(n) * s_
            if cb < N_QB:
                qs = rot * Q_SCALE
                q_ref[:, cb * LANES:(cb + 1) * LANES] = qs.astype(BF16)
                qt_ref[cb * LANES:(cb + 1) * LANES, :] = qs.T.astype(BF16)
            else:
                k_ref[:, (cb - N_QB) * LANES:(cb - N_QB + 1) * LANES] = rot.astype(BF16)
        v_ref[...] = raw[:, (N_QB + N_KB) * LANES:].astype(BF16)
        for cb in range(N_KB):
            lo = (N_QB + N_KB + cb) * LANES
            vt_ref[cb * LANES:(cb + 1) * LANES, :] = raw[:, lo:lo + LANES].T.astype(BF16)

    col_spec = lambda rows: pl.BlockSpec((rows, tm), lambda i: (0, i))
    return pl.pallas_call(
        body, name="qkv_forward", grid=(lp // tm,),
        in_specs=[_row_spec(tm, d), _full_spec((d, QKV_W)), _full_spec((1, LANES)), _full_spec((1, LANES)),
                  _row_spec(tm, LANES), _row_spec(tm, LANES), _full_spec((LANES, LANES))],
        out_specs=[_row_spec(tm, QKV_W), _row_spec(tm, N_QB * LANES), _row_spec(tm, kw), _row_spec(tm, vw),
                   col_spec(N_QB * LANES), col_spec(vw)],
        out_shape=[jax.ShapeDtypeStruct((lp, QKV_W), F32), jax.ShapeDtypeStruct((lp, N_QB * LANES), BF16),
                   jax.ShapeDtypeStruct((lp, kw), BF16), jax.ShapeDtypeStruct((lp, vw), BF16),
                   jax.ShapeDtypeStruct((N_QB * LANES, lp), BF16), jax.ShapeDtypeStruct((vw, lp), BF16)],
        compiler_params=_params(("parallel",)),
    )(h, w2, gq, gk, cos, sin, e)


def qkv_backward(dqs, dk2, dv2, raw, gq, gk, cos, sin, e):
    lp = raw.shape[0]
    tm = _tm(lp)

    def body(dq_ref, dk_ref, dv_ref, raw_ref, gq_ref, gk_ref, cos_ref, sin_ref, e_ref, d_ref, s_ref):
        @pl.when(pl.program_id(0) == 0)
        def _():
            s_ref[...] = jnp.zeros_like(s_ref)

        c, s_, em = cos_ref[...], sin_ref[...], e_ref[...]
        gsum = [jnp.zeros((1, LANES), F32), jnp.zeros((1, LANES), F32)]
        for cb in range(N_QB + N_KB):
            isq = cb < N_QB
            t = raw_ref[:, cb * LANES:(cb + 1) * LANES]
            if isq:
                drot = dq_ref[:, cb * LANES:(cb + 1) * LANES] * (HEAD_DIM ** -0.5)
            else:
                drot = dk_ref[:, (cb - N_QB) * LANES:(cb - N_QB + 1) * LANES] * math.log(2.0)
            gain = gq_ref[...] if isq else gk_ref[...]
            rstd = lax.rsqrt(_segsum(t * t, em) * (1.0 / HEAD_DIM) + QK_EPS)
            dn = drot * c + _swap_halves(drot * s_)
            xh = t * rstd
            gsum[0 if isq else 1] = gsum[0 if isq else 1] + jnp.sum(dn * xh, axis=0, keepdims=True)
            w = dn * gain
            mw = _segsum(w * xh, em) * (1.0 / HEAD_DIM)
            d_ref[:, cb * LANES:(cb + 1) * LANES] = (rstd * (w - xh * mw)).astype(BF16)
        d_ref[:, (N_QB + N_KB) * LANES:] = dv_ref[...].astype(BF16)
        s_ref[0:1, :] += gsum[0]
        s_ref[1:2, :] += gsum[1]

    kw = N_KB * LANES
    return pl.pallas_call(
        body, name="qkv_backward", grid=(lp // tm,),
        in_specs=[_row_spec(tm, N_QB * LANES), _row_spec(tm, kw), _row_spec(tm, kw), _row_spec(tm, QKV_W),
                  _full_spec((1, LANES)), _full_spec((1, LANES)), _row_spec(tm, LANES), _row_spec(tm, LANES),
                  _full_spec((LANES, LANES))],
        out_specs=[_row_spec(tm, QKV_W), _full_spec((SUBLANES, LANES))],
        out_shape=[jax.ShapeDtypeStruct((lp, QKV_W), BF16), jax.ShapeDtypeStruct((SUBLANES, LANES), F32)],
        compiler_params=_params(("arbitrary",)),
    )(dqs, dk2, dv2, raw, gq, gk, cos, sin, e)


NEG = -1e30
Q_PER_KV = N_Q_HEADS // N_KV_HEADS


def _half_masks(x):
    lane = lax.broadcasted_iota(jnp.int32, x.shape, 1)
    zero = jnp.zeros_like(x)
    return jnp.where(lane < HEAD_DIM, x, zero), jnp.where(lane >= HEAD_DIM, x, zero)


ATTN_TR = 16


def _attn_tiles(lp):
    t = _pick_tile(lp, (1408, 256))
    return t, t


def _attn_tiles_bwd(lp):
    return _pick_tile(lp, (768, 256)), _pick_tile(lp, (1408, 256))


def attn_forward_t(qs, k2, v2t, n_valid, prefetch=()):
    lp = qs.shape[0]
    tq, kb = _attn_tiles(lp)
    nk = lp // kb
    gw = 2 * LANES
    nr = kb // ATTN_TR
    pad0 = n_valid - (nk - 1) * kb
    npf = len(prefetch)
    pf_specs, pf_shapes, pf_sems = _prefetch_specs(prefetch)
    nq = lp // tq

    def body(*refs):
        q_ref, k_ref, vt_ref = refs[:3]
        pf_ins = refs[3:3 + npf]
        o_ref, lse_ref = refs[3 + npf:5 + npf]
        pf_outs = refs[5 + npf:5 + 2 * npf]
        m_s, l_s, acc_s, s_s, p_s = refs[5 + 2 * npf:10 + 2 * npf]
        j = pl.program_id(2)
        first = (pl.program_id(0) == 0) & (pl.program_id(1) == 0) & (j == 0)
        last = (pl.program_id(0) == N_KV_HEADS - 1) & (pl.program_id(1) == nq - 1) & (j == nk - 1)
        _prefetch_run(first, last, pf_ins, pf_outs, refs[10 + 2 * npf:])

        @pl.when(j == 0)
        def _():
            m_s[...] = jnp.full_like(m_s, NEG)
            l_s[...] = jnp.zeros_like(l_s)
            acc_s[...] = jnp.zeros_like(acc_s)

        ks = _half_masks(k_ref[...])
        for pair in range(2):
            qp = q_ref[:, pair * LANES:(pair + 1) * LANES]
            for half in range(2):
                hh = 2 * pair + half
                s_s[...] = _nt(ks[half], qp)

                if pad0 < kb:
                    @pl.when(j == nk - 1)
                    def _():
                        s_s[pad0:, :] = jnp.full((kb - pad0, tq), NEG, F32)

                run = s_s[pl.ds(0, ATTN_TR), :]
                for r in range(1, nr):
                    run = jnp.maximum(run, s_s[pl.ds(r * ATTN_TR, ATTN_TR), :])
                m_prev = m_s[hh:hh + 1, :]
                m_new = jnp.maximum(m_prev, jnp.max(run, axis=0, keepdims=True))
                alpha = jnp.exp2(m_prev - m_new)
                m_s[hh:hh + 1, :] = m_new
                for r in range(nr):
                    rows = pl.ds(r * ATTN_TR, ATTN_TR)
                    p_s[rows, :] = jnp.exp2(s_s[rows, :] - m_new).astype(BF16)
                vt = jnp.concatenate([vt_ref[half * HEAD_DIM:(half + 1) * HEAD_DIM, :],
                                      jnp.ones((ATTN_TR, kb), BF16)], axis=0)
                pv = _nn(vt, p_s[...])
                l_s[hh:hh + 1, :] = alpha * l_s[hh:hh + 1, :] + pv[HEAD_DIM:HEAD_DIM + 1, :]
                pv = pv[:HEAD_DIM, :]
                rs = slice(half * HEAD_DIM, (half + 1) * HEAD_DIM)
                acc_s[pair, rs, :] = alpha * acc_s[pair, rs, :] + pv

        @pl.when(j == nk - 1)
        def _():
            for pair in range(2):
                for half in range(2):
                    hh = 2 * pair + half
                    rs = slice(half * HEAD_DIM, (half + 1) * HEAD_DIM)
                    acc_s[pair, rs, :] = acc_s[pair, rs, :] * (1.0 / l_s[hh:hh + 1, :])
                o_ref[:, pair * LANES:(pair + 1) * LANES] = acc_s[pair].T.astype(BF16)
            for hh in range(Q_PER_KV):
                lse_ref[0, hh] = m_s[hh:hh + 1, :] + jnp.log2(l_s[hh:hh + 1, :])

    return pl.pallas_call(
        body, name="attn_forward", grid=(N_KV_HEADS, lp // tq, nk),
        in_specs=[pl.BlockSpec((tq, gw), lambda g, i, j: (i, g)), pl.BlockSpec((kb, LANES), lambda g, i, j: (j, g)),
                  pl.BlockSpec((LANES, kb), lambda g, i, j: (g, j))] + pf_specs,
        out_specs=[pl.BlockSpec((tq, gw), lambda g, i, j: (i, g)),
                   pl.BlockSpec((1, Q_PER_KV, 1, tq), lambda g, i, j: (g, 0, 0, i))] + pf_specs,
        out_shape=[jax.ShapeDtypeStruct((lp, N_QB * LANES), BF16),
                   jax.ShapeDtypeStruct((N_KV_HEADS, Q_PER_KV, 1, lp), F32)] + pf_shapes,
        scratch_shapes=[pltpu.VMEM((SUBLANES, tq), F32), pltpu.VMEM((SUBLANES, tq), F32),
                        pltpu.VMEM((2, LANES, tq), F32), pltpu.VMEM((kb, tq), F32), pltpu.VMEM((kb, tq), BF16)] + pf_sems,
        compiler_params=_params(("arbitrary",) * 3 if npf else ("parallel", "parallel", "arbitrary")),
    )(qs, k2, v2t, *prefetch)


def attn_backward(qs, qst, k2, v2, do, dot, lse, delta, n_valid):
    lp = qs.shape[0]
    tq, kb = _attn_tiles_bwd(lp)
    nq, nk = lp // tq, lp // kb
    gw = 2 * LANES
    pad0 = n_valid - (nk - 1) * kb

    def body(q_ref, qt_ref, k_ref, v_ref, do_ref, dot_ref, lse_ref, dl_ref, dq_ref, dk_ref, dv_ref, acc_s, dkt_s, dvt_s):
        g = pl.program_id(0)
        i = pl.program_id(1)
        j = pl.program_id(2)
        cols = pl.ds(pl.multiple_of(j * kb, kb), kb)

        @pl.when(j == 0)
        def _():
            acc_s[...] = jnp.zeros_like(acc_s)

        @pl.when(i == 0)
        def _():
            dkt_s[:, cols] = jnp.zeros((LANES, kb), F32)
            dvt_s[:, cols] = jnp.zeros((LANES, kb), F32)

        head = lax.broadcasted_iota(jnp.int32, (tq, N_Q_HEADS), 1)

        def column(ref, hh):
            return jnp.sum(jnp.where(head == Q_PER_KV * g + hh, ref[...], 0.0), axis=1, keepdims=True)

        def step(masked):
            ks = _half_masks(k_ref[...])
            vs = _half_masks(v_ref[...])
            if masked:
                col = lax.broadcasted_iota(jnp.int32, (1, kb), 1)
                bias = jnp.where(col < pad0, 0.0, NEG)
            for pair in range(2):
                qp = q_ref[:, pair * LANES:(pair + 1) * LANES]
                dop = do_ref[:, pair * LANES:(pair + 1) * LANES]
                for half in range(2):
                    hh = 2 * pair + half
                    rs = slice(half * HEAD_DIM, (half + 1) * HEAD_DIM)
                    rt = slice(pair * LANES + half * HEAD_DIM, pair * LANES + (half + 1) * HEAD_DIM)
                    s = _nt(qp, ks[half])
                    if masked:
                        s = s + bias
                    p = jnp.exp2(s - column(lse_ref, hh))
                    dp = _nt(dop, vs[half])
                    ds = (p * (dp - column(dl_ref, hh))).astype(BF16)
                    pb = p.astype(BF16)
                    acc_s[pair] += _nn(ds, ks[half])
                    dvt_s[rs, cols] += _nn(dot_ref[rt, :], pb)
                    dkt_s[rs, cols] += _nn(qt_ref[rt, :], ds)

        if pad0 < kb:
            pl.when(j < nk - 1)(lambda: step(False))
            pl.when(j == nk - 1)(lambda: step(True))
        else:
            step(False)

        @pl.when(j == nk - 1)
        def _():
            for pair in range(2):
                dq_ref[:, pair * LANES:(pair + 1) * LANES] = acc_s[pair]

        @pl.when(i == nq - 1)
        def _():
            dk_ref[cols, :] = dkt_s[:, cols].T
            dv_ref[cols, :] = dvt_s[:, cols].T

    cspec = pl.BlockSpec((tq, N_Q_HEADS), lambda g, i, j: (i, 0))
    qspec = pl.BlockSpec((tq, gw), lambda g, i, j: (i, g))
    tspec = pl.BlockSpec((gw, tq), lambda g, i, j: (g, i))
    kspec = pl.BlockSpec((kb, LANES), lambda g, i, j: (j, g))
    gspec = pl.BlockSpec((lp, LANES), lambda g, i, j: (0, g))
    return pl.pallas_call(
        body, name="attn_backward", grid=(N_KV_HEADS, nq, nk),
        in_specs=[qspec, tspec, kspec, kspec, qspec, tspec, cspec, cspec],
        out_specs=[qspec, gspec, gspec],
        out_shape=[jax.ShapeDtypeStruct((lp, N_QB * LANES), F32),
                   jax.ShapeDtypeStruct((lp, N_KB * LANES), F32), jax.ShapeDtypeStruct((lp, N_KB * LANES), F32)],
        scratch_shapes=[pltpu.VMEM((2, tq, LANES), F32), pltpu.VMEM((LANES, lp), F32), pltpu.VMEM((LANES, lp), F32)],
        compiler_params=_params(("parallel", "arbitrary", "arbitrary")),
    )(qs, qst, k2, v2, do, dot, lse, delta)


def attn_out_backward(dr, wout, o, e16):
    lp, d = dr.shape
    tm = _tm(lp)

    def body(dr_ref, w_ref, o_ref, e_ref, do_ref, dl_ref, dot_ref):
        do32 = _nt(dr_ref[...].astype(BF16), w_ref[...])
        do = do32.astype(BF16)
        do_ref[...] = do
        for cb in range(d // LANES):
            dot_ref[cb * LANES:(cb + 1) * LANES, :] = do32[:, cb * LANES:(cb + 1) * LANES].T.astype(BF16)
        dl_ref[...] = _segsum(do.astype(F32) * o_ref[...].astype(F32), e_ref[...])

    rs = _row_spec(tm, d)
    return pl.pallas_call(
        body, name="attn_out_backward", grid=(lp // tm,),
        in_specs=[rs, _full_spec((d, d)), rs, _full_spec((d, N_Q_HEADS))],
        out_specs=[rs, _row_spec(tm, N_Q_HEADS), pl.BlockSpec((d, tm), lambda i: (0, i))],
        out_shape=[jax.ShapeDtypeStruct((lp, d), BF16), jax.ShapeDtypeStruct((lp, N_Q_HEADS), F32),
                   jax.ShapeDtypeStruct((d, lp), BF16)],
        compiler_params=_params(("parallel",)),
    )(dr, wout, o, e16)


N_CHIPS = 4


def _mesh_pos():
    return lax.axis_index("x"), lax.axis_index("y"), lax.axis_index("c")


def chip_exchange(arrs, scatter, name):
    n = len(arrs)
    hbm = pl.BlockSpec(memory_space=pl.ANY)

    def body(*refs):
        ins, outs = refs[:n], refs[n:2 * n]
        send_sems, recv_sems, loc_sems = refs[2 * n:]
        x, y, c = _mesh_pos()
        me = 2 * x + y
        chips = [(1 - x, y), (x, 1 - y), (1 - x, 1 - y)]
        started = []
        for a in range(n):
            loc = pltpu.make_async_copy(ins[a].at[me] if scatter else ins[a], outs[a].at[me], loc_sems.at[a])
            loc.start()
            started.append(loc)
            for k, (px, py) in enumerate(chips):
                src = ins[a].at[2 * px + py] if scatter else ins[a]
                cp = pltpu.make_async_remote_copy(
                    src_ref=src, dst_ref=outs[a].at[me], send_sem=send_sems.at[3 * a + k], recv_sem=recv_sems.at[3 * a + k],
                    device_id=(px, py, c), device_id_type=MESH)
                cp.start()
                started.append(cp)
        for cp in started:
            cp.wait()

    out_shape = [jax.ShapeDtypeStruct(a.shape if scatter else (N_CHIPS,) + a.shape, a.dtype) for a in arrs]
    return pl.pallas_call(
        body, name=name, in_specs=[hbm] * n, out_specs=[hbm] * n, out_shape=out_shape,
        scratch_shapes=[pltpu.SemaphoreType.DMA((3 * n,)), pltpu.SemaphoreType.DMA((3 * n,)), pltpu.SemaphoreType.DMA((n,))],
    )(*arrs)


def _same_core_copies(ins, outs, send_sems, recv_sems, loc_sems):
    x, y, c = _mesh_pos()
    me = 2 * x + y
    chips = [(1 - x, y), (x, 1 - y), (1 - x, 1 - y)]
    cps = []
    for a in range(len(ins)):
        cps.append(pltpu.make_async_copy(ins[a], outs[a].at[me], loc_sems.at[a]))
        for k, (px, py) in enumerate(chips):
            cps.append(pltpu.make_async_remote_copy(
                src_ref=ins[a], dst_ref=outs[a].at[me], send_sem=send_sems.at[3 * a + k], recv_sem=recv_sems.at[3 * a + k],
                device_id=(px, py, c), device_id_type=MESH))
    return cps


def _prefetch_specs(arrs):
    n = len(arrs)
    hbm = pl.BlockSpec(memory_space=pl.ANY)
    shapes = [jax.ShapeDtypeStruct((N_CHIPS,) + a.shape, a.dtype) for a in arrs]
    sems = [pltpu.SemaphoreType.DMA((3 * n,)), pltpu.SemaphoreType.DMA((3 * n,)), pltpu.SemaphoreType.DMA((n,))] if n else []
    return [hbm] * n, shapes, sems


def _prefetch_run(first, last, ins, outs, sems):
    if not ins:
        return

    @pl.when(first)
    def _():
        for cp in _same_core_copies(ins, outs, *sems):
            cp.start()

    @pl.when(last)
    def _():
        for cp in _same_core_copies(ins, outs, *sems):
            cp.wait()


def gather_two_level(arrs, name):
    n = len(arrs)
    hbm = pl.BlockSpec(memory_space=pl.ANY)

    def body(*refs):
        ins, outs = refs[:n], refs[n:2 * n]
        ici_send, ici_recv, d2d_send, d2d_recv, loc_sems = refs[2 * n:]
        x, y, c = _mesh_pos()
        me = 2 * x + y
        chips = [(1 - x, y), (x, 1 - y), (1 - x, 1 - y)]
        started = []
        for a in range(n):
            hn = arrs[a].shape[0] // 2
            mine = pl.ds(c * hn, hn)
            loc = pltpu.make_async_copy(ins[a], outs[a].at[me], loc_sems.at[a])
            loc.start()
            started.append(loc)
            first = []
            for k, (px, py) in enumerate(chips):
                cp = pltpu.make_async_remote_copy(
                    src_ref=ins[a].at[mine], dst_ref=outs[a].at[me, mine], send_sem=ici_send.at[3 * a + k],
                    recv_sem=ici_recv.at[3 * a + k], device_id=(px, py, c), device_id_type=MESH)
                cp.start()
                first.append(cp)
            for k, (px, py) in enumerate(chips):
                q = 2 * px + py
                first[k].wait_recv()
                fw = pltpu.make_async_remote_copy(
                    src_ref=outs[a].at[q, mine], dst_ref=outs[a].at[q, mine], send_sem=d2d_send.at[3 * a + k],
                    recv_sem=d2d_recv.at[3 * a + k], device_id=(x, y, 1 - c), device_id_type=MESH)
                fw.start()
                started.append(fw)
            for cp in first:
                cp.wait_send()
        for cp in started:
            cp.wait()

    out_shape = [jax.ShapeDtypeStruct((N_CHIPS,) + a.shape, a.dtype) for a in arrs]
    return pl.pallas_call(
        body, name=name, in_specs=[hbm] * n, out_specs=[hbm] * n, out_shape=out_shape,
        scratch_shapes=[pltpu.SemaphoreType.DMA((3 * n,))] * 4 + [pltpu.SemaphoreType.DMA((n,))],
    )(*arrs)


def sibling_exchange(arrs, name):
    n = len(arrs)
    hbm = pl.BlockSpec(memory_space=pl.ANY)

    def body(*refs):
        ins, outs = refs[:n], refs[n:2 * n]
        send_sems, recv_sems = refs[2 * n:]
        x, y, c = _mesh_pos()
        started = []
        for a in range(n):
            cp = pltpu.make_async_remote_copy(
                src_ref=ins[a], dst_ref=outs[a], send_sem=send_sems.at[a], recv_sem=recv_sems.at[a],
                device_id=(x, y, 1 - c), device_id_type=MESH)
            cp.start()
            started.append(cp)
        for cp in started:
            cp.wait()

    return pl.pallas_call(
        body, name=name, in_specs=[hbm] * n, out_specs=[hbm] * n,
        out_shape=[jax.ShapeDtypeStruct(a.shape, a.dtype) for a in arrs],
        scratch_shapes=[pltpu.SemaphoreType.DMA((n,)), pltpu.SemaphoreType.DMA((n,))],
    )(*arrs)


def _rows_tile(r, c):
    return _pick_tile(r, tuple(t for t in (512, 256, 128, 64, 32, 16, 8) if t * c * 4 <= 2 * 1024 * 1024))


def chip_sum(recv, name):
    _, r, c = recv.shape
    tr = _rows_tile(r, c)

    def body(r_ref, o_ref):
        acc = r_ref[0].astype(F32)
        for q in range(1, N_CHIPS):
            acc = acc + r_ref[q].astype(F32)
        o_ref[...] = acc

    return pl.pallas_call(
        body, name=name, grid=(r // tr,),
        in_specs=[pl.BlockSpec((N_CHIPS, tr, c), lambda i: (0, i, 0))],
        out_specs=pl.BlockSpec((tr, c), lambda i: (i, 0)),
        out_shape=jax.ShapeDtypeStruct((r, c), F32),
        compiler_params=_params(("parallel",)),
    )(recv)


def pair_sum(part, sib, name, dtype=F32):
    r, c = part.shape
    tr = _rows_tile(r, c)

    def body(p_ref, s_ref, o_ref):
        o_ref[...] = (p_ref[...].astype(F32) + s_ref[...].astype(F32)).astype(dtype)

    rs = pl.BlockSpec((tr, c), lambda i: (i, 0))
    return pl.pallas_call(
        body, name=name, grid=(r // tr,), in_specs=[rs] * 2, out_specs=rs,
        out_shape=jax.ShapeDtypeStruct((r, c), dtype), compiler_params=_params(("parallel",)),
    )(part, sib)


def adamw(g, w, m, v, name):
    r, c = w.shape
    tr = _rows_tile(r, c)

    def body(g_ref, w_ref, m_ref, v_ref, d_ref, nm_ref, nv_ref):
        g_ = g_ref[...]
        m_ = ADAM_B1 * m_ref[...] + (1.0 - ADAM_B1) * g_
        v_ = ADAM_B2 * v_ref[...] + (1.0 - ADAM_B2) * (g_ * g_)
        m_hat = m_ / (1.0 - ADAM_B1 ** ADAM_STEP)
        v_hat = v_ / (1.0 - ADAM_B2 ** ADAM_STEP)
        d_ref[...] = -ADAM_LR * (m_hat / (jnp.sqrt(v_hat) + ADAM_EPS) + ADAM_WD * w_ref[...])
        nm_ref[...] = m_
        nv_ref[...] = v_

    rs = pl.BlockSpec((tr, c), lambda i: (i, 0))
    return pl.pallas_call(
        body, name=name, grid=(r // tr,), in_specs=[rs] * 4, out_specs=[rs] * 3,
        out_shape=[jax.ShapeDtypeStruct((r, c), F32)] * 3,
        compiler_params=_params(("parallel",)),
    )(g, w, m, v)


def adamw_halves(mine, other, core, w, m, v, name):
    r, c = w.shape
    tr = _rows_tile(r // 2, c)
    th = (r // 2) // tr

    def body(core_ref, a_ref, b_ref, w_ref, m_ref, v_ref, g_ref, d_ref, nm_ref, nv_ref):
        g_ = jnp.where(pl.program_id(0) // th == core_ref[0], a_ref[...], b_ref[...])
        m_ = ADAM_B1 * m_ref[...] + (1.0 - ADAM_B1) * g_
        v_ = ADAM_B2 * v_ref[...] + (1.0 - ADAM_B2) * (g_ * g_)
        m_hat = m_ / (1.0 - ADAM_B1 ** ADAM_STEP)
        v_hat = v_ / (1.0 - ADAM_B2 ** ADAM_STEP)
        g_ref[...] = g_
        d_ref[...] = -ADAM_LR * (m_hat / (jnp.sqrt(v_hat) + ADAM_EPS) + ADAM_WD * w_ref[...])
        nm_ref[...] = m_
        nv_ref[...] = v_

    half = pl.BlockSpec((tr, c), lambda i, core_ref: (i % th, 0))
    rows = pl.BlockSpec((tr, c), lambda i, core_ref: (i, 0))
    return pl.pallas_call(
        body, name=name,
        grid_spec=pltpu.PrefetchScalarGridSpec(num_scalar_prefetch=1, grid=(r // tr,), in_specs=[half, half, rows, rows, rows],
                                               out_specs=[rows] * 4),
        out_shape=[jax.ShapeDtypeStruct((r, c), F32)] * 4,
        compiler_params=_params(("parallel",)),
    )(core, mine, other, w, m, v)


WEIGHTS = ['meta_tokens', 's5_lambda_re', 's5_lambda_im', 's5_log_dt', 's5_b_re', 's5_b_im', 's5_c_re', 's5_c_im', 's5_d',
           's5_w_glu', 's5_w_out', 'attn_w_qkv', 'attn_q_gain', 'attn_k_gain', 'attn_w_out', 'ffn_w_gate', 'ffn_w_up',
           'ffn_w_down', 'ln_gain', 'ln_bias']
BIG = ['s5_w_glu', 's5_w_out', 'attn_w_qkv', 'attn_w_out', 'ffn_w_gate', 'ffn_w_up', 'ffn_w_down']
ROW_SHARDED = {'s5_w_glu', 's5_w_out', 'attn_w_out', 'ffn_w_down'}
SMALL_SHARDED = ['meta_tokens', 'ln_gain', 'ln_bias']
REPLICATED = ['s5_lambda_re', 's5_lambda_im', 's5_log_dt', 's5_b_re', 's5_b_im', 's5_c_re', 's5_c_im', 's5_d',
              'attn_q_gain', 'attn_k_gain']
REP_ALIGN = N_CHIPS * LANES * LANES


def _natural(gathered, row_sharded):
    p, n, a, b = gathered.shape
    if row_sharded:
        return jnp.transpose(gathered, (1, 0, 2, 3)).reshape(n, p * a, b)
    return jnp.transpose(gathered, (1, 2, 0, 3)).reshape(n, a, p * b)


def _shard_major(full, row_sharded):
    n, a, b = full.shape
    if row_sharded:
        return jnp.transpose(full.reshape(n, N_CHIPS, a // N_CHIPS, b), (1, 0, 2, 3))
    return jnp.transpose(full.reshape(n, a, N_CHIPS, b // N_CHIPS), (2, 0, 1, 3))


def _dup_heads(w):
    lead = w.shape[:-1]
    w = w.reshape(lead + (N_KV_HEADS, 1, HEAD_DIM))
    return jnp.broadcast_to(w, lead + (N_KV_HEADS, 2, HEAD_DIM)).reshape(lead + (N_KV_HEADS * 2 * HEAD_DIM,))


def _fold_heads(d):
    lead = d.shape[:-1]
    return d.reshape(lead + (N_KV_HEADS, 2, HEAD_DIM)).sum(axis=-2).reshape(lead + (N_KV_HEADS * HEAD_DIM,))


def _pack_rep(tree, extra=None):
    extra = jnp.zeros((1,), F32) if extra is None else extra.reshape(1)
    flat = jnp.concatenate([tree[n].reshape(-1) for n in REPLICATED] + [extra])
    pad = _round_up(flat.shape[0], REP_ALIGN) - flat.shape[0]
    return jnp.pad(flat, (0, pad))


def _unpack_rep(flat, like):
    out, off = {}, 0
    for n in REPLICATED:
        size = math.prod(like[n].shape)
        out[n] = flat[off:off + size].reshape(like[n].shape)
        off += size
    return out


def _train_step(x, loss_target, w, mom, vel):
    s = x.shape[1]
    n_valid = N_META + s
    lp = _round_up(n_valid, 2 * LANES)
    nq = N_Q_HEADS * HEAD_DIM
    nkv = N_KV_HEADS * HEAD_DIM

    small = jnp.concatenate([w[n].reshape(-1, w[n].shape[-1]) for n in SMALL_SHARDED], axis=0)
    shard = {n: w[n].astype(BF16) for n in BIG}
    uses = [[('s5_w_glu', 0), ('s5_w_out', 0), ('ffn_w_gate', 0), ('ffn_w_up', 0), ('ffn_w_down', 0)],
            [('attn_w_qkv', 0), ('attn_w_out', 0), ('ffn_w_gate', 1), ('ffn_w_up', 1), ('ffn_w_down', 1)],
            [('s5_w_glu', 1), ('s5_w_out', 1), ('ffn_w_gate', 2), ('ffn_w_up', 2), ('ffn_w_down', 2),
             ('attn_w_qkv', 1), ('attn_w_out', 1), ('ffn_w_gate', 3), ('ffn_w_up', 3), ('ffn_w_down', 3)]]
    full = {}

    def unpack(stage, gathered):
        for (n, l), g in zip(uses[stage], gathered):
            full[(n, l)] = _natural(g[:, None], n in ROW_SHARDED)[0]

    first = gather_two_level([shard[n][l] for n, l in uses[0]] + [small], "gather_weights")
    unpack(0, first[:-1])
    small_full = jnp.transpose(first[-1], (1, 0, 2)).reshape(small.shape[0], D_MODEL)
    meta_full = small_full[:N_META]
    ln_gain = small_full[N_META:N_META + 2 * DEPTH].reshape(DEPTH, 2, 1, D_MODEL)
    ln_bias = small_full[N_META + 2 * DEPTH:].reshape(DEPTH, 2, 1, D_MODEL)

    def qkv_dup(wqkv):
        return jnp.concatenate([wqkv[..., :nq], _dup_heads(wqkv[..., nq:nq + nkv]), _dup_heads(wqkv[..., nq + nkv:])], axis=-1)

    w2 = {}

    cos, sin = rope_tables(lp, n_valid)
    e128 = head_sum_matrix()
    e16 = jnp.kron(jnp.eye(N_Q_HEADS, dtype=F32), jnp.ones((HEAD_DIM, 1), F32)).astype(BF16)
    gq = jnp.tile(w['attn_q_gain'], (1, 2))[:, None, :]
    gk = jnp.tile(w['attn_k_gain'], (1, 2))[:, None, :]

    pad_rows = jnp.zeros((lp - n_valid, D_MODEL), F32)
    h = jnp.concatenate([meta_full, x[0], pad_rows], axis=0)
    tgt = jnp.concatenate([jnp.zeros((N_META, D_MODEL), F32), loss_target[0], pad_rows], axis=0)

    saved = []
    s5_names = ['s5_lambda_re', 's5_lambda_im', 's5_log_dt', 's5_b_re', 's5_b_im', 's5_c_re', 's5_c_im']
    for i in range(DEPTH):
        j = i // 2
        sv = {'h': h}
        if i % 2 == 0:
            ops, sv['prep_vjp'] = jax.vjp(s5_prep, *[w[n][j] for n in s5_names])
            m_, wx_, ci_, at_ = ops
            two = lambda t: t.reshape((2 * S5_NJ,) + t.shape[2:])
            sv['ops'] = (blockdiag_expand(m_, S5_CH, S5_CH, "s5_expand_m"),
                         blockdiag_expand(two(wx_), S5_CH, S5_STATE, "s5_expand_wx").reshape(2, S5_NJ, S5_W, S5_W),
                         blockdiag_expand(two(ci_), S5_STATE, S5_CH, "s5_expand_ci").reshape(2, S5_NJ, S5_W, S5_W), at_)
            pf = [shard[n][l] for n, l in uses[1]] if i == 0 else []
            y, sv['lhs'], sv['sp'], sv['sn'], *got = s5_forward(h, *sv['ops'], n_valid, prefetch=pf)
            if i == 0:
                unpack(1, got)
            sv['v'], sv['t'], sv['g'], sv['z'] = glu_forward(y, h, w['s5_d'][j][None], full['s5_w_glu', j])
            sv['r1'], h1 = proj_ln_forward(sv['z'], full['s5_w_out', j], h, ln_gain[i, 0], ln_bias[i, 0], "s5_out_ln")
        else:
            w2[j] = qkv_dup(full['attn_w_qkv', j])
            sv['raw'], sv['qs'], sv['k2'], sv['v2'], sv['qst'], v2t = qkv_forward(h, w2[j], gq[j], gk[j], cos, sin, e128)
            pf = [shard[n][l] for n, l in uses[2]] if i == 1 else []
            sv['o'], lse, *got = attn_forward_t(sv['qs'], sv['k2'], v2t, n_valid, prefetch=pf)
            if i == 1:
                unpack(2, got)
            sv['lse'] = lse.reshape(N_Q_HEADS, lp).T
            sv['r1'], h1 = proj_ln_forward(sv['o'], full['attn_w_out', j], h, ln_gain[i, 0], ln_bias[i, 0], "attn_out_ln")
        sv['h1'] = h1
        sv['a'], sv['b'], sv['f'] = ffn_up_forward(h1, full['ffn_w_gate', i], full['ffn_w_up', i])
        sv['r2'], h = proj_ln_forward(sv['f'], full['ffn_w_down', i], h1, ln_gain[i, 1], ln_bias[i, 1], "ffn_down_ln")
        saved.append(sv)

    dh, loss_part = loss_backward(h, tgt, n_valid)
    loss_local = jnp.sum(loss_part)

    gfull = {n: [None] * w[n].shape[0] for n in BIG}
    d_ln_gain = [[None, None] for _ in range(DEPTH)]
    d_ln_bias = [[None, None] for _ in range(DEPTH)]
    grep = {n: [None] * w[n].shape[0] for n in REPLICATED}
    pending = None
    for i in reversed(range(DEPTH)):
        j = i // 2
        sv = saved[i]
        if pending is None:
            dr2, s2 = ln_backward(dh, sv['r2'], ln_gain[i, 1])
        else:
            dr2, s2 = pending
            pending = None
        d_ln_gain[i][1], d_ln_bias[i][1] = s2[0], s2[1]
        da, db = ffn_backward_act(dr2, full['ffn_w_down', i], sv['a'], sv['b'])
        gfull['ffn_w_down'][i] = mm_tn(sv['f'], dr2, "grad_ffn_down")
        dr1, s1 = resid_nt(dr2, [da, db], [full['ffn_w_gate', i], full['ffn_w_up', i]], "ffn_backward_x",
                           ln=(sv['r1'], ln_gain[i, 0]))
        gfull['ffn_w_gate'][i] = mm_tn(sv['h1'], da, "grad_ffn_gate")
        gfull['ffn_w_up'][i] = mm_tn(sv['h1'], db, "grad_ffn_up")
        d_ln_gain[i][0], d_ln_bias[i][0] = s1[0], s1[1]
        if i % 2 == 0:
            dt, dgd = glu_backward1(dr1, full['s5_w_out', j], sv['g'], sv['t'])
            gfull['s5_w_out'][j] = mm_tn(sv['z'], dr1, "grad_s5_out")
            dv, dhs, sd = glu_backward2(dt, dgd, full['s5_w_glu', j], sv['v'], sv['h'], w['s5_d'][j][None], dr1)
            grep['s5_d'][j] = sd[0]
            gfull['s5_w_glu'][j] = mm_tn(sv['g'], dt, "grad_s5_glu")
            dh, ldy, dxf, dxr, daf, dar = s5_backward(dv, dhs, *sv['ops'], sv['sp'], sv['sn'], n_valid)
            dm = bmm_tn_compact(sv['lhs'], ldy, S5_CH, S5_CH, "grad_s5_m")
            dwx = jnp.stack([bmm_tn_compact(sv['lhs'], dxf, S5_CH, S5_STATE, "grad_s5_wxf"),
                             bmm_tn_compact(sv['lhs'], dxr, S5_CH, S5_STATE, "grad_s5_wxr")])
            dci = jnp.stack([bmm_tn_compact(sv['sp'], ldy, S5_STATE, S5_CH, "grad_s5_cif"),
                             bmm_tn_compact(sv['sn'], ldy, S5_STATE, S5_CH, "grad_s5_cir")])
            dps = sv['prep_vjp']((dm, dwx, dci, jnp.stack([daf, dar])))
            for n, g in zip(s5_names, dps):
                grep[n][j] = g
        else:
            do, delta, dot = attn_out_backward(dr1, full['attn_w_out', j], sv['o'], e16)
            gfull['attn_w_out'][j] = mm_tn(sv['o'], dr1, "grad_attn_out")
            dq, dk2, dv2 = attn_backward(sv['qs'], sv['qst'], sv['k2'], sv['v2'], do, dot, sv['lse'], delta, n_valid)
            draw, gs = qkv_backward(dq, dk2, dv2, sv['raw'], gq[j], gk[j], cos, sin, e128)
            grep['attn_q_gain'][j] = gs[0, :HEAD_DIM] + gs[0, HEAD_DIM:]
            grep['attn_k_gain'][j] = gs[1, :HEAD_DIM] + gs[1, HEAD_DIM:]
            pending = resid_nt(dr1, [draw], [w2[j]], "attn_backward_x", ln=(saved[i - 1]['r2'], ln_gain[i - 1, 1]))
            dw2 = mm_tn(sv['h'], draw, "grad_attn_qkv")
            kq = N_QB * LANES
            kk = N_KB * LANES
            gfull['attn_w_qkv'][j] = jnp.concatenate(
                [dw2[:, :kq], _fold_heads(dw2[:, kq:kq + kk]), _fold_heads(dw2[:, kq + kk:])], axis=1)
    grad_x = dh[N_META:n_valid][None]

    core = lax.axis_index("c")
    contrib = [_shard_major(jnp.stack(gfull[n]), n in ROW_SHARDED) for n in BIG]
    small_g = jnp.concatenate([dh[:N_META], jnp.stack([g for pair in d_ln_gain for g in pair]),
                               jnp.stack([g for pair in d_ln_bias for g in pair])], axis=0)
    contrib.append(jnp.transpose(small_g.reshape(-1, N_CHIPS, D_MODEL // N_CHIPS), (1, 0, 2)))
    rep_g = _pack_rep({n: jnp.stack(grep[n]) for n in REPLICATED}, loss_local)
    contrib.append(rep_g.reshape(N_CHIPS, -1, LANES))
    names = BIG + ['small', 'rep']
    wire = [BF16] * len(BIG) + [F32, F32]
    keep, give = [], []
    for t, dt in zip(contrib, wire):
        hn = t.shape[1] // 2
        keep.append(lax.dynamic_slice_in_dim(t, core * hn, hn, axis=1))
        give.append(lax.dynamic_slice_in_dim(t, (1 - core) * hn, hn, axis=1).astype(dt))
    got = sibling_exchange(give, "sibling_contrib")
    two_d = lambda t: t.reshape(-1, t.shape[-1])
    pair = [pair_sum(two_d(a), two_d(b), "pair_sum_" + n, dt).reshape(a.shape)
            for n, a, b, dt in zip(names, keep, got, wire)]
    recv = chip_exchange(pair, True, "scatter_grads")
    halves = [chip_sum(r.reshape(N_CHIPS, -1, r.shape[-1]), "chip_sum_" + n) for n, r in zip(names, recv)]
    others = sibling_exchange(halves, "sibling_halves")
    core_arr = jnp.reshape(core, (1,)).astype(jnp.int32)

    out = {}

    def update_halves(n, a, b, wn, mn, vn):
        shape = wn.shape
        flat = (-1, shape[-1])
        res = adamw_halves(a, b, core_arr, wn.reshape(flat), mn.reshape(flat), vn.reshape(flat), "adamw_" + n)
        return tuple(t.reshape(shape) for t in res)

    for n, a, b in zip(BIG, halves, others):
        out[n] = update_halves(n, a, b, w[n], mom[n], vel[n])
    cat = lambda tree: jnp.concatenate([tree[n].reshape(-1, tree[n].shape[-1]) for n in SMALL_SHARDED], axis=0)
    sm = update_halves("small", halves[-2], others[-2], cat(w), cat(mom), cat(vel))
    off = 0
    for n in SMALL_SHARDED:
        rows = math.prod(w[n].shape[:-1])
        out[n] = tuple(t[off:off + rows].reshape(w[n].shape) for t in sm)
        off += rows
    rep_quarter = jnp.where(core == 0, jnp.concatenate([halves[-1], others[-1]], axis=0),
                            jnp.concatenate([others[-1], halves[-1]], axis=0))

    def update(n, g, wn, mn, vn):
        shape = wn.shape
        flat = (-1, shape[-1])
        d, nm, nv = adamw(g, wn.reshape(flat), mn.reshape(flat), vn.reshape(flat), "adamw_" + n)
        return tuple(t.reshape(shape) for t in (g, d, nm, nv))

    rep_all = chip_exchange([rep_quarter], False, "gather_rep")[0].reshape(-1, LANES)
    rp = update("rep", rep_all, _pack_rep(w).reshape(-1, LANES), _pack_rep(mom).reshape(-1, LANES),
                _pack_rep(vel).reshape(-1, LANES))
    unpacked = [_unpack_rep(t.reshape(-1), w) for t in rp]
    loss = rp[0].reshape(-1)[sum(math.prod(w[n].shape) for n in REPLICATED)]
    for n in REPLICATED:
        out[n] = tuple(u[n] for u in unpacked)

    return (loss, grad_x, *[out[n][0] for n in WEIGHTS], *[out[n][1] for n in WEIGHTS],
            *[out[n][2] for n in WEIGHTS], *[out[n][3] for n in WEIGHTS])


def kernel(x, meta_tokens, s5_lambda_re, s5_lambda_im, s5_log_dt, s5_b_re, s5_b_im, s5_c_re, s5_c_im, s5_d, s5_w_glu, s5_w_out, attn_w_qkv, attn_q_gain, attn_k_gain, attn_w_out, ffn_w_gate, ffn_w_up, ffn_w_down, ln_gain, ln_bias, loss_target, m_meta_tokens, m_s5_lambda_re, m_s5_lambda_im, m_s5_log_dt, m_s5_b_re, m_s5_b_im, m_s5_c_re, m_s5_c_im, m_s5_d, m_s5_w_glu, m_s5_w_out, m_attn_w_qkv, m_attn_q_gain, m_attn_k_gain, m_attn_w_out, m_ffn_w_gate, m_ffn_w_up, m_ffn_w_down, m_ln_gain, m_ln_bias, v_meta_tokens, v_s5_lambda_re, v_s5_lambda_im, v_s5_log_dt, v_s5_b_re, v_s5_b_im, v_s5_c_re, v_s5_c_im, v_s5_d, v_s5_w_glu, v_s5_w_out, v_attn_w_qkv, v_attn_q_gain, v_attn_k_gain, v_attn_w_out, v_ffn_w_gate, v_ffn_w_up, v_ffn_w_down, v_ln_gain, v_ln_bias):
    given = locals()
    w = {n: given[n] for n in WEIGHTS}
    mom = {n: given["m_" + n] for n in WEIGHTS}
    vel = {n: given["v_" + n] for n in WEIGHTS}
    return _train_step(x, loss_target, w, mom, vel)
```

```python
import math

import jax
import jax.numpy as jnp
from jax import lax
from jax.experimental import pallas as pl
from jax.experimental.pallas import tpu as pltpu

F32 = jnp.float32
BF16 = jnp.bfloat16
MESH = pl.DeviceIdType.MESH

D_MODEL = 1024
N_META = 16
GRID_W = 64
HEAD_DIM = 64
N_Q_HEADS = 16
N_KV_HEADS = 4
ROPE_THETA = 10000.0
QK_EPS = 1e-6
S5_CH = 16
S5_GROUPS = 64
S5_STATE = 64
D_FF = 2816
LN_EPS = 1e-5
DEPTH = 4
ALPHA = (2.0 * DEPTH) ** 0.25
ADAM_LR, ADAM_B1, ADAM_B2, ADAM_EPS, ADAM_WD, ADAM_STEP = 0.001, 0.9, 0.999, 1e-08, 0.01, 10

LANES = 128
SUBLANES = 8
VMEM_LIMIT = 56 * 1024 * 1024

S5_T = 8
S5_GB = LANES // S5_CH
S5_NJ = S5_GROUPS // S5_GB
S5_W = S5_T * LANES
S5_SW = 2 * S5_GB * S5_STATE
S5_HALF = S5_SW // 2


def _round_up(a, b):
    return -(-a // b) * b


def _pick_tile(n, prefs):
    for t in prefs:
        if n % t == 0:
            return t
    return n


def _params(sem=None):
    kw = dict(vmem_limit_bytes=VMEM_LIMIT)
    if sem is not None:
        kw["dimension_semantics"] = sem
    return pltpu.CompilerParams(**kw)


def _dot(a, b, dims):
    return lax.dot_general(a, b, (dims, ((), ())), preferred_element_type=F32)


def _nn(a, b):
    return _dot(a, b, ((1,), (0,)))


def _nt(a, b):
    return _dot(a, b, ((1,), (1,)))


def _tn(a, b):
    return _dot(a, b, ((0,), (0,)))


def _compact(w):
    g, a0, a1, b0, b1 = w.shape
    w = jnp.transpose(w.reshape(S5_NJ, S5_GB, a0, a1, b0, b1), (0, 2, 1, 3, 4, 5))
    return w.reshape(S5_NJ, a0 * S5_GB * a1, b0 * b1)


def s5_prep(lam_re, lam_im, log_dt, b_re, b_im, c_re, c_im):
    hi = lax.Precision.HIGHEST
    t = S5_T
    dt = jnp.exp(log_dt)[..., None]
    taus = jnp.arange(t + 1, dtype=F32)[:, None, None, None]
    mag = jnp.exp(lam_re * dt)
    ang = lam_im * dt
    pr = jnp.concatenate([jnp.ones_like(mag)[None], (mag * jnp.cos(ang))[None],
                          jnp.exp(lam_re * dt * taus[2:]) * jnp.cos(ang * taus[2:])], axis=0)
    pi = jnp.concatenate([jnp.zeros_like(mag)[None], (mag * jnp.sin(ang))[None],
                          jnp.exp(lam_re * dt * taus[2:]) * jnp.sin(ang * taus[2:])], axis=0)
    abr, abi = pr[1], pi[1]
    nr, ni = abr - 1.0, abi
    den = lam_re * lam_re + lam_im * lam_im
    cr = (nr * lam_re + ni * lam_im) / den
    ci_ = (ni * lam_re - nr * lam_im) / den
    bbr = cr[..., None] * b_re - ci_[..., None] * b_im
    bbi = cr[..., None] * b_im + ci_[..., None] * b_re
    er = c_re[None] * pr[:, :, :, None, :] - c_im[None] * pi[:, :, :, None, :]
    ei = c_re[None] * pi[:, :, :, None, :] + c_im[None] * pr[:, :, :, None, :]
    nd, ng, ch = er.shape[1], er.shape[2], er.shape[3]
    lhs = jnp.concatenate([er[:t], -ei[:t]], axis=-1)
    lhs = jnp.transpose(lhs, (1, 2, 0, 3, 4)).reshape(nd, ng, t * ch, 2 * S5_STATE)
    rhs = jnp.concatenate([bbr, bbi], axis=-2)
    kk = jnp.einsum("dgmp,dgpc->dgmc", lhs, rhs, precision=hi)
    kk = jnp.transpose(kk.reshape(nd, ng, t, ch, ch), (2, 0, 1, 3, 4))
    zero = jnp.zeros_like(kk[0, 0])
    mg = jnp.stack([jnp.stack([(kk[i - s, 0] if i > s else zero) + (kk[s - i, 1] if s > i else zero)
                               + ((kk[0, 0] + kk[0, 1]) if i == s else zero) for i in range(t)])
                    for s in range(t)])
    mg = jnp.transpose(mg, (2, 0, 4, 1, 3))
    m = _compact(mg)
    pw_f = jnp.stack([pr[t - 1 - s, 0] for s in range(t)]), jnp.stack([pi[t - 1 - s, 0] for s in range(t)])
    pw_r = jnp.stack([pr[s, 1] for s in range(t)]), jnp.stack([pi[s, 1] for s in range(t)])
    wx = []
    for d, (qr, qi) in enumerate((pw_f, pw_r)):
        wr = qr[..., None] * bbr[d][None] - qi[..., None] * bbi[d][None]
        wi = qr[..., None] * bbi[d][None] + qi[..., None] * bbr[d][None]
        w = jnp.stack([wr, wi], axis=0)
        w = jnp.transpose(w, (2, 1, 4, 0, 3))
        wx.append(_compact(w))
    ci = []
    for d in range(2):
        exps = [i + 1 for i in range(t)] if d == 0 else [t - i for i in range(t)]
        e_r = jnp.stack([er[e, d] for e in exps])
        e_i = jnp.stack([ei[e, d] for e in exps])
        w = jnp.stack([e_r, -e_i], axis=0)
        w = jnp.transpose(w, (2, 0, 4, 1, 3))
        ci.append(_compact(w))
    at = jnp.stack([pr[t], pi[t]], axis=1)
    at = at.reshape(2, 2, S5_NJ, S5_GB * S5_STATE)
    at = jnp.transpose(at, (0, 2, 1, 3)).reshape(2, S5_NJ, 1, S5_SW)
    return m, jnp.stack(wx), jnp.stack(ci), at


def _chunk_rows(ref, nc):
    return jnp.concatenate([ref[pl.ds(s, nc, stride=S5_T), :] for s in range(S5_T)], axis=1)


def _cmul(ar, ai, sr, si):
    return ar * sr - ai * si, ar * si + ai * sr


def _scan_tiles(nc, reverse, step):
    nt = nc // SUBLANES

    def body(it, carry):
        tix = (nt - 1 - it) if reverse else it
        k0 = pl.multiple_of(tix * SUBLANES, SUBLANES)
        return step(k0, carry)

    return body, nt


def _s5_specs(nc):
    hbm = pl.BlockSpec(memory_space=pl.ANY)
    aspec = pl.BlockSpec((1, 1, S5_SW), lambda j: (j, 0, 0))
    cspec = pl.BlockSpec((1, nc, S5_W), lambda j: (j, 0, 0))
    return hbm, aspec, cspec


def _s5_fetch(j, tok_hbm, w_hbms, tok_s, w_s, sems):
    cols = pl.ds(pl.multiple_of(j * LANES, LANES), LANES)
    cps = [pltpu.make_async_copy(tok_hbm.at[:, cols], tok_s, sems.at[0])]
    for i, w in enumerate(w_hbms):
        cps.append(pltpu.make_async_copy(w.at[j], w_s.at[i], sems.at[1 + i]))
    for cp in cps:
        cp.start()
    return cols, cps


def s5_forward(u, m, wx, ci, at, n_valid, prefetch=()):
    lp = u.shape[0]
    nc = lp // S5_T
    nvc = n_valid // S5_T
    npf = len(prefetch)
    pf_specs, pf_shapes, pf_sems = _prefetch_specs(prefetch)

    def body(*refs):
        u_hbm, m_hbm, wxf_hbm, wxr_hbm, cif_hbm, cir_hbm, atf_ref, atr_ref = refs[:8]
        pf_ins = refs[8:8 + npf]
        y_hbm, lhs_ref, sp_ref, sn_ref = refs[8 + npf:12 + npf]
        pf_outs = refs[12 + npf:12 + 2 * npf]
        tok_s, w_s, xf_s, xr_s, sems = refs[12 + 2 * npf:17 + 2 * npf]
        j = pl.program_id(0)
        _prefetch_run(j == 0, j == S5_NJ - 1, pf_ins, pf_outs, refs[17 + 2 * npf:])
        cols, cps = _s5_fetch(j, u_hbm, (m_hbm, wxf_hbm, wxr_hbm, cif_hbm, cir_hbm), tok_s, w_s, sems)
        cps[0].wait()
        lhs = _chunk_rows(tok_s, nc)
        rows = lax.broadcasted_iota(jnp.int32, lhs.shape, 0)
        lhs = jnp.where(rows < nvc, lhs, 0.0).astype(BF16)
        lhs_ref[0] = lhs
        cps[2].wait()
        cps[3].wait()
        xf_s[...] = _nn(lhs, w_s[1])
        xr_s[...] = _nn(lhs, w_s[2])
        afr, afi = atf_ref[0, :, :S5_HALF], atf_ref[0, :, S5_HALF:]
        arr, ari = atr_ref[0, :, :S5_HALF], atr_ref[0, :, S5_HALF:]

        def scan_step(x_s, ar, ai, descending):
            def step(k0, carry):
                sr, si = carry
                x = x_s[pl.ds(k0, SUBLANES), :]
                outs = [None] * SUBLANES
                order = reversed(range(SUBLANES)) if descending else range(SUBLANES)
                for r in order:
                    outs[r] = jnp.concatenate([sr, si], axis=1)
                    nr, ni = _cmul(ar, ai, sr, si)
                    sr = nr + x[r:r + 1, :S5_HALF]
                    si = ni + x[r:r + 1, S5_HALF:]
                x_s[pl.ds(k0, SUBLANES), :] = jnp.concatenate(outs, axis=0)
                return sr, si
            return step

        zero = jnp.zeros((1, S5_HALF), F32)
        fb, nt = _scan_tiles(nc, False, scan_step(xf_s, afr, afi, False))
        lax.fori_loop(0, nt, fb, (zero, zero))
        rb, nt = _scan_tiles(nc, True, scan_step(xr_s, arr, ari, True))
        lax.fori_loop(0, nt, rb, (zero, zero))
        sp = xf_s[...].astype(BF16)
        sn = xr_s[...].astype(BF16)
        sp_ref[0] = sp
        sn_ref[0] = sn
        cps[1].wait()
        cps[4].wait()
        cps[5].wait()
        y = _nn(lhs, w_s[0]) + _nn(sp, w_s[3]) + _nn(sn, w_s[4])
        for i in range(S5_T):
            tok_s[pl.ds(i, nc, stride=S5_T), :] = y[:, i * LANES:(i + 1) * LANES]
        pltpu.sync_copy(tok_s, y_hbm.at[:, cols])

    hbm, aspec, cspec = _s5_specs(nc)
    return pl.pallas_call(
        body, name="s5_forward", grid=(S5_NJ,),
        in_specs=[hbm] * 6 + [aspec, aspec] + pf_specs,
        out_specs=[hbm, cspec, cspec, cspec] + pf_specs,
        out_shape=[jax.ShapeDtypeStruct((lp, D_MODEL), F32)] + [jax.ShapeDtypeStruct((S5_NJ, nc, S5_W), BF16)] * 3 + pf_shapes,
        scratch_shapes=[pltpu.VMEM((lp, LANES), F32), pltpu.VMEM((5, S5_W, S5_W), BF16),
                        pltpu.VMEM((nc, S5_SW), F32), pltpu.VMEM((nc, S5_SW), F32), pltpu.SemaphoreType.DMA((6,))] + pf_sems,
        compiler_params=_params(("arbitrary",)),
    )(u, m, wx[0], wx[1], ci[0], ci[1], at[0], at[1], *prefetch)


def s5_backward(dy, dhs, m, wx, ci, at, sp, sn, n_valid):
    lp = dy.shape[0]
    nc = lp // S5_T
    nvc = n_valid // S5_T

    def body(dy_hbm, dhs_hbm, m_hbm, wxf_hbm, wxr_hbm, cif_hbm, cir_hbm, atf_ref, atr_ref, sp_ref, sn_ref,
             dh_hbm, ldy_ref, dxf_ref, dxr_ref, daf_ref, dar_ref, tok_s, w_s, gf_s, gr_s, sems):
        j = pl.program_id(0)
        cols, cps = _s5_fetch(j, dy_hbm, (m_hbm, wxf_hbm, wxr_hbm, cif_hbm, cir_hbm), tok_s, w_s, sems)
        cps[0].wait()
        ldy = _chunk_rows(tok_s, nc)
        rows = lax.broadcasted_iota(jnp.int32, ldy.shape, 0)
        ldy = jnp.where(rows < nvc, ldy, 0.0).astype(BF16)
        ldy_ref[0] = ldy
        resid = pltpu.make_async_copy(dhs_hbm.at[:, cols], tok_s, sems.at[0])
        resid.start()
        cps[4].wait()
        cps[5].wait()
        gf_s[...] = _nt(ldy, w_s[3])
        gr_s[...] = _nt(ldy, w_s[4])
        afr, afi = atf_ref[0, :, :S5_HALF], atf_ref[0, :, S5_HALF:]
        arr, ari = atr_ref[0, :, :S5_HALF], atr_ref[0, :, S5_HALF:]

        def adj_step(g_s, s_ref, ar, ai, descending):
            def step(k0, carry):
                gr_, gi_, dr_, di_ = carry
                g = g_s[pl.ds(k0, SUBLANES), :]
                p = s_ref[0, pl.ds(k0, SUBLANES), :].astype(F32)
                outs = [None] * SUBLANES
                order = reversed(range(SUBLANES)) if descending else range(SUBLANES)
                for r in order:
                    outs[r] = jnp.concatenate([gr_, gi_], axis=1)
                    pr_, pi_ = p[r:r + 1, :S5_HALF], p[r:r + 1, S5_HALF:]
                    dr_ = dr_ + gr_ * pr_ + gi_ * pi_
                    di_ = di_ + gi_ * pr_ - gr_ * pi_
                    nr, ni = _cmul(ar, -ai, gr_, gi_)
                    gr_ = nr + g[r:r + 1, :S5_HALF]
                    gi_ = ni + g[r:r + 1, S5_HALF:]
                g_s[pl.ds(k0, SUBLANES), :] = jnp.concatenate(outs, axis=0)
                return gr_, gi_, dr_, di_
            return step

        zero = jnp.zeros((1, S5_HALF), F32)
        fb, nt = _scan_tiles(nc, True, adj_step(gf_s, sp_ref, afr, afi, True))
        _, _, dr_, di_ = lax.fori_loop(0, nt, fb, (zero,) * 4)
        daf_ref[0] = jnp.concatenate([dr_, di_], axis=1)
        rb, nt = _scan_tiles(nc, False, adj_step(gr_s, sn_ref, arr, ari, False))
        _, _, dr_, di_ = lax.fori_loop(0, nt, rb, (zero,) * 4)
        dar_ref[0] = jnp.concatenate([dr_, di_], axis=1)
        dxf = gf_s[...].astype(BF16)
        dxr = gr_s[...].astype(BF16)
        dxf_ref[0] = dxf
        dxr_ref[0] = dxr
        cps[1].wait()
        cps[2].wait()
        cps[3].wait()
        du = _nt(ldy, w_s[0]) + _nt(dxf, w_s[1]) + _nt(dxr, w_s[2])
        rows = lax.broadcasted_iota(jnp.int32, du.shape, 0)
        du = jnp.where(rows < nvc, du, 0.0)
        resid.wait()
        for s in range(S5_T):
            tok_s[pl.ds(s, nc, stride=S5_T), :] += du[:, s * LANES:(s + 1) * LANES]
        pltpu.sync_copy(tok_s, dh_hbm.at[:, cols])

    hbm, aspec, cspec = _s5_specs(nc)
    return pl.pallas_call(
        body, name="s5_backward", grid=(S5_NJ,),
        in_specs=[hbm] * 7 + [aspec, aspec, cspec, cspec],
        out_specs=[hbm, cspec, cspec, cspec, aspec, aspec],
        out_shape=[jax.ShapeDtypeStruct((lp, D_MODEL), F32)] + [jax.ShapeDtypeStruct((S5_NJ, nc, S5_W), BF16)] * 3
        + [jax.ShapeDtypeStruct((S5_NJ, 1, S5_SW), F32)] * 2,
        scratch_shapes=[pltpu.VMEM((lp, LANES), F32), pltpu.VMEM((5, S5_W, S5_W), BF16),
                        pltpu.VMEM((nc, S5_SW), F32), pltpu.VMEM((nc, S5_SW), F32), pltpu.SemaphoreType.DMA((6,))],
        compiler_params=_params(("arbitrary",)),
    )(dy, dhs, m, wx[0], wx[1], ci[0], ci[1], at[0], at[1], sp, sn)


S5_CW = LANES


def _replicate_matrix(b1):
    b0n = S5_CW // b1
    eye0 = jnp.eye(b0n, dtype=F32)
    eye1 = jnp.eye(b1, dtype=F32)
    r = jnp.einsum("ab,cd->acbd", eye0, eye1)[:, :, :, None, :]
    r = jnp.broadcast_to(r, (b0n, b1, b0n, S5_GB, b1))
    return r.reshape(S5_CW, b0n * S5_GB * b1).astype(BF16)


def _same_group(a1, b1):
    rg = (lax.broadcasted_iota(jnp.int32, (S5_W, S5_W), 0) // a1) % S5_GB
    cg = (lax.broadcasted_iota(jnp.int32, (S5_W, S5_W), 1) // b1) % S5_GB
    return rg == cg


def blockdiag_expand(compact, a1, b1, name):
    nj = compact.shape[0]

    def body(c_ref, r_ref, o_ref):
        rep = _nn(c_ref[0].astype(BF16), r_ref[...])
        o_ref[0] = jnp.where(_same_group(a1, b1), rep, 0.0).astype(BF16)

    return pl.pallas_call(
        body, name=name, grid=(nj,),
        in_specs=[pl.BlockSpec((1, S5_W, S5_CW), lambda j: (j, 0, 0)), _full_spec((S5_CW, S5_W))],
        out_specs=pl.BlockSpec((1, S5_W, S5_W), lambda j: (j, 0, 0)),
        out_shape=jax.ShapeDtypeStruct((nj, S5_W, S5_W), BF16),
        compiler_params=_params(("parallel",)),
    )(compact, _replicate_matrix(b1))


def bmm_tn_compact(a, b, a1, b1, name):
    nj, k, wa = a.shape
    wb = b.shape[2]

    def body(a_ref, b_ref, r_ref, o_ref):
        prod = jnp.where(_same_group(a1, b1), _tn(a_ref[0], b_ref[0]), 0.0)
        hi = prod.astype(BF16)
        lo = (prod - hi.astype(F32)).astype(BF16)
        o_ref[0] = _nt(hi, r_ref[...]) + _nt(lo, r_ref[...])

    return pl.pallas_call(
        body, name=name, grid=(nj,),
        in_specs=[pl.BlockSpec((1, k, wa), lambda j: (j, 0, 0)), pl.BlockSpec((1, k, wb), lambda j: (j, 0, 0)),
                  _full_spec((S5_CW, S5_W))],
        out_specs=pl.BlockSpec((1, wa, S5_CW), lambda j: (j, 0, 0)),
        out_shape=jax.ShapeDtypeStruct((nj, wa, S5_CW), F32),
        compiler_params=_params(("parallel",)),
    )(a, b, _replicate_matrix(b1))


def _tm(lp):
    return _pick_tile(lp, (768, 256))


def _row_spec(tm, width):
    return pl.BlockSpec((tm, width), lambda i: (i, 0))


def _full_spec(shape):
    return pl.BlockSpec(shape, lambda *_: (0,) * len(shape))


def _gelu(v):
    return 0.5 * v * (1.0 + lax.erf(v * (2.0 ** -0.5)))


def _gelu_grad(v):
    return 0.5 * (1.0 + lax.erf(v * (2.0 ** -0.5))) + v * jnp.exp(-0.5 * v * v) * (2.0 * math.pi) ** -0.5


def _layer_norm(r, gain, bias):
    mean = jnp.mean(r, axis=-1, keepdims=True)
    c = r - mean
    var = jnp.mean(c * c, axis=-1, keepdims=True)
    return c * lax.rsqrt(var + LN_EPS) * gain + bias


def glu_forward(y, h, dvec, wglu):
    lp, d = y.shape
    tm = _tm(lp)

    def body(y_ref, h_ref, d_ref, w_ref, v_ref, t_ref, g_ref, z_ref):
        v = y_ref[...] + d_ref[...] * h_ref[...]
        g = _gelu(v)
        gb = g.astype(BF16)
        t = _nn(gb, w_ref[...])
        v_ref[...] = v
        t_ref[...] = t
        g_ref[...] = gb
        z_ref[...] = (g * jax.nn.sigmoid(t)).astype(BF16)

    rs = _row_spec(tm, d)
    return pl.pallas_call(
        body, name="glu_forward", grid=(lp // tm,),
        in_specs=[rs, rs, _full_spec((1, d)), _full_spec((d, d))],
        out_specs=[rs, rs, rs, rs],
        out_shape=[jax.ShapeDtypeStruct((lp, d), F32)] * 2 + [jax.ShapeDtypeStruct((lp, d), BF16)] * 2,
        compiler_params=_params(("parallel",)),
    )(y, h, dvec, wglu)


def proj_ln_forward(z, w, h, gain, bias, name):
    lp, k = z.shape
    d = w.shape[1]
    tm = _tm(lp)

    def body(z_ref, w_ref, h_ref, g_ref, b_ref, r_ref, o_ref):
        r = ALPHA * h_ref[...] + _nn(z_ref[...], w_ref[...])
        r_ref[...] = r
        o_ref[...] = _layer_norm(r, g_ref[...], b_ref[...])

    rs = _row_spec(tm, d)
    return pl.pallas_call(
        body, name=name, grid=(lp // tm,),
        in_specs=[_row_spec(tm, k), _full_spec((k, d)), rs, _full_spec((1, d)), _full_spec((1, d))],
        out_specs=[rs, rs],
        out_shape=[jax.ShapeDtypeStruct((lp, d), F32)] * 2,
        compiler_params=_params(("parallel",)),
    )(z, w, h, gain, bias)


FFN_TM = 384


def ffn_up_forward(h, wg, wu):
    lp, d = h.shape
    dff = wg.shape[1]
    tm = _pick_tile(lp, (FFN_TM, 256))

    def body(h_ref, wg_ref, wu_ref, a_ref, b_ref, f_ref):
        hb = h_ref[...].astype(BF16)
        a = _nn(hb, wg_ref[...])
        b = _nn(hb, wu_ref[...])
        a_ref[...] = a.astype(BF16)
        b_ref[...] = b.astype(BF16)
        f_ref[...] = (a * jax.nn.sigmoid(a) * b).astype(BF16)

    ws = _full_spec((d, dff))
    os_ = _row_spec(tm, dff)
    return pl.pallas_call(
        body, name="ffn_up_forward", grid=(lp // tm,),
        in_specs=[_row_spec(tm, d), ws, ws],
        out_specs=[os_, os_, os_],
        out_shape=[jax.ShapeDtypeStruct((lp, dff), BF16)] * 3,
        compiler_params=_params(("parallel",)),
    )(h, wg, wu)


def _ln_backward_rows(dh_, r_, gain):
    mean = jnp.mean(r_, axis=-1, keepdims=True)
    c = r_ - mean
    var = jnp.mean(c * c, axis=-1, keepdims=True)
    rstd = lax.rsqrt(var + LN_EPS)
    xh = c * rstd
    dxh = dh_ * gain
    m1 = jnp.mean(dxh, axis=-1, keepdims=True)
    m2 = jnp.mean(dxh * xh, axis=-1, keepdims=True)
    return rstd * (dxh - m1 - xh * m2), jnp.sum(dh_ * xh, axis=0, keepdims=True), jnp.sum(dh_, axis=0, keepdims=True)


def _accumulate_ln_sums(s_ref, sg, sb):
    @pl.when(pl.program_id(0) == 0)
    def _():
        s_ref[...] = jnp.zeros_like(s_ref)

    s_ref[0:1, :] += sg
    s_ref[1:2, :] += sb


def ln_backward(dh, r, gain):
    lp, d = dh.shape
    tm = _tm(lp)

    def body(dh_ref, r_ref, g_ref, dr_ref, s_ref):
        dr, sg, sb = _ln_backward_rows(dh_ref[...], r_ref[...], g_ref[...])
        dr_ref[...] = dr
        _accumulate_ln_sums(s_ref, sg, sb)

    rs = _row_spec(tm, d)
    return pl.pallas_call(
        body, name="ln_backward", grid=(lp // tm,),
        in_specs=[rs, rs, _full_spec((1, d))],
        out_specs=[rs, _full_spec((SUBLANES, d))],
        out_shape=[jax.ShapeDtypeStruct((lp, d), F32), jax.ShapeDtypeStruct((SUBLANES, d), F32)],
        compiler_params=_params(("arbitrary",)),
    )(dh, r, gain)


def ffn_backward_act(dr, wd, a, b):
    lp, d = dr.shape
    dff = wd.shape[0]
    tm = _pick_tile(lp, (FFN_TM, 256))

    def body(dr_ref, wd_ref, a_ref, b_ref, da_ref, db_ref):
        df = _nt(dr_ref[...].astype(BF16), wd_ref[...])
        a_ = a_ref[...].astype(F32)
        b_ = b_ref[...].astype(F32)
        sg = jax.nn.sigmoid(a_)
        da_ref[...] = (df * b_ * sg * (1.0 + a_ * (1.0 - sg))).astype(BF16)
        db_ref[...] = (df * a_ * sg).astype(BF16)

    os_ = _row_spec(tm, dff)
    return pl.pallas_call(
        body, name="ffn_backward_act", grid=(lp // tm,),
        in_specs=[_row_spec(tm, d), _full_spec((dff, d)), os_, os_],
        out_specs=[os_, os_],
        out_shape=[jax.ShapeDtypeStruct((lp, dff), BF16)] * 2,
        compiler_params=_params(("parallel",)),
    )(dr, wd, a, b)


def resid_nt(dr, xs, ws, name, ln=None):
    lp, d = dr.shape
    tm = _pick_tile(lp, (FFN_TM, 256))
    n = len(xs)

    def body(*refs):
        acc = ALPHA * refs[0][...]
        for i in range(n):
            acc = acc + _nt(refs[1 + i][...], refs[1 + n + i][...])
        if ln is None:
            refs[-1][...] = acc
        else:
            r_ref, g_ref, o_ref, s_ref = refs[1 + 2 * n:]
            dr_, sg, sb = _ln_backward_rows(acc, r_ref[...], g_ref[...])
            o_ref[...] = dr_
            _accumulate_ln_sums(s_ref, sg, sb)

    rs = _row_spec(tm, d)
    in_specs = [rs] + [_row_spec(tm, x.shape[1]) for x in xs] + [_full_spec(w.shape) for w in ws]
    if ln is None:
        return pl.pallas_call(
            body, name=name, grid=(lp // tm,),
            in_specs=in_specs, out_specs=rs,
            out_shape=jax.ShapeDtypeStruct((lp, d), F32),
            compiler_params=_params(("parallel",)),
        )(dr, *xs, *ws)
    return pl.pallas_call(
        body, name=name, grid=(lp // tm,),
        in_specs=in_specs + [rs, _full_spec((1, d))], out_specs=[rs, _full_spec((SUBLANES, d))],
        out_shape=[jax.ShapeDtypeStruct((lp, d), F32), jax.ShapeDtypeStruct((SUBLANES, d), F32)],
        compiler_params=_params(("arbitrary",)),
    )(dr, *xs, *ws, *ln)


def mm_tn(x, y, name):
    lp, k = x.shape
    n = y.shape[1]
    tm = _tm(lp)
    nb = n

    def body(x_ref, y_ref, o_ref):
        @pl.when(pl.program_id(1) == 0)
        def _():
            o_ref[...] = jnp.zeros_like(o_ref)

        o_ref[...] += _tn(x_ref[...].astype(BF16), y_ref[...].astype(BF16))

    return pl.pallas_call(
        body, name=name, grid=(n // nb, lp // tm),
        in_specs=[pl.BlockSpec((tm, k), lambda j, i: (i, 0)), pl.BlockSpec((tm, nb), lambda j, i: (i, j))],
        out_specs=pl.BlockSpec((k, nb), lambda j, i: (0, j)),
        out_shape=jax.ShapeDtypeStruct((k, n), F32),
        compiler_params=_params(("parallel", "arbitrary")),
    )(x, y)


def glu_backward1(dr, wout, g, t):
    lp, d = dr.shape
    tm = _tm(lp)

    def body(dr_ref, w_ref, g_ref, t_ref, dt_ref, dgd_ref):
        dz = _nt(dr_ref[...].astype(BF16), w_ref[...])
        s = jax.nn.sigmoid(t_ref[...])
        dgd_ref[...] = dz * s
        dt_ref[...] = (dz * g_ref[...].astype(F32) * s * (1.0 - s)).astype(BF16)

    rs = _row_spec(tm, d)
    return pl.pallas_call(
        body, name="glu_backward1", grid=(lp // tm,),
        in_specs=[rs, _full_spec((d, d)), rs, rs],
        out_specs=[rs, rs],
        out_shape=[jax.ShapeDtypeStruct((lp, d), BF16), jax.ShapeDtypeStruct((lp, d), F32)],
        compiler_params=_params(("parallel",)),
    )(dr, wout, g, t)


def glu_backward2(dt, dgd, wglu, v, h, dvec, dr):
    lp, d = dt.shape
    tm = _tm(lp)

    def body(dt_ref, dgd_ref, w_ref, v_ref, h_ref, d_ref, dr_ref, dv_ref, dhs_ref, s_ref):
        dg = dgd_ref[...] + _nt(dt_ref[...], w_ref[...])
        dv = dg * _gelu_grad(v_ref[...])
        dv_ref[...] = dv
        dhs_ref[...] = ALPHA * dr_ref[...] + dv * d_ref[...]

        @pl.when(pl.program_id(0) == 0)
        def _():
            s_ref[...] = jnp.zeros_like(s_ref)

        s_ref[0:1, :] += jnp.sum(dv * h_ref[...], axis=0, keepdims=True)

    rs = _row_spec(tm, d)
    return pl.pallas_call(
        body, name="glu_backward2", grid=(lp // tm,),
        in_specs=[rs, rs, _full_spec((d, d)), rs, rs, _full_spec((1, d)), rs],
        out_specs=[rs, rs, _full_spec((SUBLANES, d))],
        out_shape=[jax.ShapeDtypeStruct((lp, d), F32)] * 2 + [jax.ShapeDtypeStruct((SUBLANES, d), F32)],
        compiler_params=_params(("arbitrary",)),
    )(dt, dgd, wglu, v, h, dvec, dr)


def loss_backward(hf, tgt, n_valid):
    lp, d = hf.shape
    tm = _tm(lp)

    def body(h_ref, t_ref, dh_ref, s_ref):
        rows = pl.program_id(0) * tm + lax.broadcasted_iota(jnp.int32, (tm, d), 0)
        ok = (rows >= N_META) & (rows < n_valid)
        e = jnp.where(ok, h_ref[...] - t_ref[...], 0.0)
        dh_ref[...] = e * (1.0 / d)

        @pl.when(pl.program_id(0) == 0)
        def _():
            s_ref[...] = jnp.zeros_like(s_ref)

        sq = e * e
        part = sq[:, 0:LANES]
        for c in range(1, d // LANES):
            part = part + sq[:, c * LANES:(c + 1) * LANES]
        acc = part[0:SUBLANES]
        for r in range(1, tm // SUBLANES):
            acc = acc + part[r * SUBLANES:(r + 1) * SUBLANES]
        s_ref[...] += acc * (0.5 / d)

    rs = _row_spec(tm, d)
    return pl.pallas_call(
        body, name="loss_backward", grid=(lp // tm,),
        in_specs=[rs, rs], out_specs=[rs, _full_spec((SUBLANES, LANES))],
        out_shape=[jax.ShapeDtypeStruct((lp, d), F32), jax.ShapeDtypeStruct((SUBLANES, LANES), F32)],
        compiler_params=_params(("arbitrary",)),
    )(hf, tgt)


N_QB = N_Q_HEADS // 2
N_KB = N_KV_HEADS
QKV_W = (N_QB + 2 * N_KB) * LANES
Q_SCALE = HEAD_DIM ** -0.5 * math.log2(math.e)


def rope_tables(lp, n_valid):
    t = jnp.arange(lp, dtype=jnp.int32)
    real = (t >= N_META) & (t < n_valid)
    pos = jnp.where(real, t - N_META, 0)
    row = (pos // GRID_W).astype(F32)
    col = (pos % GRID_W).astype(F32)
    axis_dim = HEAD_DIM // 2
    inv = ROPE_THETA ** (-jnp.arange(0, axis_dim, 2, dtype=F32) / axis_dim)
    ar = row[:, None] * inv[None, :]
    ac = col[:, None] * inv[None, :]
    cos = jnp.concatenate([jnp.cos(ar), jnp.cos(ar), jnp.cos(ac), jnp.cos(ac)], axis=1)
    sin = jnp.concatenate([-jnp.sin(ar), jnp.sin(ar), -jnp.sin(ac), jnp.sin(ac)], axis=1)
    return jnp.tile(cos, (1, 2)), jnp.tile(sin, (1, 2))


def head_sum_matrix():
    return jnp.kron(jnp.eye(2, dtype=F32), jnp.ones((HEAD_DIM, HEAD_DIM), F32)).astype(BF16)


def _segsum(x, e):
    hi = x.astype(BF16)
    lo = (x - hi.astype(F32)).astype(BF16)
    return _nn(hi, e) + _nn(lo, e)


def _swap_halves(x):
    lane = lax.broadcasted_iota(jnp.int32, x.shape, 1)
    quarter = HEAD_DIM // 4
    return jnp.where(lane % (2 * quarter) < quarter, pltpu.roll(x, LANES - quarter, 1), pltpu.roll(x, quarter, 1))


def qkv_forward(h, w2, gq, gk, cos, sin, e):
    lp, d = h.shape
    tm = _tm(lp)
    kw, vw = N_KB * LANES, N_KB * LANES

    def body(h_ref, w_ref, gq_ref, gk_ref, cos_ref, sin_ref, e_ref, raw_ref, q_ref, k_ref, v_ref, qt_ref, vt_ref):
        raw = _nn(h_ref[...].astype(BF16), w_ref[...])
        raw_ref[...] = raw
        c, s_, em = cos_ref[...], sin_ref[...], e_ref[...]
        for cb in range(N_QB + N_KB):
            t = raw[:, cb * LANES:(cb + 1) * LANES]
            rstd = lax.rsqrt(_segsum(t * t, em) * (1.0 / HEAD_DIM) + QK_EPS)
            n = t * rstd * (gq_ref[...] if cb < N_QB else gk_ref[...])
            rot = n * c + _swap_halves(n) * s_
            if cb < N_QB:
                qs = rot * Q_SCALE
                q_ref[:, cb * LANES:(cb + 1) * LANES] = qs.astype(BF16)
                qt_ref[cb * LANES:(cb + 1) * LANES, :] = qs.T.astype(BF16)
            else:
                k_ref[:, (cb - N_QB) * LANES:(cb - N_QB + 1) * LANES] = rot.astype(BF16)
        v_ref[...] = raw[:, (N_QB + N_KB) * LANES:].astype(BF16)
        for cb in range(N_KB):
            lo = (N_QB + N_KB + cb) * LANES
            vt_ref[cb * LANES:(cb + 1) * LANES, :] = raw[:, lo:lo + LANES].T.astype(BF16)

    col_spec = lambda rows: pl.BlockSpec((rows, tm), lambda i: (0, i))
    return pl.pallas_call(
        body, name="qkv_forward", grid=(lp // tm,),
        in_specs=[_row_spec(tm, d), _full_spec((d, QKV_W)), _full_spec((1, LANES)), _full_spec((1, LANES)),
                  _row_spec(tm, LANES), _row_spec(tm, LANES), _full_spec((LANES, LANES))],
        out_specs=[_row_spec(tm, QKV_W), _row_spec(tm, N_QB * LANES), _row_spec(tm, kw), _row_spec(tm, vw),
                   col_spec(N_QB * LANES), col_spec(vw)],
        out_shape=[jax.ShapeDtypeStruct((lp, QKV_W), F32), jax.ShapeDtypeStruct((lp, N_QB * LANES), BF16),
                   jax.ShapeDtypeStruct((lp, kw), BF16), jax.ShapeDtypeStruct((lp, vw), BF16),
                   jax.ShapeDtypeStruct((N_QB * LANES, lp), BF16), jax.ShapeDtypeStruct((vw, lp), BF16)],
        compiler_params=_params(("parallel",)),
    )(h, w2, gq, gk, cos, sin, e)


def qkv_backward(dqs, dk2, dv2, raw, gq, gk, cos, sin, e):
    lp = raw.shape[0]
    tm = _tm(lp)

    def body(dq_ref, dk_ref, dv_ref, raw_ref, gq_ref, gk_ref, cos_ref, sin_ref, e_ref, d_ref, s_ref):
        @pl.when(pl.program_id(0) == 0)
        def _():
            s_ref[...] = jnp.zeros_like(s_ref)

        c, s_, em = cos_ref[...], sin_ref[...], e_ref[...]
        gsum = [jnp.zeros((1, LANES), F32), jnp.zeros((1, LANES), F32)]
        for cb in range(N_QB + N_KB):
            isq = cb < N_QB
            t = raw_ref[:, cb * LANES:(cb + 1) * LANES]
            if isq:
                drot = dq_ref[:, cb * LANES:(cb + 1) * LANES] * (HEAD_DIM ** -0.5)
            else:
                drot = dk_ref[:, (cb - N_QB) * LANES:(cb - N_QB + 1) * LANES] * math.log(2.0)
            gain = gq_ref[...] if isq else gk_ref[...]
            rstd = lax.rsqrt(_segsum(t * t, em) * (1.0 / HEAD_DIM) + QK_EPS)
            dn = drot * c + _swap_halves(drot * s_)
            xh = t * rstd
            gsum[0 if isq else 1] = gsum[0 if isq else 1] + jnp.sum(dn * xh, axis=0, keepdims=True)
            w = dn * gain
            mw = _segsum(w * xh, em) * (1.0 / HEAD_DIM)
            d_ref[:, cb * LANES:(cb + 1) * LANES] = (rstd * (w - xh * mw)).astype(BF16)
        d_ref[:, (N_QB + N_KB) * LANES:] = dv_ref[...].astype(BF16)
        s_ref[0:1, :] += gsum[0]
        s_ref[1:2, :] += gsum[1]

    kw = N_KB * LANES
    return pl.pallas_call(
        body, name="qkv_backward", grid=(lp // tm,),
        in_specs=[_row_spec(tm, N_QB * LANES), _row_spec(tm, kw), _row_spec(tm, kw), _row_spec(tm, QKV_W),
                  _full_spec((1, LANES)), _full_spec((1, LANES)), _row_spec(tm, LANES), _row_spec(tm, LANES),
                  _full_spec((LANES, LANES))],
        out_specs=[_row_spec(tm, QKV_W), _full_spec((SUBLANES, LANES))],
        out_shape=[jax.ShapeDtypeStruct((lp, QKV_W), BF16), jax.ShapeDtypeStruct((SUBLANES, LANES), F32)],
        compiler_params=_params(("arbitrary",)),
    )(dqs, dk2, dv2, raw, gq, gk, cos, sin, e)


NEG = -1e30
Q_PER_KV = N_Q_HEADS // N_KV_HEADS


def _half_masks(x):
    lane = lax.broadcasted_iota(jnp.int32, x.shape, 1)
    zero = jnp.zeros_like(x)
    return jnp.where(lane < HEAD_DIM, x, zero), jnp.where(lane >= HEAD_DIM, x, zero)


ATTN_TR = 16


def _attn_tiles(lp):
    t = _pick_tile(lp, (1408, 256))
    return t, t


def _attn_tiles_bwd(lp):
    return _pick_tile(lp, (768, 256)), _pick_tile(lp, (1408, 256))


def attn_forward_t(qs, k2, v2t, n_valid, prefetch=()):
    lp = qs.shape[0]
    tq, kb = _attn_tiles(lp)
    nk = lp // kb
    gw = 2 * LANES
    nr = kb // ATTN_TR
    pad0 = n_valid - (nk - 1) * kb
    npf = len(prefetch)
    pf_specs, pf_shapes, pf_sems = _prefetch_specs(prefetch)
    nq = lp // tq

    def body(*refs):
        q_ref, k_ref, vt_ref = refs[:3]
        pf_ins = refs[3:3 + npf]
        o_ref, lse_ref = refs[3 + npf:5 + npf]
        pf_outs = refs[5 + npf:5 + 2 * npf]
        m_s, l_s, acc_s, s_s, p_s = refs[5 + 2 * npf:10 + 2 * npf]
        j = pl.program_id(2)
        first = (pl.program_id(0) == 0) & (pl.program_id(1) == 0) & (j == 0)
        last = (pl.program_id(0) == N_KV_HEADS - 1) & (pl.program_id(1) == nq - 1) & (j == nk - 1)
        _prefetch_run(first, last, pf_ins, pf_outs, refs[10 + 2 * npf:])

        @pl.when(j == 0)
        def _():
            m_s[...] = jnp.full_like(m_s, NEG)
            l_s[...] = jnp.zeros_like(l_s)
            acc_s[...] = jnp.zeros_like(acc_s)

        ks = _half_masks(k_ref[...])
        for pair in range(2):
            qp = q_ref[:, pair * LANES:(pair + 1) * LANES]
            for half in range(2):
                hh = 2 * pair + half
                s_s[...] = _nt(ks[half], qp)

                if pad0 < kb:
                    @pl.when(j == nk - 1)
                    def _():
                        s_s[pad0:, :] = jnp.full((kb - pad0, tq), NEG, F32)

                run = s_s[pl.ds(0, ATTN_TR), :]
                for r in range(1, nr):
                    run = jnp.maximum(run, s_s[pl.ds(r * ATTN_TR, ATTN_TR), :])
                m_prev = m_s[hh:hh + 1, :]
                m_new = jnp.maximum(m_prev, jnp.max(run, axis=0, keepdims=True))
                alpha = jnp.exp2(m_prev - m_new)
                m_s[hh:hh + 1, :] = m_new
                for r in range(nr):
                    rows = pl.ds(r * ATTN_TR, ATTN_TR)
                    p_s[rows, :] = jnp.exp2(s_s[rows, :] - m_new).astype(BF16)
                vt = jnp.concatenate([vt_ref[half * HEAD_DIM:(half + 1) * HEAD_DIM, :],
                                      jnp.ones((ATTN_TR, kb), BF16)], axis=0)
                pv = _nn(vt, p_s[...])
                l_s[hh:hh + 1, :] = alpha * l_s[hh:hh + 1, :] + pv[HEAD_DIM:HEAD_DIM + 1, :]
                pv = pv[:HEAD_DIM, :]
                rs = slice(half * HEAD_DIM, (half + 1) * HEAD_DIM)
                acc_s[pair, rs, :] = alpha * acc_s[pair, rs, :] + pv

        @pl.when(j == nk - 1)
        def _():
            for pair in range(2):
                for half in range(2):
                    hh = 2 * pair + half
                    rs = slice(half * HEAD_DIM, (half + 1) * HEAD_DIM)
                    acc_s[pair, rs, :] = acc_s[pair, rs, :] * (1.0 / l_s[hh:hh + 1, :])
                o_ref[:, pair * LANES:(pair + 1) * LANES] = acc_s[pair].T.astype(BF16)
            for hh in range(Q_PER_KV):
                lse_ref[0, hh] = m_s[hh:hh + 1, :] + jnp.log2(l_s[hh:hh + 1, :])

    return pl.pallas_call(
        body, name="attn_forward", grid=(N_KV_HEADS, lp // tq, nk),
        in_specs=[pl.BlockSpec((tq, gw), lambda g, i, j: (i, g)), pl.BlockSpec((kb, LANES), lambda g, i, j: (j, g)),
                  pl.BlockSpec((LANES, kb), lambda g, i, j: (g, j))] + pf_specs,
        out_specs=[pl.BlockSpec((tq, gw), lambda g, i, j: (i, g)),
                   pl.BlockSpec((1, Q_PER_KV, 1, tq), lambda g, i, j: (g, 0, 0, i))] + pf_specs,
        out_shape=[jax.ShapeDtypeStruct((lp, N_QB * LANES), BF16),
                   jax.ShapeDtypeStruct((N_KV_HEADS, Q_PER_KV, 1, lp), F32)] + pf_shapes,
        scratch_shapes=[pltpu.VMEM((SUBLANES, tq), F32), pltpu.VMEM((SUBLANES, tq), F32),
                        pltpu.VMEM((2, LANES, tq), F32), pltpu.VMEM((kb, tq), F32), pltpu.VMEM((kb, tq), BF16)] + pf_sems,
        compiler_params=_params(("arbitrary",) * 3 if npf else ("parallel", "parallel", "arbitrary")),
    )(qs, k2, v2t, *prefetch)


def attn_backward(qs, qst, k2, v2, do, dot, lse, delta, n_valid):
    lp = qs.shape[0]
    tq, kb = _attn_tiles_bwd(lp)
    nq, nk = lp // tq, lp // kb
    gw = 2 * LANES
    pad0 = n_valid - (nk - 1) * kb

    def body(q_ref, qt_ref, k_ref, v_ref, do_ref, dot_ref, lse_ref, dl_ref, dq_ref, dk_ref, dv_ref, acc_s, dkt_s, dvt_s):
        g = pl.program_id(0)
        i = pl.program_id(1)
        j = pl.program_id(2)
        cols = pl.ds(pl.multiple_of(j * kb, kb), kb)

        @pl.when(j == 0)
        def _():
            acc_s[...] = jnp.zeros_like(acc_s)

        @pl.when(i == 0)
        def _():
            dkt_s[:, cols] = jnp.zeros((LANES, kb), F32)
            dvt_s[:, cols] = jnp.zeros((LANES, kb), F32)

        head = lax.broadcasted_iota(jnp.int32, (tq, N_Q_HEADS), 1)

        def column(ref, hh):
            return jnp.sum(jnp.where(head == Q_PER_KV * g + hh, ref[...], 0.0), axis=1, keepdims=True)

        def step(masked):
            ks = _half_masks(k_ref[...])
            vs = _half_masks(v_ref[...])
            if masked:
                col = lax.broadcasted_iota(jnp.int32, (1, kb), 1)
                bias = jnp.where(col < pad0, 0.0, NEG)
            for pair in range(2):
                qp = q_ref[:, pair * LANES:(pair + 1) * LANES]
                dop = do_ref[:, pair * LANES:(pair + 1) * LANES]
                for half in range(2):
                    hh = 2 * pair + half
                    rs = slice(half * HEAD_DIM, (half + 1) * HEAD_DIM)
                    rt = slice(pair * LANES + half * HEAD_DIM, pair * LANES + (half + 1) * HEAD_DIM)
                    s = _nt(qp, ks[half])
                    if masked:
                        s = s + bias
                    p = jnp.exp2(s - column(lse_ref, hh))
                    dp = _nt(dop, vs[half])
                    ds = (p * (dp - column(dl_ref, hh))).astype(BF16)
                    pb = p.astype(BF16)
                    acc_s[pair] += _nn(ds, ks[half])
                    dvt_s[rs, cols] += _nn(dot_ref[rt, :], pb)
                    dkt_s[rs, cols] += _nn(qt_ref[rt, :], ds)

        if pad0 < kb:
            pl.when(j < nk - 1)(lambda: step(False))
            pl.when(j == nk - 1)(lambda: step(True))
        else:
            step(False)

        @pl.when(j == nk - 1)
        def _():
            for pair in range(2):
                dq_ref[:, pair * LANES:(pair + 1) * LANES] = acc_s[pair]

        @pl.when(i == nq - 1)
        def _():
            dk_ref[cols, :] = dkt_s[:, cols].T
            dv_ref[cols, :] = dvt_s[:, cols].T

    cspec = pl.BlockSpec((tq, N_Q_HEADS), lambda g, i, j: (i, 0))
    qspec = pl.BlockSpec((tq, gw), lambda g, i, j: (i, g))
    tspec = pl.BlockSpec((gw, tq), lambda g, i, j: (g, i))
    kspec = pl.BlockSpec((kb, LANES), lambda g, i, j: (j, g))
    gspec = pl.BlockSpec((lp, LANES), lambda g, i, j: (0, g))
    return pl.pallas_call(
        body, name="attn_backward", grid=(N_KV_HEADS, nq, nk),
        in_specs=[qspec, tspec, kspec, kspec, qspec, tspec, cspec, cspec],
        out_specs=[qspec, gspec, gspec],
        out_shape=[jax.ShapeDtypeStruct((lp, N_QB * LANES), F32),
                   jax.ShapeDtypeStruct((lp, N_KB * LANES), F32), jax.ShapeDtypeStruct((lp, N_KB * LANES), F32)],
        scratch_shapes=[pltpu.VMEM((2, tq, LANES), F32), pltpu.VMEM((LANES, lp), F32), pltpu.VMEM((LANES, lp), F32)],
        compiler_params=_params(("parallel", "arbitrary", "arbitrary")),
    )(qs, qst, k2, v2, do, dot, lse, delta)


def attn_out_backward(dr, wout, o, e16):
    lp, d = dr.shape
    tm = _tm(lp)

    def body(dr_ref, w_ref, o_ref, e_ref, do_ref, dl_ref, dot_ref):
        do32 = _nt(dr_ref[...].astype(BF16), w_ref[...])
        do = do32.astype(BF16)
        do_ref[...] = do
        for cb in range(d // LANES):
            dot_ref[cb * LANES:(cb + 1) * LANES, :] = do32[:, cb * LANES:(cb + 1) * LANES].T.astype(BF16)
        dl_ref[...] = _segsum(do.astype(F32) * o_ref[...].astype(F32), e_ref[...])

    rs = _row_spec(tm, d)
    return pl.pallas_call(
        body, name="attn_out_backward", grid=(lp // tm,),
        in_specs=[rs, _full_spec((d, d)), rs, _full_spec((d, N_Q_HEADS))],
        out_specs=[rs, _row_spec(tm, N_Q_HEADS), pl.BlockSpec((d, tm), lambda i: (0, i))],
        out_shape=[jax.ShapeDtypeStruct((lp, d), BF16), jax.ShapeDtypeStruct((lp, N_Q_HEADS), F32),
                   jax.ShapeDtypeStruct((d, lp), BF16)],
        compiler_params=_params(("parallel",)),
    )(dr, wout, o, e16)


N_CHIPS = 4


def _mesh_pos():
    return lax.axis_index("x"), lax.axis_index("y"), lax.axis_index("c")


def chip_exchange(arrs, scatter, name):
    n = len(arrs)
    hbm = pl.BlockSpec(memory_space=pl.ANY)

    def body(*refs):
        ins, outs = refs[:n], refs[n:2 * n]
        send_sems, recv_sems, loc_sems = refs[2 * n:]
        x, y, c = _mesh_pos()
        me = 2 * x + y
        chips = [(1 - x, y), (x, 1 - y), (1 - x, 1 - y)]
        started = []
        for a in range(n):
            loc = pltpu.make_async_copy(ins[a].at[me] if scatter else ins[a], outs[a].at[me], loc_sems.at[a])
            loc.start()
            started.append(loc)
            for k, (px, py) in enumerate(chips):
                src = ins[a].at[2 * px + py] if scatter else ins[a]
                cp = pltpu.make_async_remote_copy(
                    src_ref=src, dst_ref=outs[a].at[me], send_sem=send_sems.at[3 * a + k], recv_sem=recv_sems.at[3 * a + k],
                    device_id=(px, py, c), device_id_type=MESH)
                cp.start()
                started.append(cp)
        for cp in started:
            cp.wait()

    out_shape = [jax.ShapeDtypeStruct(a.shape if scatter else (N_CHIPS,) + a.shape, a.dtype) for a in arrs]
    return pl.pallas_call(
        body, name=name, in_specs=[hbm] * n, out_specs=[hbm] * n, out_shape=out_shape,
        scratch_shapes=[pltpu.SemaphoreType.DMA((3 * n,)), pltpu.SemaphoreType.DMA((3 * n,)), pltpu.SemaphoreType.DMA((n,))],
    )(*arrs)


def _same_core_copies(ins, outs, send_sems, recv_sems, loc_sems):
    x, y, c = _mesh_pos()
    me = 2 * x + y
    chips = [(1 - x, y), (x, 1 - y), (1 - x, 1 - y)]
    cps = []
    for a in range(len(ins)):
        cps.append(pltpu.make_async_copy(ins[a], outs[a].at[me], loc_sems.at[a]))
        for k, (px, py) in enumerate(chips):
            cps.append(pltpu.make_async_remote_copy(
                src_ref=ins[a], dst_ref=outs[a].at[me], send_sem=send_sems.at[3 * a + k], recv_sem=recv_sems.at[3 * a + k],
                device_id=(px, py, c), device_id_type=MESH))
    return cps


def _prefetch_specs(arrs):
    n = len(arrs)
    hbm = pl.BlockSpec(memory_space=pl.ANY)
    shapes = [jax.ShapeDtypeStruct((N_CHIPS,) + a.shape, a.dtype) for a in arrs]
    sems = [pltpu.SemaphoreType.DMA((3 * n,)), pltpu.SemaphoreType.DMA((3 * n,)), pltpu.SemaphoreType.DMA((n,))] if n else []
    return [hbm] * n, shapes, sems


def _prefetch_run(first, last, ins, outs, sems):
    if not ins:
        return

    @pl.when(first)
    def _():
        for cp in _same_core_copies(ins, outs, *sems):
            cp.start()

    @pl.when(last)
    def _():
        for cp in _same_core_copies(ins, outs, *sems):
            cp.wait()


def gather_two_level(arrs, name):
    n = len(arrs)
    hbm = pl.BlockSpec(memory_space=pl.ANY)

    def body(*refs):
        ins, outs = refs[:n], refs[n:2 * n]
        ici_send, ici_recv, d2d_send, d2d_recv, loc_sems = refs[2 * n:]
        x, y, c = _mesh_pos()
        me = 2 * x + y
        chips = [(1 - x, y), (x, 1 - y), (1 - x, 1 - y)]
        started = []
        for a in range(n):
            hn = arrs[a].shape[0] // 2
            mine = pl.ds(c * hn, hn)
            loc = pltpu.make_async_copy(ins[a], outs[a].at[me], loc_sems.at[a])
            loc.start()
            started.append(loc)
            first = []
            for k, (px, py) in enumerate(chips):
                cp = pltpu.make_async_remote_copy(
                    src_ref=ins[a].at[mine], dst_ref=outs[a].at[me, mine], send_sem=ici_send.at[3 * a + k],
                    recv_sem=ici_recv.at[3 * a + k], device_id=(px, py, c), device_id_type=MESH)
                cp.start()
                first.append(cp)
            for k, (px, py) in enumerate(chips):
                q = 2 * px + py
                first[k].wait_recv()
                fw = pltpu.make_async_remote_copy(
                    src_ref=outs[a].at[q, mine], dst_ref=outs[a].at[q, mine], send_sem=d2d_send.at[3 * a + k],
                    recv_sem=d2d_recv.at[3 * a + k], device_id=(x, y, 1 - c), device_id_type=MESH)
                fw.start()
                started.append(fw)
            for cp in first:
                cp.wait_send()
        for cp in started:
            cp.wait()

    out_shape = [jax.ShapeDtypeStruct((N_CHIPS,) + a.shape, a.dtype) for a in arrs]
    return pl.pallas_call(
        body, name=name, in_specs=[hbm] * n, out_specs=[hbm] * n, out_shape=out_shape,
        scratch_shapes=[pltpu.SemaphoreType.DMA((3 * n,))] * 4 + [pltpu.SemaphoreType.DMA((n,))],
    )(*arrs)


def sibling_exchange(arrs, name):
    n = len(arrs)
    hbm = pl.BlockSpec(memory_space=pl.ANY)

    def body(*refs):
        ins, outs = refs[:n], refs[n:2 * n]
        send_sems, recv_sems = refs[2 * n:]
        x, y, c = _mesh_pos()
        started = []
        for a in range(n):
            cp = pltpu.make_async_remote_copy(
                src_ref=ins[a], dst_ref=outs[a], send_sem=send_sems.at[a], recv_sem=recv_sems.at[a],
                device_id=(x, y, 1 - c), device_id_type=MESH)
            cp.start()
            started.append(cp)
        for cp in started:
            cp.wait()

    return pl.pallas_call(
        body, name=name, in_specs=[hbm] * n, out_specs=[hbm] * n,
        out_shape=[jax.ShapeDtypeStruct(a.shape, a.dtype) for a in arrs],
        scratch_shapes=[pltpu.SemaphoreType.DMA((n,)), pltpu.SemaphoreType.DMA((n,))],
    )(*arrs)


def _rows_tile(r, c):
    return _pick_tile(r, tuple(t for t in (512, 256, 128, 64, 32, 16, 8) if t * c * 4 <= 2 * 1024 * 1024))


def chip_sum(recv, name):
    _, r, c = recv.shape
    tr = _rows_tile(r, c)

    def body(r_ref, o_ref):
        acc = r_ref[0].astype(F32)
        for q in range(1, N_CHIPS):
            acc = acc + r_ref[q].astype(F32)
        o_ref[...] = acc

    return pl.pallas_call(
        body, name=name, grid=(r // tr,),
        in_specs=[pl.BlockSpec((N_CHIPS, tr, c), lambda i: (0, i, 0))],
        out_specs=pl.BlockSpec((tr, c), lambda i: (i, 0)),
        out_shape=jax.ShapeDtypeStruct((r, c), F32),
        compiler_params=_params(("parallel",)),
    )(recv)


def pair_sum(part, sib, name, dtype=F32):
    r, c = part.shape
    tr = _rows_tile(r, c)

    def body(p_ref, s_ref, o_ref):
        o_ref[...] = (p_ref[...].astype(F32) + s_ref[...].astype(F32)).astype(dtype)

    rs = pl.BlockSpec((tr, c), lambda i: (i, 0))
    return pl.pallas_call(
        body, name=name, grid=(r // tr,), in_specs=[rs] * 2, out_specs=rs,
        out_shape=jax.ShapeDtypeStruct((r, c), dtype), compiler_params=_params(("parallel",)),
    )(part, sib)


def adamw(g, w, m, v, name):
    r, c = w.shape
    tr = _rows_tile(r, c)

    def body(g_ref, w_ref, m_ref, v_ref, d_ref, nm_ref, nv_ref):
        g_ = g_ref[...]
        m_ = ADAM_B1 * m_ref[...] + (1.0 - ADAM_B1) * g_
        v_ = ADAM_B2 * v_ref[...] + (1.0 - ADAM_B2) * (g_ * g_)
        m_hat = m_ / (1.0 - ADAM_B1 ** ADAM_STEP)
        v_hat = v_ / (1.0 - ADAM_B2 ** ADAM_STEP)
        d_ref[...] = -ADAM_LR * (m_hat / (jnp.sqrt(v_hat) + ADAM_EPS) + ADAM_WD * w_ref[...])
        nm_ref[...] = m_
        nv_ref[...] = v_

    rs = pl.BlockSpec((tr, c), lambda i: (i, 0))
    return pl.pallas_call(
        body, name=name, grid=(r // tr,), in_specs=[rs] * 4, out_specs=[rs] * 3,
        out_shape=[jax.ShapeDtypeStruct((r, c), F32)] * 3,
        compiler_params=_params(("parallel",)),
    )(g, w, m, v)


def adamw_halves(mine, other, core, w, m, v, name):
    r, c = w.shape
    tr = _rows_tile(r // 2, c)
    th = (r // 2) // tr

    def body(core_ref, a_ref, b_ref, w_ref, m_ref, v_ref, g_ref, d_ref, nm_ref, nv_ref):
        g_ = jnp.where(pl.program_id(0) // th == core_ref[0], a_ref[...], b_ref[...])
        m_ = ADAM_B1 * m_ref[...] + (1.0 - ADAM_B1) * g_
        v_ = ADAM_B2 * v_ref[...] + (1.0 - ADAM_B2) * (g_ * g_)
        m_hat = m_ / (1.0 - ADAM_B1 ** ADAM_STEP)
        v_hat = v_ / (1.0 - ADAM_B2 ** ADAM_STEP)
        g_ref[...] = g_
        d_ref[...] = -ADAM_LR * (m_hat / (jnp.sqrt(v_hat) + ADAM_EPS) + ADAM_WD * w_ref[...])
        nm_ref[...] = m_
        nv_ref[...] = v_

    half = pl.BlockSpec((tr, c), lambda i, core_ref: (i % th, 0))
    rows = pl.BlockSpec((tr, c), lambda i, core_ref: (i, 0))
    return pl.pallas_call(
        body, name=name,
        grid_spec=pltpu.PrefetchScalarGridSpec(num_scalar_prefetch=1, grid=(r // tr,), in_specs=[half, half, rows, rows, rows],
                                               out_specs=[rows] * 4),
        out_shape=[jax.ShapeDtypeStruct((r, c), F32)] * 4,
        compiler_params=_params(("parallel",)),
    )(core, mine, other, w, m, v)


WEIGHTS = ['meta_tokens', 's5_lambda_re', 's5_lambda_im', 's5_log_dt', 's5_b_re', 's5_b_im', 's5_c_re', 's5_c_im', 's5_d',
           's5_w_glu', 's5_w_out', 'attn_w_qkv', 'attn_q_gain', 'attn_k_gain', 'attn_w_out', 'ffn_w_gate', 'ffn_w_up',
           'ffn_w_down', 'ln_gain', 'ln_bias']
BIG = ['s5_w_glu', 's5_w_out', 'attn_w_qkv', 'attn_w_out', 'ffn_w_gate', 'ffn_w_up', 'ffn_w_down']
ROW_SHARDED = {'s5_w_glu', 's5_w_out', 'attn_w_out', 'ffn_w_down'}
SMALL_SHARDED = ['meta_tokens', 'ln_gain', 'ln_bias']
REPLICATED = ['s5_lambda_re', 's5_lambda_im', 's5_log_dt', 's5_b_re', 's5_b_im', 's5_c_re', 's5_c_im', 's5_d',
              'attn_q_gain', 'attn_k_gain']
REP_ALIGN = N_CHIPS * LANES * LANES


def _natural(gathered, row_sharded):
    p, n, a, b = gathered.shape
    if row_sharded:
        return jnp.transpose(gathered, (1, 0, 2, 3)).reshape(n, p * a, b)
    return jnp.transpose(gathered, (1, 2, 0, 3)).reshape(n, a, p * b)


def _shard_major(full, row_sharded):
    n, a, b = full.shape
    if row_sharded:
        return jnp.transpose(full.reshape(n, N_CHIPS, a // N_CHIPS, b), (1, 0, 2, 3))
    return jnp.transpose(full.reshape(n, a, N_CHIPS, b // N_CHIPS), (2, 0, 1, 3))


def _dup_heads(w):
    lead = w.shape[:-1]
    w = w.reshape(lead + (N_KV_HEADS, 1, HEAD_DIM))
    return jnp.broadcast_to(w, lead + (N_KV_HEADS, 2, HEAD_DIM)).reshape(lead + (N_KV_HEADS * 2 * HEAD_DIM,))


def _fold_heads(d):
    lead = d.shape[:-1]
    return d.reshape(lead + (N_KV_HEADS, 2, HEAD_DIM)).sum(axis=-2).reshape(lead + (N_KV_HEADS * HEAD_DIM,))


def _pack_rep(tree, extra=None):
    extra = jnp.zeros((1,), F32) if extra is None else extra.reshape(1)
    flat = jnp.concatenate([tree[n].reshape(-1) for n in REPLICATED] + [extra])
    pad = _round_up(flat.shape[0], REP_ALIGN) - flat.shape[0]
    return jnp.pad(flat, (0, pad))


def _unpack_rep(flat, like):
    out, off = {}, 0
    for n in REPLICATED:
        size = math.prod(like[n].shape)
        out[n] = flat[off:off + size].reshape(like[n].shape)
        off += size
    return out


def _train_step(x, loss_target, w, mom, vel):
    s = x.shape[1]
    n_valid = N_META + s
    lp = _round_up(n_valid, 2 * LANES)
    nq = N_Q_HEADS * HEAD_DIM
    nkv = N_KV_HEADS * HEAD_DIM

    small = jnp.concatenate([w[n].reshape(-1, w[n].shape[-1]) for n in SMALL_SHARDED], axis=0)
    shard = {n: w[n].astype(BF16) for n in BIG}
    uses = [[('s5_w_glu', 0), ('s5_w_out', 0), ('ffn_w_gate', 0), ('ffn_w_up', 0), ('ffn_w_down', 0)],
            [('attn_w_qkv', 0), ('attn_w_out', 0), ('ffn_w_gate', 1), ('ffn_w_up', 1), ('ffn_w_down', 1)],
            [('s5_w_glu', 1), ('s5_w_out', 1), ('ffn_w_gate', 2), ('ffn_w_up', 2), ('ffn_w_down', 2),
             ('attn_w_qkv', 1), ('attn_w_out', 1), ('ffn_w_gate', 3), ('ffn_w_up', 3), ('ffn_w_down', 3)]]
    full = {}

    def unpack(stage, gathered):
        for (n, l), g in zip(uses[stage], gathered):
            full[(n, l)] = _natural(g[:, None], n in ROW_SHARDED)[0]

    first = gather_two_level([shard[n][l] for n, l in uses[0]] + [small], "gather_weights")
    unpack(0, first[:-1])
    small_full = jnp.transpose(first[-1], (1, 0, 2)).reshape(small.shape[0], D_MODEL)
    meta_full = small_full[:N_META]
    ln_gain = small_full[N_META:N_META + 2 * DEPTH].reshape(DEPTH, 2, 1, D_MODEL)
    ln_bias = small_full[N_META + 2 * DEPTH:].reshape(DEPTH, 2, 1, D_MODEL)

    def qkv_dup(wqkv):
        return jnp.concatenate([wqkv[..., :nq], _dup_heads(wqkv[..., nq:nq + nkv]), _dup_heads(wqkv[..., nq + nkv:])], axis=-1)

    w2 = {}

    cos, sin = rope_tables(lp, n_valid)
    e128 = head_sum_matrix()
    e16 = jnp.kron(jnp.eye(N_Q_HEADS, dtype=F32), jnp.ones((HEAD_DIM, 1), F32)).astype(BF16)
    gq = jnp.tile(w['attn_q_gain'], (1, 2))[:, None, :]
    gk = jnp.tile(w['attn_k_gain'], (1, 2))[:, None, :]

    pad_rows = jnp.zeros((lp - n_valid, D_MODEL), F32)
    h = jnp.concatenate([meta_full, x[0], pad_rows], axis=0)
    tgt = jnp.concatenate([jnp.zeros((N_META, D_MODEL), F32), loss_target[0], pad_rows], axis=0)

    saved = []
    s5_names = ['s5_lambda_re', 's5_lambda_im', 's5_log_dt', 's5_b_re', 's5_b_im', 's5_c_re', 's5_c_im']
    for i in range(DEPTH):
        j = i // 2
        sv = {'h': h}
        if i % 2 == 0:
            ops, sv['prep_vjp'] = jax.vjp(s5_prep, *[w[n][j] for n in s5_names])
            m_, wx_, ci_, at_ = ops
            two = lambda t: t.reshape((2 * S5_NJ,) + t.shape[2:])
            sv['ops'] = (blockdiag_expand(m_, S5_CH, S5_CH, "s5_expand_m"),
                         blockdiag_expand(two(wx_), S5_CH, S5_STATE, "s5_expand_wx").reshape(2, S5_NJ, S5_W, S5_W),
                         blockdiag_expand(two(ci_), S5_STATE, S5_CH, "s5_expand_ci").reshape(2, S5_NJ, S5_W, S5_W), at_)
            pf = [shard[n][l] for n, l in uses[1]] if i == 0 else []
            y, sv['lhs'], sv['sp'], sv['sn'], *got = s5_forward(h, *sv['ops'], n_valid, prefetch=pf)
            if i == 0:
                unpack(1, got)
            sv['v'], sv['t'], sv['g'], sv['z'] = glu_forward(y, h, w['s5_d'][j][None], full['s5_w_glu', j])
            sv['r1'], h1 = proj_ln_forward(sv['z'], full['s5_w_out', j], h, ln_gain[i, 0], ln_bias[i, 0], "s5_out_ln")
        else:
            w2[j] = qkv_dup(full['attn_w_qkv', j])
            sv['raw'], sv['qs'], sv['k2'], sv['v2'], sv['qst'], v2t = qkv_forward(h, w2[j], gq[j], gk[j], cos, sin, e128)
            pf = [shard[n][l] for n, l in uses[2]] if i == 1 else []
            sv['o'], lse, *got = attn_forward_t(sv['qs'], sv['k2'], v2t, n_valid, prefetch=pf)
            if i == 1:
                unpack(2, got)
            sv['lse'] = lse.reshape(N_Q_HEADS, lp).T
            sv['r1'], h1 = proj_ln_forward(sv['o'], full['attn_w_out', j], h, ln_gain[i, 0], ln_bias[i, 0], "attn_out_ln")
        sv['h1'] = h1
        sv['a'], sv['b'], sv['f'] = ffn_up_forward(h1, full['ffn_w_gate', i], full['ffn_w_up', i])
        sv['r2'], h = proj_ln_forward(sv['f'], full['ffn_w_down', i], h1, ln_gain[i, 1], ln_bias[i, 1], "ffn_down_ln")
        saved.append(sv)

    dh, loss_part = loss_backward(h, tgt, n_valid)
    loss_local = jnp.sum(loss_part)

    gfull = {n: [None] * w[n].shape[0] for n in BIG}
    d_ln_gain = [[None, None] for _ in range(DEPTH)]
    d_ln_bias = [[None, None] for _ in range(DEPTH)]
    grep = {n: [None] * w[n].shape[0] for n in REPLICATED}
    pending = None
    for i in reversed(range(DEPTH)):
        j = i // 2
        sv = saved[i]
        if pending is None:
            dr2, s2 = ln_backward(dh, sv['r2'], ln_gain[i, 1])
        else:
            dr2, s2 = pending
            pending = None
        d_ln_gain[i][1], d_ln_bias[i][1] = s2[0], s2[1]
        da, db = ffn_backward_act(dr2, full['ffn_w_down', i], sv['a'], sv['b'])
        gfull['ffn_w_down'][i] = mm_tn(sv['f'], dr2, "grad_ffn_down")
        dr1, s1 = resid_nt(dr2, [da, db], [full['ffn_w_gate', i], full['ffn_w_up', i]], "ffn_backward_x",
                           ln=(sv['r1'], ln_gain[i, 0]))
        gfull['ffn_w_gate'][i] = mm_tn(sv['h1'], da, "grad_ffn_gate")
        gfull['ffn_w_up'][i] = mm_tn(sv['h1'], db, "grad_ffn_up")
        d_ln_gain[i][0], d_ln_bias[i][0] = s1[0], s1[1]
        if i % 2 == 0:
            dt, dgd = glu_backward1(dr1, full['s5_w_out', j], sv['g'], sv['t'])
            gfull['s5_w_out'][j] = mm_tn(sv['z'], dr1, "grad_s5_out")
            dv, dhs, sd = glu_backward2(dt, dgd, full['s5_w_glu', j], sv['v'], sv['h'], w['s5_d'][j][None], dr1)
            grep['s5_d'][j] = sd[0]
            gfull['s5_w_glu'][j] = mm_tn(sv['g'], dt, "grad_s5_glu")
            dh, ldy, dxf, dxr, daf, dar = s5_backward(dv, dhs, *sv['ops'], sv['sp'], sv['sn'], n_valid)
            dm = bmm_tn_compact(sv['lhs'], ldy, S5_CH, S5_CH, "grad_s5_m")
            dwx = jnp.stack([bmm_tn_compact(sv['lhs'], dxf, S5_CH, S5_STATE, "grad_s5_wxf"),
                             bmm_tn_compact(sv['lhs'], dxr, S5_CH, S5_STATE, "grad_s5_wxr")])
            dci = jnp.stack([bmm_tn_compact(sv['sp'], ldy, S5_STATE, S5_CH, "grad_s5_cif"),
                             bmm_tn_compact(sv['sn'], ldy, S5_STATE, S5_CH, "grad_s5_cir")])
            dps = sv['prep_vjp']((dm, dwx, dci, jnp.stack([daf, dar])))
            for n, g in zip(s5_names, dps):
                grep[n][j] = g
        else:
            do, delta, dot = attn_out_backward(dr1, full['attn_w_out', j], sv['o'], e16)
            gfull['attn_w_out'][j] = mm_tn(sv['o'], dr1, "grad_attn_out")
            dq, dk2, dv2 = attn_backward(sv['qs'], sv['qst'], sv['k2'], sv['v2'], do, dot, sv['lse'], delta, n_valid)
            draw, gs = qkv_backward(dq, dk2, dv2, sv['raw'], gq[j], gk[j], cos, sin, e128)
            grep['attn_q_gain'][j] = gs[0, :HEAD_DIM] + gs[0, HEAD_DIM:]
            grep['attn_k_gain'][j] = gs[1, :HEAD_DIM] + gs[1, HEAD_DIM:]
            pending = resid_nt(dr1, [draw], [w2[j]], "attn_backward_x", ln=(saved[i - 1]['r2'], ln_gain[i - 1, 1]))
            dw2 = mm_tn(sv['h'], draw, "grad_attn_qkv")
            kq = N_QB * LANES
            kk = N_KB * LANES
            gfull['attn_w_qkv'][j] = jnp.concatenate(
                [dw2[:, :kq], _fold_heads(dw2[:, kq:kq + kk]), _fold_heads(dw2[:, kq + kk:])], axis=1)
    grad_x = dh[N_META:n_valid][None]

    core = lax.axis_index("c")
    contrib = [_shard_major(jnp.stack(gfull[n]), n in ROW_SHARDED) for n in BIG]
    small_g = jnp.concatenate([dh[:N_META], jnp.stack([g for pair in d_ln_gain for g in pair]),
                               jnp.stack([g for pair in d_ln_bias for g in pair])], axis=0)
    contrib.append(jnp.transpose(small_g.reshape(-1, N_CHIPS, D_MODEL // N_CHIPS), (1, 0, 2)))
    rep_g = _pack_rep({n: jnp.stack(grep[n]) for n in REPLICATED}, loss_local)
    contrib.append(rep_g.reshape(N_CHIPS, -1, LANES))
    names = BIG + ['small', 'rep']
    wire = [BF16] * len(BIG) + [F32, F32]
    keep, give = [], []
    for t, dt in zip(contrib, wire):
        hn = t.shape[1] // 2
        keep.append(lax.dynamic_slice_in_dim(t, core * hn, hn, axis=1))
        give.append(lax.dynamic_slice_in_dim(t, (1 - core) * hn, hn, axis=1).astype(dt))
    got = sibling_exchange(give, "sibling_contrib")
    two_d = lambda t: t.reshape(-1, t.shape[-1])
    pair = [pair_sum(two_d(a), two_d(b), "pair_sum_" + n, dt).reshape(a.shape)
            for n, a, b, dt in zip(names, keep, got, wire)]
    recv = chip_exchange(pair, True, "scatter_grads")
    halves = [chip_sum(r.reshape(N_CHIPS, -1, r.shape[-1]), "chip_sum_" + n) for n, r in zip(names, recv)]
    others = sibling_exchange(halves, "sibling_halves")
    core_arr = jnp.reshape(core, (1,)).astype(jnp.int32)

    out = {}

    def update_halves(n, a, b, wn, mn, vn):
        shape = wn.shape
        flat = (-1, shape[-1])
        res = adamw_halves(a, b, core_arr, wn.reshape(flat), mn.reshape(flat), vn.reshape(flat), "adamw_" + n)
        return tuple(t.reshape(shape) for t in res)

    for n, a, b in zip(BIG, halves, others):
        out[n] = update_halves(n, a, b, w[n], mom[n], vel[n])
    cat = lambda tree: jnp.concatenate([tree[n].reshape(-1, tree[n].shape[-1]) for n in SMALL_SHARDED], axis=0)
    sm = update_halves("small", halves[-2], others[-2], cat(w), cat(mom), cat(vel))
    off = 0
    for n in SMALL_SHARDED:
        rows = math.prod(w[n].shape[:-1])
        out[n] = tuple(t[off:off + rows].reshape(w[n].shape) for t in sm)
        off += rows
    rep_quarter = jnp.where(core == 0, jnp.concatenate([halves[-1], others[-1]], axis=0),
                            jnp.concatenate([others[-1], halves[-1]], axis=0))

    def update(n, g, wn, mn, vn):
        shape = wn.shape
        flat = (-1, shape[-1])
        d, nm, nv = adamw(g, wn.reshape(flat), mn.reshape(flat), vn.reshape(flat), "adamw_" + n)
        return tuple(t.reshape(shape) for t in (g, d, nm, nv))

    rep_all = chip_exchange([rep_quarter], False, "gather_rep")[0].reshape(-1, LANES)
    rp = update("rep", rep_all, _pack_rep(w).reshape(-1, LANES), _pack_rep(mom).reshape(-1, LANES),
                _pack_rep(vel).reshape(-1, LANES))
    unpacked = [_unpack_rep(t.reshape(-1), w) for t in rp]
    loss = rp[0].reshape(-1)[sum(math.prod(w[n].shape) for n in REPLICATED)]
    for n in REPLICATED:
        out[n] = tuple(u[n] for u in unpacked)

    return (loss, grad_x, *[out[n][0] for n in WEIGHTS], *[out[n][1] for n in WEIGHTS],
            *[out[n][2] for n in WEIGHTS], *[out[n][3] for n in WEIGHTS])


def kernel(x, meta_tokens, s5_lambda_re, s5_lambda_im, s5_log_dt, s5_b_re, s5_b_im, s5_c_re, s5_c_im, s5_d, s5_w_glu, s5_w_out, attn_w_qkv, attn_q_gain, attn_k_gain, attn_w_out, ffn_w_gate, ffn_w_up, ffn_w_down, ln_gain, ln_bias, loss_target, m_meta_tokens, m_s5_lambda_re, m_s5_lambda_im, m_s5_log_dt, m_s5_b_re, m_s5_b_im, m_s5_c_re, m_s5_c_im, m_s5_d, m_s5_w_glu, m_s5_w_out, m_attn_w_qkv, m_attn_q_gain, m_attn_k_gain, m_attn_w_out, m_ffn_w_gate, m_ffn_w_up, m_ffn_w_down, m_ln_gain, m_ln_bias, v_meta_tokens, v_s5_lambda_re, v_s5_lambda_im, v_s5_log_dt, v_s5_b_re, v_s5_b_im, v_s5_c_re, v_s5_c_im, v_s5_d, v_s5_w_glu, v_s5_w_out, v_attn_w_qkv, v_attn_q_gain, v_attn_k_gain, v_attn_w_out, v_ffn_w_gate, v_ffn_w_up, v_ffn_w_down, v_ln_gain, v_ln_bias):
    given = locals()
    w = {n: given[n] for n in WEIGHTS}
    mom = {n: given["m_" + n] for n in WEIGHTS}
    vel = {n: given["v_" + n] for n in WEIGHTS}
    return _train_step(x, loss_target, w, mom, vel)
```

```python
import math

import jax
import jax.numpy as jnp
from jax import lax
from jax.experimental import pallas as pl
from jax.experimental.pallas import tpu as pltpu

F32 = jnp.float32
BF16 = jnp.bfloat16
MESH = pl.DeviceIdType.MESH

D_MODEL = 1024
N_META = 16
GRID_W = 64
HEAD_DIM = 64
N_Q_HEADS = 16
N_KV_HEADS = 4
ROPE_THETA = 10000.0
QK_EPS = 1e-6
S5_CH = 16
S5_GROUPS = 64
S5_STATE = 64
D_FF = 2816
LN_EPS = 1e-5
DEPTH = 4
ALPHA = (2.0 * DEPTH) ** 0.25
ADAM_LR, ADAM_B1, ADAM_B2, ADAM_EPS, ADAM_WD, ADAM_STEP = 0.001, 0.9, 0.999, 1e-08, 0.01, 10

LANES = 128
SUBLANES = 8
VMEM_LIMIT = 56 * 1024 * 1024

S5_T = 8
S5_GB = LANES // S5_CH
S5_NJ = S5_GROUPS // S5_GB
S5_W = S5_T * LANES
S5_SW = 2 * S5_GB * S5_STATE
S5_HALF = S5_SW // 2


def _round_up(a, b):
    return -(-a // b) * b


def _pick_tile(n, prefs):
    for t in prefs:
        if n % t == 0:
            return t
    return n


def _params(sem=None):
    kw = dict(vmem_limit_bytes=VMEM_LIMIT)
    if sem is not None:
        kw["dimension_semantics"] = sem
    return pltpu.CompilerParams(**kw)


def _dot(a, b, dims):
    return lax.dot_general(a, b, (dims, ((), ())), preferred_element_type=F32)


def _nn(a, b):
    return _dot(a, b, ((1,), (0,)))


def _nt(a, b):
    return _dot(a, b, ((1,), (1,)))


def _tn(a, b):
    return _dot(a, b, ((0,), (0,)))


def _compact(w):
    g, a0, a1, b0, b1 = w.shape
    w = jnp.transpose(w.reshape(S5_NJ, S5_GB, a0, a1, b0, b1), (0, 2, 1, 3, 4, 5))
    return w.reshape(S5_NJ, a0 * S5_GB * a1, b0 * b1)


def s5_prep(lam_re, lam_im, log_dt, b_re, b_im, c_re, c_im):
    hi = lax.Precision.HIGHEST
    t = S5_T
    dt = jnp.exp(log_dt)[..., None]
    taus = jnp.arange(t + 1, dtype=F32)[:, None, None, None]
    mag = jnp.exp(lam_re * dt)
    ang = lam_im * dt
    pr = jnp.concatenate([jnp.ones_like(mag)[None], (mag * jnp.cos(ang))[None],
                          jnp.exp(lam_re * dt * taus[2:]) * jnp.cos(ang * taus[2:])], axis=0)
    pi = jnp.concatenate([jnp.zeros_like(mag)[None], (mag * jnp.sin(ang))[None],
                          jnp.exp(lam_re * dt * taus[2:]) * jnp.sin(ang * taus[2:])], axis=0)
    abr, abi = pr[1], pi[1]
    nr, ni = abr - 1.0, abi
    den = lam_re * lam_re + lam_im * lam_im
    cr = (nr * lam_re + ni * lam_im) / den
    ci_ = (ni * lam_re - nr * lam_im) / den
    bbr = cr[..., None] * b_re - ci_[..., None] * b_im
    bbi = cr[..., None] * b_im + ci_[..., None] * b_re
    er = c_re[None] * pr[:, :, :, None, :] - c_im[None] * pi[:, :, :, None, :]
    ei = c_re[None] * pi[:, :, :, None, :] + c_im[None] * pr[:, :, :, None, :]
    nd, ng, ch = er.shape[1], er.shape[2], er.shape[3]
    lhs = jnp.concatenate([er[:t], -ei[:t]], axis=-1)
    lhs = jnp.transpose(lhs, (1, 2, 0, 3, 4)).reshape(nd, ng, t * ch, 2 * S5_STATE)
    rhs = jnp.concatenate([bbr, bbi], axis=-2)
    kk = jnp.einsum("dgmp,dgpc->dgmc", lhs, rhs, precision=hi)
    kk = jnp.transpose(kk.reshape(nd, ng, t, ch, ch), (2, 0, 1, 3, 4))
    zero = jnp.zeros_like(kk[0, 0])
    mg = jnp.stack([jnp.stack([(kk[i - s, 0] if i > s else zero) + (kk[s - i, 1] if s > i else zero)
                               + ((kk[0, 0] + kk[0, 1]) if i == s else zero) for i in range(t)])
                    for s in range(t)])
    mg = jnp.transpose(mg, (2, 0, 4, 1, 3))
    m = _compact(mg)
    pw_f = jnp.stack([pr[t - 1 - s, 0] for s in range(t)]), jnp.stack([pi[t - 1 - s, 0] for s in range(t)])
    pw_r = jnp.stack([pr[s, 1] for s in range(t)]), jnp.stack([pi[s, 1] for s in range(t)])
    wx = []
    for d, (qr, qi) in enumerate((pw_f, pw_r)):
        wr = qr[..., None] * bbr[d][None] - qi[..., None] * bbi[d][None]
        wi = qr[..., None] * bbi[d][None] + qi[..., None] * bbr[d][None]
        w = jnp.stack([wr, wi], axis=0)
        w = jnp.transpose(w, (2, 1, 4, 0, 3))
        wx.append(_compact(w))
    ci = []
    for d in range(2):
        exps = [i + 1 for i in range(t)] if d == 0 else [t - i for i in range(t)]
        e_r = jnp.stack([er[e, d] for e in exps])
        e_i = jnp.stack([ei[e, d] for e in exps])
        w = jnp.stack([e_r, -e_i], axis=0)
        w = jnp.transpose(w, (2, 0, 4, 1, 3))
        ci.append(_compact(w))
    at = jnp.stack([pr[t], pi[t]], axis=1)
    at = at.reshape(2, 2, S5_NJ, S5_GB * S5_STATE)
    at = jnp.transpose(at, (0, 2, 1, 3)).reshape(2, S5_NJ, 1, S5_SW)
    return m, jnp.stack(wx), jnp.stack(ci), at


def _chunk_rows(ref, nc):
    return jnp.concatenate([ref[pl.ds(s, nc, stride=S5_T), :] for s in range(S5_T)], axis=1)


def _cmul(ar, ai, sr, si):
    return ar * sr - ai * si, ar * si + ai * sr


def _scan_tiles(nc, reverse, step):
    nt = nc // SUBLANES

    def body(it, carry):
        tix = (nt - 1 - it) if reverse else it
        k0 = pl.multiple_of(tix * SUBLANES, SUBLANES)
        return step(k0, carry)

    return body, nt


def _s5_specs(nc):
    hbm = pl.BlockSpec(memory_space=pl.ANY)
    aspec = pl.BlockSpec((1, 1, S5_SW), lambda j: (j, 0, 0))
    cspec = pl.BlockSpec((1, nc, S5_W), lambda j: (j, 0, 0))
    return hbm, aspec, cspec


def _s5_fetch(j, tok_hbm, w_hbms, tok_s, w_s, sems):
    cols = pl.ds(pl.multiple_of(j * LANES, LANES), LANES)
    cps = [pltpu.make_async_copy(tok_hbm.at[:, cols], tok_s, sems.at[0])]
    for i, w in enumerate(w_hbms):
        cps.append(pltpu.make_async_copy(w.at[j], w_s.at[i], sems.at[1 + i]))
    for cp in cps:
        cp.start()
    return cols, cps


def s5_forward(u, m, wx, ci, at, n_valid, prefetch=()):
    lp = u.shape[0]
    nc = lp // S5_T
    nvc = n_valid // S5_T
    npf = len(prefetch)
    pf_specs, pf_shapes, pf_sems = _prefetch_specs(prefetch)

    def body(*refs):
        u_hbm, m_hbm, wxf_hbm, wxr_hbm, cif_hbm, cir_hbm, atf_ref, atr_ref = refs[:8]
        pf_ins = refs[8:8 + npf]
        y_hbm, lhs_ref, sp_ref, sn_ref = refs[8 + npf:12 + npf]
        pf_outs = refs[12 + npf:12 + 2 * npf]
        tok_s, w_s, xf_s, xr_s, sems = refs[12 + 2 * npf:17 + 2 * npf]
        j = pl.program_id(0)
        _prefetch_run(j == 0, j == S5_NJ - 1, pf_ins, pf_outs, refs[17 + 2 * npf:])
        cols, cps = _s5_fetch(j, u_hbm, (m_hbm, wxf_hbm, wxr_hbm, cif_hbm, cir_hbm), tok_s, w_s, sems)
        cps[0].wait()
        lhs = _chunk_rows(tok_s, nc)
        rows = lax.broadcasted_iota(jnp.int32, lhs.shape, 0)
        lhs = jnp.where(rows < nvc, lhs, 0.0).astype(BF16)
        lhs_ref[0] = lhs
        cps[2].wait()
        cps[3].wait()
        xf_s[...] = _nn(lhs, w_s[1])
        xr_s[...] = _nn(lhs, w_s[2])
        afr, afi = atf_ref[0, :, :S5_HALF], atf_ref[0, :, S5_HALF:]
        arr, ari = atr_ref[0, :, :S5_HALF], atr_ref[0, :, S5_HALF:]

        def scan_step(x_s, ar, ai, descending):
            def step(k0, carry):
                sr, si = carry
                x = x_s[pl.ds(k0, SUBLANES), :]
                outs = [None] * SUBLANES
                order = reversed(range(SUBLANES)) if descending else range(SUBLANES)
                for r in order:
                    outs[r] = jnp.concatenate([sr, si], axis=1)
                    nr, ni = _cmul(ar, ai, sr, si)
                    sr = nr + x[r:r + 1, :S5_HALF]
                    si = ni + x[r:r + 1, S5_HALF:]
                x_s[pl.ds(k0, SUBLANES), :] = jnp.concatenate(outs, axis=0)
                return sr, si
            return step

        zero = jnp.zeros((1, S5_HALF), F32)
        fb, nt = _scan_tiles(nc, False, scan_step(xf_s, afr, afi, False))
        lax.fori_loop(0, nt, fb, (zero, zero))
        rb, nt = _scan_tiles(nc, True, scan_step(xr_s, arr, ari, True))
        lax.fori_loop(0, nt, rb, (zero, zero))
        sp = xf_s[...].astype(BF16)
        sn = xr_s[...].astype(BF16)
        sp_ref[0] = sp
        sn_ref[0] = sn
        cps[1].wait()
        cps[4].wait()
        cps[5].wait()
        y = _nn(lhs, w_s[0]) + _nn(sp, w_s[3]) + _nn(sn, w_s[4])
        for i in range(S5_T):
            tok_s[pl.ds(i, nc, stride=S5_T), :] = y[:, i * LANES:(i + 1) * LANES]
        pltpu.sync_copy(tok_s, y_hbm.at[:, cols])

    hbm, aspec, cspec = _s5_specs(nc)
    return pl.pallas_call(
        body, name="s5_forward", grid=(S5_NJ,),
        in_specs=[hbm] * 6 + [aspec, aspec] + pf_specs,
        out_specs=[hbm, cspec, cspec, cspec] + pf_specs,
        out_shape=[jax.ShapeDtypeStruct((lp, D_MODEL), F32)] + [jax.ShapeDtypeStruct((S5_NJ, nc, S5_W), BF16)] * 3 + pf_shapes,
        scratch_shapes=[pltpu.VMEM((lp, LANES), F32), pltpu.VMEM((5, S5_W, S5_W), BF16),
                        pltpu.VMEM((nc, S5_SW), F32), pltpu.VMEM((nc, S5_SW), F32), pltpu.SemaphoreType.DMA((6,))] + pf_sems,
        compiler_params=_params(("arbitrary",)),
    )(u, m, wx[0], wx[1], ci[0], ci[1], at[0], at[1], *prefetch)


def s5_backward(dy, dhs, m, wx, ci, at, sp, sn, n_valid):
    lp = dy.shape[0]
    nc = lp // S5_T
    nvc = n_valid // S5_T

    def body(dy_hbm, dhs_hbm, m_hbm, wxf_hbm, wxr_hbm, cif_hbm, cir_hbm, atf_ref, atr_ref, sp_ref, sn_ref,
             dh_hbm, ldy_ref, dxf_ref, dxr_ref, daf_ref, dar_ref, tok_s, w_s, gf_s, gr_s, sems):
        j = pl.program_id(0)
        cols, cps = _s5_fetch(j, dy_hbm, (m_hbm, wxf_hbm, wxr_hbm, cif_hbm, cir_hbm), tok_s, w_s, sems)
        cps[0].wait()
        ldy = _chunk_rows(tok_s, nc)
        rows = lax.broadcasted_iota(jnp.int32, ldy.shape, 0)
        ldy = jnp.where(rows < nvc, ldy, 0.0).astype(BF16)
        ldy_ref[0] = ldy
        resid = pltpu.make_async_copy(dhs_hbm.at[:, cols], tok_s, sems.at[0])
        resid.start()
        cps[4].wait()
        cps[5].wait()
        gf_s[...] = _nt(ldy, w_s[3])
        gr_s[...] = _nt(ldy, w_s[4])
        afr, afi = atf_ref[0, :, :S5_HALF], atf_ref[0, :, S5_HALF:]
        arr, ari = atr_ref[0, :, :S5_HALF], atr_ref[0, :, S5_HALF:]

        def adj_step(g_s, s_ref, ar, ai, descending):
            def step(k0, carry):
                gr_, gi_, dr_, di_ = carry
                g = g_s[pl.ds(k0, SUBLANES), :]
                p = s_ref[0, pl.ds(k0, SUBLANES), :].astype(F32)
                outs = [None] * SUBLANES
                order = reversed(range(SUBLANES)) if descending else range(SUBLANES)
                for r in order:
                    outs[r] = jnp.concatenate([gr_, gi_], axis=1)
                    pr_, pi_ = p[r:r + 1, :S5_HALF], p[r:r + 1, S5_HALF:]
                    dr_ = dr_ + gr_ * pr_ + gi_ * pi_
                    di_ = di_ + gi_ * pr_ - gr_ * pi_
                    nr, ni = _cmul(ar, -ai, gr_, gi_)
                    gr_ = nr + g[r:r + 1, :S5_HALF]
                    gi_ = ni + g[r:r + 1, S5_HALF:]
                g_s[pl.ds(k0, SUBLANES), :] = jnp.concatenate(outs, axis=0)
                return gr_, gi_, dr_, di_
            return step

        zero = jnp.zeros((1, S5_HALF), F32)
        fb, nt = _scan_tiles(nc, True, adj_step(gf_s, sp_ref, afr, afi, True))
        _, _, dr_, di_ = lax.fori_loop(0, nt, fb, (zero,) * 4)
        daf_ref[0] = jnp.concatenate([dr_, di_], axis=1)
        rb, nt = _scan_tiles(nc, False, adj_step(gr_s, sn_ref, arr, ari, False))
        _, _, dr_, di_ = lax.fori_loop(0, nt, rb, (zero,) * 4)
        dar_ref[0] = jnp.concatenate([dr_, di_], axis=1)
        dxf = gf_s[...].astype(BF16)
        dxr = gr_s[...].astype(BF16)
        dxf_ref[0] = dxf
        dxr_ref[0] = dxr
        cps[1].wait()
        cps[2].wait()
        cps[3].wait()
        du = _nt(ldy, w_s[0]) + _nt(dxf, w_s[1]) + _nt(dxr, w_s[2])
        rows = lax.broadcasted_iota(jnp.int32, du.shape, 0)
        du = jnp.where(rows < nvc, du, 0.0)
        resid.wait()
        for s in range(S5_T):
            tok_s[pl.ds(s, nc, stride=S5_T), :] += du[:, s * LANES:(s + 1) * LANES]
        pltpu.sync_copy(tok_s, dh_hbm.at[:, cols])

    hbm, aspec, cspec = _s5_specs(nc)
    return pl.pallas_call(
        body, name="s5_backward", grid=(S5_NJ,),
        in_specs=[hbm] * 7 + [aspec, aspec, cspec, cspec],
        out_specs=[hbm, cspec, cspec, cspec, aspec, aspec],
        out_shape=[jax.ShapeDtypeStruct((lp, D_MODEL), F32)] + [jax.ShapeDtypeStruct((S5_NJ, nc, S5_W), BF16)] * 3
        + [jax.ShapeDtypeStruct((S5_NJ, 1, S5_SW), F32)] * 2,
        scratch_shapes=[pltpu.VMEM((lp, LANES), F32), pltpu.VMEM((5, S5_W, S5_W), BF16),
                        pltpu.VMEM((nc, S5_SW), F32), pltpu.VMEM((nc, S5_SW), F32), pltpu.SemaphoreType.DMA((6,))],
        compiler_params=_params(("arbitrary",)),
    )(dy, dhs, m, wx[0], wx[1], ci[0], ci[1], at[0], at[1], sp, sn)


S5_CW = LANES


def _replicate_matrix(b1):
    b0n = S5_CW // b1
    eye0 = jnp.eye(b0n, dtype=F32)
    eye1 = jnp.eye(b1, dtype=F32)
    r = jnp.einsum("ab,cd->acbd", eye0, eye1)[:, :, :, None, :]
    r = jnp.broadcast_to(r, (b0n, b1, b0n, S5_GB, b1))
    return r.reshape(S5_CW, b0n * S5_GB * b1).astype(BF16)


def _same_group(a1, b1):
    rg = (lax.broadcasted_iota(jnp.int32, (S5_W, S5_W), 0) // a1) % S5_GB
    cg = (lax.broadcasted_iota(jnp.int32, (S5_W, S5_W), 1) // b1) % S5_GB
    return rg == cg


def blockdiag_expand(compact, a1, b1, name):
    nj = compact.shape[0]

    def body(c_ref, r_ref, o_ref):
        rep = _nn(c_ref[0].astype(BF16), r_ref[...])
        o_ref[0] = jnp.where(_same_group(a1, b1), rep, 0.0).astype(BF16)

    return pl.pallas_call(
        body, name=name, grid=(nj,),
        in_specs=[pl.BlockSpec((1, S5_W, S5_CW), lambda j: (j, 0, 0)), _full_spec((S5_CW, S5_W))],
        out_specs=pl.BlockSpec((1, S5_W, S5_W), lambda j: (j, 0, 0)),
        out_shape=jax.ShapeDtypeStruct((nj, S5_W, S5_W), BF16),
        compiler_params=_params(("parallel",)),
    )(compact, _replicate_matrix(b1))


def bmm_tn_compact(a, b, a1, b1, name):
    nj, k, wa = a.shape
    wb = b.shape[2]

    def body(a_ref, b_ref, r_ref, o_ref):
        prod = jnp.where(_same_group(a1, b1), _tn(a_ref[0], b_ref[0]), 0.0)
        hi = prod.astype(BF16)
        lo = (prod - hi.astype(F32)).astype(BF16)
        o_ref[0] = _nt(hi, r_ref[...]) + _nt(lo, r_ref[...])

    return pl.pallas_call(
        body, name=name, grid=(nj,),
        in_specs=[pl.BlockSpec((1, k, wa), lambda j: (j, 0, 0)), pl.BlockSpec((1, k, wb), lambda j: (j, 0, 0)),
                  _full_spec((S5_CW, S5_W))],
        out_specs=pl.BlockSpec((1, wa, S5_CW), lambda j: (j, 0, 0)),
        out_shape=jax.ShapeDtypeStruct((nj, wa, S5_CW), F32),
        compiler_params=_params(("parallel",)),
    )(a, b, _replicate_matrix(b1))


def _tm(lp):
    return _pick_tile(lp, (768, 256))


def _row_spec(tm, width):
    return pl.BlockSpec((tm, width), lambda i: (i, 0))


def _full_spec(shape):
    return pl.BlockSpec(shape, lambda *_: (0,) * len(shape))


def _gelu(v):
    return 0.5 * v * (1.0 + lax.erf(v * (2.0 ** -0.5)))


def _gelu_grad(v):
    return 0.5 * (1.0 + lax.erf(v * (2.0 ** -0.5))) + v * jnp.exp(-0.5 * v * v) * (2.0 * math.pi) ** -0.5


def _layer_norm(r, gain, bias):
    mean = jnp.mean(r, axis=-1, keepdims=True)
    c = r - mean
    var = jnp.mean(c * c, axis=-1, keepdims=True)
    return c * lax.rsqrt(var + LN_EPS) * gain + bias


def glu_forward(y, h, dvec, wglu):
    lp, d = y.shape
    tm = _tm(lp)

    def body(y_ref, h_ref, d_ref, w_ref, v_ref, t_ref, g_ref, z_ref):
        v = y_ref[...] + d_ref[...] * h_ref[...]
        g = _gelu(v)
        gb = g.astype(BF16)
        t = _nn(gb, w_ref[...])
        v_ref[...] = v
        t_ref[...] = t
        g_ref[...] = gb
        z_ref[...] = (g * jax.nn.sigmoid(t)).astype(BF16)

    rs = _row_spec(tm, d)
    return pl.pallas_call(
        body, name="glu_forward", grid=(lp // tm,),
        in_specs=[rs, rs, _full_spec((1, d)), _full_spec((d, d))],
        out_specs=[rs, rs, rs, rs],
        out_shape=[jax.ShapeDtypeStruct((lp, d), F32)] * 2 + [jax.ShapeDtypeStruct((lp, d), BF16)] * 2,
        compiler_params=_params(("parallel",)),
    )(y, h, dvec, wglu)


def proj_ln_forward(z, w, h, gain, bias, name):
    lp, k = z.shape
    d = w.shape[1]
    tm = _tm(lp)

    def body(z_ref, w_ref, h_ref, g_ref, b_ref, r_ref, o_ref):
        r = ALPHA * h_ref[...] + _nn(z_ref[...], w_ref[...])
        r_ref[...] = r
        o_ref[...] = _layer_norm(r, g_ref[...], b_ref[...])

    rs = _row_spec(tm, d)
    return pl.pallas_call(
        body, name=name, grid=(lp // tm,),
        in_specs=[_row_spec(tm, k), _full_spec((k, d)), rs, _full_spec((1, d)), _full_spec((1, d))],
        out_specs=[rs, rs],
        out_shape=[jax.ShapeDtypeStruct((lp, d), F32)] * 2,
        compiler_params=_params(("parallel",)),
    )(z, w, h, gain, bias)


FFN_TM = 384


def ffn_up_forward(h, wg, wu):
    lp, d = h.shape
    dff = wg.shape[1]
    tm = _pick_tile(lp, (FFN_TM, 256))

    def body(h_ref, wg_ref, wu_ref, a_ref, b_ref, f_ref):
        hb = h_ref[...].astype(BF16)
        a = _nn(hb, wg_ref[...])
        b = _nn(hb, wu_ref[...])
        a_ref[...] = a.astype(BF16)
        b_ref[...] = b.astype(BF16)
        f_ref[...] = (a * jax.nn.sigmoid(a) * b).astype(BF16)

    ws = _full_spec((d, dff))
    os_ = _row_spec(tm, dff)
    return pl.pallas_call(
        body, name="ffn_up_forward", grid=(lp // tm,),
        in_specs=[_row_spec(tm, d), ws, ws],
        out_specs=[os_, os_, os_],
        out_shape=[jax.ShapeDtypeStruct((lp, dff), BF16)] * 3,
        compiler_params=_params(("parallel",)),
    )(h, wg, wu)


def _ln_backward_rows(dh_, r_, gain):
    mean = jnp.mean(r_, axis=-1, keepdims=True)
    c = r_ - mean
    var = jnp.mean(c * c, axis=-1, keepdims=True)
    rstd = lax.rsqrt(var + LN_EPS)
    xh = c * rstd
    dxh = dh_ * gain
    m1 = jnp.mean(dxh, axis=-1, keepdims=True)
    m2 = jnp.mean(dxh * xh, axis=-1, keepdims=True)
    return rstd * (dxh - m1 - xh * m2), jnp.sum(dh_ * xh, axis=0, keepdims=True), jnp.sum(dh_, axis=0, keepdims=True)


def _accumulate_ln_sums(s_ref, sg, sb):
    @pl.when(pl.program_id(0) == 0)
    def _():
        s_ref[...] = jnp.zeros_like(s_ref)

    s_ref[0:1, :] += sg
    s_ref[1:2, :] += sb


def ln_backward(dh, r, gain):
    lp, d = dh.shape
    tm = _tm(lp)

    def body(dh_ref, r_ref, g_ref, dr_ref, s_ref):
        dr, sg, sb = _ln_backward_rows(dh_ref[...], r_ref[...], g_ref[...])
        dr_ref[...] = dr
        _accumulate_ln_sums(s_ref, sg, sb)

    rs = _row_spec(tm, d)
    return pl.pallas_call(
        body, name="ln_backward", grid=(lp // tm,),
        in_specs=[rs, rs, _full_spec((1, d))],
        out_specs=[rs, _full_spec((SUBLANES, d))],
        out_shape=[jax.ShapeDtypeStruct((lp, d), F32), jax.ShapeDtypeStruct((SUBLANES, d), F32)],
        compiler_params=_params(("arbitrary",)),
    )(dh, r, gain)


def ffn_backward_act(dr, wd, a, b):
    lp, d = dr.shape
    dff = wd.shape[0]
    tm = _pick_tile(lp, (FFN_TM, 256))

    def body(dr_ref, wd_ref, a_ref, b_ref, da_ref, db_ref):
        df = _nt(dr_ref[...].astype(BF16), wd_ref[...])
        a_ = a_ref[...].astype(F32)
        b_ = b_ref[...].astype(F32)
        sg = jax.nn.sigmoid(a_)
        da_ref[...] = (df * b_ * sg * (1.0 + a_ * (1.0 - sg))).astype(BF16)
        db_ref[...] = (df * a_ * sg).astype(BF16)

    os_ = _row_spec(tm, dff)
    return pl.pallas_call(
        body, name="ffn_backward_act", grid=(lp // tm,),
        in_specs=[_row_spec(tm, d), _full_spec((dff, d)), os_, os_],
        out_specs=[os_, os_],
        out_shape=[jax.ShapeDtypeStruct((lp, dff), BF16)] * 2,
        compiler_params=_params(("parallel",)),
    )(dr, wd, a, b)


def resid_nt(dr, xs, ws, name, ln=None):
    lp, d = dr.shape
    tm = _pick_tile(lp, (FFN_TM, 256))
    n = len(xs)

    def body(*refs):
        acc = ALPHA * refs[0][...]
        for i in range(n):
            acc = acc + _nt(refs[1 + i][...], refs[1 + n + i][...])
        if ln is None:
            refs[-1][...] = acc
        else:
            r_ref, g_ref, o_ref, s_ref = refs[1 + 2 * n:]
            dr_, sg, sb = _ln_backward_rows(acc, r_ref[...], g_ref[...])
            o_ref[...] = dr_
            _accumulate_ln_sums(s_ref, sg, sb)

    rs = _row_spec(tm, d)
    in_specs = [rs] + [_row_spec(tm, x.shape[1]) for x in xs] + [_full_spec(w.shape) for w in ws]
    if ln is None:
        return pl.pallas_call(
            body, name=name, grid=(lp // tm,),
            in_specs=in_specs, out_specs=rs,
            out_shape=jax.ShapeDtypeStruct((lp, d), F32),
            compiler_params=_params(("parallel",)),
        )(dr, *xs, *ws)
    return pl.pallas_call(
        body, name=name, grid=(lp // tm,),
        in_specs=in_specs + [rs, _full_spec((1, d))], out_specs=[rs, _full_spec((SUBLANES, d))],
        out_shape=[jax.ShapeDtypeStruct((lp, d), F32), jax.ShapeDtypeStruct((SUBLANES, d), F32)],
        compiler_params=_params(("arbitrary",)),
    )(dr, *xs, *ws, *ln)


def mm_tn(x, y, name, layers=1, layer=0, stack=None):
    lp, k = x.shape
    n = y.shape[1]
    tm = _tm(lp)

    def body(x_ref, y_ref, *rest):
        o_ref = rest[-1]

        @pl.when(pl.program_id(0) == 0)
        def _():
            o_ref[...] = jnp.zeros_like(o_ref)

        o_ref[0] += _tn(x_ref[...].astype(BF16), y_ref[...].astype(BF16))

    in_specs = [pl.BlockSpec((tm, k), lambda i: (i, 0)), pl.BlockSpec((tm, n), lambda i: (i, 0))]
    args = [x, y]
    aliases = {}
    if stack is not None:
        in_specs.append(pl.BlockSpec(memory_space=pl.ANY))
        args.append(stack)
        aliases = {2: 0}
    return pl.pallas_call(
        body, name=name, grid=(lp // tm,),
        in_specs=in_specs,
        out_specs=pl.BlockSpec((1, k, n), lambda i: (layer, 0, 0)),
        out_shape=jax.ShapeDtypeStruct((layers, k, n), F32),
        input_output_aliases=aliases,
        compiler_params=_params(("arbitrary",)),
    )(*args)


def glu_backward1(dr, wout, g, t):
    lp, d = dr.shape
    tm = _tm(lp)

    def body(dr_ref, w_ref, g_ref, t_ref, dt_ref, dgd_ref):
        dz = _nt(dr_ref[...].astype(BF16), w_ref[...])
        s = jax.nn.sigmoid(t_ref[...])
        dgd_ref[...] = dz * s
        dt_ref[...] = (dz * g_ref[...].astype(F32) * s * (1.0 - s)).astype(BF16)

    rs = _row_spec(tm, d)
    return pl.pallas_call(
        body, name="glu_backward1", grid=(lp // tm,),
        in_specs=[rs, _full_spec((d, d)), rs, rs],
        out_specs=[rs, rs],
        out_shape=[jax.ShapeDtypeStruct((lp, d), BF16), jax.ShapeDtypeStruct((lp, d), F32)],
        compiler_params=_params(("parallel",)),
    )(dr, wout, g, t)


def glu_backward2(dt, dgd, wglu, v, h, dvec, dr):
    lp, d = dt.shape
    tm = _tm(lp)

    def body(dt_ref, dgd_ref, w_ref, v_ref, h_ref, d_ref, dr_ref, dv_ref, dhs_ref, s_ref):
        dg = dgd_ref[...] + _nt(dt_ref[...], w_ref[...])
        dv = dg * _gelu_grad(v_ref[...])
        dv_ref[...] = dv
        dhs_ref[...] = ALPHA * dr_ref[...] + dv * d_ref[...]

        @pl.when(pl.program_id(0) == 0)
        def _():
            s_ref[...] = jnp.zeros_like(s_ref)

        s_ref[0:1, :] += jnp.sum(dv * h_ref[...], axis=0, keepdims=True)

    rs = _row_spec(tm, d)
    return pl.pallas_call(
        body, name="glu_backward2", grid=(lp // tm,),
        in_specs=[rs, rs, _full_spec((d, d)), rs, rs, _full_spec((1, d)), rs],
        out_specs=[rs, rs, _full_spec((SUBLANES, d))],
        out_shape=[jax.ShapeDtypeStruct((lp, d), F32)] * 2 + [jax.ShapeDtypeStruct((SUBLANES, d), F32)],
        compiler_params=_params(("arbitrary",)),
    )(dt, dgd, wglu, v, h, dvec, dr)


def loss_backward(hf, tgt, n_valid):
    lp, d = hf.shape
    tm = _tm(lp)

    def body(h_ref, t_ref, dh_ref, s_ref):
        rows = pl.program_id(0) * tm + lax.broadcasted_iota(jnp.int32, (tm, d), 0)
        ok = (rows >= N_META) & (rows < n_valid)
        e = jnp.where(ok, h_ref[...] - t_ref[...], 0.0)
        dh_ref[...] = e * (1.0 / d)

        @pl.when(pl.program_id(0) == 0)
        def _():
            s_ref[...] = jnp.zeros_like(s_ref)

        sq = e * e
        part = sq[:, 0:LANES]
        for c in range(1, d // LANES):
            part = part + sq[:, c * LANES:(c + 1) * LANES]
        acc = part[0:SUBLANES]
        for r in range(1, tm // SUBLANES):
            acc = acc + part[r * SUBLANES:(r + 1) * SUBLANES]
        s_ref[...] += acc * (0.5 / d)

    rs = _row_spec(tm, d)
    return pl.pallas_call(
        body, name="loss_backward", grid=(lp // tm,),
        in_specs=[rs, rs], out_specs=[rs, _full_spec((SUBLANES, LANES))],
        out_shape=[jax.ShapeDtypeStruct((lp, d), F32), jax.ShapeDtypeStruct((SUBLANES, LANES), F32)],
        compiler_params=_params(("arbitrary",)),
    )(hf, tgt)


N_QB = N_Q_HEADS // 2
N_KB = N_KV_HEADS
QKV_W = (N_QB + 2 * N_KB) * LANES
Q_SCALE = HEAD_DIM ** -0.5 * math.log2(math.e)


def rope_tables(lp, n_valid):
    t = jnp.arange(lp, dtype=jnp.int32)
    real = (t >= N_META) & (t < n_valid)
    pos = jnp.where(real, t - N_META, 0)
    row = (pos // GRID_W).astype(F32)
    col = (pos % GRID_W).astype(F32)
    axis_dim = HEAD_DIM // 2
    inv = ROPE_THETA ** (-jnp.arange(0, axis_dim, 2, dtype=F32) / axis_dim)
    ar = row[:, None] * inv[None, :]
    ac = col[:, None] * inv[None, :]
    cos = jnp.concatenate([jnp.cos(ar), jnp.cos(ar), jnp.cos(ac), jnp.cos(ac)], axis=1)
    sin = jnp.concatenate([-jnp.sin(ar), jnp.sin(ar), -jnp.sin(ac), jnp.sin(ac)], axis=1)
    return jnp.tile(cos, (1, 2)), jnp.tile(sin, (1, 2))


def head_sum_matrix():
    return jnp.kron(jnp.eye(2, dtype=F32), jnp.ones((HEAD_DIM, HEAD_DIM), F32)).astype(BF16)


def _segsum(x, e):
    hi = x.astype(BF16)
    lo = (x - hi.astype(F32)).astype(BF16)
    return _nn(hi, e) + _nn(lo, e)


def _swap_halves(x):
    lane = lax.broadcasted_iota(jnp.int32, x.shape, 1)
    quarter = HEAD_DIM // 4
    return jnp.where(lane % (2 * quarter) < quarter, pltpu.roll(x, LANES - quarter, 1), pltpu.roll(x, quarter, 1))


def qkv_forward(h, w2, gq, gk, cos, sin, e):
    lp, d = h.shape
    tm = _tm(lp)
    kw, vw = N_KB * LANES, N_KB * LANES

    def body(h_ref, w_ref, gq_ref, gk_ref, cos_ref, sin_ref, e_ref, raw_ref, q_ref, k_ref, v_ref, qt_ref, vt_ref):
        raw = _nn(h_ref[...].astype(BF16), w_ref[...])
        raw_ref[...] = raw
        c, s_, em = cos_ref[...], sin_ref[...], e_ref[...]
        for cb in range(N_QB + N_KB):
            t = raw[:, cb * LANES:(cb + 1) * LANES]
            rstd = lax.rsqrt(_segsum(t * t, em) * (1.0 / HEAD_DIM) + QK_EPS)
            n = t * rstd * (gq_ref[...] if cb < N_QB else gk_ref[...])
            rot = n * c + _swap_halves(n) * s_
            if cb < N_QB:
                qs = rot * Q_SCALE
                q_ref[:, cb * LANES:(cb + 1) * LANES] = qs.astype(BF16)
                qt_ref[cb * LANES:(cb + 1) * LANES, :] = qs.T.astype(BF16)
            else:
                k_ref[:, (cb - N_QB) * LANES:(cb - N_QB + 1) * LANES] = rot.astype(BF16)
        v_ref[...] = raw[:, (N_QB + N_KB) * LANES:].astype(BF16)
        for cb in range(N_KB):
            lo = (N_QB + N_KB + cb) * LANES
            vt_ref[cb * LANES:(cb + 1) * LANES, :] = raw[:, lo:lo + LANES].T.astype(BF16)

    col_spec = lambda rows: pl.BlockSpec((rows, tm), lambda i: (0, i))
    return pl.pallas_call(
        body, name="qkv_forward", grid=(lp // tm,),
        in_specs=[_row_spec(tm, d), _full_spec((d, QKV_W)), _full_spec((1, LANES)), _full_spec((1, LANES)),
                  _row_spec(tm, LANES), _row_spec(tm, LANES), _full_spec((LANES, LANES))],
        out_specs=[_row_spec(tm, QKV_W), _row_spec(tm, N_QB * LANES), _row_spec(tm, kw), _row_spec(tm, vw),
                   col_spec(N_QB * LANES), col_spec(vw)],
        out_shape=[jax.ShapeDtypeStruct((lp, QKV_W), F32), jax.ShapeDtypeStruct((lp, N_QB * LANES), BF16),
                   jax.ShapeDtypeStruct((lp, kw), BF16), jax.ShapeDtypeStruct((lp, vw), BF16),
                   jax.ShapeDtypeStruct((N_QB * LANES, lp), BF16), jax.ShapeDtypeStruct((vw, lp), BF16)],
        compiler_params=_params(("parallel",)),
    )(h, w2, gq, gk, cos, sin, e)


def qkv_backward(dqs, dk2, dv2, raw, gq, gk, cos, sin, e):
    lp = raw.shape[0]
    tm = _tm(lp)

    def body(dq_ref, dk_ref, dv_ref, raw_ref, gq_ref, gk_ref, cos_ref, sin_ref, e_ref, d_ref, s_ref):
        @pl.when(pl.program_id(0) == 0)
        def _():
            s_ref[...] = jnp.zeros_like(s_ref)

        c, s_, em = cos_ref[...], sin_ref[...], e_ref[...]
        gsum = [jnp.zeros((1, LANES), F32), jnp.zeros((1, LANES), F32)]
        for cb in range(N_QB + N_KB):
            isq = cb < N_QB
            t = raw_ref[:, cb * LANES:(cb + 1) * LANES]
            if isq:
                drot = dq_ref[:, cb * LANES:(cb + 1) * LANES] * (HEAD_DIM ** -0.5)
            else:
                drot = dk_ref[:, (cb - N_QB) * LANES:(cb - N_QB + 1) * LANES] * math.log(2.0)
            gain = gq_ref[...] if isq else gk_ref[...]
            rstd = lax.rsqrt(_segsum(t * t, em) * (1.0 / HEAD_DIM) + QK_EPS)
            dn = drot * c + _swap_halves(drot * s_)
            xh = t * rstd
            gsum[0 if isq else 1] = gsum[0 if isq else 1] + jnp.sum(dn * xh, axis=0, keepdims=True)
            w = dn * gain
            mw = _segsum(w * xh, em) * (1.0 / HEAD_DIM)
            d_ref[:, cb * LANES:(cb + 1) * LANES] = (rstd * (w - xh * mw)).astype(BF16)
        d_ref[:, (N_QB + N_KB) * LANES:] = dv_ref[...].astype(BF16)
        s_ref[0:1, :] += gsum[0]
        s_ref[1:2, :] += gsum[1]

    kw = N_KB * LANES
    return pl.pallas_call(
        body, name="qkv_backward", grid=(lp // tm,),
        in_specs=[_row_spec(tm, N_QB * LANES), _row_spec(tm, kw), _row_spec(tm, kw), _row_spec(tm, QKV_W),
                  _full_spec((1, LANES)), _full_spec((1, LANES)), _row_spec(tm, LANES), _row_spec(tm, LANES),
                  _full_spec((LANES, LANES))],
        out_specs=[_row_spec(tm, QKV_W), _full_spec((SUBLANES, LANES))],
        out_shape=[jax.ShapeDtypeStruct((lp, QKV_W), BF16), jax.ShapeDtypeStruct((SUBLANES, LANES), F32)],
        compiler_params=_params(("arbitrary",)),
    )(dqs, dk2, dv2, raw, gq, gk, cos, sin, e)


NEG = -1e30
Q_PER_KV = N_Q_HEADS // N_KV_HEADS


def _half_masks(x):
    lane = lax.broadcasted_iota(jnp.int32, x.shape, 1)
    zero = jnp.zeros_like(x)
    return jnp.where(lane < HEAD_DIM, x, zero), jnp.where(lane >= HEAD_DIM, x, zero)


ATTN_TR = 16


def _attn_tiles(lp):
    t = _pick_tile(lp, (1408, 256))
    return t, t


def _attn_tiles_bwd(lp):
    return _pick_tile(lp, (768, 256)), _pick_tile(lp, (1408, 256))


def attn_forward_t(qs, k2, v2t, n_valid, prefetch=()):
    lp = qs.shape[0]
    tq, kb = _attn_tiles(lp)
    nk = lp // kb
    gw = 2 * LANES
    nr = kb // ATTN_TR
    pad0 = n_valid - (nk - 1) * kb
    npf = len(prefetch)
    pf_specs, pf_shapes, pf_sems = _prefetch_specs(prefetch)
    nq = lp // tq

    def body(*refs):
        q_ref, k_ref, vt_ref = refs[:3]
        pf_ins = refs[3:3 + npf]
        o_ref, lse_ref = refs[3 + npf:5 + npf]
        pf_outs = refs[5 + npf:5 + 2 * npf]
        m_s, l_s, acc_s, s_s, p_s = refs[5 + 2 * npf:10 + 2 * npf]
        j = pl.program_id(2)
        first = (pl.program_id(0) == 0) & (pl.program_id(1) == 0) & (j == 0)
        last = (pl.program_id(0) == N_KV_HEADS - 1) & (pl.program_id(1) == nq - 1) & (j == nk - 1)
        _prefetch_run(first, last, pf_ins, pf_outs, refs[10 + 2 * npf:])

        @pl.when(j == 0)
        def _():
            m_s[...] = jnp.full_like(m_s, NEG)
            l_s[...] = jnp.zeros_like(l_s)
            acc_s[...] = jnp.zeros_like(acc_s)

        ks = _half_masks(k_ref[...])
        for pair in range(2):
            qp = q_ref[:, pair * LANES:(pair + 1) * LANES]
            for half in range(2):
                hh = 2 * pair + half
                s_s[...] = _nt(ks[half], qp)

                if pad0 < kb:
                    @pl.when(j == nk - 1)
                    def _():
                        s_s[pad0:, :] = jnp.full((kb - pad0, tq), NEG, F32)

                run = s_s[pl.ds(0, ATTN_TR), :]
                for r in range(1, nr):
                    run = jnp.maximum(run, s_s[pl.ds(r * ATTN_TR, ATTN_TR), :])
                m_prev = m_s[hh:hh + 1, :]
                m_new = jnp.maximum(m_prev, jnp.max(run, axis=0, keepdims=True))
                alpha = jnp.exp2(m_prev - m_new)
                m_s[hh:hh + 1, :] = m_new
                for r in range(nr):
                    rows = pl.ds(r * ATTN_TR, ATTN_TR)
                    p_s[rows, :] = jnp.exp2(s_s[rows, :] - m_new).astype(BF16)
                vt = jnp.concatenate([vt_ref[half * HEAD_DIM:(half + 1) * HEAD_DIM, :],
                                      jnp.ones((ATTN_TR, kb), BF16)], axis=0)
                pv = _nn(vt, p_s[...])
                l_s[hh:hh + 1, :] = alpha * l_s[hh:hh + 1, :] + pv[HEAD_DIM:HEAD_DIM + 1, :]
                pv = pv[:HEAD_DIM, :]
                rs = slice(half * HEAD_DIM, (half + 1) * HEAD_DIM)
                acc_s[pair, rs, :] = alpha * acc_s[pair, rs, :] + pv

        @pl.when(j == nk - 1)
        def _():
            for pair in range(2):
                for half in range(2):
                    hh = 2 * pair + half
                    rs = slice(half * HEAD_DIM, (half + 1) * HEAD_DIM)
                    acc_s[pair, rs, :] = acc_s[pair, rs, :] * (1.0 / l_s[hh:hh + 1, :])
                o_ref[:, pair * LANES:(pair + 1) * LANES] = acc_s[pair].T.astype(BF16)
            for hh in range(Q_PER_KV):
                lse_ref[0, hh] = m_s[hh:hh + 1, :] + jnp.log2(l_s[hh:hh + 1, :])

    return pl.pallas_call(
        body, name="attn_forward", grid=(N_KV_HEADS, lp // tq, nk),
        in_specs=[pl.BlockSpec((tq, gw), lambda g, i, j: (i, g)), pl.BlockSpec((kb, LANES), lambda g, i, j: (j, g)),
                  pl.BlockSpec((LANES, kb), lambda g, i, j: (g, j))] + pf_specs,
        out_specs=[pl.BlockSpec((tq, gw), lambda g, i, j: (i, g)),
                   pl.BlockSpec((1, Q_PER_KV, 1, tq), lambda g, i, j: (g, 0, 0, i))] + pf_specs,
        out_shape=[jax.ShapeDtypeStruct((lp, N_QB * LANES), BF16),
                   jax.ShapeDtypeStruct((N_KV_HEADS, Q_PER_KV, 1, lp), F32)] + pf_shapes,
        scratch_shapes=[pltpu.VMEM((SUBLANES, tq), F32), pltpu.VMEM((SUBLANES, tq), F32),
                        pltpu.VMEM((2, LANES, tq), F32), pltpu.VMEM((kb, tq), F32), pltpu.VMEM((kb, tq), BF16)] + pf_sems,
        compiler_params=_params(("arbitrary",) * 3 if npf else ("parallel", "parallel", "arbitrary")),
    )(qs, k2, v2t, *prefetch)


def attn_backward(qs, qst, k2, v2, do, dot, lse, delta, n_valid):
    lp = qs.shape[0]
    tq, kb = _attn_tiles_bwd(lp)
    nq, nk = lp // tq, lp // kb
    gw = 2 * LANES
    pad0 = n_valid - (nk - 1) * kb

    def body(q_ref, qt_ref, k_ref, v_ref, do_ref, dot_ref, lse_ref, dl_ref, dq_ref, dk_ref, dv_ref, acc_s, dkt_s, dvt_s):
        g = pl.program_id(0)
        i = pl.program_id(1)
        j = pl.program_id(2)
        cols = pl.ds(pl.multiple_of(j * kb, kb), kb)

        @pl.when(j == 0)
        def _():
            acc_s[...] = jnp.zeros_like(acc_s)

        @pl.when(i == 0)
        def _():
            dkt_s[:, cols] = jnp.zeros((LANES, kb), F32)
            dvt_s[:, cols] = jnp.zeros((LANES, kb), F32)

        head = lax.broadcasted_iota(jnp.int32, (tq, N_Q_HEADS), 1)

        def column(ref, hh):
            return jnp.sum(jnp.where(head == Q_PER_KV * g + hh, ref[...], 0.0), axis=1, keepdims=True)

        def step(masked):
            ks = _half_masks(k_ref[...])
            vs = _half_masks(v_ref[...])
            if masked:
                col = lax.broadcasted_iota(jnp.int32, (1, kb), 1)
                bias = jnp.where(col < pad0, 0.0, NEG)
            for pair in range(2):
                qp = q_ref[:, pair * LANES:(pair + 1) * LANES]
                dop = do_ref[:, pair * LANES:(pair + 1) * LANES]
                for half in range(2):
                    hh = 2 * pair + half
                    rs = slice(half * HEAD_DIM, (half + 1) * HEAD_DIM)
                    rt = slice(pair * LANES + half * HEAD_DIM, pair * LANES + (half + 1) * HEAD_DIM)
                    s = _nt(qp, ks[half])
                    if masked:
                        s = s + bias
                    p = jnp.exp2(s - column(lse_ref, hh))
                    dp = _nt(dop, vs[half])
                    ds = (p * (dp - column(dl_ref, hh))).astype(BF16)
                    pb = p.astype(BF16)
                    acc_s[pair] += _nn(ds, ks[half])
                    dvt_s[rs, cols] += _nn(dot_ref[rt, :], pb)
                    dkt_s[rs, cols] += _nn(qt_ref[rt, :], ds)

        if pad0 < kb:
            pl.when(j < nk - 1)(lambda: step(False))
            pl.when(j == nk - 1)(lambda: step(True))
        else:
            step(False)

        @pl.when(j == nk - 1)
        def _():
            for pair in range(2):
                dq_ref[:, pair * LANES:(pair + 1) * LANES] = acc_s[pair]

        @pl.when(i == nq - 1)
        def _():
            dk_ref[cols, :] = dkt_s[:, cols].T
            dv_ref[cols, :] = dvt_s[:, cols].T

    cspec = pl.BlockSpec((tq, N_Q_HEADS), lambda g, i, j: (i, 0))
    qspec = pl.BlockSpec((tq, gw), lambda g, i, j: (i, g))
    tspec = pl.BlockSpec((gw, tq), lambda g, i, j: (g, i))
    kspec = pl.BlockSpec((kb, LANES), lambda g, i, j: (j, g))
    gspec = pl.BlockSpec((lp, LANES), lambda g, i, j: (0, g))
    return pl.pallas_call(
        body, name="attn_backward", grid=(N_KV_HEADS, nq, nk),
        in_specs=[qspec, tspec, kspec, kspec, qspec, tspec, cspec, cspec],
        out_specs=[qspec, gspec, gspec],
        out_shape=[jax.ShapeDtypeStruct((lp, N_QB * LANES), F32),
                   jax.ShapeDtypeStruct((lp, N_KB * LANES), F32), jax.ShapeDtypeStruct((lp, N_KB * LANES), F32)],
        scratch_shapes=[pltpu.VMEM((2, tq, LANES), F32), pltpu.VMEM((LANES, lp), F32), pltpu.VMEM((LANES, lp), F32)],
        compiler_params=_params(("parallel", "arbitrary", "arbitrary")),
    )(qs, qst, k2, v2, do, dot, lse, delta)


def attn_out_backward(dr, wout, o, e16):
    lp, d = dr.shape
    tm = _tm(lp)

    def body(dr_ref, w_ref, o_ref, e_ref, do_ref, dl_ref, dot_ref):
        do32 = _nt(dr_ref[...].astype(BF16), w_ref[...])
        do = do32.astype(BF16)
        do_ref[...] = do
        for cb in range(d // LANES):
            dot_ref[cb * LANES:(cb + 1) * LANES, :] = do32[:, cb * LANES:(cb + 1) * LANES].T.astype(BF16)
        dl_ref[...] = _segsum(do.astype(F32) * o_ref[...].astype(F32), e_ref[...])

    rs = _row_spec(tm, d)
    return pl.pallas_call(
        body, name="attn_out_backward", grid=(lp // tm,),
        in_specs=[rs, _full_spec((d, d)), rs, _full_spec((d, N_Q_HEADS))],
        out_specs=[rs, _row_spec(tm, N_Q_HEADS), pl.BlockSpec((d, tm), lambda i: (0, i))],
        out_shape=[jax.ShapeDtypeStruct((lp, d), BF16), jax.ShapeDtypeStruct((lp, N_Q_HEADS), F32),
                   jax.ShapeDtypeStruct((d, lp), BF16)],
        compiler_params=_params(("parallel",)),
    )(dr, wout, o, e16)


N_CHIPS = 4


def _mesh_pos():
    return lax.axis_index("x"), lax.axis_index("y"), lax.axis_index("c")


def chip_exchange(arrs, scatter, name):
    n = len(arrs)
    hbm = pl.BlockSpec(memory_space=pl.ANY)

    def body(*refs):
        ins, outs = refs[:n], refs[n:2 * n]
        send_sems, recv_sems, loc_sems = refs[2 * n:]
        x, y, c = _mesh_pos()
        me = 2 * x + y
        chips = [(1 - x, y), (x, 1 - y), (1 - x, 1 - y)]
        started = []
        for a in range(n):
            loc = pltpu.make_async_copy(ins[a].at[me] if scatter else ins[a], outs[a].at[me], loc_sems.at[a])
            loc.start()
            started.append(loc)
            for k, (px, py) in enumerate(chips):
                src = ins[a].at[2 * px + py] if scatter else ins[a]
                cp = pltpu.make_async_remote_copy(
                    src_ref=src, dst_ref=outs[a].at[me], send_sem=send_sems.at[3 * a + k], recv_sem=recv_sems.at[3 * a + k],
                    device_id=(px, py, c), device_id_type=MESH)
                cp.start()
                started.append(cp)
        for cp in started:
            cp.wait()

    out_shape = [jax.ShapeDtypeStruct(a.shape if scatter else (N_CHIPS,) + a.shape, a.dtype) for a in arrs]
    return pl.pallas_call(
        body, name=name, in_specs=[hbm] * n, out_specs=[hbm] * n, out_shape=out_shape,
        scratch_shapes=[pltpu.SemaphoreType.DMA((3 * n,)), pltpu.SemaphoreType.DMA((3 * n,)), pltpu.SemaphoreType.DMA((n,))],
    )(*arrs)


def _same_core_copies(ins, outs, send_sems, recv_sems, loc_sems):
    x, y, c = _mesh_pos()
    me = 2 * x + y
    chips = [(1 - x, y), (x, 1 - y), (1 - x, 1 - y)]
    cps = []
    for a in range(len(ins)):
        cps.append(pltpu.make_async_copy(ins[a], outs[a].at[me], loc_sems.at[a]))
        for k, (px, py) in enumerate(chips):
            cps.append(pltpu.make_async_remote_copy(
                src_ref=ins[a], dst_ref=outs[a].at[me], send_sem=send_sems.at[3 * a + k], recv_sem=recv_sems.at[3 * a + k],
                device_id=(px, py, c), device_id_type=MESH))
    return cps


def _prefetch_specs(arrs):
    n = len(arrs)
    hbm = pl.BlockSpec(memory_space=pl.ANY)
    shapes = [jax.ShapeDtypeStruct((N_CHIPS,) + a.shape, a.dtype) for a in arrs]
    sems = [pltpu.SemaphoreType.DMA((3 * n,)), pltpu.SemaphoreType.DMA((3 * n,)), pltpu.SemaphoreType.DMA((n,))] if n else []
    return [hbm] * n, shapes, sems


def _prefetch_run(first, last, ins, outs, sems):
    if not ins:
        return

    @pl.when(first)
    def _():
        for cp in _same_core_copies(ins, outs, *sems):
            cp.start()

    @pl.when(last)
    def _():
        for cp in _same_core_copies(ins, outs, *sems):
            cp.wait()


def gather_two_level(arrs, name):
    n = len(arrs)
    hbm = pl.BlockSpec(memory_space=pl.ANY)

    def body(*refs):
        ins, outs = refs[:n], refs[n:2 * n]
        ici_send, ici_recv, d2d_send, d2d_recv, loc_sems = refs[2 * n:]
        x, y, c = _mesh_pos()
        me = 2 * x + y
        chips = [(1 - x, y), (x, 1 - y), (1 - x, 1 - y)]
        started = []
        for a in range(n):
            hn = arrs[a].shape[0] // 2
            mine = pl.ds(c * hn, hn)
            loc = pltpu.make_async_copy(ins[a], outs[a].at[me], loc_sems.at[a])
            loc.start()
            started.append(loc)
            first = []
            for k, (px, py) in enumerate(chips):
                cp = pltpu.make_async_remote_copy(
                    src_ref=ins[a].at[mine], dst_ref=outs[a].at[me, mine], send_sem=ici_send.at[3 * a + k],
                    recv_sem=ici_recv.at[3 * a + k], device_id=(px, py, c), device_id_type=MESH)
                cp.start()
                first.append(cp)
            for k, (px, py) in enumerate(chips):
                q = 2 * px + py
                first[k].wait_recv()
                fw = pltpu.make_async_remote_copy(
                    src_ref=outs[a].at[q, mine], dst_ref=outs[a].at[q, mine], send_sem=d2d_send.at[3 * a + k],
                    recv_sem=d2d_recv.at[3 * a + k], device_id=(x, y, 1 - c), device_id_type=MESH)
                fw.start()
                started.append(fw)
            for cp in first:
                cp.wait_send()
        for cp in started:
            cp.wait()

    out_shape = [jax.ShapeDtypeStruct((N_CHIPS,) + a.shape, a.dtype) for a in arrs]
    return pl.pallas_call(
        body, name=name, in_specs=[hbm] * n, out_specs=[hbm] * n, out_shape=out_shape,
        scratch_shapes=[pltpu.SemaphoreType.DMA((3 * n,))] * 4 + [pltpu.SemaphoreType.DMA((n,))],
    )(*arrs)


def sibling_exchange(arrs, name):
    n = len(arrs)
    hbm = pl.BlockSpec(memory_space=pl.ANY)

    def body(*refs):
        ins, outs = refs[:n], refs[n:2 * n]
        send_sems, recv_sems = refs[2 * n:]
        x, y, c = _mesh_pos()
        started = []
        for a in range(n):
            cp = pltpu.make_async_remote_copy(
                src_ref=ins[a], dst_ref=outs[a], send_sem=send_sems.at[a], recv_sem=recv_sems.at[a],
                device_id=(x, y, 1 - c), device_id_type=MESH)
            cp.start()
            started.append(cp)
        for cp in started:
            cp.wait()

    return pl.pallas_call(
        body, name=name, in_specs=[hbm] * n, out_specs=[hbm] * n,
        out_shape=[jax.ShapeDtypeStruct(a.shape, a.dtype) for a in arrs],
        scratch_shapes=[pltpu.SemaphoreType.DMA((n,)), pltpu.SemaphoreType.DMA((n,))],
    )(*arrs)


def _rows_tile(r, c):
    return _pick_tile(r, tuple(t for t in (512, 256, 128, 64, 32, 16, 8) if t * c * 4 <= 2 * 1024 * 1024))


def chip_sum(recv, name):
    _, r, c = recv.shape
    tr = _rows_tile(r, c)

    def body(r_ref, o_ref):
        acc = r_ref[0].astype(F32)
        for q in range(1, N_CHIPS):
            acc = acc + r_ref[q].astype(F32)
        o_ref[...] = acc

    return pl.pallas_call(
        body, name=name, grid=(r // tr,),
        in_specs=[pl.BlockSpec((N_CHIPS, tr, c), lambda i: (0, i, 0))],
        out_specs=pl.BlockSpec((tr, c), lambda i: (i, 0)),
        out_shape=jax.ShapeDtypeStruct((r, c), F32),
        compiler_params=_params(("parallel",)),
    )(recv)


def pair_sum(part, sib, name, dtype=F32):
    r, c = part.shape
    tr = _rows_tile(r, c)

    def body(p_ref, s_ref, o_ref):
        o_ref[...] = (p_ref[...].astype(F32) + s_ref[...].astype(F32)).astype(dtype)

    rs = pl.BlockSpec((tr, c), lambda i: (i, 0))
    return pl.pallas_call(
        body, name=name, grid=(r // tr,), in_specs=[rs] * 2, out_specs=rs,
        out_shape=jax.ShapeDtypeStruct((r, c), dtype), compiler_params=_params(("parallel",)),
    )(part, sib)


def adamw(g, w, m, v, name):
    r, c = w.shape
    tr = _rows_tile(r, c)

    def body(g_ref, w_ref, m_ref, v_ref, d_ref, nm_ref, nv_ref):
        g_ = g_ref[...]
        m_ = ADAM_B1 * m_ref[...] + (1.0 - ADAM_B1) * g_
        v_ = ADAM_B2 * v_ref[...] + (1.0 - ADAM_B2) * (g_ * g_)
        m_hat = m_ / (1.0 - ADAM_B1 ** ADAM_STEP)
        v_hat = v_ / (1.0 - ADAM_B2 ** ADAM_STEP)
        d_ref[...] = -ADAM_LR * (m_hat / (jnp.sqrt(v_hat) + ADAM_EPS) + ADAM_WD * w_ref[...])
        nm_ref[...] = m_
        nv_ref[...] = v_

    rs = pl.BlockSpec((tr, c), lambda i: (i, 0))
    return pl.pallas_call(
        body, name=name, grid=(r // tr,), in_specs=[rs] * 4, out_specs=[rs] * 3,
        out_shape=[jax.ShapeDtypeStruct((r, c), F32)] * 3,
        compiler_params=_params(("parallel",)),
    )(g, w, m, v)


def adamw_halves(mine, other, core, w, m, v, name):
    r, c = w.shape
    tr = _rows_tile(r // 2, c)
    th = (r // 2) // tr

    def body(core_ref, a_ref, b_ref, w_ref, m_ref, v_ref, g_ref, d_ref, nm_ref, nv_ref):
        g_ = jnp.where(pl.program_id(0) // th == core_ref[0], a_ref[...], b_ref[...])
        m_ = ADAM_B1 * m_ref[...] + (1.0 - ADAM_B1) * g_
        v_ = ADAM_B2 * v_ref[...] + (1.0 - ADAM_B2) * (g_ * g_)
        m_hat = m_ / (1.0 - ADAM_B1 ** ADAM_STEP)
        v_hat = v_ / (1.0 - ADAM_B2 ** ADAM_STEP)
        g_ref[...] = g_
        d_ref[...] = -ADAM_LR * (m_hat / (jnp.sqrt(v_hat) + ADAM_EPS) + ADAM_WD * w_ref[...])
        nm_ref[...] = m_
        nv_ref[...] = v_

    half = pl.BlockSpec((tr, c), lambda i, core_ref: (i % th, 0))
    rows = pl.BlockSpec((tr, c), lambda i, core_ref: (i, 0))
    return pl.pallas_call(
        body, name=name,
        grid_spec=pltpu.PrefetchScalarGridSpec(num_scalar_prefetch=1, grid=(r // tr,), in_specs=[half, half, rows, rows, rows],
                                               out_specs=[rows] * 4),
        out_shape=[jax.ShapeDtypeStruct((r, c), F32)] * 4,
        compiler_params=_params(("parallel",)),
    )(core, mine, other, w, m, v)


WEIGHTS = ['meta_tokens', 's5_lambda_re', 's5_lambda_im', 's5_log_dt', 's5_b_re', 's5_b_im', 's5_c_re', 's5_c_im', 's5_d',
           's5_w_glu', 's5_w_out', 'attn_w_qkv', 'attn_q_gain', 'attn_k_gain', 'attn_w_out', 'ffn_w_gate', 'ffn_w_up',
           'ffn_w_down', 'ln_gain', 'ln_bias']
BIG = ['s5_w_glu', 's5_w_out', 'attn_w_qkv', 'attn_w_out', 'ffn_w_gate', 'ffn_w_up', 'ffn_w_down']
ROW_SHARDED = {'s5_w_glu', 's5_w_out', 'attn_w_out', 'ffn_w_down'}
SMALL_SHARDED = ['meta_tokens', 'ln_gain', 'ln_bias']
REPLICATED = ['s5_lambda_re', 's5_lambda_im', 's5_log_dt', 's5_b_re', 's5_b_im', 's5_c_re', 's5_c_im', 's5_d',
              'attn_q_gain', 'attn_k_gain']
REP_ALIGN = N_CHIPS * LANES * LANES


def _natural(gathered, row_sharded):
    p, n, a, b = gathered.shape
    if row_sharded:
        return jnp.transpose(gathered, (1, 0, 2, 3)).reshape(n, p * a, b)
    return jnp.transpose(gathered, (1, 2, 0, 3)).reshape(n, a, p * b)


def _shard_major(full, row_sharded):
    n, a, b = full.shape
    if row_sharded:
        return jnp.transpose(full.reshape(n, N_CHIPS, a // N_CHIPS, b), (1, 0, 2, 3))
    return jnp.transpose(full.reshape(n, a, N_CHIPS, b // N_CHIPS), (2, 0, 1, 3))


def _dup_heads(w):
    lead = w.shape[:-1]
    w = w.reshape(lead + (N_KV_HEADS, 1, HEAD_DIM))
    return jnp.broadcast_to(w, lead + (N_KV_HEADS, 2, HEAD_DIM)).reshape(lead + (N_KV_HEADS * 2 * HEAD_DIM,))


def _fold_heads(d):
    lead = d.shape[:-1]
    return d.reshape(lead + (N_KV_HEADS, 2, HEAD_DIM)).sum(axis=-2).reshape(lead + (N_KV_HEADS * HEAD_DIM,))


def _pack_rep(tree, extra=None):
    extra = jnp.zeros((1,), F32) if extra is None else extra.reshape(1)
    flat = jnp.concatenate([tree[n].reshape(-1) for n in REPLICATED] + [extra])
    pad = _round_up(flat.shape[0], REP_ALIGN) - flat.shape[0]
    return jnp.pad(flat, (0, pad))


def _unpack_rep(flat, like):
    out, off = {}, 0
    for n in REPLICATED:
        size = math.prod(like[n].shape)
        out[n] = flat[off:off + size].reshape(like[n].shape)
        off += size
    return out


def _train_step(x, loss_target, w, mom, vel):
    s = x.shape[1]
    n_valid = N_META + s
    lp = _round_up(n_valid, 2 * LANES)
    nq = N_Q_HEADS * HEAD_DIM
    nkv = N_KV_HEADS * HEAD_DIM

    small = jnp.concatenate([w[n].reshape(-1, w[n].shape[-1]) for n in SMALL_SHARDED], axis=0)
    shard = {n: w[n].astype(BF16) for n in BIG}
    uses = [[('s5_w_glu', 0), ('s5_w_out', 0), ('ffn_w_gate', 0), ('ffn_w_up', 0), ('ffn_w_down', 0)],
            [('attn_w_qkv', 0), ('attn_w_out', 0), ('ffn_w_gate', 1), ('ffn_w_up', 1), ('ffn_w_down', 1)],
            [('s5_w_glu', 1), ('s5_w_out', 1), ('ffn_w_gate', 2), ('ffn_w_up', 2), ('ffn_w_down', 2),
             ('attn_w_qkv', 1), ('attn_w_out', 1), ('ffn_w_gate', 3), ('ffn_w_up', 3), ('ffn_w_down', 3)]]
    full = {}

    def unpack(stage, gathered):
        for (n, l), g in zip(uses[stage], gathered):
            full[(n, l)] = _natural(g[:, None], n in ROW_SHARDED)[0]

    first = gather_two_level([shard[n][l] for n, l in uses[0]] + [small], "gather_weights")
    unpack(0, first[:-1])
    small_full = jnp.transpose(first[-1], (1, 0, 2)).reshape(small.shape[0], D_MODEL)
    meta_full = small_full[:N_META]
    ln_gain = small_full[N_META:N_META + 2 * DEPTH].reshape(DEPTH, 2, 1, D_MODEL)
    ln_bias = small_full[N_META + 2 * DEPTH:].reshape(DEPTH, 2, 1, D_MODEL)

    def qkv_dup(wqkv):
        return jnp.concatenate([wqkv[..., :nq], _dup_heads(wqkv[..., nq:nq + nkv]), _dup_heads(wqkv[..., nq + nkv:])], axis=-1)

    w2 = {}

    cos, sin = rope_tables(lp, n_valid)
    e128 = head_sum_matrix()
    e16 = jnp.kron(jnp.eye(N_Q_HEADS, dtype=F32), jnp.ones((HEAD_DIM, 1), F32)).astype(BF16)
    gq = jnp.tile(w['attn_q_gain'], (1, 2))[:, None, :]
    gk = jnp.tile(w['attn_k_gain'], (1, 2))[:, None, :]

    pad_rows = jnp.zeros((lp - n_valid, D_MODEL), F32)
    h = jnp.concatenate([meta_full, x[0], pad_rows], axis=0)
    tgt = jnp.concatenate([jnp.zeros((N_META, D_MODEL), F32), loss_target[0], pad_rows], axis=0)

    saved = []
    s5_names = ['s5_lambda_re', 's5_lambda_im', 's5_log_dt', 's5_b_re', 's5_b_im', 's5_c_re', 's5_c_im']
    for i in range(DEPTH):
        j = i // 2
        sv = {'h': h}
        if i % 2 == 0:
            ops, sv['prep_vjp'] = jax.vjp(s5_prep, *[w[n][j] for n in s5_names])
            m_, wx_, ci_, at_ = ops
            two = lambda t: t.reshape((2 * S5_NJ,) + t.shape[2:])
            sv['ops'] = (blockdiag_expand(m_, S5_CH, S5_CH, "s5_expand_m"),
                         blockdiag_expand(two(wx_), S5_CH, S5_STATE, "s5_expand_wx").reshape(2, S5_NJ, S5_W, S5_W),
                         blockdiag_expand(two(ci_), S5_STATE, S5_CH, "s5_expand_ci").reshape(2, S5_NJ, S5_W, S5_W), at_)
            pf = [shard[n][l] for n, l in uses[1]] if i == 0 else []
            y, sv['lhs'], sv['sp'], sv['sn'], *got = s5_forward(h, *sv['ops'], n_valid, prefetch=pf)
            if i == 0:
                unpack(1, got)
            sv['v'], sv['t'], sv['g'], sv['z'] = glu_forward(y, h, w['s5_d'][j][None], full['s5_w_glu', j])
            sv['r1'], h1 = proj_ln_forward(sv['z'], full['s5_w_out', j], h, ln_gain[i, 0], ln_bias[i, 0], "s5_out_ln")
        else:
            w2[j] = qkv_dup(full['attn_w_qkv', j])
            sv['raw'], sv['qs'], sv['k2'], sv['v2'], sv['qst'], v2t = qkv_forward(h, w2[j], gq[j], gk[j], cos, sin, e128)
            pf = [shard[n][l] for n, l in uses[2]] if i == 1 else []
            sv['o'], lse, *got = attn_forward_t(sv['qs'], sv['k2'], v2t, n_valid, prefetch=pf)
            if i == 1:
                unpack(2, got)
            sv['lse'] = lse.reshape(N_Q_HEADS, lp).T
            sv['r1'], h1 = proj_ln_forward(sv['o'], full['attn_w_out', j], h, ln_gain[i, 0], ln_bias[i, 0], "attn_out_ln")
        sv['h1'] = h1
        sv['a'], sv['b'], sv['f'] = ffn_up_forward(h1, full['ffn_w_gate', i], full['ffn_w_up', i])
        sv['r2'], h = proj_ln_forward(sv['f'], full['ffn_w_down', i], h1, ln_gain[i, 1], ln_bias[i, 1], "ffn_down_ln")
        saved.append(sv)

    dh, loss_part = loss_backward(h, tgt, n_valid)
    loss_local = jnp.sum(loss_part)

    gfull = {n: None for n in BIG}
    gqkv = [None] * w['attn_w_qkv'].shape[0]

    def grad_into(n, l, x, y, name):
        gfull[n] = mm_tn(x, y, name, layers=w[n].shape[0], layer=l, stack=gfull[n])

    d_ln_gain = [[None, None] for _ in range(DEPTH)]
    d_ln_bias = [[None, None] for _ in range(DEPTH)]
    grep = {n: [None] * w[n].shape[0] for n in REPLICATED}
    pending = None
    for i in reversed(range(DEPTH)):
        j = i // 2
        sv = saved[i]
        if pending is None:
            dr2, s2 = ln_backward(dh, sv['r2'], ln_gain[i, 1])
        else:
            dr2, s2 = pending
            pending = None
        d_ln_gain[i][1], d_ln_bias[i][1] = s2[0], s2[1]
        da, db = ffn_backward_act(dr2, full['ffn_w_down', i], sv['a'], sv['b'])
        grad_into('ffn_w_down', i, sv['f'], dr2, "grad_ffn_down")
        dr1, s1 = resid_nt(dr2, [da, db], [full['ffn_w_gate', i], full['ffn_w_up', i]], "ffn_backward_x",
                           ln=(sv['r1'], ln_gain[i, 0]))
        grad_into('ffn_w_gate', i, sv['h1'], da, "grad_ffn_gate")
        grad_into('ffn_w_up', i, sv['h1'], db, "grad_ffn_up")
        d_ln_gain[i][0], d_ln_bias[i][0] = s1[0], s1[1]
        if i % 2 == 0:
            dt, dgd = glu_backward1(dr1, full['s5_w_out', j], sv['g'], sv['t'])
            grad_into('s5_w_out', j, sv['z'], dr1, "grad_s5_out")
            dv, dhs, sd = glu_backward2(dt, dgd, full['s5_w_glu', j], sv['v'], sv['h'], w['s5_d'][j][None], dr1)
            grep['s5_d'][j] = sd[0]
            grad_into('s5_w_glu', j, sv['g'], dt, "grad_s5_glu")
            dh, ldy, dxf, dxr, daf, dar = s5_backward(dv, dhs, *sv['ops'], sv['sp'], sv['sn'], n_valid)
            dm = bmm_tn_compact(sv['lhs'], ldy, S5_CH, S5_CH, "grad_s5_m")
            dwx = jnp.stack([bmm_tn_compact(sv['lhs'], dxf, S5_CH, S5_STATE, "grad_s5_wxf"),
                             bmm_tn_compact(sv['lhs'], dxr, S5_CH, S5_STATE, "grad_s5_wxr")])
            dci = jnp.stack([bmm_tn_compact(sv['sp'], ldy, S5_STATE, S5_CH, "grad_s5_cif"),
                             bmm_tn_compact(sv['sn'], ldy, S5_STATE, S5_CH, "grad_s5_cir")])
            dps = sv['prep_vjp']((dm, dwx, dci, jnp.stack([daf, dar])))
            for n, g in zip(s5_names, dps):
                grep[n][j] = g
        else:
            do, delta, dot = attn_out_backward(dr1, full['attn_w_out', j], sv['o'], e16)
            grad_into('attn_w_out', j, sv['o'], dr1, "grad_attn_out")
            dq, dk2, dv2 = attn_backward(sv['qs'], sv['qst'], sv['k2'], sv['v2'], do, dot, sv['lse'], delta, n_valid)
            draw, gs = qkv_backward(dq, dk2, dv2, sv['raw'], gq[j], gk[j], cos, sin, e128)
            grep['attn_q_gain'][j] = gs[0, :HEAD_DIM] + gs[0, HEAD_DIM:]
            grep['attn_k_gain'][j] = gs[1, :HEAD_DIM] + gs[1, HEAD_DIM:]
            pending = resid_nt(dr1, [draw], [w2[j]], "attn_backward_x", ln=(saved[i - 1]['r2'], ln_gain[i - 1, 1]))
            dw2 = mm_tn(sv['h'], draw, "grad_attn_qkv")[0]
            kq = N_QB * LANES
            kk = N_KB * LANES
            gqkv[j] = jnp.concatenate(
                [dw2[:, :kq], _fold_heads(dw2[:, kq:kq + kk]), _fold_heads(dw2[:, kq + kk:])], axis=1)
    grad_x = dh[N_META:n_valid][None]

    core = lax.axis_index("c")
    gfull['attn_w_qkv'] = jnp.stack(gqkv)
    contrib = [_shard_major(gfull[n], n in ROW_SHARDED) for n in BIG]
    small_g = jnp.concatenate([dh[:N_META], jnp.stack([g for pair in d_ln_gain for g in pair]),
                               jnp.stack([g for pair in d_ln_bias for g in pair])], axis=0)
    contrib.append(jnp.transpose(small_g.reshape(-1, N_CHIPS, D_MODEL // N_CHIPS), (1, 0, 2)))
    rep_g = _pack_rep({n: jnp.stack(grep[n]) for n in REPLICATED}, loss_local)
    contrib.append(rep_g.reshape(N_CHIPS, -1, LANES))
    names = BIG + ['small', 'rep']
    wire = [BF16] * len(BIG) + [F32, F32]
    keep, give = [], []
    for t, dt in zip(contrib, wire):
        hn = t.shape[1] // 2
        keep.append(lax.dynamic_slice_in_dim(t, core * hn, hn, axis=1))
        give.append(lax.dynamic_slice_in_dim(t, (1 - core) * hn, hn, axis=1).astype(dt))
    got = sibling_exchange(give, "sibling_contrib")
    two_d = lambda t: t.reshape(-1, t.shape[-1])
    pair = [pair_sum(two_d(a), two_d(b), "pair_sum_" + n, dt).reshape(a.shape)
            for n, a, b, dt in zip(names, keep, got, wire)]
    recv = chip_exchange(pair, True, "scatter_grads")
    halves = [chip_sum(r.reshape(N_CHIPS, -1, r.shape[-1]), "chip_sum_" + n) for n, r in zip(names, recv)]
    others = sibling_exchange(halves, "sibling_halves")
    core_arr = jnp.reshape(core, (1,)).astype(jnp.int32)

    out = {}

    def update_halves(n, a, b, wn, mn, vn):
        shape = wn.shape
        flat = (-1, shape[-1])
        res = adamw_halves(a, b, core_arr, wn.reshape(flat), mn.reshape(flat), vn.reshape(flat), "adamw_" + n)
        return tuple(t.reshape(shape) for t in res)

    for n, a, b in zip(BIG, halves, others):
        out[n] = update_halves(n, a, b, w[n], mom[n], vel[n])
    cat = lambda tree: jnp.concatenate([tree[n].reshape(-1, tree[n].shape[-1]) for n in SMALL_SHARDED], axis=0)
    sm = update_halves("small", halves[-2], others[-2], cat(w), cat(mom), cat(vel))
    off = 0
    for n in SMALL_SHARDED:
        rows = math.prod(w[n].shape[:-1])
        out[n] = tuple(t[off:off + rows].reshape(w[n].shape) for t in sm)
        off += rows
    rep_quarter = jnp.where(core == 0, jnp.concatenate([halves[-1], others[-1]], axis=0),
                            jnp.concatenate([others[-1], halves[-1]], axis=0))

    def update(n, g, wn, mn, vn):
        shape = wn.shape
        flat = (-1, shape[-1])
        d, nm, nv = adamw(g, wn.reshape(flat), mn.reshape(flat), vn.reshape(flat), "adamw_" + n)
        return tuple(t.reshape(shape) for t in (g, d, nm, nv))

    rep_all = chip_exchange([rep_quarter], False, "gather_rep")[0].reshape(-1, LANES)
    rp = update("rep", rep_all, _pack_rep(w).reshape(-1, LANES), _pack_rep(mom).reshape(-1, LANES),
                _pack_rep(vel).reshape(-1, LANES))
    unpacked = [_unpack_rep(t.reshape(-1), w) for t in rp]
    loss = rp[0].reshape(-1)[sum(math.prod(w[n].shape) for n in REPLICATED)]
    for n in REPLICATED:
        out[n] = tuple(u[n] for u in unpacked)

    return (loss, grad_x, *[out[n][0] for n in WEIGHTS], *[out[n][1] for n in WEIGHTS],
            *[out[n][2] for n in WEIGHTS], *[out[n][3] for n in WEIGHTS])


def kernel(x, meta_tokens, s5_lambda_re, s5_lambda_im, s5_log_dt, s5_b_re, s5_b_im, s5_c_re, s5_c_im, s5_d, s5_w_glu, s5_w_out, attn_w_qkv, attn_q_gain, attn_k_gain, attn_w_out, ffn_w_gate, ffn_w_up, ffn_w_down, ln_gain, ln_bias, loss_target, m_meta_tokens, m_s5_lambda_re, m_s5_lambda_im, m_s5_log_dt, m_s5_b_re, m_s5_b_im, m_s5_c_re, m_s5_c_im, m_s5_d, m_s5_w_glu, m_s5_w_out, m_attn_w_qkv, m_attn_q_gain, m_attn_k_gain, m_attn_w_out, m_ffn_w_gate, m_ffn_w_up, m_ffn_w_down, m_ln_gain, m_ln_bias, v_meta_tokens, v_s5_lambda_re, v_s5_lambda_im, v_s5_log_dt, v_s5_b_re, v_s5_b_im, v_s5_c_re, v_s5_c_im, v_s5_d, v_s5_w_glu, v_s5_w_out, v_attn_w_qkv, v_attn_q_gain, v_attn_k_gain, v_attn_w_out, v_ffn_w_gate, v_ffn_w_up, v_ffn_w_down, v_ln_gain, v_ln_bias):
    given = locals()
    w = {n: given[n] for n in WEIGHTS}
    mom = {n: given["m_" + n] for n in WEIGHTS}
    vel = {n: given["v_" + n] for n in WEIGHTS}
    return _train_step(x, loss_target, w, mom, vel)
```

```python
import math

import jax
import jax.numpy as jnp
from jax import lax
from jax.experimental import pallas as pl
from jax.experimental.pallas import tpu as pltpu

F32 = jnp.float32
BF16 = jnp.bfloat16
MESH = pl.DeviceIdType.MESH

D_MODEL = 1024
N_META = 16
GRID_W = 64
HEAD_DIM = 64
N_Q_HEADS = 16
N_KV_HEADS = 4
ROPE_THETA = 10000.0
QK_EPS = 1e-6
S5_CH = 16
S5_GROUPS = 64
S5_STATE = 64
D_FF = 2816
LN_EPS = 1e-5
DEPTH = 4
ALPHA = (2.0 * DEPTH) ** 0.25
ADAM_LR, ADAM_B1, ADAM_B2, ADAM_EPS, ADAM_WD, ADAM_STEP = 0.001, 0.9, 0.999, 1e-08, 0.01, 10

LANES = 128
SUBLANES = 8
VMEM_LIMIT = 56 * 1024 * 1024

S5_T = 8
S5_GB = LANES // S5_CH
S5_NJ = S5_GROUPS // S5_GB
S5_W = S5_T * LANES
S5_SW = 2 * S5_GB * S5_STATE
S5_HALF = S5_SW // 2


def _round_up(a, b):
    return -(-a // b) * b


def _pick_tile(n, prefs):
    for t in prefs:
        if n % t == 0:
            return t
    return n


def _params(sem=None):
    kw = dict(vmem_limit_bytes=VMEM_LIMIT)
    if sem is not None:
        kw["dimension_semantics"] = sem
    return pltpu.CompilerParams(**kw)


def _dot(a, b, dims):
    return lax.dot_general(a, b, (dims, ((), ())), preferred_element_type=F32)


def _nn(a, b):
    return _dot(a, b, ((1,), (0,)))


def _nt(a, b):
    return _dot(a, b, ((1,), (1,)))


def _tn(a, b):
    return _dot(a, b, ((0,), (0,)))


def _compact(w):
    g, a0, a1, b0, b1 = w.shape
    w = jnp.transpose(w.reshape(S5_NJ, S5_GB, a0, a1, b0, b1), (0, 2, 1, 3, 4, 5))
    return w.reshape(S5_NJ, a0 * S5_GB * a1, b0 * b1)


def s5_prep(lam_re, lam_im, log_dt, b_re, b_im, c_re, c_im):
    hi = lax.Precision.HIGHEST
    t = S5_T
    dt = jnp.exp(log_dt)[..., None]
    taus = jnp.arange(t + 1, dtype=F32)[:, None, None, None]
    mag = jnp.exp(lam_re * dt)
    ang = lam_im * dt
    pr = jnp.concatenate([jnp.ones_like(mag)[None], (mag * jnp.cos(ang))[None],
                          jnp.exp(lam_re * dt * taus[2:]) * jnp.cos(ang * taus[2:])], axis=0)
    pi = jnp.concatenate([jnp.zeros_like(mag)[None], (mag * jnp.sin(ang))[None],
                          jnp.exp(lam_re * dt * taus[2:]) * jnp.sin(ang * taus[2:])], axis=0)
    abr, abi = pr[1], pi[1]
    nr, ni = abr - 1.0, abi
    den = lam_re * lam_re + lam_im * lam_im
    cr = (nr * lam_re + ni * lam_im) / den
    ci_ = (ni * lam_re - nr * lam_im) / den
    bbr = cr[..., None] * b_re - ci_[..., None] * b_im
    bbi = cr[..., None] * b_im + ci_[..., None] * b_re
    er = c_re[None] * pr[:, :, :, None, :] - c_im[None] * pi[:, :, :, None, :]
    ei = c_re[None] * pi[:, :, :, None, :] + c_im[None] * pr[:, :, :, None, :]
    nd, ng, ch = er.shape[1], er.shape[2], er.shape[3]
    lhs = jnp.concatenate([er[:t], -ei[:t]], axis=-1)
    lhs = jnp.transpose(lhs, (1, 2, 0, 3, 4)).reshape(nd, ng, t * ch, 2 * S5_STATE)
    rhs = jnp.concatenate([bbr, bbi], axis=-2)
    kk = jnp.einsum("dgmp,dgpc->dgmc", lhs, rhs, precision=hi)
    kk = jnp.transpose(kk.reshape(nd, ng, t, ch, ch), (2, 0, 1, 3, 4))
    zero = jnp.zeros_like(kk[0, 0])
    mg = jnp.stack([jnp.stack([(kk[i - s, 0] if i > s else zero) + (kk[s - i, 1] if s > i else zero)
                               + ((kk[0, 0] + kk[0, 1]) if i == s else zero) for i in range(t)])
                    for s in range(t)])
    mg = jnp.transpose(mg, (2, 0, 4, 1, 3))
    m = _compact(mg)
    pw_f = jnp.stack([pr[t - 1 - s, 0] for s in range(t)]), jnp.stack([pi[t - 1 - s, 0] for s in range(t)])
    pw_r = jnp.stack([pr[s, 1] for s in range(t)]), jnp.stack([pi[s, 1] for s in range(t)])
    wx = []
    for d, (qr, qi) in enumerate((pw_f, pw_r)):
        wr = qr[..., None] * bbr[d][None] - qi[..., None] * bbi[d][None]
        wi = qr[..., None] * bbi[d][None] + qi[..., None] * bbr[d][None]
        w = jnp.stack([wr, wi], axis=0)
        w = jnp.transpose(w, (2, 1, 4, 0, 3))
        wx.append(_compact(w))
    ci = []
    for d in range(2):
        exps = [i + 1 for i in range(t)] if d == 0 else [t - i for i in range(t)]
        e_r = jnp.stack([er[e, d] for e in exps])
        e_i = jnp.stack([ei[e, d] for e in exps])
        w = jnp.stack([e_r, -e_i], axis=0)
        w = jnp.transpose(w, (2, 0, 4, 1, 3))
        ci.append(_compact(w))
    at = jnp.stack([pr[t], pi[t]], axis=1)
    at = at.reshape(2, 2, S5_NJ, S5_GB * S5_STATE)
    at = jnp.transpose(at, (0, 2, 1, 3)).reshape(2, S5_NJ, 1, S5_SW)
    return m, jnp.stack(wx), jnp.stack(ci), at


def _chunk_rows(ref, nc):
    return jnp.concatenate([ref[pl.ds(s, nc, stride=S5_T), :] for s in range(S5_T)], axis=1)


def _cmul(ar, ai, sr, si):
    return ar * sr - ai * si, ar * si + ai * sr


def _scan_tiles(nc, reverse, step):
    nt = nc // SUBLANES

    def body(it, carry):
        tix = (nt - 1 - it) if reverse else it
        k0 = pl.multiple_of(tix * SUBLANES, SUBLANES)
        return step(k0, carry)

    return body, nt


def _s5_specs(nc):
    hbm = pl.BlockSpec(memory_space=pl.ANY)
    aspec = pl.BlockSpec((1, 1, S5_SW), lambda j: (j, 0, 0))
    cspec = pl.BlockSpec((1, nc, S5_W), lambda j: (j, 0, 0))
    return hbm, aspec, cspec


def _s5_fetch(j, tok_hbm, w_hbms, tok_s, w_s, sems):
    cols = pl.ds(pl.multiple_of(j * LANES, LANES), LANES)
    cps = [pltpu.make_async_copy(tok_hbm.at[:, cols], tok_s, sems.at[0])]
    for i, w in enumerate(w_hbms):
        cps.append(pltpu.make_async_copy(w.at[j], w_s.at[i], sems.at[1 + i]))
    for cp in cps:
        cp.start()
    return cols, cps


def s5_forward(u, m, wx, ci, at, n_valid, prefetch=()):
    lp = u.shape[0]
    nc = lp // S5_T
    nvc = n_valid // S5_T
    npf = len(prefetch)
    pf_specs, pf_shapes, pf_sems = _prefetch_specs(prefetch)

    def body(*refs):
        u_hbm, m_hbm, wxf_hbm, wxr_hbm, cif_hbm, cir_hbm, atf_ref, atr_ref = refs[:8]
        pf_ins = refs[8:8 + npf]
        y_hbm, lhs_ref, sp_ref, sn_ref = refs[8 + npf:12 + npf]
        pf_outs = refs[12 + npf:12 + 2 * npf]
        tok_s, w_s, xf_s, xr_s, sems = refs[12 + 2 * npf:17 + 2 * npf]
        j = pl.program_id(0)
        _prefetch_run(j == 0, j == S5_NJ - 1, pf_ins, pf_outs, refs[17 + 2 * npf:])
        cols, cps = _s5_fetch(j, u_hbm, (m_hbm, wxf_hbm, wxr_hbm, cif_hbm, cir_hbm), tok_s, w_s, sems)
        cps[0].wait()
        lhs = _chunk_rows(tok_s, nc)
        rows = lax.broadcasted_iota(jnp.int32, lhs.shape, 0)
        lhs = jnp.where(rows < nvc, lhs, 0.0).astype(BF16)
        lhs_ref[0] = lhs
        cps[2].wait()
        cps[3].wait()
        xf_s[...] = _nn(lhs, w_s[1])
        xr_s[...] = _nn(lhs, w_s[2])
        afr, afi = atf_ref[0, :, :S5_HALF], atf_ref[0, :, S5_HALF:]
        arr, ari = atr_ref[0, :, :S5_HALF], atr_ref[0, :, S5_HALF:]

        def scan_step(x_s, ar, ai, descending):
            def step(k0, carry):
                sr, si = carry
                x = x_s[pl.ds(k0, SUBLANES), :]
                outs = [None] * SUBLANES
                order = reversed(range(SUBLANES)) if descending else range(SUBLANES)
                for r in order:
                    outs[r] = jnp.concatenate([sr, si], axis=1)
                    nr, ni = _cmul(ar, ai, sr, si)
                    sr = nr + x[r:r + 1, :S5_HALF]
                    si = ni + x[r:r + 1, S5_HALF:]
                x_s[pl.ds(k0, SUBLANES), :] = jnp.concatenate(outs, axis=0)
                return sr, si
            return step

        zero = jnp.zeros((1, S5_HALF), F32)
        fb, nt = _scan_tiles(nc, False, scan_step(xf_s, afr, afi, False))
        lax.fori_loop(0, nt, fb, (zero, zero))
        rb, nt = _scan_tiles(nc, True, scan_step(xr_s, arr, ari, True))
        lax.fori_loop(0, nt, rb, (zero, zero))
        sp = xf_s[...].astype(BF16)
        sn = xr_s[...].astype(BF16)
        sp_ref[0] = sp
        sn_ref[0] = sn
        cps[1].wait()
        cps[4].wait()
        cps[5].wait()
        y = _nn(lhs, w_s[0]) + _nn(sp, w_s[3]) + _nn(sn, w_s[4])
        for i in range(S5_T):
            tok_s[pl.ds(i, nc, stride=S5_T), :] = y[:, i * LANES:(i + 1) * LANES]
        pltpu.sync_copy(tok_s, y_hbm.at[:, cols])

    hbm, aspec, cspec = _s5_specs(nc)
    return pl.pallas_call(
        body, name="s5_forward", grid=(S5_NJ,),
        in_specs=[hbm] * 6 + [aspec, aspec] + pf_specs,
        out_specs=[hbm, cspec, cspec, cspec] + pf_specs,
        out_shape=[jax.ShapeDtypeStruct((lp, D_MODEL), F32)] + [jax.ShapeDtypeStruct((S5_NJ, nc, S5_W), BF16)] * 3 + pf_shapes,
        scratch_shapes=[pltpu.VMEM((lp, LANES), F32), pltpu.VMEM((5, S5_W, S5_W), BF16),
                        pltpu.VMEM((nc, S5_SW), F32), pltpu.VMEM((nc, S5_SW), F32), pltpu.SemaphoreType.DMA((6,))] + pf_sems,
        compiler_params=_params(("arbitrary",)),
    )(u, m, wx[0], wx[1], ci[0], ci[1], at[0], at[1], *prefetch)


def s5_backward(dy, dhs, m, wx, ci, at, sp, sn, n_valid):
    lp = dy.shape[0]
    nc = lp // S5_T
    nvc = n_valid // S5_T

    def body(dy_hbm, dhs_hbm, m_hbm, wxf_hbm, wxr_hbm, cif_hbm, cir_hbm, atf_ref, atr_ref, sp_ref, sn_ref,
             dh_hbm, ldy_ref, dxf_ref, dxr_ref, daf_ref, dar_ref, tok_s, w_s, gf_s, gr_s, sems):
        j = pl.program_id(0)
        cols, cps = _s5_fetch(j, dy_hbm, (m_hbm, wxf_hbm, wxr_hbm, cif_hbm, cir_hbm), tok_s, w_s, sems)
        cps[0].wait()
        ldy = _chunk_rows(tok_s, nc)
        rows = lax.broadcasted_iota(jnp.int32, ldy.shape, 0)
        ldy = jnp.where(rows < nvc, ldy, 0.0).astype(BF16)
        ldy_ref[0] = ldy
        resid = pltpu.make_async_copy(dhs_hbm.at[:, cols], tok_s, sems.at[0])
        resid.start()
        cps[4].wait()
        cps[5].wait()
        gf_s[...] = _nt(ldy, w_s[3])
        gr_s[...] = _nt(ldy, w_s[4])
        afr, afi = atf_ref[0, :, :S5_HALF], atf_ref[0, :, S5_HALF:]
        arr, ari = atr_ref[0, :, :S5_HALF], atr_ref[0, :, S5_HALF:]

        def adj_step(g_s, s_ref, ar, ai, descending):
            def step(k0, carry):
                gr_, gi_, dr_, di_ = carry
                g = g_s[pl.ds(k0, SUBLANES), :]
                p = s_ref[0, pl.ds(k0, SUBLANES), :].astype(F32)
                outs = [None] * SUBLANES
                order = reversed(range(SUBLANES)) if descending else range(SUBLANES)
                for r in order:
                    outs[r] = jnp.concatenate([gr_, gi_], axis=1)
                    pr_, pi_ = p[r:r + 1, :S5_HALF], p[r:r + 1, S5_HALF:]
                    dr_ = dr_ + gr_ * pr_ + gi_ * pi_
                    di_ = di_ + gi_ * pr_ - gr_ * pi_
                    nr, ni = _cmul(ar, -ai, gr_, gi_)
                    gr_ = nr + g[r:r + 1, :S5_HALF]
                    gi_ = ni + g[r:r + 1, S5_HALF:]
                g_s[pl.ds(k0, SUBLANES), :] = jnp.concatenate(outs, axis=0)
                return gr_, gi_, dr_, di_
            return step

        zero = jnp.zeros((1, S5_HALF), F32)
        fb, nt = _scan_tiles(nc, True, adj_step(gf_s, sp_ref, afr, afi, True))
        _, _, dr_, di_ = lax.fori_loop(0, nt, fb, (zero,) * 4)
        daf_ref[0] = jnp.concatenate([dr_, di_], axis=1)
        rb, nt = _scan_tiles(nc, False, adj_step(gr_s, sn_ref, arr, ari, False))
        _, _, dr_, di_ = lax.fori_loop(0, nt, rb, (zero,) * 4)
        dar_ref[0] = jnp.concatenate([dr_, di_], axis=1)
        dxf = gf_s[...].astype(BF16)
        dxr = gr_s[...].astype(BF16)
        dxf_ref[0] = dxf
        dxr_ref[0] = dxr
        cps[1].wait()
        cps[2].wait()
        cps[3].wait()
        du = _nt(ldy, w_s[0]) + _nt(dxf, w_s[1]) + _nt(dxr, w_s[2])
        rows = lax.broadcasted_iota(jnp.int32, du.shape, 0)
        du = jnp.where(rows < nvc, du, 0.0)
        resid.wait()
        for s in range(S5_T):
            tok_s[pl.ds(s, nc, stride=S5_T), :] += du[:, s * LANES:(s + 1) * LANES]
        pltpu.sync_copy(tok_s, dh_hbm.at[:, cols])

    hbm, aspec, cspec = _s5_specs(nc)
    return pl.pallas_call(
        body, name="s5_backward", grid=(S5_NJ,),
        in_specs=[hbm] * 7 + [aspec, aspec, cspec, cspec],
        out_specs=[hbm, cspec, cspec, cspec, aspec, aspec],
        out_shape=[jax.ShapeDtypeStruct((lp, D_MODEL), F32)] + [jax.ShapeDtypeStruct((S5_NJ, nc, S5_W), BF16)] * 3
        + [jax.ShapeDtypeStruct((S5_NJ, 1, S5_SW), F32)] * 2,
        scratch_shapes=[pltpu.VMEM((lp, LANES), F32), pltpu.VMEM((5, S5_W, S5_W), BF16),
                        pltpu.VMEM((nc, S5_SW), F32), pltpu.VMEM((nc, S5_SW), F32), pltpu.SemaphoreType.DMA((6,))],
        compiler_params=_params(("arbitrary",)),
    )(dy, dhs, m, wx[0], wx[1], ci[0], ci[1], at[0], at[1], sp, sn)


S5_CW = LANES


def _replicate_matrix(b1):
    b0n = S5_CW // b1
    eye0 = jnp.eye(b0n, dtype=F32)
    eye1 = jnp.eye(b1, dtype=F32)
    r = jnp.einsum("ab,cd->acbd", eye0, eye1)[:, :, :, None, :]
    r = jnp.broadcast_to(r, (b0n, b1, b0n, S5_GB, b1))
    return r.reshape(S5_CW, b0n * S5_GB * b1).astype(BF16)


def _same_group(a1, b1):
    rg = (lax.broadcasted_iota(jnp.int32, (S5_W, S5_W), 0) // a1) % S5_GB
    cg = (lax.broadcasted_iota(jnp.int32, (S5_W, S5_W), 1) // b1) % S5_GB
    return rg == cg


def blockdiag_expand(compact, a1, b1, name):
    nj = compact.shape[0]

    def body(c_ref, r_ref, o_ref):
        rep = _nn(c_ref[0].astype(BF16), r_ref[...])
        o_ref[0] = jnp.where(_same_group(a1, b1), rep, 0.0).astype(BF16)

    return pl.pallas_call(
        body, name=name, grid=(nj,),
        in_specs=[pl.BlockSpec((1, S5_W, S5_CW), lambda j: (j, 0, 0)), _full_spec((S5_CW, S5_W))],
        out_specs=pl.BlockSpec((1, S5_W, S5_W), lambda j: (j, 0, 0)),
        out_shape=jax.ShapeDtypeStruct((nj, S5_W, S5_W), BF16),
        compiler_params=_params(("parallel",)),
    )(compact, _replicate_matrix(b1))


def bmm_tn_compact(a, b, a1, b1, name):
    nj, k, wa = a.shape
    wb = b.shape[2]

    def body(a_ref, b_ref, r_ref, o_ref):
        prod = jnp.where(_same_group(a1, b1), _tn(a_ref[0], b_ref[0]), 0.0)
        hi = prod.astype(BF16)
        lo = (prod - hi.astype(F32)).astype(BF16)
        o_ref[0] = _nt(hi, r_ref[...]) + _nt(lo, r_ref[...])

    return pl.pallas_call(
        body, name=name, grid=(nj,),
        in_specs=[pl.BlockSpec((1, k, wa), lambda j: (j, 0, 0)), pl.BlockSpec((1, k, wb), lambda j: (j, 0, 0)),
                  _full_spec((S5_CW, S5_W))],
        out_specs=pl.BlockSpec((1, wa, S5_CW), lambda j: (j, 0, 0)),
        out_shape=jax.ShapeDtypeStruct((nj, wa, S5_CW), F32),
        compiler_params=_params(("parallel",)),
    )(a, b, _replicate_matrix(b1))


def s5_operator_grads(lhs, ldy, dxf, dxr, sp, sn):
    nj, k, wd = lhs.shape

    def body(lhs_ref, ldy_ref, dxf_ref, dxr_ref, sp_ref, sn_ref, r16_ref, r64_ref,
             dm_ref, dwxf_ref, dwxr_ref, dcif_ref, dcir_ref):
        def compact(prod, a1, b1, r_ref):
            prod = jnp.where(_same_group(a1, b1), prod, 0.0)
            hi = prod.astype(BF16)
            lo = (prod - hi.astype(F32)).astype(BF16)
            return _nt(hi, r_ref[...]) + _nt(lo, r_ref[...])

        lt = lhs_ref[0].T
        ldy_ = ldy_ref[0]
        dm_ref[0] = compact(_nn(lt, ldy_), S5_CH, S5_CH, r16_ref)
        dwxf_ref[0] = compact(_nn(lt, dxf_ref[0]), S5_CH, S5_STATE, r64_ref)
        dwxr_ref[0] = compact(_nn(lt, dxr_ref[0]), S5_CH, S5_STATE, r64_ref)
        dcif_ref[0] = compact(_nn(sp_ref[0].T, ldy_), S5_STATE, S5_CH, r16_ref)
        dcir_ref[0] = compact(_nn(sn_ref[0].T, ldy_), S5_STATE, S5_CH, r16_ref)

    ispec = pl.BlockSpec((1, k, wd), lambda j: (j, 0, 0))
    ospec = pl.BlockSpec((1, wd, S5_CW), lambda j: (j, 0, 0))
    rspec = _full_spec((S5_CW, S5_W))
    return pl.pallas_call(
        body, name="s5_operator_grads", grid=(nj,),
        in_specs=[ispec] * 6 + [rspec, rspec],
        out_specs=[ospec] * 5,
        out_shape=[jax.ShapeDtypeStruct((nj, wd, S5_CW), F32)] * 5,
        compiler_params=_params(("parallel",)),
    )(lhs, ldy, dxf, dxr, sp, sn, _replicate_matrix(S5_CH), _replicate_matrix(S5_STATE))


def _tm(lp):
    return _pick_tile(lp, (768, 256))


def _row_spec(tm, width):
    return pl.BlockSpec((tm, width), lambda i: (i, 0))


def _full_spec(shape):
    return pl.BlockSpec(shape, lambda *_: (0,) * len(shape))


def _gelu(v):
    return 0.5 * v * (1.0 + lax.erf(v * (2.0 ** -0.5)))


def _gelu_grad(v):
    return 0.5 * (1.0 + lax.erf(v * (2.0 ** -0.5))) + v * jnp.exp(-0.5 * v * v) * (2.0 * math.pi) ** -0.5


def _layer_norm(r, gain, bias):
    mean = jnp.mean(r, axis=-1, keepdims=True)
    c = r - mean
    var = jnp.mean(c * c, axis=-1, keepdims=True)
    return c * lax.rsqrt(var + LN_EPS) * gain + bias


def glu_forward(y, h, dvec, wglu):
    lp, d = y.shape
    tm = _tm(lp)

    def body(y_ref, h_ref, d_ref, w_ref, v_ref, t_ref, g_ref, z_ref):
        v = y_ref[...] + d_ref[...] * h_ref[...]
        g = _gelu(v)
        gb = g.astype(BF16)
        t = _nn(gb, w_ref[...])
        v_ref[...] = v
        t_ref[...] = t
        g_ref[...] = gb
        z_ref[...] = (g * jax.nn.sigmoid(t)).astype(BF16)

    rs = _row_spec(tm, d)
    return pl.pallas_call(
        body, name="glu_forward", grid=(lp // tm,),
        in_specs=[rs, rs, _full_spec((1, d)), _full_spec((d, d))],
        out_specs=[rs, rs, rs, rs],
        out_shape=[jax.ShapeDtypeStruct((lp, d), F32)] * 2 + [jax.ShapeDtypeStruct((lp, d), BF16)] * 2,
        compiler_params=_params(("parallel",)),
    )(y, h, dvec, wglu)


def proj_ln_forward(z, w, h, gain, bias, name):
    lp, k = z.shape
    d = w.shape[1]
    tm = _tm(lp)

    def body(z_ref, w_ref, h_ref, g_ref, b_ref, r_ref, o_ref):
        r = ALPHA * h_ref[...] + _nn(z_ref[...], w_ref[...])
        r_ref[...] = r
        o_ref[...] = _layer_norm(r, g_ref[...], b_ref[...])

    rs = _row_spec(tm, d)
    return pl.pallas_call(
        body, name=name, grid=(lp // tm,),
        in_specs=[_row_spec(tm, k), _full_spec((k, d)), rs, _full_spec((1, d)), _full_spec((1, d))],
        out_specs=[rs, rs],
        out_shape=[jax.ShapeDtypeStruct((lp, d), F32)] * 2,
        compiler_params=_params(("parallel",)),
    )(z, w, h, gain, bias)


FFN_TM = 384


def ffn_up_forward(h, wg, wu):
    lp, d = h.shape
    dff = wg.shape[1]
    tm = _pick_tile(lp, (FFN_TM, 256))

    def body(h_ref, wg_ref, wu_ref, a_ref, b_ref, f_ref):
        hb = h_ref[...].astype(BF16)
        a = _nn(hb, wg_ref[...])
        b = _nn(hb, wu_ref[...])
        a_ref[...] = a.astype(BF16)
        b_ref[...] = b.astype(BF16)
        f_ref[...] = (a * jax.nn.sigmoid(a) * b).astype(BF16)

    ws = _full_spec((d, dff))
    os_ = _row_spec(tm, dff)
    return pl.pallas_call(
        body, name="ffn_up_forward", grid=(lp // tm,),
        in_specs=[_row_spec(tm, d), ws, ws],
        out_specs=[os_, os_, os_],
        out_shape=[jax.ShapeDtypeStruct((lp, dff), BF16)] * 3,
        compiler_params=_params(("parallel",)),
    )(h, wg, wu)


def _ln_backward_rows(dh_, r_, gain):
    mean = jnp.mean(r_, axis=-1, keepdims=True)
    c = r_ - mean
    var = jnp.mean(c * c, axis=-1, keepdims=True)
    rstd = lax.rsqrt(var + LN_EPS)
    xh = c * rstd
    dxh = dh_ * gain
    m1 = jnp.mean(dxh, axis=-1, keepdims=True)
    m2 = jnp.mean(dxh * xh, axis=-1, keepdims=True)
    return rstd * (dxh - m1 - xh * m2), jnp.sum(dh_ * xh, axis=0, keepdims=True), jnp.sum(dh_, axis=0, keepdims=True)


def _accumulate_ln_sums(s_ref, sg, sb):
    @pl.when(pl.program_id(0) == 0)
    def _():
        s_ref[...] = jnp.zeros_like(s_ref)

    s_ref[0:1, :] += sg
    s_ref[1:2, :] += sb


def ln_backward(dh, r, gain):
    lp, d = dh.shape
    tm = _tm(lp)

    def body(dh_ref, r_ref, g_ref, dr_ref, s_ref):
        dr, sg, sb = _ln_backward_rows(dh_ref[...], r_ref[...], g_ref[...])
        dr_ref[...] = dr
        _accumulate_ln_sums(s_ref, sg, sb)

    rs = _row_spec(tm, d)
    return pl.pallas_call(
        body, name="ln_backward", grid=(lp // tm,),
        in_specs=[rs, rs, _full_spec((1, d))],
        out_specs=[rs, _full_spec((SUBLANES, d))],
        out_shape=[jax.ShapeDtypeStruct((lp, d), F32), jax.ShapeDtypeStruct((SUBLANES, d), F32)],
        compiler_params=_params(("arbitrary",)),
    )(dh, r, gain)


def ffn_backward_act(dr, wd, a, b):
    lp, d = dr.shape
    dff = wd.shape[0]
    tm = _pick_tile(lp, (FFN_TM, 256))

    def body(dr_ref, wd_ref, a_ref, b_ref, da_ref, db_ref):
        df = _nt(dr_ref[...].astype(BF16), wd_ref[...])
        a_ = a_ref[...].astype(F32)
        b_ = b_ref[...].astype(F32)
        sg = jax.nn.sigmoid(a_)
        da_ref[...] = (df * b_ * sg * (1.0 + a_ * (1.0 - sg))).astype(BF16)
        db_ref[...] = (df * a_ * sg).astype(BF16)

    os_ = _row_spec(tm, dff)
    return pl.pallas_call(
        body, name="ffn_backward_act", grid=(lp // tm,),
        in_specs=[_row_spec(tm, d), _full_spec((dff, d)), os_, os_],
        out_specs=[os_, os_],
        out_shape=[jax.ShapeDtypeStruct((lp, dff), BF16)] * 2,
        compiler_params=_params(("parallel",)),
    )(dr, wd, a, b)


def resid_nt(dr, xs, ws, name, ln=None):
    lp, d = dr.shape
    tm = _pick_tile(lp, (FFN_TM, 256))
    n = len(xs)

    def body(*refs):
        acc = ALPHA * refs[0][...]
        for i in range(n):
            acc = acc + _nt(refs[1 + i][...], refs[1 + n + i][...])
        if ln is None:
            refs[-1][...] = acc
        else:
            r_ref, g_ref, o_ref, s_ref = refs[1 + 2 * n:]
            dr_, sg, sb = _ln_backward_rows(acc, r_ref[...], g_ref[...])
            o_ref[...] = dr_
            _accumulate_ln_sums(s_ref, sg, sb)

    rs = _row_spec(tm, d)
    in_specs = [rs] + [_row_spec(tm, x.shape[1]) for x in xs] + [_full_spec(w.shape) for w in ws]
    if ln is None:
        return pl.pallas_call(
            body, name=name, grid=(lp // tm,),
            in_specs=in_specs, out_specs=rs,
            out_shape=jax.ShapeDtypeStruct((lp, d), F32),
            compiler_params=_params(("parallel",)),
        )(dr, *xs, *ws)
    return pl.pallas_call(
        body, name=name, grid=(lp // tm,),
        in_specs=in_specs + [rs, _full_spec((1, d))], out_specs=[rs, _full_spec((SUBLANES, d))],
        out_shape=[jax.ShapeDtypeStruct((lp, d), F32), jax.ShapeDtypeStruct((SUBLANES, d), F32)],
        compiler_params=_params(("arbitrary",)),
    )(dr, *xs, *ws, *ln)


def mm_tn(x, y, name, layers=1, layer=0, stack=None):
    lp, k = x.shape
    n = y.shape[1]
    tm = _tm(lp)

    def body(x_ref, y_ref, *rest):
        o_ref = rest[-1]

        @pl.when(pl.program_id(0) == 0)
        def _():
            o_ref[...] = jnp.zeros_like(o_ref)

        o_ref[0] += _tn(x_ref[...].astype(BF16), y_ref[...].astype(BF16))

    in_specs = [pl.BlockSpec((tm, k), lambda i: (i, 0)), pl.BlockSpec((tm, n), lambda i: (i, 0))]
    args = [x, y]
    aliases = {}
    if stack is not None:
        in_specs.append(pl.BlockSpec(memory_space=pl.ANY))
        args.append(stack)
        aliases = {2: 0}
    return pl.pallas_call(
        body, name=name, grid=(lp // tm,),
        in_specs=in_specs,
        out_specs=pl.BlockSpec((1, k, n), lambda i: (layer, 0, 0)),
        out_shape=jax.ShapeDtypeStruct((layers, k, n), F32),
        input_output_aliases=aliases,
        compiler_params=_params(("arbitrary",)),
    )(*args)


def glu_backward1(dr, wout, g, t):
    lp, d = dr.shape
    tm = _tm(lp)

    def body(dr_ref, w_ref, g_ref, t_ref, dt_ref, dgd_ref):
        dz = _nt(dr_ref[...].astype(BF16), w_ref[...])
        s = jax.nn.sigmoid(t_ref[...])
        dgd_ref[...] = dz * s
        dt_ref[...] = (dz * g_ref[...].astype(F32) * s * (1.0 - s)).astype(BF16)

    rs = _row_spec(tm, d)
    return pl.pallas_call(
        body, name="glu_backward1", grid=(lp // tm,),
        in_specs=[rs, _full_spec((d, d)), rs, rs],
        out_specs=[rs, rs],
        out_shape=[jax.ShapeDtypeStruct((lp, d), BF16), jax.ShapeDtypeStruct((lp, d), F32)],
        compiler_params=_params(("parallel",)),
    )(dr, wout, g, t)


def glu_backward2(dt, dgd, wglu, v, h, dvec, dr):
    lp, d = dt.shape
    tm = _tm(lp)

    def body(dt_ref, dgd_ref, w_ref, v_ref, h_ref, d_ref, dr_ref, dv_ref, dhs_ref, s_ref):
        dg = dgd_ref[...] + _nt(dt_ref[...], w_ref[...])
        dv = dg * _gelu_grad(v_ref[...])
        dv_ref[...] = dv
        dhs_ref[...] = ALPHA * dr_ref[...] + dv * d_ref[...]

        @pl.when(pl.program_id(0) == 0)
        def _():
            s_ref[...] = jnp.zeros_like(s_ref)

        s_ref[0:1, :] += jnp.sum(dv * h_ref[...], axis=0, keepdims=True)

    rs = _row_spec(tm, d)
    return pl.pallas_call(
        body, name="glu_backward2", grid=(lp // tm,),
        in_specs=[rs, rs, _full_spec((d, d)), rs, rs, _full_spec((1, d)), rs],
        out_specs=[rs, rs, _full_spec((SUBLANES, d))],
        out_shape=[jax.ShapeDtypeStruct((lp, d), F32)] * 2 + [jax.ShapeDtypeStruct((SUBLANES, d), F32)],
        compiler_params=_params(("arbitrary",)),
    )(dt, dgd, wglu, v, h, dvec, dr)


def loss_backward(hf, tgt, n_valid):
    lp, d = hf.shape
    tm = _tm(lp)

    def body(h_ref, t_ref, dh_ref, s_ref):
        rows = pl.program_id(0) * tm + lax.broadcasted_iota(jnp.int32, (tm, d), 0)
        ok = (rows >= N_META) & (rows < n_valid)
        e = jnp.where(ok, h_ref[...] - t_ref[...], 0.0)
        dh_ref[...] = e * (1.0 / d)

        @pl.when(pl.program_id(0) == 0)
        def _():
            s_ref[...] = jnp.zeros_like(s_ref)

        sq = e * e
        part = sq[:, 0:LANES]
        for c in range(1, d // LANES):
            part = part + sq[:, c * LANES:(c + 1) * LANES]
        acc = part[0:SUBLANES]
        for r in range(1, tm // SUBLANES):
            acc = acc + part[r * SUBLANES:(r + 1) * SUBLANES]
        s_ref[...] += acc * (0.5 / d)

    rs = _row_spec(tm, d)
    return pl.pallas_call(
        body, name="loss_backward", grid=(lp // tm,),
        in_specs=[rs, rs], out_specs=[rs, _full_spec((SUBLANES, LANES))],
        out_shape=[jax.ShapeDtypeStruct((lp, d), F32), jax.ShapeDtypeStruct((SUBLANES, LANES), F32)],
        compiler_params=_params(("arbitrary",)),
    )(hf, tgt)


N_QB = N_Q_HEADS // 2
N_KB = N_KV_HEADS
QKV_W = (N_QB + 2 * N_KB) * LANES
Q_SCALE = HEAD_DIM ** -0.5 * math.log2(math.e)


def rope_tables(lp, n_valid):
    t = jnp.arange(lp, dtype=jnp.int32)
    real = (t >= N_META) & (t < n_valid)
    pos = jnp.where(real, t - N_META, 0)
    row = (pos // GRID_W).astype(F32)
    col = (pos % GRID_W).astype(F32)
    axis_dim = HEAD_DIM // 2
    inv = ROPE_THETA ** (-jnp.arange(0, axis_dim, 2, dtype=F32) / axis_dim)
    ar = row[:, None] * inv[None, :]
    ac = col[:, None] * inv[None, :]
    cos = jnp.concatenate([jnp.cos(ar), jnp.cos(ar), jnp.cos(ac), jnp.cos(ac)], axis=1)
    sin = jnp.concatenate([-jnp.sin(ar), jnp.sin(ar), -jnp.sin(ac), jnp.sin(ac)], axis=1)
    return jnp.tile(cos, (1, 2)), jnp.tile(sin, (1, 2))


def head_sum_matrix():
    return jnp.kron(jnp.eye(2, dtype=F32), jnp.ones((HEAD_DIM, HEAD_DIM), F32)).astype(BF16)


def _segsum(x, e):
    hi = x.astype(BF16)
    lo = (x - hi.astype(F32)).astype(BF16)
    return _nn(hi, e) + _nn(lo, e)


def _swap_halves(x):
    lane = lax.broadcasted_iota(jnp.int32, x.shape, 1)
    quarter = HEAD_DIM // 4
    return jnp.where(lane % (2 * quarter) < quarter, pltpu.roll(x, LANES - quarter, 1), pltpu.roll(x, quarter, 1))


def qkv_forward(h, w2, gq, gk, cos, sin, e):
    lp, d = h.shape
    tm = _tm(lp)
    kw, vw = N_KB * LANES, N_KB * LANES

    def body(h_ref, w_ref, gq_ref, gk_ref, cos_ref, sin_ref, e_ref, raw_ref, q_ref, k_ref, v_ref, qt_ref, vt_ref):
        raw = _nn(h_ref[...].astype(BF16), w_ref[...])
        raw_ref[...] = raw
        c, s_, em = cos_ref[...], sin_ref[...], e_ref[...]
        for cb in range(N_QB + N_KB):
            t = raw[:, cb * LANES:(cb + 1) * LANES]
            rstd = lax.rsqrt(_segsum(t * t, em) * (1.0 / HEAD_DIM) + QK_EPS)
            n = t * rstd * (gq_ref[...] if cb < N_QB else gk_ref[...])
            rot = n * c + _swap_halves(n) * s_
            if cb < N_QB:
                qs = rot * Q_SCALE
                q_ref[:, cb * LANES:(cb + 1) * LANES] = qs.astype(BF16)
                qt_ref[cb * LANES:(cb + 1) * LANES, :] = qs.T.astype(BF16)
            else:
                k_ref[:, (cb - N_QB) * LANES:(cb - N_QB + 1) * LANES] = rot.astype(BF16)
        v_ref[...] = raw[:, (N_QB + N_KB) * LANES:].astype(BF16)
        for cb in range(N_KB):
            lo = (N_QB + N_KB + cb) * LANES
            vt_ref[cb * LANES:(cb + 1) * LANES, :] = raw[:, lo:lo + LANES].T.astype(BF16)

    col_spec = lambda rows: pl.BlockSpec((rows, tm), lambda i: (0, i))
    return pl.pallas_call(
        body, name="qkv_forward", grid=(lp // tm,),
        in_specs=[_row_spec(tm, d), _full_spec((d, QKV_W)), _full_spec((1, LANES)), _full_spec((1, LANES)),
                  _row_spec(tm, LANES), _row_spec(tm, LANES), _full_spec((LANES, LANES))],
        out_specs=[_row_spec(tm, QKV_W), _row_spec(tm, N_QB * LANES), _row_spec(tm, kw), _row_spec(tm, vw),
                   col_spec(N_QB * LANES), col_spec(vw)],
        out_shape=[jax.ShapeDtypeStruct((lp, QKV_W), F32), jax.ShapeDtypeStruct((lp, N_QB * LANES), BF16),
                   jax.ShapeDtypeStruct((lp, kw), BF16), jax.ShapeDtypeStruct((lp, vw), BF16),
                   jax.ShapeDtypeStruct((N_QB * LANES, lp), BF16), jax.ShapeDtypeStruct((vw, lp), BF16)],
        compiler_params=_params(("parallel",)),
    )(h, w2, gq, gk, cos, sin, e)


def qkv_backward(dqs, dk2, dv2, raw, gq, gk, cos, sin, e):
    lp = raw.shape[0]
    tm = _tm(lp)

    def body(dq_ref, dk_ref, dv_ref, raw_ref, gq_ref, gk_ref, cos_ref, sin_ref, e_ref, d_ref, s_ref):
        @pl.when(pl.program_id(0) == 0)
        def _():
            s_ref[...] = jnp.zeros_like(s_ref)

        c, s_, em = cos_ref[...], sin_ref[...], e_ref[...]
        gsum = [jnp.zeros((1, LANES), F32), jnp.zeros((1, LANES), F32)]
        for cb in range(N_QB + N_KB):
            isq = cb < N_QB
            t = raw_ref[:, cb * LANES:(cb + 1) * LANES]
            if isq:
                drot = dq_ref[:, cb * LANES:(cb + 1) * LANES] * (HEAD_DIM ** -0.5)
            else:
                drot = dk_ref[:, (cb - N_QB) * LANES:(cb - N_QB + 1) * LANES] * math.log(2.0)
            gain = gq_ref[...] if isq else gk_ref[...]
            rstd = lax.rsqrt(_segsum(t * t, em) * (1.0 / HEAD_DIM) + QK_EPS)
            dn = drot * c + _swap_halves(drot * s_)
            xh = t * rstd
            gsum[0 if isq else 1] = gsum[0 if isq else 1] + jnp.sum(dn * xh, axis=0, keepdims=True)
            w = dn * gain
            mw = _segsum(w * xh, em) * (1.0 / HEAD_DIM)
            d_ref[:, cb * LANES:(cb + 1) * LANES] = (rstd * (w - xh * mw)).astype(BF16)
        d_ref[:, (N_QB + N_KB) * LANES:] = dv_ref[...].astype(BF16)
        s_ref[0:1, :] += gsum[0]
        s_ref[1:2, :] += gsum[1]

    kw = N_KB * LANES
    return pl.pallas_call(
        body, name="qkv_backward", grid=(lp // tm,),
        in_specs=[_row_spec(tm, N_QB * LANES), _row_spec(tm, kw), _row_spec(tm, kw), _row_spec(tm, QKV_W),
                  _full_spec((1, LANES)), _full_spec((1, LANES)), _row_spec(tm, LANES), _row_spec(tm, LANES),
                  _full_spec((LANES, LANES))],
        out_specs=[_row_spec(tm, QKV_W), _full_spec((SUBLANES, LANES))],
        out_shape=[jax.ShapeDtypeStruct((lp, QKV_W), BF16), jax.ShapeDtypeStruct((SUBLANES, LANES), F32)],
        compiler_params=_params(("arbitrary",)),
    )(dqs, dk2, dv2, raw, gq, gk, cos, sin, e)


NEG = -1e30
Q_PER_KV = N_Q_HEADS // N_KV_HEADS


def _half_masks(x):
    lane = lax.broadcasted_iota(jnp.int32, x.shape, 1)
    zero = jnp.zeros_like(x)
    return jnp.where(lane < HEAD_DIM, x, zero), jnp.where(lane >= HEAD_DIM, x, zero)


ATTN_TR = 16


def _attn_tiles(lp):
    t = _pick_tile(lp, (1408, 256))
    return t, t


def _attn_tiles_bwd(lp):
    return _pick_tile(lp, (768, 256)), _pick_tile(lp, (1408, 256))


def attn_forward_t(qs, k2, v2t, n_valid, prefetch=()):
    lp = qs.shape[0]
    tq, kb = _attn_tiles(lp)
    nk = lp // kb
    gw = 2 * LANES
    nr = kb // ATTN_TR
    pad0 = n_valid - (nk - 1) * kb
    npf = len(prefetch)
    pf_specs, pf_shapes, pf_sems = _prefetch_specs(prefetch)
    nq = lp // tq

    def body(*refs):
        q_ref, k_ref, vt_ref = refs[:3]
        pf_ins = refs[3:3 + npf]
        o_ref, lse_ref = refs[3 + npf:5 + npf]
        pf_outs = refs[5 + npf:5 + 2 * npf]
        m_s, l_s, acc_s, s_s, p_s = refs[5 + 2 * npf:10 + 2 * npf]
        j = pl.program_id(2)
        first = (pl.program_id(0) == 0) & (pl.program_id(1) == 0) & (j == 0)
        last = (pl.program_id(0) == N_KV_HEADS - 1) & (pl.program_id(1) == nq - 1) & (j == nk - 1)
        _prefetch_run(first, last, pf_ins, pf_outs, refs[10 + 2 * npf:])

        @pl.when(j == 0)
        def _():
            m_s[...] = jnp.full_like(m_s, NEG)
            l_s[...] = jnp.zeros_like(l_s)
            acc_s[...] = jnp.zeros_like(acc_s)

        ks = _half_masks(k_ref[...])
        for pair in range(2):
            qp = q_ref[:, pair * LANES:(pair + 1) * LANES]
            for half in range(2):
                hh = 2 * pair + half
                s_s[...] = _nt(ks[half], qp)

                if pad0 < kb:
                    @pl.when(j == nk - 1)
                    def _():
                        s_s[pad0:, :] = jnp.full((kb - pad0, tq), NEG, F32)

                run = s_s[pl.ds(0, ATTN_TR), :]
                for r in range(1, nr):
                    run = jnp.maximum(run, s_s[pl.ds(r * ATTN_TR, ATTN_TR), :])
                m_prev = m_s[hh:hh + 1, :]
                m_new = jnp.maximum(m_prev, jnp.max(run, axis=0, keepdims=True))
                alpha = jnp.exp2(m_prev - m_new)
                m_s[hh:hh + 1, :] = m_new
                for r in range(nr):
                    rows = pl.ds(r * ATTN_TR, ATTN_TR)
                    p_s[rows, :] = jnp.exp2(s_s[rows, :] - m_new).astype(BF16)
                vt = jnp.concatenate([vt_ref[half * HEAD_DIM:(half + 1) * HEAD_DIM, :],
                                      jnp.ones((ATTN_TR, kb), BF16)], axis=0)
                pv = _nn(vt, p_s[...])
                l_s[hh:hh + 1, :] = alpha * l_s[hh:hh + 1, :] + pv[HEAD_DIM:HEAD_DIM + 1, :]
                pv = pv[:HEAD_DIM, :]
                rs = slice(half * HEAD_DIM, (half + 1) * HEAD_DIM)
                acc_s[pair, rs, :] = alpha * acc_s[pair, rs, :] + pv

        @pl.when(j == nk - 1)
        def _():
            for pair in range(2):
                for half in range(2):
                    hh = 2 * pair + half
                    rs = slice(half * HEAD_DIM, (half + 1) * HEAD_DIM)
                    acc_s[pair, rs, :] = acc_s[pair, rs, :] * (1.0 / l_s[hh:hh + 1, :])
                o_ref[:, pair * LANES:(pair + 1) * LANES] = acc_s[pair].T.astype(BF16)
            for hh in range(Q_PER_KV):
                lse_ref[0, hh] = m_s[hh:hh + 1, :] + jnp.log2(l_s[hh:hh + 1, :])

    return pl.pallas_call(
        body, name="attn_forward", grid=(N_KV_HEADS, lp // tq, nk),
        in_specs=[pl.BlockSpec((tq, gw), lambda g, i, j: (i, g)), pl.BlockSpec((kb, LANES), lambda g, i, j: (j, g)),
                  pl.BlockSpec((LANES, kb), lambda g, i, j: (g, j))] + pf_specs,
        out_specs=[pl.BlockSpec((tq, gw), lambda g, i, j: (i, g)),
                   pl.BlockSpec((1, Q_PER_KV, 1, tq), lambda g, i, j: (g, 0, 0, i))] + pf_specs,
        out_shape=[jax.ShapeDtypeStruct((lp, N_QB * LANES), BF16),
                   jax.ShapeDtypeStruct((N_KV_HEADS, Q_PER_KV, 1, lp), F32)] + pf_shapes,
        scratch_shapes=[pltpu.VMEM((SUBLANES, tq), F32), pltpu.VMEM((SUBLANES, tq), F32),
                        pltpu.VMEM((2, LANES, tq), F32), pltpu.VMEM((kb, tq), F32), pltpu.VMEM((kb, tq), BF16)] + pf_sems,
        compiler_params=_params(("arbitrary",) * 3 if npf else ("parallel", "parallel", "arbitrary")),
    )(qs, k2, v2t, *prefetch)


def attn_backward(qs, qst, k2, v2, do, dot, lse, delta, n_valid):
    lp = qs.shape[0]
    tq, kb = _attn_tiles_bwd(lp)
    nq, nk = lp // tq, lp // kb
    gw = 2 * LANES
    pad0 = n_valid - (nk - 1) * kb

    def body(q_ref, qt_ref, k_ref, v_ref, do_ref, dot_ref, lse_ref, dl_ref, dq_ref, dk_ref, dv_ref, acc_s, dkt_s, dvt_s):
        g = pl.program_id(0)
        i = pl.program_id(1)
        j = pl.program_id(2)
        cols = pl.ds(pl.multiple_of(j * kb, kb), kb)

        @pl.when(j == 0)
        def _():
            acc_s[...] = jnp.zeros_like(acc_s)

        @pl.when(i == 0)
        def _():
            dkt_s[:, cols] = jnp.zeros((LANES, kb), F32)
            dvt_s[:, cols] = jnp.zeros((LANES, kb), F32)

        head = lax.broadcasted_iota(jnp.int32, (tq, N_Q_HEADS), 1)

        def column(ref, hh):
            return jnp.sum(jnp.where(head == Q_PER_KV * g + hh, ref[...], 0.0), axis=1, keepdims=True)

        def step(masked):
            ks = _half_masks(k_ref[...])
            vs = _half_masks(v_ref[...])
            if masked:
                col = lax.broadcasted_iota(jnp.int32, (1, kb), 1)
                bias = jnp.where(col < pad0, 0.0, NEG)
            for pair in range(2):
                qp = q_ref[:, pair * LANES:(pair + 1) * LANES]
                dop = do_ref[:, pair * LANES:(pair + 1) * LANES]
                for half in range(2):
                    hh = 2 * pair + half
                    rs = slice(half * HEAD_DIM, (half + 1) * HEAD_DIM)
                    rt = slice(pair * LANES + half * HEAD_DIM, pair * LANES + (half + 1) * HEAD_DIM)
                    s = _nt(qp, ks[half])
                    if masked:
                        s = s + bias
                    p = jnp.exp2(s - column(lse_ref, hh))
                    dp = _nt(dop, vs[half])
                    ds = (p * (dp - column(dl_ref, hh))).astype(BF16)
                    pb = p.astype(BF16)
                    acc_s[pair] += _nn(ds, ks[half])
                    dvt_s[rs, cols] += _nn(dot_ref[rt, :], pb)
                    dkt_s[rs, cols] += _nn(qt_ref[rt, :], ds)

        if pad0 < kb:
            pl.when(j < nk - 1)(lambda: step(False))
            pl.when(j == nk - 1)(lambda: step(True))
        else:
            step(False)

        @pl.when(j == nk - 1)
        def _():
            for pair in range(2):
                dq_ref[:, pair * LANES:(pair + 1) * LANES] = acc_s[pair]

        @pl.when(i == nq - 1)
        def _():
            dk_ref[cols, :] = dkt_s[:, cols].T
            dv_ref[cols, :] = dvt_s[:, cols].T

    cspec = pl.BlockSpec((tq, N_Q_HEADS), lambda g, i, j: (i, 0))
    qspec = pl.BlockSpec((tq, gw), lambda g, i, j: (i, g))
    tspec = pl.BlockSpec((gw, tq), lambda g, i, j: (g, i))
    kspec = pl.BlockSpec((kb, LANES), lambda g, i, j: (j, g))
    gspec = pl.BlockSpec((lp, LANES), lambda g, i, j: (0, g))
    return pl.pallas_call(
        body, name="attn_backward", grid=(N_KV_HEADS, nq, nk),
        in_specs=[qspec, tspec, kspec, kspec, qspec, tspec, cspec, cspec],
        out_specs=[qspec, gspec, gspec],
        out_shape=[jax.ShapeDtypeStruct((lp, N_QB * LANES), F32),
                   jax.ShapeDtypeStruct((lp, N_KB * LANES), F32), jax.ShapeDtypeStruct((lp, N_KB * LANES), F32)],
        scratch_shapes=[pltpu.VMEM((2, tq, LANES), F32), pltpu.VMEM((LANES, lp), F32), pltpu.VMEM((LANES, lp), F32)],
        compiler_params=_params(("parallel", "arbitrary", "arbitrary")),
    )(qs, qst, k2, v2, do, dot, lse, delta)


def attn_out_backward(dr, wout, o, e16):
    lp, d = dr.shape
    tm = _tm(lp)

    def body(dr_ref, w_ref, o_ref, e_ref, do_ref, dl_ref, dot_ref):
        do32 = _nt(dr_ref[...].astype(BF16), w_ref[...])
        do = do32.astype(BF16)
        do_ref[...] = do
        for cb in range(d // LANES):
            dot_ref[cb * LANES:(cb + 1) * LANES, :] = do32[:, cb * LANES:(cb + 1) * LANES].T.astype(BF16)
        dl_ref[...] = _segsum(do.astype(F32) * o_ref[...].astype(F32), e_ref[...])

    rs = _row_spec(tm, d)
    return pl.pallas_call(
        body, name="attn_out_backward", grid=(lp // tm,),
        in_specs=[rs, _full_spec((d, d)), rs, _full_spec((d, N_Q_HEADS))],
        out_specs=[rs, _row_spec(tm, N_Q_HEADS), pl.BlockSpec((d, tm), lambda i: (0, i))],
        out_shape=[jax.ShapeDtypeStruct((lp, d), BF16), jax.ShapeDtypeStruct((lp, N_Q_HEADS), F32),
                   jax.ShapeDtypeStruct((d, lp), BF16)],
        compiler_params=_params(("parallel",)),
    )(dr, wout, o, e16)


N_CHIPS = 4


def _mesh_pos():
    return lax.axis_index("x"), lax.axis_index("y"), lax.axis_index("c")


def chip_exchange(arrs, scatter, name):
    n = len(arrs)
    hbm = pl.BlockSpec(memory_space=pl.ANY)

    def body(*refs):
        ins, outs = refs[:n], refs[n:2 * n]
        send_sems, recv_sems, loc_sems = refs[2 * n:]
        x, y, c = _mesh_pos()
        me = 2 * x + y
        chips = [(1 - x, y), (x, 1 - y), (1 - x, 1 - y)]
        started = []
        for a in range(n):
            loc = pltpu.make_async_copy(ins[a].at[me] if scatter else ins[a], outs[a].at[me], loc_sems.at[a])
            loc.start()
            started.append(loc)
            for k, (px, py) in enumerate(chips):
                src = ins[a].at[2 * px + py] if scatter else ins[a]
                cp = pltpu.make_async_remote_copy(
                    src_ref=src, dst_ref=outs[a].at[me], send_sem=send_sems.at[3 * a + k], recv_sem=recv_sems.at[3 * a + k],
                    device_id=(px, py, c), device_id_type=MESH)
                cp.start()
                started.append(cp)
        for cp in started:
            cp.wait()

    out_shape = [jax.ShapeDtypeStruct(a.shape if scatter else (N_CHIPS,) + a.shape, a.dtype) for a in arrs]
    return pl.pallas_call(
        body, name=name, in_specs=[hbm] * n, out_specs=[hbm] * n, out_shape=out_shape,
        scratch_shapes=[pltpu.SemaphoreType.DMA((3 * n,)), pltpu.SemaphoreType.DMA((3 * n,)), pltpu.SemaphoreType.DMA((n,))],
    )(*arrs)


def _same_core_copies(ins, outs, send_sems, recv_sems, loc_sems):
    x, y, c = _mesh_pos()
    me = 2 * x + y
    chips = [(1 - x, y), (x, 1 - y), (1 - x, 1 - y)]
    cps = []
    for a in range(len(ins)):
        cps.append(pltpu.make_async_copy(ins[a], outs[a].at[me], loc_sems.at[a]))
        for k, (px, py) in enumerate(chips):
            cps.append(pltpu.make_async_remote_copy(
                src_ref=ins[a], dst_ref=outs[a].at[me], send_sem=send_sems.at[3 * a + k], recv_sem=recv_sems.at[3 * a + k],
                device_id=(px, py, c), device_id_type=MESH))
    return cps


def _prefetch_specs(arrs):
    n = len(arrs)
    hbm = pl.BlockSpec(memory_space=pl.ANY)
    shapes = [jax.ShapeDtypeStruct((N_CHIPS,) + a.shape, a.dtype) for a in arrs]
    sems = [pltpu.SemaphoreType.DMA((3 * n,)), pltpu.SemaphoreType.DMA((3 * n,)), pltpu.SemaphoreType.DMA((n,))] if n else []
    return [hbm] * n, shapes, sems


def _prefetch_run(first, last, ins, outs, sems):
    if not ins:
        return

    @pl.when(first)
    def _():
        for cp in _same_core_copies(ins, outs, *sems):
            cp.start()

    @pl.when(last)
    def _():
        for cp in _same_core_copies(ins, outs, *sems):
            cp.wait()


def gather_two_level(arrs, name):
    n = len(arrs)
    hbm = pl.BlockSpec(memory_space=pl.ANY)

    def body(*refs):
        ins, outs = refs[:n], refs[n:2 * n]
        ici_send, ici_recv, d2d_send, d2d_recv, loc_sems = refs[2 * n:]
        x, y, c = _mesh_pos()
        me = 2 * x + y
        chips = [(1 - x, y), (x, 1 - y), (1 - x, 1 - y)]
        started = []
        for a in range(n):
            hn = arrs[a].shape[0] // 2
            mine = pl.ds(c * hn, hn)
            loc = pltpu.make_async_copy(ins[a], outs[a].at[me], loc_sems.at[a])
            loc.start()
            started.append(loc)
            first = []
            for k, (px, py) in enumerate(chips):
                cp = pltpu.make_async_remote_copy(
                    src_ref=ins[a].at[mine], dst_ref=outs[a].at[me, mine], send_sem=ici_send.at[3 * a + k],
                    recv_sem=ici_recv.at[3 * a + k], device_id=(px, py, c), device_id_type=MESH)
                cp.start()
                first.append(cp)
            for k, (px, py) in enumerate(chips):
                q = 2 * px + py
                first[k].wait_recv()
                fw = pltpu.make_async_remote_copy(
                    src_ref=outs[a].at[q, mine], dst_ref=outs[a].at[q, mine], send_sem=d2d_send.at[3 * a + k],
                    recv_sem=d2d_recv.at[3 * a + k], device_id=(x, y, 1 - c), device_id_type=MESH)
                fw.start()
                started.append(fw)
            for cp in first:
                cp.wait_send()
        for cp in started:
            cp.wait()

    out_shape = [jax.ShapeDtypeStruct((N_CHIPS,) + a.shape, a.dtype) for a in arrs]
    return pl.pallas_call(
        body, name=name, in_specs=[hbm] * n, out_specs=[hbm] * n, out_shape=out_shape,
        scratch_shapes=[pltpu.SemaphoreType.DMA((3 * n,))] * 4 + [pltpu.SemaphoreType.DMA((n,))],
    )(*arrs)


def sibling_exchange(arrs, name):
    n = len(arrs)
    hbm = pl.BlockSpec(memory_space=pl.ANY)

    def body(*refs):
        ins, outs = refs[:n], refs[n:2 * n]
        send_sems, recv_sems = refs[2 * n:]
        x, y, c = _mesh_pos()
        started = []
        for a in range(n):
            cp = pltpu.make_async_remote_copy(
                src_ref=ins[a], dst_ref=outs[a], send_sem=send_sems.at[a], recv_sem=recv_sems.at[a],
                device_id=(x, y, 1 - c), device_id_type=MESH)
            cp.start()
            started.append(cp)
        for cp in started:
            cp.wait()

    return pl.pallas_call(
        body, name=name, in_specs=[hbm] * n, out_specs=[hbm] * n,
        out_shape=[jax.ShapeDtypeStruct(a.shape, a.dtype) for a in arrs],
        scratch_shapes=[pltpu.SemaphoreType.DMA((n,)), pltpu.SemaphoreType.DMA((n,))],
    )(*arrs)


def _rows_tile(r, c):
    return _pick_tile(r, tuple(t for t in (512, 256, 128, 64, 32, 16, 8) if t * c * 4 <= 2 * 1024 * 1024))


def chip_sum(recv, name):
    _, r, c = recv.shape
    tr = _rows_tile(r, c)

    def body(r_ref, o_ref):
        acc = r_ref[0].astype(F32)
        for q in range(1, N_CHIPS):
            acc = acc + r_ref[q].astype(F32)
        o_ref[...] = acc

    return pl.pallas_call(
        body, name=name, grid=(r // tr,),
        in_specs=[pl.BlockSpec((N_CHIPS, tr, c), lambda i: (0, i, 0))],
        out_specs=pl.BlockSpec((tr, c), lambda i: (i, 0)),
        out_shape=jax.ShapeDtypeStruct((r, c), F32),
        compiler_params=_params(("parallel",)),
    )(recv)


def pair_sum(part, sib, name, dtype=F32):
    r, c = part.shape
    tr = _rows_tile(r, c)

    def body(p_ref, s_ref, o_ref):
        o_ref[...] = (p_ref[...].astype(F32) + s_ref[...].astype(F32)).astype(dtype)

    rs = pl.BlockSpec((tr, c), lambda i: (i, 0))
    return pl.pallas_call(
        body, name=name, grid=(r // tr,), in_specs=[rs] * 2, out_specs=rs,
        out_shape=jax.ShapeDtypeStruct((r, c), dtype), compiler_params=_params(("parallel",)),
    )(part, sib)


def adamw(g, w, m, v, name):
    r, c = w.shape
    tr = _rows_tile(r, c)

    def body(g_ref, w_ref, m_ref, v_ref, d_ref, nm_ref, nv_ref):
        g_ = g_ref[...]
        m_ = ADAM_B1 * m_ref[...] + (1.0 - ADAM_B1) * g_
        v_ = ADAM_B2 * v_ref[...] + (1.0 - ADAM_B2) * (g_ * g_)
        m_hat = m_ / (1.0 - ADAM_B1 ** ADAM_STEP)
        v_hat = v_ / (1.0 - ADAM_B2 ** ADAM_STEP)
        d_ref[...] = -ADAM_LR * (m_hat / (jnp.sqrt(v_hat) + ADAM_EPS) + ADAM_WD * w_ref[...])
        nm_ref[...] = m_
        nv_ref[...] = v_

    rs = pl.BlockSpec((tr, c), lambda i: (i, 0))
    return pl.pallas_call(
        body, name=name, grid=(r // tr,), in_specs=[rs] * 4, out_specs=[rs] * 3,
        out_shape=[jax.ShapeDtypeStruct((r, c), F32)] * 3,
        compiler_params=_params(("parallel",)),
    )(g, w, m, v)


def adamw_halves(mine, other, core, w, m, v, name):
    r, c = w.shape
    tr = _rows_tile(r // 2, c)
    th = (r // 2) // tr

    def body(core_ref, a_ref, b_ref, w_ref, m_ref, v_ref, g_ref, d_ref, nm_ref, nv_ref):
        g_ = jnp.where(pl.program_id(0) // th == core_ref[0], a_ref[...], b_ref[...])
        m_ = ADAM_B1 * m_ref[...] + (1.0 - ADAM_B1) * g_
        v_ = ADAM_B2 * v_ref[...] + (1.0 - ADAM_B2) * (g_ * g_)
        m_hat = m_ / (1.0 - ADAM_B1 ** ADAM_STEP)
        v_hat = v_ / (1.0 - ADAM_B2 ** ADAM_STEP)
        g_ref[...] = g_
        d_ref[...] = -ADAM_LR * (m_hat / (jnp.sqrt(v_hat) + ADAM_EPS) + ADAM_WD * w_ref[...])
        nm_ref[...] = m_
        nv_ref[...] = v_

    half = pl.BlockSpec((tr, c), lambda i, core_ref: (i % th, 0))
    rows = pl.BlockSpec((tr, c), lambda i, core_ref: (i, 0))
    return pl.pallas_call(
        body, name=name,
        grid_spec=pltpu.PrefetchScalarGridSpec(num_scalar_prefetch=1, grid=(r // tr,), in_specs=[half, half, rows, rows, rows],
                                               out_specs=[rows] * 4),
        out_shape=[jax.ShapeDtypeStruct((r, c), F32)] * 4,
        compiler_params=_params(("parallel",)),
    )(core, mine, other, w, m, v)


WEIGHTS = ['meta_tokens', 's5_lambda_re', 's5_lambda_im', 's5_log_dt', 's5_b_re', 's5_b_im', 's5_c_re', 's5_c_im', 's5_d',
           's5_w_glu', 's5_w_out', 'attn_w_qkv', 'attn_q_gain', 'attn_k_gain', 'attn_w_out', 'ffn_w_gate', 'ffn_w_up',
           'ffn_w_down', 'ln_gain', 'ln_bias']
BIG = ['s5_w_glu', 's5_w_out', 'attn_w_qkv', 'attn_w_out', 'ffn_w_gate', 'ffn_w_up', 'ffn_w_down']
ROW_SHARDED = {'s5_w_glu', 's5_w_out', 'attn_w_out', 'ffn_w_down'}
SMALL_SHARDED = ['meta_tokens', 'ln_gain', 'ln_bias']
REPLICATED = ['s5_lambda_re', 's5_lambda_im', 's5_log_dt', 's5_b_re', 's5_b_im', 's5_c_re', 's5_c_im', 's5_d',
              'attn_q_gain', 'attn_k_gain']
REP_ALIGN = N_CHIPS * LANES * LANES


def _natural(gathered, row_sharded):
    p, n, a, b = gathered.shape
    if row_sharded:
        return jnp.transpose(gathered, (1, 0, 2, 3)).reshape(n, p * a, b)
    return jnp.transpose(gathered, (1, 2, 0, 3)).reshape(n, a, p * b)


def _shard_major(full, row_sharded):
    n, a, b = full.shape
    if row_sharded:
        return jnp.transpose(full.reshape(n, N_CHIPS, a // N_CHIPS, b), (1, 0, 2, 3))
    return jnp.transpose(full.reshape(n, a, N_CHIPS, b // N_CHIPS), (2, 0, 1, 3))


def _dup_heads(w):
    lead = w.shape[:-1]
    w = w.reshape(lead + (N_KV_HEADS, 1, HEAD_DIM))
    return jnp.broadcast_to(w, lead + (N_KV_HEADS, 2, HEAD_DIM)).reshape(lead + (N_KV_HEADS * 2 * HEAD_DIM,))


def _fold_heads(d):
    lead = d.shape[:-1]
    return d.reshape(lead + (N_KV_HEADS, 2, HEAD_DIM)).sum(axis=-2).reshape(lead + (N_KV_HEADS * HEAD_DIM,))


def _pack_rep(tree, extra=None):
    extra = jnp.zeros((1,), F32) if extra is None else extra.reshape(1)
    flat = jnp.concatenate([tree[n].reshape(-1) for n in REPLICATED] + [extra])
    pad = _round_up(flat.shape[0], REP_ALIGN) - flat.shape[0]
    return jnp.pad(flat, (0, pad))


def _unpack_rep(flat, like):
    out, off = {}, 0
    for n in REPLICATED:
        size = math.prod(like[n].shape)
        out[n] = flat[off:off + size].reshape(like[n].shape)
        off += size
    return out


def _train_step(x, loss_target, w, mom, vel):
    s = x.shape[1]
    n_valid = N_META + s
    lp = _round_up(n_valid, 2 * LANES)
    nq = N_Q_HEADS * HEAD_DIM
    nkv = N_KV_HEADS * HEAD_DIM

    small = jnp.concatenate([w[n].reshape(-1, w[n].shape[-1]) for n in SMALL_SHARDED], axis=0)
    shard = {n: w[n].astype(BF16) for n in BIG}
    uses = [[('s5_w_glu', 0), ('s5_w_out', 0), ('ffn_w_gate', 0), ('ffn_w_up', 0), ('ffn_w_down', 0)],
            [('attn_w_qkv', 0), ('attn_w_out', 0), ('ffn_w_gate', 1), ('ffn_w_up', 1), ('ffn_w_down', 1)],
            [('s5_w_glu', 1), ('s5_w_out', 1), ('ffn_w_gate', 2), ('ffn_w_up', 2), ('ffn_w_down', 2),
             ('attn_w_qkv', 1), ('attn_w_out', 1), ('ffn_w_gate', 3), ('ffn_w_up', 3), ('ffn_w_down', 3)]]
    full = {}

    def unpack(stage, gathered):
        for (n, l), g in zip(uses[stage], gathered):
            full[(n, l)] = _natural(g[:, None], n in ROW_SHARDED)[0]

    first = gather_two_level([shard[n][l] for n, l in uses[0]] + [small], "gather_weights")
    unpack(0, first[:-1])
    small_full = jnp.transpose(first[-1], (1, 0, 2)).reshape(small.shape[0], D_MODEL)
    meta_full = small_full[:N_META]
    ln_gain = small_full[N_META:N_META + 2 * DEPTH].reshape(DEPTH, 2, 1, D_MODEL)
    ln_bias = small_full[N_META + 2 * DEPTH:].reshape(DEPTH, 2, 1, D_MODEL)

    def qkv_dup(wqkv):
        return jnp.concatenate([wqkv[..., :nq], _dup_heads(wqkv[..., nq:nq + nkv]), _dup_heads(wqkv[..., nq + nkv:])], axis=-1)

    w2 = {}

    cos, sin = rope_tables(lp, n_valid)
    e128 = head_sum_matrix()
    e16 = jnp.kron(jnp.eye(N_Q_HEADS, dtype=F32), jnp.ones((HEAD_DIM, 1), F32)).astype(BF16)
    gq = jnp.tile(w['attn_q_gain'], (1, 2))[:, None, :]
    gk = jnp.tile(w['attn_k_gain'], (1, 2))[:, None, :]

    pad_rows = jnp.zeros((lp - n_valid, D_MODEL), F32)
    h = jnp.concatenate([meta_full, x[0], pad_rows], axis=0)
    tgt = jnp.concatenate([jnp.zeros((N_META, D_MODEL), F32), loss_target[0], pad_rows], axis=0)

    saved = []
    s5_names = ['s5_lambda_re', 's5_lambda_im', 's5_log_dt', 's5_b_re', 's5_b_im', 's5_c_re', 's5_c_im']
    for i in range(DEPTH):
        j = i // 2
        sv = {'h': h}
        if i % 2 == 0:
            ops, sv['prep_vjp'] = jax.vjp(s5_prep, *[w[n][j] for n in s5_names])
            m_, wx_, ci_, at_ = ops
            two = lambda t: t.reshape((2 * S5_NJ,) + t.shape[2:])
            sv['ops'] = (blockdiag_expand(m_, S5_CH, S5_CH, "s5_expand_m"),
                         blockdiag_expand(two(wx_), S5_CH, S5_STATE, "s5_expand_wx").reshape(2, S5_NJ, S5_W, S5_W),
                         blockdiag_expand(two(ci_), S5_STATE, S5_CH, "s5_expand_ci").reshape(2, S5_NJ, S5_W, S5_W), at_)
            pf = [shard[n][l] for n, l in uses[1]] if i == 0 else []
            y, sv['lhs'], sv['sp'], sv['sn'], *got = s5_forward(h, *sv['ops'], n_valid, prefetch=pf)
            if i == 0:
                unpack(1, got)
            sv['v'], sv['t'], sv['g'], sv['z'] = glu_forward(y, h, w['s5_d'][j][None], full['s5_w_glu', j])
            sv['r1'], h1 = proj_ln_forward(sv['z'], full['s5_w_out', j], h, ln_gain[i, 0], ln_bias[i, 0], "s5_out_ln")
        else:
            w2[j] = qkv_dup(full['attn_w_qkv', j])
            sv['raw'], sv['qs'], sv['k2'], sv['v2'], sv['qst'], v2t = qkv_forward(h, w2[j], gq[j], gk[j], cos, sin, e128)
            pf = [shard[n][l] for n, l in uses[2]] if i == 1 else []
            sv['o'], lse, *got = attn_forward_t(sv['qs'], sv['k2'], v2t, n_valid, prefetch=pf)
            if i == 1:
                unpack(2, got)
            sv['lse'] = lse.reshape(N_Q_HEADS, lp).T
            sv['r1'], h1 = proj_ln_forward(sv['o'], full['attn_w_out', j], h, ln_gain[i, 0], ln_bias[i, 0], "attn_out_ln")
        sv['h1'] = h1
        sv['a'], sv['b'], sv['f'] = ffn_up_forward(h1, full['ffn_w_gate', i], full['ffn_w_up', i])
        sv['r2'], h = proj_ln_forward(sv['f'], full['ffn_w_down', i], h1, ln_gain[i, 1], ln_bias[i, 1], "ffn_down_ln")
        saved.append(sv)

    dh, loss_part = loss_backward(h, tgt, n_valid)
    loss_local = jnp.sum(loss_part)

    gfull = {n: None for n in BIG}
    gqkv = [None] * w['attn_w_qkv'].shape[0]

    def grad_into(n, l, x, y, name):
        gfull[n] = mm_tn(x, y, name, layers=w[n].shape[0], layer=l, stack=gfull[n])

    d_ln_gain = [[None, None] for _ in range(DEPTH)]
    d_ln_bias = [[None, None] for _ in range(DEPTH)]
    grep = {n: [None] * w[n].shape[0] for n in REPLICATED}
    pending = None
    for i in reversed(range(DEPTH)):
        j = i // 2
        sv = saved[i]
        if pending is None:
            dr2, s2 = ln_backward(dh, sv['r2'], ln_gain[i, 1])
        else:
            dr2, s2 = pending
            pending = None
        d_ln_gain[i][1], d_ln_bias[i][1] = s2[0], s2[1]
        da, db = ffn_backward_act(dr2, full['ffn_w_down', i], sv['a'], sv['b'])
        grad_into('ffn_w_down', i, sv['f'], dr2, "grad_ffn_down")
        dr1, s1 = resid_nt(dr2, [da, db], [full['ffn_w_gate', i], full['ffn_w_up', i]], "ffn_backward_x",
                           ln=(sv['r1'], ln_gain[i, 0]))
        grad_into('ffn_w_gate', i, sv['h1'], da, "grad_ffn_gate")
        grad_into('ffn_w_up', i, sv['h1'], db, "grad_ffn_up")
        d_ln_gain[i][0], d_ln_bias[i][0] = s1[0], s1[1]
        if i % 2 == 0:
            dt, dgd = glu_backward1(dr1, full['s5_w_out', j], sv['g'], sv['t'])
            grad_into('s5_w_out', j, sv['z'], dr1, "grad_s5_out")
            dv, dhs, sd = glu_backward2(dt, dgd, full['s5_w_glu', j], sv['v'], sv['h'], w['s5_d'][j][None], dr1)
            grep['s5_d'][j] = sd[0]
            grad_into('s5_w_glu', j, sv['g'], dt, "grad_s5_glu")
            dh, ldy, dxf, dxr, daf, dar = s5_backward(dv, dhs, *sv['ops'], sv['sp'], sv['sn'], n_valid)
            dm, dwxf, dwxr, dcif, dcir = s5_operator_grads(sv['lhs'], ldy, dxf, dxr, sv['sp'], sv['sn'])
            dwx = jnp.stack([dwxf, dwxr])
            dci = jnp.stack([dcif, dcir])
            dps = sv['prep_vjp']((dm, dwx, dci, jnp.stack([daf, dar])))
            for n, g in zip(s5_names, dps):
                grep[n][j] = g
        else:
            do, delta, dot = attn_out_backward(dr1, full['attn_w_out', j], sv['o'], e16)
            grad_into('attn_w_out', j, sv['o'], dr1, "grad_attn_out")
            dq, dk2, dv2 = attn_backward(sv['qs'], sv['qst'], sv['k2'], sv['v2'], do, dot, sv['lse'], delta, n_valid)
            draw, gs = qkv_backward(dq, dk2, dv2, sv['raw'], gq[j], gk[j], cos, sin, e128)
            grep['attn_q_gain'][j] = gs[0, :HEAD_DIM] + gs[0, HEAD_DIM:]
            grep['attn_k_gain'][j] = gs[1, :HEAD_DIM] + gs[1, HEAD_DIM:]
            pending = resid_nt(dr1, [draw], [w2[j]], "attn_backward_x", ln=(saved[i - 1]['r2'], ln_gain[i - 1, 1]))
            dw2 = mm_tn(sv['h'], draw, "grad_attn_qkv")[0]
            kq = N_QB * LANES
            kk = N_KB * LANES
            gqkv[j] = jnp.concatenate(
                [dw2[:, :kq], _fold_heads(dw2[:, kq:kq + kk]), _fold_heads(dw2[:, kq + kk:])], axis=1)
    grad_x = dh[N_META:n_valid][None]

    core = lax.axis_index("c")
    gfull['attn_w_qkv'] = jnp.stack(gqkv)
    contrib = [_shard_major(gfull[n], n in ROW_SHARDED) for n in BIG]
    small_g = jnp.concatenate([dh[:N_META], jnp.stack([g for pair in d_ln_gain for g in pair]),
                               jnp.stack([g for pair in d_ln_bias for g in pair])], axis=0)
    contrib.append(jnp.transpose(small_g.reshape(-1, N_CHIPS, D_MODEL // N_CHIPS), (1, 0, 2)))
    rep_g = _pack_rep({n: jnp.stack(grep[n]) for n in REPLICATED}, loss_local)
    contrib.append(rep_g.reshape(N_CHIPS, -1, LANES))
    names = BIG + ['small', 'rep']
    wire = [BF16] * len(BIG) + [F32, F32]
    keep, give = [], []
    for t, dt in zip(contrib, wire):
        hn = t.shape[1] // 2
        keep.append(lax.dynamic_slice_in_dim(t, core * hn, hn, axis=1))
        give.append(lax.dynamic_slice_in_dim(t, (1 - core) * hn, hn, axis=1).astype(dt))
    got = sibling_exchange(give, "sibling_contrib")
    two_d = lambda t: t.reshape(-1, t.shape[-1])
    pair = [pair_sum(two_d(a), two_d(b), "pair_sum_" + n, dt).reshape(a.shape)
            for n, a, b, dt in zip(names, keep, got, wire)]
    recv = chip_exchange(pair, True, "scatter_grads")
    halves = [chip_sum(r.reshape(N_CHIPS, -1, r.shape[-1]), "chip_sum_" + n) for n, r in zip(names, recv)]
    others = sibling_exchange(halves, "sibling_halves")
    core_arr = jnp.reshape(core, (1,)).astype(jnp.int32)

    out = {}

    def update_halves(n, a, b, wn, mn, vn):
        shape = wn.shape
        flat = (-1, shape[-1])
        res = adamw_halves(a, b, core_arr, wn.reshape(flat), mn.reshape(flat), vn.reshape(flat), "adamw_" + n)
        return tuple(t.reshape(shape) for t in res)

    for n, a, b in zip(BIG, halves, others):
        out[n] = update_halves(n, a, b, w[n], mom[n], vel[n])
    cat = lambda tree: jnp.concatenate([tree[n].reshape(-1, tree[n].shape[-1]) for n in SMALL_SHARDED], axis=0)
    sm = update_halves("small", halves[-2], others[-2], cat(w), cat(mom), cat(vel))
    off = 0
    for n in SMALL_SHARDED:
        rows = math.prod(w[n].shape[:-1])
        out[n] = tuple(t[off:off + rows].reshape(w[n].shape) for t in sm)
        off += rows
    rep_quarter = jnp.where(core == 0, jnp.concatenate([halves[-1], others[-1]], axis=0),
                            jnp.concatenate([others[-1], halves[-1]], axis=0))

    def update(n, g, wn, mn, vn):
        shape = wn.shape
        flat = (-1, shape[-1])
        d, nm, nv = adamw(g, wn.reshape(flat), mn.reshape(flat), vn.reshape(flat), "adamw_" + n)
        return tuple(t.reshape(shape) for t in (g, d, nm, nv))

    rep_all = chip_exchange([rep_quarter], False, "gather_rep")[0].reshape(-1, LANES)
    rp = update("rep", rep_all, _pack_rep(w).reshape(-1, LANES), _pack_rep(mom).reshape(-1, LANES),
                _pack_rep(vel).reshape(-1, LANES))
    unpacked = [_unpack_rep(t.reshape(-1), w) for t in rp]
    loss = rp[0].reshape(-1)[sum(math.prod(w[n].shape) for n in REPLICATED)]
    for n in REPLICATED:
        out[n] = tuple(u[n] for u in unpacked)

    return (loss, grad_x, *[out[n][0] for n in WEIGHTS], *[out[n][1] for n in WEIGHTS],
            *[out[n][2] for n in WEIGHTS], *[out[n][3] for n in WEIGHTS])


def kernel(x, meta_tokens, s5_lambda_re, s5_lambda_im, s5_log_dt, s5_b_re, s5_b_im, s5_c_re, s5_c_im, s5_d, s5_w_glu, s5_w_out, attn_w_qkv, attn_q_gain, attn_k_gain, attn_w_out, ffn_w_gate, ffn_w_up, ffn_w_down, ln_gain, ln_bias, loss_target, m_meta_tokens, m_s5_lambda_re, m_s5_lambda_im, m_s5_log_dt, m_s5_b_re, m_s5_b_im, m_s5_c_re, m_s5_c_im, m_s5_d, m_s5_w_glu, m_s5_w_out, m_attn_w_qkv, m_attn_q_gain, m_attn_k_gain, m_attn_w_out, m_ffn_w_gate, m_ffn_w_up, m_ffn_w_down, m_ln_gain, m_ln_bias, v_meta_tokens, v_s5_lambda_re, v_s5_lambda_im, v_s5_log_dt, v_s5_b_re, v_s5_b_im, v_s5_c_re, v_s5_c_im, v_s5_d, v_s5_w_glu, v_s5_w_out, v_attn_w_qkv, v_attn_q_gain, v_attn_k_gain, v_attn_w_out, v_ffn_w_gate, v_ffn_w_up, v_ffn_w_down, v_ln_gain, v_ln_bias):
    given = locals()
    w = {n: given[n] for n in WEIGHTS}
    mom = {n: given["m_" + n] for n in WEIGHTS}
    vel = {n: given["v_" + n] for n in WEIGHTS}
    return _train_step(x, loss_target, w, mom, vel)
```
